```python
import math
import jax, jax.numpy as jnp
from jax import lax
import numpy as np

D_MODEL = 1024
BATCH = 8
SEQ = 4096
DEPTH = 1

PLE_DIM = 256
D_FF = 2816
FFN_RES_WEIGHT = 0.5
NORM_EPS = 1e-6
Q_BLOCK = 128

MLA_HEADS = 8
MLA_NOPE = 64
MLA_ROPE = 32
MLA_QK = MLA_NOPE + MLA_ROPE
MLA_V = 64
Q_LORA = 384
KV_LORA = 256
ROPE_BASE = 10000.0

SB_HEADS = 8
SB_HEAD_DIM = 64
SB_WIDTH = SB_HEADS * SB_HEAD_DIM
MLA_WIDTH = MLA_HEADS * MLA_V

COL_CQ = Q_LORA
COL_CKV = KV_LORA
COL_KROPE = MLA_ROPE
COL_SB = 3 * SB_WIDTH
COL_GATES = 2 * D_MODEL
IN_COLS = COL_CQ + COL_CKV + COL_KROPE + COL_SB + COL_GATES
SPLITS = list(np.cumsum([COL_CQ, COL_CKV, COL_KROPE, COL_SB])[:])

kernel_name = "hybrid_mla_stickbreaking_macaron_ple"


def rms_norm(x, g):
    xf = x.astype(jnp.float32)
    r = lax.rsqrt(jnp.mean(xf * xf, axis=-1, keepdims=True) + NORM_EPS)
    return (xf * r).astype(x.dtype) * g


def apply_rope(x, positions):
    r = x.shape[-1]
    inv_freq = ROPE_BASE ** (-jnp.arange(0, r, 2, dtype=jnp.float32) / r)
    ang = positions.astype(jnp.float32)[..., None] * inv_freq
    cos = jnp.cos(ang)[:, :, None, :].astype(x.dtype)
    sin = jnp.sin(ang)[:, :, None, :].astype(x.dtype)
    x1, x2 = x[..., : r // 2], x[..., r // 2:]
    return jnp.concatenate([x1 * cos - x2 * sin, x2 * cos + x1 * sin], axis=-1)


def swiglu(u, w_in, w_out):
    a, b = jnp.split(u @ w_in, 2, axis=-1)
    return (jax.nn.silu(a) * b) @ w_out


def causal_softmax_attention(q, k, v):
    s_len = q.shape[2]
    scale = 1.0 / math.sqrt(q.shape[-1])
    outs = []
    for i in range(s_len // Q_BLOCK):
        k_len = (i + 1) * Q_BLOCK
        qb = q[:, :, i * Q_BLOCK:(i + 1) * Q_BLOCK]
        kb, vb = k[:, :, :k_len], v[:, :, :k_len]
        sc = jnp.einsum('bhqd,bhkd->bhqk', qb, kb).astype(jnp.float32) * scale
        q_pos = i * Q_BLOCK + jnp.arange(Q_BLOCK)
        mask = jnp.arange(k_len)[None, :] <= q_pos[:, None]
        w = jax.nn.softmax(jnp.where(mask, sc, -jnp.inf), axis=-1)
        outs.append(jnp.einsum('bhqk,bhkd->bhqd', w.astype(vb.dtype), vb))
    return jnp.concatenate(outs, axis=2)


def stick_breaking_attention(q, k, v):
    s_len = q.shape[2]
    scale = 1.0 / math.sqrt(q.shape[-1])
    outs = []
    for i in range(s_len // Q_BLOCK):
        k_len = (i + 1) * Q_BLOCK
        qb = q[:, :, i * Q_BLOCK:(i + 1) * Q_BLOCK]
        kb, vb = k[:, :, :k_len], v[:, :, :k_len]
        z = jnp.einsum('bhqd,bhkd->bhqk', qb, kb).astype(jnp.float32) * scale
        q_pos = i * Q_BLOCK + jnp.arange(Q_BLOCK)
        mask = jnp.arange(k_len)[None, :] < q_pos[:, None]
        log_1m = jnp.where(mask, jax.nn.log_sigmoid(-z), 0.0)
        suffix = lax.cumsum(log_1m, axis=3, reverse=True) - log_1m
        a = jnp.where(mask, jnp.exp(jax.nn.log_sigmoid(z) + suffix), 0.0)
        outs.append(jnp.einsum('bhqk,bhkd->bhqd', a.astype(vb.dtype), vb))
    return jnp.concatenate(outs, axis=2)


def _fwd_setup_inputs(seed: int = 0) -> dict:
    key = jax.random.key(seed)
    ks = iter(jax.random.split(key, 32))
    f32 = jnp.float32

    def w(shape, fan_in):
        return jax.random.normal(next(ks), (DEPTH,) + shape, f32) * fan_in ** -0.5

    def gain(n):
        return 1.0 + 0.02 * jax.random.normal(next(ks), (DEPTH, n), f32)

    x = jax.random.normal(next(ks), (BATCH, SEQ, D_MODEL), f32)
    p = jax.random.normal(next(ks), (DEPTH, BATCH, SEQ, PLE_DIM), f32)
    offset = jax.random.randint(next(ks), (BATCH, 1), 0, 1024, dtype=jnp.int32)
    positions = offset + jnp.arange(SEQ, dtype=jnp.int32)[None, :]
    return {
        "x": x,
        "p": p,
        "positions": positions,
        "ffn1_norm": gain(D_MODEL),
        "ffn1_w_in": w((D_MODEL, 2 * D_FF), D_MODEL),
        "ffn1_w_out": w((D_FF, D_MODEL), D_FF),
        "mix_norm": gain(D_MODEL),
        "w_in": w((D_MODEL, IN_COLS), D_MODEL),
        "q_latent_norm": gain(Q_LORA),
        "w_q_up": w((Q_LORA, MLA_HEADS * MLA_QK), Q_LORA),
        "kv_latent_norm": gain(KV_LORA),
        "w_kv_up": w((KV_LORA, MLA_HEADS * (MLA_NOPE + MLA_V)), KV_LORA),
        "q_head_norm": gain(MLA_QK),
        "k_head_norm": gain(MLA_QK),
        "w_branch_mla": w((MLA_WIDTH, D_MODEL), MLA_WIDTH),
        "w_branch_sb": w((SB_WIDTH, D_MODEL), SB_WIDTH),
        "w_out": w((D_MODEL, D_MODEL), D_MODEL),
        "ffn2_norm": gain(D_MODEL),
        "ffn2_w_in": w((D_MODEL, 2 * D_FF), D_MODEL),
        "ffn2_w_out": w((D_FF, D_MODEL), D_FF),
        "ple_norm": gain(D_MODEL),
        "w_ple_gate": w((D_MODEL, D_MODEL), D_MODEL),
        "w_ple_proj": w((PLE_DIM, D_MODEL), PLE_DIM),
    }


def _fwd_reference(x, p, positions, ffn1_norm, ffn1_w_in, ffn1_w_out, mix_norm, w_in,
              q_latent_norm, w_q_up, kv_latent_norm, w_kv_up, q_head_norm, k_head_norm,
              w_branch_mla, w_branch_sb, w_out, ffn2_norm, ffn2_w_in, ffn2_w_out,
              ple_norm, w_ple_gate, w_ple_proj):
    b, s, _ = x.shape
    h = x
    for i in range(DEPTH):
        h = h + FFN_RES_WEIGHT * swiglu(rms_norm(h, ffn1_norm[i]), ffn1_w_in[i], ffn1_w_out[i])

        u = rms_norm(h, mix_norm[i])
        proj = u @ w_in[i]
        c_q, c_kv, k_rope, sb_qkv, gates = jnp.split(proj, SPLITS, axis=-1)

        q = (rms_norm(c_q, q_latent_norm[i]) @ w_q_up[i]).reshape(b, s, MLA_HEADS, MLA_QK)
        kv = (rms_norm(c_kv, kv_latent_norm[i]) @ w_kv_up[i]).reshape(b, s, MLA_HEADS, MLA_NOPE + MLA_V)
        k_nope, v_mla = kv[..., :MLA_NOPE], kv[..., MLA_NOPE:]
        k_r = jnp.broadcast_to(k_rope[:, :, None, :], (b, s, MLA_HEADS, MLA_ROPE))
        k = jnp.concatenate([k_nope, k_r], axis=-1)
        q = rms_norm(q, q_head_norm[i])
        k = rms_norm(k, k_head_norm[i])
        q = jnp.concatenate([q[..., :MLA_NOPE], apply_rope(q[..., MLA_NOPE:], positions)], axis=-1)
        k = jnp.concatenate([k[..., :MLA_NOPE], apply_rope(k[..., MLA_NOPE:], positions)], axis=-1)
        o_mla = causal_softmax_attention(q.transpose(0, 2, 1, 3), k.transpose(0, 2, 1, 3),
                                         v_mla.transpose(0, 2, 1, 3))
        o_mla = o_mla.transpose(0, 2, 1, 3).reshape(b, s, MLA_WIDTH)

        sq, sk, sv = [t.reshape(b, s, SB_HEADS, SB_HEAD_DIM).transpose(0, 2, 1, 3)
                      for t in jnp.split(sb_qkv, 3, axis=-1)]
        o_sb = stick_breaking_attention(sq, sk, sv)
        o_sb = o_sb.transpose(0, 2, 1, 3).reshape(b, s, SB_WIDTH)

        g_mla, g_sb = jnp.split(jax.nn.sigmoid(gates), 2, axis=-1)
        merged = g_mla * (o_mla @ w_branch_mla[i]) + g_sb * (o_sb @ w_branch_sb[i])
        h = h + merged @ w_out[i]

        h = h + FFN_RES_WEIGHT * swiglu(rms_norm(h, ffn2_norm[i]), ffn2_w_in[i], ffn2_w_out[i])

        ple_gate = jax.nn.sigmoid(rms_norm(h, ple_norm[i]) @ w_ple_gate[i])
        h = h + ple_gate * (p[i] @ w_ple_proj[i])
    return h


import jax as _jax
import jax.numpy as _jnp

TWIN_FORMAT = 'train_step'
FWD_PARAMS = ['x', 'p', 'positions', 'ffn1_norm', 'ffn1_w_in', 'ffn1_w_out', 'mix_norm', 'w_in', 'q_latent_norm', 'w_q_up', 'kv_latent_norm', 'w_kv_up', 'q_head_norm', 'k_head_norm', 'w_branch_mla', 'w_branch_sb', 'w_out', 'ffn2_norm', 'ffn2_w_in', 'ffn2_w_out', 'ple_norm', 'w_ple_gate', 'w_ple_proj']
TWIN_WEIGHTS = ['ffn1_norm', 'ffn1_w_in', 'ffn1_w_out', 'mix_norm', 'w_in', 'q_latent_norm', 'w_q_up', 'kv_latent_norm', 'w_kv_up', 'q_head_norm', 'k_head_norm', 'w_branch_mla', 'w_branch_sb', 'w_out', 'ffn2_norm', 'ffn2_w_in', 'ffn2_w_out', 'ple_norm', 'w_ple_gate', 'w_ple_proj']
TWIN_DIFF_INPUT = 'x'
TWIN_INPUTS = ['x', 'p', 'positions', 'ffn1_norm', 'ffn1_w_in', 'ffn1_w_out', 'mix_norm', 'w_in', 'q_latent_norm', 'w_q_up', 'kv_latent_norm', 'w_kv_up', 'q_head_norm', 'k_head_norm', 'w_branch_mla', 'w_branch_sb', 'w_out', 'ffn2_norm', 'ffn2_w_in', 'ffn2_w_out', 'ple_norm', 'w_ple_gate', 'w_ple_proj', 'loss_target', 'm_ffn1_norm', 'm_ffn1_w_in', 'm_ffn1_w_out', 'm_mix_norm', 'm_w_in', 'm_q_latent_norm', 'm_w_q_up', 'm_kv_latent_norm', 'm_w_kv_up', 'm_q_head_norm', 'm_k_head_norm', 'm_w_branch_mla', 'm_w_branch_sb', 'm_w_out', 'm_ffn2_norm', 'm_ffn2_w_in', 'm_ffn2_w_out', 'm_ple_norm', 'm_w_ple_gate', 'm_w_ple_proj', 'v_ffn1_norm', 'v_ffn1_w_in', 'v_ffn1_w_out', 'v_mix_norm', 'v_w_in', 'v_q_latent_norm', 'v_w_q_up', 'v_kv_latent_norm', 'v_w_kv_up', 'v_q_head_norm', 'v_k_head_norm', 'v_w_branch_mla', 'v_w_branch_sb', 'v_w_out', 'v_ffn2_norm', 'v_ffn2_w_in', 'v_ffn2_w_out', 'v_ple_norm', 'v_w_ple_gate', 'v_w_ple_proj']
TWIN_OUTPUTS = ['loss', 'grad_x', 'grad_ffn1_norm', 'grad_ffn1_w_in', 'grad_ffn1_w_out', 'grad_mix_norm', 'grad_w_in', 'grad_q_latent_norm', 'grad_w_q_up', 'grad_kv_latent_norm', 'grad_w_kv_up', 'grad_q_head_norm', 'grad_k_head_norm', 'grad_w_branch_mla', 'grad_w_branch_sb', 'grad_w_out', 'grad_ffn2_norm', 'grad_ffn2_w_in', 'grad_ffn2_w_out', 'grad_ple_norm', 'grad_w_ple_gate', 'grad_w_ple_proj', 'delta_ffn1_norm', 'delta_ffn1_w_in', 'delta_ffn1_w_out', 'delta_mix_norm', 'delta_w_in', 'delta_q_latent_norm', 'delta_w_q_up', 'delta_kv_latent_norm', 'delta_w_kv_up', 'delta_q_head_norm', 'delta_k_head_norm', 'delta_w_branch_mla', 'delta_w_branch_sb', 'delta_w_out', 'delta_ffn2_norm', 'delta_ffn2_w_in', 'delta_ffn2_w_out', 'delta_ple_norm', 'delta_w_ple_gate', 'delta_w_ple_proj', 'new_m_ffn1_norm', 'new_m_ffn1_w_in', 'new_m_ffn1_w_out', 'new_m_mix_norm', 'new_m_w_in', 'new_m_q_latent_norm', 'new_m_w_q_up', 'new_m_kv_latent_norm', 'new_m_w_kv_up', 'new_m_q_head_norm', 'new_m_k_head_norm', 'new_m_w_branch_mla', 'new_m_w_branch_sb', 'new_m_w_out', 'new_m_ffn2_norm', 'new_m_ffn2_w_in', 'new_m_ffn2_w_out', 'new_m_ple_norm', 'new_m_w_ple_gate', 'new_m_w_ple_proj', 'new_v_ffn1_norm', 'new_v_ffn1_w_in', 'new_v_ffn1_w_out', 'new_v_mix_norm', 'new_v_w_in', 'new_v_q_latent_norm', 'new_v_w_q_up', 'new_v_kv_latent_norm', 'new_v_w_kv_up', 'new_v_q_head_norm', 'new_v_k_head_norm', 'new_v_w_branch_mla', 'new_v_w_branch_sb', 'new_v_w_out', 'new_v_ffn2_norm', 'new_v_ffn2_w_in', 'new_v_ffn2_w_out', 'new_v_ple_norm', 'new_v_w_ple_gate', 'new_v_w_ple_proj']
TWIN_LEAF_KINDS = {'loss': 'loss', 'grad_x': 'grad_x', 'grad_ffn1_norm': 'grad_w', 'grad_ffn1_w_in': 'grad_w', 'grad_ffn1_w_out': 'grad_w', 'grad_mix_norm': 'grad_w', 'grad_w_in': 'grad_w', 'grad_q_latent_norm': 'grad_w', 'grad_w_q_up': 'grad_w', 'grad_kv_latent_norm': 'grad_w', 'grad_w_kv_up': 'grad_w', 'grad_q_head_norm': 'grad_w', 'grad_k_head_norm': 'grad_w', 'grad_w_branch_mla': 'grad_w', 'grad_w_branch_sb': 'grad_w', 'grad_w_out': 'grad_w', 'grad_ffn2_norm': 'grad_w', 'grad_ffn2_w_in': 'grad_w', 'grad_ffn2_w_out': 'grad_w', 'grad_ple_norm': 'grad_w', 'grad_w_ple_gate': 'grad_w', 'grad_w_ple_proj': 'grad_w', 'delta_ffn1_norm': 'delta_w', 'delta_ffn1_w_in': 'delta_w', 'delta_ffn1_w_out': 'delta_w', 'delta_mix_norm': 'delta_w', 'delta_w_in': 'delta_w', 'delta_q_latent_norm': 'delta_w', 'delta_w_q_up': 'delta_w', 'delta_kv_latent_norm': 'delta_w', 'delta_w_kv_up': 'delta_w', 'delta_q_head_norm': 'delta_w', 'delta_k_head_norm': 'delta_w', 'delta_w_branch_mla': 'delta_w', 'delta_w_branch_sb': 'delta_w', 'delta_w_out': 'delta_w', 'delta_ffn2_norm': 'delta_w', 'delta_ffn2_w_in': 'delta_w', 'delta_ffn2_w_out': 'delta_w', 'delta_ple_norm': 'delta_w', 'delta_w_ple_gate': 'delta_w', 'delta_w_ple_proj': 'delta_w', 'new_m_ffn1_norm': 'new_m', 'new_m_ffn1_w_in': 'new_m', 'new_m_ffn1_w_out': 'new_m', 'new_m_mix_norm': 'new_m', 'new_m_w_in': 'new_m', 'new_m_q_latent_norm': 'new_m', 'new_m_w_q_up': 'new_m', 'new_m_kv_latent_norm': 'new_m', 'new_m_w_kv_up': 'new_m', 'new_m_q_head_norm': 'new_m', 'new_m_k_head_norm': 'new_m', 'new_m_w_branch_mla': 'new_m', 'new_m_w_branch_sb': 'new_m', 'new_m_w_out': 'new_m', 'new_m_ffn2_norm': 'new_m', 'new_m_ffn2_w_in': 'new_m', 'new_m_ffn2_w_out': 'new_m', 'new_m_ple_norm': 'new_m', 'new_m_w_ple_gate': 'new_m', 'new_m_w_ple_proj': 'new_m', 'new_v_ffn1_norm': 'new_v', 'new_v_ffn1_w_in': 'new_v', 'new_v_ffn1_w_out': 'new_v', 'new_v_mix_norm': 'new_v', 'new_v_w_in': 'new_v', 'new_v_q_latent_norm': 'new_v', 'new_v_w_q_up': 'new_v', 'new_v_kv_latent_norm': 'new_v', 'new_v_w_kv_up': 'new_v', 'new_v_q_head_norm': 'new_v', 'new_v_k_head_norm': 'new_v', 'new_v_w_branch_mla': 'new_v', 'new_v_w_branch_sb': 'new_v', 'new_v_w_out': 'new_v', 'new_v_ffn2_norm': 'new_v', 'new_v_ffn2_w_in': 'new_v', 'new_v_ffn2_w_out': 'new_v', 'new_v_ple_norm': 'new_v', 'new_v_w_ple_gate': 'new_v', 'new_v_w_ple_proj': 'new_v'}


def _forward(args):
    return _fwd_reference(*[args[k] for k in FWD_PARAMS])


def _output_shape():
    out = _jax.eval_shape(lambda: _forward(_fwd_setup_inputs(0)))
    return out.shape, out.dtype

N_MICROBATCH = 1
ADAM_LR = 0.001
ADAM_B1 = 0.9
ADAM_B2 = 0.999
ADAM_EPS = 1e-08
ADAM_WD = 0.01
ADAM_STEP = 10
PER_EXAMPLE_BATCH_AXIS = {'x': 0, 'p': 1, 'positions': 0, 'loss_target': 0}
SHARED_INPUTS = []
_WEIGHT_DTYPES = {'ffn1_norm': _jnp.float32, 'ffn1_w_in': _jnp.float32, 'ffn1_w_out': _jnp.float32, 'mix_norm': _jnp.float32, 'w_in': _jnp.float32, 'q_latent_norm': _jnp.float32, 'w_q_up': _jnp.float32, 'kv_latent_norm': _jnp.float32, 'w_kv_up': _jnp.float32, 'q_head_norm': _jnp.float32, 'k_head_norm': _jnp.float32, 'w_branch_mla': _jnp.float32, 'w_branch_sb': _jnp.float32, 'w_out': _jnp.float32, 'ffn2_norm': _jnp.float32, 'ffn2_w_in': _jnp.float32, 'ffn2_w_out': _jnp.float32, 'ple_norm': _jnp.float32, 'w_ple_gate': _jnp.float32, 'w_ple_proj': _jnp.float32}
MOMENT_SCALE = {'ffn1_norm': 6.190592e+00, 'ffn1_w_in': 7.348701e-02, 'ffn1_w_out': 1.234279e-01, 'mix_norm': 5.366336e+00, 'w_in': 9.728345e-02, 'q_latent_norm': 5.813599e-02, 'w_q_up': 3.985810e-02, 'kv_latent_norm': 3.552644e-01, 'w_kv_up': 5.521700e-02, 'q_head_norm': 5.012124e-01, 'k_head_norm': 5.005813e-01, 'w_branch_mla': 4.389359e-02, 'w_branch_sb': 1.939475e-01, 'w_out': 1.607492e-01, 'ffn2_norm': 6.185922e+00, 'ffn2_w_in': 6.535552e-02, 'ffn2_w_out': 1.132197e-01, 'ple_norm': 9.448769e-01, 'w_ple_gate': 7.188010e-02, 'w_ple_proj': 4.223282e-01}


def _to_microbatches(a, axis):
    t = _jnp.moveaxis(a, axis, 0)
    t = t.reshape((N_MICROBATCH, t.shape[0] // N_MICROBATCH) + t.shape[1:])
    return _jnp.moveaxis(t, 1, axis + 1)


def setup_inputs(seed: int = 0) -> dict:
    inp = _fwd_setup_inputs(seed)
    key = _jax.random.fold_in(_jax.random.key(seed), 7919)
    shape, _ = _output_shape()
    out = dict(inp)
    out["loss_target"] = _jax.random.normal(_jax.random.fold_in(key, 0), shape, _jnp.float32)
    for i, name in enumerate(TWIN_WEIGHTS):
        w = inp[name].astype(_jnp.float32)
        if MOMENT_SCALE is None:
            s = _jnp.sqrt(_jnp.mean(_jnp.square(w)) + 1e-30)
        else:
            s = MOMENT_SCALE[name]
        km, kv = _jax.random.split(_jax.random.fold_in(key, i + 1))
        out[name] = w
        out["m_" + name] = s * _jax.random.normal(km, w.shape, _jnp.float32)
        out["v_" + name] = (s * s) * _jax.random.uniform(kv, w.shape, _jnp.float32, 0.5, 1.5)
    if N_MICROBATCH > 1:
        for name, axis in PER_EXAMPLE_BATCH_AXIS.items():
            out[name] = _to_microbatches(out[name], axis)
    return {'x': out['x'], 'p': out['p'], 'positions': out['positions'], 'ffn1_norm': out['ffn1_norm'], 'ffn1_w_in': out['ffn1_w_in'], 'ffn1_w_out': out['ffn1_w_out'], 'mix_norm': out['mix_norm'], 'w_in': out['w_in'], 'q_latent_norm': out['q_latent_norm'], 'w_q_up': out['w_q_up'], 'kv_latent_norm': out['kv_latent_norm'], 'w_kv_up': out['w_kv_up'], 'q_head_norm': out['q_head_norm'], 'k_head_norm': out['k_head_norm'], 'w_branch_mla': out['w_branch_mla'], 'w_branch_sb': out['w_branch_sb'], 'w_out': out['w_out'], 'ffn2_norm': out['ffn2_norm'], 'ffn2_w_in': out['ffn2_w_in'], 'ffn2_w_out': out['ffn2_w_out'], 'ple_norm': out['ple_norm'], 'w_ple_gate': out['w_ple_gate'], 'w_ple_proj': out['w_ple_proj'], 'loss_target': out['loss_target'], 'm_ffn1_norm': out['m_ffn1_norm'], 'm_ffn1_w_in': out['m_ffn1_w_in'], 'm_ffn1_w_out': out['m_ffn1_w_out'], 'm_mix_norm': out['m_mix_norm'], 'm_w_in': out['m_w_in'], 'm_q_latent_norm': out['m_q_latent_norm'], 'm_w_q_up': out['m_w_q_up'], 'm_kv_latent_norm': out['m_kv_latent_norm'], 'm_w_kv_up': out['m_w_kv_up'], 'm_q_head_norm': out['m_q_head_norm'], 'm_k_head_norm': out['m_k_head_norm'], 'm_w_branch_mla': out['m_w_branch_mla'], 'm_w_branch_sb': out['m_w_branch_sb'], 'm_w_out': out['m_w_out'], 'm_ffn2_norm': out['m_ffn2_norm'], 'm_ffn2_w_in': out['m_ffn2_w_in'], 'm_ffn2_w_out': out['m_ffn2_w_out'], 'm_ple_norm': out['m_ple_norm'], 'm_w_ple_gate': out['m_w_ple_gate'], 'm_w_ple_proj': out['m_w_ple_proj'], 'v_ffn1_norm': out['v_ffn1_norm'], 'v_ffn1_w_in': out['v_ffn1_w_in'], 'v_ffn1_w_out': out['v_ffn1_w_out'], 'v_mix_norm': out['v_mix_norm'], 'v_w_in': out['v_w_in'], 'v_q_latent_norm': out['v_q_latent_norm'], 'v_w_q_up': out['v_w_q_up'], 'v_kv_latent_norm': out['v_kv_latent_norm'], 'v_w_kv_up': out['v_w_kv_up'], 'v_q_head_norm': out['v_q_head_norm'], 'v_k_head_norm': out['v_k_head_norm'], 'v_w_branch_mla': out['v_w_branch_mla'], 'v_w_branch_sb': out['v_w_branch_sb'], 'v_w_out': out['v_w_out'], 'v_ffn2_norm': out['v_ffn2_norm'], 'v_ffn2_w_in': out['v_ffn2_w_in'], 'v_ffn2_w_out': out['v_ffn2_w_out'], 'v_ple_norm': out['v_ple_norm'], 'v_w_ple_gate': out['v_w_ple_gate'], 'v_w_ple_proj': out['v_w_ple_proj']}


def _loss(weights, diff, rest, loss_target):
    with _jax.named_scope("forward"):
        args = {**rest, TWIN_DIFF_INPUT: diff, **{k: w.astype(_WEIGHT_DTYPES[k]) for k, w in weights.items()}}
        y = _forward(args)
    with _jax.named_scope("loss_head"):
        err = _jnp.square(y.astype(_jnp.float32) - loss_target)
        return 0.5 * _jnp.sum(_jnp.mean(err, axis=-1)) if err.ndim else 0.5 * err


def _adamw(w, g, m, v):
    m = ADAM_B1 * m + (1.0 - ADAM_B1) * g
    v = ADAM_B2 * v + (1.0 - ADAM_B2) * _jnp.square(g)
    m_hat = m / (1.0 - ADAM_B1 ** ADAM_STEP)
    v_hat = v / (1.0 - ADAM_B2 ** ADAM_STEP)
    delta = -ADAM_LR * (m_hat / (_jnp.sqrt(v_hat) + ADAM_EPS) + ADAM_WD * w)
    return delta, m, v


def reference(x, p, positions, ffn1_norm, ffn1_w_in, ffn1_w_out, mix_norm, w_in, q_latent_norm, w_q_up, kv_latent_norm, w_kv_up, q_head_norm, k_head_norm, w_branch_mla, w_branch_sb, w_out, ffn2_norm, ffn2_w_in, ffn2_w_out, ple_norm, w_ple_gate, w_ple_proj, loss_target, m_ffn1_norm, m_ffn1_w_in, m_ffn1_w_out, m_mix_norm, m_w_in, m_q_latent_norm, m_w_q_up, m_kv_latent_norm, m_w_kv_up, m_q_head_norm, m_k_head_norm, m_w_branch_mla, m_w_branch_sb, m_w_out, m_ffn2_norm, m_ffn2_w_in, m_ffn2_w_out, m_ple_norm, m_w_ple_gate, m_w_ple_proj, v_ffn1_norm, v_ffn1_w_in, v_ffn1_w_out, v_mix_norm, v_w_in, v_q_latent_norm, v_w_q_up, v_kv_latent_norm, v_w_kv_up, v_q_head_norm, v_k_head_norm, v_w_branch_mla, v_w_branch_sb, v_w_out, v_ffn2_norm, v_ffn2_w_in, v_ffn2_w_out, v_ple_norm, v_w_ple_gate, v_w_ple_proj):
    given = dict(x=x, p=p, positions=positions, ffn1_norm=ffn1_norm, ffn1_w_in=ffn1_w_in, ffn1_w_out=ffn1_w_out, mix_norm=mix_norm, w_in=w_in, q_latent_norm=q_latent_norm, w_q_up=w_q_up, kv_latent_norm=kv_latent_norm, w_kv_up=w_kv_up, q_head_norm=q_head_norm, k_head_norm=k_head_norm, w_branch_mla=w_branch_mla, w_branch_sb=w_branch_sb, w_out=w_out, ffn2_norm=ffn2_norm, ffn2_w_in=ffn2_w_in, ffn2_w_out=ffn2_w_out, ple_norm=ple_norm, w_ple_gate=w_ple_gate, w_ple_proj=w_ple_proj, loss_target=loss_target, m_ffn1_norm=m_ffn1_norm, m_ffn1_w_in=m_ffn1_w_in, m_ffn1_w_out=m_ffn1_w_out, m_mix_norm=m_mix_norm, m_w_in=m_w_in, m_q_latent_norm=m_q_latent_norm, m_w_q_up=m_w_q_up, m_kv_latent_norm=m_kv_latent_norm, m_w_kv_up=m_w_kv_up, m_q_head_norm=m_q_head_norm, m_k_head_norm=m_k_head_norm, m_w_branch_mla=m_w_branch_mla, m_w_branch_sb=m_w_branch_sb, m_w_out=m_w_out, m_ffn2_norm=m_ffn2_norm, m_ffn2_w_in=m_ffn2_w_in, m_ffn2_w_out=m_ffn2_w_out, m_ple_norm=m_ple_norm, m_w_ple_gate=m_w_ple_gate, m_w_ple_proj=m_w_ple_proj, v_ffn1_norm=v_ffn1_norm, v_ffn1_w_in=v_ffn1_w_in, v_ffn1_w_out=v_ffn1_w_out, v_mix_norm=v_mix_norm, v_w_in=v_w_in, v_q_latent_norm=v_q_latent_norm, v_w_q_up=v_w_q_up, v_kv_latent_norm=v_kv_latent_norm, v_w_kv_up=v_w_kv_up, v_q_head_norm=v_q_head_norm, v_k_head_norm=v_k_head_norm, v_w_branch_mla=v_w_branch_mla, v_w_branch_sb=v_w_branch_sb, v_w_out=v_w_out, v_ffn2_norm=v_ffn2_norm, v_ffn2_w_in=v_ffn2_w_in, v_ffn2_w_out=v_ffn2_w_out, v_ple_norm=v_ple_norm, v_w_ple_gate=v_w_ple_gate, v_w_ple_proj=v_w_ple_proj)
    weights = {n: given[n] for n in TWIN_WEIGHTS}
    shared = {n: given[n] for n in SHARED_INPUTS}
    per_example = {n: given[n] for n in ['x', 'p', 'positions']}
    grad_fn = _jax.value_and_grad(_loss, argnums=(0, 1))

    def one_microbatch(ex, loss_target):
        ex = dict(ex)
        diff = ex.pop(TWIN_DIFF_INPUT)
        return grad_fn(weights, diff, {**shared, **ex}, loss_target)

    if N_MICROBATCH == 1:
        loss, (grad_w, grad_x) = one_microbatch(per_example, given["loss_target"])
    else:
        def body(carry, xs):
            loss_sum, grad_sum = carry
            l_k, (gw_k, gx_k) = one_microbatch(xs[0], xs[1])
            with _jax.named_scope("update"):
                return (loss_sum + l_k, _jax.tree.map(_jnp.add, grad_sum, gw_k)), gx_k

        init = (_jnp.zeros((), _jnp.float32), _jax.tree.map(_jnp.zeros_like, weights))
        (loss, grad_w), grad_x = _jax.lax.scan(body, init, (per_example, given["loss_target"]))
    with _jax.named_scope("update"):
        delta_w, new_m, new_v = {}, {}, {}
        for n in TWIN_WEIGHTS:
            delta_w[n], new_m[n], new_v[n] = _adamw(weights[n], grad_w[n], given["m_" + n], given["v_" + n])
    return (loss, grad_x, *[grad_w[n] for n in TWIN_WEIGHTS], *[delta_w[n] for n in TWIN_WEIGHTS],
            *[new_m[n] for n in TWIN_WEIGHTS], *[new_v[n] for n in TWIN_WEIGHTS])
```

```python
import functools
import math

import jax
import jax.numpy as jnp
from jax import lax
from jax.experimental import pallas as pl
from jax.experimental.pallas import tpu as pltpu

F32 = jnp.float32
BF16 = jnp.bfloat16
MESH = pl.DeviceIdType.MESH

D_MODEL = 1024
D_FF = 2816
PLE_DIM = 256
NORM_EPS = 1e-6
HEADS = 8
MLA_NOPE = 64
MLA_ROPE = 32
MLA_QK = 96
Q_LORA = 384
KV_LORA = 256
SB_WIDTH = 512
ROPE_BASE = 10000.0
HEAD_PAD = 128
N_CHIPS = 4

ADAM_LR = 0.001
ADAM_B1 = 0.9
ADAM_B2 = 0.999
ADAM_EPS = 1e-08
ADAM_WD = 0.01
ADAM_STEP = 10

SEG_CQ = (0, 384)
SEG_CKV = (384, 256)
SEG_KROPE = (640, 128)
SEG_SBQ = (768, 512)
SEG_SBK = (1280, 512)
SEG_SBV = (1792, 512)
SEG_GATES = (2304, 2048)
IN_COLS_PAD = 4352

TM = 512
TM_SMALL = 256
TQ = 256
FLAT_W = 1024
ROW_ALIGN = 32
VMEM_LIMIT = 56 * 1024 * 1024

NT = (((1,), (1,)), ((), ()))
TN = (((0,), (0,)), ((), ()))


def _cp(sem):
    return pltpu.CompilerParams(dimension_semantics=sem, vmem_limit_bytes=VMEM_LIMIT)


def _rows(tm, w):
    return pl.BlockSpec((tm, w), lambda i: (i, 0))


def _whole(shape):
    return pl.BlockSpec(shape, lambda i: (0,) * len(shape))


def _dot(a, b):
    return jnp.dot(a, b, preferred_element_type=F32)


def _dot_nt(a, b):
    return lax.dot_general(a, b, NT, preferred_element_type=F32)


def _dot_tn(a, b):
    return lax.dot_general(a, b, TN, preferred_element_type=F32)


def _rstd(x, n):
    return lax.rsqrt(jnp.sum(x * x, axis=-1, keepdims=True) / n + NORM_EPS)


def _rms_bwd(x, r, g, dy, n):
    gy = dy * g
    return r * gy - x * ((r * r * r) * (jnp.sum(x * gy, axis=-1, keepdims=True) / n))


def _sigmoid(x):
    return jax.nn.sigmoid(x)


def _pick(n, cands):
    for c in cands:
        if n % c == 0:
            return c
    return n


def _norm_call(h, g, name):
    s, d = h.shape
    tm = min(TM, s)

    def body(h_ref, g_ref, u_ref):
        x = h_ref[...]
        u_ref[...] = ((x * _rstd(x, d)) * g_ref[...]).astype(BF16)

    return pl.pallas_call(
        body, name=name, grid=(s // tm,),
        in_specs=[_rows(tm, d), _whole((1, d))], out_specs=_rows(tm, d),
        out_shape=jax.ShapeDtypeStruct((s, d), BF16), compiler_params=_cp(("parallel",)))(h, g)


def _ffn_in_call(u, w, name):
    s, d = u.shape
    n = w.shape[1] // 2
    tn = n // 2
    tm = min(TM, s)
    nj = n // tn

    def body(u_ref, wa_ref, wb_ref, a_ref, b_ref, hm_ref):
        uu = u_ref[...]
        a = _dot(uu, wa_ref[...])
        b = _dot(uu, wb_ref[...])
        a_ref[...] = a
        b_ref[...] = b
        hm_ref[...] = ((a * _sigmoid(a)) * b).astype(BF16)

    blk = pl.BlockSpec((tm, tn), lambda j, i: (i, j))
    return pl.pallas_call(
        body, name=name, grid=(nj, s // tm),
        in_specs=[pl.BlockSpec((tm, d), lambda j, i: (i, 0)),
                  pl.BlockSpec((d, tn), lambda j, i: (0, j)),
                  pl.BlockSpec((d, tn), lambda j, i: (0, j + nj))],
        out_specs=[blk, blk, blk],
        out_shape=[jax.ShapeDtypeStruct((s, n), F32), jax.ShapeDtypeStruct((s, n), F32),
                   jax.ShapeDtypeStruct((s, n), BF16)],
        compiler_params=_cp(("parallel", "parallel")))(u, w, w)


def _ffn_out_call(hm, w, h, name):
    s, n = hm.shape
    d = w.shape[1]
    tm = min(TM, s)

    def body(hm_ref, w_ref, h_ref, o_ref):
        o_ref[...] = h_ref[...] + 0.5 * _dot(hm_ref[...], w_ref[...])

    return pl.pallas_call(
        body, name=name, grid=(s // tm,),
        in_specs=[_rows(tm, n), _whole((n, d)), _rows(tm, d)], out_specs=_rows(tm, d),
        out_shape=jax.ShapeDtypeStruct((s, d), F32), compiler_params=_cp(("parallel",)))(hm, w, h)


def _mix_in_call(u, w, name):
    s, d = u.shape
    tm = min(TM_SMALL, s)
    segs = [(SEG_CQ, F32), (SEG_CKV, F32), (SEG_KROPE, F32), (SEG_SBQ, BF16), (SEG_SBK, BF16),
            (SEG_SBV, BF16), (SEG_GATES, F32)]

    def body(u_ref, w_ref, *outs):
        uu = u_ref[...]
        for ((off, width), _), o_ref in zip(segs, outs):
            o_ref[...] = _dot(uu, w_ref[:, off:off + width]).astype(o_ref.dtype)

    return pl.pallas_call(
        body, name=name, grid=(s // tm,),
        in_specs=[_rows(tm, d), _whole((d, IN_COLS_PAD))],
        out_specs=[_rows(tm, width) for (_, width), _ in segs],
        out_shape=[jax.ShapeDtypeStruct((s, width), dt) for (_, width), dt in segs],
        compiler_params=_cp(("parallel",)))(u, w)


def _lane(shape):
    return lax.broadcasted_iota(jnp.int32, shape, len(shape) - 1)


def _rot_half(y):
    lane = _lane(y.shape)
    swapped = jnp.where(lane < MLA_NOPE + MLA_ROPE // 2, pltpu.roll(y, HEAD_PAD - 16, 1), pltpu.roll(y, 16, 1))
    return jnp.where((lane >= MLA_NOPE) & (lane < MLA_QK), swapped, 0.0)


def _rope_tables(pos_ref, freq_ref, sign_ref):
    ang = pos_ref[...].astype(F32) * freq_ref[...]
    return jnp.cos(ang), jnp.sin(ang) * sign_ref[...]


def _head_fwd(x, g, cosv, ssv):
    r = _rstd(x, MLA_QK)
    y = (x * r) * g
    return y * cosv + _rot_half(y) * ssv, r


def _head_bwd(x, r, g, cosv, ssv, dout):
    dy = dout * cosv + _rot_half(dout * ssv)
    return _rms_bwd(x, r, g, dy, MLA_QK), jnp.sum(dy * (x * r), axis=0, keepdims=True)


def _mla_prep_call(cq, ckv, krope, pos, freq, sign, g_ql, g_kvl, g_qh, g_kh, wq, wkv, name):
    s = cq.shape[0]
    tm = min(TM_SMALL, s)
    width = HEADS * HEAD_PAD

    def body(cq_ref, ckv_ref, kr_ref, pos_ref, freq_ref, sign_ref, gql_ref, gkvl_ref, gqh_ref, gkh_ref,
             wq_ref, wkv_ref, q_ref, k_ref, v_ref):
        cosv, ssv = _rope_tables(pos_ref, freq_ref, sign_ref)
        x = cq_ref[...]
        qr = _dot(((x * _rstd(x, Q_LORA)) * gql_ref[...]).astype(BF16), wq_ref[...])
        x = ckv_ref[...]
        kv = _dot(((x * _rstd(x, KV_LORA)) * gkvl_ref[...]).astype(BF16), wkv_ref[...])
        kr = kr_ref[...]
        lane = _lane((tm, HEAD_PAD))
        for h in range(HEADS):
            sl = slice(h * HEAD_PAD, (h + 1) * HEAD_PAD)
            qh, _ = _head_fwd(qr[:, sl], gqh_ref[...], cosv, ssv)
            q_ref[:, sl] = qh.astype(BF16)
            kvh = kv[:, sl]
            kh, _ = _head_fwd(jnp.where(lane < MLA_NOPE, kvh, kr), gkh_ref[...], cosv, ssv)
            k_ref[:, sl] = kh.astype(BF16)
            v_ref[:, sl] = jnp.where(lane >= MLA_NOPE, kvh, 0.0).astype(BF16)

    out = jax.ShapeDtypeStruct((s, width), BF16)
    return pl.pallas_call(
        body, name=name, grid=(s // tm,),
        in_specs=[_rows(tm, Q_LORA), _rows(tm, KV_LORA), _rows(tm, HEAD_PAD), _rows(tm, 1),
                  _whole((1, HEAD_PAD)), _whole((1, HEAD_PAD)), _whole((1, Q_LORA)), _whole((1, KV_LORA)),
                  _whole((1, HEAD_PAD)), _whole((1, HEAD_PAD)), _whole((Q_LORA, width)), _whole((KV_LORA, width))],
        out_specs=[_rows(tm, width)] * 3, out_shape=[out, out, out],
        compiler_params=_cp(("parallel",)))(cq, ckv, krope, pos, freq, sign, g_ql, g_kvl, g_qh, g_kh, wq, wkv)


def _attn_specs(s):
    qspec = pl.BlockSpec((TQ, HEAD_PAD), lambda h, i: (i, h))
    kspec = pl.BlockSpec((s, HEAD_PAD), lambda h, i: (0, h))
    return qspec, kspec


def _tri(cmp):
    r = lax.broadcasted_iota(jnp.int32, (TQ, TQ), 0)
    c = lax.broadcasted_iota(jnp.int32, (TQ, TQ), 1)
    return cmp(r, c)


def _mla_fwd_call(q, k, v, name):
    s, width = q.shape
    scale = 1.0 / math.sqrt(MLA_QK)

    def body(q_ref, k_ref, v_ref, o_ref, lse_ref):
        qi = pl.program_id(1)
        qb = q_ref[...]
        causal = _tri(lambda r, c: c <= r)

        def blk(kb, carry, diag):
            m, l, acc = carry
            ks = pl.multiple_of(kb * TQ, TQ)
            kt = k_ref[pl.ds(ks, TQ), :]
            vt = v_ref[pl.ds(ks, TQ), :]
            sc = _dot_nt(qb, kt) * scale
            if diag:
                sc = jnp.where(causal, sc, -1e30)
            mn = jnp.maximum(m, jnp.max(sc, axis=-1, keepdims=True))
            al = jnp.exp(m - mn)
            p = jnp.exp(sc - mn)
            l = al * l + jnp.sum(p, axis=-1, keepdims=True)
            acc = al * acc + _dot(p.astype(BF16), vt)
            return mn, l, acc

        init = (jnp.full((TQ, 1), -1e30, F32), jnp.zeros((TQ, 1), F32), jnp.zeros((TQ, HEAD_PAD), F32))
        carry = blk(qi, init, True)
        m, l, acc = lax.fori_loop(0, qi, lambda kb, c: blk(kb, c, False), carry)
        o_ref[...] = (acc / l).astype(BF16)
        lse_ref[...] = jnp.broadcast_to(m + jnp.log(l), (TQ, HEAD_PAD))

    qspec, kspec = _attn_specs(s)
    return pl.pallas_call(
        body, name=name, grid=(width // HEAD_PAD, s // TQ),
        in_specs=[qspec, kspec, kspec], out_specs=[qspec, qspec],
        out_shape=[jax.ShapeDtypeStruct((s, width), BF16), jax.ShapeDtypeStruct((s, width), F32)],
        compiler_params=_cp(("parallel", "arbitrary")))(q, k, v)


def _mla_bwd_call(q, k, v, o, do, lse, name):
    s, width = q.shape
    scale = 1.0 / math.sqrt(MLA_QK)

    def body(q_ref, k_ref, v_ref, o_ref, do_ref, lse_ref, dq_ref, dk_ref, dv_ref):
        qi = pl.program_id(1)

        @pl.when(qi == 0)
        def _():
            dk_ref[...] = jnp.zeros_like(dk_ref)
            dv_ref[...] = jnp.zeros_like(dv_ref)

        qb = q_ref[...]
        dob = do_ref[...]
        lse = lse_ref[:, 0:1]
        dlt = jnp.sum(dob.astype(F32) * o_ref[...].astype(F32), axis=-1, keepdims=True)
        causal = _tri(lambda r, c: c <= r)

        def blk(kb, dq, diag):
            ks = pl.multiple_of(kb * TQ, TQ)
            kt = k_ref[pl.ds(ks, TQ), :]
            vt = v_ref[pl.ds(ks, TQ), :]
            p = jnp.exp(_dot_nt(qb, kt) * scale - lse)
            if diag:
                p = jnp.where(causal, p, 0.0)
            ds = (p * (_dot_nt(dob, vt) - dlt) * scale).astype(BF16)
            dv_ref[pl.ds(ks, TQ), :] += _dot_tn(p.astype(BF16), dob)
            dk_ref[pl.ds(ks, TQ), :] += _dot_tn(ds, qb)
            return dq + _dot(ds, kt)

        dq = blk(qi, jnp.zeros((TQ, HEAD_PAD), F32), True)
        dq_ref[...] = lax.fori_loop(0, qi, lambda kb, c: blk(kb, c, False), dq)

    qspec, kspec = _attn_specs(s)
    out = jax.ShapeDtypeStruct((s, width), F32)
    return pl.pallas_call(
        body, name=name, grid=(width // HEAD_PAD, s // TQ),
        in_specs=[qspec, kspec, kspec, qspec, qspec, qspec], out_specs=[qspec, kspec, kspec],
        out_shape=[out, out, out],
        compiler_params=_cp(("parallel", "arbitrary")))(q, k, v, o, do, lse)


def _dot_hilo(x, u):
    hi = x.astype(BF16)
    lo = (x - hi.astype(F32)).astype(BF16)
    return _dot(hi, u) + _dot(lo, u)


def _sb_logs(z):
    sp = jnp.log(1.0 + jnp.exp(-jnp.abs(z)))
    return jnp.minimum(z, 0.0) - sp, jnp.minimum(-z, 0.0) - sp


def _sb_fwd_call(q, k, v, name):
    s, width = q.shape
    scale = 1.0 / math.sqrt(SB_WIDTH // HEADS)
    hd = SB_WIDTH // HEADS

    def body(q_ref, k_ref, v_ref, o_ref, t_ref):
        qi = pl.program_id(1)
        qb = q_ref[...]
        strict = _tri(lambda r, c: c < r)
        after = _tri(lambda r, c: r > c).astype(BF16)
        lane = _lane((1, HEAD_PAD))
        o_acc = jnp.zeros((TQ, HEAD_PAD), F32)
        t_acc = jnp.zeros((TQ, HEAD_PAD), F32)
        for hh in range(HEAD_PAD // hd):
            hm = (lane >= hh * hd) & (lane < (hh + 1) * hd)
            qh = jnp.where(hm, qb, jnp.zeros_like(qb))

            def blk(kb, carry, diag, hm=hm, qh=qh):
                cs, acc = carry
                ks = pl.multiple_of(kb * TQ, TQ)
                kt = k_ref[pl.ds(ks, TQ), :]
                vt = v_ref[pl.ds(ks, TQ), :]
                ls, l1m = _sb_logs(_dot_nt(qh, kt) * scale)
                if diag:
                    l1m = jnp.where(strict, l1m, 0.0)
                a = jnp.exp(ls + _dot_hilo(l1m, after) + cs)
                if diag:
                    a = jnp.where(strict, a, 0.0)
                acc = acc + _dot(a.astype(BF16), jnp.where(hm, vt, jnp.zeros_like(vt)))
                return cs + jnp.sum(l1m, axis=-1, keepdims=True), acc

            carry = blk(qi, (jnp.zeros((TQ, 1), F32), jnp.zeros((TQ, HEAD_PAD), F32)), True)
            cs, acc = lax.fori_loop(0, qi, lambda j, c, blk=blk: blk(qi - 1 - j, c, False), carry)
            o_acc = o_acc + acc
            t_acc = jnp.where(hm, cs, t_acc)
        o_ref[...] = o_acc.astype(BF16)
        t_ref[...] = t_acc

    qspec, kspec = _attn_specs(s)
    return pl.pallas_call(
        body, name=name, grid=(width // HEAD_PAD, s // TQ),
        in_specs=[qspec, kspec, kspec], out_specs=[qspec, qspec],
        out_shape=[jax.ShapeDtypeStruct((s, width), BF16), jax.ShapeDtypeStruct((s, width), F32)],
        compiler_params=_cp(("parallel", "arbitrary")))(q, k, v)


def _sb_bwd_call(q, k, v, do, tot, name):
    s, width = q.shape
    hd = SB_WIDTH // HEADS
    scale = 1.0 / math.sqrt(hd)

    def body(q_ref, k_ref, v_ref, do_ref, t_ref, dq_ref, dk_ref, dv_ref):
        qi = pl.program_id(1)

        @pl.when(qi == 0)
        def _():
            dk_ref[...] = jnp.zeros_like(dk_ref)
            dv_ref[...] = jnp.zeros_like(dv_ref)

        qb = q_ref[...]
        dob = do_ref[...]
        strict = _tri(lambda r, c: c < r)
        upto = _tri(lambda r, c: r <= c).astype(BF16)
        before = _tri(lambda r, c: r < c).astype(BF16)
        lane = _lane((1, HEAD_PAD))
        dq_acc = jnp.zeros((TQ, HEAD_PAD), F32)
        for hh in range(HEAD_PAD // hd):
            hm = (lane >= hh * hd) & (lane < (hh + 1) * hd)
            qh = jnp.where(hm, qb, jnp.zeros_like(qb))
            doh = jnp.where(hm, dob, jnp.zeros_like(dob))
            tt = t_ref[:, hh * hd:hh * hd + 1]

            def blk(kb, carry, diag, qh=qh, doh=doh, tt=tt):
                cl, cg, dq = carry
                ks = pl.multiple_of(kb * TQ, TQ)
                kt = k_ref[pl.ds(ks, TQ), :]
                vt = v_ref[pl.ds(ks, TQ), :]
                ls, l1m = _sb_logs(_dot_nt(qh, kt) * scale)
                if diag:
                    l1m = jnp.where(strict, l1m, 0.0)
                a = jnp.exp(ls + (tt - cl - _dot_hilo(l1m, upto)))
                if diag:
                    a = jnp.where(strict, a, 0.0)
                g = a * _dot_nt(doh, vt)
                cex = cg + _dot_hilo(g, before)
                sg = jnp.exp(ls)
                dz = g * (1.0 - sg) - cex * sg
                if diag:
                    dz = jnp.where(strict, dz, 0.0)
                dzb = (dz * scale).astype(BF16)
                dv_ref[pl.ds(ks, TQ), :] += _dot_tn(a.astype(BF16), doh)
                dk_ref[pl.ds(ks, TQ), :] += _dot_tn(dzb, qh)
                return (cl + jnp.sum(l1m, axis=-1, keepdims=True), cg + jnp.sum(g, axis=-1, keepdims=True),
                        dq + _dot(dzb, kt))

            init = (jnp.zeros((TQ, 1), F32), jnp.zeros((TQ, 1), F32), jnp.zeros((TQ, HEAD_PAD), F32))
            carry = lax.fori_loop(0, qi, lambda kb, c, blk=blk: blk(kb, c, False), init)
            _, _, dq = blk(qi, carry, True)
            dq_acc = jnp.where(hm, dq, dq_acc)
        dq_ref[...] = dq_acc.astype(BF16)

    qspec, kspec = _attn_specs(s)
    return pl.pallas_call(
        body, name=name, grid=(width // HEAD_PAD, s // TQ),
        in_specs=[qspec, kspec, kspec, qspec, qspec], out_specs=[qspec, kspec, kspec],
        out_shape=[jax.ShapeDtypeStruct((s, width), BF16), jax.ShapeDtypeStruct((s, width), F32),
                   jax.ShapeDtypeStruct((s, width), F32)],
        compiler_params=_cp(("parallel", "arbitrary")))(q, k, v, do, tot)


def _merge_out_call(om, osb, gates, h, wbm, wbs, wo, name):
    s, d = h.shape
    tm = min(TM_SMALL, s)

    def body(om_ref, os_ref, g_ref, h_ref, wbm_ref, wbs_ref, wo_ref, h2_ref, bm_ref, bs_ref, mg_ref):
        bm = _dot(om_ref[...], wbm_ref[...])
        bs = _dot(os_ref[...], wbs_ref[...])
        mg = (_sigmoid(g_ref[:, :d]) * bm + _sigmoid(g_ref[:, d:]) * bs).astype(BF16)
        bm_ref[...] = bm
        bs_ref[...] = bs
        mg_ref[...] = mg
        h2_ref[...] = h_ref[...] + _dot(mg, wo_ref[...])

    return pl.pallas_call(
        body, name=name, grid=(s // tm,),
        in_specs=[_rows(tm, om.shape[1]), _rows(tm, SB_WIDTH), _rows(tm, 2 * d), _rows(tm, d),
                  _whole(wbm.shape), _whole(wbs.shape), _whole(wo.shape)],
        out_specs=[_rows(tm, d)] * 4,
        out_shape=[jax.ShapeDtypeStruct((s, d), F32), jax.ShapeDtypeStruct((s, d), F32),
                   jax.ShapeDtypeStruct((s, d), F32), jax.ShapeDtypeStruct((s, d), BF16)],
        compiler_params=_cp(("parallel",)))(om, osb, gates, h, wbm, wbs, wo)


def _ple_call(h, g, wg, p, wp, tgt, name):
    s, d = h.shape
    tm = min(TM_SMALL, s)

    def body(h_ref, g_ref, wg_ref, p_ref, wp_ref, t_ref, dh_ref, dhs_ref, un_ref, dgl_ref, dpp_ref, dg_ref, sq_ref):
        @pl.when(pl.program_id(0) == 0)
        def _():
            dg_ref[...] = jnp.zeros_like(dg_ref)
            sq_ref[...] = jnp.zeros_like(sq_ref)

        x = h_ref[...]
        gain = g_ref[...]
        r = _rstd(x, d)
        xh = x * r
        un = (xh * gain).astype(BF16)
        sg = _sigmoid(_dot(un, wg_ref[...]))
        pp = _dot(p_ref[...].astype(BF16), wp_ref[...])
        diff = (x + sg * pp) - t_ref[...]
        sq_ref[...] += jnp.sum(diff * diff, axis=0, keepdims=True)
        dy = diff * (1.0 / d)
        dgl = ((dy * pp) * (sg * (1.0 - sg))).astype(BF16)
        dun = _dot_nt(dgl, wg_ref[...])
        dg_ref[...] += jnp.sum(dun * xh, axis=0, keepdims=True)
        dh = dy + _rms_bwd(x, r, gain, dun, d)
        dh_ref[...] = dh
        dhs_ref[...] = (0.5 * dh).astype(BF16)
        un_ref[...] = un
        dgl_ref[...] = dgl
        dpp_ref[...] = (dy * sg).astype(BF16)

    bf = jax.ShapeDtypeStruct((s, d), BF16)
    vec = jax.ShapeDtypeStruct((1, d), F32)
    return pl.pallas_call(
        body, name=name, grid=(s // tm,),
        in_specs=[_rows(tm, d), _whole((1, d)), _whole(wg.shape), _rows(tm, PLE_DIM), _whole(wp.shape), _rows(tm, d)],
        out_specs=[_rows(tm, d)] * 5 + [_whole((1, d))] * 2,
        out_shape=[jax.ShapeDtypeStruct((s, d), F32), bf, bf, bf, bf, vec, vec],
        compiler_params=_cp(("arbitrary",)))(h, g, wg, p, wp, tgt)


def _ffn_bwd_a_call(dhs, a, b, wo, name):
    s, n = a.shape
    d = dhs.shape[1]
    tn = n // 2
    tm = min(TM, s)

    def body(dh_ref, a_ref, b_ref, wo_ref, da_ref, db_ref):
        dhm = _dot_nt(dh_ref[...], wo_ref[...])
        av = a_ref[...]
        sa = _sigmoid(av)
        da_ref[...] = (dhm * b_ref[...] * (sa * (1.0 + av * (1.0 - sa)))).astype(BF16)
        db_ref[...] = (dhm * (av * sa)).astype(BF16)

    blk = pl.BlockSpec((tm, tn), lambda j, i: (i, j))
    return pl.pallas_call(
        body, name=name, grid=(n // tn, s // tm),
        in_specs=[pl.BlockSpec((tm, d), lambda j, i: (i, 0)), blk, blk, pl.BlockSpec((tn, d), lambda j, i: (j, 0))],
        out_specs=[blk, blk],
        out_shape=[jax.ShapeDtypeStruct((s, n), BF16)] * 2,
        compiler_params=_cp(("parallel", "parallel")))(dhs, a, b, wo)


def _norm_bwd_call(dy_list, w_list, h, g, dh_in, name, half_out):
    s, d = h.shape
    tm = min(TM_SMALL, s)
    nk = len(dy_list)
    factor = 0.5 if half_out else 1.0

    def body(*refs):
        dy_refs = refs[:nk]
        w_refs = refs[nk:2 * nk]
        h_ref, g_ref, dhin_ref, dh_ref, dhb_ref, dg_ref = refs[2 * nk:]

        @pl.when(pl.program_id(0) == 0)
        def _():
            dg_ref[...] = jnp.zeros_like(dg_ref)

        du = _dot_nt(dy_refs[0][...], w_refs[0][...])
        for dy_ref, w_ref in zip(dy_refs[1:], w_refs[1:]):
            du = du + _dot_nt(dy_ref[...], w_ref[...])
        x = h_ref[...]
        r = _rstd(x, d)
        dg_ref[...] += jnp.sum(du * (x * r), axis=0, keepdims=True)
        dh = dhin_ref[...] + _rms_bwd(x, r, g_ref[...], du, d)
        dh_ref[...] = dh
        dhb_ref[...] = (factor * dh).astype(BF16)

    return pl.pallas_call(
        body, name=name, grid=(s // tm,),
        in_specs=[_rows(tm, dy.shape[1]) for dy in dy_list] + [_whole(w.shape) for w in w_list]
        + [_rows(tm, d), _whole((1, d)), _rows(tm, d)],
        out_specs=[_rows(tm, d), _rows(tm, d), _whole((1, d))],
        out_shape=[jax.ShapeDtypeStruct((s, d), F32), jax.ShapeDtypeStruct((s, d), BF16),
                   jax.ShapeDtypeStruct((1, d), F32)],
        compiler_params=_cp(("arbitrary",)))(*dy_list, *w_list, h, g, dh_in)


def _merge_bwd_call(dhb, gates, bm, bs, wo, wbm, wbs, name):
    s, d = bm.shape
    tm = min(TM_SMALL, s)

    def body(dh_ref, g_ref, bm_ref, bs_ref, wo_ref, wbm_ref, wbs_ref, dg_ref, dbm_ref, dbs_ref, dom_ref, dos_ref):
        dmg = _dot_nt(dh_ref[...], wo_ref[...])
        s1 = _sigmoid(g_ref[:, :d])
        s2 = _sigmoid(g_ref[:, d:])
        dg_ref[:, :d] = (dmg * bm_ref[...] * (s1 * (1.0 - s1))).astype(BF16)
        dg_ref[:, d:] = (dmg * bs_ref[...] * (s2 * (1.0 - s2))).astype(BF16)
        dbm = (dmg * s1).astype(BF16)
        dbs = (dmg * s2).astype(BF16)
        dbm_ref[...] = dbm
        dbs_ref[...] = dbs
        dom_ref[...] = _dot_nt(dbm, wbm_ref[...]).astype(BF16)
        dos_ref[...] = _dot_nt(dbs, wbs_ref[...]).astype(BF16)

    wm = wbm.shape[0]
    return pl.pallas_call(
        body, name=name, grid=(s // tm,),
        in_specs=[_rows(tm, d), _rows(tm, 2 * d), _rows(tm, d), _rows(tm, d),
                  _whole(wo.shape), _whole(wbm.shape), _whole(wbs.shape)],
        out_specs=[_rows(tm, 2 * d), _rows(tm, d), _rows(tm, d), _rows(tm, wm), _rows(tm, SB_WIDTH)],
        out_shape=[jax.ShapeDtypeStruct((s, 2 * d), BF16), jax.ShapeDtypeStruct((s, d), BF16),
                   jax.ShapeDtypeStruct((s, d), BF16), jax.ShapeDtypeStruct((s, wm), BF16),
                   jax.ShapeDtypeStruct((s, SB_WIDTH), BF16)],
        compiler_params=_cp(("parallel",)))(dhb, gates, bm, bs, wo, wbm, wbs)


def _mla_prep_bwd_call(cq, ckv, krope, pos, freq, sign, g_ql, g_kvl, g_qh, g_kh, wq, wkv, dq, dk, dv, name):
    s = cq.shape[0]
    tm = min(TM_SMALL, s)
    width = HEADS * HEAD_PAD

    def body(cq_ref, ckv_ref, kr_ref, pos_ref, freq_ref, sign_ref, gql_ref, gkvl_ref, gqh_ref, gkh_ref,
             wq_ref, wkv_ref, dq_ref, dk_ref, dv_ref,
             dcq_ref, dckv_ref, dkr_ref, dwq_ref, dwkv_ref, dgql_ref, dgkvl_ref, dgqh_ref, dgkh_ref, dqr_ref, dkv_ref):
        @pl.when(pl.program_id(0) == 0)
        def _():
            for ref in (dwq_ref, dwkv_ref, dgql_ref, dgkvl_ref, dgqh_ref, dgkh_ref):
                ref[...] = jnp.zeros_like(ref)

        cosv, ssv = _rope_tables(pos_ref, freq_ref, sign_ref)
        xq = cq_ref[...]
        rq = _rstd(xq, Q_LORA)
        cqn = ((xq * rq) * gql_ref[...]).astype(BF16)
        qr = _dot(cqn, wq_ref[...])
        xk = ckv_ref[...]
        rk = _rstd(xk, KV_LORA)
        ckvn = ((xk * rk) * gkvl_ref[...]).astype(BF16)
        kv = _dot(ckvn, wkv_ref[...])
        kr = kr_ref[...]
        lane = _lane((tm, HEAD_PAD))
        dkr = jnp.zeros((tm, HEAD_PAD), F32)
        dgqh = jnp.zeros((1, HEAD_PAD), F32)
        dgkh = jnp.zeros((1, HEAD_PAD), F32)
        for h in range(HEADS):
            sl = slice(h * HEAD_PAD, (h + 1) * HEAD_PAD)
            x = qr[:, sl]
            dx, dgh = _head_bwd(x, _rstd(x, MLA_QK), gqh_ref[...], cosv, ssv, dq_ref[:, sl])
            dqr_ref[:, sl] = dx.astype(BF16)
            dgqh = dgqh + dgh
            x = jnp.where(lane < MLA_NOPE, kv[:, sl], kr)
            dx, dgh = _head_bwd(x, _rstd(x, MLA_QK), gkh_ref[...], cosv, ssv, dk_ref[:, sl])
            dgkh = dgkh + dgh
            dkr = dkr + jnp.where(lane >= MLA_NOPE, dx, 0.0)
            dkv_ref[:, sl] = jnp.where(lane < MLA_NOPE, dx, dv_ref[:, sl]).astype(BF16)
        dgqh_ref[...] += dgqh
        dgkh_ref[...] += dgkh
        dkr_ref[...] = dkr.astype(BF16)
        dqr = dqr_ref[...]
        dkvb = dkv_ref[...]
        dwq_ref[...] += _dot_tn(cqn, dqr)
        dwkv_ref[...] += _dot_tn(ckvn, dkvb)
        dcqn = _dot_nt(dqr, wq_ref[...])
        dgql_ref[...] += jnp.sum(dcqn * (xq * rq), axis=0, keepdims=True)
        dcq_ref[...] = _rms_bwd(xq, rq, gql_ref[...], dcqn, Q_LORA).astype(BF16)
        dckvn = _dot_nt(dkvb, wkv_ref[...])
        dgkvl_ref[...] += jnp.sum(dckvn * (xk * rk), axis=0, keepdims=True)
        dckv_ref[...] = _rms_bwd(xk, rk, gkvl_ref[...], dckvn, KV_LORA).astype(BF16)

    vec = lambda n: jax.ShapeDtypeStruct((1, n), F32)
    outs = pl.pallas_call(
        body, name=name, grid=(s // tm,),
        in_specs=[_rows(tm, Q_LORA), _rows(tm, KV_LORA), _rows(tm, HEAD_PAD), _rows(tm, 1),
                  _whole((1, HEAD_PAD)), _whole((1, HEAD_PAD)), _whole((1, Q_LORA)), _whole((1, KV_LORA)),
                  _whole((1, HEAD_PAD)), _whole((1, HEAD_PAD)), _whole((Q_LORA, width)), _whole((KV_LORA, width)),
                  _rows(tm, width), _rows(tm, width), _rows(tm, width)],
        out_specs=[_rows(tm, Q_LORA), _rows(tm, KV_LORA), _rows(tm, HEAD_PAD), _whole((Q_LORA, width)),
                   _whole((KV_LORA, width)), _whole((1, Q_LORA)), _whole((1, KV_LORA)), _whole((1, HEAD_PAD)),
                   _whole((1, HEAD_PAD)), _rows(tm, width), _rows(tm, width)],
        out_shape=[jax.ShapeDtypeStruct((s, Q_LORA), BF16), jax.ShapeDtypeStruct((s, KV_LORA), BF16),
                   jax.ShapeDtypeStruct((s, HEAD_PAD), BF16), jax.ShapeDtypeStruct((Q_LORA, width), F32),
                   jax.ShapeDtypeStruct((KV_LORA, width), F32), vec(Q_LORA), vec(KV_LORA), vec(HEAD_PAD), vec(HEAD_PAD),
                   jax.ShapeDtypeStruct((s, width), BF16), jax.ShapeDtypeStruct((s, width), BF16)],
        compiler_params=_cp(("arbitrary",)))(cq, ckv, krope, pos, freq, sign, g_ql, g_kvl, g_qh, g_kh, wq, wkv, dq, dk, dv)
    return outs[:9]


def _tn_call(a, b, name):
    s, ka = a.shape
    nb = b.shape[1]
    ti = _pick(ka, (512, 256, 128))
    tj = _pick(nb, (1024, 512, 256, 128))
    ts = min(1024, s)
    ns = s // ts

    def body(a_ref, b_ref, o_ref):
        part = _dot_tn(a_ref[...].astype(BF16), b_ref[...].astype(BF16))

        @pl.when(pl.program_id(2) == 0)
        def _():
            o_ref[...] = part

        @pl.when(pl.program_id(2) != 0)
        def _():
            o_ref[...] += part

    return pl.pallas_call(
        body, name=name, grid=(ka // ti, nb // tj, ns),
        in_specs=[pl.BlockSpec((ts, ti), lambda i, j, t: (t, i)), pl.BlockSpec((ts, tj), lambda i, j, t: (t, j))],
        out_specs=pl.BlockSpec((ti, tj), lambda i, j, t: (i, j)),
        out_shape=jax.ShapeDtypeStruct((ka, nb), F32),
        compiler_params=_cp(("parallel", "parallel", "arbitrary")))(a, b)


def _sum_call(parts, out_dtype, name):
    n, r, w = parts.shape
    tr = _pick(r, (448, 392, 256, 128, 64, 32, 16, 8))

    def body(p_ref, o_ref):
        acc = p_ref[0].astype(F32)
        for k in range(1, n):
            acc = acc + p_ref[k].astype(F32)
        o_ref[...] = acc.astype(out_dtype)

    return pl.pallas_call(
        body, name=name, grid=(r // tr,),
        in_specs=[pl.BlockSpec((n, tr, w), lambda i: (0, i, 0))], out_specs=_rows(tr, w),
        out_shape=jax.ShapeDtypeStruct((r, w), out_dtype), compiler_params=_cp(("parallel",)))(parts)


def _add_call(a, b, out_dtype, name):
    n, r, w = a.shape
    tr = _pick(r, (448, 392, 256, 128, 64, 32, 16, 8))

    def body(a_ref, b_ref, o_ref):
        o_ref[...] = (a_ref[...] + b_ref[...]).astype(out_dtype)

    spec = pl.BlockSpec((None, tr, w), lambda k, i: (k, i, 0))
    return pl.pallas_call(
        body, name=name, grid=(n, r // tr), in_specs=[spec, spec], out_specs=spec,
        out_shape=jax.ShapeDtypeStruct((n, r, w), out_dtype), compiler_params=_cp(("parallel", "parallel")))(a, b)


def _adamw_call(w, g, m, v, name):
    r, c = w.shape
    tr = _pick(r, (256, 128, 64, 32, 16, 8))

    def body(w_ref, g_ref, m_ref, v_ref, d_ref, nm_ref, nv_ref):
        gg = g_ref[...]
        nm = ADAM_B1 * m_ref[...] + (1.0 - ADAM_B1) * gg
        nv = ADAM_B2 * v_ref[...] + (1.0 - ADAM_B2) * (gg * gg)
        m_hat = nm / (1.0 - ADAM_B1 ** ADAM_STEP)
        v_hat = nv / (1.0 - ADAM_B2 ** ADAM_STEP)
        d_ref[...] = -ADAM_LR * (m_hat / (jnp.sqrt(v_hat) + ADAM_EPS) + ADAM_WD * w_ref[...])
        nm_ref[...] = nm
        nv_ref[...] = nv

    out = jax.ShapeDtypeStruct((r, c), F32)
    return pl.pallas_call(
        body, name=name, grid=(r // tr,), in_specs=[_rows(tr, c)] * 4, out_specs=[_rows(tr, c)] * 3,
        out_shape=[out, out, out], compiler_params=_cp(("parallel",)))(w, g, m, v)


HBM = pl.BlockSpec(memory_space=pl.ANY)


def _position():
    x, y, c = lax.axis_index("x"), lax.axis_index("y"), lax.axis_index("c")
    chips = [(1 - x, y), (x, 1 - y), (1 - x, 1 - y)]
    return x, y, c, chips


def _gather_call(flat, name):
    rows, w = flat.shape
    half = rows // 2

    def body(in_ref, out_ref, send_sems, recv_sems, fwd_send, fwd_recv, local_sem):
        x, y, c, chips = _position()
        me = 2 * x + y
        mine = pltpu.make_async_copy(in_ref, out_ref.at[me], local_sem)
        mine.start()

        def part(chip, core):
            return out_ref.at[2 * chip[0] + chip[1], pl.ds(core * half, half), :]

        first = []
        for j, chip in enumerate(chips):
            cp = pltpu.make_async_remote_copy(
                src_ref=in_ref.at[pl.ds(c * half, half), :], dst_ref=part((x, y), c),
                send_sem=send_sems.at[j], recv_sem=recv_sems.at[j], device_id=(*chip, c), device_id_type=MESH)
            cp.start()
            first.append(cp)
        passed = []
        for j, chip in enumerate(chips):
            first[j].wait_recv()
            cp = pltpu.make_async_remote_copy(
                src_ref=part(chip, c), dst_ref=part(chip, c),
                send_sem=fwd_send.at[j], recv_sem=fwd_recv.at[j], device_id=(x, y, 1 - c), device_id_type=MESH)
            cp.start()
            passed.append(cp)
        for cp in passed:
            cp.wait_recv()
        for cp in first + passed:
            cp.wait_send()
        mine.wait()

    return pl.pallas_call(
        body, name=name, in_specs=[HBM], out_specs=HBM,
        out_shape=jax.ShapeDtypeStruct((N_CHIPS, rows, w), flat.dtype),
        scratch_shapes=[pltpu.SemaphoreType.DMA((3,)), pltpu.SemaphoreType.DMA((3,)), pltpu.SemaphoreType.DMA((3,)),
                        pltpu.SemaphoreType.DMA((3,)), pltpu.SemaphoreType.DMA])(flat)


def _pair_send_call(full, name):
    n, rows, w = full.shape
    half = rows // 2

    def body(in_ref, out_ref, send_sem, recv_sem):
        x, y, c, _ = _position()
        cp = pltpu.make_async_remote_copy(
            src_ref=in_ref.at[:, pl.ds((1 - c) * half, half), :], dst_ref=out_ref,
            send_sem=send_sem, recv_sem=recv_sem, device_id=(x, y, 1 - c), device_id_type=MESH)
        cp.start()
        cp.wait()

    return pl.pallas_call(
        body, name=name, in_specs=[HBM], out_specs=HBM,
        out_shape=jax.ShapeDtypeStruct((n, half, w), full.dtype),
        scratch_shapes=[pltpu.SemaphoreType.DMA, pltpu.SemaphoreType.DMA])(full)


def _chip_scatter_call(parts, name):
    n, r, w = parts.shape

    def body(in_ref, out_ref, send_sems, recv_sems, local_sem):
        x, y, c, chips = _position()
        me = 2 * x + y
        mine = pltpu.make_async_copy(in_ref.at[me], out_ref.at[me], local_sem)
        mine.start()
        copies = []
        for j, chip in enumerate(chips):
            cp = pltpu.make_async_remote_copy(
                src_ref=in_ref.at[2 * chip[0] + chip[1]], dst_ref=out_ref.at[me],
                send_sem=send_sems.at[j], recv_sem=recv_sems.at[j], device_id=(*chip, c), device_id_type=MESH)
            cp.start()
            copies.append(cp)
        for cp in copies:
            cp.wait()
        mine.wait()

    return pl.pallas_call(
        body, name=name, in_specs=[HBM], out_specs=HBM,
        out_shape=jax.ShapeDtypeStruct((n, r, w), parts.dtype),
        scratch_shapes=[pltpu.SemaphoreType.DMA((3,)), pltpu.SemaphoreType.DMA((3,)), pltpu.SemaphoreType.DMA])(parts)


def _pair_gather_call(mine_half, name):
    r, w = mine_half.shape

    def body(in_ref, out_ref, send_sem, recv_sem, local_sem):
        x, y, c, _ = _position()
        own = pltpu.make_async_copy(in_ref, out_ref.at[c], local_sem)
        own.start()
        cp = pltpu.make_async_remote_copy(
            src_ref=in_ref, dst_ref=out_ref.at[c], send_sem=send_sem, recv_sem=recv_sem,
            device_id=(x, y, 1 - c), device_id_type=MESH)
        cp.start()
        cp.wait()
        own.wait()

    return pl.pallas_call(
        body, name=name, in_specs=[HBM], out_specs=HBM,
        out_shape=jax.ShapeDtypeStruct((2, r, w), mine_half.dtype),
        scratch_shapes=[pltpu.SemaphoreType.DMA, pltpu.SemaphoreType.DMA, pltpu.SemaphoreType.DMA])(mine_half)


def _all_gather_small_call(block, name):
    r, w = block.shape

    def body(in_ref, out_ref, send_sems, recv_sems, local_sem):
        x, y, c, _ = _position()
        me = 4 * x + 2 * y + c
        own = pltpu.make_async_copy(in_ref, out_ref.at[me], local_sem)
        own.start()
        copies = []
        for k in range(1, 8):
            peer = (x ^ (k >> 2), y ^ ((k >> 1) & 1), c ^ (k & 1))
            cp = pltpu.make_async_remote_copy(
                src_ref=in_ref, dst_ref=out_ref.at[me], send_sem=send_sems.at[k - 1], recv_sem=recv_sems.at[k - 1],
                device_id=peer, device_id_type=MESH)
            cp.start()
            copies.append(cp)
        for cp in copies:
            cp.wait()
        own.wait()

    return pl.pallas_call(
        body, name=name, in_specs=[HBM], out_specs=HBM,
        out_shape=jax.ShapeDtypeStruct((8, r, w), block.dtype),
        scratch_shapes=[pltpu.SemaphoreType.DMA((7,)), pltpu.SemaphoreType.DMA((7,)), pltpu.SemaphoreType.DMA])(block)


BIG = {
    "ffn1_w_in": ((D_MODEL, 2 * D_FF), 1), "ffn1_w_out": ((D_FF, D_MODEL), 0),
    "w_in": ((D_MODEL, 4256), 1), "w_q_up": ((Q_LORA, HEADS * MLA_QK), 1), "w_kv_up": ((KV_LORA, 1024), 1),
    "w_branch_mla": ((512, D_MODEL), 1), "w_branch_sb": ((SB_WIDTH, D_MODEL), 1), "w_out": ((D_MODEL, D_MODEL), 0),
    "ffn2_w_in": ((D_MODEL, 2 * D_FF), 1), "ffn2_w_out": ((D_FF, D_MODEL), 0),
    "w_ple_gate": ((D_MODEL, D_MODEL), 0), "w_ple_proj": ((PLE_DIM, D_MODEL), 1),
}
GAINS = {"ffn1_norm": 1024, "mix_norm": 1024, "q_latent_norm": 384, "kv_latent_norm": 256, "q_head_norm": 96,
         "k_head_norm": 96, "ffn2_norm": 1024, "ple_norm": 1024}
WEIGHT_ORDER = ["ffn1_norm", "ffn1_w_in", "ffn1_w_out", "mix_norm", "w_in", "q_latent_norm", "w_q_up",
                "kv_latent_norm", "w_kv_up", "q_head_norm", "k_head_norm", "w_branch_mla", "w_branch_sb", "w_out",
                "ffn2_norm", "ffn2_w_in", "ffn2_w_out", "ple_norm", "w_ple_gate", "w_ple_proj"]


def _shard_shape(name):
    (r, c), axis = BIG[name]
    return (r // N_CHIPS, c) if axis == 0 else (r, c // N_CHIPS)


def _flat_rows(name):
    r, c = _shard_shape(name)
    rows = r * c // FLAT_W
    return rows, -(-rows // ROW_ALIGN) * ROW_ALIGN


def _flatten_shards(shards):
    parts = []
    for name in BIG:
        a = shards[name]
        rows, padded = _flat_rows(name)
        a = a.reshape(a.shape[:-2] + (rows, FLAT_W))
        if padded != rows:
            a = jnp.pad(a, [(0, 0)] * (a.ndim - 2) + [(0, padded - rows), (0, 0)])
        parts.append(a)
    return jnp.concatenate(parts, axis=-2)


def _unflatten_shards(flat):
    out, at = {}, 0
    for name in BIG:
        rows, padded = _flat_rows(name)
        out[name] = flat[..., at:at + rows, :].reshape(flat.shape[:-2] + _shard_shape(name))
        at += padded
    return out


def _to_shards(name, full):
    (r, c), axis = BIG[name]
    if axis == 0:
        return full.reshape(N_CHIPS, r // N_CHIPS, c)
    return full.reshape(r, N_CHIPS, c // N_CHIPS).transpose(1, 0, 2)


def _from_shards(name, shards):
    (r, c), axis = BIG[name]
    if axis == 0:
        return shards.reshape(r, c)
    return shards.transpose(1, 0, 2).reshape(r, c)


def _relayout_w_in(w):
    d = w.shape[0]
    z = lambda n: jnp.zeros((d, n), w.dtype)
    return jnp.concatenate([w[:, :640], z(MLA_NOPE), w[:, 640:672], z(HEAD_PAD - MLA_QK), w[:, 672:]], axis=1)


def _unlayout_w_in(g):
    return jnp.concatenate([g[:, :640], g[:, 640 + MLA_NOPE:640 + MLA_QK], g[:, 768:]], axis=1)


def _pad_heads(v):
    lead = v.shape[:-1]
    return jnp.pad(v.reshape(lead + (HEADS, MLA_QK)), [(0, 0)] * len(lead) + [(0, 0), (0, HEAD_PAD - MLA_QK)]).reshape(
        lead + (HEADS * HEAD_PAD,))


def _halves(w):
    n = w.shape[1] // 2
    return [w[:, :n], w[:, n:]]


def _local_step(x, p, pos, tgt, gains, wts):
    d = D_MODEL
    inv_freq = ROPE_BASE ** (-jnp.arange(0, MLA_ROPE, 2, dtype=F32) / MLA_ROPE)
    zeros = lambda n: jnp.zeros((n,), F32)
    freq = jnp.concatenate([zeros(MLA_NOPE), inv_freq, inv_freq, zeros(HEAD_PAD - MLA_QK)])[None]
    sign = jnp.concatenate([zeros(MLA_NOPE), -jnp.ones((16,), F32), jnp.ones((16,), F32), zeros(HEAD_PAD - MLA_QK)])[None]
    pad_gain = lambda g: jnp.pad(g, ((0, 0), (0, HEAD_PAD - MLA_QK)))
    g_qh, g_kh = pad_gain(gains["q_head_norm"]), pad_gain(gains["k_head_norm"])
    w_in = _relayout_w_in(wts["w_in"])
    wq = _pad_heads(wts["w_q_up"])
    wkv = wts["w_kv_up"]
    wbm = jnp.pad(wts["w_branch_mla"].reshape(HEADS, 64, d), ((0, 0), (64, 0), (0, 0))).reshape(HEADS * HEAD_PAD, d)
    wbs, wo = wts["w_branch_sb"], wts["w_out"]

    u1 = _norm_call(x, gains["ffn1_norm"], "norm_ffn1")
    a1, b1, hm1 = _ffn_in_call(u1, wts["ffn1_w_in"], "ffn1_in")
    h1 = _ffn_out_call(hm1, wts["ffn1_w_out"], x, "ffn1_out")
    um = _norm_call(h1, gains["mix_norm"], "norm_mix")
    cq, ckv, krope, sbq, sbk, sbv, gates = _mix_in_call(um, w_in, "mix_in")
    prep_args = (cq, ckv, krope, pos, freq, sign, gains["q_latent_norm"], gains["kv_latent_norm"], g_qh, g_kh, wq, wkv)
    q, k, v = _mla_prep_call(*prep_args, "mla_prep")
    om, lse = _mla_fwd_call(q, k, v, "mla_fwd")
    osb, tot = _sb_fwd_call(sbq, sbk, sbv, "sb_fwd")
    h2, bm, bs, mg = _merge_out_call(om, osb, gates, h1, wbm, wbs, wo, "merge_out")
    u2 = _norm_call(h2, gains["ffn2_norm"], "norm_ffn2")
    a2, b2, hm2 = _ffn_in_call(u2, wts["ffn2_w_in"], "ffn2_in")
    h3 = _ffn_out_call(hm2, wts["ffn2_w_out"], h2, "ffn2_out")

    grads = {}
    dh3, dh3s, un, dgl, dpp, grads["ple_norm"], sq = _ple_call(
        h3, gains["ple_norm"], wts["w_ple_gate"], p, wts["w_ple_proj"], tgt, "ple")
    grads["w_ple_gate"] = _tn_call(un, dgl, "dw_ple_gate")
    grads["w_ple_proj"] = _tn_call(p, dpp, "dw_ple_proj")

    da2, db2 = _ffn_bwd_a_call(dh3s, a2, b2, wts["ffn2_w_out"], "ffn2_bwd_act")
    grads["ffn2_w_out"] = _tn_call(hm2, dh3s, "dw_ffn2_out")
    grads["ffn2_w_in"] = jnp.concatenate([_tn_call(u2, da2, "dw_ffn2_in_a"), _tn_call(u2, db2, "dw_ffn2_in_b")], axis=1)
    dh2, dh2b, grads["ffn2_norm"] = _norm_bwd_call([da2, db2], _halves(wts["ffn2_w_in"]), h2, gains["ffn2_norm"], dh3,
                                                   "ffn2_bwd_norm", half_out=False)

    dgates, dbm, dbs, dom, dos = _merge_bwd_call(dh2b, gates, bm, bs, wo, wbm, wbs, "merge_bwd")
    grads["w_out"] = _tn_call(mg, dh2b, "dw_out")
    grads["w_branch_mla"] = _tn_call(om, dbm, "dw_branch_mla").reshape(HEADS, HEAD_PAD, d)[:, 64:, :].reshape(512, d)
    grads["w_branch_sb"] = _tn_call(osb, dbs, "dw_branch_sb")
    dq, dk, dv = _mla_bwd_call(q, k, v, om, dom, lse, "mla_bwd")
    dsq, dsk, dsv = _sb_bwd_call(sbq, sbk, sbv, dos, tot, "sb_bwd")
    (dcq, dckv, dkr, dwq, grads["w_kv_up"], grads["q_latent_norm"], grads["kv_latent_norm"], dgqh, dgkh) = \
        _mla_prep_bwd_call(*prep_args, dq, dk, dv, "mla_prep_bwd")
    grads["w_q_up"] = dwq.reshape(Q_LORA, HEADS, HEAD_PAD)[:, :, :MLA_QK].reshape(Q_LORA, HEADS * MLA_QK)
    grads["q_head_norm"], grads["k_head_norm"] = dgqh[:, :MLA_QK], dgkh[:, :MLA_QK]
    dproj = jnp.concatenate([dcq, dckv, dkr, dsq, dsk.astype(BF16), dsv.astype(BF16), dgates], axis=1)
    grads["w_in"] = _unlayout_w_in(_tn_call(um, dproj, "dw_in"))
    dh1, dh1s, grads["mix_norm"] = _norm_bwd_call([dproj], [w_in], h1, gains["mix_norm"], dh2, "mix_bwd_norm",
                                                  half_out=True)

    da1, db1 = _ffn_bwd_a_call(dh1s, a1, b1, wts["ffn1_w_out"], "ffn1_bwd_act")
    grads["ffn1_w_out"] = _tn_call(hm1, dh1s, "dw_ffn1_out")
    grads["ffn1_w_in"] = jnp.concatenate([_tn_call(u1, da1, "dw_ffn1_in_a"), _tn_call(u1, db1, "dw_ffn1_in_b")], axis=1)
    dx, _, grads["ffn1_norm"] = _norm_bwd_call([da1, db1], _halves(wts["ffn1_w_in"]), x, gains["ffn1_norm"], dh1,
                                               "ffn1_bwd_norm", half_out=False)
    return sq, dx, grads


def kernel(x, p, positions, ffn1_norm, ffn1_w_in, ffn1_w_out, mix_norm, w_in, q_latent_norm, w_q_up, kv_latent_norm, w_kv_up, q_head_norm, k_head_norm, w_branch_mla, w_branch_sb, w_out, ffn2_norm, ffn2_w_in, ffn2_w_out, ple_norm, w_ple_gate, w_ple_proj, loss_target, m_ffn1_norm, m_ffn1_w_in, m_ffn1_w_out, m_mix_norm, m_w_in, m_q_latent_norm, m_w_q_up, m_kv_latent_norm, m_w_kv_up, m_q_head_norm, m_k_head_norm, m_w_branch_mla, m_w_branch_sb, m_w_out, m_ffn2_norm, m_ffn2_w_in, m_ffn2_w_out, m_ple_norm, m_w_ple_gate, m_w_ple_proj, v_ffn1_norm, v_ffn1_w_in, v_ffn1_w_out, v_mix_norm, v_w_in, v_q_latent_norm, v_w_q_up, v_kv_latent_norm, v_w_kv_up, v_q_head_norm, v_k_head_norm, v_w_branch_mla, v_w_branch_sb, v_w_out, v_ffn2_norm, v_ffn2_w_in, v_ffn2_w_out, v_ple_norm, v_w_ple_gate, v_w_ple_proj):
    given = dict(locals())
    w_shard = {n: given[n][0] for n in WEIGHT_ORDER}
    m_shard = {n: given["m_" + n][0] for n in WEIGHT_ORDER}
    v_shard = {n: given["v_" + n][0] for n in WEIGHT_ORDER}
    gains = {n: w_shard[n][None] for n in GAINS}

    flat_w = _flatten_shards({n: w_shard[n].astype(BF16) for n in BIG})
    gathered = _unflatten_shards(_gather_call(flat_w, "gather_weights"))
    wts = {n: _from_shards(n, gathered[n]) for n in BIG}

    sq, dx, grads = _local_step(x[0], p[0, 0], positions.reshape(-1, 1), loss_target[0], gains, wts)
    loss = lax.psum(0.5 * jnp.sum(sq) / D_MODEL, ("x", "y", "c"))

    c = lax.axis_index("c")
    flat_g = _flatten_shards({n: _to_shards(n, grads[n]) for n in BIG})
    half = flat_g.shape[1] // 2
    from_sibling = _pair_send_call(flat_g, "grads_pair_send")
    mine = lax.dynamic_slice_in_dim(flat_g, c * half, half, axis=1)
    pair_sum = _add_call(mine, from_sibling, BF16, "grads_pair_sum")
    by_chip = _chip_scatter_call(pair_sum, "grads_chip_scatter")
    reduced_half = _sum_call(by_chip, F32, "grads_chip_sum")
    reduced = _pair_gather_call(reduced_half, "grads_pair_gather")
    g_shard = _unflatten_shards(reduced.reshape(2 * half, FLAT_W))

    gain_block = jnp.concatenate([jnp.pad(grads[n], ((0, 0), (0, D_MODEL - GAINS[n]))) for n in GAINS], axis=0)
    gain_sum = _sum_call(_all_gather_small_call(gain_block, "gains_all_gather"), F32, "gains_sum")
    for i, n in enumerate(GAINS):
        g_shard[n] = gain_sum[i, :GAINS[n]]

    outs = {"grad": {}, "delta": {}, "new_m": {}, "new_v": {}}
    gain_pack = lambda t: jnp.concatenate([jnp.pad(t[n][None], ((0, 0), (0, D_MODEL - GAINS[n]))) for n in GAINS], axis=0)
    gd, gm, gv = _adamw_call(gain_pack(w_shard), gain_sum, gain_pack(m_shard), gain_pack(v_shard), "adamw_gains")
    for i, n in enumerate(GAINS):
        outs["grad"][n] = g_shard[n][None]
        for kind, t in (("delta", gd), ("new_m", gm), ("new_v", gv)):
            outs[kind][n] = t[i, :GAINS[n]][None]
    for n in BIG:
        dlt, nm, nv = _adamw_call(w_shard[n], g_shard[n], m_shard[n], v_shard[n], "adamw_" + n)
        outs["grad"][n] = g_shard[n][None]
        outs["delta"][n], outs["new_m"][n], outs["new_v"][n] = dlt[None], nm[None], nv[None]

    return (loss, dx[None], *[outs["grad"][n] for n in WEIGHT_ORDER], *[outs["delta"][n] for n in WEIGHT_ORDER],
            *[outs["new_m"][n] for n in WEIGHT_ORDER], *[outs["new_v"][n] for n in WEIGHT_ORDER])
```

```python
import functools
import math

import jax
import jax.numpy as jnp
from jax import lax
from jax.experimental import pallas as pl
from jax.experimental.pallas import tpu as pltpu

F32 = jnp.float32
BF16 = jnp.bfloat16
MESH = pl.DeviceIdType.MESH

D_MODEL = 1024
D_FF = 2816
PLE_DIM = 256
NORM_EPS = 1e-6
HEADS = 8
MLA_NOPE = 64
MLA_ROPE = 32
MLA_QK = 96
Q_LORA = 384
KV_LORA = 256
SB_WIDTH = 512
ROPE_BASE = 10000.0
HEAD_PAD = 128
N_CHIPS = 4

ADAM_LR = 0.001
ADAM_B1 = 0.9
ADAM_B2 = 0.999
ADAM_EPS = 1e-08
ADAM_WD = 0.01
ADAM_STEP = 10

SEG_CQ = (0, 384)
SEG_CKV = (384, 256)
SEG_KROPE = (640, 128)
SEG_SBQ = (768, 512)
SEG_SBK = (1280, 512)
SEG_SBV = (1792, 512)
SEG_GATES = (2304, 2048)
IN_COLS_PAD = 4352

TM = 512
TM_SMALL = 256
TQ = 256
MLA_FWD_BLOCKS = 4
MLA_BWD_BLOCKS = 2
SB_FWD_BLOCKS = 2
SB_BWD_BLOCKS = 1
SB_HEAD = 64
SB_SCALE = 0.125
TN_MAX_COLS = 2816
FLAT_W = 1024
ROW_ALIGN = 32
FLAT_TILE = 256
FLAT_ALIGN = 2 * FLAT_TILE
VMEM_LIMIT = 56 * 1024 * 1024

NT = (((1,), (1,)), ((), ()))
TN = (((0,), (0,)), ((), ()))


def _cp(sem):
    return pltpu.CompilerParams(dimension_semantics=sem, vmem_limit_bytes=VMEM_LIMIT)


def _rows(tm, w):
    return pl.BlockSpec((tm, w), lambda i: (i, 0))


def _whole(shape):
    return pl.BlockSpec(shape, lambda i: (0,) * len(shape))


def _dot(a, b):
    return jnp.dot(a, b, preferred_element_type=F32)


def _dot_nt(a, b):
    return lax.dot_general(a, b, NT, preferred_element_type=F32)


def _dot_tn(a, b):
    return lax.dot_general(a, b, TN, preferred_element_type=F32)


def _rstd(x, n):
    return lax.rsqrt(jnp.sum(x * x, axis=-1, keepdims=True) / n + NORM_EPS)


def _rms_bwd(x, r, g, dy, n):
    gy = dy * g
    return r * gy - x * ((r * r * r) * (jnp.sum(x * gy, axis=-1, keepdims=True) / n))


def _sigmoid(x):
    return jax.nn.sigmoid(x)


def _pick(n, cands):
    for c in cands:
        if n % c == 0:
            return c
    return n


def _norm_call(h, g, name):
    s, d = h.shape
    tm = min(TM, s)

    def body(h_ref, g_ref, u_ref):
        x = h_ref[...]
        u_ref[...] = ((x * _rstd(x, d)) * g_ref[...]).astype(BF16)

    return pl.pallas_call(
        body, name=name, grid=(s // tm,),
        in_specs=[_rows(tm, d), _whole((1, d))], out_specs=_rows(tm, d),
        out_shape=jax.ShapeDtypeStruct((s, d), BF16), compiler_params=_cp(("parallel",)))(h, g)


def _ffn_in_call(u, w, name):
    s, d = u.shape
    n = w.shape[1] // 2
    tn = n // 2
    tm = min(TM, s)
    nj = n // tn

    def body(u_ref, wa_ref, wb_ref, a_ref, b_ref, hm_ref):
        uu = u_ref[...]
        a = _dot(uu, wa_ref[...])
        b = _dot(uu, wb_ref[...])
        a_ref[...] = a
        b_ref[...] = b
        hm_ref[...] = ((a * _sigmoid(a)) * b).astype(BF16)

    blk = pl.BlockSpec((tm, tn), lambda j, i: (i, j))
    return pl.pallas_call(
        body, name=name, grid=(nj, s // tm),
        in_specs=[pl.BlockSpec((tm, d), lambda j, i: (i, 0)),
                  pl.BlockSpec((d, tn), lambda j, i: (0, j)),
                  pl.BlockSpec((d, tn), lambda j, i: (0, j + nj))],
        out_specs=[blk, blk, blk],
        out_shape=[jax.ShapeDtypeStruct((s, n), F32), jax.ShapeDtypeStruct((s, n), F32),
                   jax.ShapeDtypeStruct((s, n), BF16)],
        compiler_params=_cp(("parallel", "parallel")))(u, w, w)


def _ffn_out_call(hm, w, h, name):
    s, n = hm.shape
    d = w.shape[1]
    tm = min(TM, s)

    def body(hm_ref, w_ref, h_ref, o_ref):
        o_ref[...] = h_ref[...] + 0.5 * _dot(hm_ref[...], w_ref[...])

    return pl.pallas_call(
        body, name=name, grid=(s // tm,),
        in_specs=[_rows(tm, n), _whole((n, d)), _rows(tm, d)], out_specs=_rows(tm, d),
        out_shape=jax.ShapeDtypeStruct((s, d), F32), compiler_params=_cp(("parallel",)))(hm, w, h)


def _mix_in_call(u, w, name):
    s, d = u.shape
    tm = min(TM_SMALL, s)
    segs = [(SEG_CQ, F32), (SEG_CKV, F32), (SEG_KROPE, F32), (SEG_SBQ, BF16), (SEG_SBK, BF16),
            (SEG_SBV, BF16), (SEG_GATES, F32)]

    def body(u_ref, w_ref, *outs):
        uu = u_ref[...]
        for ((off, width), _), o_ref in zip(segs, outs):
            o_ref[...] = _dot(uu, w_ref[:, off:off + width]).astype(o_ref.dtype)

    return pl.pallas_call(
        body, name=name, grid=(s // tm,),
        in_specs=[_rows(tm, d), _whole((d, IN_COLS_PAD))],
        out_specs=[_rows(tm, width) for (_, width), _ in segs],
        out_shape=[jax.ShapeDtypeStruct((s, width), dt) for (_, width), dt in segs],
        compiler_params=_cp(("parallel",)))(u, w)


def _lane(shape):
    return lax.broadcasted_iota(jnp.int32, shape, len(shape) - 1)


def _rot_half(y):
    lane = _lane(y.shape)
    swapped = jnp.where(lane < MLA_NOPE + MLA_ROPE // 2, pltpu.roll(y, HEAD_PAD - 16, 1), pltpu.roll(y, 16, 1))
    return jnp.where((lane >= MLA_NOPE) & (lane < MLA_QK), swapped, 0.0)


def _rope_tables(pos_ref, freq_ref, sign_ref):
    ang = pos_ref[...].astype(F32) * freq_ref[...]
    return jnp.cos(ang), jnp.sin(ang) * sign_ref[...]


def _head_fwd(x, g, cosv, ssv):
    r = _rstd(x, MLA_QK)
    y = (x * r) * g
    return y * cosv + _rot_half(y) * ssv, r


def _head_bwd(x, r, g, cosv, ssv, dout):
    dy = dout * cosv + _rot_half(dout * ssv)
    return _rms_bwd(x, r, g, dy, MLA_QK), jnp.sum(dy * (x * r), axis=0, keepdims=True)


def _mla_prep_call(cq, ckv, krope, pos, freq, sign, g_ql, g_kvl, g_qh, g_kh, wq, wkv, name):
    s = cq.shape[0]
    tm = min(TM_SMALL, s)
    width = HEADS * HEAD_PAD

    def body(cq_ref, ckv_ref, kr_ref, pos_ref, freq_ref, sign_ref, gql_ref, gkvl_ref, gqh_ref, gkh_ref,
             wq_ref, wkv_ref, q_ref, k_ref, v_ref):
        cosv, ssv = _rope_tables(pos_ref, freq_ref, sign_ref)
        x = cq_ref[...]
        qr = _dot(((x * _rstd(x, Q_LORA)) * gql_ref[...]).astype(BF16), wq_ref[...])
        x = ckv_ref[...]
        kv = _dot(((x * _rstd(x, KV_LORA)) * gkvl_ref[...]).astype(BF16), wkv_ref[...])
        kr = kr_ref[...]
        lane = _lane((tm, HEAD_PAD))
        for h in range(HEADS):
            sl = slice(h * HEAD_PAD, (h + 1) * HEAD_PAD)
            qh, _ = _head_fwd(qr[:, sl], gqh_ref[...], cosv, ssv)
            q_ref[:, sl] = qh.astype(BF16)
            kvh = kv[:, sl]
            kh, _ = _head_fwd(jnp.where(lane < MLA_NOPE, kvh, kr), gkh_ref[...], cosv, ssv)
            k_ref[:, sl] = kh.astype(BF16)
            v_ref[:, sl] = jnp.where(lane >= MLA_NOPE, kvh, 0.0).astype(BF16)

    out = jax.ShapeDtypeStruct((s, width), BF16)
    return pl.pallas_call(
        body, name=name, grid=(s // tm,),
        in_specs=[_rows(tm, Q_LORA), _rows(tm, KV_LORA), _rows(tm, HEAD_PAD), _rows(tm, 1),
                  _whole((1, HEAD_PAD)), _whole((1, HEAD_PAD)), _whole((1, Q_LORA)), _whole((1, KV_LORA)),
                  _whole((1, HEAD_PAD)), _whole((1, HEAD_PAD)), _whole((Q_LORA, width)), _whole((KV_LORA, width))],
        out_specs=[_rows(tm, width)] * 3, out_shape=[out, out, out],
        compiler_params=_cp(("parallel",)))(cq, ckv, krope, pos, freq, sign, g_ql, g_kvl, g_qh, g_kh, wq, wkv)


def _attn_specs(s, nb):
    qspec = pl.BlockSpec((TQ, nb * HEAD_PAD), lambda g, i: (i, g))
    kspec = pl.BlockSpec((s, nb * HEAD_PAD), lambda g, i: (0, g))
    return qspec, kspec


def _lanes(b):
    return slice(b * HEAD_PAD, (b + 1) * HEAD_PAD)


def _tri(cmp):
    r = lax.broadcasted_iota(jnp.int32, (TQ, TQ), 0)
    c = lax.broadcasted_iota(jnp.int32, (TQ, TQ), 1)
    return cmp(r, c)


def _mla_fwd_call(q, k, v, name):
    s, width = q.shape
    scale = 1.0 / math.sqrt(MLA_QK)

    nb = MLA_FWD_BLOCKS

    def body(q_ref, k_ref, v_ref, o_ref, lse_ref):
        qi = pl.program_id(1)
        qs = [q_ref[:, _lanes(b)] for b in range(nb)]
        causal = _tri(lambda r, c: c <= r)

        def step(kb, carry, diag):
            ks = pl.multiple_of(kb * TQ, TQ)
            out = []
            for b in range(nb):
                m, l, acc = carry[b]
                kt = k_ref[pl.ds(ks, TQ), _lanes(b)]
                vt = v_ref[pl.ds(ks, TQ), _lanes(b)]
                sc = _dot_nt(qs[b], kt) * scale
                if diag:
                    sc = jnp.where(causal, sc, -1e30)
                mn = jnp.maximum(m, jnp.max(sc, axis=-1, keepdims=True))
                al = jnp.exp(m - mn)
                p = jnp.exp(sc - mn)
                out.append((mn, al * l + jnp.sum(p, axis=-1, keepdims=True), al * acc + _dot(p.astype(BF16), vt)))
            return tuple(out)

        init = tuple((jnp.full((TQ, 1), -1e30, F32), jnp.zeros((TQ, 1), F32), jnp.zeros((TQ, HEAD_PAD), F32))
                     for _ in range(nb))
        carry = step(qi, init, True)
        carry = lax.fori_loop(0, qi, lambda kb, c: step(kb, c, False), carry)
        for b in range(nb):
            m, l, acc = carry[b]
            o_ref[:, _lanes(b)] = (acc / l).astype(BF16)
            lse_ref[:, _lanes(b)] = jnp.broadcast_to(m + jnp.log(l), (TQ, HEAD_PAD))

    qspec, kspec = _attn_specs(s, nb)
    return pl.pallas_call(
        body, name=name, grid=(width // (nb * HEAD_PAD), s // TQ),
        in_specs=[qspec, kspec, kspec], out_specs=[qspec, qspec],
        out_shape=[jax.ShapeDtypeStruct((s, width), BF16), jax.ShapeDtypeStruct((s, width), F32)],
        compiler_params=_cp(("parallel", "arbitrary")))(q, k, v)


def _mla_bwd_call(q, k, v, o, do, lse, name):
    s, width = q.shape
    scale = 1.0 / math.sqrt(MLA_QK)
    nb = MLA_BWD_BLOCKS

    def body(q_ref, k_ref, v_ref, o_ref, do_ref, lse_ref, dq_ref, dk_ref, dv_ref):
        qi = pl.program_id(1)

        @pl.when(qi == 0)
        def _():
            dk_ref[...] = jnp.zeros_like(dk_ref)
            dv_ref[...] = jnp.zeros_like(dv_ref)

        qs = [q_ref[:, _lanes(b)] for b in range(nb)]
        dos = [do_ref[:, _lanes(b)] for b in range(nb)]
        lses = [lse_ref[:, b * HEAD_PAD:b * HEAD_PAD + 1] for b in range(nb)]
        dlts = [jnp.sum(dos[b].astype(F32) * o_ref[:, _lanes(b)].astype(F32), axis=-1, keepdims=True) for b in range(nb)]
        causal = _tri(lambda r, c: c <= r)

        def step(kb, dqs, diag):
            ks = pl.multiple_of(kb * TQ, TQ)
            out = []
            for b in range(nb):
                kt = k_ref[pl.ds(ks, TQ), _lanes(b)]
                vt = v_ref[pl.ds(ks, TQ), _lanes(b)]
                p = jnp.exp(_dot_nt(qs[b], kt) * scale - lses[b])
                if diag:
                    p = jnp.where(causal, p, 0.0)
                ds = (p * (_dot_nt(dos[b], vt) - dlts[b]) * scale).astype(BF16)
                dv_ref[pl.ds(ks, TQ), _lanes(b)] += _dot_tn(p.astype(BF16), dos[b])
                dk_ref[pl.ds(ks, TQ), _lanes(b)] += _dot_tn(ds, qs[b])
                out.append(dqs[b] + _dot(ds, kt))
            return tuple(out)

        dqs = step(qi, tuple(jnp.zeros((TQ, HEAD_PAD), F32) for _ in range(nb)), True)
        dqs = lax.fori_loop(0, qi, lambda kb, c: step(kb, c, False), dqs)
        for b in range(nb):
            dq_ref[:, _lanes(b)] = dqs[b]

    qspec, kspec = _attn_specs(s, nb)
    out = jax.ShapeDtypeStruct((s, width), F32)
    return pl.pallas_call(
        body, name=name, grid=(width // (nb * HEAD_PAD), s // TQ),
        in_specs=[qspec, kspec, kspec, qspec, qspec, qspec], out_specs=[qspec, kspec, kspec],
        out_shape=[out, out, out],
        compiler_params=_cp(("parallel", "arbitrary")))(q, k, v, o, do, lse)


def _dot_hilo(x, u):
    hi = x.astype(BF16)
    lo = (x - hi.astype(F32)).astype(BF16)
    return _dot(hi, u) + _dot(lo, u)


def _sb_logs(z):
    sp = jnp.log(1.0 + jnp.exp(-jnp.abs(z)))
    return jnp.minimum(z, 0.0) - sp, jnp.minimum(-z, 0.0) - sp


def _sb_head_q(qb, first, hh):
    keep = first if hh == 0 else jnp.logical_not(first)
    return jnp.where(keep, qb, jnp.zeros_like(qb)) * jnp.asarray(SB_SCALE, qb.dtype)


def _sb_fwd_call(q, k, v, name):
    s, width = q.shape
    nb = SB_FWD_BLOCKS
    chains = [(b, hh) for b in range(nb) for hh in range(HEAD_PAD // SB_HEAD)]

    def body(q_ref, k_ref, v_ref, o_ref, t_ref):
        qi = pl.program_id(1)
        strict = _tri(lambda r, c: c < r)
        after = _tri(lambda r, c: r > c).astype(BF16)
        first = _lane((1, HEAD_PAD)) < SB_HEAD
        qhs = [_sb_head_q(q_ref[:, _lanes(b)], first, hh) for b, hh in chains]

        def step(kb, carry, diag):
            ks = pl.multiple_of(kb * TQ, TQ)
            out = []
            for ci, (b, hh) in enumerate(chains):
                cs, acc = carry[ci]
                kt = k_ref[pl.ds(ks, TQ), _lanes(b)]
                vt = v_ref[pl.ds(ks, TQ), _lanes(b)]
                ls, l1m = _sb_logs(_dot_nt(qhs[ci], kt))
                if diag:
                    l1m = jnp.where(strict, l1m, 0.0)
                a = jnp.exp(ls + _dot_hilo(l1m, after) + cs)
                if diag:
                    a = jnp.where(strict, a, 0.0)
                out.append((cs + jnp.sum(l1m, axis=-1, keepdims=True), acc + _dot(a.astype(BF16), vt)))
            return tuple(out)

        init = tuple((jnp.zeros((TQ, 1), F32), jnp.zeros((TQ, HEAD_PAD), F32)) for _ in chains)
        carry = step(qi, init, True)
        carry = lax.fori_loop(0, qi, lambda j, c: step(qi - 1 - j, c, False), carry)
        for b in range(nb):
            (cs0, acc0), (cs1, acc1) = carry[2 * b], carry[2 * b + 1]
            o_ref[:, _lanes(b)] = jnp.where(first, acc0, acc1).astype(BF16)
            t_ref[:, _lanes(b)] = jnp.where(first, cs0, cs1)

    qspec, kspec = _attn_specs(s, nb)
    return pl.pallas_call(
        body, name=name, grid=(width // (nb * HEAD_PAD), s // TQ),
        in_specs=[qspec, kspec, kspec], out_specs=[qspec, qspec],
        out_shape=[jax.ShapeDtypeStruct((s, width), BF16), jax.ShapeDtypeStruct((s, width), F32)],
        compiler_params=_cp(("parallel", "arbitrary")))(q, k, v)


def _sb_bwd_call(q, k, v, do, tot, name):
    s, width = q.shape
    nb = SB_BWD_BLOCKS
    chains = [(b, hh) for b in range(nb) for hh in range(HEAD_PAD // SB_HEAD)]

    def body(q_ref, k_ref, v_ref, do_ref, t_ref, dq_ref, dk_ref, dv_ref):
        qi = pl.program_id(1)

        @pl.when(qi == 0)
        def _():
            dk_ref[...] = jnp.zeros_like(dk_ref)
            dv_ref[...] = jnp.zeros_like(dv_ref)

        strict = _tri(lambda r, c: c < r)
        upto = _tri(lambda r, c: r <= c).astype(BF16)
        before = _tri(lambda r, c: r < c).astype(BF16)
        first = _lane((1, HEAD_PAD)) < SB_HEAD
        qhs = [_sb_head_q(q_ref[:, _lanes(b)], first, hh) for b, hh in chains]
        dohs = []
        for b, hh in chains:
            dob = do_ref[:, _lanes(b)]
            dohs.append(jnp.where(first if hh == 0 else jnp.logical_not(first), dob, jnp.zeros_like(dob)))
        tts = [t_ref[:, b * HEAD_PAD + hh * SB_HEAD:b * HEAD_PAD + hh * SB_HEAD + 1] for b, hh in chains]

        def step(kb, carry, diag):
            ks = pl.multiple_of(kb * TQ, TQ)
            out = []
            dkv = [[None, None] for _ in range(nb)]
            for ci, (b, hh) in enumerate(chains):
                cl, cg, dq = carry[ci]
                kt = k_ref[pl.ds(ks, TQ), _lanes(b)]
                vt = v_ref[pl.ds(ks, TQ), _lanes(b)]
                ls, l1m = _sb_logs(_dot_nt(qhs[ci], kt))
                if diag:
                    l1m = jnp.where(strict, l1m, 0.0)
                a = jnp.exp(ls + (tts[ci] - cl - _dot_hilo(l1m, upto)))
                if diag:
                    a = jnp.where(strict, a, 0.0)
                g = a * _dot_nt(dohs[ci], vt)
                cex = cg + _dot_hilo(g, before)
                dz = g - jnp.exp(ls) * (g + cex)
                if diag:
                    dz = jnp.where(strict, dz, 0.0)
                dzb = dz.astype(BF16)
                dvp = _dot_tn(a.astype(BF16), dohs[ci])
                dkp = _dot_tn(dzb, qhs[ci])
                dkv[b] = [dkp, dvp] if dkv[b][0] is None else [dkv[b][0] + dkp, dkv[b][1] + dvp]
                out.append((cl + jnp.sum(l1m, axis=-1, keepdims=True), cg + jnp.sum(g, axis=-1, keepdims=True),
                            dq + _dot(dzb, kt)))
            for b in range(nb):
                dk_ref[pl.ds(ks, TQ), _lanes(b)] += dkv[b][0]
                dv_ref[pl.ds(ks, TQ), _lanes(b)] += dkv[b][1]
            return tuple(out)

        init = tuple((jnp.zeros((TQ, 1), F32), jnp.zeros((TQ, 1), F32), jnp.zeros((TQ, HEAD_PAD), F32)) for _ in chains)
        carry = lax.fori_loop(0, qi, lambda kb, c: step(kb, c, False), init)
        carry = step(qi, carry, True)
        for b in range(nb):
            dq_ref[:, _lanes(b)] = (jnp.where(first, carry[2 * b][2], carry[2 * b + 1][2]) * SB_SCALE).astype(BF16)

    qspec, kspec = _attn_specs(s, nb)
    return pl.pallas_call(
        body, name=name, grid=(width // (nb * HEAD_PAD), s // TQ),
        in_specs=[qspec, kspec, kspec, qspec, qspec], out_specs=[qspec, kspec, kspec],
        out_shape=[jax.ShapeDtypeStruct((s, width), BF16), jax.ShapeDtypeStruct((s, width), F32),
                   jax.ShapeDtypeStruct((s, width), F32)],
        compiler_params=_cp(("parallel", "arbitrary")))(q, k, v, do, tot)


def _merge_out_call(om, osb, gates, h, wbm, wbs, wo, name):
    s, d = h.shape
    tm = min(TM_SMALL, s)

    def body(om_ref, os_ref, g_ref, h_ref, wbm_ref, wbs_ref, wo_ref, h2_ref, bm_ref, bs_ref, mg_ref):
        bm = _dot(om_ref[...], wbm_ref[...])
        bs = _dot(os_ref[...], wbs_ref[...])
        mg = (_sigmoid(g_ref[:, :d]) * bm + _sigmoid(g_ref[:, d:]) * bs).astype(BF16)
        bm_ref[...] = bm
        bs_ref[...] = bs
        mg_ref[...] = mg
        h2_ref[...] = h_ref[...] + _dot(mg, wo_ref[...])

    return pl.pallas_call(
        body, name=name, grid=(s // tm,),
        in_specs=[_rows(tm, om.shape[1]), _rows(tm, SB_WIDTH), _rows(tm, 2 * d), _rows(tm, d),
                  _whole(wbm.shape), _whole(wbs.shape), _whole(wo.shape)],
        out_specs=[_rows(tm, d)] * 4,
        out_shape=[jax.ShapeDtypeStruct((s, d), F32), jax.ShapeDtypeStruct((s, d), F32),
                   jax.ShapeDtypeStruct((s, d), F32), jax.ShapeDtypeStruct((s, d), BF16)],
        compiler_params=_cp(("parallel",)))(om, osb, gates, h, wbm, wbs, wo)


def _ple_call(h, g, wg, p, wp, tgt, name):
    s, d = h.shape
    tm = min(TM_SMALL, s)

    def body(h_ref, g_ref, wg_ref, p_ref, wp_ref, t_ref, dh_ref, dhs_ref, un_ref, dgl_ref, dpp_ref, dg_ref, sq_ref):
        @pl.when(pl.program_id(0) == 0)
        def _():
            dg_ref[...] = jnp.zeros_like(dg_ref)
            sq_ref[...] = jnp.zeros_like(sq_ref)

        x = h_ref[...]
        gain = g_ref[...]
        r = _rstd(x, d)
        xh = x * r
        un = (xh * gain).astype(BF16)
        sg = _sigmoid(_dot(un, wg_ref[...]))
        pp = _dot(p_ref[...].astype(BF16), wp_ref[...])
        diff = (x + sg * pp) - t_ref[...]
        sq_ref[...] += jnp.sum(diff * diff, axis=0, keepdims=True)
        dy = diff * (1.0 / d)
        dgl = ((dy * pp) * (sg * (1.0 - sg))).astype(BF16)
        dun = _dot_nt(dgl, wg_ref[...])
        dg_ref[...] += jnp.sum(dun * xh, axis=0, keepdims=True)
        dh = dy + _rms_bwd(x, r, gain, dun, d)
        dh_ref[...] = dh
        dhs_ref[...] = (0.5 * dh).astype(BF16)
        un_ref[...] = un
        dgl_ref[...] = dgl
        dpp_ref[...] = (dy * sg).astype(BF16)

    bf = jax.ShapeDtypeStruct((s, d), BF16)
    vec = jax.ShapeDtypeStruct((1, d), F32)
    return pl.pallas_call(
        body, name=name, grid=(s // tm,),
        in_specs=[_rows(tm, d), _whole((1, d)), _whole(wg.shape), _rows(tm, PLE_DIM), _whole(wp.shape), _rows(tm, d)],
        out_specs=[_rows(tm, d)] * 5 + [_whole((1, d))] * 2,
        out_shape=[jax.ShapeDtypeStruct((s, d), F32), bf, bf, bf, bf, vec, vec],
        compiler_params=_cp(("arbitrary",)))(h, g, wg, p, wp, tgt)


def _ffn_bwd_a_call(dhs, a, b, wo, name):
    s, n = a.shape
    d = dhs.shape[1]
    tn = n // 2
    tm = min(TM, s)

    def body(dh_ref, a_ref, b_ref, wo_ref, da_ref, db_ref):
        dhm = _dot_nt(dh_ref[...], wo_ref[...])
        av = a_ref[...]
        sa = _sigmoid(av)
        da_ref[...] = (dhm * b_ref[...] * (sa * (1.0 + av * (1.0 - sa)))).astype(BF16)
        db_ref[...] = (dhm * (av * sa)).astype(BF16)

    blk = pl.BlockSpec((tm, tn), lambda j, i: (i, j))
    return pl.pallas_call(
        body, name=name, grid=(n // tn, s // tm),
        in_specs=[pl.BlockSpec((tm, d), lambda j, i: (i, 0)), blk, blk, pl.BlockSpec((tn, d), lambda j, i: (j, 0))],
        out_specs=[blk, blk],
        out_shape=[jax.ShapeDtypeStruct((s, n), BF16)] * 2,
        compiler_params=_cp(("parallel", "parallel")))(dhs, a, b, wo)


def _norm_bwd_call(dy_list, w_list, h, g, dh_in, name, half_out):
    s, d = h.shape
    tm = min(TM_SMALL, s)
    nk = len(dy_list)
    factor = 0.5 if half_out else 1.0

    def body(*refs):
        dy_refs = refs[:nk]
        w_refs = refs[nk:2 * nk]
        h_ref, g_ref, dhin_ref, dh_ref, dhb_ref, dg_ref = refs[2 * nk:]

        @pl.when(pl.program_id(0) == 0)
        def _():
            dg_ref[...] = jnp.zeros_like(dg_ref)

        du = _dot_nt(dy_refs[0][...], w_refs[0][...])
        for dy_ref, w_ref in zip(dy_refs[1:], w_refs[1:]):
            du = du + _dot_nt(dy_ref[...], w_ref[...])
        x = h_ref[...]
        r = _rstd(x, d)
        dg_ref[...] += jnp.sum(du * (x * r), axis=0, keepdims=True)
        dh = dhin_ref[...] + _rms_bwd(x, r, g_ref[...], du, d)
        dh_ref[...] = dh
        dhb_ref[...] = (factor * dh).astype(BF16)

    return pl.pallas_call(
        body, name=name, grid=(s // tm,),
        in_specs=[_rows(tm, dy.shape[1]) for dy in dy_list] + [_whole(w.shape) for w in w_list]
        + [_rows(tm, d), _whole((1, d)), _rows(tm, d)],
        out_specs=[_rows(tm, d), _rows(tm, d), _whole((1, d))],
        out_shape=[jax.ShapeDtypeStruct((s, d), F32), jax.ShapeDtypeStruct((s, d), BF16),
                   jax.ShapeDtypeStruct((1, d), F32)],
        compiler_params=_cp(("arbitrary",)))(*dy_list, *w_list, h, g, dh_in)


def _merge_bwd_call(dhb, gates, bm, bs, wo, wbm, wbs, name):
    s, d = bm.shape
    tm = min(TM_SMALL, s)

    def body(dh_ref, g_ref, bm_ref, bs_ref, wo_ref, wbm_ref, wbs_ref, dg_ref, dbm_ref, dbs_ref, dom_ref, dos_ref):
        dmg = _dot_nt(dh_ref[...], wo_ref[...])
        s1 = _sigmoid(g_ref[:, :d])
        s2 = _sigmoid(g_ref[:, d:])
        dg_ref[:, :d] = (dmg * bm_ref[...] * (s1 * (1.0 - s1))).astype(BF16)
        dg_ref[:, d:] = (dmg * bs_ref[...] * (s2 * (1.0 - s2))).astype(BF16)
        dbm = (dmg * s1).astype(BF16)
        dbs = (dmg * s2).astype(BF16)
        dbm_ref[...] = dbm
        dbs_ref[...] = dbs
        dom_ref[...] = _dot_nt(dbm, wbm_ref[...]).astype(BF16)
        dos_ref[...] = _dot_nt(dbs, wbs_ref[...]).astype(BF16)

    wm = wbm.shape[0]
    return pl.pallas_call(
        body, name=name, grid=(s // tm,),
        in_specs=[_rows(tm, d), _rows(tm, 2 * d), _rows(tm, d), _rows(tm, d),
                  _whole(wo.shape), _whole(wbm.shape), _whole(wbs.shape)],
        out_specs=[_rows(tm, 2 * d), _rows(tm, d), _rows(tm, d), _rows(tm, wm), _rows(tm, SB_WIDTH)],
        out_shape=[jax.ShapeDtypeStruct((s, 2 * d), BF16), jax.ShapeDtypeStruct((s, d), BF16),
                   jax.ShapeDtypeStruct((s, d), BF16), jax.ShapeDtypeStruct((s, wm), BF16),
                   jax.ShapeDtypeStruct((s, SB_WIDTH), BF16)],
        compiler_params=_cp(("parallel",)))(dhb, gates, bm, bs, wo, wbm, wbs)


def _mla_prep_bwd_call(cq, ckv, krope, pos, freq, sign, g_ql, g_kvl, g_qh, g_kh, wq, wkv, dq, dk, dv, name):
    s = cq.shape[0]
    tm = min(TM_SMALL, s)
    width = HEADS * HEAD_PAD

    def body(cq_ref, ckv_ref, kr_ref, pos_ref, freq_ref, sign_ref, gql_ref, gkvl_ref, gqh_ref, gkh_ref,
             wq_ref, wkv_ref, dq_ref, dk_ref, dv_ref,
             dcq_ref, dckv_ref, dkr_ref, dwq_ref, dwkv_ref, dgql_ref, dgkvl_ref, dgqh_ref, dgkh_ref, dqr_ref, dkv_ref):
        @pl.when(pl.program_id(0) == 0)
        def _():
            for ref in (dwq_ref, dwkv_ref, dgql_ref, dgkvl_ref, dgqh_ref, dgkh_ref):
                ref[...] = jnp.zeros_like(ref)

        cosv, ssv = _rope_tables(pos_ref, freq_ref, sign_ref)
        xq = cq_ref[...]
        rq = _rstd(xq, Q_LORA)
        cqn = ((xq * rq) * gql_ref[...]).astype(BF16)
        qr = _dot(cqn, wq_ref[...])
        xk = ckv_ref[...]
        rk = _rstd(xk, KV_LORA)
        ckvn = ((xk * rk) * gkvl_ref[...]).astype(BF16)
        kv = _dot(ckvn, wkv_ref[...])
        kr = kr_ref[...]
        lane = _lane((tm, HEAD_PAD))
        dkr = jnp.zeros((tm, HEAD_PAD), F32)
        dgqh = jnp.zeros((1, HEAD_PAD), F32)
        dgkh = jnp.zeros((1, HEAD_PAD), F32)
        for h in range(HEADS):
            sl = slice(h * HEAD_PAD, (h + 1) * HEAD_PAD)
            x = qr[:, sl]
            dx, dgh = _head_bwd(x, _rstd(x, MLA_QK), gqh_ref[...], cosv, ssv, dq_ref[:, sl])
            dqr_ref[:, sl] = dx.astype(BF16)
            dgqh = dgqh + dgh
            x = jnp.where(lane < MLA_NOPE, kv[:, sl], kr)
            dx, dgh = _head_bwd(x, _rstd(x, MLA_QK), gkh_ref[...], cosv, ssv, dk_ref[:, sl])
            dgkh = dgkh + dgh
            dkr = dkr + jnp.where(lane >= MLA_NOPE, dx, 0.0)
            dkv_ref[:, sl] = jnp.where(lane < MLA_NOPE, dx, dv_ref[:, sl]).astype(BF16)
        dgqh_ref[...] += dgqh
        dgkh_ref[...] += dgkh
        dkr_ref[...] = dkr.astype(BF16)
        dqr = dqr_ref[...]
        dkvb = dkv_ref[...]
        dwq_ref[...] += _dot_tn(cqn, dqr)
        dwkv_ref[...] += _dot_tn(ckvn, dkvb)
        dcqn = _dot_nt(dqr, wq_ref[...])
        dgql_ref[...] += jnp.sum(dcqn * (xq * rq), axis=0, keepdims=True)
        dcq_ref[...] = _rms_bwd(xq, rq, gql_ref[...], dcqn, Q_LORA).astype(BF16)
        dckvn = _dot_nt(dkvb, wkv_ref[...])
        dgkvl_ref[...] += jnp.sum(dckvn * (xk * rk), axis=0, keepdims=True)
        dckv_ref[...] = _rms_bwd(xk, rk, gkvl_ref[...], dckvn, KV_LORA).astype(BF16)

    vec = lambda n: jax.ShapeDtypeStruct((1, n), F32)
    outs = pl.pallas_call(
        body, name=name, grid=(s // tm,),
        in_specs=[_rows(tm, Q_LORA), _rows(tm, KV_LORA), _rows(tm, HEAD_PAD), _rows(tm, 1),
                  _whole((1, HEAD_PAD)), _whole((1, HEAD_PAD)), _whole((1, Q_LORA)), _whole((1, KV_LORA)),
                  _whole((1, HEAD_PAD)), _whole((1, HEAD_PAD)), _whole((Q_LORA, width)), _whole((KV_LORA, width)),
                  _rows(tm, width), _rows(tm, width), _rows(tm, width)],
        out_specs=[_rows(tm, Q_LORA), _rows(tm, KV_LORA), _rows(tm, HEAD_PAD), _whole((Q_LORA, width)),
                   _whole((KV_LORA, width)), _whole((1, Q_LORA)), _whole((1, KV_LORA)), _whole((1, HEAD_PAD)),
                   _whole((1, HEAD_PAD)), _rows(tm, width), _rows(tm, width)],
        out_shape=[jax.ShapeDtypeStruct((s, Q_LORA), BF16), jax.ShapeDtypeStruct((s, KV_LORA), BF16),
                   jax.ShapeDtypeStruct((s, HEAD_PAD), BF16), jax.ShapeDtypeStruct((Q_LORA, width), F32),
                   jax.ShapeDtypeStruct((KV_LORA, width), F32), vec(Q_LORA), vec(KV_LORA), vec(HEAD_PAD), vec(HEAD_PAD),
                   jax.ShapeDtypeStruct((s, width), BF16), jax.ShapeDtypeStruct((s, width), BF16)],
        compiler_params=_cp(("arbitrary",)))(cq, ckv, krope, pos, freq, sign, g_ql, g_kvl, g_qh, g_kh, wq, wkv, dq, dk, dv)
    return outs[:9]


def _tn_call(a, b, name):
    s, ka = a.shape
    nb = b.shape[1]
    ti = _pick(ka, (512, 256, 128))
    tj = nb if nb <= TN_MAX_COLS else _pick(nb, (2176, 1024, 512, 256, 128))
    ts = min(512, s)
    ns = s // ts

    def body(a_ref, b_ref, o_ref):
        part = _dot_tn(a_ref[...].astype(BF16), b_ref[...].astype(BF16))

        @pl.when(pl.program_id(2) == 0)
        def _():
            o_ref[...] = part

        @pl.when(pl.program_id(2) != 0)
        def _():
            o_ref[...] += part

    return pl.pallas_call(
        body, name=name, grid=(ka // ti, nb // tj, ns),
        in_specs=[pl.BlockSpec((ts, ti), lambda i, j, t: (t, i)), pl.BlockSpec((ts, tj), lambda i, j, t: (t, j))],
        out_specs=pl.BlockSpec((ti, tj), lambda i, j, t: (i, j)),
        out_shape=jax.ShapeDtypeStruct((ka, nb), F32),
        compiler_params=_cp(("parallel", "parallel", "arbitrary")))(a, b)


def _sum_call(parts, out_dtype, name):
    n, r, w = parts.shape
    tr = _pick(r, (FLAT_TILE, 8))

    def body(p_ref, o_ref):
        acc = p_ref[0].astype(F32)
        for k in range(1, n):
            acc = acc + p_ref[k].astype(F32)
        o_ref[...] = acc.astype(out_dtype)

    return pl.pallas_call(
        body, name=name, grid=(r // tr,),
        in_specs=[pl.BlockSpec((n, tr, w), lambda i: (0, i, 0))], out_specs=_rows(tr, w),
        out_shape=jax.ShapeDtypeStruct((r, w), out_dtype), compiler_params=_cp(("parallel",)))(parts)


def _pair_sum_call(full, other, core, out_dtype, name):
    n, r, w = other.shape
    tr = FLAT_TILE
    nblk = r // tr

    def body(c_ref, a_ref, b_ref, o_ref):
        o_ref[...] = (a_ref[...] + b_ref[...]).astype(out_dtype)

    spec = pl.BlockSpec((None, tr, w), lambda k, i, c_ref: (k, i, 0))
    return pl.pallas_call(
        body, name=name,
        grid_spec=pltpu.PrefetchScalarGridSpec(
            num_scalar_prefetch=1, grid=(n, nblk),
            in_specs=[pl.BlockSpec((None, tr, w), lambda k, i, c_ref: (k, c_ref[0] * nblk + i, 0)), spec],
            out_specs=spec),
        out_shape=jax.ShapeDtypeStruct((n, r, w), out_dtype),
        compiler_params=_cp(("parallel", "parallel")))(core.reshape(1).astype(jnp.int32), full, other)


def _adamw_call(w, g, m, v, name):
    r, c = w.shape
    tr = _pick(r, (256, 128, 64, 32, 16, 8))

    def body(w_ref, g_ref, m_ref, v_ref, d_ref, nm_ref, nv_ref):
        gg = g_ref[...]
        nm = ADAM_B1 * m_ref[...] + (1.0 - ADAM_B1) * gg
        nv = ADAM_B2 * v_ref[...] + (1.0 - ADAM_B2) * (gg * gg)
        m_hat = nm / (1.0 - ADAM_B1 ** ADAM_STEP)
        v_hat = nv / (1.0 - ADAM_B2 ** ADAM_STEP)
        d_ref[...] = -ADAM_LR * (m_hat / (jnp.sqrt(v_hat) + ADAM_EPS) + ADAM_WD * w_ref[...])
        nm_ref[...] = nm
        nv_ref[...] = nv

    out = jax.ShapeDtypeStruct((r, c), F32)
    return pl.pallas_call(
        body, name=name, grid=(r // tr,), in_specs=[_rows(tr, c)] * 4, out_specs=[_rows(tr, c)] * 3,
        out_shape=[out, out, out], compiler_params=_cp(("parallel",)))(w, g, m, v)


HBM = pl.BlockSpec(memory_space=pl.ANY)


def _position():
    x, y, c = lax.axis_index("x"), lax.axis_index("y"), lax.axis_index("c")
    chips = [(1 - x, y), (x, 1 - y), (1 - x, 1 - y)]
    return x, y, c, chips


def _gather_call(flat, name):
    rows, w = flat.shape
    half = rows // 2

    def body(in_ref, out_ref, send_sems, recv_sems, fwd_send, fwd_recv):
        x, y, c, chips = _position()

        def part(chip, core):
            return out_ref.at[2 * chip[0] + chip[1], pl.ds(core * half, half), :]

        first = []
        for j, chip in enumerate(chips):
            cp = pltpu.make_async_remote_copy(
                src_ref=in_ref.at[pl.ds(c * half, half), :], dst_ref=part((x, y), c),
                send_sem=send_sems.at[j], recv_sem=recv_sems.at[j], device_id=(*chip, c), device_id_type=MESH)
            cp.start()
            first.append(cp)
        passed = []
        for j, chip in enumerate(chips):
            first[j].wait_recv()
            cp = pltpu.make_async_remote_copy(
                src_ref=part(chip, c), dst_ref=part(chip, c),
                send_sem=fwd_send.at[j], recv_sem=fwd_recv.at[j], device_id=(x, y, 1 - c), device_id_type=MESH)
            cp.start()
            passed.append(cp)
        for cp in passed:
            cp.wait_recv()
        for cp in first + passed:
            cp.wait_send()

    return pl.pallas_call(
        body, name=name, in_specs=[HBM], out_specs=HBM,
        out_shape=jax.ShapeDtypeStruct((N_CHIPS, rows, w), flat.dtype),
        scratch_shapes=[pltpu.SemaphoreType.DMA((3,)), pltpu.SemaphoreType.DMA((3,)), pltpu.SemaphoreType.DMA((3,)),
                        pltpu.SemaphoreType.DMA((3,))])(flat)


def _pair_send_call(full, name):
    n, rows, w = full.shape
    half = rows // 2

    def body(in_ref, out_ref, send_sem, recv_sem):
        x, y, c, _ = _position()
        cp = pltpu.make_async_remote_copy(
            src_ref=in_ref.at[:, pl.ds((1 - c) * half, half), :], dst_ref=out_ref,
            send_sem=send_sem, recv_sem=recv_sem, device_id=(x, y, 1 - c), device_id_type=MESH)
        cp.start()
        cp.wait()

    return pl.pallas_call(
        body, name=name, in_specs=[HBM], out_specs=HBM,
        out_shape=jax.ShapeDtypeStruct((n, half, w), full.dtype),
        scratch_shapes=[pltpu.SemaphoreType.DMA, pltpu.SemaphoreType.DMA])(full)


def _chip_scatter_call(parts, name):
    n, r, w = parts.shape

    def body(in_ref, out_ref, send_sems, recv_sems):
        x, y, c, chips = _position()
        me = 2 * x + y
        copies = []
        for j, chip in enumerate(chips):
            cp = pltpu.make_async_remote_copy(
                src_ref=in_ref.at[2 * chip[0] + chip[1]], dst_ref=out_ref.at[me],
                send_sem=send_sems.at[j], recv_sem=recv_sems.at[j], device_id=(*chip, c), device_id_type=MESH)
            cp.start()
            copies.append(cp)
        for cp in copies:
            cp.wait()

    return pl.pallas_call(
        body, name=name, in_specs=[HBM], out_specs=HBM,
        out_shape=jax.ShapeDtypeStruct((n, r, w), parts.dtype),
        scratch_shapes=[pltpu.SemaphoreType.DMA((3,)), pltpu.SemaphoreType.DMA((3,))])(parts)


def _pair_swap_call(mine_half, name):
    r, w = mine_half.shape

    def body(in_ref, out_ref, send_sem, recv_sem):
        x, y, c, _ = _position()
        cp = pltpu.make_async_remote_copy(
            src_ref=in_ref, dst_ref=out_ref, send_sem=send_sem, recv_sem=recv_sem,
            device_id=(x, y, 1 - c), device_id_type=MESH)
        cp.start()
        cp.wait()

    return pl.pallas_call(
        body, name=name, in_specs=[HBM], out_specs=HBM,
        out_shape=jax.ShapeDtypeStruct((r, w), mine_half.dtype),
        scratch_shapes=[pltpu.SemaphoreType.DMA, pltpu.SemaphoreType.DMA])(mine_half)


def _all_gather_small_call(block, name):
    r, w = block.shape

    def body(in_ref, out_ref, send_sems, recv_sems, local_sem):
        x, y, c, _ = _position()
        me = 4 * x + 2 * y + c
        own = pltpu.make_async_copy(in_ref, out_ref.at[me], local_sem)
        own.start()
        copies = []
        for k in range(1, 8):
            peer = (x ^ (k >> 2), y ^ ((k >> 1) & 1), c ^ (k & 1))
            cp = pltpu.make_async_remote_copy(
                src_ref=in_ref, dst_ref=out_ref.at[me], send_sem=send_sems.at[k - 1], recv_sem=recv_sems.at[k - 1],
                device_id=peer, device_id_type=MESH)
            cp.start()
            copies.append(cp)
        for cp in copies:
            cp.wait()
        own.wait()

    return pl.pallas_call(
        body, name=name, in_specs=[HBM], out_specs=HBM,
        out_shape=jax.ShapeDtypeStruct((8, r, w), block.dtype),
        scratch_shapes=[pltpu.SemaphoreType.DMA((7,)), pltpu.SemaphoreType.DMA((7,)), pltpu.SemaphoreType.DMA])(block)


BIG = {
    "ffn1_w_in": ((D_MODEL, 2 * D_FF), 1), "ffn1_w_out": ((D_FF, D_MODEL), 0),
    "w_in": ((D_MODEL, 4256), 1), "w_q_up": ((Q_LORA, HEADS * MLA_QK), 1), "w_kv_up": ((KV_LORA, 1024), 1),
    "w_branch_mla": ((512, D_MODEL), 1), "w_branch_sb": ((SB_WIDTH, D_MODEL), 1), "w_out": ((D_MODEL, D_MODEL), 0),
    "ffn2_w_in": ((D_MODEL, 2 * D_FF), 1), "ffn2_w_out": ((D_FF, D_MODEL), 0),
    "w_ple_gate": ((D_MODEL, D_MODEL), 0), "w_ple_proj": ((PLE_DIM, D_MODEL), 1),
}
GAINS = {"ffn1_norm": 1024, "mix_norm": 1024, "q_latent_norm": 384, "kv_latent_norm": 256, "q_head_norm": 96,
         "k_head_norm": 96, "ffn2_norm": 1024, "ple_norm": 1024}
WEIGHT_ORDER = ["ffn1_norm", "ffn1_w_in", "ffn1_w_out", "mix_norm", "w_in", "q_latent_norm", "w_q_up",
                "kv_latent_norm", "w_kv_up", "q_head_norm", "k_head_norm", "w_branch_mla", "w_branch_sb", "w_out",
                "ffn2_norm", "ffn2_w_in", "ffn2_w_out", "ple_norm", "w_ple_gate", "w_ple_proj"]


def _shard_shape(name):
    (r, c), axis = BIG[name]
    return (r // N_CHIPS, c) if axis == 0 else (r, c // N_CHIPS)


def _flat_rows(name):
    r, c = _shard_shape(name)
    rows = r * c // FLAT_W
    return rows, -(-rows // ROW_ALIGN) * ROW_ALIGN


def _flatten_shards(shards):
    parts = []
    for name in BIG:
        a = shards[name]
        rows, padded = _flat_rows(name)
        a = a.reshape(a.shape[:-2] + (rows, FLAT_W))
        if padded != rows:
            a = jnp.pad(a, [(0, 0)] * (a.ndim - 2) + [(0, padded - rows), (0, 0)])
        parts.append(a)
    total = sum(_flat_rows(name)[1] for name in BIG)
    tail = -total % FLAT_ALIGN
    if tail:
        parts.append(jnp.zeros(parts[0].shape[:-2] + (tail, FLAT_W), parts[0].dtype))
    return jnp.concatenate(parts, axis=-2)


def _unflatten_shards(flat):
    out, at = {}, 0
    for name in BIG:
        rows, padded = _flat_rows(name)
        out[name] = flat[..., at:at + rows, :].reshape(flat.shape[:-2] + _shard_shape(name))
        at += padded
    return out


def _to_shards(name, full):
    (r, c), axis = BIG[name]
    if axis == 0:
        return full.reshape(N_CHIPS, r // N_CHIPS, c)
    return full.reshape(r, N_CHIPS, c // N_CHIPS).transpose(1, 0, 2)


def _from_shards(name, shards):
    (r, c), axis = BIG[name]
    if axis == 0:
        return shards.reshape(r, c)
    return shards.transpose(1, 0, 2).reshape(r, c)


def _relayout_w_in(w):
    d = w.shape[0]
    z = lambda n: jnp.zeros((d, n), w.dtype)
    return jnp.concatenate([w[:, :640], z(MLA_NOPE), w[:, 640:672], z(HEAD_PAD - MLA_QK), w[:, 672:]], axis=1)


def _unlayout_w_in(g):
    return jnp.concatenate([g[:, :640], g[:, 640 + MLA_NOPE:640 + MLA_QK], g[:, 768:]], axis=1)


def _pad_heads(v):
    lead = v.shape[:-1]
    return jnp.pad(v.reshape(lead + (HEADS, MLA_QK)), [(0, 0)] * len(lead) + [(0, 0), (0, HEAD_PAD - MLA_QK)]).reshape(
        lead + (HEADS * HEAD_PAD,))


def _halves(w):
    n = w.shape[1] // 2
    return [w[:, :n], w[:, n:]]


def _local_step(x, p, pos, tgt, gains, wts):
    d = D_MODEL
    inv_freq = ROPE_BASE ** (-jnp.arange(0, MLA_ROPE, 2, dtype=F32) / MLA_ROPE)
    zeros = lambda n: jnp.zeros((n,), F32)
    freq = jnp.concatenate([zeros(MLA_NOPE), inv_freq, inv_freq, zeros(HEAD_PAD - MLA_QK)])[None]
    sign = jnp.concatenate([zeros(MLA_NOPE), -jnp.ones((16,), F32), jnp.ones((16,), F32), zeros(HEAD_PAD - MLA_QK)])[None]
    pad_gain = lambda g: jnp.pad(g, ((0, 0), (0, HEAD_PAD - MLA_QK)))
    g_qh, g_kh = pad_gain(gains["q_head_norm"]), pad_gain(gains["k_head_norm"])
    w_in = _relayout_w_in(wts["w_in"])
    wq = _pad_heads(wts["w_q_up"])
    wkv = wts["w_kv_up"]
    wbm = jnp.pad(wts["w_branch_mla"].reshape(HEADS, 64, d), ((0, 0), (64, 0), (0, 0))).reshape(HEADS * HEAD_PAD, d)
    wbs, wo = wts["w_branch_sb"], wts["w_out"]

    u1 = _norm_call(x, gains["ffn1_norm"], "norm_ffn1")
    a1, b1, hm1 = _ffn_in_call(u1, wts["ffn1_w_in"], "ffn1_in")
    h1 = _ffn_out_call(hm1, wts["ffn1_w_out"], x, "ffn1_out")
    um = _norm_call(h1, gains["mix_norm"], "norm_mix")
    cq, ckv, krope, sbq, sbk, sbv, gates = _mix_in_call(um, w_in, "mix_in")
    prep_args = (cq, ckv, krope, pos, freq, sign, gains["q_latent_norm"], gains["kv_latent_norm"], g_qh, g_kh, wq, wkv)
    q, k, v = _mla_prep_call(*prep_args, "mla_prep")
    om, lse = _mla_fwd_call(q, k, v, "mla_fwd")
    osb, tot = _sb_fwd_call(sbq, sbk, sbv, "sb_fwd")
    h2, bm, bs, mg = _merge_out_call(om, osb, gates, h1, wbm, wbs, wo, "merge_out")
    u2 = _norm_call(h2, gains["ffn2_norm"], "norm_ffn2")
    a2, b2, hm2 = _ffn_in_call(u2, wts["ffn2_w_in"], "ffn2_in")
    h3 = _ffn_out_call(hm2, wts["ffn2_w_out"], h2, "ffn2_out")

    grads = {}
    dh3, dh3s, un, dgl, dpp, grads["ple_norm"], sq = _ple_call(
        h3, gains["ple_norm"], wts["w_ple_gate"], p, wts["w_ple_proj"], tgt, "ple")
    grads["w_ple_gate"] = _tn_call(un, dgl, "dw_ple_gate")
    grads["w_ple_proj"] = _tn_call(p, dpp, "dw_ple_proj")

    da2, db2 = _ffn_bwd_a_call(dh3s, a2, b2, wts["ffn2_w_out"], "ffn2_bwd_act")
    grads["ffn2_w_out"] = _tn_call(hm2, dh3s, "dw_ffn2_out")
    grads["ffn2_w_in"] = jnp.concatenate([_tn_call(u2, da2, "dw_ffn2_in_a"), _tn_call(u2, db2, "dw_ffn2_in_b")], axis=1)
    dh2, dh2b, grads["ffn2_norm"] = _norm_bwd_call([da2, db2], _halves(wts["ffn2_w_in"]), h2, gains["ffn2_norm"], dh3,
                                                   "ffn2_bwd_norm", half_out=False)

    dgates, dbm, dbs, dom, dos = _merge_bwd_call(dh2b, gates, bm, bs, wo, wbm, wbs, "merge_bwd")
    grads["w_out"] = _tn_call(mg, dh2b, "dw_out")
    grads["w_branch_mla"] = _tn_call(om, dbm, "dw_branch_mla").reshape(HEADS, HEAD_PAD, d)[:, 64:, :].reshape(512, d)
    grads["w_branch_sb"] = _tn_call(osb, dbs, "dw_branch_sb")
    dq, dk, dv = _mla_bwd_call(q, k, v, om, dom, lse, "mla_bwd")
    dsq, dsk, dsv = _sb_bwd_call(sbq, sbk, sbv, dos, tot, "sb_bwd")
    (dcq, dckv, dkr, dwq, grads["w_kv_up"], grads["q_latent_norm"], grads["kv_latent_norm"], dgqh, dgkh) = \
        _mla_prep_bwd_call(*prep_args, dq, dk, dv, "mla_prep_bwd")
    grads["w_q_up"] = dwq.reshape(Q_LORA, HEADS, HEAD_PAD)[:, :, :MLA_QK].reshape(Q_LORA, HEADS * MLA_QK)
    grads["q_head_norm"], grads["k_head_norm"] = dgqh[:, :MLA_QK], dgkh[:, :MLA_QK]
    dproj = jnp.concatenate([dcq, dckv, dkr, dsq, dsk.astype(BF16), dsv.astype(BF16), dgates], axis=1)
    grads["w_in"] = _unlayout_w_in(_tn_call(um, dproj, "dw_in"))
    dh1, dh1s, grads["mix_norm"] = _norm_bwd_call([dproj], [w_in], h1, gains["mix_norm"], dh2, "mix_bwd_norm",
                                                  half_out=True)

    da1, db1 = _ffn_bwd_a_call(dh1s, a1, b1, wts["ffn1_w_out"], "ffn1_bwd_act")
    grads["ffn1_w_out"] = _tn_call(hm1, dh1s, "dw_ffn1_out")
    grads["ffn1_w_in"] = jnp.concatenate([_tn_call(u1, da1, "dw_ffn1_in_a"), _tn_call(u1, db1, "dw_ffn1_in_b")], axis=1)
    dx, _, grads["ffn1_norm"] = _norm_bwd_call([da1, db1], _halves(wts["ffn1_w_in"]), x, gains["ffn1_norm"], dh1,
                                               "ffn1_bwd_norm", half_out=False)
    return sq, dx, grads


def kernel(x, p, positions, ffn1_norm, ffn1_w_in, ffn1_w_out, mix_norm, w_in, q_latent_norm, w_q_up, kv_latent_norm, w_kv_up, q_head_norm, k_head_norm, w_branch_mla, w_branch_sb, w_out, ffn2_norm, ffn2_w_in, ffn2_w_out, ple_norm, w_ple_gate, w_ple_proj, loss_target, m_ffn1_norm, m_ffn1_w_in, m_ffn1_w_out, m_mix_norm, m_w_in, m_q_latent_norm, m_w_q_up, m_kv_latent_norm, m_w_kv_up, m_q_head_norm, m_k_head_norm, m_w_branch_mla, m_w_branch_sb, m_w_out, m_ffn2_norm, m_ffn2_w_in, m_ffn2_w_out, m_ple_norm, m_w_ple_gate, m_w_ple_proj, v_ffn1_norm, v_ffn1_w_in, v_ffn1_w_out, v_mix_norm, v_w_in, v_q_latent_norm, v_w_q_up, v_kv_latent_norm, v_w_kv_up, v_q_head_norm, v_k_head_norm, v_w_branch_mla, v_w_branch_sb, v_w_out, v_ffn2_norm, v_ffn2_w_in, v_ffn2_w_out, v_ple_norm, v_w_ple_gate, v_w_ple_proj):
    given = dict(locals())
    w_shard = {n: given[n][0] for n in WEIGHT_ORDER}
    m_shard = {n: given["m_" + n][0] for n in WEIGHT_ORDER}
    v_shard = {n: given["v_" + n][0] for n in WEIGHT_ORDER}
    gains = {n: w_shard[n][None] for n in GAINS}

    c = lax.axis_index("c")
    chip = 2 * lax.axis_index("x") + lax.axis_index("y")
    flat_w = _flatten_shards({n: w_shard[n].astype(BF16) for n in BIG})
    others = _gather_call(flat_w, "gather_weights")
    gathered = _unflatten_shards(lax.dynamic_update_slice_in_dim(others, flat_w[None], chip, axis=0))
    wts = {n: _from_shards(n, gathered[n]) for n in BIG}

    sq, dx, grads = _local_step(x[0], p[0, 0], positions.reshape(-1, 1), loss_target[0], gains, wts)
    loss = lax.psum(0.5 * jnp.sum(sq) / D_MODEL, ("x", "y", "c"))

    flat_g = _flatten_shards({n: _to_shards(n, grads[n]) for n in BIG})
    from_sibling = _pair_send_call(flat_g, "grads_pair_send")
    pair_sum = _pair_sum_call(flat_g, from_sibling, c, BF16, "grads_pair_sum")
    by_chip = _chip_scatter_call(pair_sum, "grads_chip_scatter")
    own = lax.dynamic_slice_in_dim(pair_sum, chip, 1, axis=0)
    by_chip = lax.dynamic_update_slice_in_dim(by_chip, own, chip, axis=0)
    reduced_half = _sum_call(by_chip, F32, "grads_chip_sum")
    sibling_half = _pair_swap_call(reduced_half, "grads_pair_swap")
    reduced = jnp.where(c == 0, jnp.concatenate([reduced_half, sibling_half]),
                        jnp.concatenate([sibling_half, reduced_half]))
    g_shard = _unflatten_shards(reduced)

    gain_block = jnp.concatenate([jnp.pad(grads[n], ((0, 0), (0, D_MODEL - GAINS[n]))) for n in GAINS], axis=0)
    gain_sum = _sum_call(_all_gather_small_call(gain_block, "gains_all_gather"), F32, "gains_sum")
    for i, n in enumerate(GAINS):
        g_shard[n] = gain_sum[i, :GAINS[n]]

    outs = {"grad": {}, "delta": {}, "new_m": {}, "new_v": {}}
    gain_pack = lambda t: jnp.concatenate([jnp.pad(t[n][None], ((0, 0), (0, D_MODEL - GAINS[n]))) for n in GAINS], axis=0)
    gd, gm, gv = _adamw_call(gain_pack(w_shard), gain_sum, gain_pack(m_shard), gain_pack(v_shard), "adamw_gains")
    for i, n in enumerate(GAINS):
        outs["grad"][n] = g_shard[n][None]
        for kind, t in (("delta", gd), ("new_m", gm), ("new_v", gv)):
            outs[kind][n] = t[i, :GAINS[n]][None]
    for n in BIG:
        dlt, nm, nv = _adamw_call(w_shard[n], g_shard[n], m_shard[n], v_shard[n], "adamw_" + n)
        outs["grad"][n] = g_shard[n][None]
        outs["delta"][n], outs["new_m"][n], outs["new_v"][n] = dlt[None], nm[None], nv[None]

    return (loss, dx[None], *[outs["grad"][n] for n in WEIGHT_ORDER], *[outs["delta"][n] for n in WEIGHT_ORDER],
            *[outs["new_m"][n] for n in WEIGHT_ORDER], *[outs["new_v"][n] for n in WEIGHT_ORDER])
```

```python
import functools
import math

import jax
import jax.numpy as jnp
from jax import lax
from jax.experimental import pallas as pl
from jax.experimental.pallas import tpu as pltpu

F32 = jnp.float32
BF16 = jnp.bfloat16
MESH = pl.DeviceIdType.MESH

D_MODEL = 1024
D_FF = 2816
PLE_DIM = 256
NORM_EPS = 1e-6
HEADS = 8
MLA_NOPE = 64
MLA_ROPE = 32
MLA_QK = 96
Q_LORA = 384
KV_LORA = 256
SB_WIDTH = 512
ROPE_BASE = 10000.0
HEAD_PAD = 128
N_CHIPS = 4

ADAM_LR = 0.001
ADAM_B1 = 0.9
ADAM_B2 = 0.999
ADAM_EPS = 1e-08
ADAM_WD = 0.01
ADAM_STEP = 10

SEG_CQ = (0, 384)
SEG_CKV = (384, 256)
SEG_KROPE = (640, 128)
SEG_SBQ = (768, 512)
SEG_SBK = (1280, 512)
SEG_SBV = (1792, 512)
SEG_GATES = (2304, 2048)
IN_COLS_PAD = 4352

TM = 512
TM_SMALL = 256
TQ = 256
MLA_FWD_BLOCKS = 4
MLA_BWD_BLOCKS = 2
SB_FWD_BLOCKS = 2
SB_BWD_BLOCKS = 1
SB_HEAD = 64
SB_SCALE = 0.125
TN_MAX_COLS = 2816
FLAT_W = 1024
ROW_ALIGN = 32
FLAT_TILE = 256
FLAT_ALIGN = 2 * FLAT_TILE
VMEM_LIMIT = 56 * 1024 * 1024

NT = (((1,), (1,)), ((), ()))
TN = (((0,), (0,)), ((), ()))


def _cp(sem):
    return pltpu.CompilerParams(dimension_semantics=sem, vmem_limit_bytes=VMEM_LIMIT)


def _rows(tm, w):
    return pl.BlockSpec((tm, w), lambda i: (i, 0))


def _whole(shape):
    return pl.BlockSpec(shape, lambda i: (0,) * len(shape))


def _dot(a, b):
    return jnp.dot(a, b, preferred_element_type=F32)


def _dot_nt(a, b):
    return lax.dot_general(a, b, NT, preferred_element_type=F32)


def _dot_tn(a, b):
    return lax.dot_general(a, b, TN, preferred_element_type=F32)


def _rstd(x, n):
    return lax.rsqrt(jnp.sum(x * x, axis=-1, keepdims=True) / n + NORM_EPS)


def _rms_bwd(x, r, g, dy, n):
    gy = dy * g
    return r * gy - x * ((r * r * r) * (jnp.sum(x * gy, axis=-1, keepdims=True) / n))


def _sigmoid(x):
    return jax.nn.sigmoid(x)


def _pick(n, cands):
    for c in cands:
        if n % c == 0:
            return c
    return n


def _norm_call(h, g, name):
    s, d = h.shape
    tm = min(TM, s)

    def body(h_ref, g_ref, u_ref):
        x = h_ref[...]
        u_ref[...] = ((x * _rstd(x, d)) * g_ref[...]).astype(BF16)

    return pl.pallas_call(
        body, name=name, grid=(s // tm,),
        in_specs=[_rows(tm, d), _whole((1, d))], out_specs=_rows(tm, d),
        out_shape=jax.ShapeDtypeStruct((s, d), BF16), compiler_params=_cp(("parallel",)))(h, g)


def _ffn_in_call(u, w, name):
    s, d = u.shape
    n = w.shape[1] // 2
    tn = n // 2
    tm = min(TM, s)
    nj = n // tn

    def body(u_ref, wa_ref, wb_ref, a_ref, b_ref, hm_ref):
        uu = u_ref[...]
        a = _dot(uu, wa_ref[...])
        b = _dot(uu, wb_ref[...])
        a_ref[...] = a
        b_ref[...] = b
        hm_ref[...] = ((a * _sigmoid(a)) * b).astype(BF16)

    blk = pl.BlockSpec((tm, tn), lambda j, i: (i, j))
    return pl.pallas_call(
        body, name=name, grid=(nj, s // tm),
        in_specs=[pl.BlockSpec((tm, d), lambda j, i: (i, 0)),
                  pl.BlockSpec((d, tn), lambda j, i: (0, j)),
                  pl.BlockSpec((d, tn), lambda j, i: (0, j + nj))],
        out_specs=[blk, blk, blk],
        out_shape=[jax.ShapeDtypeStruct((s, n), F32), jax.ShapeDtypeStruct((s, n), F32),
                   jax.ShapeDtypeStruct((s, n), BF16)],
        compiler_params=_cp(("parallel", "parallel")))(u, w, w)


def _ffn_out_call(hm, w, h, name):
    s, n = hm.shape
    d = w.shape[1]
    tm = min(TM, s)

    def body(hm_ref, w_ref, h_ref, o_ref):
        o_ref[...] = h_ref[...] + 0.5 * _dot(hm_ref[...], w_ref[...])

    return pl.pallas_call(
        body, name=name, grid=(s // tm,),
        in_specs=[_rows(tm, n), _whole((n, d)), _rows(tm, d)], out_specs=_rows(tm, d),
        out_shape=jax.ShapeDtypeStruct((s, d), F32), compiler_params=_cp(("parallel",)))(hm, w, h)


def _mix_in_call(u, w, name):
    s, d = u.shape
    tm = min(TM_SMALL, s)
    segs = [(SEG_CQ, F32), (SEG_CKV, F32), (SEG_KROPE, F32), (SEG_SBQ, BF16), (SEG_SBK, BF16),
            (SEG_SBV, BF16), (SEG_GATES, F32)]

    def body(u_ref, w_ref, *outs):
        uu = u_ref[...]
        for ((off, width), _), o_ref in zip(segs, outs):
            o_ref[...] = _dot(uu, w_ref[:, off:off + width]).astype(o_ref.dtype)

    return pl.pallas_call(
        body, name=name, grid=(s // tm,),
        in_specs=[_rows(tm, d), _whole((d, IN_COLS_PAD))],
        out_specs=[_rows(tm, width) for (_, width), _ in segs],
        out_shape=[jax.ShapeDtypeStruct((s, width), dt) for (_, width), dt in segs],
        compiler_params=_cp(("parallel",)))(u, w)


def _lane(shape):
    return lax.broadcasted_iota(jnp.int32, shape, len(shape) - 1)


def _rot_half(y):
    lane = _lane(y.shape)
    swapped = jnp.where(lane < MLA_NOPE + MLA_ROPE // 2, pltpu.roll(y, HEAD_PAD - 16, 1), pltpu.roll(y, 16, 1))
    return jnp.where((lane >= MLA_NOPE) & (lane < MLA_QK), swapped, 0.0)


def _rope_tables(pos_ref, freq_ref, sign_ref):
    ang = pos_ref[...].astype(F32) * freq_ref[...]
    return jnp.cos(ang), jnp.sin(ang) * sign_ref[...]


def _head_fwd(x, g, cosv, ssv):
    r = _rstd(x, MLA_QK)
    y = (x * r) * g
    return y * cosv + _rot_half(y) * ssv, r


def _head_bwd(x, r, g, cosv, ssv, dout):
    dy = dout * cosv + _rot_half(dout * ssv)
    return _rms_bwd(x, r, g, dy, MLA_QK), jnp.sum(dy * (x * r), axis=0, keepdims=True)


def _mla_prep_call(cq, ckv, krope, pos, freq, sign, g_ql, g_kvl, g_qh, g_kh, wq, wkv, name):
    s = cq.shape[0]
    tm = min(TM_SMALL, s)
    width = HEADS * HEAD_PAD

    def body(cq_ref, ckv_ref, kr_ref, pos_ref, freq_ref, sign_ref, gql_ref, gkvl_ref, gqh_ref, gkh_ref,
             wq_ref, wkv_ref, q_ref, k_ref, v_ref):
        cosv, ssv = _rope_tables(pos_ref, freq_ref, sign_ref)
        x = cq_ref[...]
        qr = _dot(((x * _rstd(x, Q_LORA)) * gql_ref[...]).astype(BF16), wq_ref[...])
        x = ckv_ref[...]
        kv = _dot(((x * _rstd(x, KV_LORA)) * gkvl_ref[...]).astype(BF16), wkv_ref[...])
        kr = kr_ref[...]
        lane = _lane((tm, HEAD_PAD))
        for h in range(HEADS):
            sl = slice(h * HEAD_PAD, (h + 1) * HEAD_PAD)
            qh, _ = _head_fwd(qr[:, sl], gqh_ref[...], cosv, ssv)
            q_ref[:, sl] = qh.astype(BF16)
            kvh = kv[:, sl]
            kh, _ = _head_fwd(jnp.where(lane < MLA_NOPE, kvh, kr), gkh_ref[...], cosv, ssv)
            k_ref[:, sl] = kh.astype(BF16)
            v_ref[:, sl] = jnp.where(lane >= MLA_NOPE, kvh, 0.0).astype(BF16)

    out = jax.ShapeDtypeStruct((s, width), BF16)
    return pl.pallas_call(
        body, name=name, grid=(s // tm,),
        in_specs=[_rows(tm, Q_LORA), _rows(tm, KV_LORA), _rows(tm, HEAD_PAD), _rows(tm, 1),
                  _whole((1, HEAD_PAD)), _whole((1, HEAD_PAD)), _whole((1, Q_LORA)), _whole((1, KV_LORA)),
                  _whole((1, HEAD_PAD)), _whole((1, HEAD_PAD)), _whole((Q_LORA, width)), _whole((KV_LORA, width))],
        out_specs=[_rows(tm, width)] * 3, out_shape=[out, out, out],
        compiler_params=_cp(("parallel",)))(cq, ckv, krope, pos, freq, sign, g_ql, g_kvl, g_qh, g_kh, wq, wkv)


def _attn_specs(s, nb):
    qspec = pl.BlockSpec((TQ, nb * HEAD_PAD), lambda g, i: (i, g))
    kspec = pl.BlockSpec((s, nb * HEAD_PAD), lambda g, i: (0, g))
    return qspec, kspec


def _lanes(b):
    return slice(b * HEAD_PAD, (b + 1) * HEAD_PAD)


def _tri(cmp):
    r = lax.broadcasted_iota(jnp.int32, (TQ, TQ), 0)
    c = lax.broadcasted_iota(jnp.int32, (TQ, TQ), 1)
    return cmp(r, c)


def _mla_fwd_call(q, k, v, name):
    s, width = q.shape
    scale = 1.0 / math.sqrt(MLA_QK)

    nb = MLA_FWD_BLOCKS

    def body(q_ref, k_ref, v_ref, o_ref, lse_ref):
        qi = pl.program_id(1)
        qs = [q_ref[:, _lanes(b)] for b in range(nb)]
        causal = _tri(lambda r, c: c <= r)

        def step(kb, carry, diag):
            ks = pl.multiple_of(kb * TQ, TQ)
            heads = range(nb)
            scs = [_dot_nt(qs[b], k_ref[pl.ds(ks, TQ), _lanes(b)]) * scale for b in heads]
            if diag:
                scs = [jnp.where(causal, sc, -1e30) for sc in scs]
            mns = [jnp.maximum(carry[b][0], jnp.max(scs[b], axis=-1, keepdims=True)) for b in heads]
            als = [jnp.exp(carry[b][0] - mns[b]) for b in heads]
            ps = [jnp.exp(scs[b] - mns[b]) for b in heads]
            ls = [als[b] * carry[b][1] + jnp.sum(ps[b], axis=-1, keepdims=True) for b in heads]
            accs = [als[b] * carry[b][2] + _dot(ps[b].astype(BF16), v_ref[pl.ds(ks, TQ), _lanes(b)]) for b in heads]
            return tuple((mns[b], ls[b], accs[b]) for b in heads)

        init = tuple((jnp.full((TQ, 1), -1e30, F32), jnp.zeros((TQ, 1), F32), jnp.zeros((TQ, HEAD_PAD), F32))
                     for _ in range(nb))
        carry = step(qi, init, True)
        carry = lax.fori_loop(0, qi, lambda kb, c: step(kb, c, False), carry)
        for b in range(nb):
            m, l, acc = carry[b]
            o_ref[:, _lanes(b)] = (acc / l).astype(BF16)
            lse_ref[:, _lanes(b)] = jnp.broadcast_to(m + jnp.log(l), (TQ, HEAD_PAD))

    qspec, kspec = _attn_specs(s, nb)
    return pl.pallas_call(
        body, name=name, grid=(width // (nb * HEAD_PAD), s // TQ),
        in_specs=[qspec, kspec, kspec], out_specs=[qspec, qspec],
        out_shape=[jax.ShapeDtypeStruct((s, width), BF16), jax.ShapeDtypeStruct((s, width), F32)],
        compiler_params=_cp(("parallel", "arbitrary")))(q, k, v)


def _mla_bwd_call(q, k, v, o, do, lse, name):
    s, width = q.shape
    scale = 1.0 / math.sqrt(MLA_QK)
    nb = MLA_BWD_BLOCKS

    def body(q_ref, k_ref, v_ref, o_ref, do_ref, lse_ref, dq_ref, dk_ref, dv_ref):
        qi = pl.program_id(1)

        @pl.when(qi == 0)
        def _():
            dk_ref[...] = jnp.zeros_like(dk_ref)
            dv_ref[...] = jnp.zeros_like(dv_ref)

        qs = [q_ref[:, _lanes(b)] for b in range(nb)]
        dos = [do_ref[:, _lanes(b)] for b in range(nb)]
        lses = [lse_ref[:, b * HEAD_PAD:b * HEAD_PAD + 1] for b in range(nb)]
        dlts = [jnp.sum(dos[b].astype(F32) * o_ref[:, _lanes(b)].astype(F32), axis=-1, keepdims=True) for b in range(nb)]
        causal = _tri(lambda r, c: c <= r)

        def step(kb, dqs, diag):
            ks = pl.multiple_of(kb * TQ, TQ)
            heads = range(nb)
            kts = [k_ref[pl.ds(ks, TQ), _lanes(b)] for b in heads]
            scs = [_dot_nt(qs[b], kts[b]) for b in heads]
            dps = [_dot_nt(dos[b], v_ref[pl.ds(ks, TQ), _lanes(b)]) for b in heads]
            ps = [jnp.exp(scs[b] * scale - lses[b]) for b in heads]
            if diag:
                ps = [jnp.where(causal, p, 0.0) for p in ps]
            dss = [(ps[b] * (dps[b] - dlts[b]) * scale).astype(BF16) for b in heads]
            dvs = [_dot_tn(ps[b].astype(BF16), dos[b]) for b in heads]
            dks = [_dot_tn(dss[b], qs[b]) for b in heads]
            out = tuple(dqs[b] + _dot(dss[b], kts[b]) for b in heads)
            for b in heads:
                dv_ref[pl.ds(ks, TQ), _lanes(b)] += dvs[b]
                dk_ref[pl.ds(ks, TQ), _lanes(b)] += dks[b]
            return out

        dqs = step(qi, tuple(jnp.zeros((TQ, HEAD_PAD), F32) for _ in range(nb)), True)
        dqs = lax.fori_loop(0, qi, lambda kb, c: step(kb, c, False), dqs)
        for b in range(nb):
            dq_ref[:, _lanes(b)] = dqs[b]

    qspec, kspec = _attn_specs(s, nb)
    out = jax.ShapeDtypeStruct((s, width), F32)
    return pl.pallas_call(
        body, name=name, grid=(width // (nb * HEAD_PAD), s // TQ),
        in_specs=[qspec, kspec, kspec, qspec, qspec, qspec], out_specs=[qspec, kspec, kspec],
        out_shape=[out, out, out],
        compiler_params=_cp(("parallel", "arbitrary")))(q, k, v, o, do, lse)


def _dot_hilo(x, u):
    hi = x.astype(BF16)
    lo = (x - hi.astype(F32)).astype(BF16)
    return _dot(hi, u) + _dot(lo, u)


def _sb_logs(z):
    sp = jnp.log(1.0 + jnp.exp(-jnp.abs(z)))
    return jnp.minimum(z, 0.0) - sp, jnp.minimum(-z, 0.0) - sp


def _sb_head_q(qb, first, hh):
    keep = first if hh == 0 else jnp.logical_not(first)
    return jnp.where(keep, qb, jnp.zeros_like(qb)) * jnp.asarray(SB_SCALE, qb.dtype)


def _sb_fwd_call(q, k, v, name):
    s, width = q.shape
    nb = SB_FWD_BLOCKS
    chains = [(b, hh) for b in range(nb) for hh in range(HEAD_PAD // SB_HEAD)]

    def body(q_ref, k_ref, v_ref, o_ref, t_ref):
        qi = pl.program_id(1)
        strict = _tri(lambda r, c: c < r)
        after = _tri(lambda r, c: r > c).astype(BF16)
        first = _lane((1, HEAD_PAD)) < SB_HEAD
        qhs = [_sb_head_q(q_ref[:, _lanes(b)], first, hh) for b, hh in chains]

        def step(kb, carry, diag):
            ks = pl.multiple_of(kb * TQ, TQ)
            ids = range(len(chains))
            zs = [_dot_nt(qhs[ci], k_ref[pl.ds(ks, TQ), _lanes(chains[ci][0])]) for ci in ids]
            logs = [_sb_logs(z) for z in zs]
            lss = [lg[0] for lg in logs]
            l1ms = [jnp.where(strict, lg[1], 0.0) if diag else lg[1] for lg in logs]
            sufs = [_dot_hilo(l1m, after) for l1m in l1ms]
            as_ = [jnp.exp(lss[ci] + sufs[ci] + carry[ci][0]) for ci in ids]
            if diag:
                as_ = [jnp.where(strict, a, 0.0) for a in as_]
            accs = [carry[ci][1] + _dot(as_[ci].astype(BF16), v_ref[pl.ds(ks, TQ), _lanes(chains[ci][0])]) for ci in ids]
            return tuple((carry[ci][0] + jnp.sum(l1ms[ci], axis=-1, keepdims=True), accs[ci]) for ci in ids)

        init = tuple((jnp.zeros((TQ, 1), F32), jnp.zeros((TQ, HEAD_PAD), F32)) for _ in chains)
        carry = step(qi, init, True)
        carry = lax.fori_loop(0, qi, lambda j, c: step(qi - 1 - j, c, False), carry)
        for b in range(nb):
            (cs0, acc0), (cs1, acc1) = carry[2 * b], carry[2 * b + 1]
            o_ref[:, _lanes(b)] = jnp.where(first, acc0, acc1).astype(BF16)
            t_ref[:, _lanes(b)] = jnp.where(first, cs0, cs1)

    qspec, kspec = _attn_specs(s, nb)
    return pl.pallas_call(
        body, name=name, grid=(width // (nb * HEAD_PAD), s // TQ),
        in_specs=[qspec, kspec, kspec], out_specs=[qspec, qspec],
        out_shape=[jax.ShapeDtypeStruct((s, width), BF16), jax.ShapeDtypeStruct((s, width), F32)],
        compiler_params=_cp(("parallel", "arbitrary")))(q, k, v)


def _sb_bwd_call(q, k, v, do, tot, name):
    s, width = q.shape
    nb = SB_BWD_BLOCKS
    chains = [(b, hh) for b in range(nb) for hh in range(HEAD_PAD // SB_HEAD)]

    def body(q_ref, k_ref, v_ref, do_ref, t_ref, dq_ref, dk_ref, dv_ref):
        qi = pl.program_id(1)

        @pl.when(qi == 0)
        def _():
            dk_ref[...] = jnp.zeros_like(dk_ref)
            dv_ref[...] = jnp.zeros_like(dv_ref)

        strict = _tri(lambda r, c: c < r)
        upto = _tri(lambda r, c: r <= c).astype(BF16)
        before = _tri(lambda r, c: r < c).astype(BF16)
        first = _lane((1, HEAD_PAD)) < SB_HEAD
        qhs = [_sb_head_q(q_ref[:, _lanes(b)], first, hh) for b, hh in chains]
        dohs = []
        for b, hh in chains:
            dob = do_ref[:, _lanes(b)]
            dohs.append(jnp.where(first if hh == 0 else jnp.logical_not(first), dob, jnp.zeros_like(dob)))
        tts = [t_ref[:, b * HEAD_PAD + hh * SB_HEAD:b * HEAD_PAD + hh * SB_HEAD + 1] for b, hh in chains]

        def step(kb, carry, diag):
            ks = pl.multiple_of(kb * TQ, TQ)
            ids = range(len(chains))
            kts = [k_ref[pl.ds(ks, TQ), _lanes(b)] for b, _ in chains]
            zs = [_dot_nt(qhs[ci], kts[ci]) for ci in ids]
            das = [_dot_nt(dohs[ci], v_ref[pl.ds(ks, TQ), _lanes(chains[ci][0])]) for ci in ids]
            logs = [_sb_logs(z) for z in zs]
            lss = [lg[0] for lg in logs]
            l1ms = [jnp.where(strict, lg[1], 0.0) if diag else lg[1] for lg in logs]
            pins = [_dot_hilo(l1m, upto) for l1m in l1ms]
            as_ = [jnp.exp(lss[ci] + (tts[ci] - carry[ci][0] - pins[ci])) for ci in ids]
            if diag:
                as_ = [jnp.where(strict, a, 0.0) for a in as_]
            gs = [as_[ci] * das[ci] for ci in ids]
            cexs = [carry[ci][1] + _dot_hilo(gs[ci], before) for ci in ids]
            dzs = [gs[ci] - jnp.exp(lss[ci]) * (gs[ci] + cexs[ci]) for ci in ids]
            if diag:
                dzs = [jnp.where(strict, dz, 0.0) for dz in dzs]
            dzbs = [dz.astype(BF16) for dz in dzs]
            dvps = [_dot_tn(as_[ci].astype(BF16), dohs[ci]) for ci in ids]
            dkps = [_dot_tn(dzbs[ci], qhs[ci]) for ci in ids]
            out = tuple((carry[ci][0] + jnp.sum(l1ms[ci], axis=-1, keepdims=True),
                         carry[ci][1] + jnp.sum(gs[ci], axis=-1, keepdims=True),
                         carry[ci][2] + _dot(dzbs[ci], kts[ci])) for ci in ids)
            for b in range(nb):
                dk_ref[pl.ds(ks, TQ), _lanes(b)] += dkps[2 * b] + dkps[2 * b + 1]
                dv_ref[pl.ds(ks, TQ), _lanes(b)] += dvps[2 * b] + dvps[2 * b + 1]
            return out

        init = tuple((jnp.zeros((TQ, 1), F32), jnp.zeros((TQ, 1), F32), jnp.zeros((TQ, HEAD_PAD), F32)) for _ in chains)
        carry = lax.fori_loop(0, qi, lambda kb, c: step(kb, c, False), init)
        carry = step(qi, carry, True)
        for b in range(nb):
            dq_ref[:, _lanes(b)] = (jnp.where(first, carry[2 * b][2], carry[2 * b + 1][2]) * SB_SCALE).astype(BF16)

    qspec, kspec = _attn_specs(s, nb)
    return pl.pallas_call(
        body, name=name, grid=(width // (nb * HEAD_PAD), s // TQ),
        in_specs=[qspec, kspec, kspec, qspec, qspec], out_specs=[qspec, kspec, kspec],
        out_shape=[jax.ShapeDtypeStruct((s, width), BF16), jax.ShapeDtypeStruct((s, width), F32),
                   jax.ShapeDtypeStruct((s, width), F32)],
        compiler_params=_cp(("parallel", "arbitrary")))(q, k, v, do, tot)


def _merge_out_call(om, osb, gates, h, wbm, wbs, wo, name):
    s, d = h.shape
    tm = min(TM_SMALL, s)

    def body(om_ref, os_ref, g_ref, h_ref, wbm_ref, wbs_ref, wo_ref, h2_ref, bm_ref, bs_ref, mg_ref):
        bm = _dot(om_ref[...], wbm_ref[...])
        bs = _dot(os_ref[...], wbs_ref[...])
        mg = (_sigmoid(g_ref[:, :d]) * bm + _sigmoid(g_ref[:, d:]) * bs).astype(BF16)
        bm_ref[...] = bm
        bs_ref[...] = bs
        mg_ref[...] = mg
        h2_ref[...] = h_ref[...] + _dot(mg, wo_ref[...])

    return pl.pallas_call(
        body, name=name, grid=(s // tm,),
        in_specs=[_rows(tm, om.shape[1]), _rows(tm, SB_WIDTH), _rows(tm, 2 * d), _rows(tm, d),
                  _whole(wbm.shape), _whole(wbs.shape), _whole(wo.shape)],
        out_specs=[_rows(tm, d)] * 4,
        out_shape=[jax.ShapeDtypeStruct((s, d), F32), jax.ShapeDtypeStruct((s, d), F32),
                   jax.ShapeDtypeStruct((s, d), F32), jax.ShapeDtypeStruct((s, d), BF16)],
        compiler_params=_cp(("parallel",)))(om, osb, gates, h, wbm, wbs, wo)


def _ple_call(h, g, wg, p, wp, tgt, name):
    s, d = h.shape
    tm = min(TM_SMALL, s)

    def body(h_ref, g_ref, wg_ref, p_ref, wp_ref, t_ref, dh_ref, dhs_ref, un_ref, dgl_ref, dpp_ref, dg_ref, sq_ref):
        @pl.when(pl.program_id(0) == 0)
        def _():
            dg_ref[...] = jnp.zeros_like(dg_ref)
            sq_ref[...] = jnp.zeros_like(sq_ref)

        x = h_ref[...]
        gain = g_ref[...]
        r = _rstd(x, d)
        xh = x * r
        un = (xh * gain).astype(BF16)
        sg = _sigmoid(_dot(un, wg_ref[...]))
        pp = _dot(p_ref[...].astype(BF16), wp_ref[...])
        diff = (x + sg * pp) - t_ref[...]
        sq_ref[...] += jnp.sum(diff * diff, axis=0, keepdims=True)
        dy = diff * (1.0 / d)
        dgl = ((dy * pp) * (sg * (1.0 - sg))).astype(BF16)
        dun = _dot_nt(dgl, wg_ref[...])
        dg_ref[...] += jnp.sum(dun * xh, axis=0, keepdims=True)
        dh = dy + _rms_bwd(x, r, gain, dun, d)
        dh_ref[...] = dh
        dhs_ref[...] = (0.5 * dh).astype(BF16)
        un_ref[...] = un
        dgl_ref[...] = dgl
        dpp_ref[...] = (dy * sg).astype(BF16)

    bf = jax.ShapeDtypeStruct((s, d), BF16)
    vec = jax.ShapeDtypeStruct((1, d), F32)
    return pl.pallas_call(
        body, name=name, grid=(s // tm,),
        in_specs=[_rows(tm, d), _whole((1, d)), _whole(wg.shape), _rows(tm, PLE_DIM), _whole(wp.shape), _rows(tm, d)],
        out_specs=[_rows(tm, d)] * 5 + [_whole((1, d))] * 2,
        out_shape=[jax.ShapeDtypeStruct((s, d), F32), bf, bf, bf, bf, vec, vec],
        compiler_params=_cp(("arbitrary",)))(h, g, wg, p, wp, tgt)


def _ffn_bwd_a_call(dhs, a, b, wo, name):
    s, n = a.shape
    d = dhs.shape[1]
    tn = n // 2
    tm = min(TM, s)

    def body(dh_ref, a_ref, b_ref, wo_ref, da_ref, db_ref):
        dhm = _dot_nt(dh_ref[...], wo_ref[...])
        av = a_ref[...]
        sa = _sigmoid(av)
        da_ref[...] = (dhm * b_ref[...] * (sa * (1.0 + av * (1.0 - sa)))).astype(BF16)
        db_ref[...] = (dhm * (av * sa)).astype(BF16)

    blk = pl.BlockSpec((tm, tn), lambda j, i: (i, j))
    return pl.pallas_call(
        body, name=name, grid=(n // tn, s // tm),
        in_specs=[pl.BlockSpec((tm, d), lambda j, i: (i, 0)), blk, blk, pl.BlockSpec((tn, d), lambda j, i: (j, 0))],
        out_specs=[blk, blk],
        out_shape=[jax.ShapeDtypeStruct((s, n), BF16)] * 2,
        compiler_params=_cp(("parallel", "parallel")))(dhs, a, b, wo)


def _norm_bwd_call(dy_list, w_list, h, g, dh_in, name, half_out):
    s, d = h.shape
    tm = min(TM_SMALL, s)
    nk = len(dy_list)
    factor = 0.5 if half_out else 1.0

    def body(*refs):
        dy_refs = refs[:nk]
        w_refs = refs[nk:2 * nk]
        h_ref, g_ref, dhin_ref, dh_ref, dhb_ref, dg_ref = refs[2 * nk:]

        @pl.when(pl.program_id(0) == 0)
        def _():
            dg_ref[...] = jnp.zeros_like(dg_ref)

        du = _dot_nt(dy_refs[0][...], w_refs[0][...])
        for dy_ref, w_ref in zip(dy_refs[1:], w_refs[1:]):
            du = du + _dot_nt(dy_ref[...], w_ref[...])
        x = h_ref[...]
        r = _rstd(x, d)
        dg_ref[...] += jnp.sum(du * (x * r), axis=0, keepdims=True)
        dh = dhin_ref[...] + _rms_bwd(x, r, g_ref[...], du, d)
        dh_ref[...] = dh
        dhb_ref[...] = (factor * dh).astype(BF16)

    return pl.pallas_call(
        body, name=name, grid=(s // tm,),
        in_specs=[_rows(tm, dy.shape[1]) for dy in dy_list] + [_whole(w.shape) for w in w_list]
        + [_rows(tm, d), _whole((1, d)), _rows(tm, d)],
        out_specs=[_rows(tm, d), _rows(tm, d), _whole((1, d))],
        out_shape=[jax.ShapeDtypeStruct((s, d), F32), jax.ShapeDtypeStruct((s, d), BF16),
                   jax.ShapeDtypeStruct((1, d), F32)],
        compiler_params=_cp(("arbitrary",)))(*dy_list, *w_list, h, g, dh_in)


def _merge_bwd_call(dhb, gates, bm, bs, wo, wbm, wbs, name):
    s, d = bm.shape
    tm = min(TM_SMALL, s)

    def body(dh_ref, g_ref, bm_ref, bs_ref, wo_ref, wbm_ref, wbs_ref, dg_ref, dbm_ref, dbs_ref, dom_ref, dos_ref):
        dmg = _dot_nt(dh_ref[...], wo_ref[...])
        s1 = _sigmoid(g_ref[:, :d])
        s2 = _sigmoid(g_ref[:, d:])
        dg_ref[:, :d] = (dmg * bm_ref[...] * (s1 * (1.0 - s1))).astype(BF16)
        dg_ref[:, d:] = (dmg * bs_ref[...] * (s2 * (1.0 - s2))).astype(BF16)
        dbm = (dmg * s1).astype(BF16)
        dbs = (dmg * s2).astype(BF16)
        dbm_ref[...] = dbm
        dbs_ref[...] = dbs
        dom_ref[...] = _dot_nt(dbm, wbm_ref[...]).astype(BF16)
        dos_ref[...] = _dot_nt(dbs, wbs_ref[...]).astype(BF16)

    wm = wbm.shape[0]
    return pl.pallas_call(
        body, name=name, grid=(s // tm,),
        in_specs=[_rows(tm, d), _rows(tm, 2 * d), _rows(tm, d), _rows(tm, d),
                  _whole(wo.shape), _whole(wbm.shape), _whole(wbs.shape)],
        out_specs=[_rows(tm, 2 * d), _rows(tm, d), _rows(tm, d), _rows(tm, wm), _rows(tm, SB_WIDTH)],
        out_shape=[jax.ShapeDtypeStruct((s, 2 * d), BF16), jax.ShapeDtypeStruct((s, d), BF16),
                   jax.ShapeDtypeStruct((s, d), BF16), jax.ShapeDtypeStruct((s, wm), BF16),
                   jax.ShapeDtypeStruct((s, SB_WIDTH), BF16)],
        compiler_params=_cp(("parallel",)))(dhb, gates, bm, bs, wo, wbm, wbs)


def _mla_prep_bwd_call(cq, ckv, krope, pos, freq, sign, g_ql, g_kvl, g_qh, g_kh, wq, wkv, dq, dk, dv, name):
    s = cq.shape[0]
    tm = min(TM_SMALL, s)
    width = HEADS * HEAD_PAD

    def body(cq_ref, ckv_ref, kr_ref, pos_ref, freq_ref, sign_ref, gql_ref, gkvl_ref, gqh_ref, gkh_ref,
             wq_ref, wkv_ref, dq_ref, dk_ref, dv_ref,
             dcq_ref, dckv_ref, dkr_ref, dwq_ref, dwkv_ref, dgql_ref, dgkvl_ref, dgqh_ref, dgkh_ref, dqr_ref, dkv_ref):
        @pl.when(pl.program_id(0) == 0)
        def _():
            for ref in (dwq_ref, dwkv_ref, dgql_ref, dgkvl_ref, dgqh_ref, dgkh_ref):
                ref[...] = jnp.zeros_like(ref)

        cosv, ssv = _rope_tables(pos_ref, freq_ref, sign_ref)
        xq = cq_ref[...]
        rq = _rstd(xq, Q_LORA)
        cqn = ((xq * rq) * gql_ref[...]).astype(BF16)
        qr = _dot(cqn, wq_ref[...])
        xk = ckv_ref[...]
        rk = _rstd(xk, KV_LORA)
        ckvn = ((xk * rk) * gkvl_ref[...]).astype(BF16)
        kv = _dot(ckvn, wkv_ref[...])
        kr = kr_ref[...]
        lane = _lane((tm, HEAD_PAD))
        dkr = jnp.zeros((tm, HEAD_PAD), F32)
        dgqh = jnp.zeros((1, HEAD_PAD), F32)
        dgkh = jnp.zeros((1, HEAD_PAD), F32)
        for h in range(HEADS):
            sl = slice(h * HEAD_PAD, (h + 1) * HEAD_PAD)
            x = qr[:, sl]
            dx, dgh = _head_bwd(x, _rstd(x, MLA_QK), gqh_ref[...], cosv, ssv, dq_ref[:, sl])
            dqr_ref[:, sl] = dx.astype(BF16)
            dgqh = dgqh + dgh
            x = jnp.where(lane < MLA_NOPE, kv[:, sl], kr)
            dx, dgh = _head_bwd(x, _rstd(x, MLA_QK), gkh_ref[...], cosv, ssv, dk_ref[:, sl])
            dgkh = dgkh + dgh
            dkr = dkr + jnp.where(lane >= MLA_NOPE, dx, 0.0)
            dkv_ref[:, sl] = jnp.where(lane < MLA_NOPE, dx, dv_ref[:, sl]).astype(BF16)
        dgqh_ref[...] += dgqh
        dgkh_ref[...] += dgkh
        dkr_ref[...] = dkr.astype(BF16)
        dqr = dqr_ref[...]
        dkvb = dkv_ref[...]
        dwq_ref[...] += _dot_tn(cqn, dqr)
        dwkv_ref[...] += _dot_tn(ckvn, dkvb)
        dcqn = _dot_nt(dqr, wq_ref[...])
        dgql_ref[...] += jnp.sum(dcqn * (xq * rq), axis=0, keepdims=True)
        dcq_ref[...] = _rms_bwd(xq, rq, gql_ref[...], dcqn, Q_LORA).astype(BF16)
        dckvn = _dot_nt(dkvb, wkv_ref[...])
        dgkvl_ref[...] += jnp.sum(dckvn * (xk * rk), axis=0, keepdims=True)
        dckv_ref[...] = _rms_bwd(xk, rk, gkvl_ref[...], dckvn, KV_LORA).astype(BF16)

    vec = lambda n: jax.ShapeDtypeStruct((1, n), F32)
    outs = pl.pallas_call(
        body, name=name, grid=(s // tm,),
        in_specs=[_rows(tm, Q_LORA), _rows(tm, KV_LORA), _rows(tm, HEAD_PAD), _rows(tm, 1),
                  _whole((1, HEAD_PAD)), _whole((1, HEAD_PAD)), _whole((1, Q_LORA)), _whole((1, KV_LORA)),
                  _whole((1, HEAD_PAD)), _whole((1, HEAD_PAD)), _whole((Q_LORA, width)), _whole((KV_LORA, width)),
                  _rows(tm, width), _rows(tm, width), _rows(tm, width)],
        out_specs=[_rows(tm, Q_LORA), _rows(tm, KV_LORA), _rows(tm, HEAD_PAD), _whole((Q_LORA, width)),
                   _whole((KV_LORA, width)), _whole((1, Q_LORA)), _whole((1, KV_LORA)), _whole((1, HEAD_PAD)),
                   _whole((1, HEAD_PAD)), _rows(tm, width), _rows(tm, width)],
        out_shape=[jax.ShapeDtypeStruct((s, Q_LORA), BF16), jax.ShapeDtypeStruct((s, KV_LORA), BF16),
                   jax.ShapeDtypeStruct((s, HEAD_PAD), BF16), jax.ShapeDtypeStruct((Q_LORA, width), F32),
                   jax.ShapeDtypeStruct((KV_LORA, width), F32), vec(Q_LORA), vec(KV_LORA), vec(HEAD_PAD), vec(HEAD_PAD),
                   jax.ShapeDtypeStruct((s, width), BF16), jax.ShapeDtypeStruct((s, width), BF16)],
        compiler_params=_cp(("arbitrary",)))(cq, ckv, krope, pos, freq, sign, g_ql, g_kvl, g_qh, g_kh, wq, wkv, dq, dk, dv)
    return outs[:9]


def _tn_call(a, b, name):
    s, ka = a.shape
    nb = b.shape[1]
    ti = _pick(ka, (512, 256, 128))
    tj = nb if nb <= TN_MAX_COLS else _pick(nb, (2176, 1024, 512, 256, 128))
    ts = min(1024 if tj <= 1024 else 512, s)
    ns = s // ts

    def body(a_ref, b_ref, o_ref):
        part = _dot_tn(a_ref[...].astype(BF16), b_ref[...].astype(BF16))

        @pl.when(pl.program_id(2) == 0)
        def _():
            o_ref[...] = part

        @pl.when(pl.program_id(2) != 0)
        def _():
            o_ref[...] += part

    return pl.pallas_call(
        body, name=name, grid=(ka // ti, nb // tj, ns),
        in_specs=[pl.BlockSpec((ts, ti), lambda i, j, t: (t, i)), pl.BlockSpec((ts, tj), lambda i, j, t: (t, j))],
        out_specs=pl.BlockSpec((ti, tj), lambda i, j, t: (i, j)),
        out_shape=jax.ShapeDtypeStruct((ka, nb), F32),
        compiler_params=_cp(("parallel", "parallel", "arbitrary")))(a, b)


def _sum_call(parts, out_dtype, name):
    n, r, w = parts.shape
    tr = _pick(r, (FLAT_TILE, 8))

    def body(p_ref, o_ref):
        acc = p_ref[0].astype(F32)
        for k in range(1, n):
            acc = acc + p_ref[k].astype(F32)
        o_ref[...] = acc.astype(out_dtype)

    return pl.pallas_call(
        body, name=name, grid=(r // tr,),
        in_specs=[pl.BlockSpec((n, tr, w), lambda i: (0, i, 0))], out_specs=_rows(tr, w),
        out_shape=jax.ShapeDtypeStruct((r, w), out_dtype), compiler_params=_cp(("parallel",)))(parts)


def _pair_sum_call(full, other, core, out_dtype, name):
    n, r, w = other.shape
    tr = FLAT_TILE
    nblk = r // tr

    def body(c_ref, a_ref, b_ref, o_ref):
        o_ref[...] = (a_ref[...] + b_ref[...]).astype(out_dtype)

    spec = pl.BlockSpec((None, tr, w), lambda k, i, c_ref: (k, i, 0))
    return pl.pallas_call(
        body, name=name,
        grid_spec=pltpu.PrefetchScalarGridSpec(
            num_scalar_prefetch=1, grid=(n, nblk),
            in_specs=[pl.BlockSpec((None, tr, w), lambda k, i, c_ref: (k, c_ref[0] * nblk + i, 0)), spec],
            out_specs=spec),
        out_shape=jax.ShapeDtypeStruct((n, r, w), out_dtype),
        compiler_params=_cp(("parallel", "parallel")))(core.reshape(1).astype(jnp.int32), full, other)


def _adamw_call(w, g, m, v, name):
    r, c = w.shape
    tr = _pick(r, (256, 128, 64, 32, 16, 8))

    def body(w_ref, g_ref, m_ref, v_ref, d_ref, nm_ref, nv_ref):
        gg = g_ref[...]
        nm = ADAM_B1 * m_ref[...] + (1.0 - ADAM_B1) * gg
        nv = ADAM_B2 * v_ref[...] + (1.0 - ADAM_B2) * (gg * gg)
        m_hat = nm / (1.0 - ADAM_B1 ** ADAM_STEP)
        v_hat = nv / (1.0 - ADAM_B2 ** ADAM_STEP)
        d_ref[...] = -ADAM_LR * (m_hat / (jnp.sqrt(v_hat) + ADAM_EPS) + ADAM_WD * w_ref[...])
        nm_ref[...] = nm
        nv_ref[...] = nv

    out = jax.ShapeDtypeStruct((r, c), F32)
    return pl.pallas_call(
        body, name=name, grid=(r // tr,), in_specs=[_rows(tr, c)] * 4, out_specs=[_rows(tr, c)] * 3,
        out_shape=[out, out, out], compiler_params=_cp(("parallel",)))(w, g, m, v)


HBM = pl.BlockSpec(memory_space=pl.ANY)


def _position():
    x, y, c = lax.axis_index("x"), lax.axis_index("y"), lax.axis_index("c")
    chips = [(1 - x, y), (x, 1 - y), (1 - x, 1 - y)]
    return x, y, c, chips


def _gather_call(flat, name):
    rows, w = flat.shape
    half = rows // 2

    def body(in_ref, out_ref, send_sems, recv_sems, fwd_send, fwd_recv):
        x, y, c, chips = _position()

        def part(chip, core):
            return out_ref.at[2 * chip[0] + chip[1], pl.ds(core * half, half), :]

        first = []
        for j, chip in enumerate(chips):
            cp = pltpu.make_async_remote_copy(
                src_ref=in_ref.at[pl.ds(c * half, half), :], dst_ref=part((x, y), c),
                send_sem=send_sems.at[j], recv_sem=recv_sems.at[j], device_id=(*chip, c), device_id_type=MESH)
            cp.start()
            first.append(cp)
        passed = []
        for j, chip in enumerate(chips):
            first[j].wait_recv()
            cp = pltpu.make_async_remote_copy(
                src_ref=part(chip, c), dst_ref=part(chip, c),
                send_sem=fwd_send.at[j], recv_sem=fwd_recv.at[j], device_id=(x, y, 1 - c), device_id_type=MESH)
            cp.start()
            passed.append(cp)
        for cp in passed:
            cp.wait_recv()
        for cp in first + passed:
            cp.wait_send()

    return pl.pallas_call(
        body, name=name, in_specs=[HBM], out_specs=HBM,
        out_shape=jax.ShapeDtypeStruct((N_CHIPS, rows, w), flat.dtype),
        scratch_shapes=[pltpu.SemaphoreType.DMA((3,)), pltpu.SemaphoreType.DMA((3,)), pltpu.SemaphoreType.DMA((3,)),
                        pltpu.SemaphoreType.DMA((3,))])(flat)


def _pair_send_call(full, name):
    n, rows, w = full.shape
    half = rows // 2

    def body(in_ref, out_ref, send_sem, recv_sem):
        x, y, c, _ = _position()
        cp = pltpu.make_async_remote_copy(
            src_ref=in_ref.at[:, pl.ds((1 - c) * half, half), :], dst_ref=out_ref,
            send_sem=send_sem, recv_sem=recv_sem, device_id=(x, y, 1 - c), device_id_type=MESH)
        cp.start()
        cp.wait()

    return pl.pallas_call(
        body, name=name, in_specs=[HBM], out_specs=HBM,
        out_shape=jax.ShapeDtypeStruct((n, half, w), full.dtype),
        scratch_shapes=[pltpu.SemaphoreType.DMA, pltpu.SemaphoreType.DMA])(full)


def _chip_scatter_call(parts, name):
    n, r, w = parts.shape

    def body(in_ref, out_ref, send_sems, recv_sems):
        x, y, c, chips = _position()
        me = 2 * x + y
        copies = []
        for j, chip in enumerate(chips):
            cp = pltpu.make_async_remote_copy(
                src_ref=in_ref.at[2 * chip[0] + chip[1]], dst_ref=out_ref.at[me],
                send_sem=send_sems.at[j], recv_sem=recv_sems.at[j], device_id=(*chip, c), device_id_type=MESH)
            cp.start()
            copies.append(cp)
        for cp in copies:
            cp.wait()

    return pl.pallas_call(
        body, name=name, in_specs=[HBM], out_specs=HBM,
        out_shape=jax.ShapeDtypeStruct((n, r, w), parts.dtype),
        scratch_shapes=[pltpu.SemaphoreType.DMA((3,)), pltpu.SemaphoreType.DMA((3,))])(parts)


def _pair_swap_call(mine_half, name):
    r, w = mine_half.shape

    def body(in_ref, out_ref, send_sem, recv_sem):
        x, y, c, _ = _position()
        cp = pltpu.make_async_remote_copy(
            src_ref=in_ref, dst_ref=out_ref, send_sem=send_sem, recv_sem=recv_sem,
            device_id=(x, y, 1 - c), device_id_type=MESH)
        cp.start()
        cp.wait()

    return pl.pallas_call(
        body, name=name, in_specs=[HBM], out_specs=HBM,
        out_shape=jax.ShapeDtypeStruct((r, w), mine_half.dtype),
        scratch_shapes=[pltpu.SemaphoreType.DMA, pltpu.SemaphoreType.DMA])(mine_half)


def _all_gather_small_call(block, name):
    r, w = block.shape

    def body(in_ref, out_ref, send_sems, recv_sems, local_sem):
        x, y, c, _ = _position()
        me = 4 * x + 2 * y + c
        own = pltpu.make_async_copy(in_ref, out_ref.at[me], local_sem)
        own.start()
        copies = []
        for k in range(1, 8):
            peer = (x ^ (k >> 2), y ^ ((k >> 1) & 1), c ^ (k & 1))
            cp = pltpu.make_async_remote_copy(
                src_ref=in_ref, dst_ref=out_ref.at[me], send_sem=send_sems.at[k - 1], recv_sem=recv_sems.at[k - 1],
                device_id=peer, device_id_type=MESH)
            cp.start()
            copies.append(cp)
        for cp in copies:
            cp.wait()
        own.wait()

    return pl.pallas_call(
        body, name=name, in_specs=[HBM], out_specs=HBM,
        out_shape=jax.ShapeDtypeStruct((8, r, w), block.dtype),
        scratch_shapes=[pltpu.SemaphoreType.DMA((7,)), pltpu.SemaphoreType.DMA((7,)), pltpu.SemaphoreType.DMA])(block)


BIG = {
    "ffn1_w_in": ((D_MODEL, 2 * D_FF), 1), "ffn1_w_out": ((D_FF, D_MODEL), 0),
    "w_in": ((D_MODEL, 4256), 1), "w_q_up": ((Q_LORA, HEADS * MLA_QK), 1), "w_kv_up": ((KV_LORA, 1024), 1),
    "w_branch_mla": ((512, D_MODEL), 1), "w_branch_sb": ((SB_WIDTH, D_MODEL), 1), "w_out": ((D_MODEL, D_MODEL), 0),
    "ffn2_w_in": ((D_MODEL, 2 * D_FF), 1), "ffn2_w_out": ((D_FF, D_MODEL), 0),
    "w_ple_gate": ((D_MODEL, D_MODEL), 0), "w_ple_proj": ((PLE_DIM, D_MODEL), 1),
}
GAINS = {"ffn1_norm": 1024, "mix_norm": 1024, "q_latent_norm": 384, "kv_latent_norm": 256, "q_head_norm": 96,
         "k_head_norm": 96, "ffn2_norm": 1024, "ple_norm": 1024}
WEIGHT_ORDER = ["ffn1_norm", "ffn1_w_in", "ffn1_w_out", "mix_norm", "w_in", "q_latent_norm", "w_q_up",
                "kv_latent_norm", "w_kv_up", "q_head_norm", "k_head_norm", "w_branch_mla", "w_branch_sb", "w_out",
                "ffn2_norm", "ffn2_w_in", "ffn2_w_out", "ple_norm", "w_ple_gate", "w_ple_proj"]


def _shard_shape(name):
    (r, c), axis = BIG[name]
    return (r // N_CHIPS, c) if axis == 0 else (r, c // N_CHIPS)


def _flat_rows(name):
    r, c = _shard_shape(name)
    rows = r * c // FLAT_W
    return rows, -(-rows // ROW_ALIGN) * ROW_ALIGN


def _flatten_shards(shards):
    parts = []
    for name in BIG:
        a = shards[name]
        rows, padded = _flat_rows(name)
        a = a.reshape(a.shape[:-2] + (rows, FLAT_W))
        if padded != rows:
            a = jnp.pad(a, [(0, 0)] * (a.ndim - 2) + [(0, padded - rows), (0, 0)])
        parts.append(a)
    total = sum(_flat_rows(name)[1] for name in BIG)
    tail = -total % FLAT_ALIGN
    if tail:
        parts.append(jnp.zeros(parts[0].shape[:-2] + (tail, FLAT_W), parts[0].dtype))
    return jnp.concatenate(parts, axis=-2)


def _unflatten_shards(flat):
    out, at = {}, 0
    for name in BIG:
        rows, padded = _flat_rows(name)
        out[name] = flat[..., at:at + rows, :].reshape(flat.shape[:-2] + _shard_shape(name))
        at += padded
    return out


def _to_shards(name, full):
    (r, c), axis = BIG[name]
    if axis == 0:
        return full.reshape(N_CHIPS, r // N_CHIPS, c)
    return full.reshape(r, N_CHIPS, c // N_CHIPS).transpose(1, 0, 2)


def _from_shards(name, shards):
    (r, c), axis = BIG[name]
    if axis == 0:
        return shards.reshape(r, c)
    return shards.transpose(1, 0, 2).reshape(r, c)


def _relayout_w_in(w):
    d = w.shape[0]
    z = lambda n: jnp.zeros((d, n), w.dtype)
    return jnp.concatenate([w[:, :640], z(MLA_NOPE), w[:, 640:672], z(HEAD_PAD - MLA_QK), w[:, 672:]], axis=1)


def _unlayout_w_in(g):
    return jnp.concatenate([g[:, :640], g[:, 640 + MLA_NOPE:640 + MLA_QK], g[:, 768:]], axis=1)


def _pad_heads(v):
    lead = v.shape[:-1]
    return jnp.pad(v.reshape(lead + (HEADS, MLA_QK)), [(0, 0)] * len(lead) + [(0, 0), (0, HEAD_PAD - MLA_QK)]).reshape(
        lead + (HEADS * HEAD_PAD,))


def _halves(w):
    n = w.shape[1] // 2
    return [w[:, :n], w[:, n:]]


def _local_step(x, p, pos, tgt, gains, wts):
    d = D_MODEL
    inv_freq = ROPE_BASE ** (-jnp.arange(0, MLA_ROPE, 2, dtype=F32) / MLA_ROPE)
    zeros = lambda n: jnp.zeros((n,), F32)
    freq = jnp.concatenate([zeros(MLA_NOPE), inv_freq, inv_freq, zeros(HEAD_PAD - MLA_QK)])[None]
    sign = jnp.concatenate([zeros(MLA_NOPE), -jnp.ones((16,), F32), jnp.ones((16,), F32), zeros(HEAD_PAD - MLA_QK)])[None]
    pad_gain = lambda g: jnp.pad(g, ((0, 0), (0, HEAD_PAD - MLA_QK)))
    g_qh, g_kh = pad_gain(gains["q_head_norm"]), pad_gain(gains["k_head_norm"])
    w_in = _relayout_w_in(wts["w_in"])
    wq = _pad_heads(wts["w_q_up"])
    wkv = wts["w_kv_up"]
    wbm = jnp.pad(wts["w_branch_mla"].reshape(HEADS, 64, d), ((0, 0), (64, 0), (0, 0))).reshape(HEADS * HEAD_PAD, d)
    wbs, wo = wts["w_branch_sb"], wts["w_out"]

    u1 = _norm_call(x, gains["ffn1_norm"], "norm_ffn1")
    a1, b1, hm1 = _ffn_in_call(u1, wts["ffn1_w_in"], "ffn1_in")
    h1 = _ffn_out_call(hm1, wts["ffn1_w_out"], x, "ffn1_out")
    um = _norm_call(h1, gains["mix_norm"], "norm_mix")
    cq, ckv, krope, sbq, sbk, sbv, gates = _mix_in_call(um, w_in, "mix_in")
    prep_args = (cq, ckv, krope, pos, freq, sign, gains["q_latent_norm"], gains["kv_latent_norm"], g_qh, g_kh, wq, wkv)
    q, k, v = _mla_prep_call(*prep_args, "mla_prep")
    om, lse = _mla_fwd_call(q, k, v, "mla_fwd")
    osb, tot = _sb_fwd_call(sbq, sbk, sbv, "sb_fwd")
    h2, bm, bs, mg = _merge_out_call(om, osb, gates, h1, wbm, wbs, wo, "merge_out")
    u2 = _norm_call(h2, gains["ffn2_norm"], "norm_ffn2")
    a2, b2, hm2 = _ffn_in_call(u2, wts["ffn2_w_in"], "ffn2_in")
    h3 = _ffn_out_call(hm2, wts["ffn2_w_out"], h2, "ffn2_out")

    grads = {}
    dh3, dh3s, un, dgl, dpp, grads["ple_norm"], sq = _ple_call(
        h3, gains["ple_norm"], wts["w_ple_gate"], p, wts["w_ple_proj"], tgt, "ple")
    grads["w_ple_gate"] = _tn_call(un, dgl, "dw_ple_gate")
    grads["w_ple_proj"] = _tn_call(p, dpp, "dw_ple_proj")

    da2, db2 = _ffn_bwd_a_call(dh3s, a2, b2, wts["ffn2_w_out"], "ffn2_bwd_act")
    grads["ffn2_w_out"] = _tn_call(hm2, dh3s, "dw_ffn2_out")
    grads["ffn2_w_in"] = jnp.concatenate([_tn_call(u2, da2, "dw_ffn2_in_a"), _tn_call(u2, db2, "dw_ffn2_in_b")], axis=1)
    dh2, dh2b, grads["ffn2_norm"] = _norm_bwd_call([da2, db2], _halves(wts["ffn2_w_in"]), h2, gains["ffn2_norm"], dh3,
                                                   "ffn2_bwd_norm", half_out=False)

    dgates, dbm, dbs, dom, dos = _merge_bwd_call(dh2b, gates, bm, bs, wo, wbm, wbs, "merge_bwd")
    grads["w_out"] = _tn_call(mg, dh2b, "dw_out")
    grads["w_branch_mla"] = _tn_call(om, dbm, "dw_branch_mla").reshape(HEADS, HEAD_PAD, d)[:, 64:, :].reshape(512, d)
    grads["w_branch_sb"] = _tn_call(osb, dbs, "dw_branch_sb")
    dq, dk, dv = _mla_bwd_call(q, k, v, om, dom, lse, "mla_bwd")
    dsq, dsk, dsv = _sb_bwd_call(sbq, sbk, sbv, dos, tot, "sb_bwd")
    (dcq, dckv, dkr, dwq, grads["w_kv_up"], grads["q_latent_norm"], grads["kv_latent_norm"], dgqh, dgkh) = \
        _mla_prep_bwd_call(*prep_args, dq, dk, dv, "mla_prep_bwd")
    grads["w_q_up"] = dwq.reshape(Q_LORA, HEADS, HEAD_PAD)[:, :, :MLA_QK].reshape(Q_LORA, HEADS * MLA_QK)
    grads["q_head_norm"], grads["k_head_norm"] = dgqh[:, :MLA_QK], dgkh[:, :MLA_QK]
    dproj = jnp.concatenate([dcq, dckv, dkr, dsq, dsk.astype(BF16), dsv.astype(BF16), dgates], axis=1)
    grads["w_in"] = _unlayout_w_in(_tn_call(um, dproj, "dw_in"))
    dh1, dh1s, grads["mix_norm"] = _norm_bwd_call([dproj], [w_in], h1, gains["mix_norm"], dh2, "mix_bwd_norm",
                                                  half_out=True)

    da1, db1 = _ffn_bwd_a_call(dh1s, a1, b1, wts["ffn1_w_out"], "ffn1_bwd_act")
    grads["ffn1_w_out"] = _tn_call(hm1, dh1s, "dw_ffn1_out")
    grads["ffn1_w_in"] = jnp.concatenate([_tn_call(u1, da1, "dw_ffn1_in_a"), _tn_call(u1, db1, "dw_ffn1_in_b")], axis=1)
    dx, _, grads["ffn1_norm"] = _norm_bwd_call([da1, db1], _halves(wts["ffn1_w_in"]), x, gains["ffn1_norm"], dh1,
                                               "ffn1_bwd_norm", half_out=False)
    return sq, dx, grads


def kernel(x, p, positions, ffn1_norm, ffn1_w_in, ffn1_w_out, mix_norm, w_in, q_latent_norm, w_q_up, kv_latent_norm, w_kv_up, q_head_norm, k_head_norm, w_branch_mla, w_branch_sb, w_out, ffn2_norm, ffn2_w_in, ffn2_w_out, ple_norm, w_ple_gate, w_ple_proj, loss_target, m_ffn1_norm, m_ffn1_w_in, m_ffn1_w_out, m_mix_norm, m_w_in, m_q_latent_norm, m_w_q_up, m_kv_latent_norm, m_w_kv_up, m_q_head_norm, m_k_head_norm, m_w_branch_mla, m_w_branch_sb, m_w_out, m_ffn2_norm, m_ffn2_w_in, m_ffn2_w_out, m_ple_norm, m_w_ple_gate, m_w_ple_proj, v_ffn1_norm, v_ffn1_w_in, v_ffn1_w_out, v_mix_norm, v_w_in, v_q_latent_norm, v_w_q_up, v_kv_latent_norm, v_w_kv_up, v_q_head_norm, v_k_head_norm, v_w_branch_mla, v_w_branch_sb, v_w_out, v_ffn2_norm, v_ffn2_w_in, v_ffn2_w_out, v_ple_norm, v_w_ple_gate, v_w_ple_proj):
    given = dict(locals())
    w_shard = {n: given[n][0] for n in WEIGHT_ORDER}
    m_shard = {n: given["m_" + n][0] for n in WEIGHT_ORDER}
    v_shard = {n: given["v_" + n][0] for n in WEIGHT_ORDER}
    gains = {n: w_shard[n][None] for n in GAINS}

    c = lax.axis_index("c")
    chip = 2 * lax.axis_index("x") + lax.axis_index("y")
    flat_w = _flatten_shards({n: w_shard[n].astype(BF16) for n in BIG})
    others = _gather_call(flat_w, "gather_weights")
    gathered = _unflatten_shards(lax.dynamic_update_slice_in_dim(others, flat_w[None], chip, axis=0))
    wts = {n: _from_shards(n, gathered[n]) for n in BIG}

    sq, dx, grads = _local_step(x[0], p[0, 0], positions.reshape(-1, 1), loss_target[0], gains, wts)
    loss = lax.psum(0.5 * jnp.sum(sq) / D_MODEL, ("x", "y", "c"))

    flat_g = _flatten_shards({n: _to_shards(n, grads[n]) for n in BIG})
    from_sibling = _pair_send_call(flat_g, "grads_pair_send")
    pair_sum = _pair_sum_call(flat_g, from_sibling, c, BF16, "grads_pair_sum")
    by_chip = _chip_scatter_call(pair_sum, "grads_chip_scatter")
    own = lax.dynamic_slice_in_dim(pair_sum, chip, 1, axis=0)
    by_chip = lax.dynamic_update_slice_in_dim(by_chip, own, chip, axis=0)
    reduced_half = _sum_call(by_chip, F32, "grads_chip_sum")
    sibling_half = _pair_swap_call(reduced_half, "grads_pair_swap")
    reduced = jnp.where(c == 0, jnp.concatenate([reduced_half, sibling_half]),
                        jnp.concatenate([sibling_half, reduced_half]))
    g_shard = _unflatten_shards(reduced)

    gain_block = jnp.concatenate([jnp.pad(grads[n], ((0, 0), (0, D_MODEL - GAINS[n]))) for n in GAINS], axis=0)
    gain_sum = _sum_call(_all_gather_small_call(gain_block, "gains_all_gather"), F32, "gains_sum")
    for i, n in enumerate(GAINS):
        g_shard[n] = gain_sum[i, :GAINS[n]]

    outs = {"grad": {}, "delta": {}, "new_m": {}, "new_v": {}}
    gain_pack = lambda t: jnp.concatenate([jnp.pad(t[n][None], ((0, 0), (0, D_MODEL - GAINS[n]))) for n in GAINS], axis=0)
    gd, gm, gv = _adamw_call(gain_pack(w_shard), gain_sum, gain_pack(m_shard), gain_pack(v_shard), "adamw_gains")
    for i, n in enumerate(GAINS):
        outs["grad"][n] = g_shard[n][None]
        for kind, t in (("delta", gd), ("new_m", gm), ("new_v", gv)):
            outs[kind][n] = t[i, :GAINS[n]][None]
    for n in BIG:
        dlt, nm, nv = _adamw_call(w_shard[n], g_shard[n], m_shard[n], v_shard[n], "adamw_" + n)
        outs["grad"][n] = g_shard[n][None]
        outs["delta"][n], outs["new_m"][n], outs["new_v"][n] = dlt[None], nm[None], nv[None]

    return (loss, dx[None], *[outs["grad"][n] for n in WEIGHT_ORDER], *[outs["delta"][n] for n in WEIGHT_ORDER],
            *[outs["new_m"][n] for n in WEIGHT_ORDER], *[outs["new_v"][n] for n in WEIGHT_ORDER])
```

```python
import functools
import math

import jax
import jax.numpy as jnp
from jax import lax
from jax.experimental import pallas as pl
from jax.experimental.pallas import tpu as pltpu

F32 = jnp.float32
BF16 = jnp.bfloat16
MESH = pl.DeviceIdType.MESH

D_MODEL = 1024
D_FF = 2816
PLE_DIM = 256
NORM_EPS = 1e-6
HEADS = 8
MLA_NOPE = 64
MLA_ROPE = 32
MLA_QK = 96
Q_LORA = 384
KV_LORA = 256
SB_WIDTH = 512
ROPE_BASE = 10000.0
HEAD_PAD = 128
N_CHIPS = 4

ADAM_LR = 0.001
ADAM_B1 = 0.9
ADAM_B2 = 0.999
ADAM_EPS = 1e-08
ADAM_WD = 0.01
ADAM_STEP = 10

SEG_CQ = (0, 384)
SEG_CKV = (384, 256)
SEG_KROPE = (640, 128)
SEG_SBQ = (768, 512)
SEG_SBK = (1280, 512)
SEG_SBV = (1792, 512)
SEG_GATES = (2304, 2048)
IN_COLS_PAD = 4352

TM = 512
TM_SMALL = 256
TQ = 256
MLA_FWD_BLOCKS = 4
MLA_BWD_BLOCKS = 2
SB_FWD_BLOCKS = 2
SB_BWD_BLOCKS = 1
SB_HEAD = 64
SB_SCALE = 0.125
TN_MAX_COLS = 2816
FLAT_TILES = (256, 192, 64, 8)
VMEM_LIMIT = 56 * 1024 * 1024

NT = (((1,), (1,)), ((), ()))
TN = (((0,), (0,)), ((), ()))


def _cp(sem):
    return pltpu.CompilerParams(dimension_semantics=sem, vmem_limit_bytes=VMEM_LIMIT)


def _rows(tm, w):
    return pl.BlockSpec((tm, w), lambda i: (i, 0))


def _whole(shape):
    return pl.BlockSpec(shape, lambda i: (0,) * len(shape))


def _dot(a, b):
    return jnp.dot(a, b, preferred_element_type=F32)


def _dot_nt(a, b):
    return lax.dot_general(a, b, NT, preferred_element_type=F32)


def _dot_tn(a, b):
    return lax.dot_general(a, b, TN, preferred_element_type=F32)


def _rstd(x, n):
    return lax.rsqrt(jnp.sum(x * x, axis=-1, keepdims=True) / n + NORM_EPS)


def _rms_bwd(x, r, g, dy, n):
    gy = dy * g
    return r * gy - x * ((r * r * r) * (jnp.sum(x * gy, axis=-1, keepdims=True) / n))


def _sigmoid(x):
    return jax.nn.sigmoid(x)


def _pick(n, cands):
    for c in cands:
        if n % c == 0:
            return c
    return n


def _norm_call(h, g, name):
    s, d = h.shape
    tm = min(TM, s)

    def body(h_ref, g_ref, u_ref):
        x = h_ref[...]
        u_ref[...] = ((x * _rstd(x, d)) * g_ref[...]).astype(BF16)

    return pl.pallas_call(
        body, name=name, grid=(s // tm,),
        in_specs=[_rows(tm, d), _whole((1, d))], out_specs=_rows(tm, d),
        out_shape=jax.ShapeDtypeStruct((s, d), BF16), compiler_params=_cp(("parallel",)))(h, g)


def _ffn_in_call(u, w, name):
    s, d = u.shape
    n = w.shape[1] // 2
    tn = n // 2
    tm = min(TM, s)
    nj = n // tn

    def body(u_ref, wa_ref, wb_ref, a_ref, b_ref, hm_ref):
        uu = u_ref[...]
        a = _dot(uu, wa_ref[...])
        b = _dot(uu, wb_ref[...])
        a_ref[...] = a
        b_ref[...] = b
        hm_ref[...] = ((a * _sigmoid(a)) * b).astype(BF16)

    blk = pl.BlockSpec((tm, tn), lambda j, i: (i, j))
    return pl.pallas_call(
        body, name=name, grid=(nj, s // tm),
        in_specs=[pl.BlockSpec((tm, d), lambda j, i: (i, 0)),
                  pl.BlockSpec((d, tn), lambda j, i: (0, j)),
                  pl.BlockSpec((d, tn), lambda j, i: (0, j + nj))],
        out_specs=[blk, blk, blk],
        out_shape=[jax.ShapeDtypeStruct((s, n), F32), jax.ShapeDtypeStruct((s, n), F32),
                   jax.ShapeDtypeStruct((s, n), BF16)],
        compiler_params=_cp(("parallel", "parallel")))(u, w, w)


def _ffn_out_call(hm, w, h, name):
    s, n = hm.shape
    d = w.shape[1]
    tm = min(TM, s)

    def body(hm_ref, w_ref, h_ref, o_ref):
        o_ref[...] = h_ref[...] + 0.5 * _dot(hm_ref[...], w_ref[...])

    return pl.pallas_call(
        body, name=name, grid=(s // tm,),
        in_specs=[_rows(tm, n), _whole((n, d)), _rows(tm, d)], out_specs=_rows(tm, d),
        out_shape=jax.ShapeDtypeStruct((s, d), F32), compiler_params=_cp(("parallel",)))(hm, w, h)


def _mix_in_call(u, w, name):
    s, d = u.shape
    tm = min(TM_SMALL, s)
    segs = [(SEG_CQ, F32), (SEG_CKV, F32), (SEG_KROPE, F32), (SEG_SBQ, BF16), (SEG_SBK, BF16),
            (SEG_SBV, BF16), (SEG_GATES, F32)]

    def body(u_ref, w_ref, *outs):
        uu = u_ref[...]
        for ((off, width), _), o_ref in zip(segs, outs):
            o_ref[...] = _dot(uu, w_ref[:, off:off + width]).astype(o_ref.dtype)

    return pl.pallas_call(
        body, name=name, grid=(s // tm,),
        in_specs=[_rows(tm, d), _whole((d, IN_COLS_PAD))],
        out_specs=[_rows(tm, width) for (_, width), _ in segs],
        out_shape=[jax.ShapeDtypeStruct((s, width), dt) for (_, width), dt in segs],
        compiler_params=_cp(("parallel",)))(u, w)


def _lane(shape):
    return lax.broadcasted_iota(jnp.int32, shape, len(shape) - 1)


def _rot_half(y):
    lane = _lane(y.shape)
    swapped = jnp.where(lane < MLA_NOPE + MLA_ROPE // 2, pltpu.roll(y, HEAD_PAD - 16, 1), pltpu.roll(y, 16, 1))
    return jnp.where((lane >= MLA_NOPE) & (lane < MLA_QK), swapped, 0.0)


def _rope_tables(pos_ref, freq_ref, sign_ref):
    ang = pos_ref[...].astype(F32) * freq_ref[...]
    return jnp.cos(ang), jnp.sin(ang) * sign_ref[...]


def _head_fwd(x, g, cosv, ssv):
    r = _rstd(x, MLA_QK)
    y = (x * r) * g
    return y * cosv + _rot_half(y) * ssv, r


def _head_bwd(x, r, g, cosv, ssv, dout):
    dy = dout * cosv + _rot_half(dout * ssv)
    return _rms_bwd(x, r, g, dy, MLA_QK), jnp.sum(dy * (x * r), axis=0, keepdims=True)


def _mla_prep_call(cq, ckv, krope, pos, freq, sign, g_ql, g_kvl, g_qh, g_kh, wq, wkv, name):
    s = cq.shape[0]
    tm = min(TM_SMALL, s)
    width = HEADS * HEAD_PAD

    def body(cq_ref, ckv_ref, kr_ref, pos_ref, freq_ref, sign_ref, gql_ref, gkvl_ref, gqh_ref, gkh_ref,
             wq_ref, wkv_ref, q_ref, k_ref, v_ref):
        cosv, ssv = _rope_tables(pos_ref, freq_ref, sign_ref)
        x = cq_ref[...]
        qr = _dot(((x * _rstd(x, Q_LORA)) * gql_ref[...]).astype(BF16), wq_ref[...])
        x = ckv_ref[...]
        kv = _dot(((x * _rstd(x, KV_LORA)) * gkvl_ref[...]).astype(BF16), wkv_ref[...])
        kr = kr_ref[...]
        lane = _lane((tm, HEAD_PAD))
        for h in range(HEADS):
            sl = slice(h * HEAD_PAD, (h + 1) * HEAD_PAD)
            qh, _ = _head_fwd(qr[:, sl], gqh_ref[...], cosv, ssv)
            q_ref[:, sl] = qh.astype(BF16)
            kvh = kv[:, sl]
            kh, _ = _head_fwd(jnp.where(lane < MLA_NOPE, kvh, kr), gkh_ref[...], cosv, ssv)
            k_ref[:, sl] = kh.astype(BF16)
            v_ref[:, sl] = jnp.where(lane >= MLA_NOPE, kvh, 0.0).astype(BF16)

    out = jax.ShapeDtypeStruct((s, width), BF16)
    return pl.pallas_call(
        body, name=name, grid=(s // tm,),
        in_specs=[_rows(tm, Q_LORA), _rows(tm, KV_LORA), _rows(tm, HEAD_PAD), _rows(tm, 1),
                  _whole((1, HEAD_PAD)), _whole((1, HEAD_PAD)), _whole((1, Q_LORA)), _whole((1, KV_LORA)),
                  _whole((1, HEAD_PAD)), _whole((1, HEAD_PAD)), _whole((Q_LORA, width)), _whole((KV_LORA, width))],
        out_specs=[_rows(tm, width)] * 3, out_shape=[out, out, out],
        compiler_params=_cp(("parallel",)))(cq, ckv, krope, pos, freq, sign, g_ql, g_kvl, g_qh, g_kh, wq, wkv)


def _attn_specs(s, nb):
    qspec = pl.BlockSpec((TQ, nb * HEAD_PAD), lambda g, i: (i, g))
    kspec = pl.BlockSpec((s, nb * HEAD_PAD), lambda g, i: (0, g))
    return qspec, kspec


def _lanes(b):
    return slice(b * HEAD_PAD, (b + 1) * HEAD_PAD)


def _tri(cmp):
    r = lax.broadcasted_iota(jnp.int32, (TQ, TQ), 0)
    c = lax.broadcasted_iota(jnp.int32, (TQ, TQ), 1)
    return cmp(r, c)


def _mla_fwd_call(q, k, v, name):
    s, width = q.shape
    scale = 1.0 / math.sqrt(MLA_QK)

    nb = MLA_FWD_BLOCKS

    def body(q_ref, k_ref, v_ref, o_ref, lse_ref):
        qi = pl.program_id(1)
        qs = [q_ref[:, _lanes(b)] for b in range(nb)]
        causal = _tri(lambda r, c: c <= r)

        def step(kb, carry, diag):
            ks = pl.multiple_of(kb * TQ, TQ)
            heads = range(nb)
            scs = [_dot_nt(qs[b], k_ref[pl.ds(ks, TQ), _lanes(b)]) * scale for b in heads]
            if diag:
                scs = [jnp.where(causal, sc, -1e30) for sc in scs]
            mns = [jnp.maximum(carry[b][0], jnp.max(scs[b], axis=-1, keepdims=True)) for b in heads]
            als = [jnp.exp(carry[b][0] - mns[b]) for b in heads]
            ps = [jnp.exp(scs[b] - mns[b]) for b in heads]
            ls = [als[b] * carry[b][1] + jnp.sum(ps[b], axis=-1, keepdims=True) for b in heads]
            accs = [als[b] * carry[b][2] + _dot(ps[b].astype(BF16), v_ref[pl.ds(ks, TQ), _lanes(b)]) for b in heads]
            return tuple((mns[b], ls[b], accs[b]) for b in heads)

        init = tuple((jnp.full((TQ, 1), -1e30, F32), jnp.zeros((TQ, 1), F32), jnp.zeros((TQ, HEAD_PAD), F32))
                     for _ in range(nb))
        carry = step(qi, init, True)
        carry = lax.fori_loop(0, qi, lambda kb, c: step(kb, c, False), carry)
        for b in range(nb):
            m, l, acc = carry[b]
            o_ref[:, _lanes(b)] = (acc / l).astype(BF16)
            lse_ref[:, _lanes(b)] = jnp.broadcast_to(m + jnp.log(l), (TQ, HEAD_PAD))

    qspec, kspec = _attn_specs(s, nb)
    return pl.pallas_call(
        body, name=name, grid=(width // (nb * HEAD_PAD), s // TQ),
        in_specs=[qspec, kspec, kspec], out_specs=[qspec, qspec],
        out_shape=[jax.ShapeDtypeStruct((s, width), BF16), jax.ShapeDtypeStruct((s, width), F32)],
        compiler_params=_cp(("parallel", "arbitrary")))(q, k, v)


def _mla_bwd_call(q, k, v, o, do, lse, name):
    s, width = q.shape
    scale = 1.0 / math.sqrt(MLA_QK)
    nb = MLA_BWD_BLOCKS

    def body(q_ref, k_ref, v_ref, o_ref, do_ref, lse_ref, dq_ref, dk_ref, dv_ref):
        qi = pl.program_id(1)

        @pl.when(qi == 0)
        def _():
            dk_ref[...] = jnp.zeros_like(dk_ref)
            dv_ref[...] = jnp.zeros_like(dv_ref)

        qs = [q_ref[:, _lanes(b)] for b in range(nb)]
        dos = [do_ref[:, _lanes(b)] for b in range(nb)]
        lses = [lse_ref[:, b * HEAD_PAD:b * HEAD_PAD + 1] for b in range(nb)]
        dlts = [jnp.sum(dos[b].astype(F32) * o_ref[:, _lanes(b)].astype(F32), axis=-1, keepdims=True) for b in range(nb)]
        causal = _tri(lambda r, c: c <= r)

        def step(kb, dqs, diag):
            ks = pl.multiple_of(kb * TQ, TQ)
            heads = range(nb)
            kts = [k_ref[pl.ds(ks, TQ), _lanes(b)] for b in heads]
            scs = [_dot_nt(qs[b], kts[b]) for b in heads]
            dps = [_dot_nt(dos[b], v_ref[pl.ds(ks, TQ), _lanes(b)]) for b in heads]
            ps = [jnp.exp(scs[b] * scale - lses[b]) for b in heads]
            if diag:
                ps = [jnp.where(causal, p, 0.0) for p in ps]
            dss = [(ps[b] * (dps[b] - dlts[b]) * scale).astype(BF16) for b in heads]
            dvs = [_dot_tn(ps[b].astype(BF16), dos[b]) for b in heads]
            dks = [_dot_tn(dss[b], qs[b]) for b in heads]
            out = tuple(dqs[b] + _dot(dss[b], kts[b]) for b in heads)
            for b in heads:
                dv_ref[pl.ds(ks, TQ), _lanes(b)] += dvs[b]
                dk_ref[pl.ds(ks, TQ), _lanes(b)] += dks[b]
            return out

        dqs = step(qi, tuple(jnp.zeros((TQ, HEAD_PAD), F32) for _ in range(nb)), True)
        dqs = lax.fori_loop(0, qi, lambda kb, c: step(kb, c, False), dqs)
        for b in range(nb):
            dq_ref[:, _lanes(b)] = dqs[b]

    qspec, kspec = _attn_specs(s, nb)
    out = jax.ShapeDtypeStruct((s, width), F32)
    return pl.pallas_call(
        body, name=name, grid=(width // (nb * HEAD_PAD), s // TQ),
        in_specs=[qspec, kspec, kspec, qspec, qspec, qspec], out_specs=[qspec, kspec, kspec],
        out_shape=[out, out, out],
        compiler_params=_cp(("parallel", "arbitrary")))(q, k, v, o, do, lse)


def _dot_hilo(x, u):
    hi = x.astype(BF16)
    lo = (x - hi.astype(F32)).astype(BF16)
    return _dot(hi, u) + _dot(lo, u)


def _sb_logs(z):
    ls = jnp.minimum(z, 0.0) - jnp.log(1.0 + jnp.exp(-jnp.abs(z)))
    return ls, ls - z


def _sb_head_q(qb, first, hh):
    keep = first if hh == 0 else jnp.logical_not(first)
    return jnp.where(keep, qb, jnp.zeros_like(qb)) * jnp.asarray(SB_SCALE, qb.dtype)


def _sb_fwd_call(q, k, v, name):
    s, width = q.shape
    nb = SB_FWD_BLOCKS
    chains = [(b, hh) for b in range(nb) for hh in range(HEAD_PAD // SB_HEAD)]

    def body(q_ref, k_ref, v_ref, o_ref, t_ref):
        qi = pl.program_id(1)
        strict = _tri(lambda r, c: c < r)
        after = _tri(lambda r, c: r > c).astype(BF16)
        first = _lane((1, HEAD_PAD)) < SB_HEAD
        qhs = [_sb_head_q(q_ref[:, _lanes(b)], first, hh) for b, hh in chains]

        def step(kb, carry, diag):
            ks = pl.multiple_of(kb * TQ, TQ)
            ids = range(len(chains))
            zs = [_dot_nt(qhs[ci], k_ref[pl.ds(ks, TQ), _lanes(chains[ci][0])]) for ci in ids]
            logs = [_sb_logs(z) for z in zs]
            lss = [lg[0] for lg in logs]
            l1ms = [jnp.where(strict, lg[1], 0.0) if diag else lg[1] for lg in logs]
            sufs = [_dot_hilo(l1m, after) for l1m in l1ms]
            as_ = [jnp.exp(lss[ci] + sufs[ci] + carry[ci][0]) for ci in ids]
            if diag:
                as_ = [jnp.where(strict, a, 0.0) for a in as_]
            accs = [carry[ci][1] + _dot(as_[ci].astype(BF16), v_ref[pl.ds(ks, TQ), _lanes(chains[ci][0])]) for ci in ids]
            return tuple((carry[ci][0] + jnp.sum(l1ms[ci], axis=-1, keepdims=True), accs[ci]) for ci in ids)

        init = tuple((jnp.zeros((TQ, 1), F32), jnp.zeros((TQ, HEAD_PAD), F32)) for _ in chains)
        carry = step(qi, init, True)
        carry = lax.fori_loop(0, qi, lambda j, c: step(qi - 1 - j, c, False), carry)
        for b in range(nb):
            (cs0, acc0), (cs1, acc1) = carry[2 * b], carry[2 * b + 1]
            o_ref[:, _lanes(b)] = jnp.where(first, acc0, acc1).astype(BF16)
            t_ref[:, _lanes(b)] = jnp.where(first, cs0, cs1)

    qspec, kspec = _attn_specs(s, nb)
    return pl.pallas_call(
        body, name=name, grid=(width // (nb * HEAD_PAD), s // TQ),
        in_specs=[qspec, kspec, kspec], out_specs=[qspec, qspec],
        out_shape=[jax.ShapeDtypeStruct((s, width), BF16), jax.ShapeDtypeStruct((s, width), F32)],
        compiler_params=_cp(("parallel", "arbitrary")))(q, k, v)


def _sb_bwd_call(q, k, v, do, tot, name):
    s, width = q.shape
    nb = SB_BWD_BLOCKS
    chains = [(b, hh) for b in range(nb) for hh in range(HEAD_PAD // SB_HEAD)]

    def body(q_ref, k_ref, v_ref, do_ref, t_ref, dq_ref, dk_ref, dv_ref):
        qi = pl.program_id(1)

        @pl.when(qi == 0)
        def _():
            dk_ref[...] = jnp.zeros_like(dk_ref)
            dv_ref[...] = jnp.zeros_like(dv_ref)

        strict = _tri(lambda r, c: c < r)
        upto = _tri(lambda r, c: r <= c).astype(BF16)
        before = _tri(lambda r, c: r < c).astype(BF16)
        first = _lane((1, HEAD_PAD)) < SB_HEAD
        qhs = [_sb_head_q(q_ref[:, _lanes(b)], first, hh) for b, hh in chains]
        dohs = []
        for b, hh in chains:
            dob = do_ref[:, _lanes(b)]
            dohs.append(jnp.where(first if hh == 0 else jnp.logical_not(first), dob, jnp.zeros_like(dob)))
        tts = [t_ref[:, b * HEAD_PAD + hh * SB_HEAD:b * HEAD_PAD + hh * SB_HEAD + 1] for b, hh in chains]

        def step(kb, carry, diag):
            ks = pl.multiple_of(kb * TQ, TQ)
            ids = range(len(chains))
            kts = [k_ref[pl.ds(ks, TQ), _lanes(b)] for b, _ in chains]
            zs = [_dot_nt(qhs[ci], kts[ci]) for ci in ids]
            das = [_dot_nt(dohs[ci], v_ref[pl.ds(ks, TQ), _lanes(chains[ci][0])]) for ci in ids]
            logs = [_sb_logs(z) for z in zs]
            lss = [lg[0] for lg in logs]
            l1ms = [jnp.where(strict, lg[1], 0.0) if diag else lg[1] for lg in logs]
            pins = [_dot_hilo(l1m, upto) for l1m in l1ms]
            as_ = [jnp.exp(lss[ci] + (tts[ci] - carry[ci][0] - pins[ci])) for ci in ids]
            if diag:
                as_ = [jnp.where(strict, a, 0.0) for a in as_]
            gs = [as_[ci] * das[ci] for ci in ids]
            cexs = [carry[ci][1] + _dot_hilo(gs[ci], before) for ci in ids]
            dzs = [gs[ci] - jnp.exp(lss[ci]) * (gs[ci] + cexs[ci]) for ci in ids]
            if diag:
                dzs = [jnp.where(strict, dz, 0.0) for dz in dzs]
            dzbs = [dz.astype(BF16) for dz in dzs]
            dvps = [_dot_tn(as_[ci].astype(BF16), dohs[ci]) for ci in ids]
            dkps = [_dot_tn(dzbs[ci], qhs[ci]) for ci in ids]
            out = tuple((carry[ci][0] + jnp.sum(l1ms[ci], axis=-1, keepdims=True),
                         carry[ci][1] + jnp.sum(gs[ci], axis=-1, keepdims=True),
                         carry[ci][2] + _dot(dzbs[ci], kts[ci])) for ci in ids)
            for b in range(nb):
                dk_ref[pl.ds(ks, TQ), _lanes(b)] += dkps[2 * b] + dkps[2 * b + 1]
                dv_ref[pl.ds(ks, TQ), _lanes(b)] += dvps[2 * b] + dvps[2 * b + 1]
            return out

        init = tuple((jnp.zeros((TQ, 1), F32), jnp.zeros((TQ, 1), F32), jnp.zeros((TQ, HEAD_PAD), F32)) for _ in chains)
        carry = lax.fori_loop(0, qi, lambda kb, c: step(kb, c, False), init)
        carry = step(qi, carry, True)
        for b in range(nb):
            dq_ref[:, _lanes(b)] = (jnp.where(first, carry[2 * b][2], carry[2 * b + 1][2]) * SB_SCALE).astype(BF16)

    qspec, kspec = _attn_specs(s, nb)
    return pl.pallas_call(
        body, name=name, grid=(width // (nb * HEAD_PAD), s // TQ),
        in_specs=[qspec, kspec, kspec, qspec, qspec], out_specs=[qspec, kspec, kspec],
        out_shape=[jax.ShapeDtypeStruct((s, width), BF16), jax.ShapeDtypeStruct((s, width), F32),
                   jax.ShapeDtypeStruct((s, width), F32)],
        compiler_params=_cp(("parallel", "arbitrary")))(q, k, v, do, tot)


def _merge_out_call(om, osb, gates, h, wbm, wbs, wo, name):
    s, d = h.shape
    tm = min(TM_SMALL, s)

    def body(om_ref, os_ref, g_ref, h_ref, wbm_ref, wbs_ref, wo_ref, h2_ref, bm_ref, bs_ref, mg_ref):
        bm = _dot(om_ref[...], wbm_ref[...])
        bs = _dot(os_ref[...], wbs_ref[...])
        mg = (_sigmoid(g_ref[:, :d]) * bm + _sigmoid(g_ref[:, d:]) * bs).astype(BF16)
        bm_ref[...] = bm
        bs_ref[...] = bs
        mg_ref[...] = mg
        h2_ref[...] = h_ref[...] + _dot(mg, wo_ref[...])

    return pl.pallas_call(
        body, name=name, grid=(s // tm,),
        in_specs=[_rows(tm, om.shape[1]), _rows(tm, SB_WIDTH), _rows(tm, 2 * d), _rows(tm, d),
                  _whole(wbm.shape), _whole(wbs.shape), _whole(wo.shape)],
        out_specs=[_rows(tm, d)] * 4,
        out_shape=[jax.ShapeDtypeStruct((s, d), F32), jax.ShapeDtypeStruct((s, d), F32),
                   jax.ShapeDtypeStruct((s, d), F32), jax.ShapeDtypeStruct((s, d), BF16)],
        compiler_params=_cp(("parallel",)))(om, osb, gates, h, wbm, wbs, wo)


def _ple_call(h, g, wg, p, wp, tgt, name):
    s, d = h.shape
    tm = min(TM_SMALL, s)

    def body(h_ref, g_ref, wg_ref, p_ref, wp_ref, t_ref, dh_ref, dhs_ref, un_ref, dgl_ref, dpp_ref, dg_ref, sq_ref):
        @pl.when(pl.program_id(0) == 0)
        def _():
            dg_ref[...] = jnp.zeros_like(dg_ref)
            sq_ref[...] = jnp.zeros_like(sq_ref)

        x = h_ref[...]
        gain = g_ref[...]
        r = _rstd(x, d)
        xh = x * r
        un = (xh * gain).astype(BF16)
        sg = _sigmoid(_dot(un, wg_ref[...]))
        pp = _dot(p_ref[...].astype(BF16), wp_ref[...])
        diff = (x + sg * pp) - t_ref[...]
        sq_ref[...] += jnp.sum(diff * diff, axis=0, keepdims=True)
        dy = diff * (1.0 / d)
        dgl = ((dy * pp) * (sg * (1.0 - sg))).astype(BF16)
        dun = _dot_nt(dgl, wg_ref[...])
        dg_ref[...] += jnp.sum(dun * xh, axis=0, keepdims=True)
        dh = dy + _rms_bwd(x, r, gain, dun, d)
        dh_ref[...] = dh
        dhs_ref[...] = (0.5 * dh).astype(BF16)
        un_ref[...] = un
        dgl_ref[...] = dgl
        dpp_ref[...] = (dy * sg).astype(BF16)

    bf = jax.ShapeDtypeStruct((s, d), BF16)
    vec = jax.ShapeDtypeStruct((1, d), F32)
    return pl.pallas_call(
        body, name=name, grid=(s // tm,),
        in_specs=[_rows(tm, d), _whole((1, d)), _whole(wg.shape), _rows(tm, PLE_DIM), _whole(wp.shape), _rows(tm, d)],
        out_specs=[_rows(tm, d)] * 5 + [_whole((1, d))] * 2,
        out_shape=[jax.ShapeDtypeStruct((s, d), F32), bf, bf, bf, bf, vec, vec],
        compiler_params=_cp(("arbitrary",)))(h, g, wg, p, wp, tgt)


def _ffn_bwd_a_call(dhs, a, b, wo, name):
    s, n = a.shape
    d = dhs.shape[1]
    tn = n // 2
    tm = min(TM, s)

    def body(dh_ref, a_ref, b_ref, wo_ref, da_ref, db_ref):
        dhm = _dot_nt(dh_ref[...], wo_ref[...])
        av = a_ref[...]
        sa = _sigmoid(av)
        da_ref[...] = (dhm * b_ref[...] * (sa * (1.0 + av * (1.0 - sa)))).astype(BF16)
        db_ref[...] = (dhm * (av * sa)).astype(BF16)

    blk = pl.BlockSpec((tm, tn), lambda j, i: (i, j))
    return pl.pallas_call(
        body, name=name, grid=(n // tn, s // tm),
        in_specs=[pl.BlockSpec((tm, d), lambda j, i: (i, 0)), blk, blk, pl.BlockSpec((tn, d), lambda j, i: (j, 0))],
        out_specs=[blk, blk],
        out_shape=[jax.ShapeDtypeStruct((s, n), BF16)] * 2,
        compiler_params=_cp(("parallel", "parallel")))(dhs, a, b, wo)


def _norm_bwd_call(dy_list, w_list, h, g, dh_in, name, half_out):
    s, d = h.shape
    tm = min(TM_SMALL, s)
    nk = len(dy_list)
    factor = 0.5 if half_out else 1.0

    def body(*refs):
        dy_refs = refs[:nk]
        w_refs = refs[nk:2 * nk]
        h_ref, g_ref, dhin_ref, dh_ref, dhb_ref, dg_ref = refs[2 * nk:]

        @pl.when(pl.program_id(0) == 0)
        def _():
            dg_ref[...] = jnp.zeros_like(dg_ref)

        du = _dot_nt(dy_refs[0][...], w_refs[0][...])
        for dy_ref, w_ref in zip(dy_refs[1:], w_refs[1:]):
            du = du + _dot_nt(dy_ref[...], w_ref[...])
        x = h_ref[...]
        r = _rstd(x, d)
        dg_ref[...] += jnp.sum(du * (x * r), axis=0, keepdims=True)
        dh = dhin_ref[...] + _rms_bwd(x, r, g_ref[...], du, d)
        dh_ref[...] = dh
        dhb_ref[...] = (factor * dh).astype(BF16)

    return pl.pallas_call(
        body, name=name, grid=(s // tm,),
        in_specs=[_rows(tm, dy.shape[1]) for dy in dy_list] + [_whole(w.shape) for w in w_list]
        + [_rows(tm, d), _whole((1, d)), _rows(tm, d)],
        out_specs=[_rows(tm, d), _rows(tm, d), _whole((1, d))],
        out_shape=[jax.ShapeDtypeStruct((s, d), F32), jax.ShapeDtypeStruct((s, d), BF16),
                   jax.ShapeDtypeStruct((1, d), F32)],
        compiler_params=_cp(("arbitrary",)))(*dy_list, *w_list, h, g, dh_in)


def _merge_bwd_call(dhb, gates, bm, bs, wo, wbm, wbs, name):
    s, d = bm.shape
    tm = min(TM_SMALL, s)

    def body(dh_ref, g_ref, bm_ref, bs_ref, wo_ref, wbm_ref, wbs_ref, dg_ref, dbm_ref, dbs_ref, dom_ref, dos_ref):
        dmg = _dot_nt(dh_ref[...], wo_ref[...])
        s1 = _sigmoid(g_ref[:, :d])
        s2 = _sigmoid(g_ref[:, d:])
        dg_ref[:, :d] = (dmg * bm_ref[...] * (s1 * (1.0 - s1))).astype(BF16)
        dg_ref[:, d:] = (dmg * bs_ref[...] * (s2 * (1.0 - s2))).astype(BF16)
        dbm = (dmg * s1).astype(BF16)
        dbs = (dmg * s2).astype(BF16)
        dbm_ref[...] = dbm
        dbs_ref[...] = dbs
        dom_ref[...] = _dot_nt(dbm, wbm_ref[...]).astype(BF16)
        dos_ref[...] = _dot_nt(dbs, wbs_ref[...]).astype(BF16)

    wm = wbm.shape[0]
    return pl.pallas_call(
        body, name=name, grid=(s // tm,),
        in_specs=[_rows(tm, d), _rows(tm, 2 * d), _rows(tm, d), _rows(tm, d),
                  _whole(wo.shape), _whole(wbm.shape), _whole(wbs.shape)],
        out_specs=[_rows(tm, 2 * d), _rows(tm, d), _rows(tm, d), _rows(tm, wm), _rows(tm, SB_WIDTH)],
        out_shape=[jax.ShapeDtypeStruct((s, 2 * d), BF16), jax.ShapeDtypeStruct((s, d), BF16),
                   jax.ShapeDtypeStruct((s, d), BF16), jax.ShapeDtypeStruct((s, wm), BF16),
                   jax.ShapeDtypeStruct((s, SB_WIDTH), BF16)],
        compiler_params=_cp(("parallel",)))(dhb, gates, bm, bs, wo, wbm, wbs)


def _mla_prep_bwd_call(cq, ckv, krope, pos, freq, sign, g_ql, g_kvl, g_qh, g_kh, wq, wkv, dq, dk, dv, name):
    s = cq.shape[0]
    tm = min(TM_SMALL, s)
    width = HEADS * HEAD_PAD

    def body(cq_ref, ckv_ref, kr_ref, pos_ref, freq_ref, sign_ref, gql_ref, gkvl_ref, gqh_ref, gkh_ref,
             wq_ref, wkv_ref, dq_ref, dk_ref, dv_ref,
             dcq_ref, dckv_ref, dkr_ref, dwq_ref, dwkv_ref, dgql_ref, dgkvl_ref, dgqh_ref, dgkh_ref, dqr_ref, dkv_ref):
        @pl.when(pl.program_id(0) == 0)
        def _():
            for ref in (dwq_ref, dwkv_ref, dgql_ref, dgkvl_ref, dgqh_ref, dgkh_ref):
                ref[...] = jnp.zeros_like(ref)

        cosv, ssv = _rope_tables(pos_ref, freq_ref, sign_ref)
        xq = cq_ref[...]
        rq = _rstd(xq, Q_LORA)
        cqn = ((xq * rq) * gql_ref[...]).astype(BF16)
        qr = _dot(cqn, wq_ref[...])
        xk = ckv_ref[...]
        rk = _rstd(xk, KV_LORA)
        ckvn = ((xk * rk) * gkvl_ref[...]).astype(BF16)
        kv = _dot(ckvn, wkv_ref[...])
        kr = kr_ref[...]
        lane = _lane((tm, HEAD_PAD))
        dkr = jnp.zeros((tm, HEAD_PAD), F32)
        dgqh = jnp.zeros((1, HEAD_PAD), F32)
        dgkh = jnp.zeros((1, HEAD_PAD), F32)
        for h in range(HEADS):
            sl = slice(h * HEAD_PAD, (h + 1) * HEAD_PAD)
            x = qr[:, sl]
            dx, dgh = _head_bwd(x, _rstd(x, MLA_QK), gqh_ref[...], cosv, ssv, dq_ref[:, sl])
            dqr_ref[:, sl] = dx.astype(BF16)
            dgqh = dgqh + dgh
            x = jnp.where(lane < MLA_NOPE, kv[:, sl], kr)
            dx, dgh = _head_bwd(x, _rstd(x, MLA_QK), gkh_ref[...], cosv, ssv, dk_ref[:, sl])
            dgkh = dgkh + dgh
            dkr = dkr + jnp.where(lane >= MLA_NOPE, dx, 0.0)
            dkv_ref[:, sl] = jnp.where(lane < MLA_NOPE, dx, dv_ref[:, sl]).astype(BF16)
        dgqh_ref[...] += dgqh
        dgkh_ref[...] += dgkh
        dkr_ref[...] = dkr.astype(BF16)
        dqr = dqr_ref[...]
        dkvb = dkv_ref[...]
        dwq_ref[...] += _dot_tn(cqn, dqr)
        dwkv_ref[...] += _dot_tn(ckvn, dkvb)
        dcqn = _dot_nt(dqr, wq_ref[...])
        dgql_ref[...] += jnp.sum(dcqn * (xq * rq), axis=0, keepdims=True)
        dcq_ref[...] = _rms_bwd(xq, rq, gql_ref[...], dcqn, Q_LORA).astype(BF16)
        dckvn = _dot_nt(dkvb, wkv_ref[...])
        dgkvl_ref[...] += jnp.sum(dckvn * (xk * rk), axis=0, keepdims=True)
        dckv_ref[...] = _rms_bwd(xk, rk, gkvl_ref[...], dckvn, KV_LORA).astype(BF16)

    vec = lambda n: jax.ShapeDtypeStruct((1, n), F32)
    outs = pl.pallas_call(
        body, name=name, grid=(s // tm,),
        in_specs=[_rows(tm, Q_LORA), _rows(tm, KV_LORA), _rows(tm, HEAD_PAD), _rows(tm, 1),
                  _whole((1, HEAD_PAD)), _whole((1, HEAD_PAD)), _whole((1, Q_LORA)), _whole((1, KV_LORA)),
                  _whole((1, HEAD_PAD)), _whole((1, HEAD_PAD)), _whole((Q_LORA, width)), _whole((KV_LORA, width)),
                  _rows(tm, width), _rows(tm, width), _rows(tm, width)],
        out_specs=[_rows(tm, Q_LORA), _rows(tm, KV_LORA), _rows(tm, HEAD_PAD), _whole((Q_LORA, width)),
                   _whole((KV_LORA, width)), _whole((1, Q_LORA)), _whole((1, KV_LORA)), _whole((1, HEAD_PAD)),
                   _whole((1, HEAD_PAD)), _rows(tm, width), _rows(tm, width)],
        out_shape=[jax.ShapeDtypeStruct((s, Q_LORA), BF16), jax.ShapeDtypeStruct((s, KV_LORA), BF16),
                   jax.ShapeDtypeStruct((s, HEAD_PAD), BF16), jax.ShapeDtypeStruct((Q_LORA, width), F32),
                   jax.ShapeDtypeStruct((KV_LORA, width), F32), vec(Q_LORA), vec(KV_LORA), vec(HEAD_PAD), vec(HEAD_PAD),
                   jax.ShapeDtypeStruct((s, width), BF16), jax.ShapeDtypeStruct((s, width), BF16)],
        compiler_params=_cp(("arbitrary",)))(cq, ckv, krope, pos, freq, sign, g_ql, g_kvl, g_qh, g_kh, wq, wkv, dq, dk, dv)
    return outs[:9]


def _tn_call(a, b, name, shard_cols=None):
    s, ka = a.shape
    nb = b.shape[1]
    ti = _pick(ka, (512, 256, 128))
    if shard_cols is not None:
        tj = shard_cols
    else:
        tj = nb if nb <= TN_MAX_COLS else _pick(nb, (2176, 1024, 512, 256, 128))
    ts = min(1024 if tj <= 1024 else 512, s)
    ns = s // ts

    def body(a_ref, b_ref, o_ref, acc_ref):
        part = _dot_tn(a_ref[...].astype(BF16), b_ref[...].astype(BF16))

        @pl.when(pl.program_id(2) == 0)
        def _():
            acc_ref[...] = part

        @pl.when(pl.program_id(2) != 0)
        def _():
            acc_ref[...] += part

        @pl.when(pl.program_id(2) == ns - 1)
        def _():
            o_ref[...] = acc_ref[...].astype(o_ref.dtype)

    if shard_cols is None:
        out_spec = pl.BlockSpec((ti, tj), lambda i, j, t: (i, j))
        out_shape = jax.ShapeDtypeStruct((ka, nb), BF16)
    else:
        out_spec = pl.BlockSpec((None, ti, tj), lambda i, j, t: (j, i, 0))
        out_shape = jax.ShapeDtypeStruct((nb // tj, ka, tj), BF16)
    return pl.pallas_call(
        body, name=name, grid=(ka // ti, nb // tj, ns),
        in_specs=[pl.BlockSpec((ts, ti), lambda i, j, t: (t, i)), pl.BlockSpec((ts, tj), lambda i, j, t: (t, j))],
        out_specs=out_spec, out_shape=out_shape, scratch_shapes=[pltpu.VMEM((ti, tj), F32)],
        compiler_params=_cp(("parallel", "parallel", "arbitrary")))(a, b)


def _sum_call(parts, out_dtype, name):
    n, r, w = parts.shape
    tr = _pick(r, FLAT_TILES)

    def body(p_ref, o_ref):
        acc = p_ref[0].astype(F32)
        for k in range(1, n):
            acc = acc + p_ref[k].astype(F32)
        o_ref[...] = acc.astype(out_dtype)

    return pl.pallas_call(
        body, name=name, grid=(r // tr,),
        in_specs=[pl.BlockSpec((n, tr, w), lambda i: (0, i, 0))], out_specs=_rows(tr, w),
        out_shape=jax.ShapeDtypeStruct((r, w), out_dtype), compiler_params=_cp(("parallel",)))(parts)


def _chip_sum_call(by_chip, core, name):
    n, r, w = by_chip.shape
    tr = _pick(r, FLAT_TILES)
    nblk = r // tr

    def body(c_ref, p_ref, o_ref):
        acc = p_ref[0].astype(F32)
        for k in range(1, n):
            acc = acc + p_ref[k].astype(F32)
        o_ref[...] = acc

    return pl.pallas_call(
        body, name=name,
        grid_spec=pltpu.PrefetchScalarGridSpec(
            num_scalar_prefetch=1, grid=(nblk,),
            in_specs=[pl.BlockSpec((n, tr, w), lambda i, c_ref: (0, i, 0))],
            out_specs=pl.BlockSpec((tr, w), lambda i, c_ref: (c_ref[0] * nblk + i, 0))),
        out_shape=jax.ShapeDtypeStruct((2 * r, w), F32),
        compiler_params=_cp(("parallel",)))(core.reshape(1).astype(jnp.int32), by_chip)


def _pair_sum_call(full, other, core, out_dtype, name):
    n, r, w = other.shape
    tr = _pick(r, FLAT_TILES)
    nblk = r // tr

    def body(c_ref, a_ref, b_ref, o_ref):
        o_ref[...] = (a_ref[...].astype(F32) + b_ref[...].astype(F32)).astype(out_dtype)

    spec = pl.BlockSpec((None, tr, w), lambda k, i, c_ref: (k, i, 0))
    return pl.pallas_call(
        body, name=name,
        grid_spec=pltpu.PrefetchScalarGridSpec(
            num_scalar_prefetch=1, grid=(n, nblk),
            in_specs=[pl.BlockSpec((None, tr, w), lambda k, i, c_ref: (k, c_ref[0] * nblk + i, 0)), spec],
            out_specs=spec),
        out_shape=jax.ShapeDtypeStruct((n, r, w), out_dtype),
        compiler_params=_cp(("parallel", "parallel")))(core.reshape(1).astype(jnp.int32), full, other)


def _adamw_call(w, g, row0, m, v, name):
    r, c = w.shape
    tr = _pick(math.gcd(r, row0) if row0 else r, (256, 128, 64, 32, 16, 8))
    off = row0 // tr

    def body(w_ref, g_ref, m_ref, v_ref, g_out_ref, d_ref, nm_ref, nv_ref):
        gg = g_ref[...]
        g_out_ref[...] = gg
        nm = ADAM_B1 * m_ref[...] + (1.0 - ADAM_B1) * gg
        nv = ADAM_B2 * v_ref[...] + (1.0 - ADAM_B2) * (gg * gg)
        m_hat = nm / (1.0 - ADAM_B1 ** ADAM_STEP)
        v_hat = nv / (1.0 - ADAM_B2 ** ADAM_STEP)
        d_ref[...] = -ADAM_LR * (m_hat / (jnp.sqrt(v_hat) + ADAM_EPS) + ADAM_WD * w_ref[...])
        nm_ref[...] = nm
        nv_ref[...] = nv

    out = jax.ShapeDtypeStruct((r, c), F32)
    g_spec = pl.BlockSpec((tr, c), lambda i: (off + i, 0))
    return pl.pallas_call(
        body, name=name, grid=(r // tr,), in_specs=[_rows(tr, c), g_spec, _rows(tr, c), _rows(tr, c)],
        out_specs=[_rows(tr, c)] * 4, out_shape=[out, out, out, out], compiler_params=_cp(("parallel",)))(w, g, m, v)


HBM = pl.BlockSpec(memory_space=pl.ANY)


def _position():
    x, y, c = lax.axis_index("x"), lax.axis_index("y"), lax.axis_index("c")
    chips = [(1 - x, y), (x, 1 - y), (1 - x, 1 - y)]
    return x, y, c, chips


def _gather_call(parts, name):
    n = len(parts)

    def body(*refs):
        in_refs, out_refs = refs[:n], refs[n:2 * n]
        send_sems, recv_sems, fwd_send, fwd_recv = refs[2 * n:]
        x, y, c, chips = _position()

        def piece(k, chip, core):
            half = parts[k].shape[0] // 2
            return out_refs[k].at[2 * chip[0] + chip[1], pl.ds(core * half, half), :]

        first = []
        for j, chip in enumerate(chips):
            for k in range(n):
                half = parts[k].shape[0] // 2
                cp = pltpu.make_async_remote_copy(
                    src_ref=in_refs[k].at[pl.ds(c * half, half), :], dst_ref=piece(k, (x, y), c),
                    send_sem=send_sems.at[n * j + k], recv_sem=recv_sems.at[n * j + k],
                    device_id=(*chip, c), device_id_type=MESH)
                cp.start()
                first.append(cp)
        passed = []
        for j, chip in enumerate(chips):
            for k in range(n):
                first[n * j + k].wait_recv()
                cp = pltpu.make_async_remote_copy(
                    src_ref=piece(k, chip, c), dst_ref=piece(k, chip, c),
                    send_sem=fwd_send.at[n * j + k], recv_sem=fwd_recv.at[n * j + k],
                    device_id=(x, y, 1 - c), device_id_type=MESH)
                cp.start()
                passed.append(cp)
        for cp in passed:
            cp.wait_recv()
        for cp in first + passed:
            cp.wait_send()

    sems = pltpu.SemaphoreType.DMA((3 * n,))
    return pl.pallas_call(
        body, name=name, in_specs=[HBM] * n, out_specs=[HBM] * n,
        out_shape=[jax.ShapeDtypeStruct((N_CHIPS,) + p.shape, p.dtype) for p in parts],
        scratch_shapes=[sems, sems, sems, sems])(*parts)


def _pair_send_call(parts, name):
    n = len(parts)

    def body(*refs):
        in_refs, out_refs = refs[:n], refs[n:2 * n]
        send_sems, recv_sems = refs[2 * n:]
        x, y, c, _ = _position()
        copies = []
        for k in range(n):
            half = parts[k].shape[1] // 2
            cp = pltpu.make_async_remote_copy(
                src_ref=in_refs[k].at[:, pl.ds((1 - c) * half, half), :], dst_ref=out_refs[k],
                send_sem=send_sems.at[k], recv_sem=recv_sems.at[k], device_id=(x, y, 1 - c), device_id_type=MESH)
            cp.start()
            copies.append(cp)
        for cp in copies:
            cp.wait()

    sems = pltpu.SemaphoreType.DMA((n,))
    return pl.pallas_call(
        body, name=name, in_specs=[HBM] * n, out_specs=[HBM] * n,
        out_shape=[jax.ShapeDtypeStruct((p.shape[0], p.shape[1] // 2, p.shape[2]), p.dtype) for p in parts],
        scratch_shapes=[sems, sems])(*parts)


def _chip_scatter_call(parts, name):
    n = len(parts)

    def body(*refs):
        in_refs, out_refs = refs[:n], refs[n:2 * n]
        send_sems, recv_sems = refs[2 * n:]
        x, y, c, chips = _position()
        me = 2 * x + y
        copies = []
        for j, chip in enumerate(chips):
            for k in range(n):
                cp = pltpu.make_async_remote_copy(
                    src_ref=in_refs[k].at[2 * chip[0] + chip[1]], dst_ref=out_refs[k].at[me],
                    send_sem=send_sems.at[n * j + k], recv_sem=recv_sems.at[n * j + k],
                    device_id=(*chip, c), device_id_type=MESH)
                cp.start()
                copies.append(cp)
        for cp in copies:
            cp.wait()

    sems = pltpu.SemaphoreType.DMA((3 * n,))
    return pl.pallas_call(
        body, name=name, in_specs=[HBM] * n, out_specs=[HBM] * n,
        out_shape=[jax.ShapeDtypeStruct(p.shape, p.dtype) for p in parts],
        scratch_shapes=[sems, sems])(*parts)


def _pair_swap_call(parts, name):
    n = len(parts)

    def body(*refs):
        out_refs = refs[n:2 * n]
        send_sems, recv_sems = refs[2 * n:]
        x, y, c, _ = _position()
        copies = []
        for k in range(n):
            half = parts[k].shape[0] // 2
            mine = out_refs[k].at[pl.ds(c * half, half), :]
            cp = pltpu.make_async_remote_copy(
                src_ref=mine, dst_ref=mine, send_sem=send_sems.at[k], recv_sem=recv_sems.at[k],
                device_id=(x, y, 1 - c), device_id_type=MESH)
            cp.start()
            copies.append(cp)
        for cp in copies:
            cp.wait()

    sems = pltpu.SemaphoreType.DMA((n,))
    return pl.pallas_call(
        body, name=name, in_specs=[HBM] * n, out_specs=[HBM] * n,
        out_shape=[jax.ShapeDtypeStruct(p.shape, p.dtype) for p in parts],
        input_output_aliases={k: k for k in range(n)},
        scratch_shapes=[sems, sems])(*parts)


def _all_gather_small_call(block, name):
    r, w = block.shape

    def body(in_ref, out_ref, send_sems, recv_sems, local_sem):
        x, y, c, _ = _position()
        me = 4 * x + 2 * y + c
        own = pltpu.make_async_copy(in_ref, out_ref.at[me], local_sem)
        own.start()
        copies = []
        for k in range(1, 8):
            peer = (x ^ (k >> 2), y ^ ((k >> 1) & 1), c ^ (k & 1))
            cp = pltpu.make_async_remote_copy(
                src_ref=in_ref, dst_ref=out_ref.at[me], send_sem=send_sems.at[k - 1], recv_sem=recv_sems.at[k - 1],
                device_id=peer, device_id_type=MESH)
            cp.start()
            copies.append(cp)
        for cp in copies:
            cp.wait()
        own.wait()

    return pl.pallas_call(
        body, name=name, in_specs=[HBM], out_specs=HBM,
        out_shape=jax.ShapeDtypeStruct((8, r, w), block.dtype),
        scratch_shapes=[pltpu.SemaphoreType.DMA((7,)), pltpu.SemaphoreType.DMA((7,)), pltpu.SemaphoreType.DMA])(block)


BIG = {
    "ffn1_w_in": ((D_MODEL, 2 * D_FF), 1), "ffn1_w_out": ((D_FF, D_MODEL), 0),
    "w_in": ((D_MODEL, 4256), 1), "w_q_up": ((Q_LORA, HEADS * MLA_QK), 1), "w_kv_up": ((KV_LORA, 1024), 1),
    "w_branch_mla": ((512, D_MODEL), 1), "w_branch_sb": ((SB_WIDTH, D_MODEL), 1), "w_out": ((D_MODEL, D_MODEL), 0),
    "ffn2_w_in": ((D_MODEL, 2 * D_FF), 1), "ffn2_w_out": ((D_FF, D_MODEL), 0),
    "w_ple_gate": ((D_MODEL, D_MODEL), 0), "w_ple_proj": ((PLE_DIM, D_MODEL), 1),
}
GAINS = {"ffn1_norm": 1024, "mix_norm": 1024, "q_latent_norm": 384, "kv_latent_norm": 256, "q_head_norm": 96,
         "k_head_norm": 96, "ffn2_norm": 1024, "ple_norm": 1024}
WEIGHT_ORDER = ["ffn1_norm", "ffn1_w_in", "ffn1_w_out", "mix_norm", "w_in", "q_latent_norm", "w_q_up",
                "kv_latent_norm", "w_kv_up", "q_head_norm", "k_head_norm", "w_branch_mla", "w_branch_sb", "w_out",
                "ffn2_norm", "ffn2_w_in", "ffn2_w_out", "ple_norm", "w_ple_gate", "w_ple_proj"]


def _shard_shape(name):
    (r, c), axis = BIG[name]
    return (r // N_CHIPS, c) if axis == 0 else (r, c // N_CHIPS)


PARTS = [("ffn1_w_in", "ffn2_w_in"), ("ffn1_w_out", "ffn2_w_out", "w_out", "w_ple_gate"), ("w_in",),
         ("w_kv_up", "w_branch_mla", "w_branch_sb", "w_ple_proj"), ("w_q_up",)]


def _join_parts(shards):
    return [shards[part[0]] if len(part) == 1 else jnp.concatenate([shards[n] for n in part], axis=-2) for part in PARTS]


def _part_rows():
    where = {}
    for k, part in enumerate(PARTS):
        at = 0
        for n in part:
            where[n] = (k, at)
            at += _shard_shape(n)[0]
    return where


def _split_parts(parts):
    where = _part_rows()
    return {n: parts[k][..., at:at + _shard_shape(n)[0], :] for n, (k, at) in where.items()}


def _to_shards(name, full):
    (r, c), axis = BIG[name]
    if axis == 0:
        return full.reshape(N_CHIPS, r // N_CHIPS, c)
    return full.reshape(r, N_CHIPS, c // N_CHIPS).transpose(1, 0, 2)


def _from_shards(name, shards):
    (r, c), axis = BIG[name]
    if axis == 0:
        return shards.reshape(r, c)
    return shards.transpose(1, 0, 2).reshape(r, c)


def _relayout_w_in(w):
    d = w.shape[0]
    z = lambda n: jnp.zeros((d, n), w.dtype)
    return jnp.concatenate([w[:, :640], z(MLA_NOPE), w[:, 640:672], z(HEAD_PAD - MLA_QK), w[:, 672:]], axis=1)


def _unlayout_w_in(g):
    return jnp.concatenate([g[:, :640], g[:, 640 + MLA_NOPE:640 + MLA_QK], g[:, 768:]], axis=1)


def _pad_heads(v):
    lead = v.shape[:-1]
    return jnp.pad(v.reshape(lead + (HEADS, MLA_QK)), [(0, 0)] * len(lead) + [(0, 0), (0, HEAD_PAD - MLA_QK)]).reshape(
        lead + (HEADS * HEAD_PAD,))


def _halves(w):
    n = w.shape[1] // 2
    return [w[:, :n], w[:, n:]]


def _local_step(x, p, pos, tgt, gains, wts):
    d = D_MODEL
    inv_freq = ROPE_BASE ** (-jnp.arange(0, MLA_ROPE, 2, dtype=F32) / MLA_ROPE)
    zeros = lambda n: jnp.zeros((n,), F32)
    freq = jnp.concatenate([zeros(MLA_NOPE), inv_freq, inv_freq, zeros(HEAD_PAD - MLA_QK)])[None]
    sign = jnp.concatenate([zeros(MLA_NOPE), -jnp.ones((16,), F32), jnp.ones((16,), F32), zeros(HEAD_PAD - MLA_QK)])[None]
    pad_gain = lambda g: jnp.pad(g, ((0, 0), (0, HEAD_PAD - MLA_QK)))
    g_qh, g_kh = pad_gain(gains["q_head_norm"]), pad_gain(gains["k_head_norm"])
    w_in = _relayout_w_in(wts["w_in"])
    wq = _pad_heads(wts["w_q_up"])
    wkv = wts["w_kv_up"]
    wbm = jnp.pad(wts["w_branch_mla"].reshape(HEADS, 64, d), ((0, 0), (64, 0), (0, 0))).reshape(HEADS * HEAD_PAD, d)
    wbs, wo = wts["w_branch_sb"], wts["w_out"]

    u1 = _norm_call(x, gains["ffn1_norm"], "norm_ffn1")
    a1, b1, hm1 = _ffn_in_call(u1, wts["ffn1_w_in"], "ffn1_in")
    h1 = _ffn_out_call(hm1, wts["ffn1_w_out"], x, "ffn1_out")
    um = _norm_call(h1, gains["mix_norm"], "norm_mix")
    cq, ckv, krope, sbq, sbk, sbv, gates = _mix_in_call(um, w_in, "mix_in")
    prep_args = (cq, ckv, krope, pos, freq, sign, gains["q_latent_norm"], gains["kv_latent_norm"], g_qh, g_kh, wq, wkv)
    q, k, v = _mla_prep_call(*prep_args, "mla_prep")
    om, lse = _mla_fwd_call(q, k, v, "mla_fwd")
    osb, tot = _sb_fwd_call(sbq, sbk, sbv, "sb_fwd")
    h2, bm, bs, mg = _merge_out_call(om, osb, gates, h1, wbm, wbs, wo, "merge_out")
    u2 = _norm_call(h2, gains["ffn2_norm"], "norm_ffn2")
    a2, b2, hm2 = _ffn_in_call(u2, wts["ffn2_w_in"], "ffn2_in")
    h3 = _ffn_out_call(hm2, wts["ffn2_w_out"], h2, "ffn2_out")

    grads = {}
    dh3, dh3s, un, dgl, dpp, grads["ple_norm"], sq = _ple_call(
        h3, gains["ple_norm"], wts["w_ple_gate"], p, wts["w_ple_proj"], tgt, "ple")
    grads["w_ple_gate"] = _tn_call(un, dgl, "dw_ple_gate")
    grads["w_ple_proj"] = _tn_call(p, dpp, "dw_ple_proj")

    da2, db2 = _ffn_bwd_a_call(dh3s, a2, b2, wts["ffn2_w_out"], "ffn2_bwd_act")
    grads["ffn2_w_out"] = _tn_call(hm2, dh3s, "dw_ffn2_out")
    grads["ffn2_w_in"] = jnp.concatenate([_tn_call(u2, da2, "dw_ffn2_in_a", shard_cols=D_FF // 2),
                                          _tn_call(u2, db2, "dw_ffn2_in_b", shard_cols=D_FF // 2)], axis=0)
    dh2, dh2b, grads["ffn2_norm"] = _norm_bwd_call([da2, db2], _halves(wts["ffn2_w_in"]), h2, gains["ffn2_norm"], dh3,
                                                   "ffn2_bwd_norm", half_out=False)

    dgates, dbm, dbs, dom, dos = _merge_bwd_call(dh2b, gates, bm, bs, wo, wbm, wbs, "merge_bwd")
    grads["w_out"] = _tn_call(mg, dh2b, "dw_out")
    grads["w_branch_mla"] = _tn_call(om, dbm, "dw_branch_mla").reshape(HEADS, HEAD_PAD, d)[:, 64:, :].reshape(512, d)
    grads["w_branch_sb"] = _tn_call(osb, dbs, "dw_branch_sb")
    dq, dk, dv = _mla_bwd_call(q, k, v, om, dom, lse, "mla_bwd")
    dsq, dsk, dsv = _sb_bwd_call(sbq, sbk, sbv, dos, tot, "sb_bwd")
    (dcq, dckv, dkr, dwq, grads["w_kv_up"], grads["q_latent_norm"], grads["kv_latent_norm"], dgqh, dgkh) = \
        _mla_prep_bwd_call(*prep_args, dq, dk, dv, "mla_prep_bwd")
    grads["w_q_up"] = dwq.reshape(Q_LORA, HEADS, HEAD_PAD)[:, :, :MLA_QK].reshape(Q_LORA, HEADS * MLA_QK)
    grads["q_head_norm"], grads["k_head_norm"] = dgqh[:, :MLA_QK], dgkh[:, :MLA_QK]
    dproj = jnp.concatenate([dcq, dckv, dkr, dsq, dsk.astype(BF16), dsv.astype(BF16), dgates], axis=1)
    grads["w_in"] = _unlayout_w_in(_tn_call(um, dproj, "dw_in"))
    dh1, dh1s, grads["mix_norm"] = _norm_bwd_call([dproj], [w_in], h1, gains["mix_norm"], dh2, "mix_bwd_norm",
                                                  half_out=True)

    da1, db1 = _ffn_bwd_a_call(dh1s, a1, b1, wts["ffn1_w_out"], "ffn1_bwd_act")
    grads["ffn1_w_out"] = _tn_call(hm1, dh1s, "dw_ffn1_out")
    grads["ffn1_w_in"] = jnp.concatenate([_tn_call(u1, da1, "dw_ffn1_in_a", shard_cols=D_FF // 2),
                                          _tn_call(u1, db1, "dw_ffn1_in_b", shard_cols=D_FF // 2)], axis=0)
    dx, _, grads["ffn1_norm"] = _norm_bwd_call([da1, db1], _halves(wts["ffn1_w_in"]), x, gains["ffn1_norm"], dh1,
                                               "ffn1_bwd_norm", half_out=False)
    return sq, dx, grads


def kernel(x, p, positions, ffn1_norm, ffn1_w_in, ffn1_w_out, mix_norm, w_in, q_latent_norm, w_q_up, kv_latent_norm, w_kv_up, q_head_norm, k_head_norm, w_branch_mla, w_branch_sb, w_out, ffn2_norm, ffn2_w_in, ffn2_w_out, ple_norm, w_ple_gate, w_ple_proj, loss_target, m_ffn1_norm, m_ffn1_w_in, m_ffn1_w_out, m_mix_norm, m_w_in, m_q_latent_norm, m_w_q_up, m_kv_latent_norm, m_w_kv_up, m_q_head_norm, m_k_head_norm, m_w_branch_mla, m_w_branch_sb, m_w_out, m_ffn2_norm, m_ffn2_w_in, m_ffn2_w_out, m_ple_norm, m_w_ple_gate, m_w_ple_proj, v_ffn1_norm, v_ffn1_w_in, v_ffn1_w_out, v_mix_norm, v_w_in, v_q_latent_norm, v_w_q_up, v_kv_latent_norm, v_w_kv_up, v_q_head_norm, v_k_head_norm, v_w_branch_mla, v_w_branch_sb, v_w_out, v_ffn2_norm, v_ffn2_w_in, v_ffn2_w_out, v_ple_norm, v_w_ple_gate, v_w_ple_proj):
    given = dict(locals())
    w_shard = {n: given[n][0] for n in WEIGHT_ORDER}
    m_shard = {n: given["m_" + n][0] for n in WEIGHT_ORDER}
    v_shard = {n: given["v_" + n][0] for n in WEIGHT_ORDER}
    gains = {n: w_shard[n][None] for n in GAINS}

    c = lax.axis_index("c")
    chip = 2 * lax.axis_index("x") + lax.axis_index("y")
    mine = _join_parts({n: w_shard[n].astype(BF16) for n in BIG})
    others = _gather_call(mine, "gather_weights")
    gathered = _split_parts([lax.dynamic_update_slice_in_dim(o, m[None], chip, axis=0) for o, m in zip(others, mine)])
    wts = {n: _from_shards(n, gathered[n]) for n in BIG}

    sq, dx, grads = _local_step(x[0], p[0, 0], positions.reshape(-1, 1), loss_target[0], gains, wts)

    partial = _join_parts({n: grads[n] if grads[n].ndim == 3 else _to_shards(n, grads[n].astype(BF16)) for n in BIG})
    from_sibling = _pair_send_call(partial, "grads_pair_send")
    pair_sum = [_pair_sum_call(a, b, c, BF16, "grads_pair_sum_%d" % k) for k, (a, b) in enumerate(zip(partial, from_sibling))]
    by_chip = _chip_scatter_call(pair_sum, "grads_chip_scatter")
    by_chip = [lax.dynamic_update_slice_in_dim(t, lax.dynamic_slice_in_dim(o, chip, 1, axis=0), chip, axis=0)
               for t, o in zip(by_chip, pair_sum)]
    reduced = _pair_swap_call([_chip_sum_call(t, c, "grads_chip_sum_%d" % k) for k, t in enumerate(by_chip)],
                              "grads_pair_swap")

    rows = [jnp.pad(grads[n], ((0, 0), (0, D_MODEL - GAINS[n]))) for n in GAINS] + [sq]
    gain_block = jnp.concatenate(rows + [jnp.zeros((16 - len(rows), D_MODEL), F32)], axis=0)
    gain_sum = _sum_call(_all_gather_small_call(gain_block, "gains_all_gather"), F32, "gains_sum")
    loss = 0.5 * jnp.sum(gain_sum[len(GAINS)]) / D_MODEL

    outs = {"grad": {}, "delta": {}, "new_m": {}, "new_v": {}}
    gain_pack = lambda t: jnp.concatenate([jnp.pad(t[n][None], ((0, 0), (0, D_MODEL - GAINS[n]))) for n in GAINS], axis=0)
    packed = _adamw_call(gain_pack(w_shard), gain_sum, 0, gain_pack(m_shard), gain_pack(v_shard), "adamw_gains")
    for i, n in enumerate(GAINS):
        for kind, t in zip(("grad", "delta", "new_m", "new_v"), packed):
            outs[kind][n] = t[i, :GAINS[n]][None]
    where = _part_rows()
    for n in BIG:
        k, row0 = where[n]
        for kind, t in zip(("grad", "delta", "new_m", "new_v"),
                           _adamw_call(w_shard[n], reduced[k], row0, m_shard[n], v_shard[n], "adamw_" + n)):
            outs[kind][n] = t[None]

    return (loss, dx[None], *[outs["grad"][n] for n in WEIGHT_ORDER], *[outs["delta"][n] for n in WEIGHT_ORDER],
            *[outs["new_m"][n] for n in WEIGHT_ORDER], *[outs["new_v"][n] for n in WEIGHT_ORDER])
```

```python
import functools
import math

import jax
import jax.numpy as jnp
from jax import lax
from jax.experimental import pallas as pl
from jax.experimental.pallas import tpu as pltpu

F32 = jnp.float32
BF16 = jnp.bfloat16
MESH = pl.DeviceIdType.MESH

D_MODEL = 1024
D_FF = 2816
PLE_DIM = 256
NORM_EPS = 1e-6
HEADS = 8
MLA_NOPE = 64
MLA_ROPE = 32
MLA_QK = 96
Q_LORA = 384
KV_LORA = 256
SB_WIDTH = 512
ROPE_BASE = 10000.0
HEAD_PAD = 128
N_CHIPS = 4

ADAM_LR = 0.001
ADAM_B1 = 0.9
ADAM_B2 = 0.999
ADAM_EPS = 1e-08
ADAM_WD = 0.01
ADAM_STEP = 10

SEG_CQ = (0, 384)
SEG_CKV = (384, 256)
SEG_KROPE = (640, 128)
SEG_SBQ = (768, 512)
SEG_SBK = (1280, 512)
SEG_SBV = (1792, 512)
SEG_GATES = (2304, 2048)
IN_COLS_PAD = 4352

TM = 512
TM_SMALL = 256
TQ = 256
MLA_FWD_BLOCKS = 4
MLA_BWD_BLOCKS = 4
SB_FWD_BLOCKS = 4
SB_BWD_BLOCKS = 2
SB_HEAD = 64
SB_SCALE = 0.125
TN_MAX_COLS = 2816
FLAT_TILES = (256, 192, 64, 8)
VMEM_LIMIT = 56 * 1024 * 1024

NT = (((1,), (1,)), ((), ()))
TN = (((0,), (0,)), ((), ()))


def _cp(sem):
    return pltpu.CompilerParams(dimension_semantics=sem, vmem_limit_bytes=VMEM_LIMIT)


def _rows(tm, w):
    return pl.BlockSpec((tm, w), lambda i: (i, 0))


def _whole(shape):
    return pl.BlockSpec(shape, lambda i: (0,) * len(shape))


def _dot(a, b):
    return jnp.dot(a, b, preferred_element_type=F32)


def _dot_nt(a, b):
    return lax.dot_general(a, b, NT, preferred_element_type=F32)


def _dot_tn(a, b):
    return lax.dot_general(a, b, TN, preferred_element_type=F32)


def _rstd(x, n):
    return lax.rsqrt(jnp.sum(x * x, axis=-1, keepdims=True) / n + NORM_EPS)


def _rms_bwd(x, r, g, dy, n):
    gy = dy * g
    return r * gy - x * ((r * r * r) * (jnp.sum(x * gy, axis=-1, keepdims=True) / n))


def _sigmoid(x):
    return jax.nn.sigmoid(x)


def _pick(n, cands):
    for c in cands:
        if n % c == 0:
            return c
    return n


def _norm_call(h, g, name):
    s, d = h.shape
    tm = min(TM, s)

    def body(h_ref, g_ref, u_ref):
        x = h_ref[...]
        u_ref[...] = ((x * _rstd(x, d)) * g_ref[...]).astype(BF16)

    return pl.pallas_call(
        body, name=name, grid=(s // tm,),
        in_specs=[_rows(tm, d), _whole((1, d))], out_specs=_rows(tm, d),
        out_shape=jax.ShapeDtypeStruct((s, d), BF16), compiler_params=_cp(("parallel",)))(h, g)


def _ffn_in_call(u, w, name):
    s, d = u.shape
    n = w.shape[1] // 2
    tn = n // 2
    tm = min(TM, s)
    nj = n // tn

    def body(u_ref, wa_ref, wb_ref, a_ref, b_ref, hm_ref):
        uu = u_ref[...]
        a = _dot(uu, wa_ref[...])
        b = _dot(uu, wb_ref[...])
        a_ref[...] = a
        b_ref[...] = b
        hm_ref[...] = ((a * _sigmoid(a)) * b).astype(BF16)

    blk = pl.BlockSpec((tm, tn), lambda j, i: (i, j))
    return pl.pallas_call(
        body, name=name, grid=(nj, s // tm),
        in_specs=[pl.BlockSpec((tm, d), lambda j, i: (i, 0)),
                  pl.BlockSpec((d, tn), lambda j, i: (0, j)),
                  pl.BlockSpec((d, tn), lambda j, i: (0, j + nj))],
        out_specs=[blk, blk, blk],
        out_shape=[jax.ShapeDtypeStruct((s, n), F32), jax.ShapeDtypeStruct((s, n), F32),
                   jax.ShapeDtypeStruct((s, n), BF16)],
        compiler_params=_cp(("parallel", "parallel")))(u, w, w)


def _ffn_out_call(hm, w, h, name):
    s, n = hm.shape
    d = w.shape[1]
    tm = min(TM, s)

    def body(hm_ref, w_ref, h_ref, o_ref):
        o_ref[...] = h_ref[...] + 0.5 * _dot(hm_ref[...], w_ref[...])

    return pl.pallas_call(
        body, name=name, grid=(s // tm,),
        in_specs=[_rows(tm, n), _whole((n, d)), _rows(tm, d)], out_specs=_rows(tm, d),
        out_shape=jax.ShapeDtypeStruct((s, d), F32), compiler_params=_cp(("parallel",)))(hm, w, h)


def _mix_in_call(u, w, name):
    s, d = u.shape
    tm = min(TM_SMALL, s)
    segs = [(SEG_CQ, F32), (SEG_CKV, F32), (SEG_KROPE, F32), (SEG_SBQ, BF16), (SEG_SBK, BF16),
            (SEG_SBV, BF16), (SEG_GATES, F32)]

    def body(u_ref, w_ref, *outs):
        uu = u_ref[...]
        for ((off, width), _), o_ref in zip(segs, outs):
            o_ref[...] = _dot(uu, w_ref[:, off:off + width]).astype(o_ref.dtype)

    return pl.pallas_call(
        body, name=name, grid=(s // tm,),
        in_specs=[_rows(tm, d), _whole((d, IN_COLS_PAD))],
        out_specs=[_rows(tm, width) for (_, width), _ in segs],
        out_shape=[jax.ShapeDtypeStruct((s, width), dt) for (_, width), dt in segs],
        compiler_params=_cp(("parallel",)))(u, w)


def _lane(shape):
    return lax.broadcasted_iota(jnp.int32, shape, len(shape) - 1)


def _rot_half(y):
    lane = _lane(y.shape)
    swapped = jnp.where(lane < MLA_NOPE + MLA_ROPE // 2, pltpu.roll(y, HEAD_PAD - 16, 1), pltpu.roll(y, 16, 1))
    return jnp.where((lane >= MLA_NOPE) & (lane < MLA_QK), swapped, 0.0)


def _rope_tables(pos_ref, freq_ref, sign_ref):
    ang = pos_ref[...].astype(F32) * freq_ref[...]
    return jnp.cos(ang), jnp.sin(ang) * sign_ref[...]


def _head_fwd(x, g, cosv, ssv):
    r = _rstd(x, MLA_QK)
    y = (x * r) * g
    return y * cosv + _rot_half(y) * ssv, r


def _head_bwd(x, r, g, cosv, ssv, dout):
    dy = dout * cosv + _rot_half(dout * ssv)
    return _rms_bwd(x, r, g, dy, MLA_QK), jnp.sum(dy * (x * r), axis=0, keepdims=True)


def _mla_prep_call(cq, ckv, krope, pos, freq, sign, g_ql, g_kvl, g_qh, g_kh, wq, wkv, name):
    s = cq.shape[0]
    tm = min(TM_SMALL, s)
    width = HEADS * HEAD_PAD

    def body(cq_ref, ckv_ref, kr_ref, pos_ref, freq_ref, sign_ref, gql_ref, gkvl_ref, gqh_ref, gkh_ref,
             wq_ref, wkv_ref, q_ref, k_ref, v_ref):
        cosv, ssv = _rope_tables(pos_ref, freq_ref, sign_ref)
        x = cq_ref[...]
        qr = _dot(((x * _rstd(x, Q_LORA)) * gql_ref[...]).astype(BF16), wq_ref[...])
        x = ckv_ref[...]
        kv = _dot(((x * _rstd(x, KV_LORA)) * gkvl_ref[...]).astype(BF16), wkv_ref[...])
        kr = kr_ref[...]
        lane = _lane((tm, HEAD_PAD))
        for h in range(HEADS):
            sl = slice(h * HEAD_PAD, (h + 1) * HEAD_PAD)
            qh, _ = _head_fwd(qr[:, sl], gqh_ref[...], cosv, ssv)
            q_ref[:, sl] = qh.astype(BF16)
            kvh = kv[:, sl]
            kh, _ = _head_fwd(jnp.where(lane < MLA_NOPE, kvh, kr), gkh_ref[...], cosv, ssv)
            k_ref[:, sl] = kh.astype(BF16)
            v_ref[:, sl] = jnp.where(lane >= MLA_NOPE, kvh, 0.0).astype(BF16)

    out = jax.ShapeDtypeStruct((s, width), BF16)
    return pl.pallas_call(
        body, name=name, grid=(s // tm,),
        in_specs=[_rows(tm, Q_LORA), _rows(tm, KV_LORA), _rows(tm, HEAD_PAD), _rows(tm, 1),
                  _whole((1, HEAD_PAD)), _whole((1, HEAD_PAD)), _whole((1, Q_LORA)), _whole((1, KV_LORA)),
                  _whole((1, HEAD_PAD)), _whole((1, HEAD_PAD)), _whole((Q_LORA, width)), _whole((KV_LORA, width))],
        out_specs=[_rows(tm, width)] * 3, out_shape=[out, out, out],
        compiler_params=_cp(("parallel",)))(cq, ckv, krope, pos, freq, sign, g_ql, g_kvl, g_qh, g_kh, wq, wkv)


def _attn_specs(s, nb):
    qspec = pl.BlockSpec((TQ, nb * HEAD_PAD), lambda g, i: (i, g))
    kspec = pl.BlockSpec((s, nb * HEAD_PAD), lambda g, i: (0, g))
    return qspec, kspec


def _lanes(b):
    return slice(b * HEAD_PAD, (b + 1) * HEAD_PAD)


def _tri(cmp):
    r = lax.broadcasted_iota(jnp.int32, (TQ, TQ), 0)
    c = lax.broadcasted_iota(jnp.int32, (TQ, TQ), 1)
    return cmp(r, c)


def _mla_fwd_call(q, k, v, name):
    s, width = q.shape
    scale = 1.0 / math.sqrt(MLA_QK)

    nb = MLA_FWD_BLOCKS

    def body(q_ref, k_ref, v_ref, o_ref, lse_ref):
        qi = pl.program_id(1)
        qs = [q_ref[:, _lanes(b)] for b in range(nb)]
        causal = _tri(lambda r, c: c <= r)

        def step(kb, carry, diag):
            ks = pl.multiple_of(kb * TQ, TQ)
            heads = range(nb)
            scs = [_dot_nt(qs[b], k_ref[pl.ds(ks, TQ), _lanes(b)]) * scale for b in heads]
            if diag:
                scs = [jnp.where(causal, sc, -1e30) for sc in scs]
            mns = [jnp.maximum(carry[b][0], jnp.max(scs[b], axis=-1, keepdims=True)) for b in heads]
            als = [jnp.exp(carry[b][0] - mns[b]) for b in heads]
            ps = [jnp.exp(scs[b] - mns[b]) for b in heads]
            ls = [als[b] * carry[b][1] + jnp.sum(ps[b], axis=-1, keepdims=True) for b in heads]
            accs = [als[b] * carry[b][2] + _dot(ps[b].astype(BF16), v_ref[pl.ds(ks, TQ), _lanes(b)]) for b in heads]
            return tuple((mns[b], ls[b], accs[b]) for b in heads)

        init = tuple((jnp.full((TQ, 1), -1e30, F32), jnp.zeros((TQ, 1), F32), jnp.zeros((TQ, HEAD_PAD), F32))
                     for _ in range(nb))
        carry = step(qi, init, True)
        carry = lax.fori_loop(0, qi, lambda kb, c: step(kb, c, False), carry)
        for b in range(nb):
            m, l, acc = carry[b]
            o_ref[:, _lanes(b)] = (acc / l).astype(BF16)
            lse_ref[:, _lanes(b)] = jnp.broadcast_to(m + jnp.log(l), (TQ, HEAD_PAD))

    qspec, kspec = _attn_specs(s, nb)
    return pl.pallas_call(
        body, name=name, grid=(width // (nb * HEAD_PAD), s // TQ),
        in_specs=[qspec, kspec, kspec], out_specs=[qspec, qspec],
        out_shape=[jax.ShapeDtypeStruct((s, width), BF16), jax.ShapeDtypeStruct((s, width), F32)],
        compiler_params=_cp(("parallel", "arbitrary")))(q, k, v)


def _mla_bwd_call(q, k, v, o, do, lse, name):
    s, width = q.shape
    scale = 1.0 / math.sqrt(MLA_QK)
    nb = MLA_BWD_BLOCKS

    def body(q_ref, k_ref, v_ref, o_ref, do_ref, lse_ref, dq_ref, dk_ref, dv_ref):
        qi = pl.program_id(1)

        @pl.when(qi == 0)
        def _():
            dk_ref[...] = jnp.zeros_like(dk_ref)
            dv_ref[...] = jnp.zeros_like(dv_ref)

        qs = [q_ref[:, _lanes(b)] for b in range(nb)]
        dos = [do_ref[:, _lanes(b)] for b in range(nb)]
        lses = [lse_ref[:, b * HEAD_PAD:b * HEAD_PAD + 1] for b in range(nb)]
        dlts = [jnp.sum(dos[b].astype(F32) * o_ref[:, _lanes(b)].astype(F32), axis=-1, keepdims=True) for b in range(nb)]
        causal = _tri(lambda r, c: c <= r)

        def step(kb, dqs, diag):
            ks = pl.multiple_of(kb * TQ, TQ)
            heads = range(nb)
            kts = [k_ref[pl.ds(ks, TQ), _lanes(b)] for b in heads]
            scs = [_dot_nt(qs[b], kts[b]) for b in heads]
            dps = [_dot_nt(dos[b], v_ref[pl.ds(ks, TQ), _lanes(b)]) for b in heads]
            ps = [jnp.exp(scs[b] * scale - lses[b]) for b in heads]
            if diag:
                ps = [jnp.where(causal, p, 0.0) for p in ps]
            dss = [(ps[b] * (dps[b] - dlts[b]) * scale).astype(BF16) for b in heads]
            dvs = [_dot_tn(ps[b].astype(BF16), dos[b]) for b in heads]
            dks = [_dot_tn(dss[b], qs[b]) for b in heads]
            out = tuple(dqs[b] + _dot(dss[b], kts[b]) for b in heads)
            for b in heads:
                dv_ref[pl.ds(ks, TQ), _lanes(b)] += dvs[b]
                dk_ref[pl.ds(ks, TQ), _lanes(b)] += dks[b]
            return out

        dqs = step(qi, tuple(jnp.zeros((TQ, HEAD_PAD), F32) for _ in range(nb)), True)
        dqs = lax.fori_loop(0, qi, lambda kb, c: step(kb, c, False), dqs)
        for b in range(nb):
            dq_ref[:, _lanes(b)] = dqs[b]

    qspec, kspec = _attn_specs(s, nb)
    out = jax.ShapeDtypeStruct((s, width), F32)
    return pl.pallas_call(
        body, name=name, grid=(width // (nb * HEAD_PAD), s // TQ),
        in_specs=[qspec, kspec, kspec, qspec, qspec, qspec], out_specs=[qspec, kspec, kspec],
        out_shape=[out, out, out],
        compiler_params=_cp(("parallel", "arbitrary")))(q, k, v, o, do, lse)


def _dot_hilo(x, u):
    hi = x.astype(BF16)
    lo = (x - hi.astype(F32)).astype(BF16)
    return _dot(hi, u) + _dot(lo, u)


def _sb_logs(z):
    ls = jnp.minimum(z, 0.0) - jnp.log(1.0 + jnp.exp(-jnp.abs(z)))
    return ls, ls - z


def _sb_head_q(qb, first, hh):
    keep = first if hh == 0 else jnp.logical_not(first)
    return jnp.where(keep, qb, jnp.zeros_like(qb)) * jnp.asarray(SB_SCALE, qb.dtype)


def _sb_fwd_call(q, k, v, name):
    s, width = q.shape
    nb = SB_FWD_BLOCKS
    chains = [(b, hh) for b in range(nb) for hh in range(HEAD_PAD // SB_HEAD)]

    def body(q_ref, k_ref, v_ref, o_ref, t_ref):
        qi = pl.program_id(1)
        strict = _tri(lambda r, c: c < r)
        after = _tri(lambda r, c: r > c).astype(BF16)
        first = _lane((1, HEAD_PAD)) < SB_HEAD
        qhs = [_sb_head_q(q_ref[:, _lanes(b)], first, hh) for b, hh in chains]

        def step(kb, carry, diag):
            ks = pl.multiple_of(kb * TQ, TQ)
            ids = range(len(chains))
            zs = [_dot_nt(qhs[ci], k_ref[pl.ds(ks, TQ), _lanes(chains[ci][0])]) for ci in ids]
            logs = [_sb_logs(z) for z in zs]
            lss = [lg[0] for lg in logs]
            l1ms = [jnp.where(strict, lg[1], 0.0) if diag else lg[1] for lg in logs]
            sufs = [_dot_hilo(l1m, after) for l1m in l1ms]
            as_ = [jnp.exp(lss[ci] + sufs[ci] + carry[ci][0]) for ci in ids]
            if diag:
                as_ = [jnp.where(strict, a, 0.0) for a in as_]
            accs = [carry[ci][1] + _dot(as_[ci].astype(BF16), v_ref[pl.ds(ks, TQ), _lanes(chains[ci][0])]) for ci in ids]
            return tuple((carry[ci][0] + jnp.sum(l1ms[ci], axis=-1, keepdims=True), accs[ci]) for ci in ids)

        init = tuple((jnp.zeros((TQ, 1), F32), jnp.zeros((TQ, HEAD_PAD), F32)) for _ in chains)
        carry = step(qi, init, True)
        carry = lax.fori_loop(0, qi, lambda j, c: step(qi - 1 - j, c, False), carry)
        for b in range(nb):
            (cs0, acc0), (cs1, acc1) = carry[2 * b], carry[2 * b + 1]
            o_ref[:, _lanes(b)] = jnp.where(first, acc0, acc1).astype(BF16)
            t_ref[:, _lanes(b)] = jnp.where(first, cs0, cs1)

    qspec, kspec = _attn_specs(s, nb)
    return pl.pallas_call(
        body, name=name, grid=(width // (nb * HEAD_PAD), s // TQ),
        in_specs=[qspec, kspec, kspec], out_specs=[qspec, qspec],
        out_shape=[jax.ShapeDtypeStruct((s, width), BF16), jax.ShapeDtypeStruct((s, width), F32)],
        compiler_params=_cp(("parallel", "arbitrary")))(q, k, v)


def _sb_bwd_call(q, k, v, do, tot, name):
    s, width = q.shape
    nb = SB_BWD_BLOCKS
    chains = [(b, hh) for b in range(nb) for hh in range(HEAD_PAD // SB_HEAD)]

    def body(q_ref, k_ref, v_ref, do_ref, t_ref, dq_ref, dk_ref, dv_ref):
        qi = pl.program_id(1)

        @pl.when(qi == 0)
        def _():
            dk_ref[...] = jnp.zeros_like(dk_ref)
            dv_ref[...] = jnp.zeros_like(dv_ref)

        strict = _tri(lambda r, c: c < r)
        upto = _tri(lambda r, c: r <= c).astype(BF16)
        before = _tri(lambda r, c: r < c).astype(BF16)
        first = _lane((1, HEAD_PAD)) < SB_HEAD
        qhs = [_sb_head_q(q_ref[:, _lanes(b)], first, hh) for b, hh in chains]
        dohs = []
        for b, hh in chains:
            dob = do_ref[:, _lanes(b)]
            dohs.append(jnp.where(first if hh == 0 else jnp.logical_not(first), dob, jnp.zeros_like(dob)))
        tts = [t_ref[:, b * HEAD_PAD + hh * SB_HEAD:b * HEAD_PAD + hh * SB_HEAD + 1] for b, hh in chains]

        def step(kb, carry, diag):
            ks = pl.multiple_of(kb * TQ, TQ)
            ids = range(len(chains))
            kts = [k_ref[pl.ds(ks, TQ), _lanes(b)] for b, _ in chains]
            zs = [_dot_nt(qhs[ci], kts[ci]) for ci in ids]
            das = [_dot_nt(dohs[ci], v_ref[pl.ds(ks, TQ), _lanes(chains[ci][0])]) for ci in ids]
            logs = [_sb_logs(z) for z in zs]
            lss = [lg[0] for lg in logs]
            l1ms = [jnp.where(strict, lg[1], 0.0) if diag else lg[1] for lg in logs]
            pins = [_dot_hilo(l1m, upto) for l1m in l1ms]
            as_ = [jnp.exp(lss[ci] + (tts[ci] - carry[ci][0] - pins[ci])) for ci in ids]
            if diag:
                as_ = [jnp.where(strict, a, 0.0) for a in as_]
            gs = [as_[ci] * das[ci] for ci in ids]
            cexs = [carry[ci][1] + _dot_hilo(gs[ci], before) for ci in ids]
            dzs = [gs[ci] - jnp.exp(lss[ci]) * (gs[ci] + cexs[ci]) for ci in ids]
            if diag:
                dzs = [jnp.where(strict, dz, 0.0) for dz in dzs]
            dzbs = [dz.astype(BF16) for dz in dzs]
            dvps = [_dot_tn(as_[ci].astype(BF16), dohs[ci]) for ci in ids]
            dkps = [_dot_tn(dzbs[ci], qhs[ci]) for ci in ids]
            out = tuple((carry[ci][0] + jnp.sum(l1ms[ci], axis=-1, keepdims=True),
                         carry[ci][1] + jnp.sum(gs[ci], axis=-1, keepdims=True),
                         carry[ci][2] + _dot(dzbs[ci], kts[ci])) for ci in ids)
            for b in range(nb):
                dk_ref[pl.ds(ks, TQ), _lanes(b)] += dkps[2 * b] + dkps[2 * b + 1]
                dv_ref[pl.ds(ks, TQ), _lanes(b)] += dvps[2 * b] + dvps[2 * b + 1]
            return out

        init = tuple((jnp.zeros((TQ, 1), F32), jnp.zeros((TQ, 1), F32), jnp.zeros((TQ, HEAD_PAD), F32)) for _ in chains)
        carry = lax.fori_loop(0, qi, lambda kb, c: step(kb, c, False), init)
        carry = step(qi, carry, True)
        for b in range(nb):
            dq_ref[:, _lanes(b)] = (jnp.where(first, carry[2 * b][2], carry[2 * b + 1][2]) * SB_SCALE).astype(BF16)

    qspec, kspec = _attn_specs(s, nb)
    return pl.pallas_call(
        body, name=name, grid=(width // (nb * HEAD_PAD), s // TQ),
        in_specs=[qspec, kspec, kspec, qspec, qspec], out_specs=[qspec, kspec, kspec],
        out_shape=[jax.ShapeDtypeStruct((s, width), BF16), jax.ShapeDtypeStruct((s, width), F32),
                   jax.ShapeDtypeStruct((s, width), F32)],
        compiler_params=_cp(("parallel", "arbitrary")))(q, k, v, do, tot)


def _merge_out_call(om, osb, gates, h, wbm, wbs, wo, name):
    s, d = h.shape
    tm = min(TM_SMALL, s)

    def body(om_ref, os_ref, g_ref, h_ref, wbm_ref, wbs_ref, wo_ref, h2_ref, bm_ref, bs_ref, mg_ref):
        bm = _dot(om_ref[...], wbm_ref[...])
        bs = _dot(os_ref[...], wbs_ref[...])
        mg = (_sigmoid(g_ref[:, :d]) * bm + _sigmoid(g_ref[:, d:]) * bs).astype(BF16)
        bm_ref[...] = bm
        bs_ref[...] = bs
        mg_ref[...] = mg
        h2_ref[...] = h_ref[...] + _dot(mg, wo_ref[...])

    return pl.pallas_call(
        body, name=name, grid=(s // tm,),
        in_specs=[_rows(tm, om.shape[1]), _rows(tm, SB_WIDTH), _rows(tm, 2 * d), _rows(tm, d),
                  _whole(wbm.shape), _whole(wbs.shape), _whole(wo.shape)],
        out_specs=[_rows(tm, d)] * 4,
        out_shape=[jax.ShapeDtypeStruct((s, d), F32), jax.ShapeDtypeStruct((s, d), F32),
                   jax.ShapeDtypeStruct((s, d), F32), jax.ShapeDtypeStruct((s, d), BF16)],
        compiler_params=_cp(("parallel",)))(om, osb, gates, h, wbm, wbs, wo)


def _ple_call(h, g, wg, p, wp, tgt, name):
    s, d = h.shape
    tm = min(TM_SMALL, s)

    def body(h_ref, g_ref, wg_ref, p_ref, wp_ref, t_ref, dh_ref, dhs_ref, un_ref, dgl_ref, dpp_ref, dg_ref, sq_ref):
        @pl.when(pl.program_id(0) == 0)
        def _():
            dg_ref[...] = jnp.zeros_like(dg_ref)
            sq_ref[...] = jnp.zeros_like(sq_ref)

        x = h_ref[...]
        gain = g_ref[...]
        r = _rstd(x, d)
        xh = x * r
        un = (xh * gain).astype(BF16)
        sg = _sigmoid(_dot(un, wg_ref[...]))
        pp = _dot(p_ref[...].astype(BF16), wp_ref[...])
        diff = (x + sg * pp) - t_ref[...]
        sq_ref[...] += jnp.sum(diff * diff, axis=0, keepdims=True)
        dy = diff * (1.0 / d)
        dgl = ((dy * pp) * (sg * (1.0 - sg))).astype(BF16)
        dun = _dot_nt(dgl, wg_ref[...])
        dg_ref[...] += jnp.sum(dun * xh, axis=0, keepdims=True)
        dh = dy + _rms_bwd(x, r, gain, dun, d)
        dh_ref[...] = dh
        dhs_ref[...] = (0.5 * dh).astype(BF16)
        un_ref[...] = un
        dgl_ref[...] = dgl
        dpp_ref[...] = (dy * sg).astype(BF16)

    bf = jax.ShapeDtypeStruct((s, d), BF16)
    vec = jax.ShapeDtypeStruct((1, d), F32)
    return pl.pallas_call(
        body, name=name, grid=(s // tm,),
        in_specs=[_rows(tm, d), _whole((1, d)), _whole(wg.shape), _rows(tm, PLE_DIM), _whole(wp.shape), _rows(tm, d)],
        out_specs=[_rows(tm, d)] * 5 + [_whole((1, d))] * 2,
        out_shape=[jax.ShapeDtypeStruct((s, d), F32), bf, bf, bf, bf, vec, vec],
        compiler_params=_cp(("arbitrary",)))(h, g, wg, p, wp, tgt)


def _ffn_bwd_a_call(dhs, a, b, wo, name):
    s, n = a.shape
    d = dhs.shape[1]
    tn = n // 2
    tm = min(TM, s)

    def body(dh_ref, a_ref, b_ref, wo_ref, da_ref, db_ref):
        dhm = _dot_nt(dh_ref[...], wo_ref[...])
        av = a_ref[...]
        sa = _sigmoid(av)
        da_ref[...] = (dhm * b_ref[...] * (sa * (1.0 + av * (1.0 - sa)))).astype(BF16)
        db_ref[...] = (dhm * (av * sa)).astype(BF16)

    blk = pl.BlockSpec((tm, tn), lambda j, i: (i, j))
    return pl.pallas_call(
        body, name=name, grid=(n // tn, s // tm),
        in_specs=[pl.BlockSpec((tm, d), lambda j, i: (i, 0)), blk, blk, pl.BlockSpec((tn, d), lambda j, i: (j, 0))],
        out_specs=[blk, blk],
        out_shape=[jax.ShapeDtypeStruct((s, n), BF16)] * 2,
        compiler_params=_cp(("parallel", "parallel")))(dhs, a, b, wo)


def _norm_bwd_call(dy_list, w_list, h, g, dh_in, name, half_out):
    s, d = h.shape
    tm = min(TM_SMALL, s)
    nk = len(dy_list)
    factor = 0.5 if half_out else 1.0

    def body(*refs):
        dy_refs = refs[:nk]
        w_refs = refs[nk:2 * nk]
        h_ref, g_ref, dhin_ref, dh_ref, dhb_ref, dg_ref = refs[2 * nk:]

        @pl.when(pl.program_id(0) == 0)
        def _():
            dg_ref[...] = jnp.zeros_like(dg_ref)

        du = _dot_nt(dy_refs[0][...], w_refs[0][...])
        for dy_ref, w_ref in zip(dy_refs[1:], w_refs[1:]):
            du = du + _dot_nt(dy_ref[...], w_ref[...])
        x = h_ref[...]
        r = _rstd(x, d)
        dg_ref[...] += jnp.sum(du * (x * r), axis=0, keepdims=True)
        dh = dhin_ref[...] + _rms_bwd(x, r, g_ref[...], du, d)
        dh_ref[...] = dh
        dhb_ref[...] = (factor * dh).astype(BF16)

    return pl.pallas_call(
        body, name=name, grid=(s // tm,),
        in_specs=[_rows(tm, dy.shape[1]) for dy in dy_list] + [_whole(w.shape) for w in w_list]
        + [_rows(tm, d), _whole((1, d)), _rows(tm, d)],
        out_specs=[_rows(tm, d), _rows(tm, d), _whole((1, d))],
        out_shape=[jax.ShapeDtypeStruct((s, d), F32), jax.ShapeDtypeStruct((s, d), BF16),
                   jax.ShapeDtypeStruct((1, d), F32)],
        compiler_params=_cp(("arbitrary",)))(*dy_list, *w_list, h, g, dh_in)


def _merge_bwd_call(dhb, gates, bm, bs, wo, wbm, wbs, name):
    s, d = bm.shape
    tm = min(TM_SMALL, s)

    def body(dh_ref, g_ref, bm_ref, bs_ref, wo_ref, wbm_ref, wbs_ref, dg_ref, dbm_ref, dbs_ref, dom_ref, dos_ref):
        dmg = _dot_nt(dh_ref[...], wo_ref[...])
        s1 = _sigmoid(g_ref[:, :d])
        s2 = _sigmoid(g_ref[:, d:])
        dg_ref[:, :d] = (dmg * bm_ref[...] * (s1 * (1.0 - s1))).astype(BF16)
        dg_ref[:, d:] = (dmg * bs_ref[...] * (s2 * (1.0 - s2))).astype(BF16)
        dbm = (dmg * s1).astype(BF16)
        dbs = (dmg * s2).astype(BF16)
        dbm_ref[...] = dbm
        dbs_ref[...] = dbs
        dom_ref[...] = _dot_nt(dbm, wbm_ref[...]).astype(BF16)
        dos_ref[...] = _dot_nt(dbs, wbs_ref[...]).astype(BF16)

    wm = wbm.shape[0]
    return pl.pallas_call(
        body, name=name, grid=(s // tm,),
        in_specs=[_rows(tm, d), _rows(tm, 2 * d), _rows(tm, d), _rows(tm, d),
                  _whole(wo.shape), _whole(wbm.shape), _whole(wbs.shape)],
        out_specs=[_rows(tm, 2 * d), _rows(tm, d), _rows(tm, d), _rows(tm, wm), _rows(tm, SB_WIDTH)],
        out_shape=[jax.ShapeDtypeStruct((s, 2 * d), BF16), jax.ShapeDtypeStruct((s, d), BF16),
                   jax.ShapeDtypeStruct((s, d), BF16), jax.ShapeDtypeStruct((s, wm), BF16),
                   jax.ShapeDtypeStruct((s, SB_WIDTH), BF16)],
        compiler_params=_cp(("parallel",)))(dhb, gates, bm, bs, wo, wbm, wbs)


def _mla_prep_bwd_call(cq, ckv, krope, pos, freq, sign, g_ql, g_kvl, g_qh, g_kh, wq, wkv, dq, dk, dv, name):
    s = cq.shape[0]
    tm = min(TM_SMALL, s)
    width = HEADS * HEAD_PAD

    def body(cq_ref, ckv_ref, kr_ref, pos_ref, freq_ref, sign_ref, gql_ref, gkvl_ref, gqh_ref, gkh_ref,
             wq_ref, wkv_ref, dq_ref, dk_ref, dv_ref,
             dcq_ref, dckv_ref, dkr_ref, dwq_ref, dwkv_ref, dgql_ref, dgkvl_ref, dgqh_ref, dgkh_ref, dqr_ref, dkv_ref):
        @pl.when(pl.program_id(0) == 0)
        def _():
            for ref in (dwq_ref, dwkv_ref, dgql_ref, dgkvl_ref, dgqh_ref, dgkh_ref):
                ref[...] = jnp.zeros_like(ref)

        cosv, ssv = _rope_tables(pos_ref, freq_ref, sign_ref)
        xq = cq_ref[...]
        rq = _rstd(xq, Q_LORA)
        cqn = ((xq * rq) * gql_ref[...]).astype(BF16)
        qr = _dot(cqn, wq_ref[...])
        xk = ckv_ref[...]
        rk = _rstd(xk, KV_LORA)
        ckvn = ((xk * rk) * gkvl_ref[...]).astype(BF16)
        kv = _dot(ckvn, wkv_ref[...])
        kr = kr_ref[...]
        lane = _lane((tm, HEAD_PAD))
        dkr = jnp.zeros((tm, HEAD_PAD), F32)
        dgqh = jnp.zeros((1, HEAD_PAD), F32)
        dgkh = jnp.zeros((1, HEAD_PAD), F32)
        for h in range(HEADS):
            sl = slice(h * HEAD_PAD, (h + 1) * HEAD_PAD)
            x = qr[:, sl]
            dx, dgh = _head_bwd(x, _rstd(x, MLA_QK), gqh_ref[...], cosv, ssv, dq_ref[:, sl])
            dqr_ref[:, sl] = dx.astype(BF16)
            dgqh = dgqh + dgh
            x = jnp.where(lane < MLA_NOPE, kv[:, sl], kr)
            dx, dgh = _head_bwd(x, _rstd(x, MLA_QK), gkh_ref[...], cosv, ssv, dk_ref[:, sl])
            dgkh = dgkh + dgh
            dkr = dkr + jnp.where(lane >= MLA_NOPE, dx, 0.0)
            dkv_ref[:, sl] = jnp.where(lane < MLA_NOPE, dx, dv_ref[:, sl]).astype(BF16)
        dgqh_ref[...] += dgqh
        dgkh_ref[...] += dgkh
        dkr_ref[...] = dkr.astype(BF16)
        dqr = dqr_ref[...]
        dkvb = dkv_ref[...]
        dwq_ref[...] += _dot_tn(cqn, dqr)
        dwkv_ref[...] += _dot_tn(ckvn, dkvb)
        dcqn = _dot_nt(dqr, wq_ref[...])
        dgql_ref[...] += jnp.sum(dcqn * (xq * rq), axis=0, keepdims=True)
        dcq_ref[...] = _rms_bwd(xq, rq, gql_ref[...], dcqn, Q_LORA).astype(BF16)
        dckvn = _dot_nt(dkvb, wkv_ref[...])
        dgkvl_ref[...] += jnp.sum(dckvn * (xk * rk), axis=0, keepdims=True)
        dckv_ref[...] = _rms_bwd(xk, rk, gkvl_ref[...], dckvn, KV_LORA).astype(BF16)

    vec = lambda n: jax.ShapeDtypeStruct((1, n), F32)
    outs = pl.pallas_call(
        body, name=name, grid=(s // tm,),
        in_specs=[_rows(tm, Q_LORA), _rows(tm, KV_LORA), _rows(tm, HEAD_PAD), _rows(tm, 1),
                  _whole((1, HEAD_PAD)), _whole((1, HEAD_PAD)), _whole((1, Q_LORA)), _whole((1, KV_LORA)),
                  _whole((1, HEAD_PAD)), _whole((1, HEAD_PAD)), _whole((Q_LORA, width)), _whole((KV_LORA, width)),
                  _rows(tm, width), _rows(tm, width), _rows(tm, width)],
        out_specs=[_rows(tm, Q_LORA), _rows(tm, KV_LORA), _rows(tm, HEAD_PAD), _whole((Q_LORA, width)),
                   _whole((KV_LORA, width)), _whole((1, Q_LORA)), _whole((1, KV_LORA)), _whole((1, HEAD_PAD)),
                   _whole((1, HEAD_PAD)), _rows(tm, width), _rows(tm, width)],
        out_shape=[jax.ShapeDtypeStruct((s, Q_LORA), BF16), jax.ShapeDtypeStruct((s, KV_LORA), BF16),
                   jax.ShapeDtypeStruct((s, HEAD_PAD), BF16), jax.ShapeDtypeStruct((Q_LORA, width), F32),
                   jax.ShapeDtypeStruct((KV_LORA, width), F32), vec(Q_LORA), vec(KV_LORA), vec(HEAD_PAD), vec(HEAD_PAD),
                   jax.ShapeDtypeStruct((s, width), BF16), jax.ShapeDtypeStruct((s, width), BF16)],
        compiler_params=_cp(("arbitrary",)))(cq, ckv, krope, pos, freq, sign, g_ql, g_kvl, g_qh, g_kh, wq, wkv, dq, dk, dv)
    return outs[:9]


def _tn_call(a, b, name, shard_cols=None):
    s, ka = a.shape
    nb = b.shape[1]
    ti = _pick(ka, (512, 256, 128))
    if shard_cols is not None:
        tj = shard_cols
    else:
        tj = nb if nb <= TN_MAX_COLS else _pick(nb, (2176, 1024, 512, 256, 128))
    ts = min(1024 if tj <= 1024 else 512, s)
    ns = s // ts

    def body(a_ref, b_ref, o_ref, acc_ref):
        part = _dot_tn(a_ref[...].astype(BF16), b_ref[...].astype(BF16))

        @pl.when(pl.program_id(2) == 0)
        def _():
            acc_ref[...] = part

        @pl.when(pl.program_id(2) != 0)
        def _():
            acc_ref[...] += part

        @pl.when(pl.program_id(2) == ns - 1)
        def _():
            o_ref[...] = acc_ref[...].astype(o_ref.dtype)

    if shard_cols is None:
        out_spec = pl.BlockSpec((ti, tj), lambda i, j, t: (i, j))
        out_shape = jax.ShapeDtypeStruct((ka, nb), BF16)
    else:
        out_spec = pl.BlockSpec((None, ti, tj), lambda i, j, t: (j, i, 0))
        out_shape = jax.ShapeDtypeStruct((nb // tj, ka, tj), BF16)
    return pl.pallas_call(
        body, name=name, grid=(ka // ti, nb // tj, ns),
        in_specs=[pl.BlockSpec((ts, ti), lambda i, j, t: (t, i)), pl.BlockSpec((ts, tj), lambda i, j, t: (t, j))],
        out_specs=out_spec, out_shape=out_shape, scratch_shapes=[pltpu.VMEM((ti, tj), F32)],
        compiler_params=_cp(("parallel", "parallel", "arbitrary")))(a, b)


def _sum_call(parts, out_dtype, name):
    n, r, w = parts.shape
    tr = _pick(r, FLAT_TILES)

    def body(p_ref, o_ref):
        acc = p_ref[0].astype(F32)
        for k in range(1, n):
            acc = acc + p_ref[k].astype(F32)
        o_ref[...] = acc.astype(out_dtype)

    return pl.pallas_call(
        body, name=name, grid=(r // tr,),
        in_specs=[pl.BlockSpec((n, tr, w), lambda i: (0, i, 0))], out_specs=_rows(tr, w),
        out_shape=jax.ShapeDtypeStruct((r, w), out_dtype), compiler_params=_cp(("parallel",)))(parts)


def _chip_sum_call(by_chip, core, name):
    n, r, w = by_chip.shape
    tr = _pick(r, FLAT_TILES)
    nblk = r // tr

    def body(c_ref, p_ref, o_ref):
        acc = p_ref[0].astype(F32)
        for k in range(1, n):
            acc = acc + p_ref[k].astype(F32)
        o_ref[...] = acc

    return pl.pallas_call(
        body, name=name,
        grid_spec=pltpu.PrefetchScalarGridSpec(
            num_scalar_prefetch=1, grid=(nblk,),
            in_specs=[pl.BlockSpec((n, tr, w), lambda i, c_ref: (0, i, 0))],
            out_specs=pl.BlockSpec((tr, w), lambda i, c_ref: (c_ref[0] * nblk + i, 0))),
        out_shape=jax.ShapeDtypeStruct((2 * r, w), F32),
        compiler_params=_cp(("parallel",)))(core.reshape(1).astype(jnp.int32), by_chip)


def _pair_sum_call(full, other, core, out_dtype, name):
    n, r, w = other.shape
    tr = _pick(r, FLAT_TILES)
    nblk = r // tr

    def body(c_ref, a_ref, b_ref, o_ref):
        o_ref[...] = (a_ref[...].astype(F32) + b_ref[...].astype(F32)).astype(out_dtype)

    spec = pl.BlockSpec((None, tr, w), lambda k, i, c_ref: (k, i, 0))
    return pl.pallas_call(
        body, name=name,
        grid_spec=pltpu.PrefetchScalarGridSpec(
            num_scalar_prefetch=1, grid=(n, nblk),
            in_specs=[pl.BlockSpec((None, tr, w), lambda k, i, c_ref: (k, c_ref[0] * nblk + i, 0)), spec],
            out_specs=spec),
        out_shape=jax.ShapeDtypeStruct((n, r, w), out_dtype),
        compiler_params=_cp(("parallel", "parallel")))(core.reshape(1).astype(jnp.int32), full, other)


def _adamw_call(w, g, row0, m, v, name):
    r, c = w.shape
    tr = _pick(math.gcd(r, row0) if row0 else r, (256, 128, 64, 32, 16, 8))
    off = row0 // tr

    def body(w_ref, g_ref, m_ref, v_ref, g_out_ref, d_ref, nm_ref, nv_ref):
        gg = g_ref[...]
        g_out_ref[...] = gg
        nm = ADAM_B1 * m_ref[...] + (1.0 - ADAM_B1) * gg
        nv = ADAM_B2 * v_ref[...] + (1.0 - ADAM_B2) * (gg * gg)
        m_hat = nm / (1.0 - ADAM_B1 ** ADAM_STEP)
        v_hat = nv / (1.0 - ADAM_B2 ** ADAM_STEP)
        d_ref[...] = -ADAM_LR * (m_hat / (jnp.sqrt(v_hat) + ADAM_EPS) + ADAM_WD * w_ref[...])
        nm_ref[...] = nm
        nv_ref[...] = nv

    out = jax.ShapeDtypeStruct((r, c), F32)
    g_spec = pl.BlockSpec((tr, c), lambda i: (off + i, 0))
    return pl.pallas_call(
        body, name=name, grid=(r // tr,), in_specs=[_rows(tr, c), g_spec, _rows(tr, c), _rows(tr, c)],
        out_specs=[_rows(tr, c)] * 4, out_shape=[out, out, out, out], compiler_params=_cp(("parallel",)))(w, g, m, v)


HBM = pl.BlockSpec(memory_space=pl.ANY)


def _position():
    x, y, c = lax.axis_index("x"), lax.axis_index("y"), lax.axis_index("c")
    chips = [(1 - x, y), (x, 1 - y), (1 - x, 1 - y)]
    return x, y, c, chips


def _gather_call(parts, name):
    n = len(parts)

    def body(*refs):
        in_refs, out_refs = refs[:n], refs[n:2 * n]
        send_sems, recv_sems, fwd_send, fwd_recv = refs[2 * n:]
        x, y, c, chips = _position()

        def piece(k, chip, core):
            half = parts[k].shape[0] // 2
            return out_refs[k].at[2 * chip[0] + chip[1], pl.ds(core * half, half), :]

        first = []
        for j, chip in enumerate(chips):
            for k in range(n):
                half = parts[k].shape[0] // 2
                cp = pltpu.make_async_remote_copy(
                    src_ref=in_refs[k].at[pl.ds(c * half, half), :], dst_ref=piece(k, (x, y), c),
                    send_sem=send_sems.at[n * j + k], recv_sem=recv_sems.at[n * j + k],
                    device_id=(*chip, c), device_id_type=MESH)
                cp.start()
                first.append(cp)
        passed = []
        for j, chip in enumerate(chips):
            for k in range(n):
                first[n * j + k].wait_recv()
                cp = pltpu.make_async_remote_copy(
                    src_ref=piece(k, chip, c), dst_ref=piece(k, chip, c),
                    send_sem=fwd_send.at[n * j + k], recv_sem=fwd_recv.at[n * j + k],
                    device_id=(x, y, 1 - c), device_id_type=MESH)
                cp.start()
                passed.append(cp)
        for cp in passed:
            cp.wait_recv()
        for cp in first + passed:
            cp.wait_send()

    sems = pltpu.SemaphoreType.DMA((3 * n,))
    return pl.pallas_call(
        body, name=name, in_specs=[HBM] * n, out_specs=[HBM] * n,
        out_shape=[jax.ShapeDtypeStruct((N_CHIPS,) + p.shape, p.dtype) for p in parts],
        scratch_shapes=[sems, sems, sems, sems])(*parts)


def _pair_send_call(parts, name):
    n = len(parts)

    def body(*refs):
        in_refs, out_refs = refs[:n], refs[n:2 * n]
        send_sems, recv_sems = refs[2 * n:]
        x, y, c, _ = _position()
        copies = []
        for k in range(n):
            half = parts[k].shape[1] // 2
            cp = pltpu.make_async_remote_copy(
                src_ref=in_refs[k].at[:, pl.ds((1 - c) * half, half), :], dst_ref=out_refs[k],
                send_sem=send_sems.at[k], recv_sem=recv_sems.at[k], device_id=(x, y, 1 - c), device_id_type=MESH)
            cp.start()
            copies.append(cp)
        for cp in copies:
            cp.wait()

    sems = pltpu.SemaphoreType.DMA((n,))
    return pl.pallas_call(
        body, name=name, in_specs=[HBM] * n, out_specs=[HBM] * n,
        out_shape=[jax.ShapeDtypeStruct((p.shape[0], p.shape[1] // 2, p.shape[2]), p.dtype) for p in parts],
        scratch_shapes=[sems, sems])(*parts)


def _chip_scatter_call(parts, name):
    n = len(parts)

    def body(*refs):
        in_refs, out_refs = refs[:n], refs[n:2 * n]
        send_sems, recv_sems = refs[2 * n:]
        x, y, c, chips = _position()
        me = 2 * x + y
        copies = []
        for j, chip in enumerate(chips):
            for k in range(n):
                cp = pltpu.make_async_remote_copy(
                    src_ref=in_refs[k].at[2 * chip[0] + chip[1]], dst_ref=out_refs[k].at[me],
                    send_sem=send_sems.at[n * j + k], recv_sem=recv_sems.at[n * j + k],
                    device_id=(*chip, c), device_id_type=MESH)
                cp.start()
                copies.append(cp)
        for cp in copies:
            cp.wait()

    sems = pltpu.SemaphoreType.DMA((3 * n,))
    return pl.pallas_call(
        body, name=name, in_specs=[HBM] * n, out_specs=[HBM] * n,
        out_shape=[jax.ShapeDtypeStruct(p.shape, p.dtype) for p in parts],
        scratch_shapes=[sems, sems])(*parts)


def _pair_swap_call(parts, name):
    n = len(parts)

    def body(*refs):
        out_refs = refs[n:2 * n]
        send_sems, recv_sems = refs[2 * n:]
        x, y, c, _ = _position()
        copies = []
        for k in range(n):
            half = parts[k].shape[0] // 2
            mine = out_refs[k].at[pl.ds(c * half, half), :]
            cp = pltpu.make_async_remote_copy(
                src_ref=mine, dst_ref=mine, send_sem=send_sems.at[k], recv_sem=recv_sems.at[k],
                device_id=(x, y, 1 - c), device_id_type=MESH)
            cp.start()
            copies.append(cp)
        for cp in copies:
            cp.wait()

    sems = pltpu.SemaphoreType.DMA((n,))
    return pl.pallas_call(
        body, name=name, in_specs=[HBM] * n, out_specs=[HBM] * n,
        out_shape=[jax.ShapeDtypeStruct(p.shape, p.dtype) for p in parts],
        input_output_aliases={k: k for k in range(n)},
        scratch_shapes=[sems, sems])(*parts)


def _all_gather_small_call(block, name):
    r, w = block.shape

    def body(in_ref, out_ref, send_sems, recv_sems, local_sem):
        x, y, c, _ = _position()
        me = 4 * x + 2 * y + c
        own = pltpu.make_async_copy(in_ref, out_ref.at[me], local_sem)
        own.start()
        copies = []
        for k in range(1, 8):
            peer = (x ^ (k >> 2), y ^ ((k >> 1) & 1), c ^ (k & 1))
            cp = pltpu.make_async_remote_copy(
                src_ref=in_ref, dst_ref=out_ref.at[me], send_sem=send_sems.at[k - 1], recv_sem=recv_sems.at[k - 1],
                device_id=peer, device_id_type=MESH)
            cp.start()
            copies.append(cp)
        for cp in copies:
            cp.wait()
        own.wait()

    return pl.pallas_call(
        body, name=name, in_specs=[HBM], out_specs=HBM,
        out_shape=jax.ShapeDtypeStruct((8, r, w), block.dtype),
        scratch_shapes=[pltpu.SemaphoreType.DMA((7,)), pltpu.SemaphoreType.DMA((7,)), pltpu.SemaphoreType.DMA])(block)


BIG = {
    "ffn1_w_in": ((D_MODEL, 2 * D_FF), 1), "ffn1_w_out": ((D_FF, D_MODEL), 0),
    "w_in": ((D_MODEL, 4256), 1), "w_q_up": ((Q_LORA, HEADS * MLA_QK), 1), "w_kv_up": ((KV_LORA, 1024), 1),
    "w_branch_mla": ((512, D_MODEL), 1), "w_branch_sb": ((SB_WIDTH, D_MODEL), 1), "w_out": ((D_MODEL, D_MODEL), 0),
    "ffn2_w_in": ((D_MODEL, 2 * D_FF), 1), "ffn2_w_out": ((D_FF, D_MODEL), 0),
    "w_ple_gate": ((D_MODEL, D_MODEL), 0), "w_ple_proj": ((PLE_DIM, D_MODEL), 1),
}
GAINS = {"ffn1_norm": 1024, "mix_norm": 1024, "q_latent_norm": 384, "kv_latent_norm": 256, "q_head_norm": 96,
         "k_head_norm": 96, "ffn2_norm": 1024, "ple_norm": 1024}
WEIGHT_ORDER = ["ffn1_norm", "ffn1_w_in", "ffn1_w_out", "mix_norm", "w_in", "q_latent_norm", "w_q_up",
                "kv_latent_norm", "w_kv_up", "q_head_norm", "k_head_norm", "w_branch_mla", "w_branch_sb", "w_out",
                "ffn2_norm", "ffn2_w_in", "ffn2_w_out", "ple_norm", "w_ple_gate", "w_ple_proj"]


def _shard_shape(name):
    (r, c), axis = BIG[name]
    return (r // N_CHIPS, c) if axis == 0 else (r, c // N_CHIPS)


PARTS = [("ffn1_w_in", "ffn2_w_in"), ("ffn1_w_out", "ffn2_w_out", "w_out", "w_ple_gate"), ("w_in",),
         ("w_kv_up", "w_branch_mla", "w_branch_sb", "w_ple_proj"), ("w_q_up",)]


def _join_parts(shards):
    return [shards[part[0]] if len(part) == 1 else jnp.concatenate([shards[n] for n in part], axis=-2) for part in PARTS]


def _part_rows():
    where = {}
    for k, part in enumerate(PARTS):
        at = 0
        for n in part:
            where[n] = (k, at)
            at += _shard_shape(n)[0]
    return where


def _split_parts(parts):
    where = _part_rows()
    return {n: parts[k][..., at:at + _shard_shape(n)[0], :] for n, (k, at) in where.items()}


def _to_shards(name, full):
    (r, c), axis = BIG[name]
    if axis == 0:
        return full.reshape(N_CHIPS, r // N_CHIPS, c)
    return full.reshape(r, N_CHIPS, c // N_CHIPS).transpose(1, 0, 2)


def _from_shards(name, shards):
    (r, c), axis = BIG[name]
    if axis == 0:
        return shards.reshape(r, c)
    return shards.transpose(1, 0, 2).reshape(r, c)


def _relayout_w_in(w):
    d = w.shape[0]
    z = lambda n: jnp.zeros((d, n), w.dtype)
    return jnp.concatenate([w[:, :640], z(MLA_NOPE), w[:, 640:672], z(HEAD_PAD - MLA_QK), w[:, 672:]], axis=1)


def _unlayout_w_in(g):
    return jnp.concatenate([g[:, :640], g[:, 640 + MLA_NOPE:640 + MLA_QK], g[:, 768:]], axis=1)


def _pad_heads(v):
    lead = v.shape[:-1]
    return jnp.pad(v.reshape(lead + (HEADS, MLA_QK)), [(0, 0)] * len(lead) + [(0, 0), (0, HEAD_PAD - MLA_QK)]).reshape(
        lead + (HEADS * HEAD_PAD,))


def _halves(w):
    n = w.shape[1] // 2
    return [w[:, :n], w[:, n:]]


def _local_step(x, p, pos, tgt, gains, wts):
    d = D_MODEL
    inv_freq = ROPE_BASE ** (-jnp.arange(0, MLA_ROPE, 2, dtype=F32) / MLA_ROPE)
    zeros = lambda n: jnp.zeros((n,), F32)
    freq = jnp.concatenate([zeros(MLA_NOPE), inv_freq, inv_freq, zeros(HEAD_PAD - MLA_QK)])[None]
    sign = jnp.concatenate([zeros(MLA_NOPE), -jnp.ones((16,), F32), jnp.ones((16,), F32), zeros(HEAD_PAD - MLA_QK)])[None]
    pad_gain = lambda g: jnp.pad(g, ((0, 0), (0, HEAD_PAD - MLA_QK)))
    g_qh, g_kh = pad_gain(gains["q_head_norm"]), pad_gain(gains["k_head_norm"])
    w_in = _relayout_w_in(wts["w_in"])
    wq = _pad_heads(wts["w_q_up"])
    wkv = wts["w_kv_up"]
    wbm = jnp.pad(wts["w_branch_mla"].reshape(HEADS, 64, d), ((0, 0), (64, 0), (0, 0))).reshape(HEADS * HEAD_PAD, d)
    wbs, wo = wts["w_branch_sb"], wts["w_out"]

    u1 = _norm_call(x, gains["ffn1_norm"], "norm_ffn1")
    a1, b1, hm1 = _ffn_in_call(u1, wts["ffn1_w_in"], "ffn1_in")
    h1 = _ffn_out_call(hm1, wts["ffn1_w_out"], x, "ffn1_out")
    um = _norm_call(h1, gains["mix_norm"], "norm_mix")
    cq, ckv, krope, sbq, sbk, sbv, gates = _mix_in_call(um, w_in, "mix_in")
    prep_args = (cq, ckv, krope, pos, freq, sign, gains["q_latent_norm"], gains["kv_latent_norm"], g_qh, g_kh, wq, wkv)
    q, k, v = _mla_prep_call(*prep_args, "mla_prep")
    om, lse = _mla_fwd_call(q, k, v, "mla_fwd")
    osb, tot = _sb_fwd_call(sbq, sbk, sbv, "sb_fwd")
    h2, bm, bs, mg = _merge_out_call(om, osb, gates, h1, wbm, wbs, wo, "merge_out")
    u2 = _norm_call(h2, gains["ffn2_norm"], "norm_ffn2")
    a2, b2, hm2 = _ffn_in_call(u2, wts["ffn2_w_in"], "ffn2_in")
    h3 = _ffn_out_call(hm2, wts["ffn2_w_out"], h2, "ffn2_out")

    grads = {}
    dh3, dh3s, un, dgl, dpp, grads["ple_norm"], sq = _ple_call(
        h3, gains["ple_norm"], wts["w_ple_gate"], p, wts["w_ple_proj"], tgt, "ple")
    grads["w_ple_gate"] = _tn_call(un, dgl, "dw_ple_gate")
    grads["w_ple_proj"] = _tn_call(p, dpp, "dw_ple_proj")

    da2, db2 = _ffn_bwd_a_call(dh3s, a2, b2, wts["ffn2_w_out"], "ffn2_bwd_act")
    grads["ffn2_w_out"] = _tn_call(hm2, dh3s, "dw_ffn2_out")
    grads["ffn2_w_in"] = jnp.concatenate([_tn_call(u2, da2, "dw_ffn2_in_a", shard_cols=D_FF // 2),
                                          _tn_call(u2, db2, "dw_ffn2_in_b", shard_cols=D_FF // 2)], axis=0)
    dh2, dh2b, grads["ffn2_norm"] = _norm_bwd_call([da2, db2], _halves(wts["ffn2_w_in"]), h2, gains["ffn2_norm"], dh3,
                                                   "ffn2_bwd_norm", half_out=False)

    dgates, dbm, dbs, dom, dos = _merge_bwd_call(dh2b, gates, bm, bs, wo, wbm, wbs, "merge_bwd")
    grads["w_out"] = _tn_call(mg, dh2b, "dw_out")
    grads["w_branch_mla"] = _tn_call(om, dbm, "dw_branch_mla").reshape(HEADS, HEAD_PAD, d)[:, 64:, :].reshape(512, d)
    grads["w_branch_sb"] = _tn_call(osb, dbs, "dw_branch_sb")
    dq, dk, dv = _mla_bwd_call(q, k, v, om, dom, lse, "mla_bwd")
    dsq, dsk, dsv = _sb_bwd_call(sbq, sbk, sbv, dos, tot, "sb_bwd")
    (dcq, dckv, dkr, dwq, grads["w_kv_up"], grads["q_latent_norm"], grads["kv_latent_norm"], dgqh, dgkh) = \
        _mla_prep_bwd_call(*prep_args, dq, dk, dv, "mla_prep_bwd")
    grads["w_q_up"] = dwq.reshape(Q_LORA, HEADS, HEAD_PAD)[:, :, :MLA_QK].reshape(Q_LORA, HEADS * MLA_QK)
    grads["q_head_norm"], grads["k_head_norm"] = dgqh[:, :MLA_QK], dgkh[:, :MLA_QK]
    dproj = jnp.concatenate([dcq, dckv, dkr, dsq, dsk.astype(BF16), dsv.astype(BF16), dgates], axis=1)
    grads["w_in"] = _unlayout_w_in(_tn_call(um, dproj, "dw_in"))
    dh1, dh1s, grads["mix_norm"] = _norm_bwd_call([dproj], [w_in], h1, gains["mix_norm"], dh2, "mix_bwd_norm",
                                                  half_out=True)

    da1, db1 = _ffn_bwd_a_call(dh1s, a1, b1, wts["ffn1_w_out"], "ffn1_bwd_act")
    grads["ffn1_w_out"] = _tn_call(hm1, dh1s, "dw_ffn1_out")
    grads["ffn1_w_in"] = jnp.concatenate([_tn_call(u1, da1, "dw_ffn1_in_a", shard_cols=D_FF // 2),
                                          _tn_call(u1, db1, "dw_ffn1_in_b", shard_cols=D_FF // 2)], axis=0)
    dx, _, grads["ffn1_norm"] = _norm_bwd_call([da1, db1], _halves(wts["ffn1_w_in"]), x, gains["ffn1_norm"], dh1,
                                               "ffn1_bwd_norm", half_out=False)
    return sq, dx, grads


def kernel(x, p, positions, ffn1_norm, ffn1_w_in, ffn1_w_out, mix_norm, w_in, q_latent_norm, w_q_up, kv_latent_norm, w_kv_up, q_head_norm, k_head_norm, w_branch_mla, w_branch_sb, w_out, ffn2_norm, ffn2_w_in, ffn2_w_out, ple_norm, w_ple_gate, w_ple_proj, loss_target, m_ffn1_norm, m_ffn1_w_in, m_ffn1_w_out, m_mix_norm, m_w_in, m_q_latent_norm, m_w_q_up, m_kv_latent_norm, m_w_kv_up, m_q_head_norm, m_k_head_norm, m_w_branch_mla, m_w_branch_sb, m_w_out, m_ffn2_norm, m_ffn2_w_in, m_ffn2_w_out, m_ple_norm, m_w_ple_gate, m_w_ple_proj, v_ffn1_norm, v_ffn1_w_in, v_ffn1_w_out, v_mix_norm, v_w_in, v_q_latent_norm, v_w_q_up, v_kv_latent_norm, v_w_kv_up, v_q_head_norm, v_k_head_norm, v_w_branch_mla, v_w_branch_sb, v_w_out, v_ffn2_norm, v_ffn2_w_in, v_ffn2_w_out, v_ple_norm, v_w_ple_gate, v_w_ple_proj):
    given = dict(locals())
    w_shard = {n: given[n][0] for n in WEIGHT_ORDER}
    m_shard = {n: given["m_" + n][0] for n in WEIGHT_ORDER}
    v_shard = {n: given["v_" + n][0] for n in WEIGHT_ORDER}
    gains = {n: w_shard[n][None] for n in GAINS}

    c = lax.axis_index("c")
    chip = 2 * lax.axis_index("x") + lax.axis_index("y")
    mine = _join_parts({n: w_shard[n].astype(BF16) for n in BIG})
    others = _gather_call(mine, "gather_weights")
    gathered = _split_parts([lax.dynamic_update_slice_in_dim(o, m[None], chip, axis=0) for o, m in zip(others, mine)])
    wts = {n: _from_shards(n, gathered[n]) for n in BIG}

    sq, dx, grads = _local_step(x[0], p[0, 0], positions.reshape(-1, 1), loss_target[0], gains, wts)

    partial = _join_parts({n: grads[n] if grads[n].ndim == 3 else _to_shards(n, grads[n].astype(BF16)) for n in BIG})
    from_sibling = _pair_send_call(partial, "grads_pair_send")
    pair_sum = [_pair_sum_call(a, b, c, BF16, "grads_pair_sum_%d" % k) for k, (a, b) in enumerate(zip(partial, from_sibling))]
    by_chip = _chip_scatter_call(pair_sum, "grads_chip_scatter")
    by_chip = [lax.dynamic_update_slice_in_dim(t, lax.dynamic_slice_in_dim(o, chip, 1, axis=0), chip, axis=0)
               for t, o in zip(by_chip, pair_sum)]
    reduced = _pair_swap_call([_chip_sum_call(t, c, "grads_chip_sum_%d" % k) for k, t in enumerate(by_chip)],
                              "grads_pair_swap")

    rows = [jnp.pad(grads[n], ((0, 0), (0, D_MODEL - GAINS[n]))) for n in GAINS] + [sq]
    gain_block = jnp.concatenate(rows + [jnp.zeros((16 - len(rows), D_MODEL), F32)], axis=0)
    gain_sum = _sum_call(_all_gather_small_call(gain_block, "gains_all_gather"), F32, "gains_sum")
    loss = 0.5 * jnp.sum(gain_sum[len(GAINS)]) / D_MODEL

    outs = {"grad": {}, "delta": {}, "new_m": {}, "new_v": {}}
    gain_pack = lambda t: jnp.concatenate([jnp.pad(t[n][None], ((0, 0), (0, D_MODEL - GAINS[n]))) for n in GAINS], axis=0)
    packed = _adamw_call(gain_pack(w_shard), gain_sum, 0, gain_pack(m_shard), gain_pack(v_shard), "adamw_gains")
    for i, n in enumerate(GAINS):
        for kind, t in zip(("grad", "delta", "new_m", "new_v"), packed):
            outs[kind][n] = t[i, :GAINS[n]][None]
    where = _part_rows()
    for n in BIG:
        k, row0 = where[n]
        for kind, t in zip(("grad", "delta", "new_m", "new_v"),
                           _adamw_call(w_shard[n], reduced[k], row0, m_shard[n], v_shard[n], "adamw_" + n)):
            outs[kind][n] = t[None]

    return (loss, dx[None], *[outs["grad"][n] for n in WEIGHT_ORDER], *[outs["delta"][n] for n in WEIGHT_ORDER],
            *[outs["new_m"][n] for n in WEIGHT_ORDER], *[outs["new_v"][n] for n in WEIGHT_ORDER])
```

```python
import collections
import functools
import math

import jax
import jax.numpy as jnp
from jax import lax
from jax.experimental import pallas as pl
from jax.experimental.pallas import tpu as pltpu

F32 = jnp.float32
BF16 = jnp.bfloat16
MESH = pl.DeviceIdType.MESH

D_MODEL = 1024
D_FF = 2816
PLE_DIM = 256
NORM_EPS = 1e-6
HEADS = 8
MLA_NOPE = 64
MLA_ROPE = 32
MLA_QK = 96
Q_LORA = 384
KV_LORA = 256
SB_WIDTH = 512
ROPE_BASE = 10000.0
HEAD_PAD = 128
N_CHIPS = 4

ADAM_LR = 0.001
ADAM_B1 = 0.9
ADAM_B2 = 0.999
ADAM_EPS = 1e-08
ADAM_WD = 0.01
ADAM_STEP = 10

SEG_CQ = (0, 384)
SEG_CKV = (384, 256)
SEG_KROPE = (640, 128)
SEG_SBQ = (768, 512)
SEG_SBK = (1280, 512)
SEG_SBV = (1792, 512)
SEG_GATES = (2304, 2048)
IN_COLS_PAD = 4352

TM = 512
TM_SMALL = 256
TQ = 256
MLA_FWD_BLOCKS = 4
MLA_BWD_BLOCKS = 4
SB_FWD_BLOCKS = 4
SB_BWD_BLOCKS = 2
SB_HEAD = 64
SB_SCALE = 0.125
TN_MAX_COLS = 2816
MAX_ROW_TILE = 512
VMEM_LIMIT = 56 * 1024 * 1024

NT = (((1,), (1,)), ((), ()))
TN = (((0,), (0,)), ((), ()))


def _cp(sem):
    return pltpu.CompilerParams(dimension_semantics=sem, vmem_limit_bytes=VMEM_LIMIT)


def _rows(tm, w):
    return pl.BlockSpec((tm, w), lambda i: (i, 0))


def _whole(shape):
    return pl.BlockSpec(shape, lambda i: (0,) * len(shape))


def _dot(a, b):
    return jnp.dot(a, b, preferred_element_type=F32)


def _dot_nt(a, b):
    return lax.dot_general(a, b, NT, preferred_element_type=F32)


def _dot_tn(a, b):
    return lax.dot_general(a, b, TN, preferred_element_type=F32)


def _rstd(x, n):
    return lax.rsqrt(jnp.sum(x * x, axis=-1, keepdims=True) / n + NORM_EPS)


def _rms_bwd(x, r, g, dy, n):
    gy = dy * g
    return r * gy - x * ((r * r * r) * (jnp.sum(x * gy, axis=-1, keepdims=True) / n))


def _sigmoid(x):
    return jax.nn.sigmoid(x)


def _pick(n, cands):
    for c in cands:
        if n % c == 0:
            return c
    return n


def _row_tile(r):
    for t in range(min(r, MAX_ROW_TILE) // 16 * 16, 15, -16):
        if r % t == 0:
            return t
    return r


HBM = pl.BlockSpec(memory_space=pl.ANY)

_Rider = collections.namedtuple("_Rider", "ins out_shape sems start finish")


def _with_rider(body, rider, *, name, grid, in_specs, out_specs, out_shape, args, sem):
    if rider is None:
        return pl.pallas_call(body, name=name, grid=grid, in_specs=in_specs, out_specs=out_specs, out_shape=out_shape,
                              compiler_params=_cp(sem))(*args), None
    ni, no, nri, nro = len(in_specs), len(out_specs), len(rider.ins), len(rider.out_shape)

    def riding(*refs):
        ins, r_ins = refs[:ni], refs[ni:ni + nri]
        outs, r_outs = refs[ni + nri:ni + nri + no], refs[ni + nri + no:ni + nri + no + nro]
        sems = refs[ni + nri + no + nro:]
        ids = [pl.program_id(a) for a in range(len(grid))]
        first = functools.reduce(jnp.logical_and, [i == 0 for i in ids])
        last = functools.reduce(jnp.logical_and, [i == g - 1 for i, g in zip(ids, grid)])

        @pl.when(first)
        def _():
            rider.start(r_ins, r_outs, sems)

        body(*ins, *outs)

        @pl.when(last)
        def _():
            rider.finish(r_ins, r_outs, sems)

    res = pl.pallas_call(
        riding, name=name, grid=grid, in_specs=list(in_specs) + [HBM] * nri, out_specs=list(out_specs) + [HBM] * nro,
        out_shape=list(out_shape) + list(rider.out_shape),
        scratch_shapes=[pltpu.SemaphoreType.DMA((k,)) for k in rider.sems],
        compiler_params=_cp(("arbitrary",) * len(grid)))(*args, *rider.ins)
    return res[:no], res[no:]


def _norm_call(h, g, name):
    s, d = h.shape
    tm = min(TM, s)

    def body(h_ref, g_ref, u_ref):
        x = h_ref[...]
        u_ref[...] = ((x * _rstd(x, d)) * g_ref[...]).astype(BF16)

    return pl.pallas_call(
        body, name=name, grid=(s // tm,),
        in_specs=[_rows(tm, d), _whole((1, d))], out_specs=_rows(tm, d),
        out_shape=jax.ShapeDtypeStruct((s, d), BF16), compiler_params=_cp(("parallel",)))(h, g)


def _ffn_in_call(u, w, name, rider=None):
    s, d = u.shape
    n = w.shape[1] // 2
    tn = n // 2
    tm = min(TM, s)
    nj = n // tn

    def body(u_ref, wa_ref, wb_ref, a_ref, b_ref, hm_ref):
        uu = u_ref[...]
        a = _dot(uu, wa_ref[...])
        b = _dot(uu, wb_ref[...])
        a_ref[...] = a
        b_ref[...] = b
        hm_ref[...] = ((a * _sigmoid(a)) * b).astype(BF16)

    blk = pl.BlockSpec((tm, tn), lambda j, i: (i, j))
    return _with_rider(
        body, rider, name=name, grid=(nj, s // tm),
        in_specs=[pl.BlockSpec((tm, d), lambda j, i: (i, 0)),
                  pl.BlockSpec((d, tn), lambda j, i: (0, j)),
                  pl.BlockSpec((d, tn), lambda j, i: (0, j + nj))],
        out_specs=[blk, blk, blk],
        out_shape=[jax.ShapeDtypeStruct((s, n), F32), jax.ShapeDtypeStruct((s, n), F32),
                   jax.ShapeDtypeStruct((s, n), BF16)],
        args=(u, w, w), sem=("parallel", "parallel"))


def _ffn_out_call(hm, w, h, name):
    s, n = hm.shape
    d = w.shape[1]
    tm = min(TM, s)

    def body(hm_ref, w_ref, h_ref, o_ref):
        o_ref[...] = h_ref[...] + 0.5 * _dot(hm_ref[...], w_ref[...])

    return pl.pallas_call(
        body, name=name, grid=(s // tm,),
        in_specs=[_rows(tm, n), _whole((n, d)), _rows(tm, d)], out_specs=_rows(tm, d),
        out_shape=jax.ShapeDtypeStruct((s, d), F32), compiler_params=_cp(("parallel",)))(hm, w, h)


def _mix_in_call(u, w, name):
    s, d = u.shape
    tm = min(TM_SMALL, s)
    segs = [(SEG_CQ, F32), (SEG_CKV, F32), (SEG_KROPE, F32), (SEG_SBQ, BF16), (SEG_SBK, BF16),
            (SEG_SBV, BF16), (SEG_GATES, F32)]

    def body(u_ref, w_ref, *outs):
        uu = u_ref[...]
        for ((off, width), _), o_ref in zip(segs, outs):
            o_ref[...] = _dot(uu, w_ref[:, off:off + width]).astype(o_ref.dtype)

    return pl.pallas_call(
        body, name=name, grid=(s // tm,),
        in_specs=[_rows(tm, d), _whole((d, IN_COLS_PAD))],
        out_specs=[_rows(tm, width) for (_, width), _ in segs],
        out_shape=[jax.ShapeDtypeStruct((s, width), dt) for (_, width), dt in segs],
        compiler_params=_cp(("parallel",)))(u, w)


def _lane(shape):
    return lax.broadcasted_iota(jnp.int32, shape, len(shape) - 1)


def _rot_half(y):
    lane = _lane(y.shape)
    swapped = jnp.where(lane < MLA_NOPE + MLA_ROPE // 2, pltpu.roll(y, HEAD_PAD - 16, 1), pltpu.roll(y, 16, 1))
    return jnp.where((lane >= MLA_NOPE) & (lane < MLA_QK), swapped, 0.0)


def _rope_tables(pos_ref, freq_ref, sign_ref):
    ang = pos_ref[...].astype(F32) * freq_ref[...]
    return jnp.cos(ang), jnp.sin(ang) * sign_ref[...]


def _head_fwd(x, g, cosv, ssv):
    r = _rstd(x, MLA_QK)
    y = (x * r) * g
    return y * cosv + _rot_half(y) * ssv, r


def _head_bwd(x, r, g, cosv, ssv, dout):
    dy = dout * cosv + _rot_half(dout * ssv)
    return _rms_bwd(x, r, g, dy, MLA_QK), jnp.sum(dy * (x * r), axis=0, keepdims=True)


def _mla_prep_call(cq, ckv, krope, pos, freq, sign, g_ql, g_kvl, g_qh, g_kh, wq, wkv, name):
    s = cq.shape[0]
    tm = min(TM_SMALL, s)
    width = HEADS * HEAD_PAD

    def body(cq_ref, ckv_ref, kr_ref, pos_ref, freq_ref, sign_ref, gql_ref, gkvl_ref, gqh_ref, gkh_ref,
             wq_ref, wkv_ref, q_ref, k_ref, v_ref):
        cosv, ssv = _rope_tables(pos_ref, freq_ref, sign_ref)
        x = cq_ref[...]
        qr = _dot(((x * _rstd(x, Q_LORA)) * gql_ref[...]).astype(BF16), wq_ref[...])
        x = ckv_ref[...]
        kv = _dot(((x * _rstd(x, KV_LORA)) * gkvl_ref[...]).astype(BF16), wkv_ref[...])
        kr = kr_ref[...]
        lane = _lane((tm, HEAD_PAD))
        for h in range(HEADS):
            sl = slice(h * HEAD_PAD, (h + 1) * HEAD_PAD)
            qh, _ = _head_fwd(qr[:, sl], gqh_ref[...], cosv, ssv)
            q_ref[:, sl] = qh.astype(BF16)
            kvh = kv[:, sl]
            kh, _ = _head_fwd(jnp.where(lane < MLA_NOPE, kvh, kr), gkh_ref[...], cosv, ssv)
            k_ref[:, sl] = kh.astype(BF16)
            v_ref[:, sl] = jnp.where(lane >= MLA_NOPE, kvh, 0.0).astype(BF16)

    out = jax.ShapeDtypeStruct((s, width), BF16)
    return pl.pallas_call(
        body, name=name, grid=(s // tm,),
        in_specs=[_rows(tm, Q_LORA), _rows(tm, KV_LORA), _rows(tm, HEAD_PAD), _rows(tm, 1),
                  _whole((1, HEAD_PAD)), _whole((1, HEAD_PAD)), _whole((1, Q_LORA)), _whole((1, KV_LORA)),
                  _whole((1, HEAD_PAD)), _whole((1, HEAD_PAD)), _whole((Q_LORA, width)), _whole((KV_LORA, width))],
        out_specs=[_rows(tm, width)] * 3, out_shape=[out, out, out],
        compiler_params=_cp(("parallel",)))(cq, ckv, krope, pos, freq, sign, g_ql, g_kvl, g_qh, g_kh, wq, wkv)


def _attn_specs(s, nb):
    qspec = pl.BlockSpec((TQ, nb * HEAD_PAD), lambda g, i: (i, g))
    kspec = pl.BlockSpec((s, nb * HEAD_PAD), lambda g, i: (0, g))
    return qspec, kspec


def _lanes(b):
    return slice(b * HEAD_PAD, (b + 1) * HEAD_PAD)


def _tri(cmp):
    r = lax.broadcasted_iota(jnp.int32, (TQ, TQ), 0)
    c = lax.broadcasted_iota(jnp.int32, (TQ, TQ), 1)
    return cmp(r, c)


def _mla_fwd_call(q, k, v, name, rider=None):
    s, width = q.shape
    scale = 1.0 / math.sqrt(MLA_QK)

    nb = MLA_FWD_BLOCKS

    def body(q_ref, k_ref, v_ref, o_ref, lse_ref):
        qi = pl.program_id(1)
        qs = [q_ref[:, _lanes(b)] for b in range(nb)]
        causal = _tri(lambda r, c: c <= r)

        def step(kb, carry, diag):
            ks = pl.multiple_of(kb * TQ, TQ)
            heads = range(nb)
            scs = [_dot_nt(qs[b], k_ref[pl.ds(ks, TQ), _lanes(b)]) * scale for b in heads]
            if diag:
                scs = [jnp.where(causal, sc, -1e30) for sc in scs]
            mns = [jnp.maximum(carry[b][0], jnp.max(scs[b], axis=-1, keepdims=True)) for b in heads]
            als = [jnp.exp(carry[b][0] - mns[b]) for b in heads]
            ps = [jnp.exp(scs[b] - mns[b]) for b in heads]
            ls = [als[b] * carry[b][1] + jnp.sum(ps[b], axis=-1, keepdims=True) for b in heads]
            accs = [als[b] * carry[b][2] + _dot(ps[b].astype(BF16), v_ref[pl.ds(ks, TQ), _lanes(b)]) for b in heads]
            return tuple((mns[b], ls[b], accs[b]) for b in heads)

        init = tuple((jnp.full((TQ, 1), -1e30, F32), jnp.zeros((TQ, 1), F32), jnp.zeros((TQ, HEAD_PAD), F32))
                     for _ in range(nb))
        carry = step(qi, init, True)
        carry = lax.fori_loop(0, qi, lambda kb, c: step(kb, c, False), carry)
        for b in range(nb):
            m, l, acc = carry[b]
            o_ref[:, _lanes(b)] = (acc / l).astype(BF16)
            lse_ref[:, _lanes(b)] = jnp.broadcast_to(m + jnp.log(l), (TQ, HEAD_PAD))

    qspec, kspec = _attn_specs(s, nb)
    return _with_rider(
        body, rider, name=name, grid=(width // (nb * HEAD_PAD), s // TQ),
        in_specs=[qspec, kspec, kspec], out_specs=[qspec, qspec],
        out_shape=[jax.ShapeDtypeStruct((s, width), BF16), jax.ShapeDtypeStruct((s, width), F32)],
        args=(q, k, v), sem=("parallel", "arbitrary"))


def _mla_bwd_call(q, k, v, o, do, lse, name, rider=None):
    s, width = q.shape
    scale = 1.0 / math.sqrt(MLA_QK)
    nb = MLA_BWD_BLOCKS

    def body(q_ref, k_ref, v_ref, o_ref, do_ref, lse_ref, dq_ref, dk_ref, dv_ref):
        qi = pl.program_id(1)

        @pl.when(qi == 0)
        def _():
            dk_ref[...] = jnp.zeros_like(dk_ref)
            dv_ref[...] = jnp.zeros_like(dv_ref)

        qs = [q_ref[:, _lanes(b)] for b in range(nb)]
        dos = [do_ref[:, _lanes(b)] for b in range(nb)]
        lses = [lse_ref[:, b * HEAD_PAD:b * HEAD_PAD + 1] for b in range(nb)]
        dlts = [jnp.sum(dos[b].astype(F32) * o_ref[:, _lanes(b)].astype(F32), axis=-1, keepdims=True) for b in range(nb)]
        causal = _tri(lambda r, c: c <= r)

        def step(kb, dqs, diag):
            ks = pl.multiple_of(kb * TQ, TQ)
            heads = range(nb)
            kts = [k_ref[pl.ds(ks, TQ), _lanes(b)] for b in heads]
            scs = [_dot_nt(qs[b], kts[b]) for b in heads]
            dps = [_dot_nt(dos[b], v_ref[pl.ds(ks, TQ), _lanes(b)]) for b in heads]
            ps = [jnp.exp(scs[b] * scale - lses[b]) for b in heads]
            if diag:
                ps = [jnp.where(causal, p, 0.0) for p in ps]
            dss = [(ps[b] * (dps[b] - dlts[b]) * scale).astype(BF16) for b in heads]
            dvs = [_dot_tn(ps[b].astype(BF16), dos[b]) for b in heads]
            dks = [_dot_tn(dss[b], qs[b]) for b in heads]
            out = tuple(dqs[b] + _dot(dss[b], kts[b]) for b in heads)
            for b in heads:
                dv_ref[pl.ds(ks, TQ), _lanes(b)] += dvs[b]
                dk_ref[pl.ds(ks, TQ), _lanes(b)] += dks[b]
            return out

        dqs = step(qi, tuple(jnp.zeros((TQ, HEAD_PAD), F32) for _ in range(nb)), True)
        dqs = lax.fori_loop(0, qi, lambda kb, c: step(kb, c, False), dqs)
        for b in range(nb):
            dq_ref[:, _lanes(b)] = dqs[b]

    qspec, kspec = _attn_specs(s, nb)
    out = jax.ShapeDtypeStruct((s, width), F32)
    return _with_rider(
        body, rider, name=name, grid=(width // (nb * HEAD_PAD), s // TQ),
        in_specs=[qspec, kspec, kspec, qspec, qspec, qspec], out_specs=[qspec, kspec, kspec],
        out_shape=[out, out, out], args=(q, k, v, o, do, lse), sem=("parallel", "arbitrary"))


def _dot_hilo(x, u):
    hi = x.astype(BF16)
    lo = (x - hi.astype(F32)).astype(BF16)
    return _dot(hi, u) + _dot(lo, u)


def _sb_logs(z):
    ls = jnp.minimum(z, 0.0) - jnp.log(1.0 + jnp.exp(-jnp.abs(z)))
    return ls, ls - z


def _sb_head_q(qb, first, hh):
    keep = first if hh == 0 else jnp.logical_not(first)
    return jnp.where(keep, qb, jnp.zeros_like(qb)) * jnp.asarray(SB_SCALE, qb.dtype)


def _sb_fwd_call(q, k, v, name):
    s, width = q.shape
    nb = SB_FWD_BLOCKS
    chains = [(b, hh) for b in range(nb) for hh in range(HEAD_PAD // SB_HEAD)]

    def body(q_ref, k_ref, v_ref, o_ref, t_ref):
        qi = pl.program_id(1)
        strict = _tri(lambda r, c: c < r)
        after = _tri(lambda r, c: r > c).astype(BF16)
        first = _lane((1, HEAD_PAD)) < SB_HEAD
        qhs = [_sb_head_q(q_ref[:, _lanes(b)], first, hh) for b, hh in chains]

        def step(kb, carry, diag):
            ks = pl.multiple_of(kb * TQ, TQ)
            ids = range(len(chains))
            zs = [_dot_nt(qhs[ci], k_ref[pl.ds(ks, TQ), _lanes(chains[ci][0])]) for ci in ids]
            logs = [_sb_logs(z) for z in zs]
            lss = [lg[0] for lg in logs]
            l1ms = [jnp.where(strict, lg[1], 0.0) if diag else lg[1] for lg in logs]
            sufs = [_dot_hilo(l1m, after) for l1m in l1ms]
            as_ = [jnp.exp(lss[ci] + sufs[ci] + carry[ci][0]) for ci in ids]
            if diag:
                as_ = [jnp.where(strict, a, 0.0) for a in as_]
            accs = [carry[ci][1] + _dot(as_[ci].astype(BF16), v_ref[pl.ds(ks, TQ), _lanes(chains[ci][0])]) for ci in ids]
            return tuple((carry[ci][0] + jnp.sum(l1ms[ci], axis=-1, keepdims=True), accs[ci]) for ci in ids)

        init = tuple((jnp.zeros((TQ, 1), F32), jnp.zeros((TQ, HEAD_PAD), F32)) for _ in chains)
        carry = step(qi, init, True)
        carry = lax.fori_loop(0, qi, lambda j, c: step(qi - 1 - j, c, False), carry)
        for b in range(nb):
            (cs0, acc0), (cs1, acc1) = carry[2 * b], carry[2 * b + 1]
            o_ref[:, _lanes(b)] = jnp.where(first, acc0, acc1).astype(BF16)
            t_ref[:, _lanes(b)] = jnp.where(first, cs0, cs1)

    qspec, kspec = _attn_specs(s, nb)
    return pl.pallas_call(
        body, name=name, grid=(width // (nb * HEAD_PAD), s // TQ),
        in_specs=[qspec, kspec, kspec], out_specs=[qspec, qspec],
        out_shape=[jax.ShapeDtypeStruct((s, width), BF16), jax.ShapeDtypeStruct((s, width), F32)],
        compiler_params=_cp(("parallel", "arbitrary")))(q, k, v)


def _sb_bwd_call(q, k, v, do, tot, name):
    s, width = q.shape
    nb = SB_BWD_BLOCKS
    chains = [(b, hh) for b in range(nb) for hh in range(HEAD_PAD // SB_HEAD)]

    def body(q_ref, k_ref, v_ref, do_ref, t_ref, dq_ref, dk_ref, dv_ref):
        qi = pl.program_id(1)

        @pl.when(qi == 0)
        def _():
            dk_ref[...] = jnp.zeros_like(dk_ref)
            dv_ref[...] = jnp.zeros_like(dv_ref)

        strict = _tri(lambda r, c: c < r)
        upto = _tri(lambda r, c: r <= c).astype(BF16)
        before = _tri(lambda r, c: r < c).astype(BF16)
        first = _lane((1, HEAD_PAD)) < SB_HEAD
        qhs = [_sb_head_q(q_ref[:, _lanes(b)], first, hh) for b, hh in chains]
        dohs = []
        for b, hh in chains:
            dob = do_ref[:, _lanes(b)]
            dohs.append(jnp.where(first if hh == 0 else jnp.logical_not(first), dob, jnp.zeros_like(dob)))
        tts = [t_ref[:, b * HEAD_PAD + hh * SB_HEAD:b * HEAD_PAD + hh * SB_HEAD + 1] for b, hh in chains]

        def step(kb, carry, diag):
            ks = pl.multiple_of(kb * TQ, TQ)
            ids = range(len(chains))
            kts = [k_ref[pl.ds(ks, TQ), _lanes(b)] for b, _ in chains]
            zs = [_dot_nt(qhs[ci], kts[ci]) for ci in ids]
            das = [_dot_nt(dohs[ci], v_ref[pl.ds(ks, TQ), _lanes(chains[ci][0])]) for ci in ids]
            logs = [_sb_logs(z) for z in zs]
            lss = [lg[0] for lg in logs]
            l1ms = [jnp.where(strict, lg[1], 0.0) if diag else lg[1] for lg in logs]
            pins = [_dot_hilo(l1m, upto) for l1m in l1ms]
            as_ = [jnp.exp(lss[ci] + (tts[ci] - carry[ci][0] - pins[ci])) for ci in ids]
            if diag:
                as_ = [jnp.where(strict, a, 0.0) for a in as_]
            gs = [as_[ci] * das[ci] for ci in ids]
            cexs = [carry[ci][1] + _dot_hilo(gs[ci], before) for ci in ids]
            dzs = [gs[ci] - jnp.exp(lss[ci]) * (gs[ci] + cexs[ci]) for ci in ids]
            if diag:
                dzs = [jnp.where(strict, dz, 0.0) for dz in dzs]
            dzbs = [dz.astype(BF16) for dz in dzs]
            dvps = [_dot_tn(as_[ci].astype(BF16), dohs[ci]) for ci in ids]
            dkps = [_dot_tn(dzbs[ci], qhs[ci]) for ci in ids]
            out = tuple((carry[ci][0] + jnp.sum(l1ms[ci], axis=-1, keepdims=True),
                         carry[ci][1] + jnp.sum(gs[ci], axis=-1, keepdims=True),
                         carry[ci][2] + _dot(dzbs[ci], kts[ci])) for ci in ids)
            for b in range(nb):
                dk_ref[pl.ds(ks, TQ), _lanes(b)] += dkps[2 * b] + dkps[2 * b + 1]
                dv_ref[pl.ds(ks, TQ), _lanes(b)] += dvps[2 * b] + dvps[2 * b + 1]
            return out

        init = tuple((jnp.zeros((TQ, 1), F32), jnp.zeros((TQ, 1), F32), jnp.zeros((TQ, HEAD_PAD), F32)) for _ in chains)
        carry = lax.fori_loop(0, qi, lambda kb, c: step(kb, c, False), init)
        carry = step(qi, carry, True)
        for b in range(nb):
            dq_ref[:, _lanes(b)] = (jnp.where(first, carry[2 * b][2], carry[2 * b + 1][2]) * SB_SCALE).astype(BF16)

    qspec, kspec = _attn_specs(s, nb)
    return pl.pallas_call(
        body, name=name, grid=(width // (nb * HEAD_PAD), s // TQ),
        in_specs=[qspec, kspec, kspec, qspec, qspec], out_specs=[qspec, kspec, kspec],
        out_shape=[jax.ShapeDtypeStruct((s, width), BF16), jax.ShapeDtypeStruct((s, width), F32),
                   jax.ShapeDtypeStruct((s, width), F32)],
        compiler_params=_cp(("parallel", "arbitrary")))(q, k, v, do, tot)


def _merge_out_call(om, osb, gates, h, wbm, wbs, wo, name):
    s, d = h.shape
    tm = min(TM_SMALL, s)

    def body(om_ref, os_ref, g_ref, h_ref, wbm_ref, wbs_ref, wo_ref, h2_ref, bm_ref, bs_ref, mg_ref):
        bm = _dot(om_ref[...], wbm_ref[...])
        bs = _dot(os_ref[...], wbs_ref[...])
        mg = (_sigmoid(g_ref[:, :d]) * bm + _sigmoid(g_ref[:, d:]) * bs).astype(BF16)
        bm_ref[...] = bm
        bs_ref[...] = bs
        mg_ref[...] = mg
        h2_ref[...] = h_ref[...] + _dot(mg, wo_ref[...])

    return pl.pallas_call(
        body, name=name, grid=(s // tm,),
        in_specs=[_rows(tm, om.shape[1]), _rows(tm, SB_WIDTH), _rows(tm, 2 * d), _rows(tm, d),
                  _whole(wbm.shape), _whole(wbs.shape), _whole(wo.shape)],
        out_specs=[_rows(tm, d)] * 4,
        out_shape=[jax.ShapeDtypeStruct((s, d), F32), jax.ShapeDtypeStruct((s, d), F32),
                   jax.ShapeDtypeStruct((s, d), F32), jax.ShapeDtypeStruct((s, d), BF16)],
        compiler_params=_cp(("parallel",)))(om, osb, gates, h, wbm, wbs, wo)


def _ple_call(h, g, wg, p, wp, tgt, name):
    s, d = h.shape
    tm = min(TM_SMALL, s)

    def body(h_ref, g_ref, wg_ref, p_ref, wp_ref, t_ref, dh_ref, dhs_ref, un_ref, dgl_ref, dpp_ref, dg_ref, sq_ref):
        @pl.when(pl.program_id(0) == 0)
        def _():
            dg_ref[...] = jnp.zeros_like(dg_ref)
            sq_ref[...] = jnp.zeros_like(sq_ref)

        x = h_ref[...]
        gain = g_ref[...]
        r = _rstd(x, d)
        xh = x * r
        un = (xh * gain).astype(BF16)
        sg = _sigmoid(_dot(un, wg_ref[...]))
        pp = _dot(p_ref[...].astype(BF16), wp_ref[...])
        diff = (x + sg * pp) - t_ref[...]
        sq_ref[...] += jnp.sum(diff * diff, axis=0, keepdims=True)
        dy = diff * (1.0 / d)
        dgl = ((dy * pp) * (sg * (1.0 - sg))).astype(BF16)
        dun = _dot_nt(dgl, wg_ref[...])
        dg_ref[...] += jnp.sum(dun * xh, axis=0, keepdims=True)
        dh = dy + _rms_bwd(x, r, gain, dun, d)
        dh_ref[...] = dh
        dhs_ref[...] = (0.5 * dh).astype(BF16)
        un_ref[...] = un
        dgl_ref[...] = dgl
        dpp_ref[...] = (dy * sg).astype(BF16)

    bf = jax.ShapeDtypeStruct((s, d), BF16)
    vec = jax.ShapeDtypeStruct((1, d), F32)
    return pl.pallas_call(
        body, name=name, grid=(s // tm,),
        in_specs=[_rows(tm, d), _whole((1, d)), _whole(wg.shape), _rows(tm, PLE_DIM), _whole(wp.shape), _rows(tm, d)],
        out_specs=[_rows(tm, d)] * 5 + [_whole((1, d))] * 2,
        out_shape=[jax.ShapeDtypeStruct((s, d), F32), bf, bf, bf, bf, vec, vec],
        compiler_params=_cp(("arbitrary",)))(h, g, wg, p, wp, tgt)


def _ffn_bwd_a_call(dhs, a, b, wo, name, rider=None):
    s, n = a.shape
    d = dhs.shape[1]
    tn = n // 2
    tm = min(TM, s)

    def body(dh_ref, a_ref, b_ref, wo_ref, da_ref, db_ref):
        dhm = _dot_nt(dh_ref[...], wo_ref[...])
        av = a_ref[...]
        sa = _sigmoid(av)
        da_ref[...] = (dhm * b_ref[...] * (sa * (1.0 + av * (1.0 - sa)))).astype(BF16)
        db_ref[...] = (dhm * (av * sa)).astype(BF16)

    blk = pl.BlockSpec((tm, tn), lambda j, i: (i, j))
    return _with_rider(
        body, rider, name=name, grid=(n // tn, s // tm),
        in_specs=[pl.BlockSpec((tm, d), lambda j, i: (i, 0)), blk, blk, pl.BlockSpec((tn, d), lambda j, i: (j, 0))],
        out_specs=[blk, blk],
        out_shape=[jax.ShapeDtypeStruct((s, n), BF16)] * 2, args=(dhs, a, b, wo), sem=("parallel", "parallel"))


def _norm_bwd_call(dy_list, w_list, h, g, dh_in, name, half_out):
    s, d = h.shape
    tm = min(TM_SMALL, s)
    nk = len(dy_list)
    factor = 0.5 if half_out else 1.0

    def body(*refs):
        dy_refs = refs[:nk]
        w_refs = refs[nk:2 * nk]
        h_ref, g_ref, dhin_ref, dh_ref, dhb_ref, dg_ref = refs[2 * nk:]

        @pl.when(pl.program_id(0) == 0)
        def _():
            dg_ref[...] = jnp.zeros_like(dg_ref)

        du = _dot_nt(dy_refs[0][...], w_refs[0][...])
        for dy_ref, w_ref in zip(dy_refs[1:], w_refs[1:]):
            du = du + _dot_nt(dy_ref[...], w_ref[...])
        x = h_ref[...]
        r = _rstd(x, d)
        dg_ref[...] += jnp.sum(du * (x * r), axis=0, keepdims=True)
        dh = dhin_ref[...] + _rms_bwd(x, r, g_ref[...], du, d)
        dh_ref[...] = dh
        dhb_ref[...] = (factor * dh).astype(BF16)

    return pl.pallas_call(
        body, name=name, grid=(s // tm,),
        in_specs=[_rows(tm, dy.shape[1]) for dy in dy_list] + [_whole(w.shape) for w in w_list]
        + [_rows(tm, d), _whole((1, d)), _rows(tm, d)],
        out_specs=[_rows(tm, d), _rows(tm, d), _whole((1, d))],
        out_shape=[jax.ShapeDtypeStruct((s, d), F32), jax.ShapeDtypeStruct((s, d), BF16),
                   jax.ShapeDtypeStruct((1, d), F32)],
        compiler_params=_cp(("arbitrary",)))(*dy_list, *w_list, h, g, dh_in)


def _merge_bwd_call(dhb, gates, bm, bs, wo, wbm, wbs, name):
    s, d = bm.shape
    tm = min(TM_SMALL, s)

    def body(dh_ref, g_ref, bm_ref, bs_ref, wo_ref, wbm_ref, wbs_ref, dg_ref, dbm_ref, dbs_ref, dom_ref, dos_ref):
        dmg = _dot_nt(dh_ref[...], wo_ref[...])
        s1 = _sigmoid(g_ref[:, :d])
        s2 = _sigmoid(g_ref[:, d:])
        dg_ref[:, :d] = (dmg * bm_ref[...] * (s1 * (1.0 - s1))).astype(BF16)
        dg_ref[:, d:] = (dmg * bs_ref[...] * (s2 * (1.0 - s2))).astype(BF16)
        dbm = (dmg * s1).astype(BF16)
        dbs = (dmg * s2).astype(BF16)
        dbm_ref[...] = dbm
        dbs_ref[...] = dbs
        dom_ref[...] = _dot_nt(dbm, wbm_ref[...]).astype(BF16)
        dos_ref[...] = _dot_nt(dbs, wbs_ref[...]).astype(BF16)

    wm = wbm.shape[0]
    return pl.pallas_call(
        body, name=name, grid=(s // tm,),
        in_specs=[_rows(tm, d), _rows(tm, 2 * d), _rows(tm, d), _rows(tm, d),
                  _whole(wo.shape), _whole(wbm.shape), _whole(wbs.shape)],
        out_specs=[_rows(tm, 2 * d), _rows(tm, d), _rows(tm, d), _rows(tm, wm), _rows(tm, SB_WIDTH)],
        out_shape=[jax.ShapeDtypeStruct((s, 2 * d), BF16), jax.ShapeDtypeStruct((s, d), BF16),
                   jax.ShapeDtypeStruct((s, d), BF16), jax.ShapeDtypeStruct((s, wm), BF16),
                   jax.ShapeDtypeStruct((s, SB_WIDTH), BF16)],
        compiler_params=_cp(("parallel",)))(dhb, gates, bm, bs, wo, wbm, wbs)


def _mla_prep_bwd_call(cq, ckv, krope, pos, freq, sign, g_ql, g_kvl, g_qh, g_kh, wq, wkv, dq, dk, dv, name):
    s = cq.shape[0]
    tm = min(TM_SMALL, s)
    width = HEADS * HEAD_PAD

    def body(cq_ref, ckv_ref, kr_ref, pos_ref, freq_ref, sign_ref, gql_ref, gkvl_ref, gqh_ref, gkh_ref,
             wq_ref, wkv_ref, dq_ref, dk_ref, dv_ref,
             dcq_ref, dckv_ref, dkr_ref, dwq_ref, dwkv_ref, dgql_ref, dgkvl_ref, dgqh_ref, dgkh_ref, dqr_ref, dkv_ref):
        @pl.when(pl.program_id(0) == 0)
        def _():
            for ref in (dwq_ref, dwkv_ref, dgql_ref, dgkvl_ref, dgqh_ref, dgkh_ref):
                ref[...] = jnp.zeros_like(ref)

        cosv, ssv = _rope_tables(pos_ref, freq_ref, sign_ref)
        xq = cq_ref[...]
        rq = _rstd(xq, Q_LORA)
        cqn = ((xq * rq) * gql_ref[...]).astype(BF16)
        qr = _dot(cqn, wq_ref[...])
        xk = ckv_ref[...]
        rk = _rstd(xk, KV_LORA)
        ckvn = ((xk * rk) * gkvl_ref[...]).astype(BF16)
        kv = _dot(ckvn, wkv_ref[...])
        kr = kr_ref[...]
        lane = _lane((tm, HEAD_PAD))
        dkr = jnp.zeros((tm, HEAD_PAD), F32)
        dgqh = jnp.zeros((1, HEAD_PAD), F32)
        dgkh = jnp.zeros((1, HEAD_PAD), F32)
        for h in range(HEADS):
            sl = slice(h * HEAD_PAD, (h + 1) * HEAD_PAD)
            x = qr[:, sl]
            dx, dgh = _head_bwd(x, _rstd(x, MLA_QK), gqh_ref[...], cosv, ssv, dq_ref[:, sl])
            dqr_ref[:, sl] = dx.astype(BF16)
            dgqh = dgqh + dgh
            x = jnp.where(lane < MLA_NOPE, kv[:, sl], kr)
            dx, dgh = _head_bwd(x, _rstd(x, MLA_QK), gkh_ref[...], cosv, ssv, dk_ref[:, sl])
            dgkh = dgkh + dgh
            dkr = dkr + jnp.where(lane >= MLA_NOPE, dx, 0.0)
            dkv_ref[:, sl] = jnp.where(lane < MLA_NOPE, dx, dv_ref[:, sl]).astype(BF16)
        dgqh_ref[...] += dgqh
        dgkh_ref[...] += dgkh
        dkr_ref[...] = dkr.astype(BF16)
        dqr = dqr_ref[...]
        dkvb = dkv_ref[...]
        dwq_ref[...] += _dot_tn(cqn, dqr)
        dwkv_ref[...] += _dot_tn(ckvn, dkvb)
        dcqn = _dot_nt(dqr, wq_ref[...])
        dgql_ref[...] += jnp.sum(dcqn * (xq * rq), axis=0, keepdims=True)
        dcq_ref[...] = _rms_bwd(xq, rq, gql_ref[...], dcqn, Q_LORA).astype(BF16)
        dckvn = _dot_nt(dkvb, wkv_ref[...])
        dgkvl_ref[...] += jnp.sum(dckvn * (xk * rk), axis=0, keepdims=True)
        dckv_ref[...] = _rms_bwd(xk, rk, gkvl_ref[...], dckvn, KV_LORA).astype(BF16)

    vec = lambda n: jax.ShapeDtypeStruct((1, n), F32)
    outs = pl.pallas_call(
        body, name=name, grid=(s // tm,),
        in_specs=[_rows(tm, Q_LORA), _rows(tm, KV_LORA), _rows(tm, HEAD_PAD), _rows(tm, 1),
                  _whole((1, HEAD_PAD)), _whole((1, HEAD_PAD)), _whole((1, Q_LORA)), _whole((1, KV_LORA)),
                  _whole((1, HEAD_PAD)), _whole((1, HEAD_PAD)), _whole((Q_LORA, width)), _whole((KV_LORA, width)),
                  _rows(tm, width), _rows(tm, width), _rows(tm, width)],
        out_specs=[_rows(tm, Q_LORA), _rows(tm, KV_LORA), _rows(tm, HEAD_PAD), _whole((Q_LORA, width)),
                   _whole((KV_LORA, width)), _whole((1, Q_LORA)), _whole((1, KV_LORA)), _whole((1, HEAD_PAD)),
                   _whole((1, HEAD_PAD)), _rows(tm, width), _rows(tm, width)],
        out_shape=[jax.ShapeDtypeStruct((s, Q_LORA), BF16), jax.ShapeDtypeStruct((s, KV_LORA), BF16),
                   jax.ShapeDtypeStruct((s, HEAD_PAD), BF16), jax.ShapeDtypeStruct((Q_LORA, width), F32),
                   jax.ShapeDtypeStruct((KV_LORA, width), F32), vec(Q_LORA), vec(KV_LORA), vec(HEAD_PAD), vec(HEAD_PAD),
                   jax.ShapeDtypeStruct((s, width), BF16), jax.ShapeDtypeStruct((s, width), BF16)],
        compiler_params=_cp(("arbitrary",)))(cq, ckv, krope, pos, freq, sign, g_ql, g_kvl, g_qh, g_kh, wq, wkv, dq, dk, dv)
    return outs[:9]


def _tn_call(a, b, name, shard_cols=None):
    s, ka = a.shape
    nb = b.shape[1]
    ti = _pick(ka, (512, 256, 128))
    if shard_cols is not None:
        tj = shard_cols
    else:
        tj = nb if nb <= TN_MAX_COLS else _pick(nb, (2176, 1024, 512, 256, 128))
    ts = min(1024 if tj <= 1024 else 512, s)
    ns = s // ts

    def body(a_ref, b_ref, o_ref, acc_ref):
        part = _dot_tn(a_ref[...].astype(BF16), b_ref[...].astype(BF16))

        @pl.when(pl.program_id(2) == 0)
        def _():
            acc_ref[...] = part

        @pl.when(pl.program_id(2) != 0)
        def _():
            acc_ref[...] += part

        @pl.when(pl.program_id(2) == ns - 1)
        def _():
            o_ref[...] = acc_ref[...].astype(o_ref.dtype)

    if shard_cols is None:
        out_spec = pl.BlockSpec((ti, tj), lambda i, j, t: (i, j))
        out_shape = jax.ShapeDtypeStruct((ka, nb), BF16)
    else:
        out_spec = pl.BlockSpec((None, ti, tj), lambda i, j, t: (j, i, 0))
        out_shape = jax.ShapeDtypeStruct((nb // tj, ka, tj), BF16)
    return pl.pallas_call(
        body, name=name, grid=(ka // ti, nb // tj, ns),
        in_specs=[pl.BlockSpec((ts, ti), lambda i, j, t: (t, i)), pl.BlockSpec((ts, tj), lambda i, j, t: (t, j))],
        out_specs=out_spec, out_shape=out_shape, scratch_shapes=[pltpu.VMEM((ti, tj), F32)],
        compiler_params=_cp(("parallel", "parallel", "arbitrary")))(a, b)


def _sum_call(parts, out_dtype, name):
    n, r, w = parts.shape
    tr = _row_tile(r)

    def body(p_ref, o_ref):
        acc = p_ref[0].astype(F32)
        for k in range(1, n):
            acc = acc + p_ref[k].astype(F32)
        o_ref[...] = acc.astype(out_dtype)

    return pl.pallas_call(
        body, name=name, grid=(r // tr,),
        in_specs=[pl.BlockSpec((n, tr, w), lambda i: (0, i, 0))], out_specs=_rows(tr, w),
        out_shape=jax.ShapeDtypeStruct((r, w), out_dtype), compiler_params=_cp(("parallel",)))(parts)


def _chip_sum_call(by_chip, core, name):
    n, r, w = by_chip.shape
    tr = _row_tile(r)
    nblk = r // tr

    def body(c_ref, p_ref, o_ref):
        acc = p_ref[0].astype(F32)
        for k in range(1, n):
            acc = acc + p_ref[k].astype(F32)
        o_ref[...] = acc

    return pl.pallas_call(
        body, name=name,
        grid_spec=pltpu.PrefetchScalarGridSpec(
            num_scalar_prefetch=1, grid=(nblk,),
            in_specs=[pl.BlockSpec((n, tr, w), lambda i, c_ref: (0, i, 0))],
            out_specs=pl.BlockSpec((tr, w), lambda i, c_ref: (c_ref[0] * nblk + i, 0))),
        out_shape=jax.ShapeDtypeStruct((2 * r, w), F32),
        compiler_params=_cp(("parallel",)))(core.reshape(1).astype(jnp.int32), by_chip)


def _pair_sum_call(full, other, core, out_dtype, name):
    n, r, w = other.shape
    tr = _row_tile(r)
    nblk = r // tr

    def body(c_ref, a_ref, b_ref, o_ref):
        o_ref[...] = (a_ref[...].astype(F32) + b_ref[...].astype(F32)).astype(out_dtype)

    spec = pl.BlockSpec((None, tr, w), lambda k, i, c_ref: (k, i, 0))
    return pl.pallas_call(
        body, name=name,
        grid_spec=pltpu.PrefetchScalarGridSpec(
            num_scalar_prefetch=1, grid=(n, nblk),
            in_specs=[pl.BlockSpec((None, tr, w), lambda k, i, c_ref: (k, c_ref[0] * nblk + i, 0)), spec],
            out_specs=spec),
        out_shape=jax.ShapeDtypeStruct((n, r, w), out_dtype),
        compiler_params=_cp(("parallel", "parallel")))(core.reshape(1).astype(jnp.int32), full, other)


def _adamw_call(w, g, row0, m, v, name):
    r, c = w.shape
    tr = _pick(math.gcd(r, row0) if row0 else r, (256, 128, 64, 32, 16, 8))
    off = row0 // tr

    def body(w_ref, g_ref, m_ref, v_ref, g_out_ref, d_ref, nm_ref, nv_ref):
        gg = g_ref[...]
        g_out_ref[...] = gg
        nm = ADAM_B1 * m_ref[...] + (1.0 - ADAM_B1) * gg
        nv = ADAM_B2 * v_ref[...] + (1.0 - ADAM_B2) * (gg * gg)
        m_hat = nm / (1.0 - ADAM_B1 ** ADAM_STEP)
        v_hat = nv / (1.0 - ADAM_B2 ** ADAM_STEP)
        d_ref[...] = -ADAM_LR * (m_hat / (jnp.sqrt(v_hat) + ADAM_EPS) + ADAM_WD * w_ref[...])
        nm_ref[...] = nm
        nv_ref[...] = nv

    out = jax.ShapeDtypeStruct((r, c), F32)
    g_spec = pl.BlockSpec((tr, c), lambda i: (off + i, 0))
    return pl.pallas_call(
        body, name=name, grid=(r // tr,), in_specs=[_rows(tr, c), g_spec, _rows(tr, c), _rows(tr, c)],
        out_specs=[_rows(tr, c)] * 4, out_shape=[out, out, out, out], compiler_params=_cp(("parallel",)))(w, g, m, v)


def _position():
    x, y, c = lax.axis_index("x"), lax.axis_index("y"), lax.axis_index("c")
    chips = [(1 - x, y), (x, 1 - y), (1 - x, 1 - y)]
    return x, y, c, chips


def _gather_rider(parts):
    n = len(parts)
    pairs = [(j, k) for j in range(3) for k in range(n)]

    def piece(out_refs, k, chip, core):
        half = parts[k].shape[0] // 2
        return out_refs[k].at[2 * chip[0] + chip[1], pl.ds(core * half, half), :]

    def over_ici(in_refs, out_refs, sems, j, k):
        x, y, c, chips = _position()
        half = parts[k].shape[0] // 2
        return pltpu.make_async_remote_copy(
            src_ref=in_refs[k].at[pl.ds(c * half, half), :], dst_ref=piece(out_refs, k, (x, y), c),
            send_sem=sems[0].at[n * j + k], recv_sem=sems[1].at[n * j + k], device_id=(*chips[j], c), device_id_type=MESH)

    def to_sibling(out_refs, sems, j, k):
        x, y, c, chips = _position()
        landed = piece(out_refs, k, chips[j], c)
        return pltpu.make_async_remote_copy(
            src_ref=landed, dst_ref=landed, send_sem=sems[2].at[n * j + k], recv_sem=sems[3].at[n * j + k],
            device_id=(x, y, 1 - c), device_id_type=MESH)

    def start(in_refs, out_refs, sems):
        for j, k in pairs:
            over_ici(in_refs, out_refs, sems, j, k).start()

    def finish(in_refs, out_refs, sems):
        for j, k in pairs:
            over_ici(in_refs, out_refs, sems, j, k).wait_recv()
            to_sibling(out_refs, sems, j, k).start()
        for j, k in pairs:
            to_sibling(out_refs, sems, j, k).wait_recv()
        for j, k in pairs:
            over_ici(in_refs, out_refs, sems, j, k).wait_send()
            to_sibling(out_refs, sems, j, k).wait_send()

    return _Rider(list(parts), [jax.ShapeDtypeStruct((N_CHIPS,) + p.shape, p.dtype) for p in parts], [3 * n] * 4,
                  start, finish)


def _scatter_rider(parts):
    n = len(parts)
    pairs = [(j, k) for j in range(3) for k in range(n)]

    def copy(in_refs, out_refs, sems, j, k):
        x, y, c, chips = _position()
        return pltpu.make_async_remote_copy(
            src_ref=in_refs[k].at[2 * chips[j][0] + chips[j][1]], dst_ref=out_refs[k].at[2 * x + y],
            send_sem=sems[0].at[n * j + k], recv_sem=sems[1].at[n * j + k], device_id=(*chips[j], c), device_id_type=MESH)

    def start(in_refs, out_refs, sems):
        for j, k in pairs:
            copy(in_refs, out_refs, sems, j, k).start()

    def finish(in_refs, out_refs, sems):
        for j, k in pairs:
            copy(in_refs, out_refs, sems, j, k).wait()

    return _Rider(list(parts), [jax.ShapeDtypeStruct(p.shape, p.dtype) for p in parts], [3 * n] * 2, start, finish)


def _exchange_call(rider, name):
    n, m = len(rider.ins), len(rider.out_shape)

    def body(*refs):
        rider.start(refs[:n], refs[n:n + m], refs[n + m:])
        rider.finish(refs[:n], refs[n:n + m], refs[n + m:])

    return pl.pallas_call(
        body, name=name, in_specs=[HBM] * n, out_specs=[HBM] * m, out_shape=rider.out_shape,
        scratch_shapes=[pltpu.SemaphoreType.DMA((k,)) for k in rider.sems])(*rider.ins)


def _pair_send_call(parts, name):
    n = len(parts)

    def body(*refs):
        in_refs, out_refs = refs[:n], refs[n:2 * n]
        send_sems, recv_sems = refs[2 * n:]
        x, y, c, _ = _position()
        copies = []
        for k in range(n):
            half = parts[k].shape[1] // 2
            cp = pltpu.make_async_remote_copy(
                src_ref=in_refs[k].at[:, pl.ds((1 - c) * half, half), :], dst_ref=out_refs[k],
                send_sem=send_sems.at[k], recv_sem=recv_sems.at[k], device_id=(x, y, 1 - c), device_id_type=MESH)
            cp.start()
            copies.append(cp)
        for cp in copies:
            cp.wait()

    sems = pltpu.SemaphoreType.DMA((n,))
    return pl.pallas_call(
        body, name=name, in_specs=[HBM] * n, out_specs=[HBM] * n,
        out_shape=[jax.ShapeDtypeStruct((p.shape[0], p.shape[1] // 2, p.shape[2]), p.dtype) for p in parts],
        scratch_shapes=[sems, sems])(*parts)


def _pair_swap_call(parts, name):
    n = len(parts)

    def body(*refs):
        out_refs = refs[n:2 * n]
        send_sems, recv_sems = refs[2 * n:]
        x, y, c, _ = _position()
        copies = []
        for k in range(n):
            half = parts[k].shape[0] // 2
            mine = out_refs[k].at[pl.ds(c * half, half), :]
            cp = pltpu.make_async_remote_copy(
                src_ref=mine, dst_ref=mine, send_sem=send_sems.at[k], recv_sem=recv_sems.at[k],
                device_id=(x, y, 1 - c), device_id_type=MESH)
            cp.start()
            copies.append(cp)
        for cp in copies:
            cp.wait()

    sems = pltpu.SemaphoreType.DMA((n,))
    return pl.pallas_call(
        body, name=name, in_specs=[HBM] * n, out_specs=[HBM] * n,
        out_shape=[jax.ShapeDtypeStruct(p.shape, p.dtype) for p in parts],
        input_output_aliases={k: k for k in range(n)},
        scratch_shapes=[sems, sems])(*parts)


def _all_gather_small_call(block, name):
    r, w = block.shape

    def body(in_ref, out_ref, send_sems, recv_sems, local_sem):
        x, y, c, _ = _position()
        me = 4 * x + 2 * y + c
        own = pltpu.make_async_copy(in_ref, out_ref.at[me], local_sem)
        own.start()
        copies = []
        for k in range(1, 8):
            peer = (x ^ (k >> 2), y ^ ((k >> 1) & 1), c ^ (k & 1))
            cp = pltpu.make_async_remote_copy(
                src_ref=in_ref, dst_ref=out_ref.at[me], send_sem=send_sems.at[k - 1], recv_sem=recv_sems.at[k - 1],
                device_id=peer, device_id_type=MESH)
            cp.start()
            copies.append(cp)
        for cp in copies:
            cp.wait()
        own.wait()

    return pl.pallas_call(
        body, name=name, in_specs=[HBM], out_specs=HBM,
        out_shape=jax.ShapeDtypeStruct((8, r, w), block.dtype),
        scratch_shapes=[pltpu.SemaphoreType.DMA((7,)), pltpu.SemaphoreType.DMA((7,)), pltpu.SemaphoreType.DMA])(block)


BIG = {
    "ffn1_w_in": ((D_MODEL, 2 * D_FF), 1), "ffn1_w_out": ((D_FF, D_MODEL), 0),
    "w_in": ((D_MODEL, 4256), 1), "w_q_up": ((Q_LORA, HEADS * MLA_QK), 1), "w_kv_up": ((KV_LORA, 1024), 1),
    "w_branch_mla": ((512, D_MODEL), 1), "w_branch_sb": ((SB_WIDTH, D_MODEL), 1), "w_out": ((D_MODEL, D_MODEL), 0),
    "ffn2_w_in": ((D_MODEL, 2 * D_FF), 1), "ffn2_w_out": ((D_FF, D_MODEL), 0),
    "w_ple_gate": ((D_MODEL, D_MODEL), 0), "w_ple_proj": ((PLE_DIM, D_MODEL), 1),
}
GAINS = {"ffn1_norm": 1024, "mix_norm": 1024, "q_latent_norm": 384, "kv_latent_norm": 256, "q_head_norm": 96,
         "k_head_norm": 96, "ffn2_norm": 1024, "ple_norm": 1024}
WEIGHT_ORDER = ["ffn1_norm", "ffn1_w_in", "ffn1_w_out", "mix_norm", "w_in", "q_latent_norm", "w_q_up",
                "kv_latent_norm", "w_kv_up", "q_head_norm", "k_head_norm", "w_branch_mla", "w_branch_sb", "w_out",
                "ffn2_norm", "ffn2_w_in", "ffn2_w_out", "ple_norm", "w_ple_gate", "w_ple_proj"]


def _shard_shape(name):
    (r, c), axis = BIG[name]
    return (r // N_CHIPS, c) if axis == 0 else (r, c // N_CHIPS)


GATHER_GROUPS = [
    [("ffn1_w_in",), ("ffn1_w_out",)],
    [("w_in",), ("w_out",), ("w_kv_up", "w_branch_mla", "w_branch_sb"), ("w_q_up",)],
    [("ffn2_w_in",), ("ffn2_w_out", "w_ple_gate"), ("w_ple_proj",)],
]
REDUCE_GROUPS = [
    [("ffn2_w_in",), ("ffn2_w_out", "w_out", "w_ple_gate"), ("w_branch_mla", "w_branch_sb", "w_ple_proj")],
    [("w_in",), ("w_kv_up",), ("w_q_up",)],
    [("ffn1_w_in",), ("ffn1_w_out",)],
]


def _join_parts(shards, group):
    return [shards[part[0]] if len(part) == 1 else jnp.concatenate([shards[n] for n in part], axis=-2) for part in group]


def _part_rows(group):
    where = {}
    for k, part in enumerate(group):
        at = 0
        for n in part:
            where[n] = (k, at)
            at += _shard_shape(n)[0]
    return where


def _split_parts(parts, group):
    return {n: parts[k][..., at:at + _shard_shape(n)[0], :] for n, (k, at) in _part_rows(group).items()}


def _to_shards(name, full):
    (r, c), axis = BIG[name]
    if axis == 0:
        return full.reshape(N_CHIPS, r // N_CHIPS, c)
    return full.reshape(r, N_CHIPS, c // N_CHIPS).transpose(1, 0, 2)


def _from_shards(name, shards):
    (r, c), axis = BIG[name]
    if axis == 0:
        return shards.reshape(r, c)
    return shards.transpose(1, 0, 2).reshape(r, c)


def _relayout_w_in(w):
    d = w.shape[0]
    z = lambda n: jnp.zeros((d, n), w.dtype)
    return jnp.concatenate([w[:, :640], z(MLA_NOPE), w[:, 640:672], z(HEAD_PAD - MLA_QK), w[:, 672:]], axis=1)


def _unlayout_w_in(g):
    return jnp.concatenate([g[:, :640], g[:, 640 + MLA_NOPE:640 + MLA_QK], g[:, 768:]], axis=1)


def _pad_heads(v):
    lead = v.shape[:-1]
    return jnp.pad(v.reshape(lead + (HEADS, MLA_QK)), [(0, 0)] * len(lead) + [(0, 0), (0, HEAD_PAD - MLA_QK)]).reshape(
        lead + (HEADS * HEAD_PAD,))


def _halves(w):
    n = w.shape[1] // 2
    return [w[:, :n], w[:, n:]]


def _step(x, p, pos, tgt, gains, weights, dist):
    d = D_MODEL
    full = dict(weights) if dist is None else {}
    reduced = {}

    def gather_rider(g):
        if dist is None:
            return None, None
        mine = _join_parts(weights, GATHER_GROUPS[g])
        return mine, _gather_rider(mine)

    def gathered(g, mine, others):
        if dist is not None:
            parts = [lax.dynamic_update_slice_in_dim(o, m[None], dist[0], axis=0) for o, m in zip(others, mine)]
            for n, shards in _split_parts(parts, GATHER_GROUPS[g]).items():
                full[n] = _from_shards(n, shards)

    def reduce_before(g):
        if dist is None:
            return None, None
        group = REDUCE_GROUPS[g]
        shards = {n: grads[n] if grads[n].ndim == 3 else _to_shards(n, grads[n].astype(BF16)) for part in group for n in part}
        partial = _join_parts(shards, group)
        from_sibling = _pair_send_call(partial, "grads%d_pair_send" % g)
        pair_sum = [_pair_sum_call(a, b, dist[1], BF16, "grads%d_pair_sum_%d" % (g, k))
                    for k, (a, b) in enumerate(zip(partial, from_sibling))]
        return pair_sum, _scatter_rider(pair_sum)

    def reduce_after(g, pair_sum, by_chip):
        if dist is not None:
            chip, core = dist
            by_chip = [lax.dynamic_update_slice_in_dim(t, lax.dynamic_slice_in_dim(o, chip, 1, axis=0), chip, axis=0)
                       for t, o in zip(by_chip, pair_sum)]
            bufs = _pair_swap_call([_chip_sum_call(t, core, "grads%d_chip_sum_%d" % (g, k)) for k, t in enumerate(by_chip)],
                                   "grads%d_pair_swap" % g)
            for n, (k, row0) in _part_rows(REDUCE_GROUPS[g]).items():
                reduced[n] = (bufs[k], row0)

    mine, rider = gather_rider(0)
    if dist is not None:
        gathered(0, mine, _exchange_call(rider, "gather0"))
    wts = full
    inv_freq = ROPE_BASE ** (-jnp.arange(0, MLA_ROPE, 2, dtype=F32) / MLA_ROPE)
    zeros = lambda n: jnp.zeros((n,), F32)
    freq = jnp.concatenate([zeros(MLA_NOPE), inv_freq, inv_freq, zeros(HEAD_PAD - MLA_QK)])[None]
    sign = jnp.concatenate([zeros(MLA_NOPE), -jnp.ones((16,), F32), jnp.ones((16,), F32), zeros(HEAD_PAD - MLA_QK)])[None]
    pad_gain = lambda g: jnp.pad(g, ((0, 0), (0, HEAD_PAD - MLA_QK)))
    g_qh, g_kh = pad_gain(gains["q_head_norm"]), pad_gain(gains["k_head_norm"])

    u1 = _norm_call(x, gains["ffn1_norm"], "norm_ffn1")
    mine, rider = gather_rider(1)
    (a1, b1, hm1), got = _ffn_in_call(u1, wts["ffn1_w_in"], "ffn1_in", rider)
    gathered(1, mine, got)
    h1 = _ffn_out_call(hm1, wts["ffn1_w_out"], x, "ffn1_out")
    w_in = _relayout_w_in(wts["w_in"])
    wq = _pad_heads(wts["w_q_up"])
    wkv = wts["w_kv_up"]
    wbm = jnp.pad(wts["w_branch_mla"].reshape(HEADS, 64, d), ((0, 0), (64, 0), (0, 0))).reshape(HEADS * HEAD_PAD, d)
    wbs, wo = wts["w_branch_sb"], wts["w_out"]
    um = _norm_call(h1, gains["mix_norm"], "norm_mix")
    cq, ckv, krope, sbq, sbk, sbv, gates = _mix_in_call(um, w_in, "mix_in")
    prep_args = (cq, ckv, krope, pos, freq, sign, gains["q_latent_norm"], gains["kv_latent_norm"], g_qh, g_kh, wq, wkv)
    q, k, v = _mla_prep_call(*prep_args, "mla_prep")
    mine, rider = gather_rider(2)
    (om, lse), got = _mla_fwd_call(q, k, v, "mla_fwd", rider)
    gathered(2, mine, got)
    osb, tot = _sb_fwd_call(sbq, sbk, sbv, "sb_fwd")
    h2, bm, bs, mg = _merge_out_call(om, osb, gates, h1, wbm, wbs, wo, "merge_out")
    u2 = _norm_call(h2, gains["ffn2_norm"], "norm_ffn2")
    (a2, b2, hm2), _ = _ffn_in_call(u2, wts["ffn2_w_in"], "ffn2_in")
    h3 = _ffn_out_call(hm2, wts["ffn2_w_out"], h2, "ffn2_out")

    grads, gg = {}, {}
    dh3, dh3s, un, dgl, dpp, gg["ple_norm"], sq = _ple_call(
        h3, gains["ple_norm"], wts["w_ple_gate"], p, wts["w_ple_proj"], tgt, "ple")
    grads["w_ple_gate"] = _tn_call(un, dgl, "dw_ple_gate")
    grads["w_ple_proj"] = _tn_call(p, dpp, "dw_ple_proj")

    (da2, db2), _ = _ffn_bwd_a_call(dh3s, a2, b2, wts["ffn2_w_out"], "ffn2_bwd_act")
    grads["ffn2_w_out"] = _tn_call(hm2, dh3s, "dw_ffn2_out")
    grads["ffn2_w_in"] = jnp.concatenate([_tn_call(u2, da2, "dw_ffn2_in_a", shard_cols=D_FF // 2),
                                          _tn_call(u2, db2, "dw_ffn2_in_b", shard_cols=D_FF // 2)], axis=0)
    dh2, dh2b, gg["ffn2_norm"] = _norm_bwd_call([da2, db2], _halves(wts["ffn2_w_in"]), h2, gains["ffn2_norm"], dh3,
                                                "ffn2_bwd_norm", half_out=False)

    dgates, dbm, dbs, dom, dos = _merge_bwd_call(dh2b, gates, bm, bs, wo, wbm, wbs, "merge_bwd")
    grads["w_out"] = _tn_call(mg, dh2b, "dw_out")
    grads["w_branch_mla"] = _tn_call(om, dbm, "dw_branch_mla").reshape(HEADS, HEAD_PAD, d)[:, 64:, :].reshape(512, d)
    grads["w_branch_sb"] = _tn_call(osb, dbs, "dw_branch_sb")
    pair_sum, rider = reduce_before(0)
    (dq, dk, dv), got = _mla_bwd_call(q, k, v, om, dom, lse, "mla_bwd", rider)
    reduce_after(0, pair_sum, got)
    dsq, dsk, dsv = _sb_bwd_call(sbq, sbk, sbv, dos, tot, "sb_bwd")
    (dcq, dckv, dkr, dwq, grads["w_kv_up"], gg["q_latent_norm"], gg["kv_latent_norm"], dgqh, dgkh) = \
        _mla_prep_bwd_call(*prep_args, dq, dk, dv, "mla_prep_bwd")
    grads["w_q_up"] = dwq.reshape(Q_LORA, HEADS, HEAD_PAD)[:, :, :MLA_QK].reshape(Q_LORA, HEADS * MLA_QK)
    gg["q_head_norm"], gg["k_head_norm"] = dgqh[:, :MLA_QK], dgkh[:, :MLA_QK]
    dproj = jnp.concatenate([dcq, dckv, dkr, dsq, dsk.astype(BF16), dsv.astype(BF16), dgates], axis=1)
    grads["w_in"] = _unlayout_w_in(_tn_call(um, dproj, "dw_in"))
    dh1, dh1s, gg["mix_norm"] = _norm_bwd_call([dproj], [w_in], h1, gains["mix_norm"], dh2, "mix_bwd_norm", half_out=True)

    pair_sum, rider = reduce_before(1)
    (da1, db1), got = _ffn_bwd_a_call(dh1s, a1, b1, wts["ffn1_w_out"], "ffn1_bwd_act", rider)
    reduce_after(1, pair_sum, got)
    grads["ffn1_w_out"] = _tn_call(hm1, dh1s, "dw_ffn1_out")
    grads["ffn1_w_in"] = jnp.concatenate([_tn_call(u1, da1, "dw_ffn1_in_a", shard_cols=D_FF // 2),
                                          _tn_call(u1, db1, "dw_ffn1_in_b", shard_cols=D_FF // 2)], axis=0)
    dx, _, gg["ffn1_norm"] = _norm_bwd_call([da1, db1], _halves(wts["ffn1_w_in"]), x, gains["ffn1_norm"], dh1,
                                            "ffn1_bwd_norm", half_out=False)
    pair_sum, rider = reduce_before(2)
    if dist is not None:
        reduce_after(2, pair_sum, _exchange_call(rider, "grads2_chip_scatter"))
    return sq, dx, gg, (grads if dist is None else reduced)


def kernel(x, p, positions, ffn1_norm, ffn1_w_in, ffn1_w_out, mix_norm, w_in, q_latent_norm, w_q_up, kv_latent_norm, w_kv_up, q_head_norm, k_head_norm, w_branch_mla, w_branch_sb, w_out, ffn2_norm, ffn2_w_in, ffn2_w_out, ple_norm, w_ple_gate, w_ple_proj, loss_target, m_ffn1_norm, m_ffn1_w_in, m_ffn1_w_out, m_mix_norm, m_w_in, m_q_latent_norm, m_w_q_up, m_kv_latent_norm, m_w_kv_up, m_q_head_norm, m_k_head_norm, m_w_branch_mla, m_w_branch_sb, m_w_out, m_ffn2_norm, m_ffn2_w_in, m_ffn2_w_out, m_ple_norm, m_w_ple_gate, m_w_ple_proj, v_ffn1_norm, v_ffn1_w_in, v_ffn1_w_out, v_mix_norm, v_w_in, v_q_latent_norm, v_w_q_up, v_kv_latent_norm, v_w_kv_up, v_q_head_norm, v_k_head_norm, v_w_branch_mla, v_w_branch_sb, v_w_out, v_ffn2_norm, v_ffn2_w_in, v_ffn2_w_out, v_ple_norm, v_w_ple_gate, v_w_ple_proj):
    given = dict(locals())
    w_shard = {n: given[n][0] for n in WEIGHT_ORDER}
    m_shard = {n: given["m_" + n][0] for n in WEIGHT_ORDER}
    v_shard = {n: given["v_" + n][0] for n in WEIGHT_ORDER}
    gains = {n: w_shard[n][None] for n in GAINS}

    chip = 2 * lax.axis_index("x") + lax.axis_index("y")
    sq, dx, gain_grads, reduced = _step(x[0], p[0, 0], positions.reshape(-1, 1), loss_target[0], gains,
                                        {n: w_shard[n].astype(BF16) for n in BIG}, (chip, lax.axis_index("c")))

    rows = [jnp.pad(gain_grads[n], ((0, 0), (0, D_MODEL - GAINS[n]))) for n in GAINS] + [sq]
    gain_block = jnp.concatenate(rows + [jnp.zeros((16 - len(rows), D_MODEL), F32)], axis=0)
    gain_sum = _sum_call(_all_gather_small_call(gain_block, "gains_all_gather"), F32, "gains_sum")
    loss = 0.5 * jnp.sum(gain_sum[len(GAINS)]) / D_MODEL

    outs = {"grad": {}, "delta": {}, "new_m": {}, "new_v": {}}
    gain_pack = lambda t: jnp.concatenate([jnp.pad(t[n][None], ((0, 0), (0, D_MODEL - GAINS[n]))) for n in GAINS], axis=0)
    packed = _adamw_call(gain_pack(w_shard), gain_sum, 0, gain_pack(m_shard), gain_pack(v_shard), "adamw_gains")
    for i, n in enumerate(GAINS):
        for kind, t in zip(("grad", "delta", "new_m", "new_v"), packed):
            outs[kind][n] = t[i, :GAINS[n]][None]
    for n in BIG:
        buf, row0 = reduced[n]
        for kind, t in zip(("grad", "delta", "new_m", "new_v"),
                           _adamw_call(w_shard[n], buf, row0, m_shard[n], v_shard[n], "adamw_" + n)):
            outs[kind][n] = t[None]

    return (loss, dx[None], *[outs["grad"][n] for n in WEIGHT_ORDER], *[outs["delta"][n] for n in WEIGHT_ORDER],
            *[outs["new_m"][n] for n in WEIGHT_ORDER], *[outs["new_v"][n] for n in WEIGHT_ORDER])
```

```python
import collections
import functools
import math

import jax
import jax.numpy as jnp
from jax import lax
from jax.experimental import pallas as pl
from jax.experimental.pallas import tpu as pltpu

F32 = jnp.float32
BF16 = jnp.bfloat16
MESH = pl.DeviceIdType.MESH

D_MODEL = 1024
D_FF = 2816
PLE_DIM = 256
NORM_EPS = 1e-6
HEADS = 8
MLA_NOPE = 64
MLA_ROPE = 32
MLA_QK = 96
Q_LORA = 384
KV_LORA = 256
SB_WIDTH = 512
ROPE_BASE = 10000.0
HEAD_PAD = 128
N_CHIPS = 4

ADAM_LR = 0.001
ADAM_B1 = 0.9
ADAM_B2 = 0.999
ADAM_EPS = 1e-08
ADAM_WD = 0.01
ADAM_STEP = 10

SEG_CQ = (0, 384)
SEG_CKV = (384, 256)
SEG_KROPE = (640, 128)
SEG_SBQ = (768, 512)
SEG_SBK = (1280, 512)
SEG_SBV = (1792, 512)
SEG_GATES = (2304, 2048)
IN_COLS_PAD = 4352

TM = 512
TM_SMALL = 256
TQ = 256
MLA_FWD_BLOCKS = 4
MLA_BWD_BLOCKS = 4
SB_FWD_BLOCKS = 4
SB_BWD_BLOCKS = 2
SB_HEAD = 64
SB_SCALE = 0.125
COL_CHUNK = 256
TN_MAX_COLS = 2816
TN_OPERAND_BYTES = 34 * 1024 * 1024
MAX_ROW_TILE = 512
VMEM_LIMIT = 56 * 1024 * 1024

NT = (((1,), (1,)), ((), ()))
TN = (((0,), (0,)), ((), ()))


def _cp(sem):
    return pltpu.CompilerParams(dimension_semantics=sem, vmem_limit_bytes=VMEM_LIMIT)


def _rows(tm, w):
    return pl.BlockSpec((tm, w), lambda i: (i, 0))


def _whole(shape):
    return pl.BlockSpec(shape, lambda i: (0,) * len(shape))


def _dot(a, b):
    return jnp.dot(a, b, preferred_element_type=F32)


def _dot_nt(a, b):
    return lax.dot_general(a, b, NT, preferred_element_type=F32)


def _dot_tn(a, b):
    return lax.dot_general(a, b, TN, preferred_element_type=F32)


def _rstd(x, n):
    return lax.rsqrt(jnp.sum(x * x, axis=-1, keepdims=True) / n + NORM_EPS)


def _rms_bwd(x, r, g, dy, n):
    gy = dy * g
    return r * gy - x * ((r * r * r) * (jnp.sum(x * gy, axis=-1, keepdims=True) / n))


def _sigmoid(x):
    return jax.nn.sigmoid(x)


def _pick(n, cands):
    for c in cands:
        if n % c == 0:
            return c
    return n


def _row_tile(r):
    for t in range(min(r, MAX_ROW_TILE) // 16 * 16, 15, -16):
        if r % t == 0:
            return t
    return r


HBM = pl.BlockSpec(memory_space=pl.ANY)

_Rider = collections.namedtuple("_Rider", "ins out_shape sems start finish")


def _with_rider(body, rider, *, name, grid, in_specs, out_specs, out_shape, args, sem, scratch=()):
    if rider is None:
        return pl.pallas_call(body, name=name, grid=grid, in_specs=in_specs, out_specs=out_specs, out_shape=out_shape,
                              scratch_shapes=list(scratch), compiler_params=_cp(sem))(*args), None
    ni, no, nri, nro = len(in_specs), len(out_specs), len(rider.ins), len(rider.out_shape)

    def riding(*refs):
        ins, r_ins = refs[:ni], refs[ni:ni + nri]
        outs, r_outs = refs[ni + nri:ni + nri + no], refs[ni + nri + no:ni + nri + no + nro]
        scr = refs[ni + nri + no + nro:ni + nri + no + nro + len(scratch)]
        sems = refs[ni + nri + no + nro + len(scratch):]
        ids = [pl.program_id(a) for a in range(len(grid))]
        first = functools.reduce(jnp.logical_and, [i == 0 for i in ids])
        last = functools.reduce(jnp.logical_and, [i == g - 1 for i, g in zip(ids, grid)])

        @pl.when(first)
        def _():
            rider.start(r_ins, r_outs, sems)

        body(*ins, *outs, *scr)

        @pl.when(last)
        def _():
            rider.finish(r_ins, r_outs, sems)

    res = pl.pallas_call(
        riding, name=name, grid=grid, in_specs=list(in_specs) + [HBM] * nri, out_specs=list(out_specs) + [HBM] * nro,
        out_shape=list(out_shape) + list(rider.out_shape),
        scratch_shapes=list(scratch) + [pltpu.SemaphoreType.DMA((k,)) for k in rider.sems],
        compiler_params=_cp(("arbitrary",) * len(grid)))(*args, *rider.ins)
    return res[:no], res[no:]


def _norm_call(h, g, name):
    s, d = h.shape
    tm = min(TM, s)

    def body(h_ref, g_ref, u_ref):
        x = h_ref[...]
        u_ref[...] = ((x * _rstd(x, d)) * g_ref[...]).astype(BF16)

    return pl.pallas_call(
        body, name=name, grid=(s // tm,),
        in_specs=[_rows(tm, d), _whole((1, d))], out_specs=_rows(tm, d),
        out_shape=jax.ShapeDtypeStruct((s, d), BF16), compiler_params=_cp(("parallel",)))(h, g)


def _ffn_in_call(u, w, name, rider=None):
    s, d = u.shape
    n = w.shape[1] // 2
    tn = n // 2
    tm = min(TM, s)
    nj = n // tn

    def body(u_ref, wa_ref, wb_ref, a_ref, b_ref, hm_ref):
        uu = u_ref[...]
        a = _dot(uu, wa_ref[...])
        b = _dot(uu, wb_ref[...])
        a_ref[...] = a
        b_ref[...] = b
        hm_ref[...] = ((a * _sigmoid(a)) * b).astype(BF16)

    blk = pl.BlockSpec((tm, tn), lambda j, i: (i, j))
    return _with_rider(
        body, rider, name=name, grid=(nj, s // tm),
        in_specs=[pl.BlockSpec((tm, d), lambda j, i: (i, 0)),
                  pl.BlockSpec((d, tn), lambda j, i: (0, j)),
                  pl.BlockSpec((d, tn), lambda j, i: (0, j + nj))],
        out_specs=[blk, blk, blk],
        out_shape=[jax.ShapeDtypeStruct((s, n), F32), jax.ShapeDtypeStruct((s, n), F32),
                   jax.ShapeDtypeStruct((s, n), BF16)],
        args=(u, w, w), sem=("parallel", "parallel"))


def _ffn_out_call(hm, w, h, name, rider=None):
    s, n = hm.shape
    d = w.shape[1]
    tm = min(TM, s)

    def body(hm_ref, w_ref, h_ref, o_ref):
        o_ref[...] = h_ref[...] + 0.5 * _dot(hm_ref[...], w_ref[...])

    (out,), got = _with_rider(
        body, rider, name=name, grid=(s // tm,),
        in_specs=[_rows(tm, n), _whole((n, d)), _rows(tm, d)], out_specs=[_rows(tm, d)],
        out_shape=[jax.ShapeDtypeStruct((s, d), F32)], args=(hm, w, h), sem=("parallel",))
    return out, got


def _mix_in_call(u, w, name):
    s, d = u.shape
    tm = min(TM_SMALL, s)
    segs = [(SEG_CQ, F32), (SEG_CKV, F32), (SEG_KROPE, F32), (SEG_SBQ, BF16), (SEG_SBK, BF16),
            (SEG_SBV, BF16), (SEG_GATES, F32)]

    def body(u_ref, w_ref, *outs):
        uu = u_ref[...]
        for ((off, width), _), o_ref in zip(segs, outs):
            o_ref[...] = _dot(uu, w_ref[:, off:off + width]).astype(o_ref.dtype)

    return pl.pallas_call(
        body, name=name, grid=(s // tm,),
        in_specs=[_rows(tm, d), _whole((d, IN_COLS_PAD))],
        out_specs=[_rows(tm, width) for (_, width), _ in segs],
        out_shape=[jax.ShapeDtypeStruct((s, width), dt) for (_, width), dt in segs],
        compiler_params=_cp(("parallel",)))(u, w)


def _lane(shape):
    return lax.broadcasted_iota(jnp.int32, shape, len(shape) - 1)


def _rot_half(y):
    lane = _lane(y.shape)
    swapped = jnp.where(lane < MLA_NOPE + MLA_ROPE // 2, pltpu.roll(y, HEAD_PAD - 16, 1), pltpu.roll(y, 16, 1))
    return jnp.where((lane >= MLA_NOPE) & (lane < MLA_QK), swapped, 0.0)


def _rope_tables(pos_ref, freq_ref, sign_ref):
    ang = pos_ref[...].astype(F32) * freq_ref[...]
    return jnp.cos(ang), jnp.sin(ang) * sign_ref[...]


def _head_fwd(x, g, cosv, ssv):
    r = _rstd(x, MLA_QK)
    y = (x * r) * g
    return y * cosv + _rot_half(y) * ssv, r


def _head_bwd(x, r, g, cosv, ssv, dout):
    dy = dout * cosv + _rot_half(dout * ssv)
    return _rms_bwd(x, r, g, dy, MLA_QK), jnp.sum(dy * (x * r), axis=0, keepdims=True)


def _mla_prep_call(cq, ckv, krope, pos, freq, sign, g_ql, g_kvl, g_qh, g_kh, wq, wkv, name):
    s = cq.shape[0]
    tm = min(TM_SMALL, s)
    width = HEADS * HEAD_PAD

    def body(cq_ref, ckv_ref, kr_ref, pos_ref, freq_ref, sign_ref, gql_ref, gkvl_ref, gqh_ref, gkh_ref,
             wq_ref, wkv_ref, q_ref, k_ref, v_ref):
        cosv, ssv = _rope_tables(pos_ref, freq_ref, sign_ref)
        x = cq_ref[...]
        qr = _dot(((x * _rstd(x, Q_LORA)) * gql_ref[...]).astype(BF16), wq_ref[...])
        x = ckv_ref[...]
        kv = _dot(((x * _rstd(x, KV_LORA)) * gkvl_ref[...]).astype(BF16), wkv_ref[...])
        kr = kr_ref[...]
        lane = _lane((tm, HEAD_PAD))
        for h in range(HEADS):
            sl = slice(h * HEAD_PAD, (h + 1) * HEAD_PAD)
            qh, _ = _head_fwd(qr[:, sl], gqh_ref[...], cosv, ssv)
            q_ref[:, sl] = qh.astype(BF16)
            kvh = kv[:, sl]
            kh, _ = _head_fwd(jnp.where(lane < MLA_NOPE, kvh, kr), gkh_ref[...], cosv, ssv)
            k_ref[:, sl] = kh.astype(BF16)
            v_ref[:, sl] = jnp.where(lane >= MLA_NOPE, kvh, 0.0).astype(BF16)

    out = jax.ShapeDtypeStruct((s, width), BF16)
    return pl.pallas_call(
        body, name=name, grid=(s // tm,),
        in_specs=[_rows(tm, Q_LORA), _rows(tm, KV_LORA), _rows(tm, HEAD_PAD), _rows(tm, 1),
                  _whole((1, HEAD_PAD)), _whole((1, HEAD_PAD)), _whole((1, Q_LORA)), _whole((1, KV_LORA)),
                  _whole((1, HEAD_PAD)), _whole((1, HEAD_PAD)), _whole((Q_LORA, width)), _whole((KV_LORA, width))],
        out_specs=[_rows(tm, width)] * 3, out_shape=[out, out, out],
        compiler_params=_cp(("parallel",)))(cq, ckv, krope, pos, freq, sign, g_ql, g_kvl, g_qh, g_kh, wq, wkv)


def _attn_specs(s, nb):
    qspec = pl.BlockSpec((TQ, nb * HEAD_PAD), lambda g, i: (i, g))
    kspec = pl.BlockSpec((s, nb * HEAD_PAD), lambda g, i: (0, g))
    return qspec, kspec


def _lanes(b):
    return slice(b * HEAD_PAD, (b + 1) * HEAD_PAD)


def _tri(cmp):
    r = lax.broadcasted_iota(jnp.int32, (TQ, TQ), 0)
    c = lax.broadcasted_iota(jnp.int32, (TQ, TQ), 1)
    return cmp(r, c)


def _mla_fwd_call(q, k, v, name, rider=None):
    s, width = q.shape
    scale = 1.0 / math.sqrt(MLA_QK)

    nb = MLA_FWD_BLOCKS

    def body(q_ref, k_ref, v_ref, o_ref, lse_ref):
        qi = pl.program_id(1)
        qs = [q_ref[:, _lanes(b)] for b in range(nb)]
        causal = _tri(lambda r, c: c <= r)

        def step(kb, carry, diag):
            ks = pl.multiple_of(kb * TQ, TQ)
            heads = range(nb)
            scs = [_dot_nt(qs[b], k_ref[pl.ds(ks, TQ), _lanes(b)]) * scale for b in heads]
            if diag:
                scs = [jnp.where(causal, sc, -1e30) for sc in scs]
            mns = [jnp.maximum(carry[b][0], jnp.max(scs[b], axis=-1, keepdims=True)) for b in heads]
            als = [jnp.exp(carry[b][0] - mns[b]) for b in heads]
            ps = [jnp.exp(scs[b] - mns[b]) for b in heads]
            ls = [als[b] * carry[b][1] + jnp.sum(ps[b], axis=-1, keepdims=True) for b in heads]
            accs = [als[b] * carry[b][2] + _dot(ps[b].astype(BF16), v_ref[pl.ds(ks, TQ), _lanes(b)]) for b in heads]
            return tuple((mns[b], ls[b], accs[b]) for b in heads)

        init = tuple((jnp.full((TQ, 1), -1e30, F32), jnp.zeros((TQ, 1), F32), jnp.zeros((TQ, HEAD_PAD), F32))
                     for _ in range(nb))
        carry = step(qi, init, True)
        carry = lax.fori_loop(0, qi, lambda kb, c: step(kb, c, False), carry)
        for b in range(nb):
            m, l, acc = carry[b]
            o_ref[:, _lanes(b)] = (acc / l).astype(BF16)
            lse_ref[:, _lanes(b)] = jnp.broadcast_to(m + jnp.log(l), (TQ, HEAD_PAD))

    qspec, kspec = _attn_specs(s, nb)
    return _with_rider(
        body, rider, name=name, grid=(width // (nb * HEAD_PAD), s // TQ),
        in_specs=[qspec, kspec, kspec], out_specs=[qspec, qspec],
        out_shape=[jax.ShapeDtypeStruct((s, width), BF16), jax.ShapeDtypeStruct((s, width), F32)],
        args=(q, k, v), sem=("parallel", "arbitrary"))


def _mla_bwd_call(q, k, v, o, do, lse, name, rider=None):
    s, width = q.shape
    scale = 1.0 / math.sqrt(MLA_QK)
    nb = MLA_BWD_BLOCKS

    def body(q_ref, k_ref, v_ref, o_ref, do_ref, lse_ref, dq_ref, dk_ref, dv_ref):
        qi = pl.program_id(1)

        @pl.when(qi == 0)
        def _():
            dk_ref[...] = jnp.zeros_like(dk_ref)
            dv_ref[...] = jnp.zeros_like(dv_ref)

        qs = [q_ref[:, _lanes(b)] for b in range(nb)]
        dos = [do_ref[:, _lanes(b)] for b in range(nb)]
        lses = [lse_ref[:, b * HEAD_PAD:b * HEAD_PAD + 1] for b in range(nb)]
        dlts = [jnp.sum(dos[b].astype(F32) * o_ref[:, _lanes(b)].astype(F32), axis=-1, keepdims=True) for b in range(nb)]
        causal = _tri(lambda r, c: c <= r)

        def step(kb, dqs, diag):
            ks = pl.multiple_of(kb * TQ, TQ)
            heads = range(nb)
            kts = [k_ref[pl.ds(ks, TQ), _lanes(b)] for b in heads]
            scs = [_dot_nt(qs[b], kts[b]) for b in heads]
            dps = [_dot_nt(dos[b], v_ref[pl.ds(ks, TQ), _lanes(b)]) for b in heads]
            ps = [jnp.exp(scs[b] * scale - lses[b]) for b in heads]
            if diag:
                ps = [jnp.where(causal, p, 0.0) for p in ps]
            dss = [(ps[b] * (dps[b] - dlts[b]) * scale).astype(BF16) for b in heads]
            dvs = [_dot_tn(ps[b].astype(BF16), dos[b]) for b in heads]
            dks = [_dot_tn(dss[b], qs[b]) for b in heads]
            out = tuple(dqs[b] + _dot(dss[b], kts[b]) for b in heads)
            for b in heads:
                dv_ref[pl.ds(ks, TQ), _lanes(b)] += dvs[b]
                dk_ref[pl.ds(ks, TQ), _lanes(b)] += dks[b]
            return out

        dqs = step(qi, tuple(jnp.zeros((TQ, HEAD_PAD), F32) for _ in range(nb)), True)
        dqs = lax.fori_loop(0, qi, lambda kb, c: step(kb, c, False), dqs)
        for b in range(nb):
            dq_ref[:, _lanes(b)] = dqs[b]

    qspec, kspec = _attn_specs(s, nb)
    out = jax.ShapeDtypeStruct((s, width), F32)
    return _with_rider(
        body, rider, name=name, grid=(width // (nb * HEAD_PAD), s // TQ),
        in_specs=[qspec, kspec, kspec, qspec, qspec, qspec], out_specs=[qspec, kspec, kspec],
        out_shape=[out, out, out], args=(q, k, v, o, do, lse), sem=("parallel", "arbitrary"))


def _dot_hilo(x, u):
    hi = x.astype(BF16)
    lo = (x - hi.astype(F32)).astype(BF16)
    return _dot(hi, u) + _dot(lo, u)


def _sb_logs(z):
    ls = jnp.minimum(z, 0.0) - jnp.log(1.0 + jnp.exp(-jnp.abs(z)))
    return ls, ls - z


def _sb_head_q(qb, first, hh):
    keep = first if hh == 0 else jnp.logical_not(first)
    return jnp.where(keep, qb, jnp.zeros_like(qb)) * jnp.asarray(SB_SCALE, qb.dtype)


def _sb_fwd_call(q, k, v, name):
    s, width = q.shape
    nb = SB_FWD_BLOCKS
    chains = [(b, hh) for b in range(nb) for hh in range(HEAD_PAD // SB_HEAD)]

    def body(q_ref, k_ref, v_ref, o_ref, t_ref):
        qi = pl.program_id(1)
        strict = _tri(lambda r, c: c < r)
        after = _tri(lambda r, c: r > c).astype(BF16)
        first = _lane((1, HEAD_PAD)) < SB_HEAD
        qhs = [_sb_head_q(q_ref[:, _lanes(b)], first, hh) for b, hh in chains]

        def step(kb, carry, diag):
            ks = pl.multiple_of(kb * TQ, TQ)
            ids = range(len(chains))
            zs = [_dot_nt(qhs[ci], k_ref[pl.ds(ks, TQ), _lanes(chains[ci][0])]) for ci in ids]
            logs = [_sb_logs(z) for z in zs]
            lss = [lg[0] for lg in logs]
            l1ms = [jnp.where(strict, lg[1], 0.0) if diag else lg[1] for lg in logs]
            sufs = [_dot_hilo(l1m, after) for l1m in l1ms]
            as_ = [jnp.exp(lss[ci] + sufs[ci] + carry[ci][0]) for ci in ids]
            if diag:
                as_ = [jnp.where(strict, a, 0.0) for a in as_]
            accs = [carry[ci][1] + _dot(as_[ci].astype(BF16), v_ref[pl.ds(ks, TQ), _lanes(chains[ci][0])]) for ci in ids]
            return tuple((carry[ci][0] + jnp.sum(l1ms[ci], axis=-1, keepdims=True), accs[ci]) for ci in ids)

        init = tuple((jnp.zeros((TQ, 1), F32), jnp.zeros((TQ, HEAD_PAD), F32)) for _ in chains)
        carry = step(qi, init, True)
        carry = lax.fori_loop(0, qi, lambda j, c: step(qi - 1 - j, c, False), carry)
        for b in range(nb):
            (cs0, acc0), (cs1, acc1) = carry[2 * b], carry[2 * b + 1]
            o_ref[:, _lanes(b)] = jnp.where(first, acc0, acc1).astype(BF16)
            t_ref[:, _lanes(b)] = jnp.where(first, cs0, cs1)

    qspec, kspec = _attn_specs(s, nb)
    return pl.pallas_call(
        body, name=name, grid=(width // (nb * HEAD_PAD), s // TQ),
        in_specs=[qspec, kspec, kspec], out_specs=[qspec, qspec],
        out_shape=[jax.ShapeDtypeStruct((s, width), BF16), jax.ShapeDtypeStruct((s, width), F32)],
        compiler_params=_cp(("parallel", "arbitrary")))(q, k, v)


def _sb_bwd_call(q, k, v, do, tot, name):
    s, width = q.shape
    nb = SB_BWD_BLOCKS
    chains = [(b, hh) for b in range(nb) for hh in range(HEAD_PAD // SB_HEAD)]

    def body(q_ref, k_ref, v_ref, do_ref, t_ref, dq_ref, dk_ref, dv_ref):
        qi = pl.program_id(1)

        @pl.when(qi == 0)
        def _():
            dk_ref[...] = jnp.zeros_like(dk_ref)
            dv_ref[...] = jnp.zeros_like(dv_ref)

        strict = _tri(lambda r, c: c < r)
        upto = _tri(lambda r, c: r <= c).astype(BF16)
        before = _tri(lambda r, c: r < c).astype(BF16)
        first = _lane((1, HEAD_PAD)) < SB_HEAD
        qhs = [_sb_head_q(q_ref[:, _lanes(b)], first, hh) for b, hh in chains]
        dohs = []
        for b, hh in chains:
            dob = do_ref[:, _lanes(b)]
            dohs.append(jnp.where(first if hh == 0 else jnp.logical_not(first), dob, jnp.zeros_like(dob)))
        tts = [t_ref[:, b * HEAD_PAD + hh * SB_HEAD:b * HEAD_PAD + hh * SB_HEAD + 1] for b, hh in chains]

        def step(kb, carry, diag):
            ks = pl.multiple_of(kb * TQ, TQ)
            ids = range(len(chains))
            kts = [k_ref[pl.ds(ks, TQ), _lanes(b)] for b, _ in chains]
            zs = [_dot_nt(qhs[ci], kts[ci]) for ci in ids]
            das = [_dot_nt(dohs[ci], v_ref[pl.ds(ks, TQ), _lanes(chains[ci][0])]) for ci in ids]
            logs = [_sb_logs(z) for z in zs]
            lss = [lg[0] for lg in logs]
            l1ms = [jnp.where(strict, lg[1], 0.0) if diag else lg[1] for lg in logs]
            pins = [_dot_hilo(l1m, upto) for l1m in l1ms]
            as_ = [jnp.exp(lss[ci] + (tts[ci] - carry[ci][0] - pins[ci])) for ci in ids]
            if diag:
                as_ = [jnp.where(strict, a, 0.0) for a in as_]
            gs = [as_[ci] * das[ci] for ci in ids]
            cexs = [carry[ci][1] + _dot_hilo(gs[ci], before) for ci in ids]
            dzs = [gs[ci] - jnp.exp(lss[ci]) * (gs[ci] + cexs[ci]) for ci in ids]
            if diag:
                dzs = [jnp.where(strict, dz, 0.0) for dz in dzs]
            dzbs = [dz.astype(BF16) for dz in dzs]
            dvps = [_dot_tn(as_[ci].astype(BF16), dohs[ci]) for ci in ids]
            dkps = [_dot_tn(dzbs[ci], qhs[ci]) for ci in ids]
            out = tuple((carry[ci][0] + jnp.sum(l1ms[ci], axis=-1, keepdims=True),
                         carry[ci][1] + jnp.sum(gs[ci], axis=-1, keepdims=True),
                         carry[ci][2] + _dot(dzbs[ci], kts[ci])) for ci in ids)
            for b in range(nb):
                dk_ref[pl.ds(ks, TQ), _lanes(b)] += dkps[2 * b] + dkps[2 * b + 1]
                dv_ref[pl.ds(ks, TQ), _lanes(b)] += dvps[2 * b] + dvps[2 * b + 1]
            return out

        init = tuple((jnp.zeros((TQ, 1), F32), jnp.zeros((TQ, 1), F32), jnp.zeros((TQ, HEAD_PAD), F32)) for _ in chains)
        carry = lax.fori_loop(0, qi, lambda kb, c: step(kb, c, False), init)
        carry = step(qi, carry, True)
        for b in range(nb):
            dq_ref[:, _lanes(b)] = (jnp.where(first, carry[2 * b][2], carry[2 * b + 1][2]) * SB_SCALE).astype(BF16)

    qspec, kspec = _attn_specs(s, nb)
    return pl.pallas_call(
        body, name=name, grid=(width // (nb * HEAD_PAD), s // TQ),
        in_specs=[qspec, kspec, kspec, qspec, qspec], out_specs=[qspec, kspec, kspec],
        out_shape=[jax.ShapeDtypeStruct((s, width), BF16), jax.ShapeDtypeStruct((s, width), F32),
                   jax.ShapeDtypeStruct((s, width), F32)],
        compiler_params=_cp(("parallel", "arbitrary")))(q, k, v, do, tot)


def _merge_out_call(om, osb, gates, h, wbm, wbs, wo, name):
    s, d = h.shape
    tm = min(TM_SMALL, s)

    def body(om_ref, os_ref, g_ref, h_ref, wbm_ref, wbs_ref, wo_ref, h2_ref, bm_ref, bs_ref, mg_ref):
        bm = _dot(om_ref[...], wbm_ref[...])
        bs = _dot(os_ref[...], wbs_ref[...])
        mg = (_sigmoid(g_ref[:, :d]) * bm + _sigmoid(g_ref[:, d:]) * bs).astype(BF16)
        bm_ref[...] = bm
        bs_ref[...] = bs
        mg_ref[...] = mg
        h2_ref[...] = h_ref[...] + _dot(mg, wo_ref[...])

    return pl.pallas_call(
        body, name=name, grid=(s // tm,),
        in_specs=[_rows(tm, om.shape[1]), _rows(tm, SB_WIDTH), _rows(tm, 2 * d), _rows(tm, d),
                  _whole(wbm.shape), _whole(wbs.shape), _whole(wo.shape)],
        out_specs=[_rows(tm, d)] * 4,
        out_shape=[jax.ShapeDtypeStruct((s, d), F32), jax.ShapeDtypeStruct((s, d), F32),
                   jax.ShapeDtypeStruct((s, d), F32), jax.ShapeDtypeStruct((s, d), BF16)],
        compiler_params=_cp(("parallel",)))(om, osb, gates, h, wbm, wbs, wo)


def _ple_call(h, g, wg, p, wp, tgt, name):
    s, d = h.shape
    tm = min(TM_SMALL, s)

    def body(h_ref, g_ref, wg_ref, p_ref, wp_ref, t_ref, dh_ref, dhs_ref, un_ref, dgl_ref, dpp_ref, dg_ref, sq_ref):
        @pl.when(pl.program_id(0) == 0)
        def _():
            dg_ref[...] = jnp.zeros_like(dg_ref)
            sq_ref[...] = jnp.zeros_like(sq_ref)

        x = h_ref[...]
        gain = g_ref[...]
        r = _rstd(x, d)
        xh = x * r
        un = (xh * gain).astype(BF16)
        sg = _sigmoid(_dot(un, wg_ref[...]))
        pp = _dot(p_ref[...].astype(BF16), wp_ref[...])
        diff = (x + sg * pp) - t_ref[...]
        sq_ref[...] += jnp.sum(diff * diff, axis=0, keepdims=True)
        dy = diff * (1.0 / d)
        dgl = ((dy * pp) * (sg * (1.0 - sg))).astype(BF16)
        dun = _dot_nt(dgl, wg_ref[...])
        dg_ref[...] += jnp.sum(dun * xh, axis=0, keepdims=True)
        dh = dy + _rms_bwd(x, r, gain, dun, d)
        dh_ref[...] = dh
        dhs_ref[...] = (0.5 * dh).astype(BF16)
        un_ref[...] = un
        dgl_ref[...] = dgl
        dpp_ref[...] = (dy * sg).astype(BF16)

    bf = jax.ShapeDtypeStruct((s, d), BF16)
    vec = jax.ShapeDtypeStruct((1, d), F32)
    return pl.pallas_call(
        body, name=name, grid=(s // tm,),
        in_specs=[_rows(tm, d), _whole((1, d)), _whole(wg.shape), _rows(tm, PLE_DIM), _whole(wp.shape), _rows(tm, d)],
        out_specs=[_rows(tm, d)] * 5 + [_whole((1, d))] * 2,
        out_shape=[jax.ShapeDtypeStruct((s, d), F32), bf, bf, bf, bf, vec, vec],
        compiler_params=_cp(("arbitrary",)))(h, g, wg, p, wp, tgt)


def _ffn_bwd_a_call(dhs, a, b, wo, name, rider=None):
    s, n = a.shape
    d = dhs.shape[1]
    tn = n // 2
    tm = min(TM, s)

    def body(dh_ref, a_ref, b_ref, wo_ref, da_ref, db_ref):
        dh = dh_ref[...]
        for c0 in range(0, tn, COL_CHUNK):
            sl = slice(c0, min(c0 + COL_CHUNK, tn))
            dhm = _dot_nt(dh, wo_ref[sl, :])
            av = a_ref[:, sl]
            sa = _sigmoid(av)
            da_ref[:, sl] = (dhm * b_ref[:, sl] * (sa * (1.0 + av * (1.0 - sa)))).astype(BF16)
            db_ref[:, sl] = (dhm * (av * sa)).astype(BF16)

    blk = pl.BlockSpec((tm, tn), lambda j, i: (i, j))
    return _with_rider(
        body, rider, name=name, grid=(n // tn, s // tm),
        in_specs=[pl.BlockSpec((tm, d), lambda j, i: (i, 0)), blk, blk, pl.BlockSpec((tn, d), lambda j, i: (j, 0))],
        out_specs=[blk, blk],
        out_shape=[jax.ShapeDtypeStruct((s, n), BF16)] * 2, args=(dhs, a, b, wo), sem=("parallel", "parallel"))


def _norm_bwd_call(dy_list, w_list, h, g, dh_in, name, half_out, rider=None):
    s, d = h.shape
    tm = min(TM_SMALL, s)
    nk = len(dy_list)
    factor = 0.5 if half_out else 1.0

    def body(*refs):
        dy_refs = refs[:nk]
        w_refs = refs[nk:2 * nk]
        h_ref, g_ref, dhin_ref, dh_ref, dhb_ref, dg_ref = refs[2 * nk:]

        @pl.when(pl.program_id(0) == 0)
        def _():
            dg_ref[...] = jnp.zeros_like(dg_ref)

        du = _dot_nt(dy_refs[0][...], w_refs[0][...])
        for dy_ref, w_ref in zip(dy_refs[1:], w_refs[1:]):
            du = du + _dot_nt(dy_ref[...], w_ref[...])
        x = h_ref[...]
        r = _rstd(x, d)
        dg_ref[...] += jnp.sum(du * (x * r), axis=0, keepdims=True)
        dh = dhin_ref[...] + _rms_bwd(x, r, g_ref[...], du, d)
        dh_ref[...] = dh
        dhb_ref[...] = (factor * dh).astype(BF16)

    outs, got = _with_rider(
        body, rider, name=name, grid=(s // tm,),
        in_specs=[_rows(tm, dy.shape[1]) for dy in dy_list] + [_whole(w.shape) for w in w_list]
        + [_rows(tm, d), _whole((1, d)), _rows(tm, d)],
        out_specs=[_rows(tm, d), _rows(tm, d), _whole((1, d))],
        out_shape=[jax.ShapeDtypeStruct((s, d), F32), jax.ShapeDtypeStruct((s, d), BF16),
                   jax.ShapeDtypeStruct((1, d), F32)],
        args=(*dy_list, *w_list, h, g, dh_in), sem=("arbitrary",))
    return outs if rider is None else (outs, got)


def _merge_bwd_call(dhb, gates, bm, bs, wo, wbm, wbs, name):
    s, d = bm.shape
    tm = min(TM_SMALL, s)

    def body(dh_ref, g_ref, bm_ref, bs_ref, wo_ref, wbm_ref, wbs_ref, dg_ref, dbm_ref, dbs_ref, dom_ref, dos_ref):
        dmg = _dot_nt(dh_ref[...], wo_ref[...])
        s1 = _sigmoid(g_ref[:, :d])
        s2 = _sigmoid(g_ref[:, d:])
        dg_ref[:, :d] = (dmg * bm_ref[...] * (s1 * (1.0 - s1))).astype(BF16)
        dg_ref[:, d:] = (dmg * bs_ref[...] * (s2 * (1.0 - s2))).astype(BF16)
        dbm = (dmg * s1).astype(BF16)
        dbs = (dmg * s2).astype(BF16)
        dbm_ref[...] = dbm
        dbs_ref[...] = dbs
        dom_ref[...] = _dot_nt(dbm, wbm_ref[...]).astype(BF16)
        dos_ref[...] = _dot_nt(dbs, wbs_ref[...]).astype(BF16)

    wm = wbm.shape[0]
    return pl.pallas_call(
        body, name=name, grid=(s // tm,),
        in_specs=[_rows(tm, d), _rows(tm, 2 * d), _rows(tm, d), _rows(tm, d),
                  _whole(wo.shape), _whole(wbm.shape), _whole(wbs.shape)],
        out_specs=[_rows(tm, 2 * d), _rows(tm, d), _rows(tm, d), _rows(tm, wm), _rows(tm, SB_WIDTH)],
        out_shape=[jax.ShapeDtypeStruct((s, 2 * d), BF16), jax.ShapeDtypeStruct((s, d), BF16),
                   jax.ShapeDtypeStruct((s, d), BF16), jax.ShapeDtypeStruct((s, wm), BF16),
                   jax.ShapeDtypeStruct((s, SB_WIDTH), BF16)],
        compiler_params=_cp(("parallel",)))(dhb, gates, bm, bs, wo, wbm, wbs)


def _mla_prep_bwd_call(cq, ckv, krope, pos, freq, sign, g_ql, g_kvl, g_qh, g_kh, wq, wkv, dq, dk, dv, name):
    s = cq.shape[0]
    tm = min(TM_SMALL, s)
    width = HEADS * HEAD_PAD

    def body(cq_ref, ckv_ref, kr_ref, pos_ref, freq_ref, sign_ref, gql_ref, gkvl_ref, gqh_ref, gkh_ref,
             wq_ref, wkv_ref, dq_ref, dk_ref, dv_ref,
             dcq_ref, dckv_ref, dkr_ref, dwq_ref, dwkv_ref, dgql_ref, dgkvl_ref, dgqh_ref, dgkh_ref, dqr_ref, dkv_ref):
        @pl.when(pl.program_id(0) == 0)
        def _():
            for ref in (dwq_ref, dwkv_ref, dgql_ref, dgkvl_ref, dgqh_ref, dgkh_ref):
                ref[...] = jnp.zeros_like(ref)

        cosv, ssv = _rope_tables(pos_ref, freq_ref, sign_ref)
        xq = cq_ref[...]
        rq = _rstd(xq, Q_LORA)
        cqn = ((xq * rq) * gql_ref[...]).astype(BF16)
        qr = _dot(cqn, wq_ref[...])
        xk = ckv_ref[...]
        rk = _rstd(xk, KV_LORA)
        ckvn = ((xk * rk) * gkvl_ref[...]).astype(BF16)
        kv = _dot(ckvn, wkv_ref[...])
        kr = kr_ref[...]
        lane = _lane((tm, HEAD_PAD))
        dkr = jnp.zeros((tm, HEAD_PAD), F32)
        dgqh = jnp.zeros((1, HEAD_PAD), F32)
        dgkh = jnp.zeros((1, HEAD_PAD), F32)
        for h in range(HEADS):
            sl = slice(h * HEAD_PAD, (h + 1) * HEAD_PAD)
            x = qr[:, sl]
            dx, dgh = _head_bwd(x, _rstd(x, MLA_QK), gqh_ref[...], cosv, ssv, dq_ref[:, sl])
            dqr_ref[:, sl] = dx.astype(BF16)
            dgqh = dgqh + dgh
            x = jnp.where(lane < MLA_NOPE, kv[:, sl], kr)
            dx, dgh = _head_bwd(x, _rstd(x, MLA_QK), gkh_ref[...], cosv, ssv, dk_ref[:, sl])
            dgkh = dgkh + dgh
            dkr = dkr + jnp.where(lane >= MLA_NOPE, dx, 0.0)
            dkv_ref[:, sl] = jnp.where(lane < MLA_NOPE, dx, dv_ref[:, sl]).astype(BF16)
        dgqh_ref[...] += dgqh
        dgkh_ref[...] += dgkh
        dkr_ref[...] = dkr.astype(BF16)
        dqr = dqr_ref[...]
        dkvb = dkv_ref[...]
        dwq_ref[...] += _dot_tn(cqn, dqr)
        dwkv_ref[...] += _dot_tn(ckvn, dkvb)
        dcqn = _dot_nt(dqr, wq_ref[...])
        dgql_ref[...] += jnp.sum(dcqn * (xq * rq), axis=0, keepdims=True)
        dcq_ref[...] = _rms_bwd(xq, rq, gql_ref[...], dcqn, Q_LORA).astype(BF16)
        dckvn = _dot_nt(dkvb, wkv_ref[...])
        dgkvl_ref[...] += jnp.sum(dckvn * (xk * rk), axis=0, keepdims=True)
        dckv_ref[...] = _rms_bwd(xk, rk, gkvl_ref[...], dckvn, KV_LORA).astype(BF16)

    vec = lambda n: jax.ShapeDtypeStruct((1, n), F32)
    outs = pl.pallas_call(
        body, name=name, grid=(s // tm,),
        in_specs=[_rows(tm, Q_LORA), _rows(tm, KV_LORA), _rows(tm, HEAD_PAD), _rows(tm, 1),
                  _whole((1, HEAD_PAD)), _whole((1, HEAD_PAD)), _whole((1, Q_LORA)), _whole((1, KV_LORA)),
                  _whole((1, HEAD_PAD)), _whole((1, HEAD_PAD)), _whole((Q_LORA, width)), _whole((KV_LORA, width)),
                  _rows(tm, width), _rows(tm, width), _rows(tm, width)],
        out_specs=[_rows(tm, Q_LORA), _rows(tm, KV_LORA), _rows(tm, HEAD_PAD), _whole((Q_LORA, width)),
                   _whole((KV_LORA, width)), _whole((1, Q_LORA)), _whole((1, KV_LORA)), _whole((1, HEAD_PAD)),
                   _whole((1, HEAD_PAD)), _rows(tm, width), _rows(tm, width)],
        out_shape=[jax.ShapeDtypeStruct((s, Q_LORA), BF16), jax.ShapeDtypeStruct((s, KV_LORA), BF16),
                   jax.ShapeDtypeStruct((s, HEAD_PAD), BF16), jax.ShapeDtypeStruct((Q_LORA, width), F32),
                   jax.ShapeDtypeStruct((KV_LORA, width), F32), vec(Q_LORA), vec(KV_LORA), vec(HEAD_PAD), vec(HEAD_PAD),
                   jax.ShapeDtypeStruct((s, width), BF16), jax.ShapeDtypeStruct((s, width), BF16)],
        compiler_params=_cp(("arbitrary",)))(cq, ckv, krope, pos, freq, sign, g_ql, g_kvl, g_qh, g_kh, wq, wkv, dq, dk, dv)
    return outs[:9]


def _tn_call(a, b, name, shard_cols=None, rider=None):
    s, ka = a.shape
    nb = b.shape[1]
    ti = _pick(ka, (512, 256, 128))
    if shard_cols is not None:
        tj = shard_cols
    else:
        tj = nb if nb <= TN_MAX_COLS else _pick(nb, (2176, 1024, 512, 256, 128))
    ts = s if 2 * s * (ti + tj) * a.dtype.itemsize <= TN_OPERAND_BYTES else s // 2
    ns = s // ts

    def body(a_ref, b_ref, o_ref, acc_ref):
        part = _dot_tn(a_ref[...].astype(BF16), b_ref[...].astype(BF16))
        if ns == 1:
            o_ref[...] = part.astype(o_ref.dtype)
            return

        @pl.when(pl.program_id(2) == 0)
        def _():
            acc_ref[...] = part

        @pl.when(pl.program_id(2) != 0)
        def _():
            acc_ref[...] += part

        @pl.when(pl.program_id(2) == ns - 1)
        def _():
            o_ref[...] = acc_ref[...].astype(o_ref.dtype)

    if shard_cols is None:
        out_spec = pl.BlockSpec((ti, tj), lambda i, j, t: (i, j))
        out_shape = jax.ShapeDtypeStruct((ka, nb), BF16)
    else:
        out_spec = pl.BlockSpec((None, ti, tj), lambda i, j, t: (j, i, 0))
        out_shape = jax.ShapeDtypeStruct((nb // tj, ka, tj), BF16)
    (out,), got = _with_rider(
        body, rider, name=name, grid=(ka // ti, nb // tj, ns),
        in_specs=[pl.BlockSpec((ts, ti), lambda i, j, t: (t, i)), pl.BlockSpec((ts, tj), lambda i, j, t: (t, j))],
        out_specs=[out_spec], out_shape=[out_shape], scratch=[pltpu.VMEM((ti, tj), F32)], args=(a, b),
        sem=("parallel", "parallel", "arbitrary"))
    return out if rider is None else (out, got)


def _sum_call(parts, out_dtype, name):
    n, r, w = parts.shape
    tr = _row_tile(r)

    def body(p_ref, o_ref):
        acc = p_ref[0].astype(F32)
        for k in range(1, n):
            acc = acc + p_ref[k].astype(F32)
        o_ref[...] = acc.astype(out_dtype)

    return pl.pallas_call(
        body, name=name, grid=(r // tr,),
        in_specs=[pl.BlockSpec((n, tr, w), lambda i: (0, i, 0))], out_specs=_rows(tr, w),
        out_shape=jax.ShapeDtypeStruct((r, w), out_dtype), compiler_params=_cp(("parallel",)))(parts)


def _chip_sum_call(by_chip, core, name):
    n, r, w = by_chip.shape
    tr = _row_tile(r)
    nblk = r // tr

    def body(c_ref, p_ref, o_ref):
        acc = p_ref[0].astype(F32)
        for k in range(1, n):
            acc = acc + p_ref[k].astype(F32)
        o_ref[...] = acc

    return pl.pallas_call(
        body, name=name,
        grid_spec=pltpu.PrefetchScalarGridSpec(
            num_scalar_prefetch=1, grid=(nblk,),
            in_specs=[pl.BlockSpec((n, tr, w), lambda i, c_ref: (0, i, 0))],
            out_specs=pl.BlockSpec((tr, w), lambda i, c_ref: (c_ref[0] * nblk + i, 0))),
        out_shape=jax.ShapeDtypeStruct((2 * r, w), F32),
        compiler_params=_cp(("parallel",)))(core.reshape(1).astype(jnp.int32), by_chip)


def _pair_sum_call(full, other, core, out_dtype, name):
    n, r, w = other.shape
    tr = _row_tile(r)
    nblk = r // tr

    def body(c_ref, a_ref, b_ref, o_ref):
        o_ref[...] = (a_ref[...].astype(F32) + b_ref[...].astype(F32)).astype(out_dtype)

    spec = pl.BlockSpec((None, tr, w), lambda k, i, c_ref: (k, i, 0))
    return pl.pallas_call(
        body, name=name,
        grid_spec=pltpu.PrefetchScalarGridSpec(
            num_scalar_prefetch=1, grid=(n, nblk),
            in_specs=[pl.BlockSpec((None, tr, w), lambda k, i, c_ref: (k, c_ref[0] * nblk + i, 0)), spec],
            out_specs=spec),
        out_shape=jax.ShapeDtypeStruct((n, r, w), out_dtype),
        compiler_params=_cp(("parallel", "parallel")))(core.reshape(1).astype(jnp.int32), full, other)


def _adamw_call(w, g, row0, m, v, name):
    r, c = w.shape
    tr = _pick(math.gcd(r, row0) if row0 else r, (256, 128, 64, 32, 16, 8))
    off = row0 // tr

    def body(w_ref, g_ref, m_ref, v_ref, g_out_ref, d_ref, nm_ref, nv_ref):
        gg = g_ref[...]
        g_out_ref[...] = gg
        nm = ADAM_B1 * m_ref[...] + (1.0 - ADAM_B1) * gg
        nv = ADAM_B2 * v_ref[...] + (1.0 - ADAM_B2) * (gg * gg)
        m_hat = nm / (1.0 - ADAM_B1 ** ADAM_STEP)
        v_hat = nv / (1.0 - ADAM_B2 ** ADAM_STEP)
        d_ref[...] = -ADAM_LR * (m_hat / (jnp.sqrt(v_hat) + ADAM_EPS) + ADAM_WD * w_ref[...])
        nm_ref[...] = nm
        nv_ref[...] = nv

    out = jax.ShapeDtypeStruct((r, c), F32)
    g_spec = pl.BlockSpec((tr, c), lambda i: (off + i, 0))
    return pl.pallas_call(
        body, name=name, grid=(r // tr,), in_specs=[_rows(tr, c), g_spec, _rows(tr, c), _rows(tr, c)],
        out_specs=[_rows(tr, c)] * 4, out_shape=[out, out, out, out], compiler_params=_cp(("parallel",)))(w, g, m, v)


def _position():
    x, y, c = lax.axis_index("x"), lax.axis_index("y"), lax.axis_index("c")
    chips = [(1 - x, y), (x, 1 - y), (1 - x, 1 - y)]
    return x, y, c, chips


def _gather_rider(parts):
    n = len(parts)
    pairs = [(j, k) for j in range(3) for k in range(n)]

    def piece(out_refs, k, chip, core):
        half = parts[k].shape[0] // 2
        return out_refs[k].at[2 * chip[0] + chip[1], pl.ds(core * half, half), :]

    def over_ici(in_refs, out_refs, sems, j, k):
        x, y, c, chips = _position()
        half = parts[k].shape[0] // 2
        return pltpu.make_async_remote_copy(
            src_ref=in_refs[k].at[pl.ds(c * half, half), :], dst_ref=piece(out_refs, k, (x, y), c),
            send_sem=sems[0].at[n * j + k], recv_sem=sems[1].at[n * j + k], device_id=(*chips[j], c), device_id_type=MESH)

    def to_sibling(out_refs, sems, j, k):
        x, y, c, chips = _position()
        landed = piece(out_refs, k, chips[j], c)
        return pltpu.make_async_remote_copy(
            src_ref=landed, dst_ref=landed, send_sem=sems[2].at[n * j + k], recv_sem=sems[3].at[n * j + k],
            device_id=(x, y, 1 - c), device_id_type=MESH)

    def start(in_refs, out_refs, sems):
        for j, k in pairs:
            over_ici(in_refs, out_refs, sems, j, k).start()

    def finish(in_refs, out_refs, sems):
        for j, k in pairs:
            over_ici(in_refs, out_refs, sems, j, k).wait_recv()
            to_sibling(out_refs, sems, j, k).start()
        for j, k in pairs:
            to_sibling(out_refs, sems, j, k).wait_recv()
        for j, k in pairs:
            over_ici(in_refs, out_refs, sems, j, k).wait_send()
            to_sibling(out_refs, sems, j, k).wait_send()

    return _Rider(list(parts), [jax.ShapeDtypeStruct((N_CHIPS,) + p.shape, p.dtype) for p in parts], [3 * n] * 4,
                  start, finish)


def _scatter_rider(parts):
    n = len(parts)
    pairs = [(j, k) for j in range(3) for k in range(n)]

    def copy(in_refs, out_refs, sems, j, k):
        x, y, c, chips = _position()
        return pltpu.make_async_remote_copy(
            src_ref=in_refs[k].at[2 * chips[j][0] + chips[j][1]], dst_ref=out_refs[k].at[2 * x + y],
            send_sem=sems[0].at[n * j + k], recv_sem=sems[1].at[n * j + k], device_id=(*chips[j], c), device_id_type=MESH)

    def start(in_refs, out_refs, sems):
        for j, k in pairs:
            copy(in_refs, out_refs, sems, j, k).start()

    def finish(in_refs, out_refs, sems):
        for j, k in pairs:
            copy(in_refs, out_refs, sems, j, k).wait()

    return _Rider(list(parts), [jax.ShapeDtypeStruct(p.shape, p.dtype) for p in parts], [3 * n] * 2, start, finish)


def _exchange_call(rider, name):
    n, m = len(rider.ins), len(rider.out_shape)

    def body(*refs):
        rider.start(refs[:n], refs[n:n + m], refs[n + m:])
        rider.finish(refs[:n], refs[n:n + m], refs[n + m:])

    return pl.pallas_call(
        body, name=name, in_specs=[HBM] * n, out_specs=[HBM] * m, out_shape=rider.out_shape,
        scratch_shapes=[pltpu.SemaphoreType.DMA((k,)) for k in rider.sems])(*rider.ins)


def _pair_send_call(parts, name):
    n = len(parts)

    def body(*refs):
        in_refs, out_refs = refs[:n], refs[n:2 * n]
        send_sems, recv_sems = refs[2 * n:]
        x, y, c, _ = _position()
        copies = []
        for k in range(n):
            half = parts[k].shape[1] // 2
            cp = pltpu.make_async_remote_copy(
                src_ref=in_refs[k].at[:, pl.ds((1 - c) * half, half), :], dst_ref=out_refs[k],
                send_sem=send_sems.at[k], recv_sem=recv_sems.at[k], device_id=(x, y, 1 - c), device_id_type=MESH)
            cp.start()
            copies.append(cp)
        for cp in copies:
            cp.wait()

    sems = pltpu.SemaphoreType.DMA((n,))
    return pl.pallas_call(
        body, name=name, in_specs=[HBM] * n, out_specs=[HBM] * n,
        out_shape=[jax.ShapeDtypeStruct((p.shape[0], p.shape[1] // 2, p.shape[2]), p.dtype) for p in parts],
        scratch_shapes=[sems, sems])(*parts)


def _pair_swap_call(parts, name):
    n = len(parts)

    def body(*refs):
        out_refs = refs[n:2 * n]
        send_sems, recv_sems = refs[2 * n:]
        x, y, c, _ = _position()
        copies = []
        for k in range(n):
            half = parts[k].shape[0] // 2
            mine = out_refs[k].at[pl.ds(c * half, half), :]
            cp = pltpu.make_async_remote_copy(
                src_ref=mine, dst_ref=mine, send_sem=send_sems.at[k], recv_sem=recv_sems.at[k],
                device_id=(x, y, 1 - c), device_id_type=MESH)
            cp.start()
            copies.append(cp)
        for cp in copies:
            cp.wait()

    sems = pltpu.SemaphoreType.DMA((n,))
    return pl.pallas_call(
        body, name=name, in_specs=[HBM] * n, out_specs=[HBM] * n,
        out_shape=[jax.ShapeDtypeStruct(p.shape, p.dtype) for p in parts],
        input_output_aliases={k: k for k in range(n)},
        scratch_shapes=[sems, sems])(*parts)


def _all_gather_small_call(block, name):
    r, w = block.shape

    def body(in_ref, out_ref, send_sems, recv_sems, local_sem):
        x, y, c, _ = _position()
        me = 4 * x + 2 * y + c
        own = pltpu.make_async_copy(in_ref, out_ref.at[me], local_sem)
        own.start()
        copies = []
        for k in range(1, 8):
            peer = (x ^ (k >> 2), y ^ ((k >> 1) & 1), c ^ (k & 1))
            cp = pltpu.make_async_remote_copy(
                src_ref=in_ref, dst_ref=out_ref.at[me], send_sem=send_sems.at[k - 1], recv_sem=recv_sems.at[k - 1],
                device_id=peer, device_id_type=MESH)
            cp.start()
            copies.append(cp)
        for cp in copies:
            cp.wait()
        own.wait()

    return pl.pallas_call(
        body, name=name, in_specs=[HBM], out_specs=HBM,
        out_shape=jax.ShapeDtypeStruct((8, r, w), block.dtype),
        scratch_shapes=[pltpu.SemaphoreType.DMA((7,)), pltpu.SemaphoreType.DMA((7,)), pltpu.SemaphoreType.DMA])(block)


BIG = {
    "ffn1_w_in": ((D_MODEL, 2 * D_FF), 1), "ffn1_w_out": ((D_FF, D_MODEL), 0),
    "w_in": ((D_MODEL, 4256), 1), "w_q_up": ((Q_LORA, HEADS * MLA_QK), 1), "w_kv_up": ((KV_LORA, 1024), 1),
    "w_branch_mla": ((512, D_MODEL), 1), "w_branch_sb": ((SB_WIDTH, D_MODEL), 1), "w_out": ((D_MODEL, D_MODEL), 0),
    "ffn2_w_in": ((D_MODEL, 2 * D_FF), 1), "ffn2_w_out": ((D_FF, D_MODEL), 0),
    "w_ple_gate": ((D_MODEL, D_MODEL), 0), "w_ple_proj": ((PLE_DIM, D_MODEL), 1),
}
GAINS = {"ffn1_norm": 1024, "mix_norm": 1024, "q_latent_norm": 384, "kv_latent_norm": 256, "q_head_norm": 96,
         "k_head_norm": 96, "ffn2_norm": 1024, "ple_norm": 1024}
WEIGHT_ORDER = ["ffn1_norm", "ffn1_w_in", "ffn1_w_out", "mix_norm", "w_in", "q_latent_norm", "w_q_up",
                "kv_latent_norm", "w_kv_up", "q_head_norm", "k_head_norm", "w_branch_mla", "w_branch_sb", "w_out",
                "ffn2_norm", "ffn2_w_in", "ffn2_w_out", "ple_norm", "w_ple_gate", "w_ple_proj"]


def _shard_shape(name):
    (r, c), axis = BIG[name]
    return (r // N_CHIPS, c) if axis == 0 else (r, c // N_CHIPS)


GATHER_GROUPS = [
    [("ffn1_w_in",)],
    [("ffn1_w_out",), ("w_in",)],
    [("w_out",), ("w_kv_up", "w_branch_mla", "w_branch_sb"), ("w_q_up",)],
    [("ffn2_w_in",), ("ffn2_w_out", "w_ple_gate"), ("w_ple_proj",)],
]
REDUCE_GROUPS = [
    [("ffn2_w_in",), ("ffn2_w_out", "w_out", "w_ple_gate"), ("w_branch_mla", "w_branch_sb", "w_ple_proj")],
    [("w_in",), ("w_kv_up",), ("w_q_up",)],
    [("ffn1_w_out",)],
    [("ffn1_w_in",)],
]


def _join_parts(shards, group):
    return [shards[part[0]] if len(part) == 1 else jnp.concatenate([shards[n] for n in part], axis=-2) for part in group]


def _part_rows(group):
    where = {}
    for k, part in enumerate(group):
        at = 0
        for n in part:
            where[n] = (k, at)
            at += _shard_shape(n)[0]
    return where


def _split_parts(parts, group):
    return {n: parts[k][..., at:at + _shard_shape(n)[0], :] for n, (k, at) in _part_rows(group).items()}


def _to_shards(name, full):
    (r, c), axis = BIG[name]
    if axis == 0:
        return full.reshape(N_CHIPS, r // N_CHIPS, c)
    return full.reshape(r, N_CHIPS, c // N_CHIPS).transpose(1, 0, 2)


def _from_shards(name, shards):
    (r, c), axis = BIG[name]
    if axis == 0:
        return shards.reshape(r, c)
    return shards.transpose(1, 0, 2).reshape(r, c)


def _relayout_w_in(w):
    d = w.shape[0]
    z = lambda n: jnp.zeros((d, n), w.dtype)
    return jnp.concatenate([w[:, :640], z(MLA_NOPE), w[:, 640:672], z(HEAD_PAD - MLA_QK), w[:, 672:]], axis=1)


def _unlayout_w_in(g):
    return jnp.concatenate([g[:, :640], g[:, 640 + MLA_NOPE:640 + MLA_QK], g[:, 768:]], axis=1)


def _pad_heads(v):
    lead = v.shape[:-1]
    return jnp.pad(v.reshape(lead + (HEADS, MLA_QK)), [(0, 0)] * len(lead) + [(0, 0), (0, HEAD_PAD - MLA_QK)]).reshape(
        lead + (HEADS * HEAD_PAD,))


def _halves(w):
    n = w.shape[1] // 2
    return [w[:, :n], w[:, n:]]


def _step(x, p, pos, tgt, gains, weights, dist):
    d = D_MODEL
    full = dict(weights) if dist is None else {}
    reduced = {}

    def gather_rider(g):
        if dist is None:
            return None, None
        mine = _join_parts(weights, GATHER_GROUPS[g])
        return mine, _gather_rider(mine)

    def gathered(g, mine, others):
        if dist is not None:
            parts = [lax.dynamic_update_slice_in_dim(o, m[None], dist[0], axis=0) for o, m in zip(others, mine)]
            for n, shards in _split_parts(parts, GATHER_GROUPS[g]).items():
                full[n] = _from_shards(n, shards)

    def reduce_before(g):
        if dist is None:
            return None, None
        group = REDUCE_GROUPS[g]
        shards = {n: grads[n] if grads[n].ndim == 3 else _to_shards(n, grads[n].astype(BF16)) for part in group for n in part}
        partial = _join_parts(shards, group)
        from_sibling = _pair_send_call(partial, "grads%d_pair_send" % g)
        pair_sum = [_pair_sum_call(a, b, dist[1], BF16, "grads%d_pair_sum_%d" % (g, k))
                    for k, (a, b) in enumerate(zip(partial, from_sibling))]
        return pair_sum, _scatter_rider(pair_sum)

    def reduce_after(g, pair_sum, by_chip):
        if dist is not None:
            chip, core = dist
            by_chip = [lax.dynamic_update_slice_in_dim(t, lax.dynamic_slice_in_dim(o, chip, 1, axis=0), chip, axis=0)
                       for t, o in zip(by_chip, pair_sum)]
            bufs = _pair_swap_call([_chip_sum_call(t, core, "grads%d_chip_sum_%d" % (g, k)) for k, t in enumerate(by_chip)],
                                   "grads%d_pair_swap" % g)
            for n, (k, row0) in _part_rows(REDUCE_GROUPS[g]).items():
                reduced[n] = (bufs[k], row0)

    mine, rider = gather_rider(0)
    if dist is not None:
        gathered(0, mine, _exchange_call(rider, "gather0"))
    wts = full
    inv_freq = ROPE_BASE ** (-jnp.arange(0, MLA_ROPE, 2, dtype=F32) / MLA_ROPE)
    zeros = lambda n: jnp.zeros((n,), F32)
    freq = jnp.concatenate([zeros(MLA_NOPE), inv_freq, inv_freq, zeros(HEAD_PAD - MLA_QK)])[None]
    sign = jnp.concatenate([zeros(MLA_NOPE), -jnp.ones((16,), F32), jnp.ones((16,), F32), zeros(HEAD_PAD - MLA_QK)])[None]
    pad_gain = lambda g: jnp.pad(g, ((0, 0), (0, HEAD_PAD - MLA_QK)))
    g_qh, g_kh = pad_gain(gains["q_head_norm"]), pad_gain(gains["k_head_norm"])

    u1 = _norm_call(x, gains["ffn1_norm"], "norm_ffn1")
    mine, rider = gather_rider(1)
    (a1, b1, hm1), got = _ffn_in_call(u1, wts["ffn1_w_in"], "ffn1_in", rider)
    gathered(1, mine, got)
    mine, rider = gather_rider(2)
    h1, got = _ffn_out_call(hm1, wts["ffn1_w_out"], x, "ffn1_out", rider)
    gathered(2, mine, got)
    w_in = _relayout_w_in(wts["w_in"])
    wq = _pad_heads(wts["w_q_up"])
    wkv = wts["w_kv_up"]
    wbm = jnp.pad(wts["w_branch_mla"].reshape(HEADS, 64, d), ((0, 0), (64, 0), (0, 0))).reshape(HEADS * HEAD_PAD, d)
    wbs, wo = wts["w_branch_sb"], wts["w_out"]
    um = _norm_call(h1, gains["mix_norm"], "norm_mix")
    cq, ckv, krope, sbq, sbk, sbv, gates = _mix_in_call(um, w_in, "mix_in")
    prep_args = (cq, ckv, krope, pos, freq, sign, gains["q_latent_norm"], gains["kv_latent_norm"], g_qh, g_kh, wq, wkv)
    q, k, v = _mla_prep_call(*prep_args, "mla_prep")
    mine, rider = gather_rider(3)
    (om, lse), got = _mla_fwd_call(q, k, v, "mla_fwd", rider)
    gathered(3, mine, got)
    osb, tot = _sb_fwd_call(sbq, sbk, sbv, "sb_fwd")
    h2, bm, bs, mg = _merge_out_call(om, osb, gates, h1, wbm, wbs, wo, "merge_out")
    u2 = _norm_call(h2, gains["ffn2_norm"], "norm_ffn2")
    (a2, b2, hm2), _ = _ffn_in_call(u2, wts["ffn2_w_in"], "ffn2_in")
    h3, _ = _ffn_out_call(hm2, wts["ffn2_w_out"], h2, "ffn2_out")

    grads, gg = {}, {}
    dh3, dh3s, un, dgl, dpp, gg["ple_norm"], sq = _ple_call(
        h3, gains["ple_norm"], wts["w_ple_gate"], p, wts["w_ple_proj"], tgt, "ple")
    grads["w_ple_gate"] = _tn_call(un, dgl, "dw_ple_gate")
    grads["w_ple_proj"] = _tn_call(p, dpp, "dw_ple_proj")

    (da2, db2), _ = _ffn_bwd_a_call(dh3s, a2, b2, wts["ffn2_w_out"], "ffn2_bwd_act")
    grads["ffn2_w_out"] = _tn_call(hm2, dh3s, "dw_ffn2_out")
    grads["ffn2_w_in"] = jnp.concatenate([_tn_call(u2, da2, "dw_ffn2_in_a", shard_cols=D_FF // 2),
                                          _tn_call(u2, db2, "dw_ffn2_in_b", shard_cols=D_FF // 2)], axis=0)
    dh2, dh2b, gg["ffn2_norm"] = _norm_bwd_call([da2, db2], _halves(wts["ffn2_w_in"]), h2, gains["ffn2_norm"], dh3,
                                                "ffn2_bwd_norm", half_out=False)

    dgates, dbm, dbs, dom, dos = _merge_bwd_call(dh2b, gates, bm, bs, wo, wbm, wbs, "merge_bwd")
    grads["w_out"] = _tn_call(mg, dh2b, "dw_out")
    grads["w_branch_mla"] = _tn_call(om, dbm, "dw_branch_mla").reshape(HEADS, HEAD_PAD, d)[:, 64:, :].reshape(512, d)
    grads["w_branch_sb"] = _tn_call(osb, dbs, "dw_branch_sb")
    pair_sum, rider = reduce_before(0)
    (dq, dk, dv), got = _mla_bwd_call(q, k, v, om, dom, lse, "mla_bwd", rider)
    reduce_after(0, pair_sum, got)
    dsq, dsk, dsv = _sb_bwd_call(sbq, sbk, sbv, dos, tot, "sb_bwd")
    (dcq, dckv, dkr, dwq, grads["w_kv_up"], gg["q_latent_norm"], gg["kv_latent_norm"], dgqh, dgkh) = \
        _mla_prep_bwd_call(*prep_args, dq, dk, dv, "mla_prep_bwd")
    grads["w_q_up"] = dwq.reshape(Q_LORA, HEADS, HEAD_PAD)[:, :, :MLA_QK].reshape(Q_LORA, HEADS * MLA_QK)
    gg["q_head_norm"], gg["k_head_norm"] = dgqh[:, :MLA_QK], dgkh[:, :MLA_QK]
    dproj = jnp.concatenate([dcq, dckv, dkr, dsq, dsk.astype(BF16), dsv.astype(BF16), dgates], axis=1)
    grads["w_in"] = _unlayout_w_in(_tn_call(um, dproj, "dw_in"))
    dh1, dh1s, gg["mix_norm"] = _norm_bwd_call([dproj], [w_in], h1, gains["mix_norm"], dh2, "mix_bwd_norm", half_out=True)

    pair_sum, rider = reduce_before(1)
    (da1, db1), got = _ffn_bwd_a_call(dh1s, a1, b1, wts["ffn1_w_out"], "ffn1_bwd_act", rider)
    reduce_after(1, pair_sum, got)
    grads["ffn1_w_out"] = _tn_call(hm1, dh1s, "dw_ffn1_out")
    pair_sum, rider = reduce_before(2)
    res = _tn_call(u1, da1, "dw_ffn1_in_a", shard_cols=D_FF // 2, rider=rider)
    dwa, got = (res, None) if rider is None else res
    reduce_after(2, pair_sum, got)
    grads["ffn1_w_in"] = jnp.concatenate([dwa, _tn_call(u1, db1, "dw_ffn1_in_b", shard_cols=D_FF // 2)], axis=0)
    pair_sum, rider = reduce_before(3)
    res = _norm_bwd_call([da1, db1], _halves(wts["ffn1_w_in"]), x, gains["ffn1_norm"], dh1, "ffn1_bwd_norm",
                         half_out=False, rider=rider)
    (dx, _, gg["ffn1_norm"]), got = (res, None) if rider is None else res
    reduce_after(3, pair_sum, got)
    return sq, dx, gg, (grads if dist is None else reduced)


def kernel(x, p, positions, ffn1_norm, ffn1_w_in, ffn1_w_out, mix_norm, w_in, q_latent_norm, w_q_up, kv_latent_norm, w_kv_up, q_head_norm, k_head_norm, w_branch_mla, w_branch_sb, w_out, ffn2_norm, ffn2_w_in, ffn2_w_out, ple_norm, w_ple_gate, w_ple_proj, loss_target, m_ffn1_norm, m_ffn1_w_in, m_ffn1_w_out, m_mix_norm, m_w_in, m_q_latent_norm, m_w_q_up, m_kv_latent_norm, m_w_kv_up, m_q_head_norm, m_k_head_norm, m_w_branch_mla, m_w_branch_sb, m_w_out, m_ffn2_norm, m_ffn2_w_in, m_ffn2_w_out, m_ple_norm, m_w_ple_gate, m_w_ple_proj, v_ffn1_norm, v_ffn1_w_in, v_ffn1_w_out, v_mix_norm, v_w_in, v_q_latent_norm, v_w_q_up, v_kv_latent_norm, v_w_kv_up, v_q_head_norm, v_k_head_norm, v_w_branch_mla, v_w_branch_sb, v_w_out, v_ffn2_norm, v_ffn2_w_in, v_ffn2_w_out, v_ple_norm, v_w_ple_gate, v_w_ple_proj):
    given = dict(locals())
    w_shard = {n: given[n][0] for n in WEIGHT_ORDER}
    m_shard = {n: given["m_" + n][0] for n in WEIGHT_ORDER}
    v_shard = {n: given["v_" + n][0] for n in WEIGHT_ORDER}
    gains = {n: w_shard[n][None] for n in GAINS}

    chip = 2 * lax.axis_index("x") + lax.axis_index("y")
    sq, dx, gain_grads, reduced = _step(x[0], p[0, 0], positions.reshape(-1, 1), loss_target[0], gains,
                                        {n: w_shard[n].astype(BF16) for n in BIG}, (chip, lax.axis_index("c")))

    rows = [jnp.pad(gain_grads[n], ((0, 0), (0, D_MODEL - GAINS[n]))) for n in GAINS] + [sq]
    gain_block = jnp.concatenate(rows + [jnp.zeros((16 - len(rows), D_MODEL), F32)], axis=0)
    gain_sum = _sum_call(_all_gather_small_call(gain_block, "gains_all_gather"), F32, "gains_sum")
    loss = 0.5 * jnp.sum(gain_sum[len(GAINS)]) / D_MODEL

    outs = {"grad": {}, "delta": {}, "new_m": {}, "new_v": {}}
    gain_pack = lambda t: jnp.concatenate([jnp.pad(t[n][None], ((0, 0), (0, D_MODEL - GAINS[n]))) for n in GAINS], axis=0)
    packed = _adamw_call(gain_pack(w_shard), gain_sum, 0, gain_pack(m_shard), gain_pack(v_shard), "adamw_gains")
    for i, n in enumerate(GAINS):
        for kind, t in zip(("grad", "delta", "new_m", "new_v"), packed):
            outs[kind][n] = t[i, :GAINS[n]][None]
    for n in BIG:
        buf, row0 = reduced[n]
        for kind, t in zip(("grad", "delta", "new_m", "new_v"),
                           _adamw_call(w_shard[n], buf, row0, m_shard[n], v_shard[n], "adamw_" + n)):
            outs[kind][n] = t[None]

    return (loss, dx[None], *[outs["grad"][n] for n in WEIGHT_ORDER], *[outs["delta"][n] for n in WEIGHT_ORDER],
            *[outs["new_m"][n] for n in WEIGHT_ORDER], *[outs["new_v"][n] for n in WEIGHT_ORDER])
```

```python
import collections
import functools
import math

import jax
import jax.numpy as jnp
from jax import lax
from jax.experimental import pallas as pl
from jax.experimental.pallas import tpu as pltpu

F32 = jnp.float32
BF16 = jnp.bfloat16
MESH = pl.DeviceIdType.MESH

D_MODEL = 1024
D_FF = 2816
PLE_DIM = 256
NORM_EPS = 1e-6
HEADS = 8
MLA_NOPE = 64
MLA_ROPE = 32
MLA_QK = 96
Q_LORA = 384
KV_LORA = 256
SB_WIDTH = 512
ROPE_BASE = 10000.0
HEAD_PAD = 128
N_CHIPS = 4

ADAM_LR = 0.001
ADAM_B1 = 0.9
ADAM_B2 = 0.999
ADAM_EPS = 1e-08
ADAM_WD = 0.01
ADAM_STEP = 10

SEG_CQ = (0, 384)
SEG_CKV = (384, 256)
SEG_KROPE = (640, 128)
SEG_SBQ = (768, 512)
SEG_SBK = (1280, 512)
SEG_SBV = (1792, 512)
SEG_GATES = (2304, 2048)
IN_COLS_PAD = 4352

TM = 512
TM_SMALL = 256
TQ = 256
MLA_FWD_BLOCKS = 4
MLA_BWD_BLOCKS = 4
SB_FWD_BLOCKS = 4
SB_BWD_BLOCKS = 2
SB_HEAD = 64
SB_SCALE = 0.125
SB_DEAD = -104.0
COL_CHUNK = 256
TN_MAX_COLS = 2816
TN_OPERAND_BYTES = 34 * 1024 * 1024
MAX_ROW_TILE = 512
VMEM_LIMIT = 56 * 1024 * 1024

NT = (((1,), (1,)), ((), ()))
TN = (((0,), (0,)), ((), ()))


def _cp(sem):
    return pltpu.CompilerParams(dimension_semantics=sem, vmem_limit_bytes=VMEM_LIMIT)


def _rows(tm, w):
    return pl.BlockSpec((tm, w), lambda i: (i, 0))


def _whole(shape):
    return pl.BlockSpec(shape, lambda i: (0,) * len(shape))


def _dot(a, b):
    return jnp.dot(a, b, preferred_element_type=F32)


def _dot_nt(a, b):
    return lax.dot_general(a, b, NT, preferred_element_type=F32)


def _dot_tn(a, b):
    return lax.dot_general(a, b, TN, preferred_element_type=F32)


def _rstd(x, n):
    return lax.rsqrt(jnp.sum(x * x, axis=-1, keepdims=True) / n + NORM_EPS)


def _rms_bwd(x, r, g, dy, n):
    gy = dy * g
    return r * gy - x * ((r * r * r) * (jnp.sum(x * gy, axis=-1, keepdims=True) / n))


def _sigmoid(x):
    return jax.nn.sigmoid(x)


def _pick(n, cands):
    for c in cands:
        if n % c == 0:
            return c
    return n


def _row_tile(r):
    for t in range(min(r, MAX_ROW_TILE) // 16 * 16, 15, -16):
        if r % t == 0:
            return t
    return r


HBM = pl.BlockSpec(memory_space=pl.ANY)

_Rider = collections.namedtuple("_Rider", "ins out_shape sems start finish")


def _with_rider(body, rider, *, name, grid, in_specs, out_specs, out_shape, args, sem, scratch=()):
    if rider is None:
        return pl.pallas_call(body, name=name, grid=grid, in_specs=in_specs, out_specs=out_specs, out_shape=out_shape,
                              scratch_shapes=list(scratch), compiler_params=_cp(sem))(*args), None
    ni, no, nri, nro = len(in_specs), len(out_specs), len(rider.ins), len(rider.out_shape)

    def riding(*refs):
        ins, r_ins = refs[:ni], refs[ni:ni + nri]
        outs, r_outs = refs[ni + nri:ni + nri + no], refs[ni + nri + no:ni + nri + no + nro]
        scr = refs[ni + nri + no + nro:ni + nri + no + nro + len(scratch)]
        sems = refs[ni + nri + no + nro + len(scratch):]
        ids = [pl.program_id(a) for a in range(len(grid))]
        first = functools.reduce(jnp.logical_and, [i == 0 for i in ids])
        last = functools.reduce(jnp.logical_and, [i == g - 1 for i, g in zip(ids, grid)])

        @pl.when(first)
        def _():
            rider.start(r_ins, r_outs, sems)

        body(*ins, *outs, *scr)

        @pl.when(last)
        def _():
            rider.finish(r_ins, r_outs, sems)

    res = pl.pallas_call(
        riding, name=name, grid=grid, in_specs=list(in_specs) + [HBM] * nri, out_specs=list(out_specs) + [HBM] * nro,
        out_shape=list(out_shape) + list(rider.out_shape),
        scratch_shapes=list(scratch) + [pltpu.SemaphoreType.DMA((k,)) for k in rider.sems],
        compiler_params=_cp(("arbitrary",) * len(grid)))(*args, *rider.ins)
    return res[:no], res[no:]


def _norm_call(h, g, name):
    s, d = h.shape
    tm = min(TM, s)

    def body(h_ref, g_ref, u_ref):
        x = h_ref[...]
        u_ref[...] = ((x * _rstd(x, d)) * g_ref[...]).astype(BF16)

    return pl.pallas_call(
        body, name=name, grid=(s // tm,),
        in_specs=[_rows(tm, d), _whole((1, d))], out_specs=_rows(tm, d),
        out_shape=jax.ShapeDtypeStruct((s, d), BF16), compiler_params=_cp(("parallel",)))(h, g)


def _ffn_in_call(u, w, name, rider=None):
    s, d = u.shape
    n = w.shape[1] // 2
    tn = n // 2
    tm = min(TM, s)
    nj = n // tn

    def body(u_ref, wa_ref, wb_ref, a_ref, b_ref, hm_ref):
        uu = u_ref[...]
        a = _dot(uu, wa_ref[...])
        b = _dot(uu, wb_ref[...])
        a_ref[...] = a
        b_ref[...] = b
        hm_ref[...] = ((a * _sigmoid(a)) * b).astype(BF16)

    blk = pl.BlockSpec((tm, tn), lambda j, i: (i, j))
    return _with_rider(
        body, rider, name=name, grid=(nj, s // tm),
        in_specs=[pl.BlockSpec((tm, d), lambda j, i: (i, 0)),
                  pl.BlockSpec((d, tn), lambda j, i: (0, j)),
                  pl.BlockSpec((d, tn), lambda j, i: (0, j + nj))],
        out_specs=[blk, blk, blk],
        out_shape=[jax.ShapeDtypeStruct((s, n), F32), jax.ShapeDtypeStruct((s, n), F32),
                   jax.ShapeDtypeStruct((s, n), BF16)],
        args=(u, w, w), sem=("parallel", "parallel"))


def _ffn_out_call(hm, w, h, name, rider=None):
    s, n = hm.shape
    d = w.shape[1]
    tm = min(TM, s)

    def body(hm_ref, w_ref, h_ref, o_ref):
        o_ref[...] = h_ref[...] + 0.5 * _dot(hm_ref[...], w_ref[...])

    (out,), got = _with_rider(
        body, rider, name=name, grid=(s // tm,),
        in_specs=[_rows(tm, n), _whole((n, d)), _rows(tm, d)], out_specs=[_rows(tm, d)],
        out_shape=[jax.ShapeDtypeStruct((s, d), F32)], args=(hm, w, h), sem=("parallel",))
    return out, got


def _mix_in_call(u, w, name):
    s, d = u.shape
    tm = min(TM_SMALL, s)
    segs = [(SEG_CQ, F32), (SEG_CKV, F32), (SEG_KROPE, F32), (SEG_SBQ, BF16), (SEG_SBK, BF16),
            (SEG_SBV, BF16), (SEG_GATES, F32)]

    def body(u_ref, w_ref, *outs):
        uu = u_ref[...]
        for ((off, width), _), o_ref in zip(segs, outs):
            o_ref[...] = _dot(uu, w_ref[:, off:off + width]).astype(o_ref.dtype)

    return pl.pallas_call(
        body, name=name, grid=(s // tm,),
        in_specs=[_rows(tm, d), _whole((d, IN_COLS_PAD))],
        out_specs=[_rows(tm, width) for (_, width), _ in segs],
        out_shape=[jax.ShapeDtypeStruct((s, width), dt) for (_, width), dt in segs],
        compiler_params=_cp(("parallel",)))(u, w)


def _lane(shape):
    return lax.broadcasted_iota(jnp.int32, shape, len(shape) - 1)


def _rot_half(y):
    lane = _lane(y.shape)
    swapped = jnp.where(lane < MLA_NOPE + MLA_ROPE // 2, pltpu.roll(y, HEAD_PAD - 16, 1), pltpu.roll(y, 16, 1))
    return jnp.where((lane >= MLA_NOPE) & (lane < MLA_QK), swapped, 0.0)


def _rope_tables(pos_ref, freq_ref, sign_ref):
    ang = pos_ref[...].astype(F32) * freq_ref[...]
    return jnp.cos(ang), jnp.sin(ang) * sign_ref[...]


def _head_fwd(x, g, cosv, ssv):
    r = _rstd(x, MLA_QK)
    y = (x * r) * g
    return y * cosv + _rot_half(y) * ssv, r


def _head_bwd(x, r, g, cosv, ssv, dout):
    dy = dout * cosv + _rot_half(dout * ssv)
    return _rms_bwd(x, r, g, dy, MLA_QK), jnp.sum(dy * (x * r), axis=0, keepdims=True)


def _mla_prep_call(cq, ckv, krope, pos, freq, sign, g_ql, g_kvl, g_qh, g_kh, wq, wkv, name):
    s = cq.shape[0]
    tm = min(TM_SMALL, s)
    width = HEADS * HEAD_PAD

    def body(cq_ref, ckv_ref, kr_ref, pos_ref, freq_ref, sign_ref, gql_ref, gkvl_ref, gqh_ref, gkh_ref,
             wq_ref, wkv_ref, q_ref, k_ref, v_ref):
        cosv, ssv = _rope_tables(pos_ref, freq_ref, sign_ref)
        x = cq_ref[...]
        qr = _dot(((x * _rstd(x, Q_LORA)) * gql_ref[...]).astype(BF16), wq_ref[...])
        x = ckv_ref[...]
        kv = _dot(((x * _rstd(x, KV_LORA)) * gkvl_ref[...]).astype(BF16), wkv_ref[...])
        kr = kr_ref[...]
        lane = _lane((tm, HEAD_PAD))
        for h in range(HEADS):
            sl = slice(h * HEAD_PAD, (h + 1) * HEAD_PAD)
            qh, _ = _head_fwd(qr[:, sl], gqh_ref[...], cosv, ssv)
            q_ref[:, sl] = qh.astype(BF16)
            kvh = kv[:, sl]
            kh, _ = _head_fwd(jnp.where(lane < MLA_NOPE, kvh, kr), gkh_ref[...], cosv, ssv)
            k_ref[:, sl] = kh.astype(BF16)
            v_ref[:, sl] = jnp.where(lane >= MLA_NOPE, kvh, 0.0).astype(BF16)

    out = jax.ShapeDtypeStruct((s, width), BF16)
    return pl.pallas_call(
        body, name=name, grid=(s // tm,),
        in_specs=[_rows(tm, Q_LORA), _rows(tm, KV_LORA), _rows(tm, HEAD_PAD), _rows(tm, 1),
                  _whole((1, HEAD_PAD)), _whole((1, HEAD_PAD)), _whole((1, Q_LORA)), _whole((1, KV_LORA)),
                  _whole((1, HEAD_PAD)), _whole((1, HEAD_PAD)), _whole((Q_LORA, width)), _whole((KV_LORA, width))],
        out_specs=[_rows(tm, width)] * 3, out_shape=[out, out, out],
        compiler_params=_cp(("parallel",)))(cq, ckv, krope, pos, freq, sign, g_ql, g_kvl, g_qh, g_kh, wq, wkv)


def _attn_specs(s, nb):
    qspec = pl.BlockSpec((TQ, nb * HEAD_PAD), lambda g, i: (i, g))
    kspec = pl.BlockSpec((s, nb * HEAD_PAD), lambda g, i: (0, g))
    return qspec, kspec


def _lanes(b):
    return slice(b * HEAD_PAD, (b + 1) * HEAD_PAD)


def _tri(cmp):
    r = lax.broadcasted_iota(jnp.int32, (TQ, TQ), 0)
    c = lax.broadcasted_iota(jnp.int32, (TQ, TQ), 1)
    return cmp(r, c)


def _mla_fwd_call(q, k, v, name, rider=None):
    s, width = q.shape
    scale = 1.0 / math.sqrt(MLA_QK)

    nb = MLA_FWD_BLOCKS

    def body(q_ref, k_ref, v_ref, o_ref, lse_ref):
        qi = pl.program_id(1)
        qs = [q_ref[:, _lanes(b)] for b in range(nb)]
        causal = _tri(lambda r, c: c <= r)

        def step(kb, carry, diag):
            ks = pl.multiple_of(kb * TQ, TQ)
            heads = range(nb)
            scs = [_dot_nt(qs[b], k_ref[pl.ds(ks, TQ), _lanes(b)]) * scale for b in heads]
            if diag:
                scs = [jnp.where(causal, sc, -1e30) for sc in scs]
            mns = [jnp.maximum(carry[b][0], jnp.max(scs[b], axis=-1, keepdims=True)) for b in heads]
            als = [jnp.exp(carry[b][0] - mns[b]) for b in heads]
            ps = [jnp.exp(scs[b] - mns[b]) for b in heads]
            ls = [als[b] * carry[b][1] + jnp.sum(ps[b], axis=-1, keepdims=True) for b in heads]
            accs = [als[b] * carry[b][2] + _dot(ps[b].astype(BF16), v_ref[pl.ds(ks, TQ), _lanes(b)]) for b in heads]
            return tuple((mns[b], ls[b], accs[b]) for b in heads)

        init = tuple((jnp.full((TQ, 1), -1e30, F32), jnp.zeros((TQ, 1), F32), jnp.zeros((TQ, HEAD_PAD), F32))
                     for _ in range(nb))
        carry = step(qi, init, True)
        carry = lax.fori_loop(0, qi, lambda kb, c: step(kb, c, False), carry)
        for b in range(nb):
            m, l, acc = carry[b]
            o_ref[:, _lanes(b)] = (acc / l).astype(BF16)
            lse_ref[:, _lanes(b)] = jnp.broadcast_to(m + jnp.log(l), (TQ, HEAD_PAD))

    qspec, kspec = _attn_specs(s, nb)
    return _with_rider(
        body, rider, name=name, grid=(width // (nb * HEAD_PAD), s // TQ),
        in_specs=[qspec, kspec, kspec], out_specs=[qspec, qspec],
        out_shape=[jax.ShapeDtypeStruct((s, width), BF16), jax.ShapeDtypeStruct((s, width), F32)],
        args=(q, k, v), sem=("parallel", "arbitrary"))


def _mla_bwd_call(q, k, v, o, do, lse, name, rider=None):
    s, width = q.shape
    scale = 1.0 / math.sqrt(MLA_QK)
    nb = MLA_BWD_BLOCKS

    def body(q_ref, k_ref, v_ref, o_ref, do_ref, lse_ref, dq_ref, dk_ref, dv_ref):
        qi = pl.program_id(1)

        @pl.when(qi == 0)
        def _():
            dk_ref[...] = jnp.zeros_like(dk_ref)
            dv_ref[...] = jnp.zeros_like(dv_ref)

        qs = [q_ref[:, _lanes(b)] for b in range(nb)]
        dos = [do_ref[:, _lanes(b)] for b in range(nb)]
        lses = [lse_ref[:, b * HEAD_PAD:b * HEAD_PAD + 1] for b in range(nb)]
        dlts = [jnp.sum(dos[b].astype(F32) * o_ref[:, _lanes(b)].astype(F32), axis=-1, keepdims=True) for b in range(nb)]
        causal = _tri(lambda r, c: c <= r)

        def step(kb, dqs, diag):
            ks = pl.multiple_of(kb * TQ, TQ)
            heads = range(nb)
            kts = [k_ref[pl.ds(ks, TQ), _lanes(b)] for b in heads]
            scs = [_dot_nt(qs[b], kts[b]) for b in heads]
            dps = [_dot_nt(dos[b], v_ref[pl.ds(ks, TQ), _lanes(b)]) for b in heads]
            ps = [jnp.exp(scs[b] * scale - lses[b]) for b in heads]
            if diag:
                ps = [jnp.where(causal, p, 0.0) for p in ps]
            dss = [(ps[b] * (dps[b] - dlts[b]) * scale).astype(BF16) for b in heads]
            dvs = [_dot_tn(ps[b].astype(BF16), dos[b]) for b in heads]
            dks = [_dot_tn(dss[b], qs[b]) for b in heads]
            out = tuple(dqs[b] + _dot(dss[b], kts[b]) for b in heads)
            for b in heads:
                dv_ref[pl.ds(ks, TQ), _lanes(b)] += dvs[b]
                dk_ref[pl.ds(ks, TQ), _lanes(b)] += dks[b]
            return out

        dqs = step(qi, tuple(jnp.zeros((TQ, HEAD_PAD), F32) for _ in range(nb)), True)
        dqs = lax.fori_loop(0, qi, lambda kb, c: step(kb, c, False), dqs)
        for b in range(nb):
            dq_ref[:, _lanes(b)] = dqs[b]

    qspec, kspec = _attn_specs(s, nb)
    out = jax.ShapeDtypeStruct((s, width), F32)
    return _with_rider(
        body, rider, name=name, grid=(width // (nb * HEAD_PAD), s // TQ),
        in_specs=[qspec, kspec, kspec, qspec, qspec, qspec], out_specs=[qspec, kspec, kspec],
        out_shape=[out, out, out], args=(q, k, v, o, do, lse), sem=("parallel", "arbitrary"))


def _dot_hilo(x, u):
    hi = x.astype(BF16)
    lo = (x - hi.astype(F32)).astype(BF16)
    return _dot(hi, u) + _dot(lo, u)


def _sb_logs(z):
    ls = jnp.minimum(z, 0.0) - jnp.log(1.0 + jnp.exp(-jnp.abs(z)))
    return ls, ls - z


def _sb_head_q(qb, first, hh):
    keep = first if hh == 0 else jnp.logical_not(first)
    return jnp.where(keep, qb, jnp.zeros_like(qb)) * jnp.asarray(SB_SCALE, qb.dtype)


def _sb_fwd_call(q, k, v, name):
    s, width = q.shape
    nb = SB_FWD_BLOCKS
    chains = [(b, hh) for b in range(nb) for hh in range(HEAD_PAD // SB_HEAD)]

    def body(q_ref, k_ref, v_ref, o_ref):
        qi = pl.program_id(1)
        strict = _tri(lambda r, c: c < r)
        after = _tri(lambda r, c: r > c).astype(BF16)
        first = _lane((1, HEAD_PAD)) < SB_HEAD
        qhs = [_sb_head_q(q_ref[:, _lanes(b)], first, hh) for b, hh in chains]

        def step(kb, carry, diag):
            ks = pl.multiple_of(kb * TQ, TQ)
            ids = range(len(chains))
            zs = [_dot_nt(qhs[ci], k_ref[pl.ds(ks, TQ), _lanes(chains[ci][0])]) for ci in ids]
            logs = [_sb_logs(z) for z in zs]
            lss = [lg[0] for lg in logs]
            l1ms = [jnp.where(strict, lg[1], 0.0) if diag else lg[1] for lg in logs]
            sufs = [_dot_hilo(l1m, after) for l1m in l1ms]
            as_ = [jnp.exp(lss[ci] + sufs[ci] + carry[ci][0]) for ci in ids]
            if diag:
                as_ = [jnp.where(strict, a, 0.0) for a in as_]
            accs = [carry[ci][1] + _dot(as_[ci].astype(BF16), v_ref[pl.ds(ks, TQ), _lanes(chains[ci][0])]) for ci in ids]
            return tuple((carry[ci][0] + jnp.sum(l1ms[ci], axis=-1, keepdims=True), accs[ci]) for ci in ids)

        init = tuple((jnp.zeros((TQ, 1), F32), jnp.zeros((TQ, HEAD_PAD), F32)) for _ in chains)
        carry = _sb_sweep(step, qi, init)
        for b in range(nb):
            o_ref[:, _lanes(b)] = jnp.where(first, carry[2 * b][1], carry[2 * b + 1][1])

    qspec, kspec = _attn_specs(s, nb)
    return pl.pallas_call(
        body, name=name, grid=(width // (nb * HEAD_PAD), s // TQ),
        in_specs=[qspec, kspec, kspec], out_specs=qspec, out_shape=jax.ShapeDtypeStruct((s, width), F32),
        compiler_params=_cp(("parallel", "arbitrary")))(q, k, v)


def _sb_sweep(step, qi, init):
    def live(carry):
        top = carry[0][0]
        for c in carry[1:]:
            top = jnp.maximum(top, c[0])
        return jnp.max(top)

    carry = step(qi, init, True)

    def cond(state):
        j, alive, _ = state
        return jnp.logical_and(j < qi, alive > SB_DEAD)

    def body(state):
        j, _, carry = state
        carry = step(qi - 1 - j, carry, False)
        return j + 1, live(carry), carry

    return lax.while_loop(cond, body, (jnp.int32(0), live(carry), carry))[2]


def _sb_bwd_call(q, k, v, do, o, name):
    s, width = q.shape
    nb = SB_BWD_BLOCKS
    chains = [(b, hh) for b in range(nb) for hh in range(HEAD_PAD // SB_HEAD)]

    def body(q_ref, k_ref, v_ref, do_ref, o_ref, dq_ref, dk_ref, dv_ref):
        qi = pl.program_id(1)

        @pl.when(qi == 0)
        def _():
            dk_ref[...] = jnp.zeros_like(dk_ref)
            dv_ref[...] = jnp.zeros_like(dv_ref)

        strict = _tri(lambda r, c: c < r)
        after = _tri(lambda r, c: r > c).astype(BF16)
        from_here = _tri(lambda r, c: r >= c).astype(BF16)
        first = _lane((1, HEAD_PAD)) < SB_HEAD
        qhs = [_sb_head_q(q_ref[:, _lanes(b)], first, hh) for b, hh in chains]
        dohs = []
        for b, hh in chains:
            dob = do_ref[:, _lanes(b)]
            dohs.append(jnp.where(first if hh == 0 else jnp.logical_not(first), dob, jnp.zeros_like(dob)))
        gtots = [jnp.sum(dohs[ci].astype(F32) * o_ref[:, _lanes(chains[ci][0])], axis=-1, keepdims=True)
                 for ci in range(len(chains))]

        def step(kb, carry, diag):
            ks = pl.multiple_of(kb * TQ, TQ)
            ids = range(len(chains))
            kts = [k_ref[pl.ds(ks, TQ), _lanes(b)] for b, _ in chains]
            zs = [_dot_nt(qhs[ci], kts[ci]) for ci in ids]
            das = [_dot_nt(dohs[ci], v_ref[pl.ds(ks, TQ), _lanes(chains[ci][0])]) for ci in ids]
            logs = [_sb_logs(z) for z in zs]
            lss = [lg[0] for lg in logs]
            l1ms = [jnp.where(strict, lg[1], 0.0) if diag else lg[1] for lg in logs]
            sufs = [_dot_hilo(l1m, after) for l1m in l1ms]
            as_ = [jnp.exp(lss[ci] + sufs[ci] + carry[ci][0]) for ci in ids]
            if diag:
                as_ = [jnp.where(strict, a, 0.0) for a in as_]
            abs_ = [a.astype(BF16) for a in as_]
            gs = [abs_[ci].astype(F32) * das[ci] for ci in ids]
            cexs = [gtots[ci] - (carry[ci][1] + _dot_hilo(gs[ci], from_here)) for ci in ids]
            dzs = [gs[ci] - jnp.exp(lss[ci]) * (gs[ci] + cexs[ci]) for ci in ids]
            if diag:
                dzs = [jnp.where(strict, dz, 0.0) for dz in dzs]
            dzbs = [dz.astype(BF16) for dz in dzs]
            dvps = [_dot_tn(abs_[ci], dohs[ci]) for ci in ids]
            dkps = [_dot_tn(dzbs[ci], qhs[ci]) for ci in ids]
            out = tuple((carry[ci][0] + jnp.sum(l1ms[ci], axis=-1, keepdims=True),
                         carry[ci][1] + jnp.sum(gs[ci], axis=-1, keepdims=True),
                         carry[ci][2] + _dot(dzbs[ci], kts[ci])) for ci in ids)
            for b in range(nb):
                dk_ref[pl.ds(ks, TQ), _lanes(b)] += dkps[2 * b] + dkps[2 * b + 1]
                dv_ref[pl.ds(ks, TQ), _lanes(b)] += dvps[2 * b] + dvps[2 * b + 1]
            return out

        init = tuple((jnp.zeros((TQ, 1), F32), jnp.zeros((TQ, 1), F32), jnp.zeros((TQ, HEAD_PAD), F32)) for _ in chains)
        carry = _sb_sweep(step, qi, init)
        for b in range(nb):
            dq_ref[:, _lanes(b)] = (jnp.where(first, carry[2 * b][2], carry[2 * b + 1][2]) * SB_SCALE).astype(BF16)

    qspec, kspec = _attn_specs(s, nb)
    return pl.pallas_call(
        body, name=name, grid=(width // (nb * HEAD_PAD), s // TQ),
        in_specs=[qspec, kspec, kspec, qspec, qspec], out_specs=[qspec, kspec, kspec],
        out_shape=[jax.ShapeDtypeStruct((s, width), BF16), jax.ShapeDtypeStruct((s, width), F32),
                   jax.ShapeDtypeStruct((s, width), F32)],
        compiler_params=_cp(("parallel", "arbitrary")))(q, k, v, do, o)


def _merge_out_call(om, osb, gates, h, wbm, wbs, wo, name):
    s, d = h.shape
    tm = min(TM_SMALL, s)

    def body(om_ref, os_ref, g_ref, h_ref, wbm_ref, wbs_ref, wo_ref, h2_ref, bm_ref, bs_ref, mg_ref):
        bm = _dot(om_ref[...], wbm_ref[...])
        bs = _dot(os_ref[...].astype(BF16), wbs_ref[...])
        mg = (_sigmoid(g_ref[:, :d]) * bm + _sigmoid(g_ref[:, d:]) * bs).astype(BF16)
        bm_ref[...] = bm
        bs_ref[...] = bs
        mg_ref[...] = mg
        h2_ref[...] = h_ref[...] + _dot(mg, wo_ref[...])

    return pl.pallas_call(
        body, name=name, grid=(s // tm,),
        in_specs=[_rows(tm, om.shape[1]), _rows(tm, SB_WIDTH), _rows(tm, 2 * d), _rows(tm, d),
                  _whole(wbm.shape), _whole(wbs.shape), _whole(wo.shape)],
        out_specs=[_rows(tm, d)] * 4,
        out_shape=[jax.ShapeDtypeStruct((s, d), F32), jax.ShapeDtypeStruct((s, d), F32),
                   jax.ShapeDtypeStruct((s, d), F32), jax.ShapeDtypeStruct((s, d), BF16)],
        compiler_params=_cp(("parallel",)))(om, osb, gates, h, wbm, wbs, wo)


def _ple_call(h, g, wg, p, wp, tgt, name):
    s, d = h.shape
    tm = min(TM_SMALL, s)

    def body(h_ref, g_ref, wg_ref, p_ref, wp_ref, t_ref, dh_ref, dhs_ref, un_ref, dgl_ref, dpp_ref, dg_ref, sq_ref):
        @pl.when(pl.program_id(0) == 0)
        def _():
            dg_ref[...] = jnp.zeros_like(dg_ref)
            sq_ref[...] = jnp.zeros_like(sq_ref)

        x = h_ref[...]
        gain = g_ref[...]
        r = _rstd(x, d)
        xh = x * r
        un = (xh * gain).astype(BF16)
        sg = _sigmoid(_dot(un, wg_ref[...]))
        pp = _dot(p_ref[...].astype(BF16), wp_ref[...])
        diff = (x + sg * pp) - t_ref[...]
        sq_ref[...] += jnp.sum(diff * diff, axis=0, keepdims=True)
        dy = diff * (1.0 / d)
        dgl = ((dy * pp) * (sg * (1.0 - sg))).astype(BF16)
        dun = _dot_nt(dgl, wg_ref[...])
        dg_ref[...] += jnp.sum(dun * xh, axis=0, keepdims=True)
        dh = dy + _rms_bwd(x, r, gain, dun, d)
        dh_ref[...] = dh
        dhs_ref[...] = (0.5 * dh).astype(BF16)
        un_ref[...] = un
        dgl_ref[...] = dgl
        dpp_ref[...] = (dy * sg).astype(BF16)

    bf = jax.ShapeDtypeStruct((s, d), BF16)
    vec = jax.ShapeDtypeStruct((1, d), F32)
    return pl.pallas_call(
        body, name=name, grid=(s // tm,),
        in_specs=[_rows(tm, d), _whole((1, d)), _whole(wg.shape), _rows(tm, PLE_DIM), _whole(wp.shape), _rows(tm, d)],
        out_specs=[_rows(tm, d)] * 5 + [_whole((1, d))] * 2,
        out_shape=[jax.ShapeDtypeStruct((s, d), F32), bf, bf, bf, bf, vec, vec],
        compiler_params=_cp(("arbitrary",)))(h, g, wg, p, wp, tgt)


def _ffn_bwd_a_call(dhs, a, b, wo, name, rider=None):
    s, n = a.shape
    d = dhs.shape[1]
    tn = n // 2
    tm = min(TM, s)

    def body(dh_ref, a_ref, b_ref, wo_ref, da_ref, db_ref):
        dh = dh_ref[...]
        for c0 in range(0, tn, COL_CHUNK):
            sl = slice(c0, min(c0 + COL_CHUNK, tn))
            dhm = _dot_nt(dh, wo_ref[sl, :])
            av = a_ref[:, sl]
            sa = _sigmoid(av)
            da_ref[:, sl] = (dhm * b_ref[:, sl] * (sa * (1.0 + av * (1.0 - sa)))).astype(BF16)
            db_ref[:, sl] = (dhm * (av * sa)).astype(BF16)

    blk = pl.BlockSpec((tm, tn), lambda j, i: (i, j))
    return _with_rider(
        body, rider, name=name, grid=(n // tn, s // tm),
        in_specs=[pl.BlockSpec((tm, d), lambda j, i: (i, 0)), blk, blk, pl.BlockSpec((tn, d), lambda j, i: (j, 0))],
        out_specs=[blk, blk],
        out_shape=[jax.ShapeDtypeStruct((s, n), BF16)] * 2, args=(dhs, a, b, wo), sem=("parallel", "parallel"))


def _norm_bwd_call(dy_list, w_list, h, g, dh_in, name, half_out, rider=None):
    s, d = h.shape
    tm = min(TM_SMALL, s)
    nk = len(dy_list)
    factor = 0.5 if half_out else 1.0

    def body(*refs):
        dy_refs = refs[:nk]
        w_refs = refs[nk:2 * nk]
        h_ref, g_ref, dhin_ref, dh_ref, dhb_ref, dg_ref = refs[2 * nk:]

        @pl.when(pl.program_id(0) == 0)
        def _():
            dg_ref[...] = jnp.zeros_like(dg_ref)

        du = _dot_nt(dy_refs[0][...], w_refs[0][...])
        for dy_ref, w_ref in zip(dy_refs[1:], w_refs[1:]):
            du = du + _dot_nt(dy_ref[...], w_ref[...])
        x = h_ref[...]
        r = _rstd(x, d)
        dg_ref[...] += jnp.sum(du * (x * r), axis=0, keepdims=True)
        dh = dhin_ref[...] + _rms_bwd(x, r, g_ref[...], du, d)
        dh_ref[...] = dh
        dhb_ref[...] = (factor * dh).astype(BF16)

    outs, got = _with_rider(
        body, rider, name=name, grid=(s // tm,),
        in_specs=[_rows(tm, dy.shape[1]) for dy in dy_list] + [_whole(w.shape) for w in w_list]
        + [_rows(tm, d), _whole((1, d)), _rows(tm, d)],
        out_specs=[_rows(tm, d), _rows(tm, d), _whole((1, d))],
        out_shape=[jax.ShapeDtypeStruct((s, d), F32), jax.ShapeDtypeStruct((s, d), BF16),
                   jax.ShapeDtypeStruct((1, d), F32)],
        args=(*dy_list, *w_list, h, g, dh_in), sem=("arbitrary",))
    return outs if rider is None else (outs, got)


def _merge_bwd_call(dhb, gates, bm, bs, wo, wbm, wbs, name):
    s, d = bm.shape
    tm = min(TM_SMALL, s)

    def body(dh_ref, g_ref, bm_ref, bs_ref, wo_ref, wbm_ref, wbs_ref, dg_ref, dbm_ref, dbs_ref, dom_ref, dos_ref):
        dmg = _dot_nt(dh_ref[...], wo_ref[...])
        s1 = _sigmoid(g_ref[:, :d])
        s2 = _sigmoid(g_ref[:, d:])
        dg_ref[:, :d] = (dmg * bm_ref[...] * (s1 * (1.0 - s1))).astype(BF16)
        dg_ref[:, d:] = (dmg * bs_ref[...] * (s2 * (1.0 - s2))).astype(BF16)
        dbm = (dmg * s1).astype(BF16)
        dbs = (dmg * s2).astype(BF16)
        dbm_ref[...] = dbm
        dbs_ref[...] = dbs
        dom_ref[...] = _dot_nt(dbm, wbm_ref[...]).astype(BF16)
        dos_ref[...] = _dot_nt(dbs, wbs_ref[...]).astype(BF16)

    wm = wbm.shape[0]
    return pl.pallas_call(
        body, name=name, grid=(s // tm,),
        in_specs=[_rows(tm, d), _rows(tm, 2 * d), _rows(tm, d), _rows(tm, d),
                  _whole(wo.shape), _whole(wbm.shape), _whole(wbs.shape)],
        out_specs=[_rows(tm, 2 * d), _rows(tm, d), _rows(tm, d), _rows(tm, wm), _rows(tm, SB_WIDTH)],
        out_shape=[jax.ShapeDtypeStruct((s, 2 * d), BF16), jax.ShapeDtypeStruct((s, d), BF16),
                   jax.ShapeDtypeStruct((s, d), BF16), jax.ShapeDtypeStruct((s, wm), BF16),
                   jax.ShapeDtypeStruct((s, SB_WIDTH), BF16)],
        compiler_params=_cp(("parallel",)))(dhb, gates, bm, bs, wo, wbm, wbs)


def _mla_prep_bwd_call(cq, ckv, krope, pos, freq, sign, g_ql, g_kvl, g_qh, g_kh, wq, wkv, dq, dk, dv, name):
    s = cq.shape[0]
    tm = min(TM_SMALL, s)
    width = HEADS * HEAD_PAD

    def body(cq_ref, ckv_ref, kr_ref, pos_ref, freq_ref, sign_ref, gql_ref, gkvl_ref, gqh_ref, gkh_ref,
             wq_ref, wkv_ref, dq_ref, dk_ref, dv_ref,
             dcq_ref, dckv_ref, dkr_ref, dwq_ref, dwkv_ref, dgql_ref, dgkvl_ref, dgqh_ref, dgkh_ref, dqr_ref, dkv_ref):
        @pl.when(pl.program_id(0) == 0)
        def _():
            for ref in (dwq_ref, dwkv_ref, dgql_ref, dgkvl_ref, dgqh_ref, dgkh_ref):
                ref[...] = jnp.zeros_like(ref)

        cosv, ssv = _rope_tables(pos_ref, freq_ref, sign_ref)
        xq = cq_ref[...]
        rq = _rstd(xq, Q_LORA)
        cqn = ((xq * rq) * gql_ref[...]).astype(BF16)
        qr = _dot(cqn, wq_ref[...])
        xk = ckv_ref[...]
        rk = _rstd(xk, KV_LORA)
        ckvn = ((xk * rk) * gkvl_ref[...]).astype(BF16)
        kv = _dot(ckvn, wkv_ref[...])
        kr = kr_ref[...]
        lane = _lane((tm, HEAD_PAD))
        dkr = jnp.zeros((tm, HEAD_PAD), F32)
        dgqh = jnp.zeros((1, HEAD_PAD), F32)
        dgkh = jnp.zeros((1, HEAD_PAD), F32)
        for h in range(HEADS):
            sl = slice(h * HEAD_PAD, (h + 1) * HEAD_PAD)
            x = qr[:, sl]
            dx, dgh = _head_bwd(x, _rstd(x, MLA_QK), gqh_ref[...], cosv, ssv, dq_ref[:, sl])
            dqr_ref[:, sl] = dx.astype(BF16)
            dgqh = dgqh + dgh
            x = jnp.where(lane < MLA_NOPE, kv[:, sl], kr)
            dx, dgh = _head_bwd(x, _rstd(x, MLA_QK), gkh_ref[...], cosv, ssv, dk_ref[:, sl])
            dgkh = dgkh + dgh
            dkr = dkr + jnp.where(lane >= MLA_NOPE, dx, 0.0)
            dkv_ref[:, sl] = jnp.where(lane < MLA_NOPE, dx, dv_ref[:, sl]).astype(BF16)
        dgqh_ref[...] += dgqh
        dgkh_ref[...] += dgkh
        dkr_ref[...] = dkr.astype(BF16)
        dqr = dqr_ref[...]
        dkvb = dkv_ref[...]
        dwq_ref[...] += _dot_tn(cqn, dqr)
        dwkv_ref[...] += _dot_tn(ckvn, dkvb)
        dcqn = _dot_nt(dqr, wq_ref[...])
        dgql_ref[...] += jnp.sum(dcqn * (xq * rq), axis=0, keepdims=True)
        dcq_ref[...] = _rms_bwd(xq, rq, gql_ref[...], dcqn, Q_LORA).astype(BF16)
        dckvn = _dot_nt(dkvb, wkv_ref[...])
        dgkvl_ref[...] += jnp.sum(dckvn * (xk * rk), axis=0, keepdims=True)
        dckv_ref[...] = _rms_bwd(xk, rk, gkvl_ref[...], dckvn, KV_LORA).astype(BF16)

    vec = lambda n: jax.ShapeDtypeStruct((1, n), F32)
    outs = pl.pallas_call(
        body, name=name, grid=(s // tm,),
        in_specs=[_rows(tm, Q_LORA), _rows(tm, KV_LORA), _rows(tm, HEAD_PAD), _rows(tm, 1),
                  _whole((1, HEAD_PAD)), _whole((1, HEAD_PAD)), _whole((1, Q_LORA)), _whole((1, KV_LORA)),
                  _whole((1, HEAD_PAD)), _whole((1, HEAD_PAD)), _whole((Q_LORA, width)), _whole((KV_LORA, width)),
                  _rows(tm, width), _rows(tm, width), _rows(tm, width)],
        out_specs=[_rows(tm, Q_LORA), _rows(tm, KV_LORA), _rows(tm, HEAD_PAD), _whole((Q_LORA, width)),
                   _whole((KV_LORA, width)), _whole((1, Q_LORA)), _whole((1, KV_LORA)), _whole((1, HEAD_PAD)),
                   _whole((1, HEAD_PAD)), _rows(tm, width), _rows(tm, width)],
        out_shape=[jax.ShapeDtypeStruct((s, Q_LORA), BF16), jax.ShapeDtypeStruct((s, KV_LORA), BF16),
                   jax.ShapeDtypeStruct((s, HEAD_PAD), BF16), jax.ShapeDtypeStruct((Q_LORA, width), F32),
                   jax.ShapeDtypeStruct((KV_LORA, width), F32), vec(Q_LORA), vec(KV_LORA), vec(HEAD_PAD), vec(HEAD_PAD),
                   jax.ShapeDtypeStruct((s, width), BF16), jax.ShapeDtypeStruct((s, width), BF16)],
        compiler_params=_cp(("arbitrary",)))(cq, ckv, krope, pos, freq, sign, g_ql, g_kvl, g_qh, g_kh, wq, wkv, dq, dk, dv)
    return outs[:9]


def _tn_call(a, b, name, shard_cols=None, rider=None):
    s, ka = a.shape
    nb = b.shape[1]
    ti = _pick(ka, (512, 256, 128))
    if shard_cols is not None:
        tj = shard_cols
    else:
        tj = nb if nb <= TN_MAX_COLS else _pick(nb, (2176, 1024, 512, 256, 128))
    ts = s if 2 * s * (ti + tj) * a.dtype.itemsize <= TN_OPERAND_BYTES else s // 2
    ns = s // ts

    def body(a_ref, b_ref, o_ref, acc_ref):
        part = _dot_tn(a_ref[...].astype(BF16), b_ref[...].astype(BF16))
        if ns == 1:
            o_ref[...] = part.astype(o_ref.dtype)
            return

        @pl.when(pl.program_id(2) == 0)
        def _():
            acc_ref[...] = part

        @pl.when(pl.program_id(2) != 0)
        def _():
            acc_ref[...] += part

        @pl.when(pl.program_id(2) == ns - 1)
        def _():
            o_ref[...] = acc_ref[...].astype(o_ref.dtype)

    if shard_cols is None:
        out_spec = pl.BlockSpec((ti, tj), lambda i, j, t: (i, j))
        out_shape = jax.ShapeDtypeStruct((ka, nb), BF16)
    else:
        out_spec = pl.BlockSpec((None, ti, tj), lambda i, j, t: (j, i, 0))
        out_shape = jax.ShapeDtypeStruct((nb // tj, ka, tj), BF16)
    (out,), got = _with_rider(
        body, rider, name=name, grid=(ka // ti, nb // tj, ns),
        in_specs=[pl.BlockSpec((ts, ti), lambda i, j, t: (t, i)), pl.BlockSpec((ts, tj), lambda i, j, t: (t, j))],
        out_specs=[out_spec], out_shape=[out_shape], scratch=[pltpu.VMEM((ti, tj), F32)], args=(a, b),
        sem=("parallel", "parallel", "arbitrary"))
    return out if rider is None else (out, got)


def _sum_call(parts, out_dtype, name):
    n, r, w = parts.shape
    tr = _row_tile(r)

    def body(p_ref, o_ref):
        acc = p_ref[0].astype(F32)
        for k in range(1, n):
            acc = acc + p_ref[k].astype(F32)
        o_ref[...] = acc.astype(out_dtype)

    return pl.pallas_call(
        body, name=name, grid=(r // tr,),
        in_specs=[pl.BlockSpec((n, tr, w), lambda i: (0, i, 0))], out_specs=_rows(tr, w),
        out_shape=jax.ShapeDtypeStruct((r, w), out_dtype), compiler_params=_cp(("parallel",)))(parts)


def _chip_sum_call(by_chip, core, name):
    n, r, w = by_chip.shape
    tr = _row_tile(r)
    nblk = r // tr

    def body(c_ref, p_ref, o_ref):
        acc = p_ref[0].astype(F32)
        for k in range(1, n):
            acc = acc + p_ref[k].astype(F32)
        o_ref[...] = acc

    return pl.pallas_call(
        body, name=name,
        grid_spec=pltpu.PrefetchScalarGridSpec(
            num_scalar_prefetch=1, grid=(nblk,),
            in_specs=[pl.BlockSpec((n, tr, w), lambda i, c_ref: (0, i, 0))],
            out_specs=pl.BlockSpec((tr, w), lambda i, c_ref: (c_ref[0] * nblk + i, 0))),
        out_shape=jax.ShapeDtypeStruct((2 * r, w), F32),
        compiler_params=_cp(("parallel",)))(core.reshape(1).astype(jnp.int32), by_chip)


def _pair_sum_call(full, other, core, out_dtype, name):
    n, r, w = other.shape
    tr = _row_tile(r)
    nblk = r // tr

    def body(c_ref, a_ref, b_ref, o_ref):
        o_ref[...] = (a_ref[...].astype(F32) + b_ref[...].astype(F32)).astype(out_dtype)

    spec = pl.BlockSpec((None, tr, w), lambda k, i, c_ref: (k, i, 0))
    return pl.pallas_call(
        body, name=name,
        grid_spec=pltpu.PrefetchScalarGridSpec(
            num_scalar_prefetch=1, grid=(n, nblk),
            in_specs=[pl.BlockSpec((None, tr, w), lambda k, i, c_ref: (k, c_ref[0] * nblk + i, 0)), spec],
            out_specs=spec),
        out_shape=jax.ShapeDtypeStruct((n, r, w), out_dtype),
        compiler_params=_cp(("parallel", "parallel")))(core.reshape(1).astype(jnp.int32), full, other)


def _adamw_call(w, g, row0, m, v, name):
    r, c = w.shape
    tr = _pick(math.gcd(r, row0) if row0 else r, (256, 128, 64, 32, 16, 8))
    off = row0 // tr

    def body(w_ref, g_ref, m_ref, v_ref, g_out_ref, d_ref, nm_ref, nv_ref):
        gg = g_ref[...]
        g_out_ref[...] = gg
        nm = ADAM_B1 * m_ref[...] + (1.0 - ADAM_B1) * gg
        nv = ADAM_B2 * v_ref[...] + (1.0 - ADAM_B2) * (gg * gg)
        m_hat = nm / (1.0 - ADAM_B1 ** ADAM_STEP)
        v_hat = nv / (1.0 - ADAM_B2 ** ADAM_STEP)
        d_ref[...] = -ADAM_LR * (m_hat / (jnp.sqrt(v_hat) + ADAM_EPS) + ADAM_WD * w_ref[...])
        nm_ref[...] = nm
        nv_ref[...] = nv

    out = jax.ShapeDtypeStruct((r, c), F32)
    g_spec = pl.BlockSpec((tr, c), lambda i: (off + i, 0))
    return pl.pallas_call(
        body, name=name, grid=(r // tr,), in_specs=[_rows(tr, c), g_spec, _rows(tr, c), _rows(tr, c)],
        out_specs=[_rows(tr, c)] * 4, out_shape=[out, out, out, out], compiler_params=_cp(("parallel",)))(w, g, m, v)


def _position():
    x, y, c = lax.axis_index("x"), lax.axis_index("y"), lax.axis_index("c")
    chips = [(1 - x, y), (x, 1 - y), (1 - x, 1 - y)]
    return x, y, c, chips


def _gather_rider(parts):
    n = len(parts)
    pairs = [(j, k) for j in range(3) for k in range(n)]

    def piece(out_refs, k, chip, core):
        half = parts[k].shape[0] // 2
        return out_refs[k].at[2 * chip[0] + chip[1], pl.ds(core * half, half), :]

    def over_ici(in_refs, out_refs, sems, j, k):
        x, y, c, chips = _position()
        half = parts[k].shape[0] // 2
        return pltpu.make_async_remote_copy(
            src_ref=in_refs[k].at[pl.ds(c * half, half), :], dst_ref=piece(out_refs, k, (x, y), c),
            send_sem=sems[0].at[n * j + k], recv_sem=sems[1].at[n * j + k], device_id=(*chips[j], c), device_id_type=MESH)

    def to_sibling(out_refs, sems, j, k):
        x, y, c, chips = _position()
        landed = piece(out_refs, k, chips[j], c)
        return pltpu.make_async_remote_copy(
            src_ref=landed, dst_ref=landed, send_sem=sems[2].at[n * j + k], recv_sem=sems[3].at[n * j + k],
            device_id=(x, y, 1 - c), device_id_type=MESH)

    def start(in_refs, out_refs, sems):
        for j, k in pairs:
            over_ici(in_refs, out_refs, sems, j, k).start()

    def finish(in_refs, out_refs, sems):
        for j, k in pairs:
            over_ici(in_refs, out_refs, sems, j, k).wait_recv()
            to_sibling(out_refs, sems, j, k).start()
        for j, k in pairs:
            to_sibling(out_refs, sems, j, k).wait_recv()
        for j, k in pairs:
            over_ici(in_refs, out_refs, sems, j, k).wait_send()
            to_sibling(out_refs, sems, j, k).wait_send()

    return _Rider(list(parts), [jax.ShapeDtypeStruct((N_CHIPS,) + p.shape, p.dtype) for p in parts], [3 * n] * 4,
                  start, finish)


def _scatter_rider(parts):
    n = len(parts)
    pairs = [(j, k) for j in range(3) for k in range(n)]

    def copy(in_refs, out_refs, sems, j, k):
        x, y, c, chips = _position()
        return pltpu.make_async_remote_copy(
            src_ref=in_refs[k].at[2 * chips[j][0] + chips[j][1]], dst_ref=out_refs[k].at[2 * x + y],
            send_sem=sems[0].at[n * j + k], recv_sem=sems[1].at[n * j + k], device_id=(*chips[j], c), device_id_type=MESH)

    def start(in_refs, out_refs, sems):
        for j, k in pairs:
            copy(in_refs, out_refs, sems, j, k).start()

    def finish(in_refs, out_refs, sems):
        for j, k in pairs:
            copy(in_refs, out_refs, sems, j, k).wait()

    return _Rider(list(parts), [jax.ShapeDtypeStruct(p.shape, p.dtype) for p in parts], [3 * n] * 2, start, finish)


def _exchange_call(rider, name):
    n, m = len(rider.ins), len(rider.out_shape)

    def body(*refs):
        rider.start(refs[:n], refs[n:n + m], refs[n + m:])
        rider.finish(refs[:n], refs[n:n + m], refs[n + m:])

    return pl.pallas_call(
        body, name=name, in_specs=[HBM] * n, out_specs=[HBM] * m, out_shape=rider.out_shape,
        scratch_shapes=[pltpu.SemaphoreType.DMA((k,)) for k in rider.sems])(*rider.ins)


def _pair_send_call(parts, name):
    n = len(parts)

    def body(*refs):
        in_refs, out_refs = refs[:n], refs[n:2 * n]
        send_sems, recv_sems = refs[2 * n:]
        x, y, c, _ = _position()
        copies = []
        for k in range(n):
            half = parts[k].shape[1] // 2
            cp = pltpu.make_async_remote_copy(
                src_ref=in_refs[k].at[:, pl.ds((1 - c) * half, half), :], dst_ref=out_refs[k],
                send_sem=send_sems.at[k], recv_sem=recv_sems.at[k], device_id=(x, y, 1 - c), device_id_type=MESH)
            cp.start()
            copies.append(cp)
        for cp in copies:
            cp.wait()

    sems = pltpu.SemaphoreType.DMA((n,))
    return pl.pallas_call(
        body, name=name, in_specs=[HBM] * n, out_specs=[HBM] * n,
        out_shape=[jax.ShapeDtypeStruct((p.shape[0], p.shape[1] // 2, p.shape[2]), p.dtype) for p in parts],
        scratch_shapes=[sems, sems])(*parts)


def _pair_swap_call(parts, name):
    n = len(parts)

    def body(*refs):
        out_refs = refs[n:2 * n]
        send_sems, recv_sems = refs[2 * n:]
        x, y, c, _ = _position()
        copies = []
        for k in range(n):
            half = parts[k].shape[0] // 2
            mine = out_refs[k].at[pl.ds(c * half, half), :]
            cp = pltpu.make_async_remote_copy(
                src_ref=mine, dst_ref=mine, send_sem=send_sems.at[k], recv_sem=recv_sems.at[k],
                device_id=(x, y, 1 - c), device_id_type=MESH)
            cp.start()
            copies.append(cp)
        for cp in copies:
            cp.wait()

    sems = pltpu.SemaphoreType.DMA((n,))
    return pl.pallas_call(
        body, name=name, in_specs=[HBM] * n, out_specs=[HBM] * n,
        out_shape=[jax.ShapeDtypeStruct(p.shape, p.dtype) for p in parts],
        input_output_aliases={k: k for k in range(n)},
        scratch_shapes=[sems, sems])(*parts)


def _all_gather_small_call(block, name):
    r, w = block.shape

    def body(in_ref, out_ref, send_sems, recv_sems, local_sem):
        x, y, c, _ = _position()
        me = 4 * x + 2 * y + c
        own = pltpu.make_async_copy(in_ref, out_ref.at[me], local_sem)
        own.start()
        copies = []
        for k in range(1, 8):
            peer = (x ^ (k >> 2), y ^ ((k >> 1) & 1), c ^ (k & 1))
            cp = pltpu.make_async_remote_copy(
                src_ref=in_ref, dst_ref=out_ref.at[me], send_sem=send_sems.at[k - 1], recv_sem=recv_sems.at[k - 1],
                device_id=peer, device_id_type=MESH)
            cp.start()
            copies.append(cp)
        for cp in copies:
            cp.wait()
        own.wait()

    return pl.pallas_call(
        body, name=name, in_specs=[HBM], out_specs=HBM,
        out_shape=jax.ShapeDtypeStruct((8, r, w), block.dtype),
        scratch_shapes=[pltpu.SemaphoreType.DMA((7,)), pltpu.SemaphoreType.DMA((7,)), pltpu.SemaphoreType.DMA])(block)


BIG = {
    "ffn1_w_in": ((D_MODEL, 2 * D_FF), 1), "ffn1_w_out": ((D_FF, D_MODEL), 0),
    "w_in": ((D_MODEL, 4256), 1), "w_q_up": ((Q_LORA, HEADS * MLA_QK), 1), "w_kv_up": ((KV_LORA, 1024), 1),
    "w_branch_mla": ((512, D_MODEL), 1), "w_branch_sb": ((SB_WIDTH, D_MODEL), 1), "w_out": ((D_MODEL, D_MODEL), 0),
    "ffn2_w_in": ((D_MODEL, 2 * D_FF), 1), "ffn2_w_out": ((D_FF, D_MODEL), 0),
    "w_ple_gate": ((D_MODEL, D_MODEL), 0), "w_ple_proj": ((PLE_DIM, D_MODEL), 1),
}
GAINS = {"ffn1_norm": 1024, "mix_norm": 1024, "q_latent_norm": 384, "kv_latent_norm": 256, "q_head_norm": 96,
         "k_head_norm": 96, "ffn2_norm": 1024, "ple_norm": 1024}
WEIGHT_ORDER = ["ffn1_norm", "ffn1_w_in", "ffn1_w_out", "mix_norm", "w_in", "q_latent_norm", "w_q_up",
                "kv_latent_norm", "w_kv_up", "q_head_norm", "k_head_norm", "w_branch_mla", "w_branch_sb", "w_out",
                "ffn2_norm", "ffn2_w_in", "ffn2_w_out", "ple_norm", "w_ple_gate", "w_ple_proj"]


def _shard_shape(name):
    (r, c), axis = BIG[name]
    return (r // N_CHIPS, c) if axis == 0 else (r, c // N_CHIPS)


GATHER_GROUPS = [
    [("ffn1_w_in",)],
    [("ffn1_w_out",), ("w_in",)],
    [("w_out",), ("w_kv_up", "w_branch_mla", "w_branch_sb"), ("w_q_up",)],
    [("ffn2_w_in",), ("ffn2_w_out", "w_ple_gate"), ("w_ple_proj",)],
]
REDUCE_GROUPS = [
    [("ffn2_w_in",), ("ffn2_w_out", "w_out", "w_ple_gate"), ("w_branch_mla", "w_branch_sb", "w_ple_proj")],
    [("w_in",), ("w_kv_up",), ("w_q_up",)],
    [("ffn1_w_out",)],
    [("ffn1_w_in",)],
]


def _join_parts(shards, group):
    return [shards[part[0]] if len(part) == 1 else jnp.concatenate([shards[n] for n in part], axis=-2) for part in group]


def _part_rows(group):
    where = {}
    for k, part in enumerate(group):
        at = 0
        for n in part:
            where[n] = (k, at)
            at += _shard_shape(n)[0]
    return where


def _split_parts(parts, group):
    return {n: parts[k][..., at:at + _shard_shape(n)[0], :] for n, (k, at) in _part_rows(group).items()}


def _to_shards(name, full):
    (r, c), axis = BIG[name]
    if axis == 0:
        return full.reshape(N_CHIPS, r // N_CHIPS, c)
    return full.reshape(r, N_CHIPS, c // N_CHIPS).transpose(1, 0, 2)


def _from_shards(name, shards):
    (r, c), axis = BIG[name]
    if axis == 0:
        return shards.reshape(r, c)
    return shards.transpose(1, 0, 2).reshape(r, c)


def _relayout_w_in(w):
    d = w.shape[0]
    z = lambda n: jnp.zeros((d, n), w.dtype)
    return jnp.concatenate([w[:, :640], z(MLA_NOPE), w[:, 640:672], z(HEAD_PAD - MLA_QK), w[:, 672:]], axis=1)


def _unlayout_w_in(g):
    return jnp.concatenate([g[:, :640], g[:, 640 + MLA_NOPE:640 + MLA_QK], g[:, 768:]], axis=1)


def _pad_heads(v):
    lead = v.shape[:-1]
    return jnp.pad(v.reshape(lead + (HEADS, MLA_QK)), [(0, 0)] * len(lead) + [(0, 0), (0, HEAD_PAD - MLA_QK)]).reshape(
        lead + (HEADS * HEAD_PAD,))


def _halves(w):
    n = w.shape[1] // 2
    return [w[:, :n], w[:, n:]]


def _step(x, p, pos, tgt, gains, weights, dist):
    d = D_MODEL
    full = dict(weights) if dist is None else {}
    reduced = {}

    def gather_rider(g):
        if dist is None:
            return None, None
        mine = _join_parts(weights, GATHER_GROUPS[g])
        return mine, _gather_rider(mine)

    def gathered(g, mine, others):
        if dist is not None:
            parts = [lax.dynamic_update_slice_in_dim(o, m[None], dist[0], axis=0) for o, m in zip(others, mine)]
            for n, shards in _split_parts(parts, GATHER_GROUPS[g]).items():
                full[n] = _from_shards(n, shards)

    def reduce_before(g):
        if dist is None:
            return None, None
        group = REDUCE_GROUPS[g]
        shards = {n: grads[n] if grads[n].ndim == 3 else _to_shards(n, grads[n].astype(BF16)) for part in group for n in part}
        partial = _join_parts(shards, group)
        from_sibling = _pair_send_call(partial, "grads%d_pair_send" % g)
        pair_sum = [_pair_sum_call(a, b, dist[1], BF16, "grads%d_pair_sum_%d" % (g, k))
                    for k, (a, b) in enumerate(zip(partial, from_sibling))]
        return pair_sum, _scatter_rider(pair_sum)

    def reduce_after(g, pair_sum, by_chip):
        if dist is not None:
            chip, core = dist
            by_chip = [lax.dynamic_update_slice_in_dim(t, lax.dynamic_slice_in_dim(o, chip, 1, axis=0), chip, axis=0)
                       for t, o in zip(by_chip, pair_sum)]
            bufs = _pair_swap_call([_chip_sum_call(t, core, "grads%d_chip_sum_%d" % (g, k)) for k, t in enumerate(by_chip)],
                                   "grads%d_pair_swap" % g)
            for n, (k, row0) in _part_rows(REDUCE_GROUPS[g]).items():
                reduced[n] = (bufs[k], row0)

    mine, rider = gather_rider(0)
    if dist is not None:
        gathered(0, mine, _exchange_call(rider, "gather0"))
    wts = full
    inv_freq = ROPE_BASE ** (-jnp.arange(0, MLA_ROPE, 2, dtype=F32) / MLA_ROPE)
    zeros = lambda n: jnp.zeros((n,), F32)
    freq = jnp.concatenate([zeros(MLA_NOPE), inv_freq, inv_freq, zeros(HEAD_PAD - MLA_QK)])[None]
    sign = jnp.concatenate([zeros(MLA_NOPE), -jnp.ones((16,), F32), jnp.ones((16,), F32), zeros(HEAD_PAD - MLA_QK)])[None]
    pad_gain = lambda g: jnp.pad(g, ((0, 0), (0, HEAD_PAD - MLA_QK)))
    g_qh, g_kh = pad_gain(gains["q_head_norm"]), pad_gain(gains["k_head_norm"])

    u1 = _norm_call(x, gains["ffn1_norm"], "norm_ffn1")
    mine, rider = gather_rider(1)
    (a1, b1, hm1), got = _ffn_in_call(u1, wts["ffn1_w_in"], "ffn1_in", rider)
    gathered(1, mine, got)
    mine, rider = gather_rider(2)
    h1, got = _ffn_out_call(hm1, wts["ffn1_w_out"], x, "ffn1_out", rider)
    gathered(2, mine, got)
    w_in = _relayout_w_in(wts["w_in"])
    wq = _pad_heads(wts["w_q_up"])
    wkv = wts["w_kv_up"]
    wbm = jnp.pad(wts["w_branch_mla"].reshape(HEADS, 64, d), ((0, 0), (64, 0), (0, 0))).reshape(HEADS * HEAD_PAD, d)
    wbs, wo = wts["w_branch_sb"], wts["w_out"]
    um = _norm_call(h1, gains["mix_norm"], "norm_mix")
    cq, ckv, krope, sbq, sbk, sbv, gates = _mix_in_call(um, w_in, "mix_in")
    prep_args = (cq, ckv, krope, pos, freq, sign, gains["q_latent_norm"], gains["kv_latent_norm"], g_qh, g_kh, wq, wkv)
    q, k, v = _mla_prep_call(*prep_args, "mla_prep")
    mine, rider = gather_rider(3)
    (om, lse), got = _mla_fwd_call(q, k, v, "mla_fwd", rider)
    gathered(3, mine, got)
    osb = _sb_fwd_call(sbq, sbk, sbv, "sb_fwd")
    h2, bm, bs, mg = _merge_out_call(om, osb, gates, h1, wbm, wbs, wo, "merge_out")
    u2 = _norm_call(h2, gains["ffn2_norm"], "norm_ffn2")
    (a2, b2, hm2), _ = _ffn_in_call(u2, wts["ffn2_w_in"], "ffn2_in")
    h3, _ = _ffn_out_call(hm2, wts["ffn2_w_out"], h2, "ffn2_out")

    grads, gg = {}, {}
    dh3, dh3s, un, dgl, dpp, gg["ple_norm"], sq = _ple_call(
        h3, gains["ple_norm"], wts["w_ple_gate"], p, wts["w_ple_proj"], tgt, "ple")
    grads["w_ple_gate"] = _tn_call(un, dgl, "dw_ple_gate")
    grads["w_ple_proj"] = _tn_call(p, dpp, "dw_ple_proj")

    (da2, db2), _ = _ffn_bwd_a_call(dh3s, a2, b2, wts["ffn2_w_out"], "ffn2_bwd_act")
    grads["ffn2_w_out"] = _tn_call(hm2, dh3s, "dw_ffn2_out")
    grads["ffn2_w_in"] = jnp.concatenate([_tn_call(u2, da2, "dw_ffn2_in_a", shard_cols=D_FF // 2),
                                          _tn_call(u2, db2, "dw_ffn2_in_b", shard_cols=D_FF // 2)], axis=0)
    dh2, dh2b, gg["ffn2_norm"] = _norm_bwd_call([da2, db2], _halves(wts["ffn2_w_in"]), h2, gains["ffn2_norm"], dh3,
                                                "ffn2_bwd_norm", half_out=False)

    dgates, dbm, dbs, dom, dos = _merge_bwd_call(dh2b, gates, bm, bs, wo, wbm, wbs, "merge_bwd")
    grads["w_out"] = _tn_call(mg, dh2b, "dw_out")
    grads["w_branch_mla"] = _tn_call(om, dbm, "dw_branch_mla").reshape(HEADS, HEAD_PAD, d)[:, 64:, :].reshape(512, d)
    grads["w_branch_sb"] = _tn_call(osb, dbs, "dw_branch_sb")
    pair_sum, rider = reduce_before(0)
    (dq, dk, dv), got = _mla_bwd_call(q, k, v, om, dom, lse, "mla_bwd", rider)
    reduce_after(0, pair_sum, got)
    dsq, dsk, dsv = _sb_bwd_call(sbq, sbk, sbv, dos, osb, "sb_bwd")
    (dcq, dckv, dkr, dwq, grads["w_kv_up"], gg["q_latent_norm"], gg["kv_latent_norm"], dgqh, dgkh) = \
        _mla_prep_bwd_call(*prep_args, dq, dk, dv, "mla_prep_bwd")
    grads["w_q_up"] = dwq.reshape(Q_LORA, HEADS, HEAD_PAD)[:, :, :MLA_QK].reshape(Q_LORA, HEADS * MLA_QK)
    gg["q_head_norm"], gg["k_head_norm"] = dgqh[:, :MLA_QK], dgkh[:, :MLA_QK]
    dproj = jnp.concatenate([dcq, dckv, dkr, dsq, dsk.astype(BF16), dsv.astype(BF16), dgates], axis=1)
    grads["w_in"] = _unlayout_w_in(_tn_call(um, dproj, "dw_in"))
    dh1, dh1s, gg["mix_norm"] = _norm_bwd_call([dproj], [w_in], h1, gains["mix_norm"], dh2, "mix_bwd_norm", half_out=True)

    pair_sum, rider = reduce_before(1)
    (da1, db1), got = _ffn_bwd_a_call(dh1s, a1, b1, wts["ffn1_w_out"], "ffn1_bwd_act", rider)
    reduce_after(1, pair_sum, got)
    grads["ffn1_w_out"] = _tn_call(hm1, dh1s, "dw_ffn1_out")
    pair_sum, rider = reduce_before(2)
    res = _tn_call(u1, da1, "dw_ffn1_in_a", shard_cols=D_FF // 2, rider=rider)
    dwa, got = (res, None) if rider is None else res
    reduce_after(2, pair_sum, got)
    grads["ffn1_w_in"] = jnp.concatenate([dwa, _tn_call(u1, db1, "dw_ffn1_in_b", shard_cols=D_FF // 2)], axis=0)
    pair_sum, rider = reduce_before(3)
    res = _norm_bwd_call([da1, db1], _halves(wts["ffn1_w_in"]), x, gains["ffn1_norm"], dh1, "ffn1_bwd_norm",
                         half_out=False, rider=rider)
    (dx, _, gg["ffn1_norm"]), got = (res, None) if rider is None else res
    reduce_after(3, pair_sum, got)
    return sq, dx, gg, (grads if dist is None else reduced)


def kernel(x, p, positions, ffn1_norm, ffn1_w_in, ffn1_w_out, mix_norm, w_in, q_latent_norm, w_q_up, kv_latent_norm, w_kv_up, q_head_norm, k_head_norm, w_branch_mla, w_branch_sb, w_out, ffn2_norm, ffn2_w_in, ffn2_w_out, ple_norm, w_ple_gate, w_ple_proj, loss_target, m_ffn1_norm, m_ffn1_w_in, m_ffn1_w_out, m_mix_norm, m_w_in, m_q_latent_norm, m_w_q_up, m_kv_latent_norm, m_w_kv_up, m_q_head_norm, m_k_head_norm, m_w_branch_mla, m_w_branch_sb, m_w_out, m_ffn2_norm, m_ffn2_w_in, m_ffn2_w_out, m_ple_norm, m_w_ple_gate, m_w_ple_proj, v_ffn1_norm, v_ffn1_w_in, v_ffn1_w_out, v_mix_norm, v_w_in, v_q_latent_norm, v_w_q_up, v_kv_latent_norm, v_w_kv_up, v_q_head_norm, v_k_head_norm, v_w_branch_mla, v_w_branch_sb, v_w_out, v_ffn2_norm, v_ffn2_w_in, v_ffn2_w_out, v_ple_norm, v_w_ple_gate, v_w_ple_proj):
    given = dict(locals())
    w_shard = {n: given[n][0] for n in WEIGHT_ORDER}
    m_shard = {n: given["m_" + n][0] for n in WEIGHT_ORDER}
    v_shard = {n: given["v_" + n][0] for n in WEIGHT_ORDER}
    gains = {n: w_shard[n][None] for n in GAINS}

    chip = 2 * lax.axis_index("x") + lax.axis_index("y")
    sq, dx, gain_grads, reduced = _step(x[0], p[0, 0], positions.reshape(-1, 1), loss_target[0], gains,
                                        {n: w_shard[n].astype(BF16) for n in BIG}, (chip, lax.axis_index("c")))

    rows = [jnp.pad(gain_grads[n], ((0, 0), (0, D_MODEL - GAINS[n]))) for n in GAINS] + [sq]
    gain_block = jnp.concatenate(rows + [jnp.zeros((16 - len(rows), D_MODEL), F32)], axis=0)
    gain_sum = _sum_call(_all_gather_small_call(gain_block, "gains_all_gather"), F32, "gains_sum")
    loss = 0.5 * jnp.sum(gain_sum[len(GAINS)]) / D_MODEL

    outs = {"grad": {}, "delta": {}, "new_m": {}, "new_v": {}}
    gain_pack = lambda t: jnp.concatenate([jnp.pad(t[n][None], ((0, 0), (0, D_MODEL - GAINS[n]))) for n in GAINS], axis=0)
    packed = _adamw_call(gain_pack(w_shard), gain_sum, 0, gain_pack(m_shard), gain_pack(v_shard), "adamw_gains")
    for i, n in enumerate(GAINS):
        for kind, t in zip(("grad", "delta", "new_m", "new_v"), packed):
            outs[kind][n] = t[i, :GAINS[n]][None]
    for n in BIG:
        buf, row0 = reduced[n]
        for kind, t in zip(("grad", "delta", "new_m", "new_v"),
                           _adamw_call(w_shard[n], buf, row0, m_shard[n], v_shard[n], "adamw_" + n)):
            outs[kind][n] = t[None]

    return (loss, dx[None], *[outs["grad"][n] for n in WEIGHT_ORDER], *[outs["delta"][n] for n in WEIGHT_ORDER],
            *[outs["new_m"][n] for n in WEIGHT_ORDER], *[outs["new_v"][n] for n in WEIGHT_ORDER])
```

```python
import collections
import functools
import math

import jax
import jax.numpy as jnp
from jax import lax
from jax.experimental import pallas as pl
from jax.experimental.pallas import tpu as pltpu

F32 = jnp.float32
BF16 = jnp.bfloat16
MESH = pl.DeviceIdType.MESH

D_MODEL = 1024
D_FF = 2816
PLE_DIM = 256
NORM_EPS = 1e-6
HEADS = 8
MLA_NOPE = 64
MLA_ROPE = 32
MLA_QK = 96
Q_LORA = 384
KV_LORA = 256
SB_WIDTH = 512
ROPE_BASE = 10000.0
HEAD_PAD = 128
N_CHIPS = 4

ADAM_LR = 0.001
ADAM_B1 = 0.9
ADAM_B2 = 0.999
ADAM_EPS = 1e-08
ADAM_WD = 0.01
ADAM_STEP = 10

SEG_CQ = (0, 384)
SEG_CKV = (384, 256)
SEG_KROPE = (640, 128)
SEG_SBQ = (768, 512)
SEG_SBK = (1280, 512)
SEG_SBV = (1792, 512)
SEG_GATES = (2304, 2048)
IN_COLS_PAD = 4352

TM = 512
TM_SMALL = 256
TQ = 256
MLA_FWD_BLOCKS = 4
MLA_BWD_BLOCKS = 4
SB_FWD_BLOCKS = 4
SB_BWD_BLOCKS = 2
SB_HEAD = 64
SB_SCALE = 0.125
SB_DEAD = -104.0
COL_CHUNK = 256
TN_MAX_COLS = 2816
TN_OPERAND_BYTES = 34 * 1024 * 1024
MAX_ROW_TILE = 512
VMEM_LIMIT = 56 * 1024 * 1024

NT = (((1,), (1,)), ((), ()))
TN = (((0,), (0,)), ((), ()))


def _cp(sem):
    return pltpu.CompilerParams(dimension_semantics=sem, vmem_limit_bytes=VMEM_LIMIT)


def _rows(tm, w):
    return pl.BlockSpec((tm, w), lambda i: (i, 0))


def _whole(shape):
    return pl.BlockSpec(shape, lambda i: (0,) * len(shape))


def _dot(a, b):
    return jnp.dot(a, b, preferred_element_type=F32)


def _dot_nt(a, b):
    return lax.dot_general(a, b, NT, preferred_element_type=F32)


def _dot_tn(a, b):
    return lax.dot_general(a, b, TN, preferred_element_type=F32)


def _rstd(x, n):
    return lax.rsqrt(jnp.sum(x * x, axis=-1, keepdims=True) / n + NORM_EPS)


def _rms_bwd(x, r, g, dy, n):
    gy = dy * g
    return r * gy - x * ((r * r * r) * (jnp.sum(x * gy, axis=-1, keepdims=True) / n))


def _sigmoid(x):
    return jax.nn.sigmoid(x)


def _pick(n, cands):
    for c in cands:
        if n % c == 0:
            return c
    return n


def _row_tile(r):
    for t in range(min(r, MAX_ROW_TILE) // 16 * 16, 15, -16):
        if r % t == 0:
            return t
    return r


HBM = pl.BlockSpec(memory_space=pl.ANY)

_Rider = collections.namedtuple("_Rider", "ins out_shape sems start finish")


def _with_rider(body, rider, *, name, grid, in_specs, out_specs, out_shape, args, sem, scratch=()):
    if rider is None:
        return pl.pallas_call(body, name=name, grid=grid, in_specs=in_specs, out_specs=out_specs, out_shape=out_shape,
                              scratch_shapes=list(scratch), compiler_params=_cp(sem))(*args), None
    ni, no, nri, nro = len(in_specs), len(out_specs), len(rider.ins), len(rider.out_shape)

    def riding(*refs):
        ins, r_ins = refs[:ni], refs[ni:ni + nri]
        outs, r_outs = refs[ni + nri:ni + nri + no], refs[ni + nri + no:ni + nri + no + nro]
        scr = refs[ni + nri + no + nro:ni + nri + no + nro + len(scratch)]
        sems = refs[ni + nri + no + nro + len(scratch):]
        ids = [pl.program_id(a) for a in range(len(grid))]
        first = functools.reduce(jnp.logical_and, [i == 0 for i in ids])
        last = functools.reduce(jnp.logical_and, [i == g - 1 for i, g in zip(ids, grid)])

        @pl.when(first)
        def _():
            rider.start(r_ins, r_outs, sems)

        body(*ins, *outs, *scr)

        @pl.when(last)
        def _():
            rider.finish(r_ins, r_outs, sems)

    res = pl.pallas_call(
        riding, name=name, grid=grid, in_specs=list(in_specs) + [HBM] * nri, out_specs=list(out_specs) + [HBM] * nro,
        out_shape=list(out_shape) + list(rider.out_shape),
        scratch_shapes=list(scratch) + [pltpu.SemaphoreType.DMA((k,)) for k in rider.sems],
        compiler_params=_cp(("arbitrary",) * len(grid)))(*args, *rider.ins)
    return res[:no], res[no:]


def _norm_call(h, g, name):
    s, d = h.shape
    tm = min(TM, s)

    def body(h_ref, g_ref, u_ref):
        x = h_ref[...]
        u_ref[...] = ((x * _rstd(x, d)) * g_ref[...]).astype(BF16)

    return pl.pallas_call(
        body, name=name, grid=(s // tm,),
        in_specs=[_rows(tm, d), _whole((1, d))], out_specs=_rows(tm, d),
        out_shape=jax.ShapeDtypeStruct((s, d), BF16), compiler_params=_cp(("parallel",)))(h, g)


def _ffn_in_call(u, w, name, rider=None):
    s, d = u.shape
    tn = w.shape[2]
    nj = w.shape[0] // 2
    n = nj * tn
    tm = min(TM, s)

    def body(u_ref, wa_ref, wb_ref, a_ref, b_ref, hm_ref):
        uu = u_ref[...]
        a = _dot(uu, wa_ref[...])
        b = _dot(uu, wb_ref[...])
        a_ref[...] = a
        b_ref[...] = b
        hm_ref[...] = ((a * _sigmoid(a)) * b).astype(BF16)

    blk = pl.BlockSpec((tm, tn), lambda j, i: (i, j))
    return _with_rider(
        body, rider, name=name, grid=(nj, s // tm),
        in_specs=[pl.BlockSpec((tm, d), lambda j, i: (i, 0)),
                  pl.BlockSpec((None, d, tn), lambda j, i: (j, 0, 0)),
                  pl.BlockSpec((None, d, tn), lambda j, i: (j + nj, 0, 0))],
        out_specs=[blk, blk, blk],
        out_shape=[jax.ShapeDtypeStruct((s, n), F32), jax.ShapeDtypeStruct((s, n), F32),
                   jax.ShapeDtypeStruct((s, n), BF16)],
        args=(u, w, w), sem=("parallel", "parallel"))


def _ffn_out_call(hm, w, h, gain, name, rider=None):
    s, n = hm.shape
    d = w.shape[1]
    tm = min(TM, s)

    def body(hm_ref, w_ref, h_ref, g_ref, o_ref, u_ref):
        x = h_ref[...] + 0.5 * _dot(hm_ref[...], w_ref[...])
        o_ref[...] = x
        u_ref[...] = ((x * _rstd(x, d)) * g_ref[...]).astype(BF16)

    return _with_rider(
        body, rider, name=name, grid=(s // tm,),
        in_specs=[_rows(tm, n), _whole((n, d)), _rows(tm, d), _whole((1, d))], out_specs=[_rows(tm, d), _rows(tm, d)],
        out_shape=[jax.ShapeDtypeStruct((s, d), F32), jax.ShapeDtypeStruct((s, d), BF16)], args=(hm, w, h, gain),
        sem=("parallel",))


def _mix_in_call(u, w, name):
    s, d = u.shape
    tm = min(TM_SMALL, s)
    segs = [(SEG_CQ, F32), (SEG_CKV, F32), (SEG_KROPE, F32), (SEG_SBQ, BF16), (SEG_SBK, BF16),
            (SEG_SBV, BF16), (SEG_GATES, F32)]

    def body(u_ref, w_ref, *outs):
        uu = u_ref[...]
        for ((off, width), _), o_ref in zip(segs, outs):
            o_ref[...] = _dot(uu, w_ref[:, off:off + width]).astype(o_ref.dtype)

    return pl.pallas_call(
        body, name=name, grid=(s // tm,),
        in_specs=[_rows(tm, d), _whole((d, IN_COLS_PAD))],
        out_specs=[_rows(tm, width) for (_, width), _ in segs],
        out_shape=[jax.ShapeDtypeStruct((s, width), dt) for (_, width), dt in segs],
        compiler_params=_cp(("parallel",)))(u, w)


def _lane(shape):
    return lax.broadcasted_iota(jnp.int32, shape, len(shape) - 1)


def _rot_half(y):
    lane = _lane(y.shape)
    swapped = jnp.where(lane < MLA_NOPE + MLA_ROPE // 2, pltpu.roll(y, HEAD_PAD - 16, 1), pltpu.roll(y, 16, 1))
    return jnp.where((lane >= MLA_NOPE) & (lane < MLA_QK), swapped, 0.0)


def _rope_tables(pos_ref, freq_ref, sign_ref):
    ang = pos_ref[...].astype(F32) * freq_ref[...]
    return jnp.cos(ang), jnp.sin(ang) * sign_ref[...]


def _head_fwd(x, g, cosv, ssv):
    r = _rstd(x, MLA_QK)
    y = (x * r) * g
    return y * cosv + _rot_half(y) * ssv, r


def _head_bwd(x, r, g, cosv, ssv, dout):
    dy = dout * cosv + _rot_half(dout * ssv)
    return _rms_bwd(x, r, g, dy, MLA_QK), jnp.sum(dy * (x * r), axis=0, keepdims=True)


def _mla_prep_call(cq, ckv, krope, pos, freq, sign, g_ql, g_kvl, g_qh, g_kh, wq, wkv, name):
    s = cq.shape[0]
    tm = min(TM_SMALL, s)
    width = HEADS * HEAD_PAD

    def body(cq_ref, ckv_ref, kr_ref, pos_ref, freq_ref, sign_ref, gql_ref, gkvl_ref, gqh_ref, gkh_ref,
             wq_ref, wkv_ref, q_ref, k_ref, v_ref):
        cosv, ssv = _rope_tables(pos_ref, freq_ref, sign_ref)
        x = cq_ref[...]
        qr = _dot(((x * _rstd(x, Q_LORA)) * gql_ref[...]).astype(BF16), wq_ref[...])
        x = ckv_ref[...]
        kv = _dot(((x * _rstd(x, KV_LORA)) * gkvl_ref[...]).astype(BF16), wkv_ref[...])
        kr = kr_ref[...]
        lane = _lane((tm, HEAD_PAD))
        for h in range(HEADS):
            sl = slice(h * HEAD_PAD, (h + 1) * HEAD_PAD)
            qh, _ = _head_fwd(qr[:, sl], gqh_ref[...], cosv, ssv)
            q_ref[:, sl] = qh.astype(BF16)
            kvh = kv[:, sl]
            kh, _ = _head_fwd(jnp.where(lane < MLA_NOPE, kvh, kr), gkh_ref[...], cosv, ssv)
            k_ref[:, sl] = kh.astype(BF16)
            v_ref[:, sl] = jnp.where(lane >= MLA_NOPE, kvh, 0.0).astype(BF16)

    out = jax.ShapeDtypeStruct((s, width), BF16)
    return pl.pallas_call(
        body, name=name, grid=(s // tm,),
        in_specs=[_rows(tm, Q_LORA), _rows(tm, KV_LORA), _rows(tm, HEAD_PAD), _rows(tm, 1),
                  _whole((1, HEAD_PAD)), _whole((1, HEAD_PAD)), _whole((1, Q_LORA)), _whole((1, KV_LORA)),
                  _whole((1, HEAD_PAD)), _whole((1, HEAD_PAD)), _whole((Q_LORA, width)), _whole((KV_LORA, width))],
        out_specs=[_rows(tm, width)] * 3, out_shape=[out, out, out],
        compiler_params=_cp(("parallel",)))(cq, ckv, krope, pos, freq, sign, g_ql, g_kvl, g_qh, g_kh, wq, wkv)


def _attn_specs(s, nb):
    qspec = pl.BlockSpec((TQ, nb * HEAD_PAD), lambda g, i: (i, g))
    kspec = pl.BlockSpec((s, nb * HEAD_PAD), lambda g, i: (0, g))
    return qspec, kspec


def _lanes(b):
    return slice(b * HEAD_PAD, (b + 1) * HEAD_PAD)


def _tri(cmp):
    r = lax.broadcasted_iota(jnp.int32, (TQ, TQ), 0)
    c = lax.broadcasted_iota(jnp.int32, (TQ, TQ), 1)
    return cmp(r, c)


def _mla_fwd_call(q, k, v, name, rider=None):
    s, width = q.shape
    scale = 1.0 / math.sqrt(MLA_QK)

    nb = MLA_FWD_BLOCKS

    def body(q_ref, k_ref, v_ref, o_ref, lse_ref):
        qi = pl.program_id(1)
        qs = [q_ref[:, _lanes(b)] for b in range(nb)]
        causal = _tri(lambda r, c: c <= r)

        def step(kb, carry, diag):
            ks = pl.multiple_of(kb * TQ, TQ)
            heads = range(nb)
            scs = [_dot_nt(qs[b], k_ref[pl.ds(ks, TQ), _lanes(b)]) * scale for b in heads]
            if diag:
                scs = [jnp.where(causal, sc, -1e30) for sc in scs]
            mns = [jnp.maximum(carry[b][0], jnp.max(scs[b], axis=-1, keepdims=True)) for b in heads]
            als = [jnp.exp(carry[b][0] - mns[b]) for b in heads]
            ps = [jnp.exp(scs[b] - mns[b]) for b in heads]
            ls = [als[b] * carry[b][1] + jnp.sum(ps[b], axis=-1, keepdims=True) for b in heads]
            accs = [als[b] * carry[b][2] + _dot(ps[b].astype(BF16), v_ref[pl.ds(ks, TQ), _lanes(b)]) for b in heads]
            return tuple((mns[b], ls[b], accs[b]) for b in heads)

        init = tuple((jnp.full((TQ, 1), -1e30, F32), jnp.zeros((TQ, 1), F32), jnp.zeros((TQ, HEAD_PAD), F32))
                     for _ in range(nb))
        carry = step(qi, init, True)
        carry = lax.fori_loop(0, qi, lambda kb, c: step(kb, c, False), carry)
        for b in range(nb):
            m, l, acc = carry[b]
            o_ref[:, _lanes(b)] = (acc / l).astype(BF16)
            lse_ref[:, _lanes(b)] = jnp.broadcast_to(m + jnp.log(l), (TQ, HEAD_PAD))

    qspec, kspec = _attn_specs(s, nb)
    return _with_rider(
        body, rider, name=name, grid=(width // (nb * HEAD_PAD), s // TQ),
        in_specs=[qspec, kspec, kspec], out_specs=[qspec, qspec],
        out_shape=[jax.ShapeDtypeStruct((s, width), BF16), jax.ShapeDtypeStruct((s, width), F32)],
        args=(q, k, v), sem=("parallel", "arbitrary"))


def _mla_bwd_call(q, k, v, o, do, lse, name, rider=None):
    s, width = q.shape
    scale = 1.0 / math.sqrt(MLA_QK)
    nb = MLA_BWD_BLOCKS

    def body(q_ref, k_ref, v_ref, o_ref, do_ref, lse_ref, dq_ref, dk_ref, dv_ref):
        qi = pl.program_id(1)

        @pl.when(qi == 0)
        def _():
            dk_ref[...] = jnp.zeros_like(dk_ref)
            dv_ref[...] = jnp.zeros_like(dv_ref)

        qs = [q_ref[:, _lanes(b)] for b in range(nb)]
        dos = [do_ref[:, _lanes(b)] for b in range(nb)]
        lses = [lse_ref[:, b * HEAD_PAD:b * HEAD_PAD + 1] for b in range(nb)]
        dlts = [jnp.sum(dos[b].astype(F32) * o_ref[:, _lanes(b)].astype(F32), axis=-1, keepdims=True) for b in range(nb)]
        causal = _tri(lambda r, c: c <= r)

        def step(kb, dqs, diag):
            ks = pl.multiple_of(kb * TQ, TQ)
            heads = range(nb)
            kts = [k_ref[pl.ds(ks, TQ), _lanes(b)] for b in heads]
            scs = [_dot_nt(qs[b], kts[b]) for b in heads]
            dps = [_dot_nt(dos[b], v_ref[pl.ds(ks, TQ), _lanes(b)]) for b in heads]
            ps = [jnp.exp(scs[b] * scale - lses[b]) for b in heads]
            if diag:
                ps = [jnp.where(causal, p, 0.0) for p in ps]
            dss = [(ps[b] * (dps[b] - dlts[b]) * scale).astype(BF16) for b in heads]
            dvs = [_dot_tn(ps[b].astype(BF16), dos[b]) for b in heads]
            dks = [_dot_tn(dss[b], qs[b]) for b in heads]
            out = tuple(dqs[b] + _dot(dss[b], kts[b]) for b in heads)
            for b in heads:
                dv_ref[pl.ds(ks, TQ), _lanes(b)] += dvs[b]
                dk_ref[pl.ds(ks, TQ), _lanes(b)] += dks[b]
            return out

        dqs = step(qi, tuple(jnp.zeros((TQ, HEAD_PAD), F32) for _ in range(nb)), True)
        dqs = lax.fori_loop(0, qi, lambda kb, c: step(kb, c, False), dqs)
        for b in range(nb):
            dq_ref[:, _lanes(b)] = dqs[b]

    qspec, kspec = _attn_specs(s, nb)
    out = jax.ShapeDtypeStruct((s, width), F32)
    return _with_rider(
        body, rider, name=name, grid=(width // (nb * HEAD_PAD), s // TQ),
        in_specs=[qspec, kspec, kspec, qspec, qspec, qspec], out_specs=[qspec, kspec, kspec],
        out_shape=[out, out, out], args=(q, k, v, o, do, lse), sem=("parallel", "arbitrary"))


def _dot_hilo(x, u):
    hi = x.astype(BF16)
    lo = (x - hi.astype(F32)).astype(BF16)
    return _dot(hi, u) + _dot(lo, u)


def _sb_logs(z):
    ls = jnp.minimum(z, 0.0) - jnp.log(1.0 + jnp.exp(-jnp.abs(z)))
    return ls, ls - z


def _sb_head_q(qb, first, hh):
    keep = first if hh == 0 else jnp.logical_not(first)
    return jnp.where(keep, qb, jnp.zeros_like(qb)) * jnp.asarray(SB_SCALE, qb.dtype)


def _sb_fwd_call(q, k, v, name):
    s, width = q.shape
    nb = SB_FWD_BLOCKS
    chains = [(b, hh) for b in range(nb) for hh in range(HEAD_PAD // SB_HEAD)]

    def body(q_ref, k_ref, v_ref, o_ref):
        qi = pl.program_id(1)
        strict = _tri(lambda r, c: c < r)
        after = _tri(lambda r, c: r > c).astype(BF16)
        first = _lane((1, HEAD_PAD)) < SB_HEAD
        qhs = [_sb_head_q(q_ref[:, _lanes(b)], first, hh) for b, hh in chains]

        def step(kb, carry, diag):
            ks = pl.multiple_of(kb * TQ, TQ)
            ids = range(len(chains))
            zs = [_dot_nt(qhs[ci], k_ref[pl.ds(ks, TQ), _lanes(chains[ci][0])]) for ci in ids]
            logs = [_sb_logs(z) for z in zs]
            lss = [lg[0] for lg in logs]
            l1ms = [jnp.where(strict, lg[1], 0.0) if diag else lg[1] for lg in logs]
            sufs = [_dot_hilo(l1m, after) for l1m in l1ms]
            as_ = [jnp.exp(lss[ci] + sufs[ci] + carry[ci][0]) for ci in ids]
            if diag:
                as_ = [jnp.where(strict, a, 0.0) for a in as_]
            accs = [carry[ci][1] + _dot(as_[ci].astype(BF16), v_ref[pl.ds(ks, TQ), _lanes(chains[ci][0])]) for ci in ids]
            return tuple((carry[ci][0] + jnp.sum(l1ms[ci], axis=-1, keepdims=True), accs[ci]) for ci in ids)

        init = tuple((jnp.zeros((TQ, 1), F32), jnp.zeros((TQ, HEAD_PAD), F32)) for _ in chains)
        carry = _sb_sweep(step, qi, init)
        for b in range(nb):
            o_ref[:, _lanes(b)] = jnp.where(first, carry[2 * b][1], carry[2 * b + 1][1])

    qspec, kspec = _attn_specs(s, nb)
    return pl.pallas_call(
        body, name=name, grid=(width // (nb * HEAD_PAD), s // TQ),
        in_specs=[qspec, kspec, kspec], out_specs=qspec, out_shape=jax.ShapeDtypeStruct((s, width), F32),
        compiler_params=_cp(("parallel", "arbitrary")))(q, k, v)


def _sb_sweep(step, qi, init):
    def live(carry):
        top = carry[0][0]
        for c in carry[1:]:
            top = jnp.maximum(top, c[0])
        return jnp.max(top)

    carry = step(qi, init, True)

    def cond(state):
        j, alive, _ = state
        return jnp.logical_and(j < qi, alive > SB_DEAD)

    def body(state):
        j, _, carry = state
        carry = step(qi - 1 - j, carry, False)
        return j + 1, live(carry), carry

    return lax.while_loop(cond, body, (jnp.int32(0), live(carry), carry))[2]


def _sb_bwd_call(q, k, v, do, o, name):
    s, width = q.shape
    nb = SB_BWD_BLOCKS
    chains = [(b, hh) for b in range(nb) for hh in range(HEAD_PAD // SB_HEAD)]

    def body(q_ref, k_ref, v_ref, do_ref, o_ref, dq_ref, dk_ref, dv_ref):
        qi = pl.program_id(1)

        @pl.when(qi == 0)
        def _():
            dk_ref[...] = jnp.zeros_like(dk_ref)
            dv_ref[...] = jnp.zeros_like(dv_ref)

        strict = _tri(lambda r, c: c < r)
        after = _tri(lambda r, c: r > c).astype(BF16)
        from_here = _tri(lambda r, c: r >= c).astype(BF16)
        first = _lane((1, HEAD_PAD)) < SB_HEAD
        qhs = [_sb_head_q(q_ref[:, _lanes(b)], first, hh) for b, hh in chains]
        dohs = []
        for b, hh in chains:
            dob = do_ref[:, _lanes(b)]
            dohs.append(jnp.where(first if hh == 0 else jnp.logical_not(first), dob, jnp.zeros_like(dob)))
        gtots = [jnp.sum(dohs[ci].astype(F32) * o_ref[:, _lanes(chains[ci][0])], axis=-1, keepdims=True)
                 for ci in range(len(chains))]

        def step(kb, carry, diag):
            ks = pl.multiple_of(kb * TQ, TQ)
            ids = range(len(chains))
            kts = [k_ref[pl.ds(ks, TQ), _lanes(b)] for b, _ in chains]
            zs = [_dot_nt(qhs[ci], kts[ci]) for ci in ids]
            das = [_dot_nt(dohs[ci], v_ref[pl.ds(ks, TQ), _lanes(chains[ci][0])]) for ci in ids]
            logs = [_sb_logs(z) for z in zs]
            lss = [lg[0] for lg in logs]
            l1ms = [jnp.where(strict, lg[1], 0.0) if diag else lg[1] for lg in logs]
            sufs = [_dot_hilo(l1m, after) for l1m in l1ms]
            as_ = [jnp.exp(lss[ci] + sufs[ci] + carry[ci][0]) for ci in ids]
            if diag:
                as_ = [jnp.where(strict, a, 0.0) for a in as_]
            abs_ = [a.astype(BF16) for a in as_]
            gs = [abs_[ci].astype(F32) * das[ci] for ci in ids]
            cexs = [gtots[ci] - (carry[ci][1] + _dot_hilo(gs[ci], from_here)) for ci in ids]
            dzs = [gs[ci] - jnp.exp(lss[ci]) * (gs[ci] + cexs[ci]) for ci in ids]
            if diag:
                dzs = [jnp.where(strict, dz, 0.0) for dz in dzs]
            dzbs = [dz.astype(BF16) for dz in dzs]
            dvps = [_dot_tn(abs_[ci], dohs[ci]) for ci in ids]
            dkps = [_dot_tn(dzbs[ci], qhs[ci]) for ci in ids]
            out = tuple((carry[ci][0] + jnp.sum(l1ms[ci], axis=-1, keepdims=True),
                         carry[ci][1] + jnp.sum(gs[ci], axis=-1, keepdims=True),
                         carry[ci][2] + _dot(dzbs[ci], kts[ci])) for ci in ids)
            for b in range(nb):
                dk_ref[pl.ds(ks, TQ), _lanes(b)] += dkps[2 * b] + dkps[2 * b + 1]
                dv_ref[pl.ds(ks, TQ), _lanes(b)] += dvps[2 * b] + dvps[2 * b + 1]
            return out

        init = tuple((jnp.zeros((TQ, 1), F32), jnp.zeros((TQ, 1), F32), jnp.zeros((TQ, HEAD_PAD), F32)) for _ in chains)
        carry = _sb_sweep(step, qi, init)
        for b in range(nb):
            dq_ref[:, _lanes(b)] = (jnp.where(first, carry[2 * b][2], carry[2 * b + 1][2]) * SB_SCALE).astype(BF16)

    qspec, kspec = _attn_specs(s, nb)
    return pl.pallas_call(
        body, name=name, grid=(width // (nb * HEAD_PAD), s // TQ),
        in_specs=[qspec, kspec, kspec, qspec, qspec], out_specs=[qspec, kspec, kspec],
        out_shape=[jax.ShapeDtypeStruct((s, width), BF16), jax.ShapeDtypeStruct((s, width), F32),
                   jax.ShapeDtypeStruct((s, width), F32)],
        compiler_params=_cp(("parallel", "arbitrary")))(q, k, v, do, o)


def _merge_out_call(om, osb, gates, h, wbm, wbs, wo, gain, name):
    s, d = h.shape
    tm = min(TM_SMALL, s)

    def body(om_ref, os_ref, g_ref, h_ref, wbm_ref, wbs_ref, wo_ref, gain_ref, h2_ref, bm_ref, bs_ref, mg_ref, u_ref):
        bm = _dot(om_ref[...], wbm_ref[...])
        bs = _dot(os_ref[...].astype(BF16), wbs_ref[...])
        mg = (_sigmoid(g_ref[:, :d]) * bm + _sigmoid(g_ref[:, d:]) * bs).astype(BF16)
        bm_ref[...] = bm
        bs_ref[...] = bs
        mg_ref[...] = mg
        x = h_ref[...] + _dot(mg, wo_ref[...])
        h2_ref[...] = x
        u_ref[...] = ((x * _rstd(x, d)) * gain_ref[...]).astype(BF16)

    return pl.pallas_call(
        body, name=name, grid=(s // tm,),
        in_specs=[_rows(tm, om.shape[1]), _rows(tm, SB_WIDTH), _rows(tm, 2 * d), _rows(tm, d),
                  _whole(wbm.shape), _whole(wbs.shape), _whole(wo.shape), _whole((1, d))],
        out_specs=[_rows(tm, d)] * 5,
        out_shape=[jax.ShapeDtypeStruct((s, d), F32), jax.ShapeDtypeStruct((s, d), F32),
                   jax.ShapeDtypeStruct((s, d), F32), jax.ShapeDtypeStruct((s, d), BF16),
                   jax.ShapeDtypeStruct((s, d), BF16)],
        compiler_params=_cp(("parallel",)))(om, osb, gates, h, wbm, wbs, wo, gain)


def _ple_call(h, g, wg, p, wp, tgt, name):
    s, d = h.shape
    tm = min(TM_SMALL, s)

    def body(h_ref, g_ref, wg_ref, p_ref, wp_ref, t_ref, dh_ref, dhs_ref, un_ref, dgl_ref, dpp_ref, dg_ref, sq_ref):
        @pl.when(pl.program_id(0) == 0)
        def _():
            dg_ref[...] = jnp.zeros_like(dg_ref)
            sq_ref[...] = jnp.zeros_like(sq_ref)

        x = h_ref[...]
        gain = g_ref[...]
        r = _rstd(x, d)
        xh = x * r
        un = (xh * gain).astype(BF16)
        sg = _sigmoid(_dot(un, wg_ref[...]))
        pp = _dot(p_ref[...].astype(BF16), wp_ref[...])
        diff = (x + sg * pp) - t_ref[...]
        sq_ref[...] += jnp.sum(diff * diff, axis=0, keepdims=True)
        dy = diff * (1.0 / d)
        dgl = ((dy * pp) * (sg * (1.0 - sg))).astype(BF16)
        dun = _dot_nt(dgl, wg_ref[...])
        dg_ref[...] += jnp.sum(dun * xh, axis=0, keepdims=True)
        dh = dy + _rms_bwd(x, r, gain, dun, d)
        dh_ref[...] = dh
        dhs_ref[...] = (0.5 * dh).astype(BF16)
        un_ref[...] = un
        dgl_ref[...] = dgl
        dpp_ref[...] = (dy * sg).astype(BF16)

    bf = jax.ShapeDtypeStruct((s, d), BF16)
    vec = jax.ShapeDtypeStruct((1, d), F32)
    return pl.pallas_call(
        body, name=name, grid=(s // tm,),
        in_specs=[_rows(tm, d), _whole((1, d)), _whole(wg.shape), _rows(tm, PLE_DIM), _whole(wp.shape), _rows(tm, d)],
        out_specs=[_rows(tm, d)] * 5 + [_whole((1, d))] * 2,
        out_shape=[jax.ShapeDtypeStruct((s, d), F32), bf, bf, bf, bf, vec, vec],
        compiler_params=_cp(("arbitrary",)))(h, g, wg, p, wp, tgt)


def _ffn_bwd_a_call(dhs, a, b, wo, name, rider=None):
    s, n = a.shape
    d = dhs.shape[1]
    tn = n // 2
    tm = min(TM, s)

    def body(dh_ref, a_ref, b_ref, wo_ref, da_ref, db_ref):
        dh = dh_ref[...]
        for c0 in range(0, tn, COL_CHUNK):
            sl = slice(c0, min(c0 + COL_CHUNK, tn))
            dhm = _dot_nt(dh, wo_ref[sl, :])
            av = a_ref[:, sl]
            sa = _sigmoid(av)
            da_ref[:, sl] = (dhm * b_ref[:, sl] * (sa * (1.0 + av * (1.0 - sa)))).astype(BF16)
            db_ref[:, sl] = (dhm * (av * sa)).astype(BF16)

    blk = pl.BlockSpec((tm, tn), lambda j, i: (i, j))
    return _with_rider(
        body, rider, name=name, grid=(n // tn, s // tm),
        in_specs=[pl.BlockSpec((tm, d), lambda j, i: (i, 0)), blk, blk, pl.BlockSpec((tn, d), lambda j, i: (j, 0))],
        out_specs=[blk, blk],
        out_shape=[jax.ShapeDtypeStruct((s, n), BF16)] * 2, args=(dhs, a, b, wo), sem=("parallel", "parallel"))


def _norm_bwd_call(dy_list, w_list, h, g, dh_in, name, half_out, rider=None):
    s, d = h.shape
    tm = min(TM_SMALL, s)
    nk, nw = len(dy_list), len(w_list)
    factor = 0.5 if half_out else 1.0
    sharded = nw == 1 and w_list[0].ndim == 3

    def body(*refs):
        dy_refs = refs[:nk]
        w_refs = refs[nk:nk + nw]
        h_ref, g_ref, dhin_ref, dh_ref, dhb_ref, dg_ref = refs[nk + nw:]

        @pl.when(pl.program_id(0) == 0)
        def _():
            dg_ref[...] = jnp.zeros_like(dg_ref)

        if sharded:
            c = w_list[0].shape[2]
            per = dy_list[0].shape[1] // c
            du = None
            for k in range(w_list[0].shape[0]):
                part = _dot_nt(dy_refs[k // per][:, (k % per) * c:(k % per + 1) * c], w_refs[0][k])
                du = part if du is None else du + part
        else:
            du = _dot_nt(dy_refs[0][...], w_refs[0][...])
            for dy_ref, w_ref in zip(dy_refs[1:], w_refs[1:]):
                du = du + _dot_nt(dy_ref[...], w_ref[...])
        x = h_ref[...]
        r = _rstd(x, d)
        dg_ref[...] += jnp.sum(du * (x * r), axis=0, keepdims=True)
        dh = dhin_ref[...] + _rms_bwd(x, r, g_ref[...], du, d)
        dh_ref[...] = dh
        dhb_ref[...] = (factor * dh).astype(BF16)

    outs, got = _with_rider(
        body, rider, name=name, grid=(s // tm,),
        in_specs=[_rows(tm, dy.shape[1]) for dy in dy_list] + [_whole(w.shape) for w in w_list]
        + [_rows(tm, d), _whole((1, d)), _rows(tm, d)],
        out_specs=[_rows(tm, d), _rows(tm, d), _whole((1, d))],
        out_shape=[jax.ShapeDtypeStruct((s, d), F32), jax.ShapeDtypeStruct((s, d), BF16),
                   jax.ShapeDtypeStruct((1, d), F32)],
        args=(*dy_list, *w_list, h, g, dh_in), sem=("arbitrary",))
    return outs if rider is None else (outs, got)


def _merge_bwd_call(dhb, gates, bm, bs, wo, wbm, wbs, name):
    s, d = bm.shape
    tm = min(TM_SMALL, s)

    def body(dh_ref, g_ref, bm_ref, bs_ref, wo_ref, wbm_ref, wbs_ref, dg_ref, dbm_ref, dbs_ref, dom_ref, dos_ref):
        dmg = _dot_nt(dh_ref[...], wo_ref[...])
        s1 = _sigmoid(g_ref[:, :d])
        s2 = _sigmoid(g_ref[:, d:])
        dg_ref[:, :d] = (dmg * bm_ref[...] * (s1 * (1.0 - s1))).astype(BF16)
        dg_ref[:, d:] = (dmg * bs_ref[...] * (s2 * (1.0 - s2))).astype(BF16)
        dbm = (dmg * s1).astype(BF16)
        dbs = (dmg * s2).astype(BF16)
        dbm_ref[...] = dbm
        dbs_ref[...] = dbs
        dom_ref[...] = _dot_nt(dbm, wbm_ref[...]).astype(BF16)
        dos_ref[...] = _dot_nt(dbs, wbs_ref[...]).astype(BF16)

    wm = wbm.shape[0]
    return pl.pallas_call(
        body, name=name, grid=(s // tm,),
        in_specs=[_rows(tm, d), _rows(tm, 2 * d), _rows(tm, d), _rows(tm, d),
                  _whole(wo.shape), _whole(wbm.shape), _whole(wbs.shape)],
        out_specs=[_rows(tm, 2 * d), _rows(tm, d), _rows(tm, d), _rows(tm, wm), _rows(tm, SB_WIDTH)],
        out_shape=[jax.ShapeDtypeStruct((s, 2 * d), BF16), jax.ShapeDtypeStruct((s, d), BF16),
                   jax.ShapeDtypeStruct((s, d), BF16), jax.ShapeDtypeStruct((s, wm), BF16),
                   jax.ShapeDtypeStruct((s, SB_WIDTH), BF16)],
        compiler_params=_cp(("parallel",)))(dhb, gates, bm, bs, wo, wbm, wbs)


def _mla_prep_bwd_call(cq, ckv, krope, pos, freq, sign, g_ql, g_kvl, g_qh, g_kh, wq, wkv, dq, dk, dv, name):
    s = cq.shape[0]
    tm = min(TM_SMALL, s)
    width = HEADS * HEAD_PAD

    def body(cq_ref, ckv_ref, kr_ref, pos_ref, freq_ref, sign_ref, gql_ref, gkvl_ref, gqh_ref, gkh_ref,
             wq_ref, wkv_ref, dq_ref, dk_ref, dv_ref,
             dcq_ref, dckv_ref, dkr_ref, dwq_ref, dwkv_ref, dgql_ref, dgkvl_ref, dgqh_ref, dgkh_ref, dqr_ref, dkv_ref):
        @pl.when(pl.program_id(0) == 0)
        def _():
            for ref in (dwq_ref, dwkv_ref, dgql_ref, dgkvl_ref, dgqh_ref, dgkh_ref):
                ref[...] = jnp.zeros_like(ref)

        cosv, ssv = _rope_tables(pos_ref, freq_ref, sign_ref)
        xq = cq_ref[...]
        rq = _rstd(xq, Q_LORA)
        cqn = ((xq * rq) * gql_ref[...]).astype(BF16)
        qr = _dot(cqn, wq_ref[...])
        xk = ckv_ref[...]
        rk = _rstd(xk, KV_LORA)
        ckvn = ((xk * rk) * gkvl_ref[...]).astype(BF16)
        kv = _dot(ckvn, wkv_ref[...])
        kr = kr_ref[...]
        lane = _lane((tm, HEAD_PAD))
        dkr = jnp.zeros((tm, HEAD_PAD), F32)
        dgqh = jnp.zeros((1, HEAD_PAD), F32)
        dgkh = jnp.zeros((1, HEAD_PAD), F32)
        for h in range(HEADS):
            sl = slice(h * HEAD_PAD, (h + 1) * HEAD_PAD)
            x = qr[:, sl]
            dx, dgh = _head_bwd(x, _rstd(x, MLA_QK), gqh_ref[...], cosv, ssv, dq_ref[:, sl])
            dqr_ref[:, sl] = dx.astype(BF16)
            dgqh = dgqh + dgh
            x = jnp.where(lane < MLA_NOPE, kv[:, sl], kr)
            dx, dgh = _head_bwd(x, _rstd(x, MLA_QK), gkh_ref[...], cosv, ssv, dk_ref[:, sl])
            dgkh = dgkh + dgh
            dkr = dkr + jnp.where(lane >= MLA_NOPE, dx, 0.0)
            dkv_ref[:, sl] = jnp.where(lane < MLA_NOPE, dx, dv_ref[:, sl]).astype(BF16)
        dgqh_ref[...] += dgqh
        dgkh_ref[...] += dgkh
        dkr_ref[...] = dkr.astype(BF16)
        dqr = dqr_ref[...]
        dkvb = dkv_ref[...]
        dwq_ref[...] += _dot_tn(cqn, dqr)
        dwkv_ref[...] += _dot_tn(ckvn, dkvb)
        dcqn = _dot_nt(dqr, wq_ref[...])
        dgql_ref[...] += jnp.sum(dcqn * (xq * rq), axis=0, keepdims=True)
        dcq_ref[...] = _rms_bwd(xq, rq, gql_ref[...], dcqn, Q_LORA).astype(BF16)
        dckvn = _dot_nt(dkvb, wkv_ref[...])
        dgkvl_ref[...] += jnp.sum(dckvn * (xk * rk), axis=0, keepdims=True)
        dckv_ref[...] = _rms_bwd(xk, rk, gkvl_ref[...], dckvn, KV_LORA).astype(BF16)

    vec = lambda n: jax.ShapeDtypeStruct((1, n), F32)
    outs = pl.pallas_call(
        body, name=name, grid=(s // tm,),
        in_specs=[_rows(tm, Q_LORA), _rows(tm, KV_LORA), _rows(tm, HEAD_PAD), _rows(tm, 1),
                  _whole((1, HEAD_PAD)), _whole((1, HEAD_PAD)), _whole((1, Q_LORA)), _whole((1, KV_LORA)),
                  _whole((1, HEAD_PAD)), _whole((1, HEAD_PAD)), _whole((Q_LORA, width)), _whole((KV_LORA, width)),
                  _rows(tm, width), _rows(tm, width), _rows(tm, width)],
        out_specs=[_rows(tm, Q_LORA), _rows(tm, KV_LORA), _rows(tm, HEAD_PAD), _whole((Q_LORA, width)),
                   _whole((KV_LORA, width)), _whole((1, Q_LORA)), _whole((1, KV_LORA)), _whole((1, HEAD_PAD)),
                   _whole((1, HEAD_PAD)), _rows(tm, width), _rows(tm, width)],
        out_shape=[jax.ShapeDtypeStruct((s, Q_LORA), BF16), jax.ShapeDtypeStruct((s, KV_LORA), BF16),
                   jax.ShapeDtypeStruct((s, HEAD_PAD), BF16), jax.ShapeDtypeStruct((Q_LORA, width), F32),
                   jax.ShapeDtypeStruct((KV_LORA, width), F32), vec(Q_LORA), vec(KV_LORA), vec(HEAD_PAD), vec(HEAD_PAD),
                   jax.ShapeDtypeStruct((s, width), BF16), jax.ShapeDtypeStruct((s, width), BF16)],
        compiler_params=_cp(("arbitrary",)))(cq, ckv, krope, pos, freq, sign, g_ql, g_kvl, g_qh, g_kh, wq, wkv, dq, dk, dv)
    return outs[:9]


def _tn_call(a, b, name, shard_cols=None, rider=None):
    s, ka = a.shape
    nb = b.shape[1]
    ti = _pick(ka, (512, 256, 128))
    if shard_cols is not None:
        tj = shard_cols
    else:
        tj = nb if nb <= TN_MAX_COLS else _pick(nb, (2176, 1024, 512, 256, 128))
    ts = s if 2 * s * (ti + tj) * a.dtype.itemsize <= TN_OPERAND_BYTES else s // 2
    ns = s // ts

    def body(a_ref, b_ref, o_ref, acc_ref):
        part = _dot_tn(a_ref[...].astype(BF16), b_ref[...].astype(BF16))
        if ns == 1:
            o_ref[...] = part.astype(o_ref.dtype)
            return

        @pl.when(pl.program_id(2) == 0)
        def _():
            acc_ref[...] = part

        @pl.when(pl.program_id(2) != 0)
        def _():
            acc_ref[...] += part

        @pl.when(pl.program_id(2) == ns - 1)
        def _():
            o_ref[...] = acc_ref[...].astype(o_ref.dtype)

    if shard_cols is None:
        out_spec = pl.BlockSpec((ti, tj), lambda i, j, t: (i, j))
        out_shape = jax.ShapeDtypeStruct((ka, nb), BF16)
    else:
        out_spec = pl.BlockSpec((None, ti, tj), lambda i, j, t: (j, i, 0))
        out_shape = jax.ShapeDtypeStruct((nb // tj, ka, tj), BF16)
    (out,), got = _with_rider(
        body, rider, name=name, grid=(ka // ti, nb // tj, ns),
        in_specs=[pl.BlockSpec((ts, ti), lambda i, j, t: (t, i)), pl.BlockSpec((ts, tj), lambda i, j, t: (t, j))],
        out_specs=[out_spec], out_shape=[out_shape], scratch=[pltpu.VMEM((ti, tj), F32)], args=(a, b),
        sem=("parallel", "parallel", "arbitrary"))
    return out if rider is None else (out, got)


def _sum_call(parts, out_dtype, name):
    n, r, w = parts.shape
    tr = _row_tile(r)

    def body(p_ref, o_ref):
        acc = p_ref[0].astype(F32)
        for k in range(1, n):
            acc = acc + p_ref[k].astype(F32)
        o_ref[...] = acc.astype(out_dtype)

    return pl.pallas_call(
        body, name=name, grid=(r // tr,),
        in_specs=[pl.BlockSpec((n, tr, w), lambda i: (0, i, 0))], out_specs=_rows(tr, w),
        out_shape=jax.ShapeDtypeStruct((r, w), out_dtype), compiler_params=_cp(("parallel",)))(parts)


def _chip_sum_call(by_chip, core, name):
    n, r, w = by_chip.shape
    tr = _row_tile(r)
    nblk = r // tr

    def body(c_ref, p_ref, o_ref):
        acc = p_ref[0].astype(F32)
        for k in range(1, n):
            acc = acc + p_ref[k].astype(F32)
        o_ref[...] = acc

    return pl.pallas_call(
        body, name=name,
        grid_spec=pltpu.PrefetchScalarGridSpec(
            num_scalar_prefetch=1, grid=(nblk,),
            in_specs=[pl.BlockSpec((n, tr, w), lambda i, c_ref: (0, i, 0))],
            out_specs=pl.BlockSpec((tr, w), lambda i, c_ref: (c_ref[0] * nblk + i, 0))),
        out_shape=jax.ShapeDtypeStruct((2 * r, w), F32),
        compiler_params=_cp(("parallel",)))(core.reshape(1).astype(jnp.int32), by_chip)


def _pair_sum_call(full, other, core, out_dtype, name):
    n, r, w = other.shape
    tr = _row_tile(r)
    nblk = r // tr

    def body(c_ref, a_ref, b_ref, o_ref):
        o_ref[...] = (a_ref[...].astype(F32) + b_ref[...].astype(F32)).astype(out_dtype)

    spec = pl.BlockSpec((None, tr, w), lambda k, i, c_ref: (k, i, 0))
    return pl.pallas_call(
        body, name=name,
        grid_spec=pltpu.PrefetchScalarGridSpec(
            num_scalar_prefetch=1, grid=(n, nblk),
            in_specs=[pl.BlockSpec((None, tr, w), lambda k, i, c_ref: (k, c_ref[0] * nblk + i, 0)), spec],
            out_specs=spec),
        out_shape=jax.ShapeDtypeStruct((n, r, w), out_dtype),
        compiler_params=_cp(("parallel", "parallel")))(core.reshape(1).astype(jnp.int32), full, other)


def _adamw_call(w, g, row0, m, v, name):
    r, c = w.shape
    tr = _pick(math.gcd(r, row0) if row0 else r, (256, 128, 64, 32, 16, 8))
    off = row0 // tr

    def body(w_ref, g_ref, m_ref, v_ref, g_out_ref, d_ref, nm_ref, nv_ref):
        gg = g_ref[...]
        g_out_ref[...] = gg
        nm = ADAM_B1 * m_ref[...] + (1.0 - ADAM_B1) * gg
        nv = ADAM_B2 * v_ref[...] + (1.0 - ADAM_B2) * (gg * gg)
        m_hat = nm / (1.0 - ADAM_B1 ** ADAM_STEP)
        v_hat = nv / (1.0 - ADAM_B2 ** ADAM_STEP)
        d_ref[...] = -ADAM_LR * (m_hat / (jnp.sqrt(v_hat) + ADAM_EPS) + ADAM_WD * w_ref[...])
        nm_ref[...] = nm
        nv_ref[...] = nv

    out = jax.ShapeDtypeStruct((r, c), F32)
    g_spec = pl.BlockSpec((tr, c), lambda i: (off + i, 0))
    return pl.pallas_call(
        body, name=name, grid=(r // tr,), in_specs=[_rows(tr, c), g_spec, _rows(tr, c), _rows(tr, c)],
        out_specs=[_rows(tr, c)] * 4, out_shape=[out, out, out, out], compiler_params=_cp(("parallel",)))(w, g, m, v)


def _position():
    x, y, c = lax.axis_index("x"), lax.axis_index("y"), lax.axis_index("c")
    chips = [(1 - x, y), (x, 1 - y), (1 - x, 1 - y)]
    return x, y, c, chips


def _gather_rider(parts):
    n = len(parts)
    pairs = [(j, k) for j in range(3) for k in range(n)]

    def piece(out_refs, k, chip, core):
        half = parts[k].shape[0] // 2
        return out_refs[k].at[2 * chip[0] + chip[1], pl.ds(core * half, half), :]

    def over_ici(in_refs, out_refs, sems, j, k):
        x, y, c, chips = _position()
        half = parts[k].shape[0] // 2
        return pltpu.make_async_remote_copy(
            src_ref=in_refs[k].at[pl.ds(c * half, half), :], dst_ref=piece(out_refs, k, (x, y), c),
            send_sem=sems[0].at[n * j + k], recv_sem=sems[1].at[n * j + k], device_id=(*chips[j], c), device_id_type=MESH)

    def to_sibling(out_refs, sems, j, k):
        x, y, c, chips = _position()
        landed = piece(out_refs, k, chips[j], c)
        return pltpu.make_async_remote_copy(
            src_ref=landed, dst_ref=landed, send_sem=sems[2].at[n * j + k], recv_sem=sems[3].at[n * j + k],
            device_id=(x, y, 1 - c), device_id_type=MESH)

    def start(in_refs, out_refs, sems):
        for j, k in pairs:
            over_ici(in_refs, out_refs, sems, j, k).start()

    def finish(in_refs, out_refs, sems):
        for j, k in pairs:
            over_ici(in_refs, out_refs, sems, j, k).wait_recv()
            to_sibling(out_refs, sems, j, k).start()
        for j, k in pairs:
            to_sibling(out_refs, sems, j, k).wait_recv()
        for j, k in pairs:
            over_ici(in_refs, out_refs, sems, j, k).wait_send()
            to_sibling(out_refs, sems, j, k).wait_send()

    return _Rider(list(parts), [jax.ShapeDtypeStruct((N_CHIPS,) + p.shape, p.dtype) for p in parts], [3 * n] * 4,
                  start, finish)


def _scatter_rider(parts):
    n = len(parts)
    pairs = [(j, k) for j in range(3) for k in range(n)]

    def copy(in_refs, out_refs, sems, j, k):
        x, y, c, chips = _position()
        return pltpu.make_async_remote_copy(
            src_ref=in_refs[k].at[2 * chips[j][0] + chips[j][1]], dst_ref=out_refs[k].at[2 * x + y],
            send_sem=sems[0].at[n * j + k], recv_sem=sems[1].at[n * j + k], device_id=(*chips[j], c), device_id_type=MESH)

    def start(in_refs, out_refs, sems):
        for j, k in pairs:
            copy(in_refs, out_refs, sems, j, k).start()

    def finish(in_refs, out_refs, sems):
        for j, k in pairs:
            copy(in_refs, out_refs, sems, j, k).wait()

    return _Rider(list(parts), [jax.ShapeDtypeStruct(p.shape, p.dtype) for p in parts], [3 * n] * 2, start, finish)


def _exchange_call(rider, name):
    n, m = len(rider.ins), len(rider.out_shape)

    def body(*refs):
        rider.start(refs[:n], refs[n:n + m], refs[n + m:])
        rider.finish(refs[:n], refs[n:n + m], refs[n + m:])

    return pl.pallas_call(
        body, name=name, in_specs=[HBM] * n, out_specs=[HBM] * m, out_shape=rider.out_shape,
        scratch_shapes=[pltpu.SemaphoreType.DMA((k,)) for k in rider.sems])(*rider.ins)


def _pair_send_call(parts, name):
    n = len(parts)

    def body(*refs):
        in_refs, out_refs = refs[:n], refs[n:2 * n]
        send_sems, recv_sems = refs[2 * n:]
        x, y, c, _ = _position()
        copies = []
        for k in range(n):
            half = parts[k].shape[1] // 2
            cp = pltpu.make_async_remote_copy(
                src_ref=in_refs[k].at[:, pl.ds((1 - c) * half, half), :], dst_ref=out_refs[k],
                send_sem=send_sems.at[k], recv_sem=recv_sems.at[k], device_id=(x, y, 1 - c), device_id_type=MESH)
            cp.start()
            copies.append(cp)
        for cp in copies:
            cp.wait()

    sems = pltpu.SemaphoreType.DMA((n,))
    return pl.pallas_call(
        body, name=name, in_specs=[HBM] * n, out_specs=[HBM] * n,
        out_shape=[jax.ShapeDtypeStruct((p.shape[0], p.shape[1] // 2, p.shape[2]), p.dtype) for p in parts],
        scratch_shapes=[sems, sems])(*parts)


def _pair_swap_call(parts, name):
    n = len(parts)

    def body(*refs):
        out_refs = refs[n:2 * n]
        send_sems, recv_sems = refs[2 * n:]
        x, y, c, _ = _position()
        copies = []
        for k in range(n):
            half = parts[k].shape[0] // 2
            mine = out_refs[k].at[pl.ds(c * half, half), :]
            cp = pltpu.make_async_remote_copy(
                src_ref=mine, dst_ref=mine, send_sem=send_sems.at[k], recv_sem=recv_sems.at[k],
                device_id=(x, y, 1 - c), device_id_type=MESH)
            cp.start()
            copies.append(cp)
        for cp in copies:
            cp.wait()

    sems = pltpu.SemaphoreType.DMA((n,))
    return pl.pallas_call(
        body, name=name, in_specs=[HBM] * n, out_specs=[HBM] * n,
        out_shape=[jax.ShapeDtypeStruct(p.shape, p.dtype) for p in parts],
        input_output_aliases={k: k for k in range(n)},
        scratch_shapes=[sems, sems])(*parts)


def _all_gather_small_call(block, name):
    r, w = block.shape

    def body(in_ref, out_ref, send_sems, recv_sems, local_sem):
        x, y, c, _ = _position()
        me = 4 * x + 2 * y + c
        own = pltpu.make_async_copy(in_ref, out_ref.at[me], local_sem)
        own.start()
        copies = []
        for k in range(1, 8):
            peer = (x ^ (k >> 2), y ^ ((k >> 1) & 1), c ^ (k & 1))
            cp = pltpu.make_async_remote_copy(
                src_ref=in_ref, dst_ref=out_ref.at[me], send_sem=send_sems.at[k - 1], recv_sem=recv_sems.at[k - 1],
                device_id=peer, device_id_type=MESH)
            cp.start()
            copies.append(cp)
        for cp in copies:
            cp.wait()
        own.wait()

    return pl.pallas_call(
        body, name=name, in_specs=[HBM], out_specs=HBM,
        out_shape=jax.ShapeDtypeStruct((8, r, w), block.dtype),
        scratch_shapes=[pltpu.SemaphoreType.DMA((7,)), pltpu.SemaphoreType.DMA((7,)), pltpu.SemaphoreType.DMA])(block)


BIG = {
    "ffn1_w_in": ((D_MODEL, 2 * D_FF), 1), "ffn1_w_out": ((D_FF, D_MODEL), 0),
    "w_in": ((D_MODEL, 4256), 1), "w_q_up": ((Q_LORA, HEADS * MLA_QK), 1), "w_kv_up": ((KV_LORA, 1024), 1),
    "w_branch_mla": ((512, D_MODEL), 1), "w_branch_sb": ((SB_WIDTH, D_MODEL), 1), "w_out": ((D_MODEL, D_MODEL), 0),
    "ffn2_w_in": ((D_MODEL, 2 * D_FF), 1), "ffn2_w_out": ((D_FF, D_MODEL), 0),
    "w_ple_gate": ((D_MODEL, D_MODEL), 0), "w_ple_proj": ((PLE_DIM, D_MODEL), 1),
}
GAINS = {"ffn1_norm": 1024, "mix_norm": 1024, "q_latent_norm": 384, "kv_latent_norm": 256, "q_head_norm": 96,
         "k_head_norm": 96, "ffn2_norm": 1024, "ple_norm": 1024}
WEIGHT_ORDER = ["ffn1_norm", "ffn1_w_in", "ffn1_w_out", "mix_norm", "w_in", "q_latent_norm", "w_q_up",
                "kv_latent_norm", "w_kv_up", "q_head_norm", "k_head_norm", "w_branch_mla", "w_branch_sb", "w_out",
                "ffn2_norm", "ffn2_w_in", "ffn2_w_out", "ple_norm", "w_ple_gate", "w_ple_proj"]


def _shard_shape(name):
    (r, c), axis = BIG[name]
    return (r // N_CHIPS, c) if axis == 0 else (r, c // N_CHIPS)


GATHER_GROUPS = [
    [("ffn1_w_in",)],
    [("ffn1_w_out",), ("w_in",)],
    [("w_out",), ("w_kv_up", "w_branch_mla", "w_branch_sb"), ("w_q_up",)],
    [("ffn2_w_in",), ("ffn2_w_out", "w_ple_gate"), ("w_ple_proj",)],
]
REDUCE_GROUPS = [
    [("ffn2_w_in",), ("ffn2_w_out", "w_out", "w_ple_gate"), ("w_branch_mla", "w_branch_sb", "w_ple_proj")],
    [("w_in",), ("w_kv_up",), ("w_q_up",)],
    [("ffn1_w_out",)],
    [("ffn1_w_in",)],
]


def _join_parts(shards, group):
    return [shards[part[0]] if len(part) == 1 else jnp.concatenate([shards[n] for n in part], axis=-2) for part in group]


def _part_rows(group):
    where = {}
    for k, part in enumerate(group):
        at = 0
        for n in part:
            where[n] = (k, at)
            at += _shard_shape(n)[0]
    return where


def _split_parts(parts, group):
    return {n: parts[k][..., at:at + _shard_shape(n)[0], :] for n, (k, at) in _part_rows(group).items()}


def _to_shards(name, full):
    (r, c), axis = BIG[name]
    if axis == 0:
        return full.reshape(N_CHIPS, r // N_CHIPS, c)
    return full.reshape(r, N_CHIPS, c // N_CHIPS).transpose(1, 0, 2)


def _from_shards(name, shards):
    (r, c), axis = BIG[name]
    if axis == 0:
        return shards.reshape(r, c)
    return shards.transpose(1, 0, 2).reshape(r, c)


def _relayout_w_in(w):
    d = w.shape[0]
    z = lambda n: jnp.zeros((d, n), w.dtype)
    return jnp.concatenate([w[:, :640], z(MLA_NOPE), w[:, 640:672], z(HEAD_PAD - MLA_QK), w[:, 672:]], axis=1)


def _unlayout_w_in(g):
    return jnp.concatenate([g[:, :640], g[:, 640 + MLA_NOPE:640 + MLA_QK], g[:, 768:]], axis=1)


def _pad_heads(v):
    lead = v.shape[:-1]
    return jnp.pad(v.reshape(lead + (HEADS, MLA_QK)), [(0, 0)] * len(lead) + [(0, 0), (0, HEAD_PAD - MLA_QK)]).reshape(
        lead + (HEADS * HEAD_PAD,))


SHARD_MAJOR = ("ffn1_w_in", "ffn2_w_in")


def _step(x, p, pos, tgt, gains, weights, dist):
    d = D_MODEL
    full = {} if dist is not None else {n: _to_shards(n, w) if n in SHARD_MAJOR else w for n, w in weights.items()}
    reduced = {}

    def gather_rider(g):
        if dist is None:
            return None, None
        mine = _join_parts(weights, GATHER_GROUPS[g])
        return mine, _gather_rider(mine)

    def gathered(g, mine, others):
        if dist is not None:
            parts = [lax.dynamic_update_slice_in_dim(o, m[None], dist[0], axis=0) for o, m in zip(others, mine)]
            for n, shards in _split_parts(parts, GATHER_GROUPS[g]).items():
                full[n] = shards if n in SHARD_MAJOR else _from_shards(n, shards)

    def reduce_before(g):
        if dist is None:
            return None, None
        group = REDUCE_GROUPS[g]
        shards = {n: grads[n] if grads[n].ndim == 3 else _to_shards(n, grads[n].astype(BF16)) for part in group for n in part}
        partial = _join_parts(shards, group)
        from_sibling = _pair_send_call(partial, "grads%d_pair_send" % g)
        pair_sum = [_pair_sum_call(a, b, dist[1], BF16, "grads%d_pair_sum_%d" % (g, k))
                    for k, (a, b) in enumerate(zip(partial, from_sibling))]
        return pair_sum, _scatter_rider(pair_sum)

    def reduce_after(g, pair_sum, by_chip):
        if dist is not None:
            chip, core = dist
            by_chip = [lax.dynamic_update_slice_in_dim(t, lax.dynamic_slice_in_dim(o, chip, 1, axis=0), chip, axis=0)
                       for t, o in zip(by_chip, pair_sum)]
            bufs = _pair_swap_call([_chip_sum_call(t, core, "grads%d_chip_sum_%d" % (g, k)) for k, t in enumerate(by_chip)],
                                   "grads%d_pair_swap" % g)
            for n, (k, row0) in _part_rows(REDUCE_GROUPS[g]).items():
                reduced[n] = (bufs[k], row0)

    mine, rider = gather_rider(0)
    if dist is not None:
        gathered(0, mine, _exchange_call(rider, "gather0"))
    wts = full
    inv_freq = ROPE_BASE ** (-jnp.arange(0, MLA_ROPE, 2, dtype=F32) / MLA_ROPE)
    zeros = lambda n: jnp.zeros((n,), F32)
    freq = jnp.concatenate([zeros(MLA_NOPE), inv_freq, inv_freq, zeros(HEAD_PAD - MLA_QK)])[None]
    sign = jnp.concatenate([zeros(MLA_NOPE), -jnp.ones((16,), F32), jnp.ones((16,), F32), zeros(HEAD_PAD - MLA_QK)])[None]
    pad_gain = lambda g: jnp.pad(g, ((0, 0), (0, HEAD_PAD - MLA_QK)))
    g_qh, g_kh = pad_gain(gains["q_head_norm"]), pad_gain(gains["k_head_norm"])

    u1 = _norm_call(x, gains["ffn1_norm"], "norm_ffn1")
    mine, rider = gather_rider(1)
    (a1, b1, hm1), got = _ffn_in_call(u1, wts["ffn1_w_in"], "ffn1_in", rider)
    gathered(1, mine, got)
    mine, rider = gather_rider(2)
    (h1, um), got = _ffn_out_call(hm1, wts["ffn1_w_out"], x, gains["mix_norm"], "ffn1_out", rider)
    gathered(2, mine, got)
    w_in = _relayout_w_in(wts["w_in"])
    wq = _pad_heads(wts["w_q_up"])
    wkv = wts["w_kv_up"]
    wbm = jnp.pad(wts["w_branch_mla"].reshape(HEADS, 64, d), ((0, 0), (64, 0), (0, 0))).reshape(HEADS * HEAD_PAD, d)
    wbs, wo = wts["w_branch_sb"], wts["w_out"]
    cq, ckv, krope, sbq, sbk, sbv, gates = _mix_in_call(um, w_in, "mix_in")
    prep_args = (cq, ckv, krope, pos, freq, sign, gains["q_latent_norm"], gains["kv_latent_norm"], g_qh, g_kh, wq, wkv)
    q, k, v = _mla_prep_call(*prep_args, "mla_prep")
    mine, rider = gather_rider(3)
    (om, lse), got = _mla_fwd_call(q, k, v, "mla_fwd", rider)
    gathered(3, mine, got)
    osb = _sb_fwd_call(sbq, sbk, sbv, "sb_fwd")
    h2, bm, bs, mg, u2 = _merge_out_call(om, osb, gates, h1, wbm, wbs, wo, gains["ffn2_norm"], "merge_out")
    (a2, b2, hm2), _ = _ffn_in_call(u2, wts["ffn2_w_in"], "ffn2_in")
    (h3, _), _ = _ffn_out_call(hm2, wts["ffn2_w_out"], h2, gains["ple_norm"], "ffn2_out")

    grads, gg = {}, {}
    dh3, dh3s, un, dgl, dpp, gg["ple_norm"], sq = _ple_call(
        h3, gains["ple_norm"], wts["w_ple_gate"], p, wts["w_ple_proj"], tgt, "ple")
    grads["w_ple_gate"] = _tn_call(un, dgl, "dw_ple_gate")
    grads["w_ple_proj"] = _tn_call(p, dpp, "dw_ple_proj")

    (da2, db2), _ = _ffn_bwd_a_call(dh3s, a2, b2, wts["ffn2_w_out"], "ffn2_bwd_act")
    grads["ffn2_w_out"] = _tn_call(hm2, dh3s, "dw_ffn2_out")
    grads["ffn2_w_in"] = jnp.concatenate([_tn_call(u2, da2, "dw_ffn2_in_a", shard_cols=D_FF // 2),
                                          _tn_call(u2, db2, "dw_ffn2_in_b", shard_cols=D_FF // 2)], axis=0)
    dh2, dh2b, gg["ffn2_norm"] = _norm_bwd_call([da2, db2], [wts["ffn2_w_in"]], h2, gains["ffn2_norm"], dh3,
                                                "ffn2_bwd_norm", half_out=False)

    dgates, dbm, dbs, dom, dos = _merge_bwd_call(dh2b, gates, bm, bs, wo, wbm, wbs, "merge_bwd")
    grads["w_out"] = _tn_call(mg, dh2b, "dw_out")
    grads["w_branch_mla"] = _tn_call(om, dbm, "dw_branch_mla").reshape(HEADS, HEAD_PAD, d)[:, 64:, :].reshape(512, d)
    grads["w_branch_sb"] = _tn_call(osb, dbs, "dw_branch_sb")
    pair_sum, rider = reduce_before(0)
    (dq, dk, dv), got = _mla_bwd_call(q, k, v, om, dom, lse, "mla_bwd", rider)
    reduce_after(0, pair_sum, got)
    dsq, dsk, dsv = _sb_bwd_call(sbq, sbk, sbv, dos, osb, "sb_bwd")
    (dcq, dckv, dkr, dwq, grads["w_kv_up"], gg["q_latent_norm"], gg["kv_latent_norm"], dgqh, dgkh) = \
        _mla_prep_bwd_call(*prep_args, dq, dk, dv, "mla_prep_bwd")
    grads["w_q_up"] = dwq.reshape(Q_LORA, HEADS, HEAD_PAD)[:, :, :MLA_QK].reshape(Q_LORA, HEADS * MLA_QK)
    gg["q_head_norm"], gg["k_head_norm"] = dgqh[:, :MLA_QK], dgkh[:, :MLA_QK]
    dproj = jnp.concatenate([dcq, dckv, dkr, dsq, dsk.astype(BF16), dsv.astype(BF16), dgates], axis=1)
    grads["w_in"] = _unlayout_w_in(_tn_call(um, dproj, "dw_in"))
    dh1, dh1s, gg["mix_norm"] = _norm_bwd_call([dproj], [w_in], h1, gains["mix_norm"], dh2, "mix_bwd_norm", half_out=True)

    pair_sum, rider = reduce_before(1)
    (da1, db1), got = _ffn_bwd_a_call(dh1s, a1, b1, wts["ffn1_w_out"], "ffn1_bwd_act", rider)
    reduce_after(1, pair_sum, got)
    grads["ffn1_w_out"] = _tn_call(hm1, dh1s, "dw_ffn1_out")
    pair_sum, rider = reduce_before(2)
    res = _tn_call(u1, da1, "dw_ffn1_in_a", shard_cols=D_FF // 2, rider=rider)
    dwa, got = (res, None) if rider is None else res
    reduce_after(2, pair_sum, got)
    grads["ffn1_w_in"] = jnp.concatenate([dwa, _tn_call(u1, db1, "dw_ffn1_in_b", shard_cols=D_FF // 2)], axis=0)
    pair_sum, rider = reduce_before(3)
    res = _norm_bwd_call([da1, db1], [wts["ffn1_w_in"]], x, gains["ffn1_norm"], dh1, "ffn1_bwd_norm",
                         half_out=False, rider=rider)
    (dx, _, gg["ffn1_norm"]), got = (res, None) if rider is None else res
    reduce_after(3, pair_sum, got)
    return sq, dx, gg, (grads if dist is None else reduced)


def kernel(x, p, positions, ffn1_norm, ffn1_w_in, ffn1_w_out, mix_norm, w_in, q_latent_norm, w_q_up, kv_latent_norm, w_kv_up, q_head_norm, k_head_norm, w_branch_mla, w_branch_sb, w_out, ffn2_norm, ffn2_w_in, ffn2_w_out, ple_norm, w_ple_gate, w_ple_proj, loss_target, m_ffn1_norm, m_ffn1_w_in, m_ffn1_w_out, m_mix_norm, m_w_in, m_q_latent_norm, m_w_q_up, m_kv_latent_norm, m_w_kv_up, m_q_head_norm, m_k_head_norm, m_w_branch_mla, m_w_branch_sb, m_w_out, m_ffn2_norm, m_ffn2_w_in, m_ffn2_w_out, m_ple_norm, m_w_ple_gate, m_w_ple_proj, v_ffn1_norm, v_ffn1_w_in, v_ffn1_w_out, v_mix_norm, v_w_in, v_q_latent_norm, v_w_q_up, v_kv_latent_norm, v_w_kv_up, v_q_head_norm, v_k_head_norm, v_w_branch_mla, v_w_branch_sb, v_w_out, v_ffn2_norm, v_ffn2_w_in, v_ffn2_w_out, v_ple_norm, v_w_ple_gate, v_w_ple_proj):
    given = dict(locals())
    w_shard = {n: given[n][0] for n in WEIGHT_ORDER}
    m_shard = {n: given["m_" + n][0] for n in WEIGHT_ORDER}
    v_shard = {n: given["v_" + n][0] for n in WEIGHT_ORDER}
    gains = {n: w_shard[n][None] for n in GAINS}

    chip = 2 * lax.axis_index("x") + lax.axis_index("y")
    sq, dx, gain_grads, reduced = _step(x[0], p[0, 0], positions.reshape(-1, 1), loss_target[0], gains,
                                        {n: w_shard[n].astype(BF16) for n in BIG}, (chip, lax.axis_index("c")))

    rows = [jnp.pad(gain_grads[n], ((0, 0), (0, D_MODEL - GAINS[n]))) for n in GAINS] + [sq]
    gain_block = jnp.concatenate(rows + [jnp.zeros((16 - len(rows), D_MODEL), F32)], axis=0)
    gain_sum = _sum_call(_all_gather_small_call(gain_block, "gains_all_gather"), F32, "gains_sum")
    loss = 0.5 * jnp.sum(gain_sum[len(GAINS)]) / D_MODEL

    outs = {"grad": {}, "delta": {}, "new_m": {}, "new_v": {}}
    gain_pack = lambda t: jnp.concatenate([jnp.pad(t[n][None], ((0, 0), (0, D_MODEL - GAINS[n]))) for n in GAINS], axis=0)
    packed = _adamw_call(gain_pack(w_shard), gain_sum, 0, gain_pack(m_shard), gain_pack(v_shard), "adamw_gains")
    for i, n in enumerate(GAINS):
        for kind, t in zip(("grad", "delta", "new_m", "new_v"), packed):
            outs[kind][n] = t[i, :GAINS[n]][None]
    for n in BIG:
        buf, row0 = reduced[n]
        for kind, t in zip(("grad", "delta", "new_m", "new_v"),
                           _adamw_call(w_shard[n], buf, row0, m_shard[n], v_shard[n], "adamw_" + n)):
            outs[kind][n] = t[None]

    return (loss, dx[None], *[outs["grad"][n] for n in WEIGHT_ORDER], *[outs["delta"][n] for n in WEIGHT_ORDER],
            *[outs["new_m"][n] for n in WEIGHT_ORDER], *[outs["new_v"][n] for n in WEIGHT_ORDER])
```

```python
import collections
import functools
import math

import jax
import jax.numpy as jnp
from jax import lax
from jax.experimental import pallas as pl
from jax.experimental.pallas import tpu as pltpu

F32 = jnp.float32
BF16 = jnp.bfloat16
MESH = pl.DeviceIdType.MESH

D_MODEL = 1024
D_FF = 2816
PLE_DIM = 256
NORM_EPS = 1e-6
HEADS = 8
MLA_NOPE = 64
MLA_ROPE = 32
MLA_QK = 96
Q_LORA = 384
KV_LORA = 256
SB_WIDTH = 512
ROPE_BASE = 10000.0
LOG2_E = math.log2(math.e)
HEAD_PAD = 128
N_CHIPS = 4

ADAM_LR = 0.001
ADAM_B1 = 0.9
ADAM_B2 = 0.999
ADAM_EPS = 1e-08
ADAM_WD = 0.01
ADAM_STEP = 10

SEG_CQ = (0, 384)
SEG_CKV = (384, 256)
SEG_KROPE = (640, 128)
SEG_SBQ = (768, 512)
SEG_SBK = (1280, 512)
SEG_SBV = (1792, 512)
SEG_GATES = (2304, 2048)
IN_COLS_PAD = 4352

TM = 512
TM_SMALL = 256
TQ = 256
MLA_FWD_BLOCKS = 4
MLA_BWD_BLOCKS = 4
SB_FWD_BLOCKS = 4
SB_BWD_BLOCKS = 2
SB_HEAD = 64
SB_SCALE = 0.125
SB_DEAD = -104.0
COL_CHUNK = 256
TN_MAX_COLS = 2816
TN_OPERAND_BYTES = 34 * 1024 * 1024
MAX_ROW_TILE = 512
VMEM_LIMIT = 56 * 1024 * 1024

NT = (((1,), (1,)), ((), ()))
TN = (((0,), (0,)), ((), ()))


def _cp(sem):
    return pltpu.CompilerParams(dimension_semantics=sem, vmem_limit_bytes=VMEM_LIMIT)


def _rows(tm, w):
    return pl.BlockSpec((tm, w), lambda i: (i, 0))


def _whole(shape):
    return pl.BlockSpec(shape, lambda i: (0,) * len(shape))


def _dot(a, b):
    return jnp.dot(a, b, preferred_element_type=F32)


def _dot_nt(a, b):
    return lax.dot_general(a, b, NT, preferred_element_type=F32)


def _dot_tn(a, b):
    return lax.dot_general(a, b, TN, preferred_element_type=F32)


def _rstd(x, n):
    return lax.rsqrt(jnp.sum(x * x, axis=-1, keepdims=True) / n + NORM_EPS)


def _rms_bwd(x, r, g, dy, n):
    gy = dy * g
    return r * gy - x * ((r * r * r) * (jnp.sum(x * gy, axis=-1, keepdims=True) / n))


def _sigmoid(x):
    return jax.nn.sigmoid(x)


def _pick(n, cands):
    for c in cands:
        if n % c == 0:
            return c
    return n


def _row_tile(r):
    for t in range(min(r, MAX_ROW_TILE) // 16 * 16, 15, -16):
        if r % t == 0:
            return t
    return r


HBM = pl.BlockSpec(memory_space=pl.ANY)

_Rider = collections.namedtuple("_Rider", "ins out_shape sems start finish")


def _with_rider(body, rider, *, name, grid, in_specs, out_specs, out_shape, args, sem, scratch=()):
    if rider is None:
        return pl.pallas_call(body, name=name, grid=grid, in_specs=in_specs, out_specs=out_specs, out_shape=out_shape,
                              scratch_shapes=list(scratch), compiler_params=_cp(sem))(*args), None
    ni, no, nri, nro = len(in_specs), len(out_specs), len(rider.ins), len(rider.out_shape)

    def riding(*refs):
        ins, r_ins = refs[:ni], refs[ni:ni + nri]
        outs, r_outs = refs[ni + nri:ni + nri + no], refs[ni + nri + no:ni + nri + no + nro]
        scr = refs[ni + nri + no + nro:ni + nri + no + nro + len(scratch)]
        sems = refs[ni + nri + no + nro + len(scratch):]
        ids = [pl.program_id(a) for a in range(len(grid))]
        first = functools.reduce(jnp.logical_and, [i == 0 for i in ids])
        last = functools.reduce(jnp.logical_and, [i == g - 1 for i, g in zip(ids, grid)])

        @pl.when(first)
        def _():
            rider.start(r_ins, r_outs, sems)

        body(*ins, *outs, *scr)

        @pl.when(last)
        def _():
            rider.finish(r_ins, r_outs, sems)

    res = pl.pallas_call(
        riding, name=name, grid=grid, in_specs=list(in_specs) + [HBM] * nri, out_specs=list(out_specs) + [HBM] * nro,
        out_shape=list(out_shape) + list(rider.out_shape),
        scratch_shapes=list(scratch) + [pltpu.SemaphoreType.DMA((k,)) for k in rider.sems],
        compiler_params=_cp(("arbitrary",) * len(grid)))(*args, *rider.ins)
    return res[:no], res[no:]


def _norm_call(h, g, name):
    s, d = h.shape
    tm = min(TM, s)

    def body(h_ref, g_ref, u_ref):
        x = h_ref[...]
        u_ref[...] = ((x * _rstd(x, d)) * g_ref[...]).astype(BF16)

    return pl.pallas_call(
        body, name=name, grid=(s // tm,),
        in_specs=[_rows(tm, d), _whole((1, d))], out_specs=_rows(tm, d),
        out_shape=jax.ShapeDtypeStruct((s, d), BF16), compiler_params=_cp(("parallel",)))(h, g)


def _ffn_in_call(u, w, name, rider=None):
    s, d = u.shape
    tn = w.shape[2]
    nj = w.shape[0] // 2
    n = nj * tn
    tm = min(TM, s)

    def body(u_ref, wa_ref, wb_ref, a_ref, b_ref, hm_ref):
        uu = u_ref[...]
        a = _dot(uu, wa_ref[...])
        b = _dot(uu, wb_ref[...])
        a_ref[...] = a
        b_ref[...] = b
        hm_ref[...] = ((a * _sigmoid(a)) * b).astype(BF16)

    blk = pl.BlockSpec((tm, tn), lambda j, i: (i, j))
    return _with_rider(
        body, rider, name=name, grid=(nj, s // tm),
        in_specs=[pl.BlockSpec((tm, d), lambda j, i: (i, 0)),
                  pl.BlockSpec((None, d, tn), lambda j, i: (j, 0, 0)),
                  pl.BlockSpec((None, d, tn), lambda j, i: (j + nj, 0, 0))],
        out_specs=[blk, blk, blk],
        out_shape=[jax.ShapeDtypeStruct((s, n), F32), jax.ShapeDtypeStruct((s, n), F32),
                   jax.ShapeDtypeStruct((s, n), BF16)],
        args=(u, w, w), sem=("parallel", "parallel"))


def _ffn_out_call(hm, w, h, gain, name, rider=None):
    s, n = hm.shape
    d = w.shape[1]
    tm = min(TM, s)

    def body(hm_ref, w_ref, h_ref, g_ref, o_ref, u_ref):
        x = h_ref[...] + 0.5 * _dot(hm_ref[...], w_ref[...])
        o_ref[...] = x
        u_ref[...] = ((x * _rstd(x, d)) * g_ref[...]).astype(BF16)

    return _with_rider(
        body, rider, name=name, grid=(s // tm,),
        in_specs=[_rows(tm, n), _whole((n, d)), _rows(tm, d), _whole((1, d))], out_specs=[_rows(tm, d), _rows(tm, d)],
        out_shape=[jax.ShapeDtypeStruct((s, d), F32), jax.ShapeDtypeStruct((s, d), BF16)], args=(hm, w, h, gain),
        sem=("parallel",))


def _mix_in_call(u, w, name):
    s, d = u.shape
    tm = min(TM_SMALL, s)
    segs = [(SEG_CQ, F32), (SEG_CKV, F32), (SEG_KROPE, F32), (SEG_SBQ, BF16), (SEG_SBK, BF16),
            (SEG_SBV, BF16), (SEG_GATES, F32)]

    def body(u_ref, w_ref, *outs):
        uu = u_ref[...]
        for ((off, width), _), o_ref in zip(segs, outs):
            o_ref[...] = _dot(uu, w_ref[:, off:off + width]).astype(o_ref.dtype)

    return pl.pallas_call(
        body, name=name, grid=(s // tm,),
        in_specs=[_rows(tm, d), _whole((d, IN_COLS_PAD))],
        out_specs=[_rows(tm, width) for (_, width), _ in segs],
        out_shape=[jax.ShapeDtypeStruct((s, width), dt) for (_, width), dt in segs],
        compiler_params=_cp(("parallel",)))(u, w)


def _lane(shape):
    return lax.broadcasted_iota(jnp.int32, shape, len(shape) - 1)


def _rot_half(y):
    lane = _lane(y.shape)
    swapped = jnp.where(lane < MLA_NOPE + MLA_ROPE // 2, pltpu.roll(y, HEAD_PAD - 16, 1), pltpu.roll(y, 16, 1))
    return jnp.where((lane >= MLA_NOPE) & (lane < MLA_QK), swapped, 0.0)


def _rope_tables(pos_ref, freq_ref, sign_ref):
    ang = pos_ref[...].astype(F32) * freq_ref[...]
    return jnp.cos(ang), jnp.sin(ang) * sign_ref[...]


def _head_fwd(x, g, cosv, ssv):
    r = _rstd(x, MLA_QK)
    y = (x * r) * g
    return y * cosv + _rot_half(y) * ssv, r


def _head_bwd(x, r, g, cosv, ssv, dout):
    dy = dout * cosv + _rot_half(dout * ssv)
    return _rms_bwd(x, r, g, dy, MLA_QK), jnp.sum(dy * (x * r), axis=0, keepdims=True)


def _mla_prep_call(cq, ckv, krope, pos, freq, sign, g_ql, g_kvl, g_qh, g_kh, wq, wkv, name):
    s = cq.shape[0]
    tm = min(TM_SMALL, s)
    width = HEADS * HEAD_PAD

    def body(cq_ref, ckv_ref, kr_ref, pos_ref, freq_ref, sign_ref, gql_ref, gkvl_ref, gqh_ref, gkh_ref,
             wq_ref, wkv_ref, q_ref, k_ref, v_ref):
        cosv, ssv = _rope_tables(pos_ref, freq_ref, sign_ref)
        x = cq_ref[...]
        qr = _dot(((x * _rstd(x, Q_LORA)) * gql_ref[...]).astype(BF16), wq_ref[...])
        x = ckv_ref[...]
        kv = _dot(((x * _rstd(x, KV_LORA)) * gkvl_ref[...]).astype(BF16), wkv_ref[...])
        kr = kr_ref[...]
        lane = _lane((tm, HEAD_PAD))
        for h in range(HEADS):
            sl = slice(h * HEAD_PAD, (h + 1) * HEAD_PAD)
            qh, _ = _head_fwd(qr[:, sl], gqh_ref[...], cosv, ssv)
            q_ref[:, sl] = qh.astype(BF16)
            kvh = kv[:, sl]
            kh, _ = _head_fwd(jnp.where(lane < MLA_NOPE, kvh, kr), gkh_ref[...], cosv, ssv)
            k_ref[:, sl] = kh.astype(BF16)
            v_ref[:, sl] = jnp.where(lane >= MLA_NOPE, kvh, jnp.where(lane == 0, 1.0, 0.0)).astype(BF16)

    out = jax.ShapeDtypeStruct((s, width), BF16)
    return pl.pallas_call(
        body, name=name, grid=(s // tm,),
        in_specs=[_rows(tm, Q_LORA), _rows(tm, KV_LORA), _rows(tm, HEAD_PAD), _rows(tm, 1),
                  _whole((1, HEAD_PAD)), _whole((1, HEAD_PAD)), _whole((1, Q_LORA)), _whole((1, KV_LORA)),
                  _whole((1, HEAD_PAD)), _whole((1, HEAD_PAD)), _whole((Q_LORA, width)), _whole((KV_LORA, width))],
        out_specs=[_rows(tm, width)] * 3, out_shape=[out, out, out],
        compiler_params=_cp(("parallel",)))(cq, ckv, krope, pos, freq, sign, g_ql, g_kvl, g_qh, g_kh, wq, wkv)


def _attn_specs(s, nb):
    qspec = pl.BlockSpec((TQ, nb * HEAD_PAD), lambda g, i: (i, g))
    kspec = pl.BlockSpec((s, nb * HEAD_PAD), lambda g, i: (0, g))
    return qspec, kspec


def _lanes(b):
    return slice(b * HEAD_PAD, (b + 1) * HEAD_PAD)


def _tri(cmp):
    r = lax.broadcasted_iota(jnp.int32, (TQ, TQ), 0)
    c = lax.broadcasted_iota(jnp.int32, (TQ, TQ), 1)
    return cmp(r, c)


def _mla_fwd_call(q, k, v, name, rider=None):
    s, width = q.shape
    scale = 1.0 / math.sqrt(MLA_QK)

    nb = MLA_FWD_BLOCKS

    def body(q_ref, k_ref, v_ref, o_ref, lse_ref):
        qi = pl.program_id(1)
        qs = [q_ref[:, _lanes(b)] for b in range(nb)]
        causal = _tri(lambda r, c: c <= r)

        def step(kb, carry, diag):
            ks = pl.multiple_of(kb * TQ, TQ)
            heads = range(nb)
            scs = [_dot_nt(qs[b], k_ref[pl.ds(ks, TQ), _lanes(b)]) * (scale * LOG2_E) for b in heads]
            if diag:
                scs = [jnp.where(causal, sc, -1e30) for sc in scs]
            mns = [jnp.maximum(carry[b][0], jnp.max(scs[b], axis=-1, keepdims=True)) for b in heads]
            als = [jnp.exp2(carry[b][0] - mns[b]) for b in heads]
            ps = [jnp.exp2(scs[b] - mns[b]).astype(BF16) for b in heads]
            accs = [als[b] * carry[b][1] + _dot(ps[b], v_ref[pl.ds(ks, TQ), _lanes(b)]) for b in heads]
            return tuple((mns[b], accs[b]) for b in heads)

        init = tuple((jnp.full((TQ, 1), -1e30, F32), jnp.zeros((TQ, HEAD_PAD), F32)) for _ in range(nb))
        carry = step(qi, init, True)
        carry = lax.fori_loop(0, qi, lambda kb, c: step(kb, c, False), carry)
        for b in range(nb):
            m, acc = carry[b]
            l = acc[:, 0:1]
            o_ref[:, _lanes(b)] = (acc / l).astype(BF16)
            lse_ref[:, _lanes(b)] = jnp.broadcast_to(m * (1.0 / LOG2_E) + jnp.log(l), (TQ, HEAD_PAD))

    qspec, kspec = _attn_specs(s, nb)
    return _with_rider(
        body, rider, name=name, grid=(width // (nb * HEAD_PAD), s // TQ),
        in_specs=[qspec, kspec, kspec], out_specs=[qspec, qspec],
        out_shape=[jax.ShapeDtypeStruct((s, width), BF16), jax.ShapeDtypeStruct((s, width), F32)],
        args=(q, k, v), sem=("parallel", "arbitrary"))


def _mla_bwd_call(q, k, v, o, do, lse, name, rider=None):
    s, width = q.shape
    scale = 1.0 / math.sqrt(MLA_QK)
    nb = MLA_BWD_BLOCKS

    def body(q_ref, k_ref, v_ref, o_ref, do_ref, lse_ref, dq_ref, dk_ref, dv_ref):
        qi = pl.program_id(1)

        @pl.when(qi == 0)
        def _():
            dk_ref[...] = jnp.zeros_like(dk_ref)
            dv_ref[...] = jnp.zeros_like(dv_ref)

        qs = [q_ref[:, _lanes(b)] for b in range(nb)]
        dos = [do_ref[:, _lanes(b)] for b in range(nb)]
        lses = [lse_ref[:, b * HEAD_PAD:b * HEAD_PAD + 1] for b in range(nb)]
        dlts = [jnp.sum(dos[b].astype(F32) * o_ref[:, _lanes(b)].astype(F32), axis=-1, keepdims=True) for b in range(nb)]
        causal = _tri(lambda r, c: c <= r)

        def step(kb, dqs, diag):
            ks = pl.multiple_of(kb * TQ, TQ)
            heads = range(nb)
            kts = [k_ref[pl.ds(ks, TQ), _lanes(b)] for b in heads]
            scs = [_dot_nt(qs[b], kts[b]) for b in heads]
            dps = [_dot_nt(dos[b], v_ref[pl.ds(ks, TQ), _lanes(b)]) for b in heads]
            ps = [jnp.exp(scs[b] * scale - lses[b]) for b in heads]
            if diag:
                ps = [jnp.where(causal, p, 0.0) for p in ps]
            dss = [(ps[b] * (dps[b] - dlts[b]) * scale).astype(BF16) for b in heads]
            dvs = [_dot_tn(ps[b].astype(BF16), dos[b]) for b in heads]
            dks = [_dot_tn(dss[b], qs[b]) for b in heads]
            out = tuple(dqs[b] + _dot(dss[b], kts[b]) for b in heads)
            for b in heads:
                dv_ref[pl.ds(ks, TQ), _lanes(b)] += dvs[b]
                dk_ref[pl.ds(ks, TQ), _lanes(b)] += dks[b]
            return out

        dqs = step(qi, tuple(jnp.zeros((TQ, HEAD_PAD), F32) for _ in range(nb)), True)
        dqs = lax.fori_loop(0, qi, lambda kb, c: step(kb, c, False), dqs)
        for b in range(nb):
            dq_ref[:, _lanes(b)] = dqs[b]

    qspec, kspec = _attn_specs(s, nb)
    out = jax.ShapeDtypeStruct((s, width), F32)
    return _with_rider(
        body, rider, name=name, grid=(width // (nb * HEAD_PAD), s // TQ),
        in_specs=[qspec, kspec, kspec, qspec, qspec, qspec], out_specs=[qspec, kspec, kspec],
        out_shape=[out, out, out], args=(q, k, v, o, do, lse), sem=("parallel", "arbitrary"))


def _dot_hilo(x, u):
    hi = x.astype(BF16)
    lo = (x - hi.astype(F32)).astype(BF16)
    return _dot(hi, u) + _dot(lo, u)


def _sb_logs(z):
    ls = jnp.minimum(z, 0.0) - jnp.log(1.0 + jnp.exp(-jnp.abs(z)))
    return ls, ls - z


def _sb_head_q(qb, first, hh):
    keep = first if hh == 0 else jnp.logical_not(first)
    return jnp.where(keep, qb, jnp.zeros_like(qb)) * jnp.asarray(SB_SCALE, qb.dtype)


def _sb_fwd_call(q, k, v, name):
    s, width = q.shape
    nb = SB_FWD_BLOCKS
    chains = [(b, hh) for b in range(nb) for hh in range(HEAD_PAD // SB_HEAD)]

    def body(q_ref, k_ref, v_ref, o_ref):
        qi = pl.program_id(1)
        strict = _tri(lambda r, c: c < r)
        after = _tri(lambda r, c: r > c).astype(BF16)
        first = _lane((1, HEAD_PAD)) < SB_HEAD
        qhs = [_sb_head_q(q_ref[:, _lanes(b)], first, hh) for b, hh in chains]

        def step(kb, carry, diag):
            ks = pl.multiple_of(kb * TQ, TQ)
            ids = range(len(chains))
            zs = [_dot_nt(qhs[ci], k_ref[pl.ds(ks, TQ), _lanes(chains[ci][0])]) for ci in ids]
            logs = [_sb_logs(z) for z in zs]
            lss = [lg[0] for lg in logs]
            l1ms = [jnp.where(strict, lg[1], 0.0) if diag else lg[1] for lg in logs]
            sufs = [_dot_hilo(l1m, after) for l1m in l1ms]
            as_ = [jnp.exp(lss[ci] + sufs[ci] + carry[ci][0]) for ci in ids]
            if diag:
                as_ = [jnp.where(strict, a, 0.0) for a in as_]
            accs = [carry[ci][1] + _dot(as_[ci].astype(BF16), v_ref[pl.ds(ks, TQ), _lanes(chains[ci][0])]) for ci in ids]
            return tuple((carry[ci][0] + jnp.sum(l1ms[ci], axis=-1, keepdims=True), accs[ci]) for ci in ids)

        init = tuple((jnp.zeros((TQ, 1), F32), jnp.zeros((TQ, HEAD_PAD), F32)) for _ in chains)
        carry = _sb_sweep(step, qi, init)
        for b in range(nb):
            o_ref[:, _lanes(b)] = jnp.where(first, carry[2 * b][1], carry[2 * b + 1][1])

    qspec, kspec = _attn_specs(s, nb)
    return pl.pallas_call(
        body, name=name, grid=(width // (nb * HEAD_PAD), s // TQ),
        in_specs=[qspec, kspec, kspec], out_specs=qspec, out_shape=jax.ShapeDtypeStruct((s, width), F32),
        compiler_params=_cp(("parallel", "arbitrary")))(q, k, v)


def _sb_sweep(step, qi, init):
    def live(carry):
        top = carry[0][0]
        for c in carry[1:]:
            top = jnp.maximum(top, c[0])
        return jnp.max(top)

    carry = step(qi, init, True)

    def cond(state):
        j, alive, _ = state
        return jnp.logical_and(j < qi, alive > SB_DEAD)

    def body(state):
        j, _, carry = state
        carry = step(qi - 1 - j, carry, False)
        return j + 1, live(carry), carry

    return lax.while_loop(cond, body, (jnp.int32(0), live(carry), carry))[2]


def _sb_bwd_call(q, k, v, do, o, name):
    s, width = q.shape
    nb = SB_BWD_BLOCKS
    chains = [(b, hh) for b in range(nb) for hh in range(HEAD_PAD // SB_HEAD)]

    def body(q_ref, k_ref, v_ref, do_ref, o_ref, dq_ref, dk_ref, dv_ref):
        qi = pl.program_id(1)

        @pl.when(qi == 0)
        def _():
            dk_ref[...] = jnp.zeros_like(dk_ref)
            dv_ref[...] = jnp.zeros_like(dv_ref)

        strict = _tri(lambda r, c: c < r)
        after = _tri(lambda r, c: r > c).astype(BF16)
        from_here = _tri(lambda r, c: r >= c).astype(BF16)
        first = _lane((1, HEAD_PAD)) < SB_HEAD
        qhs = [_sb_head_q(q_ref[:, _lanes(b)], first, hh) for b, hh in chains]
        dohs = []
        for b, hh in chains:
            dob = do_ref[:, _lanes(b)]
            dohs.append(jnp.where(first if hh == 0 else jnp.logical_not(first), dob, jnp.zeros_like(dob)))
        gtots = [jnp.sum(dohs[ci].astype(F32) * o_ref[:, _lanes(chains[ci][0])], axis=-1, keepdims=True)
                 for ci in range(len(chains))]

        def step(kb, carry, diag):
            ks = pl.multiple_of(kb * TQ, TQ)
            ids = range(len(chains))
            kts = [k_ref[pl.ds(ks, TQ), _lanes(b)] for b, _ in chains]
            zs = [_dot_nt(qhs[ci], kts[ci]) for ci in ids]
            das = [_dot_nt(dohs[ci], v_ref[pl.ds(ks, TQ), _lanes(chains[ci][0])]) for ci in ids]
            logs = [_sb_logs(z) for z in zs]
            lss = [lg[0] for lg in logs]
            l1ms = [jnp.where(strict, lg[1], 0.0) if diag else lg[1] for lg in logs]
            sufs = [_dot_hilo(l1m, after) for l1m in l1ms]
            as_ = [jnp.exp(lss[ci] + sufs[ci] + carry[ci][0]) for ci in ids]
            if diag:
                as_ = [jnp.where(strict, a, 0.0) for a in as_]
            abs_ = [a.astype(BF16) for a in as_]
            gs = [abs_[ci].astype(F32) * das[ci] for ci in ids]
            cexs = [gtots[ci] - (carry[ci][1] + _dot_hilo(gs[ci], from_here)) for ci in ids]
            dzs = [gs[ci] - jnp.exp(lss[ci]) * (gs[ci] + cexs[ci]) for ci in ids]
            if diag:
                dzs = [jnp.where(strict, dz, 0.0) for dz in dzs]
            dzbs = [dz.astype(BF16) for dz in dzs]
            dvps = [_dot_tn(abs_[ci], dohs[ci]) for ci in ids]
            dkps = [_dot_tn(dzbs[ci], qhs[ci]) for ci in ids]
            out = tuple((carry[ci][0] + jnp.sum(l1ms[ci], axis=-1, keepdims=True),
                         carry[ci][1] + jnp.sum(gs[ci], axis=-1, keepdims=True),
                         carry[ci][2] + _dot(dzbs[ci], kts[ci])) for ci in ids)
            for b in range(nb):
                dk_ref[pl.ds(ks, TQ), _lanes(b)] += dkps[2 * b] + dkps[2 * b + 1]
                dv_ref[pl.ds(ks, TQ), _lanes(b)] += dvps[2 * b] + dvps[2 * b + 1]
            return out

        init = tuple((jnp.zeros((TQ, 1), F32), jnp.zeros((TQ, 1), F32), jnp.zeros((TQ, HEAD_PAD), F32)) for _ in chains)
        carry = _sb_sweep(step, qi, init)
        for b in range(nb):
            dq_ref[:, _lanes(b)] = (jnp.where(first, carry[2 * b][2], carry[2 * b + 1][2]) * SB_SCALE).astype(BF16)

    qspec, kspec = _attn_specs(s, nb)
    return pl.pallas_call(
        body, name=name, grid=(width // (nb * HEAD_PAD), s // TQ),
        in_specs=[qspec, kspec, kspec, qspec, qspec], out_specs=[qspec, kspec, kspec],
        out_shape=[jax.ShapeDtypeStruct((s, width), BF16), jax.ShapeDtypeStruct((s, width), F32),
                   jax.ShapeDtypeStruct((s, width), F32)],
        compiler_params=_cp(("parallel", "arbitrary")))(q, k, v, do, o)


def _merge_out_call(om, osb, gates, h, wbm, wbs, wo, gain, name):
    s, d = h.shape
    tm = min(TM_SMALL, s)

    def body(om_ref, os_ref, g_ref, h_ref, wbm_ref, wbs_ref, wo_ref, gain_ref, h2_ref, bm_ref, bs_ref, mg_ref, u_ref):
        bm = _dot(om_ref[...], wbm_ref[...])
        bs = _dot(os_ref[...].astype(BF16), wbs_ref[...])
        mg = (_sigmoid(g_ref[:, :d]) * bm + _sigmoid(g_ref[:, d:]) * bs).astype(BF16)
        bm_ref[...] = bm
        bs_ref[...] = bs
        mg_ref[...] = mg
        x = h_ref[...] + _dot(mg, wo_ref[...])
        h2_ref[...] = x
        u_ref[...] = ((x * _rstd(x, d)) * gain_ref[...]).astype(BF16)

    return pl.pallas_call(
        body, name=name, grid=(s // tm,),
        in_specs=[_rows(tm, om.shape[1]), _rows(tm, SB_WIDTH), _rows(tm, 2 * d), _rows(tm, d),
                  _whole(wbm.shape), _whole(wbs.shape), _whole(wo.shape), _whole((1, d))],
        out_specs=[_rows(tm, d)] * 5,
        out_shape=[jax.ShapeDtypeStruct((s, d), F32), jax.ShapeDtypeStruct((s, d), F32),
                   jax.ShapeDtypeStruct((s, d), F32), jax.ShapeDtypeStruct((s, d), BF16),
                   jax.ShapeDtypeStruct((s, d), BF16)],
        compiler_params=_cp(("parallel",)))(om, osb, gates, h, wbm, wbs, wo, gain)


def _ple_call(h, g, wg, p, wp, tgt, name):
    s, d = h.shape
    tm = min(TM_SMALL, s)

    def body(h_ref, g_ref, wg_ref, p_ref, wp_ref, t_ref, dh_ref, dhs_ref, un_ref, dgl_ref, dpp_ref, dg_ref, sq_ref):
        @pl.when(pl.program_id(0) == 0)
        def _():
            dg_ref[...] = jnp.zeros_like(dg_ref)
            sq_ref[...] = jnp.zeros_like(sq_ref)

        x = h_ref[...]
        gain = g_ref[...]
        r = _rstd(x, d)
        xh = x * r
        un = (xh * gain).astype(BF16)
        sg = _sigmoid(_dot(un, wg_ref[...]))
        pp = _dot(p_ref[...].astype(BF16), wp_ref[...])
        diff = (x + sg * pp) - t_ref[...]
        sq_ref[...] += jnp.sum(diff * diff, axis=0, keepdims=True)
        dy = diff * (1.0 / d)
        dgl = ((dy * pp) * (sg * (1.0 - sg))).astype(BF16)
        dun = _dot_nt(dgl, wg_ref[...])
        dg_ref[...] += jnp.sum(dun * xh, axis=0, keepdims=True)
        dh = dy + _rms_bwd(x, r, gain, dun, d)
        dh_ref[...] = dh
        dhs_ref[...] = (0.5 * dh).astype(BF16)
        un_ref[...] = un
        dgl_ref[...] = dgl
        dpp_ref[...] = (dy * sg).astype(BF16)

    bf = jax.ShapeDtypeStruct((s, d), BF16)
    vec = jax.ShapeDtypeStruct((1, d), F32)
    return pl.pallas_call(
        body, name=name, grid=(s // tm,),
        in_specs=[_rows(tm, d), _whole((1, d)), _whole(wg.shape), _rows(tm, PLE_DIM), _whole(wp.shape), _rows(tm, d)],
        out_specs=[_rows(tm, d)] * 5 + [_whole((1, d))] * 2,
        out_shape=[jax.ShapeDtypeStruct((s, d), F32), bf, bf, bf, bf, vec, vec],
        compiler_params=_cp(("arbitrary",)))(h, g, wg, p, wp, tgt)


def _ffn_bwd_a_call(dhs, a, b, wo, name, rider=None):
    s, n = a.shape
    d = dhs.shape[1]
    tn = n // 2
    tm = min(TM, s)

    def body(dh_ref, a_ref, b_ref, wo_ref, da_ref, db_ref):
        dh = dh_ref[...]
        chunks = [slice(c0, min(c0 + COL_CHUNK, tn)) for c0 in range(0, tn, COL_CHUNK)]
        dhms = [_dot_nt(dh, wo_ref[sl, :]) for sl in chunks]
        for sl, dhm in zip(chunks, dhms):
            av = a_ref[:, sl]
            sa = _sigmoid(av)
            da_ref[:, sl] = (dhm * b_ref[:, sl] * (sa * (1.0 + av * (1.0 - sa)))).astype(BF16)
            db_ref[:, sl] = (dhm * (av * sa)).astype(BF16)

    blk = pl.BlockSpec((tm, tn), lambda j, i: (i, j))
    return _with_rider(
        body, rider, name=name, grid=(n // tn, s // tm),
        in_specs=[pl.BlockSpec((tm, d), lambda j, i: (i, 0)), blk, blk, pl.BlockSpec((tn, d), lambda j, i: (j, 0))],
        out_specs=[blk, blk],
        out_shape=[jax.ShapeDtypeStruct((s, n), BF16)] * 2, args=(dhs, a, b, wo), sem=("parallel", "parallel"))


def _norm_bwd_call(dy_list, w_list, h, g, dh_in, name, half_out, rider=None):
    s, d = h.shape
    tm = min(TM_SMALL, s)
    nk, nw = len(dy_list), len(w_list)
    factor = 0.5 if half_out else 1.0
    sharded = nw == 1 and w_list[0].ndim == 3

    def body(*refs):
        dy_refs = refs[:nk]
        w_refs = refs[nk:nk + nw]
        h_ref, g_ref, dhin_ref, dh_ref, dhb_ref, dg_ref = refs[nk + nw:]

        @pl.when(pl.program_id(0) == 0)
        def _():
            dg_ref[...] = jnp.zeros_like(dg_ref)

        if sharded:
            c = w_list[0].shape[2]
            per = dy_list[0].shape[1] // c
            du = None
            for k in range(w_list[0].shape[0]):
                part = _dot_nt(dy_refs[k // per][:, (k % per) * c:(k % per + 1) * c], w_refs[0][k])
                du = part if du is None else du + part
        else:
            du = _dot_nt(dy_refs[0][...], w_refs[0][...])
            for dy_ref, w_ref in zip(dy_refs[1:], w_refs[1:]):
                du = du + _dot_nt(dy_ref[...], w_ref[...])
        x = h_ref[...]
        r = _rstd(x, d)
        dg_ref[...] += jnp.sum(du * (x * r), axis=0, keepdims=True)
        dh = dhin_ref[...] + _rms_bwd(x, r, g_ref[...], du, d)
        dh_ref[...] = dh
        dhb_ref[...] = (factor * dh).astype(BF16)

    outs, got = _with_rider(
        body, rider, name=name, grid=(s // tm,),
        in_specs=[_rows(tm, dy.shape[1]) for dy in dy_list] + [_whole(w.shape) for w in w_list]
        + [_rows(tm, d), _whole((1, d)), _rows(tm, d)],
        out_specs=[_rows(tm, d), _rows(tm, d), _whole((1, d))],
        out_shape=[jax.ShapeDtypeStruct((s, d), F32), jax.ShapeDtypeStruct((s, d), BF16),
                   jax.ShapeDtypeStruct((1, d), F32)],
        args=(*dy_list, *w_list, h, g, dh_in), sem=("arbitrary",))
    return outs if rider is None else (outs, got)


def _merge_bwd_call(dhb, gates, bm, bs, wo, wbm, wbs, name):
    s, d = bm.shape
    tm = min(TM_SMALL, s)

    def body(dh_ref, g_ref, bm_ref, bs_ref, wo_ref, wbm_ref, wbs_ref, dg_ref, dbm_ref, dbs_ref, dom_ref, dos_ref):
        dmg = _dot_nt(dh_ref[...], wo_ref[...])
        s1 = _sigmoid(g_ref[:, :d])
        s2 = _sigmoid(g_ref[:, d:])
        dg_ref[:, :d] = (dmg * bm_ref[...] * (s1 * (1.0 - s1))).astype(BF16)
        dg_ref[:, d:] = (dmg * bs_ref[...] * (s2 * (1.0 - s2))).astype(BF16)
        dbm = (dmg * s1).astype(BF16)
        dbs = (dmg * s2).astype(BF16)
        dbm_ref[...] = dbm
        dbs_ref[...] = dbs
        dom_ref[...] = _dot_nt(dbm, wbm_ref[...]).astype(BF16)
        dos_ref[...] = _dot_nt(dbs, wbs_ref[...]).astype(BF16)

    wm = wbm.shape[0]
    return pl.pallas_call(
        body, name=name, grid=(s // tm,),
        in_specs=[_rows(tm, d), _rows(tm, 2 * d), _rows(tm, d), _rows(tm, d),
                  _whole(wo.shape), _whole(wbm.shape), _whole(wbs.shape)],
        out_specs=[_rows(tm, 2 * d), _rows(tm, d), _rows(tm, d), _rows(tm, wm), _rows(tm, SB_WIDTH)],
        out_shape=[jax.ShapeDtypeStruct((s, 2 * d), BF16), jax.ShapeDtypeStruct((s, d), BF16),
                   jax.ShapeDtypeStruct((s, d), BF16), jax.ShapeDtypeStruct((s, wm), BF16),
                   jax.ShapeDtypeStruct((s, SB_WIDTH), BF16)],
        compiler_params=_cp(("parallel",)))(dhb, gates, bm, bs, wo, wbm, wbs)


def _mla_prep_bwd_call(cq, ckv, krope, pos, freq, sign, g_ql, g_kvl, g_qh, g_kh, wq, wkv, dq, dk, dv, name):
    s = cq.shape[0]
    tm = min(TM_SMALL, s)
    width = HEADS * HEAD_PAD

    def body(cq_ref, ckv_ref, kr_ref, pos_ref, freq_ref, sign_ref, gql_ref, gkvl_ref, gqh_ref, gkh_ref,
             wq_ref, wkv_ref, dq_ref, dk_ref, dv_ref,
             dcq_ref, dckv_ref, dkr_ref, dwq_ref, dwkv_ref, dgql_ref, dgkvl_ref, dgqh_ref, dgkh_ref, dqr_ref, dkv_ref):
        @pl.when(pl.program_id(0) == 0)
        def _():
            for ref in (dwq_ref, dwkv_ref, dgql_ref, dgkvl_ref, dgqh_ref, dgkh_ref):
                ref[...] = jnp.zeros_like(ref)

        cosv, ssv = _rope_tables(pos_ref, freq_ref, sign_ref)
        xq = cq_ref[...]
        rq = _rstd(xq, Q_LORA)
        cqn = ((xq * rq) * gql_ref[...]).astype(BF16)
        qr = _dot(cqn, wq_ref[...])
        xk = ckv_ref[...]
        rk = _rstd(xk, KV_LORA)
        ckvn = ((xk * rk) * gkvl_ref[...]).astype(BF16)
        kv = _dot(ckvn, wkv_ref[...])
        kr = kr_ref[...]
        lane = _lane((tm, HEAD_PAD))
        dkr = jnp.zeros((tm, HEAD_PAD), F32)
        dgqh = jnp.zeros((1, HEAD_PAD), F32)
        dgkh = jnp.zeros((1, HEAD_PAD), F32)
        for h in range(HEADS):
            sl = slice(h * HEAD_PAD, (h + 1) * HEAD_PAD)
            x = qr[:, sl]
            dx, dgh = _head_bwd(x, _rstd(x, MLA_QK), gqh_ref[...], cosv, ssv, dq_ref[:, sl])
            dqr_ref[:, sl] = dx.astype(BF16)
            dgqh = dgqh + dgh
            x = jnp.where(lane < MLA_NOPE, kv[:, sl], kr)
            dx, dgh = _head_bwd(x, _rstd(x, MLA_QK), gkh_ref[...], cosv, ssv, dk_ref[:, sl])
            dgkh = dgkh + dgh
            dkr = dkr + jnp.where(lane >= MLA_NOPE, dx, 0.0)
            dkv_ref[:, sl] = jnp.where(lane < MLA_NOPE, dx, dv_ref[:, sl]).astype(BF16)
        dgqh_ref[...] += dgqh
        dgkh_ref[...] += dgkh
        dkr_ref[...] = dkr.astype(BF16)
        dqr = dqr_ref[...]
        dkvb = dkv_ref[...]
        dwq_ref[...] += _dot_tn(cqn, dqr)
        dwkv_ref[...] += _dot_tn(ckvn, dkvb)
        dcqn = _dot_nt(dqr, wq_ref[...])
        dgql_ref[...] += jnp.sum(dcqn * (xq * rq), axis=0, keepdims=True)
        dcq_ref[...] = _rms_bwd(xq, rq, gql_ref[...], dcqn, Q_LORA).astype(BF16)
        dckvn = _dot_nt(dkvb, wkv_ref[...])
        dgkvl_ref[...] += jnp.sum(dckvn * (xk * rk), axis=0, keepdims=True)
        dckv_ref[...] = _rms_bwd(xk, rk, gkvl_ref[...], dckvn, KV_LORA).astype(BF16)

    vec = lambda n: jax.ShapeDtypeStruct((1, n), F32)
    outs = pl.pallas_call(
        body, name=name, grid=(s // tm,),
        in_specs=[_rows(tm, Q_LORA), _rows(tm, KV_LORA), _rows(tm, HEAD_PAD), _rows(tm, 1),
                  _whole((1, HEAD_PAD)), _whole((1, HEAD_PAD)), _whole((1, Q_LORA)), _whole((1, KV_LORA)),
                  _whole((1, HEAD_PAD)), _whole((1, HEAD_PAD)), _whole((Q_LORA, width)), _whole((KV_LORA, width)),
                  _rows(tm, width), _rows(tm, width), _rows(tm, width)],
        out_specs=[_rows(tm, Q_LORA), _rows(tm, KV_LORA), _rows(tm, HEAD_PAD), _whole((Q_LORA, width)),
                   _whole((KV_LORA, width)), _whole((1, Q_LORA)), _whole((1, KV_LORA)), _whole((1, HEAD_PAD)),
                   _whole((1, HEAD_PAD)), _rows(tm, width), _rows(tm, width)],
        out_shape=[jax.ShapeDtypeStruct((s, Q_LORA), BF16), jax.ShapeDtypeStruct((s, KV_LORA), BF16),
                   jax.ShapeDtypeStruct((s, HEAD_PAD), BF16), jax.ShapeDtypeStruct((Q_LORA, width), F32),
                   jax.ShapeDtypeStruct((KV_LORA, width), F32), vec(Q_LORA), vec(KV_LORA), vec(HEAD_PAD), vec(HEAD_PAD),
                   jax.ShapeDtypeStruct((s, width), BF16), jax.ShapeDtypeStruct((s, width), BF16)],
        compiler_params=_cp(("arbitrary",)))(cq, ckv, krope, pos, freq, sign, g_ql, g_kvl, g_qh, g_kh, wq, wkv, dq, dk, dv)
    return outs[:9]


def _tn_call(a, b, name, shard_cols=None, rider=None):
    s, ka = a.shape
    nb = b.shape[1]
    ti = _pick(ka, (512, 256, 128))
    if shard_cols is not None:
        tj = shard_cols
    else:
        tj = nb if nb <= TN_MAX_COLS else _pick(nb, (2176, 1024, 512, 256, 128))
    ts = s if 2 * s * (ti + tj) * a.dtype.itemsize <= TN_OPERAND_BYTES else s // 2
    ns = s // ts

    def body(a_ref, b_ref, o_ref, acc_ref):
        part = _dot_tn(a_ref[...].astype(BF16), b_ref[...].astype(BF16))
        if ns == 1:
            o_ref[...] = part.astype(o_ref.dtype)
            return

        @pl.when(pl.program_id(2) == 0)
        def _():
            acc_ref[...] = part

        @pl.when(pl.program_id(2) != 0)
        def _():
            acc_ref[...] += part

        @pl.when(pl.program_id(2) == ns - 1)
        def _():
            o_ref[...] = acc_ref[...].astype(o_ref.dtype)

    if shard_cols is None:
        out_spec = pl.BlockSpec((ti, tj), lambda i, j, t: (i, j))
        out_shape = jax.ShapeDtypeStruct((ka, nb), BF16)
    else:
        out_spec = pl.BlockSpec((None, ti, tj), lambda i, j, t: (j, i, 0))
        out_shape = jax.ShapeDtypeStruct((nb // tj, ka, tj), BF16)
    (out,), got = _with_rider(
        body, rider, name=name, grid=(ka // ti, nb // tj, ns),
        in_specs=[pl.BlockSpec((ts, ti), lambda i, j, t: (t, i)), pl.BlockSpec((ts, tj), lambda i, j, t: (t, j))],
        out_specs=[out_spec], out_shape=[out_shape], scratch=[pltpu.VMEM((ti, tj), F32)], args=(a, b),
        sem=("parallel", "parallel", "arbitrary"))
    return out if rider is None else (out, got)


def _sum_call(parts, out_dtype, name):
    n, r, w = parts.shape
    tr = _row_tile(r)

    def body(p_ref, o_ref):
        acc = p_ref[0].astype(F32)
        for k in range(1, n):
            acc = acc + p_ref[k].astype(F32)
        o_ref[...] = acc.astype(out_dtype)

    return pl.pallas_call(
        body, name=name, grid=(r // tr,),
        in_specs=[pl.BlockSpec((n, tr, w), lambda i: (0, i, 0))], out_specs=_rows(tr, w),
        out_shape=jax.ShapeDtypeStruct((r, w), out_dtype), compiler_params=_cp(("parallel",)))(parts)


def _chip_sum_call(by_chip, core, name):
    n, r, w = by_chip.shape
    tr = _row_tile(r)
    nblk = r // tr

    def body(c_ref, p_ref, o_ref):
        acc = p_ref[0].astype(F32)
        for k in range(1, n):
            acc = acc + p_ref[k].astype(F32)
        o_ref[...] = acc

    return pl.pallas_call(
        body, name=name,
        grid_spec=pltpu.PrefetchScalarGridSpec(
            num_scalar_prefetch=1, grid=(nblk,),
            in_specs=[pl.BlockSpec((n, tr, w), lambda i, c_ref: (0, i, 0))],
            out_specs=pl.BlockSpec((tr, w), lambda i, c_ref: (c_ref[0] * nblk + i, 0))),
        out_shape=jax.ShapeDtypeStruct((2 * r, w), F32),
        compiler_params=_cp(("parallel",)))(core.reshape(1).astype(jnp.int32), by_chip)


def _pair_sum_call(full, other, core, out_dtype, name):
    n, r, w = other.shape
    tr = _row_tile(r)
    nblk = r // tr

    def body(c_ref, a_ref, b_ref, o_ref):
        o_ref[...] = (a_ref[...].astype(F32) + b_ref[...].astype(F32)).astype(out_dtype)

    spec = pl.BlockSpec((None, tr, w), lambda k, i, c_ref: (k, i, 0))
    return pl.pallas_call(
        body, name=name,
        grid_spec=pltpu.PrefetchScalarGridSpec(
            num_scalar_prefetch=1, grid=(n, nblk),
            in_specs=[pl.BlockSpec((None, tr, w), lambda k, i, c_ref: (k, c_ref[0] * nblk + i, 0)), spec],
            out_specs=spec),
        out_shape=jax.ShapeDtypeStruct((n, r, w), out_dtype),
        compiler_params=_cp(("parallel", "parallel")))(core.reshape(1).astype(jnp.int32), full, other)


def _adamw_call(w, g, row0, m, v, name):
    r, c = w.shape
    tr = _pick(math.gcd(r, row0) if row0 else r, (256, 128, 64, 32, 16, 8))
    off = row0 // tr

    def body(w_ref, g_ref, m_ref, v_ref, g_out_ref, d_ref, nm_ref, nv_ref):
        gg = g_ref[...]
        g_out_ref[...] = gg
        nm = ADAM_B1 * m_ref[...] + (1.0 - ADAM_B1) * gg
        nv = ADAM_B2 * v_ref[...] + (1.0 - ADAM_B2) * (gg * gg)
        m_hat = nm / (1.0 - ADAM_B1 ** ADAM_STEP)
        v_hat = nv / (1.0 - ADAM_B2 ** ADAM_STEP)
        d_ref[...] = -ADAM_LR * (m_hat / (jnp.sqrt(v_hat) + ADAM_EPS) + ADAM_WD * w_ref[...])
        nm_ref[...] = nm
        nv_ref[...] = nv

    out = jax.ShapeDtypeStruct((r, c), F32)
    g_spec = pl.BlockSpec((tr, c), lambda i: (off + i, 0))
    return pl.pallas_call(
        body, name=name, grid=(r // tr,), in_specs=[_rows(tr, c), g_spec, _rows(tr, c), _rows(tr, c)],
        out_specs=[_rows(tr, c)] * 4, out_shape=[out, out, out, out], compiler_params=_cp(("parallel",)))(w, g, m, v)


def _position():
    x, y, c = lax.axis_index("x"), lax.axis_index("y"), lax.axis_index("c")
    chips = [(1 - x, y), (x, 1 - y), (1 - x, 1 - y)]
    return x, y, c, chips


def _gather_rider(parts):
    n = len(parts)
    pairs = [(j, k) for j in range(3) for k in range(n)]

    def piece(out_refs, k, chip, core):
        half = parts[k].shape[0] // 2
        return out_refs[k].at[2 * chip[0] + chip[1], pl.ds(core * half, half), :]

    def over_ici(in_refs, out_refs, sems, j, k):
        x, y, c, chips = _position()
        half = parts[k].shape[0] // 2
        return pltpu.make_async_remote_copy(
            src_ref=in_refs[k].at[pl.ds(c * half, half), :], dst_ref=piece(out_refs, k, (x, y), c),
            send_sem=sems[0].at[n * j + k], recv_sem=sems[1].at[n * j + k], device_id=(*chips[j], c), device_id_type=MESH)

    def to_sibling(out_refs, sems, j, k):
        x, y, c, chips = _position()
        landed = piece(out_refs, k, chips[j], c)
        return pltpu.make_async_remote_copy(
            src_ref=landed, dst_ref=landed, send_sem=sems[2].at[n * j + k], recv_sem=sems[3].at[n * j + k],
            device_id=(x, y, 1 - c), device_id_type=MESH)

    def start(in_refs, out_refs, sems):
        for j, k in pairs:
            over_ici(in_refs, out_refs, sems, j, k).start()

    def finish(in_refs, out_refs, sems):
        for j, k in pairs:
            over_ici(in_refs, out_refs, sems, j, k).wait_recv()
            to_sibling(out_refs, sems, j, k).start()
        for j, k in pairs:
            to_sibling(out_refs, sems, j, k).wait_recv()
        for j, k in pairs:
            over_ici(in_refs, out_refs, sems, j, k).wait_send()
            to_sibling(out_refs, sems, j, k).wait_send()

    return _Rider(list(parts), [jax.ShapeDtypeStruct((N_CHIPS,) + p.shape, p.dtype) for p in parts], [3 * n] * 4,
                  start, finish)


def _scatter_rider(parts):
    n = len(parts)
    pairs = [(j, k) for j in range(3) for k in range(n)]

    def copy(in_refs, out_refs, sems, j, k):
        x, y, c, chips = _position()
        return pltpu.make_async_remote_copy(
            src_ref=in_refs[k].at[2 * chips[j][0] + chips[j][1]], dst_ref=out_refs[k].at[2 * x + y],
            send_sem=sems[0].at[n * j + k], recv_sem=sems[1].at[n * j + k], device_id=(*chips[j], c), device_id_type=MESH)

    def start(in_refs, out_refs, sems):
        for j, k in pairs:
            copy(in_refs, out_refs, sems, j, k).start()

    def finish(in_refs, out_refs, sems):
        for j, k in pairs:
            copy(in_refs, out_refs, sems, j, k).wait()

    return _Rider(list(parts), [jax.ShapeDtypeStruct(p.shape, p.dtype) for p in parts], [3 * n] * 2, start, finish)


def _exchange_call(rider, name):
    n, m = len(rider.ins), len(rider.out_shape)

    def body(*refs):
        rider.start(refs[:n], refs[n:n + m], refs[n + m:])
        rider.finish(refs[:n], refs[n:n + m], refs[n + m:])

    return pl.pallas_call(
        body, name=name, in_specs=[HBM] * n, out_specs=[HBM] * m, out_shape=rider.out_shape,
        scratch_shapes=[pltpu.SemaphoreType.DMA((k,)) for k in rider.sems])(*rider.ins)


def _pair_send_call(parts, name):
    n = len(parts)

    def body(*refs):
        in_refs, out_refs = refs[:n], refs[n:2 * n]
        send_sems, recv_sems = refs[2 * n:]
        x, y, c, _ = _position()
        copies = []
        for k in range(n):
            half = parts[k].shape[1] // 2
            cp = pltpu.make_async_remote_copy(
                src_ref=in_refs[k].at[:, pl.ds((1 - c) * half, half), :], dst_ref=out_refs[k],
                send_sem=send_sems.at[k], recv_sem=recv_sems.at[k], device_id=(x, y, 1 - c), device_id_type=MESH)
            cp.start()
            copies.append(cp)
        for cp in copies:
            cp.wait()

    sems = pltpu.SemaphoreType.DMA((n,))
    return pl.pallas_call(
        body, name=name, in_specs=[HBM] * n, out_specs=[HBM] * n,
        out_shape=[jax.ShapeDtypeStruct((p.shape[0], p.shape[1] // 2, p.shape[2]), p.dtype) for p in parts],
        scratch_shapes=[sems, sems])(*parts)


def _pair_swap_call(parts, name):
    n = len(parts)

    def body(*refs):
        out_refs = refs[n:2 * n]
        send_sems, recv_sems = refs[2 * n:]
        x, y, c, _ = _position()
        copies = []
        for k in range(n):
            half = parts[k].shape[0] // 2
            mine = out_refs[k].at[pl.ds(c * half, half), :]
            cp = pltpu.make_async_remote_copy(
                src_ref=mine, dst_ref=mine, send_sem=send_sems.at[k], recv_sem=recv_sems.at[k],
                device_id=(x, y, 1 - c), device_id_type=MESH)
            cp.start()
            copies.append(cp)
        for cp in copies:
            cp.wait()

    sems = pltpu.SemaphoreType.DMA((n,))
    return pl.pallas_call(
        body, name=name, in_specs=[HBM] * n, out_specs=[HBM] * n,
        out_shape=[jax.ShapeDtypeStruct(p.shape, p.dtype) for p in parts],
        input_output_aliases={k: k for k in range(n)},
        scratch_shapes=[sems, sems])(*parts)


def _all_gather_small_call(block, name):
    r, w = block.shape

    def body(in_ref, out_ref, send_sems, recv_sems, local_sem):
        x, y, c, _ = _position()
        me = 4 * x + 2 * y + c
        own = pltpu.make_async_copy(in_ref, out_ref.at[me], local_sem)
        own.start()
        copies = []
        for k in range(1, 8):
            peer = (x ^ (k >> 2), y ^ ((k >> 1) & 1), c ^ (k & 1))
            cp = pltpu.make_async_remote_copy(
                src_ref=in_ref, dst_ref=out_ref.at[me], send_sem=send_sems.at[k - 1], recv_sem=recv_sems.at[k - 1],
                device_id=peer, device_id_type=MESH)
            cp.start()
            copies.append(cp)
        for cp in copies:
            cp.wait()
        own.wait()

    return pl.pallas_call(
        body, name=name, in_specs=[HBM], out_specs=HBM,
        out_shape=jax.ShapeDtypeStruct((8, r, w), block.dtype),
        scratch_shapes=[pltpu.SemaphoreType.DMA((7,)), pltpu.SemaphoreType.DMA((7,)), pltpu.SemaphoreType.DMA])(block)


BIG = {
    "ffn1_w_in": ((D_MODEL, 2 * D_FF), 1), "ffn1_w_out": ((D_FF, D_MODEL), 0),
    "w_in": ((D_MODEL, 4256), 1), "w_q_up": ((Q_LORA, HEADS * MLA_QK), 1), "w_kv_up": ((KV_LORA, 1024), 1),
    "w_branch_mla": ((512, D_MODEL), 1), "w_branch_sb": ((SB_WIDTH, D_MODEL), 1), "w_out": ((D_MODEL, D_MODEL), 0),
    "ffn2_w_in": ((D_MODEL, 2 * D_FF), 1), "ffn2_w_out": ((D_FF, D_MODEL), 0),
    "w_ple_gate": ((D_MODEL, D_MODEL), 0), "w_ple_proj": ((PLE_DIM, D_MODEL), 1),
}
GAINS = {"ffn1_norm": 1024, "mix_norm": 1024, "q_latent_norm": 384, "kv_latent_norm": 256, "q_head_norm": 96,
         "k_head_norm": 96, "ffn2_norm": 1024, "ple_norm": 1024}
WEIGHT_ORDER = ["ffn1_norm", "ffn1_w_in", "ffn1_w_out", "mix_norm", "w_in", "q_latent_norm", "w_q_up",
                "kv_latent_norm", "w_kv_up", "q_head_norm", "k_head_norm", "w_branch_mla", "w_branch_sb", "w_out",
                "ffn2_norm", "ffn2_w_in", "ffn2_w_out", "ple_norm", "w_ple_gate", "w_ple_proj"]


def _shard_shape(name):
    (r, c), axis = BIG[name]
    return (r // N_CHIPS, c) if axis == 0 else (r, c // N_CHIPS)


GATHER_GROUPS = [
    [("ffn1_w_in",)],
    [("ffn1_w_out",), ("w_in",)],
    [("w_out",), ("w_kv_up", "w_branch_mla", "w_branch_sb"), ("w_q_up",)],
    [("ffn2_w_in",), ("ffn2_w_out", "w_ple_gate"), ("w_ple_proj",)],
]
REDUCE_GROUPS = [
    [("ffn2_w_in",), ("ffn2_w_out", "w_out", "w_ple_gate"), ("w_branch_mla", "w_branch_sb", "w_ple_proj")],
    [("w_in",), ("w_kv_up",), ("w_q_up",)],
    [("ffn1_w_out",)],
    [("ffn1_w_in",)],
]


def _join_parts(shards, group):
    return [shards[part[0]] if len(part) == 1 else jnp.concatenate([shards[n] for n in part], axis=-2) for part in group]


def _part_rows(group):
    where = {}
    for k, part in enumerate(group):
        at = 0
        for n in part:
            where[n] = (k, at)
            at += _shard_shape(n)[0]
    return where


def _split_parts(parts, group):
    return {n: parts[k][..., at:at + _shard_shape(n)[0], :] for n, (k, at) in _part_rows(group).items()}


def _to_shards(name, full):
    (r, c), axis = BIG[name]
    if axis == 0:
        return full.reshape(N_CHIPS, r // N_CHIPS, c)
    return full.reshape(r, N_CHIPS, c // N_CHIPS).transpose(1, 0, 2)


def _from_shards(name, shards):
    (r, c), axis = BIG[name]
    if axis == 0:
        return shards.reshape(r, c)
    return shards.transpose(1, 0, 2).reshape(r, c)


def _relayout_w_in(w):
    d = w.shape[0]
    z = lambda n: jnp.zeros((d, n), w.dtype)
    return jnp.concatenate([w[:, :640], z(MLA_NOPE), w[:, 640:672], z(HEAD_PAD - MLA_QK), w[:, 672:]], axis=1)


def _unlayout_w_in(g):
    return jnp.concatenate([g[:, :640], g[:, 640 + MLA_NOPE:640 + MLA_QK], g[:, 768:]], axis=1)


def _pad_heads(v):
    lead = v.shape[:-1]
    return jnp.pad(v.reshape(lead + (HEADS, MLA_QK)), [(0, 0)] * len(lead) + [(0, 0), (0, HEAD_PAD - MLA_QK)]).reshape(
        lead + (HEADS * HEAD_PAD,))


SHARD_MAJOR = ("ffn1_w_in", "ffn2_w_in")


def _step(x, p, pos, tgt, gains, weights, dist):
    d = D_MODEL
    full = {} if dist is not None else {n: _to_shards(n, w) if n in SHARD_MAJOR else w for n, w in weights.items()}
    reduced = {}

    def gather_rider(g):
        if dist is None:
            return None, None
        mine = _join_parts(weights, GATHER_GROUPS[g])
        return mine, _gather_rider(mine)

    def gathered(g, mine, others):
        if dist is not None:
            parts = [lax.dynamic_update_slice_in_dim(o, m[None], dist[0], axis=0) for o, m in zip(others, mine)]
            for n, shards in _split_parts(parts, GATHER_GROUPS[g]).items():
                full[n] = shards if n in SHARD_MAJOR else _from_shards(n, shards)

    def reduce_before(g):
        if dist is None:
            return None, None
        group = REDUCE_GROUPS[g]
        shards = {n: grads[n] if grads[n].ndim == 3 else _to_shards(n, grads[n].astype(BF16)) for part in group for n in part}
        partial = _join_parts(shards, group)
        from_sibling = _pair_send_call(partial, "grads%d_pair_send" % g)
        pair_sum = [_pair_sum_call(a, b, dist[1], BF16, "grads%d_pair_sum_%d" % (g, k))
                    for k, (a, b) in enumerate(zip(partial, from_sibling))]
        return pair_sum, _scatter_rider(pair_sum)

    def reduce_after(g, pair_sum, by_chip):
        if dist is not None:
            chip, core = dist
            by_chip = [lax.dynamic_update_slice_in_dim(t, lax.dynamic_slice_in_dim(o, chip, 1, axis=0), chip, axis=0)
                       for t, o in zip(by_chip, pair_sum)]
            bufs = _pair_swap_call([_chip_sum_call(t, core, "grads%d_chip_sum_%d" % (g, k)) for k, t in enumerate(by_chip)],
                                   "grads%d_pair_swap" % g)
            for n, (k, row0) in _part_rows(REDUCE_GROUPS[g]).items():
                reduced[n] = (bufs[k], row0)

    mine, rider = gather_rider(0)
    if dist is not None:
        gathered(0, mine, _exchange_call(rider, "gather0"))
    wts = full
    inv_freq = ROPE_BASE ** (-jnp.arange(0, MLA_ROPE, 2, dtype=F32) / MLA_ROPE)
    zeros = lambda n: jnp.zeros((n,), F32)
    freq = jnp.concatenate([zeros(MLA_NOPE), inv_freq, inv_freq, zeros(HEAD_PAD - MLA_QK)])[None]
    sign = jnp.concatenate([zeros(MLA_NOPE), -jnp.ones((16,), F32), jnp.ones((16,), F32), zeros(HEAD_PAD - MLA_QK)])[None]
    pad_gain = lambda g: jnp.pad(g, ((0, 0), (0, HEAD_PAD - MLA_QK)))
    g_qh, g_kh = pad_gain(gains["q_head_norm"]), pad_gain(gains["k_head_norm"])

    u1 = _norm_call(x, gains["ffn1_norm"], "norm_ffn1")
    mine, rider = gather_rider(1)
    (a1, b1, hm1), got = _ffn_in_call(u1, wts["ffn1_w_in"], "ffn1_in", rider)
    gathered(1, mine, got)
    mine, rider = gather_rider(2)
    (h1, um), got = _ffn_out_call(hm1, wts["ffn1_w_out"], x, gains["mix_norm"], "ffn1_out", rider)
    gathered(2, mine, got)
    w_in = _relayout_w_in(wts["w_in"])
    wq = _pad_heads(wts["w_q_up"])
    wkv = wts["w_kv_up"]
    wbm = jnp.pad(wts["w_branch_mla"].reshape(HEADS, 64, d), ((0, 0), (64, 0), (0, 0))).reshape(HEADS * HEAD_PAD, d)
    wbs, wo = wts["w_branch_sb"], wts["w_out"]
    cq, ckv, krope, sbq, sbk, sbv, gates = _mix_in_call(um, w_in, "mix_in")
    prep_args = (cq, ckv, krope, pos, freq, sign, gains["q_latent_norm"], gains["kv_latent_norm"], g_qh, g_kh, wq, wkv)
    q, k, v = _mla_prep_call(*prep_args, "mla_prep")
    mine, rider = gather_rider(3)
    (om, lse), got = _mla_fwd_call(q, k, v, "mla_fwd", rider)
    gathered(3, mine, got)
    osb = _sb_fwd_call(sbq, sbk, sbv, "sb_fwd")
    h2, bm, bs, mg, u2 = _merge_out_call(om, osb, gates, h1, wbm, wbs, wo, gains["ffn2_norm"], "merge_out")
    (a2, b2, hm2), _ = _ffn_in_call(u2, wts["ffn2_w_in"], "ffn2_in")
    (h3, _), _ = _ffn_out_call(hm2, wts["ffn2_w_out"], h2, gains["ple_norm"], "ffn2_out")

    grads, gg = {}, {}
    dh3, dh3s, un, dgl, dpp, gg["ple_norm"], sq = _ple_call(
        h3, gains["ple_norm"], wts["w_ple_gate"], p, wts["w_ple_proj"], tgt, "ple")
    grads["w_ple_gate"] = _tn_call(un, dgl, "dw_ple_gate")
    grads["w_ple_proj"] = _tn_call(p, dpp, "dw_ple_proj")

    (da2, db2), _ = _ffn_bwd_a_call(dh3s, a2, b2, wts["ffn2_w_out"], "ffn2_bwd_act")
    grads["ffn2_w_out"] = _tn_call(hm2, dh3s, "dw_ffn2_out")
    grads["ffn2_w_in"] = jnp.concatenate([_tn_call(u2, da2, "dw_ffn2_in_a", shard_cols=D_FF // 2),
                                          _tn_call(u2, db2, "dw_ffn2_in_b", shard_cols=D_FF // 2)], axis=0)
    dh2, dh2b, gg["ffn2_norm"] = _norm_bwd_call([da2, db2], [wts["ffn2_w_in"]], h2, gains["ffn2_norm"], dh3,
                                                "ffn2_bwd_norm", half_out=False)

    dgates, dbm, dbs, dom, dos = _merge_bwd_call(dh2b, gates, bm, bs, wo, wbm, wbs, "merge_bwd")
    grads["w_out"] = _tn_call(mg, dh2b, "dw_out")
    grads["w_branch_mla"] = _tn_call(om, dbm, "dw_branch_mla").reshape(HEADS, HEAD_PAD, d)[:, 64:, :].reshape(512, d)
    grads["w_branch_sb"] = _tn_call(osb, dbs, "dw_branch_sb")
    pair_sum, rider = reduce_before(0)
    (dq, dk, dv), got = _mla_bwd_call(q, k, v, om, dom, lse, "mla_bwd", rider)
    reduce_after(0, pair_sum, got)
    dsq, dsk, dsv = _sb_bwd_call(sbq, sbk, sbv, dos, osb, "sb_bwd")
    (dcq, dckv, dkr, dwq, grads["w_kv_up"], gg["q_latent_norm"], gg["kv_latent_norm"], dgqh, dgkh) = \
        _mla_prep_bwd_call(*prep_args, dq, dk, dv, "mla_prep_bwd")
    grads["w_q_up"] = dwq.reshape(Q_LORA, HEADS, HEAD_PAD)[:, :, :MLA_QK].reshape(Q_LORA, HEADS * MLA_QK)
    gg["q_head_norm"], gg["k_head_norm"] = dgqh[:, :MLA_QK], dgkh[:, :MLA_QK]
    dproj = jnp.concatenate([dcq, dckv, dkr, dsq, dsk.astype(BF16), dsv.astype(BF16), dgates], axis=1)
    grads["w_in"] = _unlayout_w_in(_tn_call(um, dproj, "dw_in"))
    dh1, dh1s, gg["mix_norm"] = _norm_bwd_call([dproj], [w_in], h1, gains["mix_norm"], dh2, "mix_bwd_norm", half_out=True)

    pair_sum, rider = reduce_before(1)
    (da1, db1), got = _ffn_bwd_a_call(dh1s, a1, b1, wts["ffn1_w_out"], "ffn1_bwd_act", rider)
    reduce_after(1, pair_sum, got)
    grads["ffn1_w_out"] = _tn_call(hm1, dh1s, "dw_ffn1_out")
    pair_sum, rider = reduce_before(2)
    res = _tn_call(u1, da1, "dw_ffn1_in_a", shard_cols=D_FF // 2, rider=rider)
    dwa, got = (res, None) if rider is None else res
    reduce_after(2, pair_sum, got)
    grads["ffn1_w_in"] = jnp.concatenate([dwa, _tn_call(u1, db1, "dw_ffn1_in_b", shard_cols=D_FF // 2)], axis=0)
    pair_sum, rider = reduce_before(3)
    res = _norm_bwd_call([da1, db1], [wts["ffn1_w_in"]], x, gains["ffn1_norm"], dh1, "ffn1_bwd_norm",
                         half_out=False, rider=rider)
    (dx, _, gg["ffn1_norm"]), got = (res, None) if rider is None else res
    reduce_after(3, pair_sum, got)
    return sq, dx, gg, (grads if dist is None else reduced)


def kernel(x, p, positions, ffn1_norm, ffn1_w_in, ffn1_w_out, mix_norm, w_in, q_latent_norm, w_q_up, kv_latent_norm, w_kv_up, q_head_norm, k_head_norm, w_branch_mla, w_branch_sb, w_out, ffn2_norm, ffn2_w_in, ffn2_w_out, ple_norm, w_ple_gate, w_ple_proj, loss_target, m_ffn1_norm, m_ffn1_w_in, m_ffn1_w_out, m_mix_norm, m_w_in, m_q_latent_norm, m_w_q_up, m_kv_latent_norm, m_w_kv_up, m_q_head_norm, m_k_head_norm, m_w_branch_mla, m_w_branch_sb, m_w_out, m_ffn2_norm, m_ffn2_w_in, m_ffn2_w_out, m_ple_norm, m_w_ple_gate, m_w_ple_proj, v_ffn1_norm, v_ffn1_w_in, v_ffn1_w_out, v_mix_norm, v_w_in, v_q_latent_norm, v_w_q_up, v_kv_latent_norm, v_w_kv_up, v_q_head_norm, v_k_head_norm, v_w_branch_mla, v_w_branch_sb, v_w_out, v_ffn2_norm, v_ffn2_w_in, v_ffn2_w_out, v_ple_norm, v_w_ple_gate, v_w_ple_proj):
    given = dict(locals())
    w_shard = {n: given[n][0] for n in WEIGHT_ORDER}
    m_shard = {n: given["m_" + n][0] for n in WEIGHT_ORDER}
    v_shard = {n: given["v_" + n][0] for n in WEIGHT_ORDER}
    gains = {n: w_shard[n][None] for n in GAINS}

    chip = 2 * lax.axis_index("x") + lax.axis_index("y")
    sq, dx, gain_grads, reduced = _step(x[0], p[0, 0], positions.reshape(-1, 1), loss_target[0], gains,
                                        {n: w_shard[n].astype(BF16) for n in BIG}, (chip, lax.axis_index("c")))

    rows = [jnp.pad(gain_grads[n], ((0, 0), (0, D_MODEL - GAINS[n]))) for n in GAINS] + [sq]
    gain_block = jnp.concatenate(rows + [jnp.zeros((16 - len(rows), D_MODEL), F32)], axis=0)
    gain_sum = _sum_call(_all_gather_small_call(gain_block, "gains_all_gather"), F32, "gains_sum")
    loss = 0.5 * jnp.sum(gain_sum[len(GAINS)]) / D_MODEL

    outs = {"grad": {}, "delta": {}, "new_m": {}, "new_v": {}}
    gain_pack = lambda t: jnp.concatenate([jnp.pad(t[n][None], ((0, 0), (0, D_MODEL - GAINS[n]))) for n in GAINS], axis=0)
    packed = _adamw_call(gain_pack(w_shard), gain_sum, 0, gain_pack(m_shard), gain_pack(v_shard), "adamw_gains")
    for i, n in enumerate(GAINS):
        for kind, t in zip(("grad", "delta", "new_m", "new_v"), packed):
            outs[kind][n] = t[i, :GAINS[n]][None]
    for n in BIG:
        buf, row0 = reduced[n]
        for kind, t in zip(("grad", "delta", "new_m", "new_v"),
                           _adamw_call(w_shard[n], buf, row0, m_shard[n], v_shard[n], "adamw_" + n)):
            outs[kind][n] = t[None]

    return (loss, dx[None], *[outs["grad"][n] for n in WEIGHT_ORDER], *[outs["delta"][n] for n in WEIGHT_ORDER],
            *[outs["new_m"][n] for n in WEIGHT_ORDER], *[outs["new_v"][n] for n in WEIGHT_ORDER])
```

```python
import collections
import functools
import math

import jax
import jax.numpy as jnp
from jax import lax
from jax.experimental import pallas as pl
from jax.experimental.pallas import tpu as pltpu

F32 = jnp.float32
BF16 = jnp.bfloat16
MESH = pl.DeviceIdType.MESH

D_MODEL = 1024
D_FF = 2816
PLE_DIM = 256
NORM_EPS = 1e-6
HEADS = 8
MLA_NOPE = 64
MLA_ROPE = 32
MLA_QK = 96
Q_LORA = 384
KV_LORA = 256
SB_WIDTH = 512
ROPE_BASE = 10000.0
LOG2_E = math.log2(math.e)
HEAD_PAD = 128
N_CHIPS = 4

ADAM_LR = 0.001
ADAM_B1 = 0.9
ADAM_B2 = 0.999
ADAM_EPS = 1e-08
ADAM_WD = 0.01
ADAM_STEP = 10

SEG_CQ = (0, 384)
SEG_CKV = (384, 256)
SEG_KROPE = (640, 128)
SEG_SBQ = (768, 512)
SEG_SBK = (1280, 512)
SEG_SBV = (1792, 512)
SEG_GATES = (2304, 2048)
IN_COLS_PAD = 4352

TM = 512
TM_SMALL = 512
TM_PREP_BWD = 256
TQ = 256
MLA_FWD_BLOCKS = 4
MLA_BWD_BLOCKS = 4
SB_FWD_BLOCKS = 4
SB_BWD_BLOCKS = 2
SB_HEAD = 64
SB_SCALE = 0.125
SB_DEAD = -104.0
COL_CHUNK = 256
TN_MAX_COLS = 2816
TN_OPERAND_BYTES = 34 * 1024 * 1024
MAX_ROW_TILE = 512
VMEM_LIMIT = 56 * 1024 * 1024

NT = (((1,), (1,)), ((), ()))
TN = (((0,), (0,)), ((), ()))


def _cp(sem):
    return pltpu.CompilerParams(dimension_semantics=sem, vmem_limit_bytes=VMEM_LIMIT)


def _rows(tm, w):
    return pl.BlockSpec((tm, w), lambda i: (i, 0))


def _whole(shape):
    return pl.BlockSpec(shape, lambda i: (0,) * len(shape))


def _dot(a, b):
    return jnp.dot(a, b, preferred_element_type=F32)


def _dot_nt(a, b):
    return lax.dot_general(a, b, NT, preferred_element_type=F32)


def _dot_tn(a, b):
    return lax.dot_general(a, b, TN, preferred_element_type=F32)


def _rstd(x, n):
    return lax.rsqrt(jnp.sum(x * x, axis=-1, keepdims=True) / n + NORM_EPS)


def _rms_bwd(x, r, g, dy, n):
    gy = dy * g
    return r * gy - x * ((r * r * r) * (jnp.sum(x * gy, axis=-1, keepdims=True) / n))


def _sigmoid(x):
    return jax.nn.sigmoid(x)


def _pick(n, cands):
    for c in cands:
        if n % c == 0:
            return c
    return n


def _row_tile(r):
    for t in range(min(r, MAX_ROW_TILE) // 16 * 16, 15, -16):
        if r % t == 0:
            return t
    return r


HBM = pl.BlockSpec(memory_space=pl.ANY)

_Rider = collections.namedtuple("_Rider", "ins out_shape sems start finish")


def _with_rider(body, rider, *, name, grid, in_specs, out_specs, out_shape, args, sem, scratch=()):
    if rider is None:
        return pl.pallas_call(body, name=name, grid=grid, in_specs=in_specs, out_specs=out_specs, out_shape=out_shape,
                              scratch_shapes=list(scratch), compiler_params=_cp(sem))(*args), None
    ni, no, nri, nro = len(in_specs), len(out_specs), len(rider.ins), len(rider.out_shape)

    def riding(*refs):
        ins, r_ins = refs[:ni], refs[ni:ni + nri]
        outs, r_outs = refs[ni + nri:ni + nri + no], refs[ni + nri + no:ni + nri + no + nro]
        scr = refs[ni + nri + no + nro:ni + nri + no + nro + len(scratch)]
        sems = refs[ni + nri + no + nro + len(scratch):]
        ids = [pl.program_id(a) for a in range(len(grid))]
        first = functools.reduce(jnp.logical_and, [i == 0 for i in ids])
        last = functools.reduce(jnp.logical_and, [i == g - 1 for i, g in zip(ids, grid)])

        @pl.when(first)
        def _():
            rider.start(r_ins, r_outs, sems)

        body(*ins, *outs, *scr)

        @pl.when(last)
        def _():
            rider.finish(r_ins, r_outs, sems)

    res = pl.pallas_call(
        riding, name=name, grid=grid, in_specs=list(in_specs) + [HBM] * nri, out_specs=list(out_specs) + [HBM] * nro,
        out_shape=list(out_shape) + list(rider.out_shape),
        scratch_shapes=list(scratch) + [pltpu.SemaphoreType.DMA((k,)) for k in rider.sems],
        compiler_params=_cp(("arbitrary",) * len(grid)))(*args, *rider.ins)
    return res[:no], res[no:]


def _norm_call(h, g, name, rider=None):
    s, d = h.shape
    tm = min(TM, s)

    def body(h_ref, g_ref, u_ref):
        x = h_ref[...]
        u_ref[...] = ((x * _rstd(x, d)) * g_ref[...]).astype(BF16)

    (u,), got = _with_rider(
        body, rider, name=name, grid=(s // tm,),
        in_specs=[_rows(tm, d), _whole((1, d))], out_specs=[_rows(tm, d)],
        out_shape=[jax.ShapeDtypeStruct((s, d), BF16)], args=(h, g), sem=("parallel",))
    return u, got


def _ffn_in_call(u, w, name, rider=None):
    s, d = u.shape
    tn = w.shape[2]
    nj = w.shape[0] // 2
    n = nj * tn
    tm = min(TM, s)

    def body(u_ref, wa_ref, wb_ref, a_ref, b_ref, hm_ref):
        uu = u_ref[...]
        a = _dot(uu, wa_ref[...])
        b = _dot(uu, wb_ref[...])
        a_ref[...] = a
        b_ref[...] = b
        hm_ref[...] = ((a * _sigmoid(a)) * b).astype(BF16)

    blk = pl.BlockSpec((tm, tn), lambda j, i: (i, j))
    return _with_rider(
        body, rider, name=name, grid=(nj, s // tm),
        in_specs=[pl.BlockSpec((tm, d), lambda j, i: (i, 0)),
                  pl.BlockSpec((None, d, tn), lambda j, i: (j, 0, 0)),
                  pl.BlockSpec((None, d, tn), lambda j, i: (j + nj, 0, 0))],
        out_specs=[blk, blk, blk],
        out_shape=[jax.ShapeDtypeStruct((s, n), F32), jax.ShapeDtypeStruct((s, n), F32),
                   jax.ShapeDtypeStruct((s, n), BF16)],
        args=(u, w, w), sem=("parallel", "parallel"))


def _ffn_out_call(hm, w, h, gain, name, rider=None):
    s, n = hm.shape
    d = w.shape[1]
    tm = min(TM, s)

    def body(hm_ref, w_ref, h_ref, g_ref, o_ref, u_ref):
        x = h_ref[...] + 0.5 * _dot(hm_ref[...], w_ref[...])
        o_ref[...] = x
        u_ref[...] = ((x * _rstd(x, d)) * g_ref[...]).astype(BF16)

    return _with_rider(
        body, rider, name=name, grid=(s // tm,),
        in_specs=[_rows(tm, n), _whole((n, d)), _rows(tm, d), _whole((1, d))], out_specs=[_rows(tm, d), _rows(tm, d)],
        out_shape=[jax.ShapeDtypeStruct((s, d), F32), jax.ShapeDtypeStruct((s, d), BF16)], args=(hm, w, h, gain),
        sem=("parallel",))


def _mix_in_call(u, w, name):
    s, d = u.shape
    tm = min(TM_SMALL, s)
    segs = [(SEG_CQ, F32), (SEG_CKV, F32), (SEG_KROPE, F32), (SEG_SBQ, BF16), (SEG_SBK, BF16),
            (SEG_SBV, BF16), (SEG_GATES, F32)]

    def body(u_ref, w_ref, *outs):
        uu = u_ref[...]
        for ((off, width), _), o_ref in zip(segs, outs):
            o_ref[...] = _dot(uu, w_ref[:, off:off + width]).astype(o_ref.dtype)

    return pl.pallas_call(
        body, name=name, grid=(s // tm,),
        in_specs=[_rows(tm, d), _whole((d, IN_COLS_PAD))],
        out_specs=[_rows(tm, width) for (_, width), _ in segs],
        out_shape=[jax.ShapeDtypeStruct((s, width), dt) for (_, width), dt in segs],
        compiler_params=_cp(("parallel",)))(u, w)


def _lane(shape):
    return lax.broadcasted_iota(jnp.int32, shape, len(shape) - 1)


def _rot_half(y):
    lane = _lane(y.shape)
    swapped = jnp.where(lane < MLA_NOPE + MLA_ROPE // 2, pltpu.roll(y, HEAD_PAD - 16, 1), pltpu.roll(y, 16, 1))
    return jnp.where((lane >= MLA_NOPE) & (lane < MLA_QK), swapped, 0.0)


def _rope_tables(pos_ref, freq_ref, sign_ref):
    ang = pos_ref[...].astype(F32) * freq_ref[...]
    return jnp.cos(ang), jnp.sin(ang) * sign_ref[...]


def _head_fwd(x, g, cosv, ssv):
    r = _rstd(x, MLA_QK)
    y = (x * r) * g
    return y * cosv + _rot_half(y) * ssv, r


def _head_bwd(x, r, g, cosv, ssv, dout):
    dy = dout * cosv + _rot_half(dout * ssv)
    return _rms_bwd(x, r, g, dy, MLA_QK), jnp.sum(dy * (x * r), axis=0, keepdims=True)


def _mla_prep_call(cq, ckv, krope, pos, freq, sign, g_ql, g_kvl, g_qh, g_kh, wq, wkv, name):
    s = cq.shape[0]
    tm = min(TM_SMALL, s)
    width = HEADS * HEAD_PAD

    def body(cq_ref, ckv_ref, kr_ref, pos_ref, freq_ref, sign_ref, gql_ref, gkvl_ref, gqh_ref, gkh_ref,
             wq_ref, wkv_ref, q_ref, k_ref, v_ref):
        cosv, ssv = _rope_tables(pos_ref, freq_ref, sign_ref)
        x = cq_ref[...]
        qr = _dot(((x * _rstd(x, Q_LORA)) * gql_ref[...]).astype(BF16), wq_ref[...])
        x = ckv_ref[...]
        kv = _dot(((x * _rstd(x, KV_LORA)) * gkvl_ref[...]).astype(BF16), wkv_ref[...])
        kr = kr_ref[...]
        lane = _lane((tm, HEAD_PAD))
        for h in range(HEADS):
            sl = slice(h * HEAD_PAD, (h + 1) * HEAD_PAD)
            qh, _ = _head_fwd(qr[:, sl], gqh_ref[...], cosv, ssv)
            q_ref[:, sl] = qh.astype(BF16)
            kvh = kv[:, sl]
            kh, _ = _head_fwd(jnp.where(lane < MLA_NOPE, kvh, kr), gkh_ref[...], cosv, ssv)
            k_ref[:, sl] = kh.astype(BF16)
            v_ref[:, sl] = jnp.where(lane >= MLA_NOPE, kvh, jnp.where(lane == 0, 1.0, 0.0)).astype(BF16)

    out = jax.ShapeDtypeStruct((s, width), BF16)
    return pl.pallas_call(
        body, name=name, grid=(s // tm,),
        in_specs=[_rows(tm, Q_LORA), _rows(tm, KV_LORA), _rows(tm, HEAD_PAD), _rows(tm, 1),
                  _whole((1, HEAD_PAD)), _whole((1, HEAD_PAD)), _whole((1, Q_LORA)), _whole((1, KV_LORA)),
                  _whole((1, HEAD_PAD)), _whole((1, HEAD_PAD)), _whole((Q_LORA, width)), _whole((KV_LORA, width))],
        out_specs=[_rows(tm, width)] * 3, out_shape=[out, out, out],
        compiler_params=_cp(("parallel",)))(cq, ckv, krope, pos, freq, sign, g_ql, g_kvl, g_qh, g_kh, wq, wkv)


def _attn_specs(s, nb):
    qspec = pl.BlockSpec((TQ, nb * HEAD_PAD), lambda g, i: (i, g))
    kspec = pl.BlockSpec((s, nb * HEAD_PAD), lambda g, i: (0, g))
    return qspec, kspec


def _lanes(b):
    return slice(b * HEAD_PAD, (b + 1) * HEAD_PAD)


def _tri(cmp):
    r = lax.broadcasted_iota(jnp.int32, (TQ, TQ), 0)
    c = lax.broadcasted_iota(jnp.int32, (TQ, TQ), 1)
    return cmp(r, c)


def _mla_fwd_call(q, k, v, name, rider=None):
    s, width = q.shape
    scale = 1.0 / math.sqrt(MLA_QK)

    nb = MLA_FWD_BLOCKS

    def body(q_ref, k_ref, v_ref, o_ref, lse_ref):
        qi = pl.program_id(1)
        qs = [q_ref[:, _lanes(b)] for b in range(nb)]
        causal = _tri(lambda r, c: c <= r)

        def step(kb, carry, diag):
            ks = pl.multiple_of(kb * TQ, TQ)
            heads = range(nb)
            scs = [_dot_nt(qs[b], k_ref[pl.ds(ks, TQ), _lanes(b)]) * (scale * LOG2_E) for b in heads]
            if diag:
                scs = [jnp.where(causal, sc, -1e30) for sc in scs]
            mns = [jnp.maximum(carry[b][0], jnp.max(scs[b], axis=-1, keepdims=True)) for b in heads]
            als = [jnp.exp2(carry[b][0] - mns[b]) for b in heads]
            ps = [jnp.exp2(scs[b] - mns[b]).astype(BF16) for b in heads]
            accs = [als[b] * carry[b][1] + _dot(ps[b], v_ref[pl.ds(ks, TQ), _lanes(b)]) for b in heads]
            return tuple((mns[b], accs[b]) for b in heads)

        init = tuple((jnp.full((TQ, 1), -1e30, F32), jnp.zeros((TQ, HEAD_PAD), F32)) for _ in range(nb))
        carry = step(qi, init, True)
        carry = lax.fori_loop(0, qi, lambda kb, c: step(kb, c, False), carry)
        for b in range(nb):
            m, acc = carry[b]
            l = acc[:, 0:1]
            o_ref[:, _lanes(b)] = (acc / l).astype(BF16)
            lse_ref[:, _lanes(b)] = jnp.broadcast_to(m * (1.0 / LOG2_E) + jnp.log(l), (TQ, HEAD_PAD))

    qspec, kspec = _attn_specs(s, nb)
    return _with_rider(
        body, rider, name=name, grid=(width // (nb * HEAD_PAD), s // TQ),
        in_specs=[qspec, kspec, kspec], out_specs=[qspec, qspec],
        out_shape=[jax.ShapeDtypeStruct((s, width), BF16), jax.ShapeDtypeStruct((s, width), F32)],
        args=(q, k, v), sem=("parallel", "arbitrary"))


def _mla_bwd_call(q, k, v, o, do, lse, name, rider=None):
    s, width = q.shape
    scale = 1.0 / math.sqrt(MLA_QK)
    nb = MLA_BWD_BLOCKS

    def body(q_ref, k_ref, v_ref, o_ref, do_ref, lse_ref, dq_ref, dk_ref, dv_ref):
        qi = pl.program_id(1)

        @pl.when(qi == 0)
        def _():
            dk_ref[...] = jnp.zeros_like(dk_ref)
            dv_ref[...] = jnp.zeros_like(dv_ref)

        qs = [q_ref[:, _lanes(b)] for b in range(nb)]
        dos = [do_ref[:, _lanes(b)] for b in range(nb)]
        lses = [lse_ref[:, b * HEAD_PAD:b * HEAD_PAD + 1] for b in range(nb)]
        dlts = [jnp.sum(dos[b].astype(F32) * o_ref[:, _lanes(b)].astype(F32), axis=-1, keepdims=True) for b in range(nb)]
        causal = _tri(lambda r, c: c <= r)

        def step(kb, dqs, diag):
            ks = pl.multiple_of(kb * TQ, TQ)
            heads = range(nb)
            kts = [k_ref[pl.ds(ks, TQ), _lanes(b)] for b in heads]
            scs = [_dot_nt(qs[b], kts[b]) for b in heads]
            dps = [_dot_nt(dos[b], v_ref[pl.ds(ks, TQ), _lanes(b)]) for b in heads]
            ps = [jnp.exp(scs[b] * scale - lses[b]) for b in heads]
            if diag:
                ps = [jnp.where(causal, p, 0.0) for p in ps]
            dss = [(ps[b] * (dps[b] - dlts[b]) * scale).astype(BF16) for b in heads]
            dvs = [_dot_tn(ps[b].astype(BF16), dos[b]) for b in heads]
            dks = [_dot_tn(dss[b], qs[b]) for b in heads]
            out = tuple(dqs[b] + _dot(dss[b], kts[b]) for b in heads)
            for b in heads:
                dv_ref[pl.ds(ks, TQ), _lanes(b)] += dvs[b]
                dk_ref[pl.ds(ks, TQ), _lanes(b)] += dks[b]
            return out

        dqs = step(qi, tuple(jnp.zeros((TQ, HEAD_PAD), F32) for _ in range(nb)), True)
        dqs = lax.fori_loop(0, qi, lambda kb, c: step(kb, c, False), dqs)
        for b in range(nb):
            dq_ref[:, _lanes(b)] = dqs[b]

    qspec, kspec = _attn_specs(s, nb)
    out = jax.ShapeDtypeStruct((s, width), F32)
    return _with_rider(
        body, rider, name=name, grid=(width // (nb * HEAD_PAD), s // TQ),
        in_specs=[qspec, kspec, kspec, qspec, qspec, qspec], out_specs=[qspec, kspec, kspec],
        out_shape=[out, out, out], args=(q, k, v, o, do, lse), sem=("parallel", "arbitrary"))


def _dot_hilo(x, u):
    hi = x.astype(BF16)
    lo = (x - hi.astype(F32)).astype(BF16)
    return _dot(hi, u) + _dot(lo, u)


def _sb_logs(z):
    ls = jnp.minimum(z, 0.0) - jnp.log(1.0 + jnp.exp(-jnp.abs(z)))
    return ls, ls - z


def _sb_head_q(qb, first, hh):
    keep = first if hh == 0 else jnp.logical_not(first)
    return jnp.where(keep, qb, jnp.zeros_like(qb)) * jnp.asarray(SB_SCALE, qb.dtype)


def _sb_fwd_call(q, k, v, name):
    s, width = q.shape
    nb = SB_FWD_BLOCKS
    chains = [(b, hh) for b in range(nb) for hh in range(HEAD_PAD // SB_HEAD)]

    def body(q_ref, k_ref, v_ref, o_ref):
        qi = pl.program_id(1)
        strict = _tri(lambda r, c: c < r)
        after = _tri(lambda r, c: r > c).astype(BF16)
        first = _lane((1, HEAD_PAD)) < SB_HEAD
        qhs = [_sb_head_q(q_ref[:, _lanes(b)], first, hh) for b, hh in chains]

        def step(kb, carry, diag):
            ks = pl.multiple_of(kb * TQ, TQ)
            ids = range(len(chains))
            zs = [_dot_nt(qhs[ci], k_ref[pl.ds(ks, TQ), _lanes(chains[ci][0])]) for ci in ids]
            logs = [_sb_logs(z) for z in zs]
            lss = [lg[0] for lg in logs]
            l1ms = [jnp.where(strict, lg[1], 0.0) if diag else lg[1] for lg in logs]
            sufs = [_dot_hilo(l1m, after) for l1m in l1ms]
            as_ = [jnp.exp(lss[ci] + sufs[ci] + carry[ci][0]) for ci in ids]
            if diag:
                as_ = [jnp.where(strict, a, 0.0) for a in as_]
            accs = [carry[ci][1] + _dot(as_[ci].astype(BF16), v_ref[pl.ds(ks, TQ), _lanes(chains[ci][0])]) for ci in ids]
            return tuple((carry[ci][0] + jnp.sum(l1ms[ci], axis=-1, keepdims=True), accs[ci]) for ci in ids)

        init = tuple((jnp.zeros((TQ, 1), F32), jnp.zeros((TQ, HEAD_PAD), F32)) for _ in chains)
        carry = _sb_sweep(step, qi, init)
        for b in range(nb):
            o_ref[:, _lanes(b)] = jnp.where(first, carry[2 * b][1], carry[2 * b + 1][1])

    qspec, kspec = _attn_specs(s, nb)
    return pl.pallas_call(
        body, name=name, grid=(width // (nb * HEAD_PAD), s // TQ),
        in_specs=[qspec, kspec, kspec], out_specs=qspec, out_shape=jax.ShapeDtypeStruct((s, width), F32),
        compiler_params=_cp(("parallel", "arbitrary")))(q, k, v)


def _sb_sweep(step, qi, init):
    def live(carry):
        top = carry[0][0]
        for c in carry[1:]:
            top = jnp.maximum(top, c[0])
        return jnp.max(top)

    carry = step(qi, init, True)

    def cond(state):
        j, alive, _ = state
        return jnp.logical_and(j < qi, alive > SB_DEAD)

    def body(state):
        j, _, carry = state
        carry = step(qi - 1 - j, carry, False)
        return j + 1, live(carry), carry

    return lax.while_loop(cond, body, (jnp.int32(0), live(carry), carry))[2]


def _sb_bwd_call(q, k, v, do, o, name):
    s, width = q.shape
    nb = SB_BWD_BLOCKS
    chains = [(b, hh) for b in range(nb) for hh in range(HEAD_PAD // SB_HEAD)]

    def body(q_ref, k_ref, v_ref, do_ref, o_ref, dq_ref, dk_ref, dv_ref):
        qi = pl.program_id(1)

        @pl.when(qi == 0)
        def _():
            dk_ref[...] = jnp.zeros_like(dk_ref)
            dv_ref[...] = jnp.zeros_like(dv_ref)

        strict = _tri(lambda r, c: c < r)
        after = _tri(lambda r, c: r > c).astype(BF16)
        from_here = _tri(lambda r, c: r >= c).astype(BF16)
        first = _lane((1, HEAD_PAD)) < SB_HEAD
        qhs = [_sb_head_q(q_ref[:, _lanes(b)], first, hh) for b, hh in chains]
        dohs = []
        for b, hh in chains:
            dob = do_ref[:, _lanes(b)]
            dohs.append(jnp.where(first if hh == 0 else jnp.logical_not(first), dob, jnp.zeros_like(dob)))
        gtots = [jnp.sum(dohs[ci].astype(F32) * o_ref[:, _lanes(chains[ci][0])], axis=-1, keepdims=True)
                 for ci in range(len(chains))]

        def step(kb, carry, diag):
            ks = pl.multiple_of(kb * TQ, TQ)
            ids = range(len(chains))
            kts = [k_ref[pl.ds(ks, TQ), _lanes(b)] for b, _ in chains]
            zs = [_dot_nt(qhs[ci], kts[ci]) for ci in ids]
            das = [_dot_nt(dohs[ci], v_ref[pl.ds(ks, TQ), _lanes(chains[ci][0])]) for ci in ids]
            logs = [_sb_logs(z) for z in zs]
            lss = [lg[0] for lg in logs]
            l1ms = [jnp.where(strict, lg[1], 0.0) if diag else lg[1] for lg in logs]
            sufs = [_dot_hilo(l1m, after) for l1m in l1ms]
            as_ = [jnp.exp(lss[ci] + sufs[ci] + carry[ci][0]) for ci in ids]
            if diag:
                as_ = [jnp.where(strict, a, 0.0) for a in as_]
            abs_ = [a.astype(BF16) for a in as_]
            gs = [abs_[ci].astype(F32) * das[ci] for ci in ids]
            cexs = [gtots[ci] - (carry[ci][1] + _dot_hilo(gs[ci], from_here)) for ci in ids]
            dzs = [gs[ci] - jnp.exp(lss[ci]) * (gs[ci] + cexs[ci]) for ci in ids]
            if diag:
                dzs = [jnp.where(strict, dz, 0.0) for dz in dzs]
            dzbs = [dz.astype(BF16) for dz in dzs]
            dvps = [_dot_tn(abs_[ci], dohs[ci]) for ci in ids]
            dkps = [_dot_tn(dzbs[ci], qhs[ci]) for ci in ids]
            out = tuple((carry[ci][0] + jnp.sum(l1ms[ci], axis=-1, keepdims=True),
                         carry[ci][1] + jnp.sum(gs[ci], axis=-1, keepdims=True),
                         carry[ci][2] + _dot(dzbs[ci], kts[ci])) for ci in ids)
            for b in range(nb):
                dk_ref[pl.ds(ks, TQ), _lanes(b)] += dkps[2 * b] + dkps[2 * b + 1]
                dv_ref[pl.ds(ks, TQ), _lanes(b)] += dvps[2 * b] + dvps[2 * b + 1]
            return out

        init = tuple((jnp.zeros((TQ, 1), F32), jnp.zeros((TQ, 1), F32), jnp.zeros((TQ, HEAD_PAD), F32)) for _ in chains)
        carry = _sb_sweep(step, qi, init)
        for b in range(nb):
            dq_ref[:, _lanes(b)] = (jnp.where(first, carry[2 * b][2], carry[2 * b + 1][2]) * SB_SCALE).astype(BF16)

    qspec, kspec = _attn_specs(s, nb)
    return pl.pallas_call(
        body, name=name, grid=(width // (nb * HEAD_PAD), s // TQ),
        in_specs=[qspec, kspec, kspec, qspec, qspec], out_specs=[qspec, kspec, kspec],
        out_shape=[jax.ShapeDtypeStruct((s, width), BF16), jax.ShapeDtypeStruct((s, width), F32),
                   jax.ShapeDtypeStruct((s, width), F32)],
        compiler_params=_cp(("parallel", "arbitrary")))(q, k, v, do, o)


def _merge_out_call(om, osb, gates, h, wbm, wbs, wo, gain, name):
    s, d = h.shape
    tm = min(TM_SMALL, s)

    def body(om_ref, os_ref, g_ref, h_ref, wbm_ref, wbs_ref, wo_ref, gain_ref, h2_ref, bm_ref, bs_ref, mg_ref, u_ref):
        bm = _dot(om_ref[...], wbm_ref[...])
        bs = _dot(os_ref[...].astype(BF16), wbs_ref[...])
        mg = (_sigmoid(g_ref[:, :d]) * bm + _sigmoid(g_ref[:, d:]) * bs).astype(BF16)
        bm_ref[...] = bm
        bs_ref[...] = bs
        mg_ref[...] = mg
        x = h_ref[...] + _dot(mg, wo_ref[...])
        h2_ref[...] = x
        u_ref[...] = ((x * _rstd(x, d)) * gain_ref[...]).astype(BF16)

    return pl.pallas_call(
        body, name=name, grid=(s // tm,),
        in_specs=[_rows(tm, om.shape[1]), _rows(tm, SB_WIDTH), _rows(tm, 2 * d), _rows(tm, d),
                  _whole(wbm.shape), _whole(wbs.shape), _whole(wo.shape), _whole((1, d))],
        out_specs=[_rows(tm, d)] * 5,
        out_shape=[jax.ShapeDtypeStruct((s, d), F32), jax.ShapeDtypeStruct((s, d), F32),
                   jax.ShapeDtypeStruct((s, d), F32), jax.ShapeDtypeStruct((s, d), BF16),
                   jax.ShapeDtypeStruct((s, d), BF16)],
        compiler_params=_cp(("parallel",)))(om, osb, gates, h, wbm, wbs, wo, gain)


def _ple_call(h, g, wg, p, wp, tgt, name):
    s, d = h.shape
    tm = min(TM_SMALL, s)

    def body(h_ref, g_ref, wg_ref, p_ref, wp_ref, t_ref, dh_ref, dhs_ref, un_ref, dgl_ref, dpp_ref, dg_ref, sq_ref):
        @pl.when(pl.program_id(0) == 0)
        def _():
            dg_ref[...] = jnp.zeros_like(dg_ref)
            sq_ref[...] = jnp.zeros_like(sq_ref)

        x = h_ref[...]
        gain = g_ref[...]
        r = _rstd(x, d)
        xh = x * r
        un = (xh * gain).astype(BF16)
        sg = _sigmoid(_dot(un, wg_ref[...]))
        pp = _dot(p_ref[...].astype(BF16), wp_ref[...])
        diff = (x + sg * pp) - t_ref[...]
        sq_ref[...] += jnp.sum(diff * diff, axis=0, keepdims=True)
        dy = diff * (1.0 / d)
        dgl = ((dy * pp) * (sg * (1.0 - sg))).astype(BF16)
        dun = _dot_nt(dgl, wg_ref[...])
        dg_ref[...] += jnp.sum(dun * xh, axis=0, keepdims=True)
        dh = dy + _rms_bwd(x, r, gain, dun, d)
        dh_ref[...] = dh
        dhs_ref[...] = (0.5 * dh).astype(BF16)
        un_ref[...] = un
        dgl_ref[...] = dgl
        dpp_ref[...] = (dy * sg).astype(BF16)

    bf = jax.ShapeDtypeStruct((s, d), BF16)
    vec = jax.ShapeDtypeStruct((1, d), F32)
    return pl.pallas_call(
        body, name=name, grid=(s // tm,),
        in_specs=[_rows(tm, d), _whole((1, d)), _whole(wg.shape), _rows(tm, PLE_DIM), _whole(wp.shape), _rows(tm, d)],
        out_specs=[_rows(tm, d)] * 5 + [_whole((1, d))] * 2,
        out_shape=[jax.ShapeDtypeStruct((s, d), F32), bf, bf, bf, bf, vec, vec],
        compiler_params=_cp(("arbitrary",)))(h, g, wg, p, wp, tgt)


def _ffn_bwd_a_call(dhs, a, b, wo, name, rider=None):
    s, n = a.shape
    d = dhs.shape[1]
    tn = n // 2
    tm = min(TM, s)

    def body(dh_ref, a_ref, b_ref, wo_ref, da_ref, db_ref):
        dh = dh_ref[...]
        chunks = [slice(c0, min(c0 + COL_CHUNK, tn)) for c0 in range(0, tn, COL_CHUNK)]
        dhms = [_dot_nt(dh, wo_ref[sl, :]) for sl in chunks]
        for sl, dhm in zip(chunks, dhms):
            av = a_ref[:, sl]
            sa = _sigmoid(av)
            da_ref[:, sl] = (dhm * b_ref[:, sl] * (sa * (1.0 + av * (1.0 - sa)))).astype(BF16)
            db_ref[:, sl] = (dhm * (av * sa)).astype(BF16)

    blk = pl.BlockSpec((tm, tn), lambda j, i: (i, j))
    return _with_rider(
        body, rider, name=name, grid=(n // tn, s // tm),
        in_specs=[pl.BlockSpec((tm, d), lambda j, i: (i, 0)), blk, blk, pl.BlockSpec((tn, d), lambda j, i: (j, 0))],
        out_specs=[blk, blk],
        out_shape=[jax.ShapeDtypeStruct((s, n), BF16)] * 2, args=(dhs, a, b, wo), sem=("parallel", "parallel"))


def _norm_bwd_call(dy_list, w_list, h, g, dh_in, name, half_out, rider=None):
    s, d = h.shape
    tm = min(TM_SMALL, s)
    nk, nw = len(dy_list), len(w_list)
    factor = 0.5 if half_out else 1.0
    sharded = nw == 1 and w_list[0].ndim == 3

    def body(*refs):
        dy_refs = refs[:nk]
        w_refs = refs[nk:nk + nw]
        h_ref, g_ref, dhin_ref, dh_ref, dhb_ref, dg_ref = refs[nk + nw:]

        @pl.when(pl.program_id(0) == 0)
        def _():
            dg_ref[...] = jnp.zeros_like(dg_ref)

        if sharded:
            c = w_list[0].shape[2]
            per = dy_list[0].shape[1] // c
            du = None
            for k in range(w_list[0].shape[0]):
                part = _dot_nt(dy_refs[k // per][:, (k % per) * c:(k % per + 1) * c], w_refs[0][k])
                du = part if du is None else du + part
        else:
            du = _dot_nt(dy_refs[0][...], w_refs[0][...])
            for dy_ref, w_ref in zip(dy_refs[1:], w_refs[1:]):
                du = du + _dot_nt(dy_ref[...], w_ref[...])
        x = h_ref[...]
        r = _rstd(x, d)
        dg_ref[...] += jnp.sum(du * (x * r), axis=0, keepdims=True)
        dh = dhin_ref[...] + _rms_bwd(x, r, g_ref[...], du, d)
        dh_ref[...] = dh
        dhb_ref[...] = (factor * dh).astype(BF16)

    outs, got = _with_rider(
        body, rider, name=name, grid=(s // tm,),
        in_specs=[_rows(tm, dy.shape[1]) for dy in dy_list] + [_whole(w.shape) for w in w_list]
        + [_rows(tm, d), _whole((1, d)), _rows(tm, d)],
        out_specs=[_rows(tm, d), _rows(tm, d), _whole((1, d))],
        out_shape=[jax.ShapeDtypeStruct((s, d), F32), jax.ShapeDtypeStruct((s, d), BF16),
                   jax.ShapeDtypeStruct((1, d), F32)],
        args=(*dy_list, *w_list, h, g, dh_in), sem=("arbitrary",))
    return outs if rider is None else (outs, got)


def _merge_bwd_call(dhb, gates, bm, bs, wo, wbm, wbs, name):
    s, d = bm.shape
    tm = min(TM_SMALL, s)

    def body(dh_ref, g_ref, bm_ref, bs_ref, wo_ref, wbm_ref, wbs_ref, dg_ref, dbm_ref, dbs_ref, dom_ref, dos_ref):
        dmg = _dot_nt(dh_ref[...], wo_ref[...])
        s1 = _sigmoid(g_ref[:, :d])
        s2 = _sigmoid(g_ref[:, d:])
        dg_ref[:, :d] = (dmg * bm_ref[...] * (s1 * (1.0 - s1))).astype(BF16)
        dg_ref[:, d:] = (dmg * bs_ref[...] * (s2 * (1.0 - s2))).astype(BF16)
        dbm = (dmg * s1).astype(BF16)
        dbs = (dmg * s2).astype(BF16)
        dbm_ref[...] = dbm
        dbs_ref[...] = dbs
        dom_ref[...] = _dot_nt(dbm, wbm_ref[...]).astype(BF16)
        dos_ref[...] = _dot_nt(dbs, wbs_ref[...]).astype(BF16)

    wm = wbm.shape[0]
    return pl.pallas_call(
        body, name=name, grid=(s // tm,),
        in_specs=[_rows(tm, d), _rows(tm, 2 * d), _rows(tm, d), _rows(tm, d),
                  _whole(wo.shape), _whole(wbm.shape), _whole(wbs.shape)],
        out_specs=[_rows(tm, 2 * d), _rows(tm, d), _rows(tm, d), _rows(tm, wm), _rows(tm, SB_WIDTH)],
        out_shape=[jax.ShapeDtypeStruct((s, 2 * d), BF16), jax.ShapeDtypeStruct((s, d), BF16),
                   jax.ShapeDtypeStruct((s, d), BF16), jax.ShapeDtypeStruct((s, wm), BF16),
                   jax.ShapeDtypeStruct((s, SB_WIDTH), BF16)],
        compiler_params=_cp(("parallel",)))(dhb, gates, bm, bs, wo, wbm, wbs)


def _mla_prep_bwd_call(cq, ckv, krope, pos, freq, sign, g_ql, g_kvl, g_qh, g_kh, wq, wkv, dq, dk, dv, name):
    s = cq.shape[0]
    tm = min(TM_PREP_BWD, s)
    width = HEADS * HEAD_PAD

    def body(cq_ref, ckv_ref, kr_ref, pos_ref, freq_ref, sign_ref, gql_ref, gkvl_ref, gqh_ref, gkh_ref,
             wq_ref, wkv_ref, dq_ref, dk_ref, dv_ref,
             dcq_ref, dckv_ref, dkr_ref, dwq_ref, dwkv_ref, dgql_ref, dgkvl_ref, dgqh_ref, dgkh_ref, dqr_ref, dkv_ref):
        @pl.when(pl.program_id(0) == 0)
        def _():
            for ref in (dwq_ref, dwkv_ref, dgql_ref, dgkvl_ref, dgqh_ref, dgkh_ref):
                ref[...] = jnp.zeros_like(ref)

        cosv, ssv = _rope_tables(pos_ref, freq_ref, sign_ref)
        xq = cq_ref[...]
        rq = _rstd(xq, Q_LORA)
        cqn = ((xq * rq) * gql_ref[...]).astype(BF16)
        qr = _dot(cqn, wq_ref[...])
        xk = ckv_ref[...]
        rk = _rstd(xk, KV_LORA)
        ckvn = ((xk * rk) * gkvl_ref[...]).astype(BF16)
        kv = _dot(ckvn, wkv_ref[...])
        kr = kr_ref[...]
        lane = _lane((tm, HEAD_PAD))
        dkr = jnp.zeros((tm, HEAD_PAD), F32)
        dgqh = jnp.zeros((1, HEAD_PAD), F32)
        dgkh = jnp.zeros((1, HEAD_PAD), F32)
        for h in range(HEADS):
            sl = slice(h * HEAD_PAD, (h + 1) * HEAD_PAD)
            x = qr[:, sl]
            dx, dgh = _head_bwd(x, _rstd(x, MLA_QK), gqh_ref[...], cosv, ssv, dq_ref[:, sl])
            dqr_ref[:, sl] = dx.astype(BF16)
            dgqh = dgqh + dgh
            x = jnp.where(lane < MLA_NOPE, kv[:, sl], kr)
            dx, dgh = _head_bwd(x, _rstd(x, MLA_QK), gkh_ref[...], cosv, ssv, dk_ref[:, sl])
            dgkh = dgkh + dgh
            dkr = dkr + jnp.where(lane >= MLA_NOPE, dx, 0.0)
            dkv_ref[:, sl] = jnp.where(lane < MLA_NOPE, dx, dv_ref[:, sl]).astype(BF16)
        dgqh_ref[...] += dgqh
        dgkh_ref[...] += dgkh
        dkr_ref[...] = dkr.astype(BF16)
        dqr = dqr_ref[...]
        dkvb = dkv_ref[...]
        dwq_ref[...] += _dot_tn(cqn, dqr)
        dwkv_ref[...] += _dot_tn(ckvn, dkvb)
        dcqn = _dot_nt(dqr, wq_ref[...])
        dgql_ref[...] += jnp.sum(dcqn * (xq * rq), axis=0, keepdims=True)
        dcq_ref[...] = _rms_bwd(xq, rq, gql_ref[...], dcqn, Q_LORA).astype(BF16)
        dckvn = _dot_nt(dkvb, wkv_ref[...])
        dgkvl_ref[...] += jnp.sum(dckvn * (xk * rk), axis=0, keepdims=True)
        dckv_ref[...] = _rms_bwd(xk, rk, gkvl_ref[...], dckvn, KV_LORA).astype(BF16)

    vec = lambda n: jax.ShapeDtypeStruct((1, n), F32)
    outs = pl.pallas_call(
        body, name=name, grid=(s // tm,),
        in_specs=[_rows(tm, Q_LORA), _rows(tm, KV_LORA), _rows(tm, HEAD_PAD), _rows(tm, 1),
                  _whole((1, HEAD_PAD)), _whole((1, HEAD_PAD)), _whole((1, Q_LORA)), _whole((1, KV_LORA)),
                  _whole((1, HEAD_PAD)), _whole((1, HEAD_PAD)), _whole((Q_LORA, width)), _whole((KV_LORA, width)),
                  _rows(tm, width), _rows(tm, width), _rows(tm, width)],
        out_specs=[_rows(tm, Q_LORA), _rows(tm, KV_LORA), _rows(tm, HEAD_PAD), _whole((Q_LORA, width)),
                   _whole((KV_LORA, width)), _whole((1, Q_LORA)), _whole((1, KV_LORA)), _whole((1, HEAD_PAD)),
                   _whole((1, HEAD_PAD)), _rows(tm, width), _rows(tm, width)],
        out_shape=[jax.ShapeDtypeStruct((s, Q_LORA), BF16), jax.ShapeDtypeStruct((s, KV_LORA), BF16),
                   jax.ShapeDtypeStruct((s, HEAD_PAD), BF16), jax.ShapeDtypeStruct((Q_LORA, width), F32),
                   jax.ShapeDtypeStruct((KV_LORA, width), F32), vec(Q_LORA), vec(KV_LORA), vec(HEAD_PAD), vec(HEAD_PAD),
                   jax.ShapeDtypeStruct((s, width), BF16), jax.ShapeDtypeStruct((s, width), BF16)],
        compiler_params=_cp(("arbitrary",)))(cq, ckv, krope, pos, freq, sign, g_ql, g_kvl, g_qh, g_kh, wq, wkv, dq, dk, dv)
    return outs[:9]


def _tn_call(a, b, name, shard_cols=None, rider=None):
    s, ka = a.shape
    nb = b.shape[1]
    ti = _pick(ka, (512, 256, 128))
    if shard_cols is not None:
        tj = shard_cols
    else:
        tj = nb if nb <= TN_MAX_COLS else _pick(nb, (2176, 1024, 512, 256, 128))
    ts = s if 2 * s * (ti + tj) * a.dtype.itemsize <= TN_OPERAND_BYTES else s // 2
    ns = s // ts

    def body(a_ref, b_ref, o_ref, acc_ref):
        part = _dot_tn(a_ref[...].astype(BF16), b_ref[...].astype(BF16))
        if ns == 1:
            o_ref[...] = part.astype(o_ref.dtype)
            return

        @pl.when(pl.program_id(2) == 0)
        def _():
            acc_ref[...] = part

        @pl.when(pl.program_id(2) != 0)
        def _():
            acc_ref[...] += part

        @pl.when(pl.program_id(2) == ns - 1)
        def _():
            o_ref[...] = acc_ref[...].astype(o_ref.dtype)

    if shard_cols is None:
        out_spec = pl.BlockSpec((ti, tj), lambda i, j, t: (i, j))
        out_shape = jax.ShapeDtypeStruct((ka, nb), BF16)
    else:
        out_spec = pl.BlockSpec((None, ti, tj), lambda i, j, t: (j, i, 0))
        out_shape = jax.ShapeDtypeStruct((nb // tj, ka, tj), BF16)
    (out,), got = _with_rider(
        body, rider, name=name, grid=(ka // ti, nb // tj, ns),
        in_specs=[pl.BlockSpec((ts, ti), lambda i, j, t: (t, i)), pl.BlockSpec((ts, tj), lambda i, j, t: (t, j))],
        out_specs=[out_spec], out_shape=[out_shape], scratch=[pltpu.VMEM((ti, tj), F32)], args=(a, b),
        sem=("parallel", "parallel", "arbitrary"))
    return out if rider is None else (out, got)


def _sum_call(parts, out_dtype, name):
    n, r, w = parts.shape
    tr = _row_tile(r)

    def body(p_ref, o_ref):
        acc = p_ref[0].astype(F32)
        for k in range(1, n):
            acc = acc + p_ref[k].astype(F32)
        o_ref[...] = acc.astype(out_dtype)

    return pl.pallas_call(
        body, name=name, grid=(r // tr,),
        in_specs=[pl.BlockSpec((n, tr, w), lambda i: (0, i, 0))], out_specs=_rows(tr, w),
        out_shape=jax.ShapeDtypeStruct((r, w), out_dtype), compiler_params=_cp(("parallel",)))(parts)


def _chip_sum_call(by_chip, core, name):
    n, r, w = by_chip.shape
    tr = _row_tile(r)
    nblk = r // tr

    def body(c_ref, p_ref, o_ref):
        acc = p_ref[0].astype(F32)
        for k in range(1, n):
            acc = acc + p_ref[k].astype(F32)
        o_ref[...] = acc

    return pl.pallas_call(
        body, name=name,
        grid_spec=pltpu.PrefetchScalarGridSpec(
            num_scalar_prefetch=1, grid=(nblk,),
            in_specs=[pl.BlockSpec((n, tr, w), lambda i, c_ref: (0, i, 0))],
            out_specs=pl.BlockSpec((tr, w), lambda i, c_ref: (c_ref[0] * nblk + i, 0))),
        out_shape=jax.ShapeDtypeStruct((2 * r, w), F32),
        compiler_params=_cp(("parallel",)))(core.reshape(1).astype(jnp.int32), by_chip)


def _pair_sum_call(full, other, core, out_dtype, name):
    n, r, w = other.shape
    tr = _row_tile(r)
    nblk = r // tr

    def body(c_ref, a_ref, b_ref, o_ref):
        o_ref[...] = (a_ref[...].astype(F32) + b_ref[...].astype(F32)).astype(out_dtype)

    spec = pl.BlockSpec((None, tr, w), lambda k, i, c_ref: (k, i, 0))
    return pl.pallas_call(
        body, name=name,
        grid_spec=pltpu.PrefetchScalarGridSpec(
            num_scalar_prefetch=1, grid=(n, nblk),
            in_specs=[pl.BlockSpec((None, tr, w), lambda k, i, c_ref: (k, c_ref[0] * nblk + i, 0)), spec],
            out_specs=spec),
        out_shape=jax.ShapeDtypeStruct((n, r, w), out_dtype),
        compiler_params=_cp(("parallel", "parallel")))(core.reshape(1).astype(jnp.int32), full, other)


def _adamw_call(w, g, row0, m, v, name):
    r, c = w.shape
    tr = _pick(math.gcd(r, row0) if row0 else r, (256, 128, 64, 32, 16, 8))
    off = row0 // tr

    def body(w_ref, g_ref, m_ref, v_ref, g_out_ref, d_ref, nm_ref, nv_ref):
        gg = g_ref[...]
        g_out_ref[...] = gg
        nm = ADAM_B1 * m_ref[...] + (1.0 - ADAM_B1) * gg
        nv = ADAM_B2 * v_ref[...] + (1.0 - ADAM_B2) * (gg * gg)
        m_hat = nm / (1.0 - ADAM_B1 ** ADAM_STEP)
        v_hat = nv / (1.0 - ADAM_B2 ** ADAM_STEP)
        d_ref[...] = -ADAM_LR * (m_hat / (jnp.sqrt(v_hat) + ADAM_EPS) + ADAM_WD * w_ref[...])
        nm_ref[...] = nm
        nv_ref[...] = nv

    out = jax.ShapeDtypeStruct((r, c), F32)
    g_spec = pl.BlockSpec((tr, c), lambda i: (off + i, 0))
    return pl.pallas_call(
        body, name=name, grid=(r // tr,), in_specs=[_rows(tr, c), g_spec, _rows(tr, c), _rows(tr, c)],
        out_specs=[_rows(tr, c)] * 4, out_shape=[out, out, out, out], compiler_params=_cp(("parallel",)))(w, g, m, v)


def _position():
    x, y, c = lax.axis_index("x"), lax.axis_index("y"), lax.axis_index("c")
    chips = [(1 - x, y), (x, 1 - y), (1 - x, 1 - y)]
    return x, y, c, chips


def _gather_rider(parts):
    n = len(parts)
    pairs = [(j, k) for j in range(3) for k in range(n)]

    def piece(out_refs, k, chip, core):
        half = parts[k].shape[0] // 2
        return out_refs[k].at[2 * chip[0] + chip[1], pl.ds(core * half, half), :]

    def over_ici(in_refs, out_refs, sems, j, k):
        x, y, c, chips = _position()
        half = parts[k].shape[0] // 2
        return pltpu.make_async_remote_copy(
            src_ref=in_refs[k].at[pl.ds(c * half, half), :], dst_ref=piece(out_refs, k, (x, y), c),
            send_sem=sems[0].at[n * j + k], recv_sem=sems[1].at[n * j + k], device_id=(*chips[j], c), device_id_type=MESH)

    def to_sibling(out_refs, sems, j, k):
        x, y, c, chips = _position()
        landed = piece(out_refs, k, chips[j], c)
        return pltpu.make_async_remote_copy(
            src_ref=landed, dst_ref=landed, send_sem=sems[2].at[n * j + k], recv_sem=sems[3].at[n * j + k],
            device_id=(x, y, 1 - c), device_id_type=MESH)

    def start(in_refs, out_refs, sems):
        for j, k in pairs:
            over_ici(in_refs, out_refs, sems, j, k).start()

    def finish(in_refs, out_refs, sems):
        for j, k in pairs:
            over_ici(in_refs, out_refs, sems, j, k).wait_recv()
            to_sibling(out_refs, sems, j, k).start()
        for j, k in pairs:
            to_sibling(out_refs, sems, j, k).wait_recv()
        for j, k in pairs:
            over_ici(in_refs, out_refs, sems, j, k).wait_send()
            to_sibling(out_refs, sems, j, k).wait_send()

    return _Rider(list(parts), [jax.ShapeDtypeStruct((N_CHIPS,) + p.shape, p.dtype) for p in parts], [3 * n] * 4,
                  start, finish)


def _scatter_rider(parts):
    n = len(parts)
    pairs = [(j, k) for j in range(3) for k in range(n)]

    def copy(in_refs, out_refs, sems, j, k):
        x, y, c, chips = _position()
        return pltpu.make_async_remote_copy(
            src_ref=in_refs[k].at[2 * chips[j][0] + chips[j][1]], dst_ref=out_refs[k].at[2 * x + y],
            send_sem=sems[0].at[n * j + k], recv_sem=sems[1].at[n * j + k], device_id=(*chips[j], c), device_id_type=MESH)

    def start(in_refs, out_refs, sems):
        for j, k in pairs:
            copy(in_refs, out_refs, sems, j, k).start()

    def finish(in_refs, out_refs, sems):
        for j, k in pairs:
            copy(in_refs, out_refs, sems, j, k).wait()

    return _Rider(list(parts), [jax.ShapeDtypeStruct(p.shape, p.dtype) for p in parts], [3 * n] * 2, start, finish)


def _pair_send_call(parts, name):
    n = len(parts)

    def body(*refs):
        in_refs, out_refs = refs[:n], refs[n:2 * n]
        send_sems, recv_sems = refs[2 * n:]
        x, y, c, _ = _position()
        copies = []
        for k in range(n):
            half = parts[k].shape[1] // 2
            cp = pltpu.make_async_remote_copy(
                src_ref=in_refs[k].at[:, pl.ds((1 - c) * half, half), :], dst_ref=out_refs[k],
                send_sem=send_sems.at[k], recv_sem=recv_sems.at[k], device_id=(x, y, 1 - c), device_id_type=MESH)
            cp.start()
            copies.append(cp)
        for cp in copies:
            cp.wait()

    sems = pltpu.SemaphoreType.DMA((n,))
    return pl.pallas_call(
        body, name=name, in_specs=[HBM] * n, out_specs=[HBM] * n,
        out_shape=[jax.ShapeDtypeStruct((p.shape[0], p.shape[1] // 2, p.shape[2]), p.dtype) for p in parts],
        scratch_shapes=[sems, sems])(*parts)


def _pair_swap_call(parts, name):
    n = len(parts)

    def body(*refs):
        out_refs = refs[n:2 * n]
        send_sems, recv_sems = refs[2 * n:]
        x, y, c, _ = _position()
        copies = []
        for k in range(n):
            half = parts[k].shape[0] // 2
            mine = out_refs[k].at[pl.ds(c * half, half), :]
            cp = pltpu.make_async_remote_copy(
                src_ref=mine, dst_ref=mine, send_sem=send_sems.at[k], recv_sem=recv_sems.at[k],
                device_id=(x, y, 1 - c), device_id_type=MESH)
            cp.start()
            copies.append(cp)
        for cp in copies:
            cp.wait()

    sems = pltpu.SemaphoreType.DMA((n,))
    return pl.pallas_call(
        body, name=name, in_specs=[HBM] * n, out_specs=[HBM] * n,
        out_shape=[jax.ShapeDtypeStruct(p.shape, p.dtype) for p in parts],
        input_output_aliases={k: k for k in range(n)},
        scratch_shapes=[sems, sems])(*parts)


def _all_gather_small_call(block, name):
    r, w = block.shape

    def body(in_ref, out_ref, send_sems, recv_sems, local_sem):
        x, y, c, _ = _position()
        me = 4 * x + 2 * y + c
        own = pltpu.make_async_copy(in_ref, out_ref.at[me], local_sem)
        own.start()
        copies = []
        for k in range(1, 8):
            peer = (x ^ (k >> 2), y ^ ((k >> 1) & 1), c ^ (k & 1))
            cp = pltpu.make_async_remote_copy(
                src_ref=in_ref, dst_ref=out_ref.at[me], send_sem=send_sems.at[k - 1], recv_sem=recv_sems.at[k - 1],
                device_id=peer, device_id_type=MESH)
            cp.start()
            copies.append(cp)
        for cp in copies:
            cp.wait()
        own.wait()

    return pl.pallas_call(
        body, name=name, in_specs=[HBM], out_specs=HBM,
        out_shape=jax.ShapeDtypeStruct((8, r, w), block.dtype),
        scratch_shapes=[pltpu.SemaphoreType.DMA((7,)), pltpu.SemaphoreType.DMA((7,)), pltpu.SemaphoreType.DMA])(block)


BIG = {
    "ffn1_w_in": ((D_MODEL, 2 * D_FF), 1), "ffn1_w_out": ((D_FF, D_MODEL), 0),
    "w_in": ((D_MODEL, 4256), 1), "w_q_up": ((Q_LORA, HEADS * MLA_QK), 1), "w_kv_up": ((KV_LORA, 1024), 1),
    "w_branch_mla": ((512, D_MODEL), 1), "w_branch_sb": ((SB_WIDTH, D_MODEL), 1), "w_out": ((D_MODEL, D_MODEL), 0),
    "ffn2_w_in": ((D_MODEL, 2 * D_FF), 1), "ffn2_w_out": ((D_FF, D_MODEL), 0),
    "w_ple_gate": ((D_MODEL, D_MODEL), 0), "w_ple_proj": ((PLE_DIM, D_MODEL), 1),
}
GAINS = {"ffn1_norm": 1024, "mix_norm": 1024, "q_latent_norm": 384, "kv_latent_norm": 256, "q_head_norm": 96,
         "k_head_norm": 96, "ffn2_norm": 1024, "ple_norm": 1024}
WEIGHT_ORDER = ["ffn1_norm", "ffn1_w_in", "ffn1_w_out", "mix_norm", "w_in", "q_latent_norm", "w_q_up",
                "kv_latent_norm", "w_kv_up", "q_head_norm", "k_head_norm", "w_branch_mla", "w_branch_sb", "w_out",
                "ffn2_norm", "ffn2_w_in", "ffn2_w_out", "ple_norm", "w_ple_gate", "w_ple_proj"]


def _shard_shape(name):
    (r, c), axis = BIG[name]
    return (r // N_CHIPS, c) if axis == 0 else (r, c // N_CHIPS)


GATHER_GROUPS = [
    [("ffn1_w_in",)],
    [("ffn1_w_out",), ("w_in",)],
    [("w_out",), ("w_kv_up", "w_branch_mla", "w_branch_sb"), ("w_q_up",)],
    [("ffn2_w_in",), ("ffn2_w_out", "w_ple_gate"), ("w_ple_proj",)],
]
REDUCE_GROUPS = [
    [("ffn2_w_in",), ("ffn2_w_out", "w_out", "w_ple_gate"), ("w_branch_mla", "w_branch_sb", "w_ple_proj")],
    [("w_in",), ("w_kv_up",), ("w_q_up",)],
    [("ffn1_w_out",)],
    [("ffn1_w_in",)],
]


def _join_parts(shards, group):
    return [shards[part[0]] if len(part) == 1 else jnp.concatenate([shards[n] for n in part], axis=-2) for part in group]


def _part_rows(group):
    where = {}
    for k, part in enumerate(group):
        at = 0
        for n in part:
            where[n] = (k, at)
            at += _shard_shape(n)[0]
    return where


def _split_parts(parts, group):
    return {n: parts[k][..., at:at + _shard_shape(n)[0], :] for n, (k, at) in _part_rows(group).items()}


def _to_shards(name, full):
    (r, c), axis = BIG[name]
    if axis == 0:
        return full.reshape(N_CHIPS, r // N_CHIPS, c)
    return full.reshape(r, N_CHIPS, c // N_CHIPS).transpose(1, 0, 2)


def _from_shards(name, shards):
    (r, c), axis = BIG[name]
    if axis == 0:
        return shards.reshape(r, c)
    return shards.transpose(1, 0, 2).reshape(r, c)


def _relayout_w_in(w):
    d = w.shape[0]
    z = lambda n: jnp.zeros((d, n), w.dtype)
    return jnp.concatenate([w[:, :640], z(MLA_NOPE), w[:, 640:672], z(HEAD_PAD - MLA_QK), w[:, 672:]], axis=1)


def _unlayout_w_in(g):
    return jnp.concatenate([g[:, :640], g[:, 640 + MLA_NOPE:640 + MLA_QK], g[:, 768:]], axis=1)


def _pad_heads(v):
    lead = v.shape[:-1]
    return jnp.pad(v.reshape(lead + (HEADS, MLA_QK)), [(0, 0)] * len(lead) + [(0, 0), (0, HEAD_PAD - MLA_QK)]).reshape(
        lead + (HEADS * HEAD_PAD,))


SHARD_MAJOR = ("ffn1_w_in", "ffn2_w_in")


def _step(x, p, pos, tgt, gains, weights, dist):
    d = D_MODEL
    full = {} if dist is not None else {n: _to_shards(n, w) if n in SHARD_MAJOR else w for n, w in weights.items()}
    reduced = {}

    def gather_rider(g):
        if dist is None:
            return None, None
        mine = _join_parts(weights, GATHER_GROUPS[g])
        return mine, _gather_rider(mine)

    def gathered(g, mine, others):
        if dist is not None:
            parts = [lax.dynamic_update_slice_in_dim(o, m[None], dist[0], axis=0) for o, m in zip(others, mine)]
            for n, shards in _split_parts(parts, GATHER_GROUPS[g]).items():
                full[n] = shards if n in SHARD_MAJOR else _from_shards(n, shards)

    def reduce_before(g):
        if dist is None:
            return None, None
        group = REDUCE_GROUPS[g]
        shards = {n: grads[n] if grads[n].ndim == 3 else _to_shards(n, grads[n].astype(BF16)) for part in group for n in part}
        partial = _join_parts(shards, group)
        from_sibling = _pair_send_call(partial, "grads%d_pair_send" % g)
        pair_sum = [_pair_sum_call(a, b, dist[1], BF16, "grads%d_pair_sum_%d" % (g, k))
                    for k, (a, b) in enumerate(zip(partial, from_sibling))]
        return pair_sum, _scatter_rider(pair_sum)

    def reduce_after(g, pair_sum, by_chip):
        if dist is not None:
            chip, core = dist
            by_chip = [lax.dynamic_update_slice_in_dim(t, lax.dynamic_slice_in_dim(o, chip, 1, axis=0), chip, axis=0)
                       for t, o in zip(by_chip, pair_sum)]
            bufs = _pair_swap_call([_chip_sum_call(t, core, "grads%d_chip_sum_%d" % (g, k)) for k, t in enumerate(by_chip)],
                                   "grads%d_pair_swap" % g)
            for n, (k, row0) in _part_rows(REDUCE_GROUPS[g]).items():
                reduced[n] = (bufs[k], row0)

    mine, rider = gather_rider(0)
    u1, got = _norm_call(x, gains["ffn1_norm"], "norm_ffn1", rider)
    gathered(0, mine, got)
    wts = full
    inv_freq = ROPE_BASE ** (-jnp.arange(0, MLA_ROPE, 2, dtype=F32) / MLA_ROPE)
    zeros = lambda n: jnp.zeros((n,), F32)
    freq = jnp.concatenate([zeros(MLA_NOPE), inv_freq, inv_freq, zeros(HEAD_PAD - MLA_QK)])[None]
    sign = jnp.concatenate([zeros(MLA_NOPE), -jnp.ones((16,), F32), jnp.ones((16,), F32), zeros(HEAD_PAD - MLA_QK)])[None]
    pad_gain = lambda g: jnp.pad(g, ((0, 0), (0, HEAD_PAD - MLA_QK)))
    g_qh, g_kh = pad_gain(gains["q_head_norm"]), pad_gain(gains["k_head_norm"])

    mine, rider = gather_rider(1)
    (a1, b1, hm1), got = _ffn_in_call(u1, wts["ffn1_w_in"], "ffn1_in", rider)
    gathered(1, mine, got)
    mine, rider = gather_rider(2)
    (h1, um), got = _ffn_out_call(hm1, wts["ffn1_w_out"], x, gains["mix_norm"], "ffn1_out", rider)
    gathered(2, mine, got)
    w_in = _relayout_w_in(wts["w_in"])
    wq = _pad_heads(wts["w_q_up"])
    wkv = wts["w_kv_up"]
    wbm = jnp.pad(wts["w_branch_mla"].reshape(HEADS, 64, d), ((0, 0), (64, 0), (0, 0))).reshape(HEADS * HEAD_PAD, d)
    wbs, wo = wts["w_branch_sb"], wts["w_out"]
    cq, ckv, krope, sbq, sbk, sbv, gates = _mix_in_call(um, w_in, "mix_in")
    prep_args = (cq, ckv, krope, pos, freq, sign, gains["q_latent_norm"], gains["kv_latent_norm"], g_qh, g_kh, wq, wkv)
    q, k, v = _mla_prep_call(*prep_args, "mla_prep")
    mine, rider = gather_rider(3)
    (om, lse), got = _mla_fwd_call(q, k, v, "mla_fwd", rider)
    gathered(3, mine, got)
    osb = _sb_fwd_call(sbq, sbk, sbv, "sb_fwd")
    h2, bm, bs, mg, u2 = _merge_out_call(om, osb, gates, h1, wbm, wbs, wo, gains["ffn2_norm"], "merge_out")
    (a2, b2, hm2), _ = _ffn_in_call(u2, wts["ffn2_w_in"], "ffn2_in")
    (h3, _), _ = _ffn_out_call(hm2, wts["ffn2_w_out"], h2, gains["ple_norm"], "ffn2_out")

    grads, gg = {}, {}
    dh3, dh3s, un, dgl, dpp, gg["ple_norm"], sq = _ple_call(
        h3, gains["ple_norm"], wts["w_ple_gate"], p, wts["w_ple_proj"], tgt, "ple")
    grads["w_ple_gate"] = _tn_call(un, dgl, "dw_ple_gate")
    grads["w_ple_proj"] = _tn_call(p, dpp, "dw_ple_proj")

    (da2, db2), _ = _ffn_bwd_a_call(dh3s, a2, b2, wts["ffn2_w_out"], "ffn2_bwd_act")
    grads["ffn2_w_out"] = _tn_call(hm2, dh3s, "dw_ffn2_out")
    grads["ffn2_w_in"] = jnp.concatenate([_tn_call(u2, da2, "dw_ffn2_in_a", shard_cols=D_FF // 2),
                                          _tn_call(u2, db2, "dw_ffn2_in_b", shard_cols=D_FF // 2)], axis=0)
    dh2, dh2b, gg["ffn2_norm"] = _norm_bwd_call([da2, db2], [wts["ffn2_w_in"]], h2, gains["ffn2_norm"], dh3,
                                                "ffn2_bwd_norm", half_out=False)

    dgates, dbm, dbs, dom, dos = _merge_bwd_call(dh2b, gates, bm, bs, wo, wbm, wbs, "merge_bwd")
    grads["w_out"] = _tn_call(mg, dh2b, "dw_out")
    grads["w_branch_mla"] = _tn_call(om, dbm, "dw_branch_mla").reshape(HEADS, HEAD_PAD, d)[:, 64:, :].reshape(512, d)
    grads["w_branch_sb"] = _tn_call(osb, dbs, "dw_branch_sb")
    pair_sum, rider = reduce_before(0)
    (dq, dk, dv), got = _mla_bwd_call(q, k, v, om, dom, lse, "mla_bwd", rider)
    reduce_after(0, pair_sum, got)
    dsq, dsk, dsv = _sb_bwd_call(sbq, sbk, sbv, dos, osb, "sb_bwd")
    (dcq, dckv, dkr, dwq, grads["w_kv_up"], gg["q_latent_norm"], gg["kv_latent_norm"], dgqh, dgkh) = \
        _mla_prep_bwd_call(*prep_args, dq, dk, dv, "mla_prep_bwd")
    grads["w_q_up"] = dwq.reshape(Q_LORA, HEADS, HEAD_PAD)[:, :, :MLA_QK].reshape(Q_LORA, HEADS * MLA_QK)
    gg["q_head_norm"], gg["k_head_norm"] = dgqh[:, :MLA_QK], dgkh[:, :MLA_QK]
    dproj = jnp.concatenate([dcq, dckv, dkr, dsq, dsk.astype(BF16), dsv.astype(BF16), dgates], axis=1)
    grads["w_in"] = _unlayout_w_in(_tn_call(um, dproj, "dw_in"))
    dh1, dh1s, gg["mix_norm"] = _norm_bwd_call([dproj], [w_in], h1, gains["mix_norm"], dh2, "mix_bwd_norm", half_out=True)

    pair_sum, rider = reduce_before(1)
    (da1, db1), got = _ffn_bwd_a_call(dh1s, a1, b1, wts["ffn1_w_out"], "ffn1_bwd_act", rider)
    reduce_after(1, pair_sum, got)
    grads["ffn1_w_out"] = _tn_call(hm1, dh1s, "dw_ffn1_out")
    pair_sum, rider = reduce_before(2)
    res = _tn_call(u1, da1, "dw_ffn1_in_a", shard_cols=D_FF // 2, rider=rider)
    dwa, got = (res, None) if rider is None else res
    reduce_after(2, pair_sum, got)
    grads["ffn1_w_in"] = jnp.concatenate([dwa, _tn_call(u1, db1, "dw_ffn1_in_b", shard_cols=D_FF // 2)], axis=0)
    pair_sum, rider = reduce_before(3)
    res = _norm_bwd_call([da1, db1], [wts["ffn1_w_in"]], x, gains["ffn1_norm"], dh1, "ffn1_bwd_norm",
                         half_out=False, rider=rider)
    (dx, _, gg["ffn1_norm"]), got = (res, None) if rider is None else res
    reduce_after(3, pair_sum, got)
    return sq, dx, gg, (grads if dist is None else reduced)


def kernel(x, p, positions, ffn1_norm, ffn1_w_in, ffn1_w_out, mix_norm, w_in, q_latent_norm, w_q_up, kv_latent_norm, w_kv_up, q_head_norm, k_head_norm, w_branch_mla, w_branch_sb, w_out, ffn2_norm, ffn2_w_in, ffn2_w_out, ple_norm, w_ple_gate, w_ple_proj, loss_target, m_ffn1_norm, m_ffn1_w_in, m_ffn1_w_out, m_mix_norm, m_w_in, m_q_latent_norm, m_w_q_up, m_kv_latent_norm, m_w_kv_up, m_q_head_norm, m_k_head_norm, m_w_branch_mla, m_w_branch_sb, m_w_out, m_ffn2_norm, m_ffn2_w_in, m_ffn2_w_out, m_ple_norm, m_w_ple_gate, m_w_ple_proj, v_ffn1_norm, v_ffn1_w_in, v_ffn1_w_out, v_mix_norm, v_w_in, v_q_latent_norm, v_w_q_up, v_kv_latent_norm, v_w_kv_up, v_q_head_norm, v_k_head_norm, v_w_branch_mla, v_w_branch_sb, v_w_out, v_ffn2_norm, v_ffn2_w_in, v_ffn2_w_out, v_ple_norm, v_w_ple_gate, v_w_ple_proj):
    given = dict(locals())
    w_shard = {n: given[n][0] for n in WEIGHT_ORDER}
    m_shard = {n: given["m_" + n][0] for n in WEIGHT_ORDER}
    v_shard = {n: given["v_" + n][0] for n in WEIGHT_ORDER}
    gains = {n: w_shard[n][None] for n in GAINS}

    chip = 2 * lax.axis_index("x") + lax.axis_index("y")
    sq, dx, gain_grads, reduced = _step(x[0], p[0, 0], positions.reshape(-1, 1), loss_target[0], gains,
                                        {n: w_shard[n].astype(BF16) for n in BIG}, (chip, lax.axis_index("c")))

    rows = [jnp.pad(gain_grads[n], ((0, 0), (0, D_MODEL - GAINS[n]))) for n in GAINS] + [sq]
    gain_block = jnp.concatenate(rows + [jnp.zeros((16 - len(rows), D_MODEL), F32)], axis=0)
    gain_sum = _sum_call(_all_gather_small_call(gain_block, "gains_all_gather"), F32, "gains_sum")
    loss = 0.5 * jnp.sum(gain_sum[len(GAINS)]) / D_MODEL

    outs = {"grad": {}, "delta": {}, "new_m": {}, "new_v": {}}
    gain_pack = lambda t: jnp.concatenate([jnp.pad(t[n][None], ((0, 0), (0, D_MODEL - GAINS[n]))) for n in GAINS], axis=0)
    packed = _adamw_call(gain_pack(w_shard), gain_sum, 0, gain_pack(m_shard), gain_pack(v_shard), "adamw_gains")
    for i, n in enumerate(GAINS):
        for kind, t in zip(("grad", "delta", "new_m", "new_v"), packed):
            outs[kind][n] = t[i, :GAINS[n]][None]
    for n in BIG:
        buf, row0 = reduced[n]
        for kind, t in zip(("grad", "delta", "new_m", "new_v"),
                           _adamw_call(w_shard[n], buf, row0, m_shard[n], v_shard[n], "adamw_" + n)):
            outs[kind][n] = t[None]

    return (loss, dx[None], *[outs["grad"][n] for n in WEIGHT_ORDER], *[outs["delta"][n] for n in WEIGHT_ORDER],
            *[outs["new_m"][n] for n in WEIGHT_ORDER], *[outs["new_v"][n] for n in WEIGHT_ORDER])
```

```python
import collections
import functools
import math

import jax
import jax.numpy as jnp
from jax import lax
from jax.experimental import pallas as pl
from jax.experimental.pallas import tpu as pltpu

F32 = jnp.float32
BF16 = jnp.bfloat16
MESH = pl.DeviceIdType.MESH

D_MODEL = 1024
D_FF = 2816
PLE_DIM = 256
NORM_EPS = 1e-6
HEADS = 8
MLA_NOPE = 64
MLA_ROPE = 32
MLA_QK = 96
Q_LORA = 384
KV_LORA = 256
SB_WIDTH = 512
ROPE_BASE = 10000.0
LOG2_E = math.log2(math.e)
HEAD_PAD = 128
N_CHIPS = 4

ADAM_LR = 0.001
ADAM_B1 = 0.9
ADAM_B2 = 0.999
ADAM_EPS = 1e-08
ADAM_WD = 0.01
ADAM_STEP = 10

SEG_CQ = (0, 384)
SEG_CKV = (384, 256)
SEG_KROPE = (640, 128)
SEG_SBQ = (768, 512)
SEG_SBK = (1280, 512)
SEG_SBV = (1792, 512)
SEG_GATES = (2304, 2048)
IN_COLS_PAD = 4352

TM = 512
TM_SMALL = 512
TM_PREP_BWD = 256
TQ = 256
MLA_FWD_BLOCKS = 4
MLA_BWD_BLOCKS = 4
SB_FWD_BLOCKS = 4
SB_BWD_BLOCKS = 2
SB_HEAD = 64
SB_SCALE = 0.125
SB_DEAD = -104.0
COL_CHUNK = 256
TN_MAX_COLS = 2816
TN_OPERAND_BYTES = 34 * 1024 * 1024
MAX_ROW_TILE = 512
VMEM_LIMIT = 56 * 1024 * 1024

NT = (((1,), (1,)), ((), ()))
TN = (((0,), (0,)), ((), ()))


def _cp(sem):
    return pltpu.CompilerParams(dimension_semantics=sem, vmem_limit_bytes=VMEM_LIMIT)


def _rows(tm, w):
    return pl.BlockSpec((tm, w), lambda i: (i, 0))


def _whole(shape):
    return pl.BlockSpec(shape, lambda i: (0,) * len(shape))


def _dot(a, b):
    return jnp.dot(a, b, preferred_element_type=F32)


def _dot_nt(a, b):
    return lax.dot_general(a, b, NT, preferred_element_type=F32)


def _dot_tn(a, b):
    return lax.dot_general(a, b, TN, preferred_element_type=F32)


def _rstd(x, n):
    return lax.rsqrt(jnp.sum(x * x, axis=-1, keepdims=True) / n + NORM_EPS)


def _rms_bwd(x, r, g, dy, n):
    gy = dy * g
    return r * gy - x * ((r * r * r) * (jnp.sum(x * gy, axis=-1, keepdims=True) / n))


def _sigmoid(x):
    return jax.nn.sigmoid(x)


def _pick(n, cands):
    for c in cands:
        if n % c == 0:
            return c
    return n


def _row_tile(r):
    for t in range(min(r, MAX_ROW_TILE) // 16 * 16, 15, -16):
        if r % t == 0:
            return t
    return r


HBM = pl.BlockSpec(memory_space=pl.ANY)

_Rider = collections.namedtuple("_Rider", "ins out_shape sems start finish")


def _with_rider(body, rider, *, name, grid, in_specs, out_specs, out_shape, args, sem, scratch=()):
    if rider is None:
        return pl.pallas_call(body, name=name, grid=grid, in_specs=in_specs, out_specs=out_specs, out_shape=out_shape,
                              scratch_shapes=list(scratch), compiler_params=_cp(sem))(*args), None
    ni, no, nri, nro = len(in_specs), len(out_specs), len(rider.ins), len(rider.out_shape)

    def riding(*refs):
        ins, r_ins = refs[:ni], refs[ni:ni + nri]
        outs, r_outs = refs[ni + nri:ni + nri + no], refs[ni + nri + no:ni + nri + no + nro]
        scr = refs[ni + nri + no + nro:ni + nri + no + nro + len(scratch)]
        sems = refs[ni + nri + no + nro + len(scratch):]
        ids = [pl.program_id(a) for a in range(len(grid))]
        first = functools.reduce(jnp.logical_and, [i == 0 for i in ids])
        last = functools.reduce(jnp.logical_and, [i == g - 1 for i, g in zip(ids, grid)])

        @pl.when(first)
        def _():
            rider.start(r_ins, r_outs, sems)

        body(*ins, *outs, *scr)

        @pl.when(last)
        def _():
            rider.finish(r_ins, r_outs, sems)

    res = pl.pallas_call(
        riding, name=name, grid=grid, in_specs=list(in_specs) + [HBM] * nri, out_specs=list(out_specs) + [HBM] * nro,
        out_shape=list(out_shape) + list(rider.out_shape),
        scratch_shapes=list(scratch) + [pltpu.SemaphoreType.DMA((k,)) for k in rider.sems],
        compiler_params=_cp(("arbitrary",) * len(grid)))(*args, *rider.ins)
    return res[:no], res[no:]


def _norm_call(h, g, name, rider=None):
    s, d = h.shape
    tm = min(TM, s)

    def body(h_ref, g_ref, u_ref):
        x = h_ref[...]
        u_ref[...] = ((x * _rstd(x, d)) * g_ref[...]).astype(BF16)

    (u,), got = _with_rider(
        body, rider, name=name, grid=(s // tm,),
        in_specs=[_rows(tm, d), _whole((1, d))], out_specs=[_rows(tm, d)],
        out_shape=[jax.ShapeDtypeStruct((s, d), BF16)], args=(h, g), sem=("parallel",))
    return u, got


def _ffn_in_call(u, w, name, rider=None):
    s, d = u.shape
    tn = w.shape[2]
    nj = w.shape[0] // 2
    n = nj * tn
    tm = min(TM, s)

    def body(u_ref, wa_ref, wb_ref, a_ref, b_ref, hm_ref):
        uu = u_ref[...]
        a = _dot(uu, wa_ref[...])
        b = _dot(uu, wb_ref[...])
        a_ref[...] = a
        b_ref[...] = b
        hm_ref[...] = ((a * _sigmoid(a)) * b).astype(BF16)

    blk = pl.BlockSpec((tm, tn), lambda j, i: (i, j))
    return _with_rider(
        body, rider, name=name, grid=(nj, s // tm),
        in_specs=[pl.BlockSpec((tm, d), lambda j, i: (i, 0)),
                  pl.BlockSpec((None, d, tn), lambda j, i: (j, 0, 0)),
                  pl.BlockSpec((None, d, tn), lambda j, i: (j + nj, 0, 0))],
        out_specs=[blk, blk, blk],
        out_shape=[jax.ShapeDtypeStruct((s, n), F32), jax.ShapeDtypeStruct((s, n), F32),
                   jax.ShapeDtypeStruct((s, n), BF16)],
        args=(u, w, w), sem=("parallel", "parallel"))


def _ffn_out_call(hm, w, h, gain, name, rider=None):
    s, n = hm.shape
    d = w.shape[1]
    tm = min(TM, s)

    def body(hm_ref, w_ref, h_ref, g_ref, o_ref, u_ref):
        x = h_ref[...] + 0.5 * _dot(hm_ref[...], w_ref[...])
        o_ref[...] = x
        u_ref[...] = ((x * _rstd(x, d)) * g_ref[...]).astype(BF16)

    return _with_rider(
        body, rider, name=name, grid=(s // tm,),
        in_specs=[_rows(tm, n), _whole((n, d)), _rows(tm, d), _whole((1, d))], out_specs=[_rows(tm, d), _rows(tm, d)],
        out_shape=[jax.ShapeDtypeStruct((s, d), F32), jax.ShapeDtypeStruct((s, d), BF16)], args=(hm, w, h, gain),
        sem=("parallel",))


def _mix_in_call(u, w, name):
    s, d = u.shape
    tm = min(TM_SMALL, s)
    segs = [(SEG_CQ, F32), (SEG_CKV, F32), (SEG_KROPE, F32), (SEG_SBQ, BF16), (SEG_SBK, BF16),
            (SEG_SBV, BF16), (SEG_GATES, F32)]

    def body(u_ref, w_ref, *outs):
        uu = u_ref[...]
        for ((off, width), _), o_ref in zip(segs, outs):
            o_ref[...] = _dot(uu, w_ref[:, off:off + width]).astype(o_ref.dtype)

    return pl.pallas_call(
        body, name=name, grid=(s // tm,),
        in_specs=[_rows(tm, d), _whole((d, IN_COLS_PAD))],
        out_specs=[_rows(tm, width) for (_, width), _ in segs],
        out_shape=[jax.ShapeDtypeStruct((s, width), dt) for (_, width), dt in segs],
        compiler_params=_cp(("parallel",)))(u, w)


def _lane(shape):
    return lax.broadcasted_iota(jnp.int32, shape, len(shape) - 1)


def _rot_half(y):
    lane = _lane(y.shape)
    swapped = jnp.where(lane < MLA_NOPE + MLA_ROPE // 2, pltpu.roll(y, HEAD_PAD - 16, 1), pltpu.roll(y, 16, 1))
    return jnp.where((lane >= MLA_NOPE) & (lane < MLA_QK), swapped, 0.0)


def _rope_tables(pos_ref, freq_ref, sign_ref):
    ang = pos_ref[...].astype(F32) * freq_ref[...]
    return jnp.cos(ang), jnp.sin(ang) * sign_ref[...]


def _head_fwd(x, g, cosv, ssv):
    r = _rstd(x, MLA_QK)
    y = (x * r) * g
    return y * cosv + _rot_half(y) * ssv, r


def _head_bwd(x, r, g, cosv, ssv, dout):
    dy = dout * cosv + _rot_half(dout * ssv)
    return _rms_bwd(x, r, g, dy, MLA_QK), jnp.sum(dy * (x * r), axis=0, keepdims=True)


def _mla_prep_call(cq, ckv, krope, pos, freq, sign, g_ql, g_kvl, g_qh, g_kh, wq, wkv, name):
    s = cq.shape[0]
    tm = min(TM_SMALL, s)
    width = HEADS * HEAD_PAD

    def body(cq_ref, ckv_ref, kr_ref, pos_ref, freq_ref, sign_ref, gql_ref, gkvl_ref, gqh_ref, gkh_ref,
             wq_ref, wkv_ref, q_ref, k_ref, v_ref):
        cosv, ssv = _rope_tables(pos_ref, freq_ref, sign_ref)
        x = cq_ref[...]
        qr = _dot(((x * _rstd(x, Q_LORA)) * gql_ref[...]).astype(BF16), wq_ref[...])
        x = ckv_ref[...]
        kv = _dot(((x * _rstd(x, KV_LORA)) * gkvl_ref[...]).astype(BF16), wkv_ref[...])
        kr = kr_ref[...]
        lane = _lane((tm, HEAD_PAD))
        for h in range(HEADS):
            sl = slice(h * HEAD_PAD, (h + 1) * HEAD_PAD)
            qh, _ = _head_fwd(qr[:, sl], gqh_ref[...], cosv, ssv)
            q_ref[:, sl] = qh.astype(BF16)
            kvh = kv[:, sl]
            kh, _ = _head_fwd(jnp.where(lane < MLA_NOPE, kvh, kr), gkh_ref[...], cosv, ssv)
            k_ref[:, sl] = kh.astype(BF16)
            v_ref[:, sl] = jnp.where(lane >= MLA_NOPE, kvh, jnp.where(lane == 0, 1.0, 0.0)).astype(BF16)

    out = jax.ShapeDtypeStruct((s, width), BF16)
    return pl.pallas_call(
        body, name=name, grid=(s // tm,),
        in_specs=[_rows(tm, Q_LORA), _rows(tm, KV_LORA), _rows(tm, HEAD_PAD), _rows(tm, 1),
                  _whole((1, HEAD_PAD)), _whole((1, HEAD_PAD)), _whole((1, Q_LORA)), _whole((1, KV_LORA)),
                  _whole((1, HEAD_PAD)), _whole((1, HEAD_PAD)), _whole((Q_LORA, width)), _whole((KV_LORA, width))],
        out_specs=[_rows(tm, width)] * 3, out_shape=[out, out, out],
        compiler_params=_cp(("parallel",)))(cq, ckv, krope, pos, freq, sign, g_ql, g_kvl, g_qh, g_kh, wq, wkv)


def _attn_specs(s, nb):
    qspec = pl.BlockSpec((TQ, nb * HEAD_PAD), lambda g, i: (i, g))
    kspec = pl.BlockSpec((s, nb * HEAD_PAD), lambda g, i: (0, g))
    return qspec, kspec


def _lanes(b):
    return slice(b * HEAD_PAD, (b + 1) * HEAD_PAD)


def _tri(cmp):
    r = lax.broadcasted_iota(jnp.int32, (TQ, TQ), 0)
    c = lax.broadcasted_iota(jnp.int32, (TQ, TQ), 1)
    return cmp(r, c)


def _mla_fwd_call(q, k, v, name, rider=None):
    s, width = q.shape
    scale = 1.0 / math.sqrt(MLA_QK)

    nb = MLA_FWD_BLOCKS

    def body(q_ref, k_ref, v_ref, o_ref, lse_ref):
        qi = pl.program_id(1)
        qs = [q_ref[:, _lanes(b)] for b in range(nb)]
        causal = _tri(lambda r, c: c <= r)

        def step(kb, carry, diag):
            ks = pl.multiple_of(kb * TQ, TQ)
            heads = range(nb)
            scs = [_dot_nt(qs[b], k_ref[pl.ds(ks, TQ), _lanes(b)]) * (scale * LOG2_E) for b in heads]
            if diag:
                scs = [jnp.where(causal, sc, -1e30) for sc in scs]
            mns = [jnp.maximum(carry[b][0], jnp.max(scs[b], axis=-1, keepdims=True)) for b in heads]
            als = [jnp.exp2(carry[b][0] - mns[b]) for b in heads]
            ps = [jnp.exp2(scs[b] - mns[b]).astype(BF16) for b in heads]
            accs = [als[b] * carry[b][1] + _dot(ps[b], v_ref[pl.ds(ks, TQ), _lanes(b)]) for b in heads]
            return tuple((mns[b], accs[b]) for b in heads)

        init = tuple((jnp.full((TQ, 1), -1e30, F32), jnp.zeros((TQ, HEAD_PAD), F32)) for _ in range(nb))
        carry = step(qi, init, True)
        carry = lax.fori_loop(0, qi, lambda kb, c: step(kb, c, False), carry)
        for b in range(nb):
            m, acc = carry[b]
            l = acc[:, 0:1]
            o_ref[:, _lanes(b)] = (acc / l).astype(BF16)
            lse_ref[:, _lanes(b)] = jnp.broadcast_to(m * (1.0 / LOG2_E) + jnp.log(l), (TQ, HEAD_PAD))

    qspec, kspec = _attn_specs(s, nb)
    return _with_rider(
        body, rider, name=name, grid=(width // (nb * HEAD_PAD), s // TQ),
        in_specs=[qspec, kspec, kspec], out_specs=[qspec, qspec],
        out_shape=[jax.ShapeDtypeStruct((s, width), BF16), jax.ShapeDtypeStruct((s, width), F32)],
        args=(q, k, v), sem=("parallel", "arbitrary"))


def _mla_bwd_call(q, k, v, o, do, lse, name, rider=None):
    s, width = q.shape
    scale = 1.0 / math.sqrt(MLA_QK)
    nb = MLA_BWD_BLOCKS

    def body(q_ref, k_ref, v_ref, o_ref, do_ref, lse_ref, dq_ref, dk_ref, dv_ref):
        qi = pl.program_id(1)

        @pl.when(qi == 0)
        def _():
            dk_ref[...] = jnp.zeros_like(dk_ref)
            dv_ref[...] = jnp.zeros_like(dv_ref)

        qs = [q_ref[:, _lanes(b)] for b in range(nb)]
        dos = [do_ref[:, _lanes(b)] for b in range(nb)]
        lses = [lse_ref[:, b * HEAD_PAD:b * HEAD_PAD + 1] for b in range(nb)]
        dlts = [jnp.sum(dos[b].astype(F32) * o_ref[:, _lanes(b)].astype(F32), axis=-1, keepdims=True) for b in range(nb)]
        causal = _tri(lambda r, c: c <= r)

        def step(kb, dqs, diag):
            ks = pl.multiple_of(kb * TQ, TQ)
            heads = range(nb)
            kts = [k_ref[pl.ds(ks, TQ), _lanes(b)] for b in heads]
            scs = [_dot_nt(qs[b], kts[b]) for b in heads]
            dps = [_dot_nt(dos[b], v_ref[pl.ds(ks, TQ), _lanes(b)]) for b in heads]
            ps = [jnp.exp(scs[b] * scale - lses[b]) for b in heads]
            if diag:
                ps = [jnp.where(causal, p, 0.0) for p in ps]
            dss = [(ps[b] * (dps[b] - dlts[b]) * scale).astype(BF16) for b in heads]
            dvs = [_dot_tn(ps[b].astype(BF16), dos[b]) for b in heads]
            dks = [_dot_tn(dss[b], qs[b]) for b in heads]
            out = tuple(dqs[b] + _dot(dss[b], kts[b]) for b in heads)
            for b in heads:
                dv_ref[pl.ds(ks, TQ), _lanes(b)] += dvs[b]
                dk_ref[pl.ds(ks, TQ), _lanes(b)] += dks[b]
            return out

        dqs = step(qi, tuple(jnp.zeros((TQ, HEAD_PAD), F32) for _ in range(nb)), True)
        dqs = lax.fori_loop(0, qi, lambda kb, c: step(kb, c, False), dqs)
        for b in range(nb):
            dq_ref[:, _lanes(b)] = dqs[b]

    qspec, kspec = _attn_specs(s, nb)
    out = jax.ShapeDtypeStruct((s, width), F32)
    return _with_rider(
        body, rider, name=name, grid=(width // (nb * HEAD_PAD), s // TQ),
        in_specs=[qspec, kspec, kspec, qspec, qspec, qspec], out_specs=[qspec, kspec, kspec],
        out_shape=[out, out, out], args=(q, k, v, o, do, lse), sem=("parallel", "arbitrary"))


def _dot_hilo(x, u):
    hi = x.astype(BF16)
    lo = (x - hi.astype(F32)).astype(BF16)
    return _dot(hi, u) + _dot(lo, u)


def _sb_logs(z):
    ls = jnp.minimum(z, 0.0) - jnp.log(1.0 + jnp.exp(-jnp.abs(z)))
    return ls, ls - z


def _sb_head_q(qb, first, hh):
    keep = first if hh == 0 else jnp.logical_not(first)
    return jnp.where(keep, qb, jnp.zeros_like(qb)) * jnp.asarray(SB_SCALE, qb.dtype)


def _sb_fwd_call(q, k, v, name):
    s, width = q.shape
    nb = SB_FWD_BLOCKS
    chains = [(b, hh) for b in range(nb) for hh in range(HEAD_PAD // SB_HEAD)]

    def body(q_ref, k_ref, v_ref, o_ref):
        qi = pl.program_id(1)
        strict = _tri(lambda r, c: c < r)
        after = _tri(lambda r, c: r > c).astype(BF16)
        first = _lane((1, HEAD_PAD)) < SB_HEAD
        qhs = [_sb_head_q(q_ref[:, _lanes(b)], first, hh) for b, hh in chains]

        def step(kb, carry, diag):
            ks = pl.multiple_of(kb * TQ, TQ)
            ids = range(len(chains))
            zs = [_dot_nt(qhs[ci], k_ref[pl.ds(ks, TQ), _lanes(chains[ci][0])]) for ci in ids]
            logs = [_sb_logs(z) for z in zs]
            lss = [lg[0] for lg in logs]
            l1ms = [jnp.where(strict, lg[1], 0.0) if diag else lg[1] for lg in logs]
            sufs = [_dot_hilo(l1m, after) for l1m in l1ms]
            as_ = [jnp.exp(lss[ci] + sufs[ci] + carry[ci][0]) for ci in ids]
            if diag:
                as_ = [jnp.where(strict, a, 0.0) for a in as_]
            accs = [carry[ci][1] + _dot(as_[ci].astype(BF16), v_ref[pl.ds(ks, TQ), _lanes(chains[ci][0])]) for ci in ids]
            return tuple((carry[ci][0] + jnp.sum(l1ms[ci], axis=-1, keepdims=True), accs[ci]) for ci in ids)

        init = tuple((jnp.zeros((TQ, 1), F32), jnp.zeros((TQ, HEAD_PAD), F32)) for _ in chains)
        carry = _sb_sweep(step, qi, init)
        for b in range(nb):
            o_ref[:, _lanes(b)] = jnp.where(first, carry[2 * b][1], carry[2 * b + 1][1])

    qspec, kspec = _attn_specs(s, nb)
    return pl.pallas_call(
        body, name=name, grid=(width // (nb * HEAD_PAD), s // TQ),
        in_specs=[qspec, kspec, kspec], out_specs=qspec, out_shape=jax.ShapeDtypeStruct((s, width), F32),
        compiler_params=_cp(("parallel", "arbitrary")))(q, k, v)


def _sb_sweep(step, qi, init):
    def live(carry):
        top = carry[0][0]
        for c in carry[1:]:
            top = jnp.maximum(top, c[0])
        return jnp.max(top)

    carry = step(qi, init, True)

    def cond(state):
        j, alive, _ = state
        return jnp.logical_and(j < qi, alive > SB_DEAD)

    def body(state):
        j, _, carry = state
        carry = step(qi - 1 - j, carry, False)
        return j + 1, live(carry), carry

    return lax.while_loop(cond, body, (jnp.int32(0), live(carry), carry))[2]


def _sb_bwd_call(q, k, v, do, o, name):
    s, width = q.shape
    nb = SB_BWD_BLOCKS
    chains = [(b, hh) for b in range(nb) for hh in range(HEAD_PAD // SB_HEAD)]

    def body(q_ref, k_ref, v_ref, do_ref, o_ref, dq_ref, dk_ref, dv_ref):
        qi = pl.program_id(1)

        @pl.when(qi == 0)
        def _():
            dk_ref[...] = jnp.zeros_like(dk_ref)
            dv_ref[...] = jnp.zeros_like(dv_ref)

        strict = _tri(lambda r, c: c < r)
        after = _tri(lambda r, c: r > c).astype(BF16)
        from_here = _tri(lambda r, c: r >= c).astype(BF16)
        first = _lane((1, HEAD_PAD)) < SB_HEAD
        qhs = [_sb_head_q(q_ref[:, _lanes(b)], first, hh) for b, hh in chains]
        dohs = []
        for b, hh in chains:
            dob = do_ref[:, _lanes(b)]
            dohs.append(jnp.where(first if hh == 0 else jnp.logical_not(first), dob, jnp.zeros_like(dob)))
        gtots = [jnp.sum(dohs[ci].astype(F32) * o_ref[:, _lanes(chains[ci][0])], axis=-1, keepdims=True)
                 for ci in range(len(chains))]

        def step(kb, carry, diag):
            ks = pl.multiple_of(kb * TQ, TQ)
            ids = range(len(chains))
            kts = [k_ref[pl.ds(ks, TQ), _lanes(b)] for b, _ in chains]
            zs = [_dot_nt(qhs[ci], kts[ci]) for ci in ids]
            das = [_dot_nt(dohs[ci], v_ref[pl.ds(ks, TQ), _lanes(chains[ci][0])]) for ci in ids]
            logs = [_sb_logs(z) for z in zs]
            lss = [lg[0] for lg in logs]
            l1ms = [jnp.where(strict, lg[1], 0.0) if diag else lg[1] for lg in logs]
            sufs = [_dot_hilo(l1m, after) for l1m in l1ms]
            as_ = [jnp.exp(lss[ci] + sufs[ci] + carry[ci][0]) for ci in ids]
            if diag:
                as_ = [jnp.where(strict, a, 0.0) for a in as_]
            abs_ = [a.astype(BF16) for a in as_]
            gs = [abs_[ci].astype(F32) * das[ci] for ci in ids]
            cexs = [gtots[ci] - (carry[ci][1] + _dot_hilo(gs[ci], from_here)) for ci in ids]
            dzs = [gs[ci] - jnp.exp(lss[ci]) * (gs[ci] + cexs[ci]) for ci in ids]
            if diag:
                dzs = [jnp.where(strict, dz, 0.0) for dz in dzs]
            dzbs = [dz.astype(BF16) for dz in dzs]
            dvps = [_dot_tn(abs_[ci], dohs[ci]) for ci in ids]
            dkps = [_dot_tn(dzbs[ci], qhs[ci]) for ci in ids]
            out = tuple((carry[ci][0] + jnp.sum(l1ms[ci], axis=-1, keepdims=True),
                         carry[ci][1] + jnp.sum(gs[ci], axis=-1, keepdims=True),
                         carry[ci][2] + _dot(dzbs[ci], kts[ci])) for ci in ids)
            for b in range(nb):
                dk_ref[pl.ds(ks, TQ), _lanes(b)] += dkps[2 * b] + dkps[2 * b + 1]
                dv_ref[pl.ds(ks, TQ), _lanes(b)] += dvps[2 * b] + dvps[2 * b + 1]
            return out

        init = tuple((jnp.zeros((TQ, 1), F32), jnp.zeros((TQ, 1), F32), jnp.zeros((TQ, HEAD_PAD), F32)) for _ in chains)
        carry = _sb_sweep(step, qi, init)
        for b in range(nb):
            dq_ref[:, _lanes(b)] = (jnp.where(first, carry[2 * b][2], carry[2 * b + 1][2]) * SB_SCALE).astype(BF16)

    qspec, kspec = _attn_specs(s, nb)
    return pl.pallas_call(
        body, name=name, grid=(width // (nb * HEAD_PAD), s // TQ),
        in_specs=[qspec, kspec, kspec, qspec, qspec], out_specs=[qspec, kspec, kspec],
        out_shape=[jax.ShapeDtypeStruct((s, width), BF16), jax.ShapeDtypeStruct((s, width), F32),
                   jax.ShapeDtypeStruct((s, width), F32)],
        compiler_params=_cp(("parallel", "arbitrary")))(q, k, v, do, o)


def _merge_out_call(om, osb, gates, h, wbm, wbs, wo, gain, name):
    s, d = h.shape
    tm = min(TM_SMALL, s)

    def body(om_ref, os_ref, g_ref, h_ref, wbm_ref, wbs_ref, wo_ref, gain_ref, h2_ref, bm_ref, bs_ref, mg_ref, u_ref):
        bm = _dot(om_ref[...], wbm_ref[...])
        bs = _dot(os_ref[...].astype(BF16), wbs_ref[...])
        mg = (_sigmoid(g_ref[:, :d]) * bm + _sigmoid(g_ref[:, d:]) * bs).astype(BF16)
        bm_ref[...] = bm
        bs_ref[...] = bs
        mg_ref[...] = mg
        x = h_ref[...] + _dot(mg, wo_ref[...])
        h2_ref[...] = x
        u_ref[...] = ((x * _rstd(x, d)) * gain_ref[...]).astype(BF16)

    return pl.pallas_call(
        body, name=name, grid=(s // tm,),
        in_specs=[_rows(tm, om.shape[1]), _rows(tm, SB_WIDTH), _rows(tm, 2 * d), _rows(tm, d),
                  _whole(wbm.shape), _whole(wbs.shape), _whole(wo.shape), _whole((1, d))],
        out_specs=[_rows(tm, d)] * 5,
        out_shape=[jax.ShapeDtypeStruct((s, d), F32), jax.ShapeDtypeStruct((s, d), F32),
                   jax.ShapeDtypeStruct((s, d), F32), jax.ShapeDtypeStruct((s, d), BF16),
                   jax.ShapeDtypeStruct((s, d), BF16)],
        compiler_params=_cp(("parallel",)))(om, osb, gates, h, wbm, wbs, wo, gain)


def _ple_call(h, g, wg, p, wp, tgt, name):
    s, d = h.shape
    tm = min(TM_SMALL, s)

    def body(h_ref, g_ref, wg_ref, p_ref, wp_ref, t_ref, dh_ref, dhs_ref, un_ref, dgl_ref, dpp_ref, dg_ref, sq_ref):
        @pl.when(pl.program_id(0) == 0)
        def _():
            dg_ref[...] = jnp.zeros_like(dg_ref)
            sq_ref[...] = jnp.zeros_like(sq_ref)

        x = h_ref[...]
        gain = g_ref[...]
        r = _rstd(x, d)
        xh = x * r
        un = (xh * gain).astype(BF16)
        sg = _sigmoid(_dot(un, wg_ref[...]))
        pp = _dot(p_ref[...].astype(BF16), wp_ref[...])
        diff = (x + sg * pp) - t_ref[...]
        sq_ref[...] += jnp.sum(diff * diff, axis=0, keepdims=True)
        dy = diff * (1.0 / d)
        dgl = ((dy * pp) * (sg * (1.0 - sg))).astype(BF16)
        dun = _dot_nt(dgl, wg_ref[...])
        dg_ref[...] += jnp.sum(dun * xh, axis=0, keepdims=True)
        dh = dy + _rms_bwd(x, r, gain, dun, d)
        dh_ref[...] = dh
        dhs_ref[...] = (0.5 * dh).astype(BF16)
        un_ref[...] = un
        dgl_ref[...] = dgl
        dpp_ref[...] = (dy * sg).astype(BF16)

    bf = jax.ShapeDtypeStruct((s, d), BF16)
    vec = jax.ShapeDtypeStruct((1, d), F32)
    return pl.pallas_call(
        body, name=name, grid=(s // tm,),
        in_specs=[_rows(tm, d), _whole((1, d)), _whole(wg.shape), _rows(tm, PLE_DIM), _whole(wp.shape), _rows(tm, d)],
        out_specs=[_rows(tm, d)] * 5 + [_whole((1, d))] * 2,
        out_shape=[jax.ShapeDtypeStruct((s, d), F32), bf, bf, bf, bf, vec, vec],
        compiler_params=_cp(("arbitrary",)))(h, g, wg, p, wp, tgt)


def _ffn_bwd_a_call(dhs, a, b, wo, name, rider=None):
    s, n = a.shape
    d = dhs.shape[1]
    tn = n // 2
    tm = min(TM, s)

    def body(dh_ref, a_ref, b_ref, wo_ref, da_ref, db_ref):
        dh = dh_ref[...]
        chunks = [slice(c0, min(c0 + COL_CHUNK, tn)) for c0 in range(0, tn, COL_CHUNK)]
        dhms = [_dot_nt(dh, wo_ref[sl, :]) for sl in chunks]
        for sl, dhm in zip(chunks, dhms):
            av = a_ref[:, sl]
            sa = _sigmoid(av)
            da_ref[:, sl] = (dhm * b_ref[:, sl] * (sa * (1.0 + av * (1.0 - sa)))).astype(BF16)
            db_ref[:, sl] = (dhm * (av * sa)).astype(BF16)

    blk = pl.BlockSpec((tm, tn), lambda j, i: (i, j))
    return _with_rider(
        body, rider, name=name, grid=(n // tn, s // tm),
        in_specs=[pl.BlockSpec((tm, d), lambda j, i: (i, 0)), blk, blk, pl.BlockSpec((tn, d), lambda j, i: (j, 0))],
        out_specs=[blk, blk],
        out_shape=[jax.ShapeDtypeStruct((s, n), BF16)] * 2, args=(dhs, a, b, wo), sem=("parallel", "parallel"))


def _norm_bwd_call(dy_list, w_list, h, g, dh_in, name, half_out, rider=None):
    s, d = h.shape
    tm = min(TM_SMALL, s)
    nk, nw = len(dy_list), len(w_list)
    factor = 0.5 if half_out else 1.0
    sharded = nw == 1 and w_list[0].ndim == 3

    def body(*refs):
        dy_refs = refs[:nk]
        w_refs = refs[nk:nk + nw]
        h_ref, g_ref, dhin_ref, dh_ref, dhb_ref, dg_ref = refs[nk + nw:]

        @pl.when(pl.program_id(0) == 0)
        def _():
            dg_ref[...] = jnp.zeros_like(dg_ref)

        if sharded:
            c = w_list[0].shape[2]
            per = dy_list[0].shape[1] // c
            du = None
            for k in range(w_list[0].shape[0]):
                part = _dot_nt(dy_refs[k // per][:, (k % per) * c:(k % per + 1) * c], w_refs[0][k])
                du = part if du is None else du + part
        else:
            du = _dot_nt(dy_refs[0][...], w_refs[0][...])
            for dy_ref, w_ref in zip(dy_refs[1:], w_refs[1:]):
                du = du + _dot_nt(dy_ref[...], w_ref[...])
        x = h_ref[...]
        r = _rstd(x, d)
        dg_ref[...] += jnp.sum(du * (x * r), axis=0, keepdims=True)
        dh = dhin_ref[...] + _rms_bwd(x, r, g_ref[...], du, d)
        dh_ref[...] = dh
        dhb_ref[...] = (factor * dh).astype(BF16)

    outs, got = _with_rider(
        body, rider, name=name, grid=(s // tm,),
        in_specs=[_rows(tm, dy.shape[1]) for dy in dy_list] + [_whole(w.shape) for w in w_list]
        + [_rows(tm, d), _whole((1, d)), _rows(tm, d)],
        out_specs=[_rows(tm, d), _rows(tm, d), _whole((1, d))],
        out_shape=[jax.ShapeDtypeStruct((s, d), F32), jax.ShapeDtypeStruct((s, d), BF16),
                   jax.ShapeDtypeStruct((1, d), F32)],
        args=(*dy_list, *w_list, h, g, dh_in), sem=("arbitrary",))
    return outs if rider is None else (outs, got)


def _merge_bwd_call(dhb, gates, bm, bs, wo, wbm, wbs, name):
    s, d = bm.shape
    tm = min(TM_SMALL, s)

    def body(dh_ref, g_ref, bm_ref, bs_ref, wo_ref, wbm_ref, wbs_ref, dg_ref, dbm_ref, dbs_ref, dom_ref, dos_ref):
        dmg = _dot_nt(dh_ref[...], wo_ref[...])
        s1 = _sigmoid(g_ref[:, :d])
        s2 = _sigmoid(g_ref[:, d:])
        dg_ref[:, :d] = (dmg * bm_ref[...] * (s1 * (1.0 - s1))).astype(BF16)
        dg_ref[:, d:] = (dmg * bs_ref[...] * (s2 * (1.0 - s2))).astype(BF16)
        dbm = (dmg * s1).astype(BF16)
        dbs = (dmg * s2).astype(BF16)
        dbm_ref[...] = dbm
        dbs_ref[...] = dbs
        dom_ref[...] = _dot_nt(dbm, wbm_ref[...]).astype(BF16)
        dos_ref[...] = _dot_nt(dbs, wbs_ref[...]).astype(BF16)

    wm = wbm.shape[0]
    return pl.pallas_call(
        body, name=name, grid=(s // tm,),
        in_specs=[_rows(tm, d), _rows(tm, 2 * d), _rows(tm, d), _rows(tm, d),
                  _whole(wo.shape), _whole(wbm.shape), _whole(wbs.shape)],
        out_specs=[_rows(tm, 2 * d), _rows(tm, d), _rows(tm, d), _rows(tm, wm), _rows(tm, SB_WIDTH)],
        out_shape=[jax.ShapeDtypeStruct((s, 2 * d), BF16), jax.ShapeDtypeStruct((s, d), BF16),
                   jax.ShapeDtypeStruct((s, d), BF16), jax.ShapeDtypeStruct((s, wm), BF16),
                   jax.ShapeDtypeStruct((s, SB_WIDTH), BF16)],
        compiler_params=_cp(("parallel",)))(dhb, gates, bm, bs, wo, wbm, wbs)


def _mla_prep_bwd_call(cq, ckv, krope, pos, freq, sign, g_ql, g_kvl, g_qh, g_kh, wq, wkv, dq, dk, dv, name):
    s = cq.shape[0]
    tm = min(TM_PREP_BWD, s)
    width = HEADS * HEAD_PAD

    def body(cq_ref, ckv_ref, kr_ref, pos_ref, freq_ref, sign_ref, gql_ref, gkvl_ref, gqh_ref, gkh_ref,
             wq_ref, wkv_ref, dq_ref, dk_ref, dv_ref,
             dcq_ref, dckv_ref, dkr_ref, dwq_ref, dwkv_ref, dgql_ref, dgkvl_ref, dgqh_ref, dgkh_ref, dqr_ref, dkv_ref):
        @pl.when(pl.program_id(0) == 0)
        def _():
            for ref in (dwq_ref, dwkv_ref, dgql_ref, dgkvl_ref, dgqh_ref, dgkh_ref):
                ref[...] = jnp.zeros_like(ref)

        cosv, ssv = _rope_tables(pos_ref, freq_ref, sign_ref)
        xq = cq_ref[...]
        rq = _rstd(xq, Q_LORA)
        cqn = ((xq * rq) * gql_ref[...]).astype(BF16)
        qr = _dot(cqn, wq_ref[...])
        xk = ckv_ref[...]
        rk = _rstd(xk, KV_LORA)
        ckvn = ((xk * rk) * gkvl_ref[...]).astype(BF16)
        kv = _dot(ckvn, wkv_ref[...])
        kr = kr_ref[...]
        lane = _lane((tm, HEAD_PAD))
        dkr = jnp.zeros((tm, HEAD_PAD), F32)
        dgqh = jnp.zeros((1, HEAD_PAD), F32)
        dgkh = jnp.zeros((1, HEAD_PAD), F32)
        for h in range(HEADS):
            sl = slice(h * HEAD_PAD, (h + 1) * HEAD_PAD)
            x = qr[:, sl]
            dx, dgh = _head_bwd(x, _rstd(x, MLA_QK), gqh_ref[...], cosv, ssv, dq_ref[:, sl])
            dqr_ref[:, sl] = dx.astype(BF16)
            dgqh = dgqh + dgh
            x = jnp.where(lane < MLA_NOPE, kv[:, sl], kr)
            dx, dgh = _head_bwd(x, _rstd(x, MLA_QK), gkh_ref[...], cosv, ssv, dk_ref[:, sl])
            dgkh = dgkh + dgh
            dkr = dkr + jnp.where(lane >= MLA_NOPE, dx, 0.0)
            dkv_ref[:, sl] = jnp.where(lane < MLA_NOPE, dx, dv_ref[:, sl]).astype(BF16)
        dgqh_ref[...] += dgqh
        dgkh_ref[...] += dgkh
        dkr_ref[...] = dkr.astype(BF16)
        dqr = dqr_ref[...]
        dkvb = dkv_ref[...]
        dwq_ref[...] += _dot_tn(cqn, dqr)
        dwkv_ref[...] += _dot_tn(ckvn, dkvb)
        dcqn = _dot_nt(dqr, wq_ref[...])
        dgql_ref[...] += jnp.sum(dcqn * (xq * rq), axis=0, keepdims=True)
        dcq_ref[...] = _rms_bwd(xq, rq, gql_ref[...], dcqn, Q_LORA).astype(BF16)
        dckvn = _dot_nt(dkvb, wkv_ref[...])
        dgkvl_ref[...] += jnp.sum(dckvn * (xk * rk), axis=0, keepdims=True)
        dckv_ref[...] = _rms_bwd(xk, rk, gkvl_ref[...], dckvn, KV_LORA).astype(BF16)

    vec = lambda n: jax.ShapeDtypeStruct((1, n), F32)
    outs = pl.pallas_call(
        body, name=name, grid=(s // tm,),
        in_specs=[_rows(tm, Q_LORA), _rows(tm, KV_LORA), _rows(tm, HEAD_PAD), _rows(tm, 1),
                  _whole((1, HEAD_PAD)), _whole((1, HEAD_PAD)), _whole((1, Q_LORA)), _whole((1, KV_LORA)),
                  _whole((1, HEAD_PAD)), _whole((1, HEAD_PAD)), _whole((Q_LORA, width)), _whole((KV_LORA, width)),
                  _rows(tm, width), _rows(tm, width), _rows(tm, width)],
        out_specs=[_rows(tm, Q_LORA), _rows(tm, KV_LORA), _rows(tm, HEAD_PAD), _whole((Q_LORA, width)),
                   _whole((KV_LORA, width)), _whole((1, Q_LORA)), _whole((1, KV_LORA)), _whole((1, HEAD_PAD)),
                   _whole((1, HEAD_PAD)), _rows(tm, width), _rows(tm, width)],
        out_shape=[jax.ShapeDtypeStruct((s, Q_LORA), BF16), jax.ShapeDtypeStruct((s, KV_LORA), BF16),
                   jax.ShapeDtypeStruct((s, HEAD_PAD), BF16), jax.ShapeDtypeStruct((Q_LORA, width), F32),
                   jax.ShapeDtypeStruct((KV_LORA, width), F32), vec(Q_LORA), vec(KV_LORA), vec(HEAD_PAD), vec(HEAD_PAD),
                   jax.ShapeDtypeStruct((s, width), BF16), jax.ShapeDtypeStruct((s, width), BF16)],
        compiler_params=_cp(("arbitrary",)))(cq, ckv, krope, pos, freq, sign, g_ql, g_kvl, g_qh, g_kh, wq, wkv, dq, dk, dv)
    return outs[:9]


def _tn_call(a, b, name, shard_cols=None, rider=None):
    s, ka = a.shape
    nb = b.shape[1]
    ti = _pick(ka, (512, 256, 128))
    if shard_cols is not None:
        tj = shard_cols
    else:
        tj = nb if nb <= TN_MAX_COLS else _pick(nb, (2176, 1024, 512, 256, 128))
    ts = s if 2 * s * (ti + tj) * a.dtype.itemsize <= TN_OPERAND_BYTES else s // 2
    ns = s // ts

    def body(a_ref, b_ref, o_ref, acc_ref):
        part = _dot_tn(a_ref[...].astype(BF16), b_ref[...].astype(BF16))
        if ns == 1:
            o_ref[...] = part.astype(o_ref.dtype)
            return

        @pl.when(pl.program_id(2) == 0)
        def _():
            acc_ref[...] = part

        @pl.when(pl.program_id(2) != 0)
        def _():
            acc_ref[...] += part

        @pl.when(pl.program_id(2) == ns - 1)
        def _():
            o_ref[...] = acc_ref[...].astype(o_ref.dtype)

    if shard_cols is None:
        out_spec = pl.BlockSpec((ti, tj), lambda i, j, t: (i, j))
        out_shape = jax.ShapeDtypeStruct((ka, nb), BF16)
    else:
        out_spec = pl.BlockSpec((None, ti, tj), lambda i, j, t: (j, i, 0))
        out_shape = jax.ShapeDtypeStruct((nb // tj, ka, tj), BF16)
    (out,), got = _with_rider(
        body, rider, name=name, grid=(ka // ti, nb // tj, ns),
        in_specs=[pl.BlockSpec((ts, ti), lambda i, j, t: (t, i)), pl.BlockSpec((ts, tj), lambda i, j, t: (t, j))],
        out_specs=[out_spec], out_shape=[out_shape], scratch=[pltpu.VMEM((ti, tj), F32)], args=(a, b),
        sem=("parallel", "parallel", "arbitrary"))
    return out if rider is None else (out, got)


def _sum_call(parts, out_dtype, name):
    n, r, w = parts.shape
    tr = _row_tile(r)

    def body(p_ref, o_ref):
        acc = p_ref[0].astype(F32)
        for k in range(1, n):
            acc = acc + p_ref[k].astype(F32)
        o_ref[...] = acc.astype(out_dtype)

    return pl.pallas_call(
        body, name=name, grid=(r // tr,),
        in_specs=[pl.BlockSpec((n, tr, w), lambda i: (0, i, 0))], out_specs=_rows(tr, w),
        out_shape=jax.ShapeDtypeStruct((r, w), out_dtype), compiler_params=_cp(("parallel",)))(parts)


def _chip_sum_call(by_chip, core, name):
    n, r, w = by_chip.shape
    tr = _row_tile(r)
    nblk = r // tr

    def body(c_ref, p_ref, o_ref):
        acc = p_ref[0].astype(F32)
        for k in range(1, n):
            acc = acc + p_ref[k].astype(F32)
        o_ref[...] = acc

    return pl.pallas_call(
        body, name=name,
        grid_spec=pltpu.PrefetchScalarGridSpec(
            num_scalar_prefetch=1, grid=(nblk,),
            in_specs=[pl.BlockSpec((n, tr, w), lambda i, c_ref: (0, i, 0))],
            out_specs=pl.BlockSpec((tr, w), lambda i, c_ref: (c_ref[0] * nblk + i, 0))),
        out_shape=jax.ShapeDtypeStruct((2 * r, w), F32),
        compiler_params=_cp(("parallel",)))(core.reshape(1).astype(jnp.int32), by_chip)


def _pair_sum_call(full, other, core, out_dtype, name):
    n, r, w = other.shape
    tr = _row_tile(r)
    nblk = r // tr

    def body(c_ref, a_ref, b_ref, o_ref):
        o_ref[...] = (a_ref[...].astype(F32) + b_ref[...].astype(F32)).astype(out_dtype)

    spec = pl.BlockSpec((None, tr, w), lambda k, i, c_ref: (k, i, 0))
    return pl.pallas_call(
        body, name=name,
        grid_spec=pltpu.PrefetchScalarGridSpec(
            num_scalar_prefetch=1, grid=(n, nblk),
            in_specs=[pl.BlockSpec((None, tr, w), lambda k, i, c_ref: (k, c_ref[0] * nblk + i, 0)), spec],
            out_specs=spec),
        out_shape=jax.ShapeDtypeStruct((n, r, w), out_dtype),
        compiler_params=_cp(("parallel", "parallel")))(core.reshape(1).astype(jnp.int32), full, other)


def _adamw_call(w, g, row0, m, v, name):
    r, c = w.shape
    span = math.gcd(r, row0) if row0 else r
    tr = next((t for t in range(min(span, 256) // 8 * 8, 0, -8) if span % t == 0), span)
    off = row0 // tr

    def body(w_ref, g_ref, m_ref, v_ref, g_out_ref, d_ref, nm_ref, nv_ref):
        gg = g_ref[...]
        g_out_ref[...] = gg
        nm = ADAM_B1 * m_ref[...] + (1.0 - ADAM_B1) * gg
        nv = ADAM_B2 * v_ref[...] + (1.0 - ADAM_B2) * (gg * gg)
        m_hat = nm / (1.0 - ADAM_B1 ** ADAM_STEP)
        v_hat = nv / (1.0 - ADAM_B2 ** ADAM_STEP)
        d_ref[...] = -ADAM_LR * (m_hat / (jnp.sqrt(v_hat) + ADAM_EPS) + ADAM_WD * w_ref[...])
        nm_ref[...] = nm
        nv_ref[...] = nv

    out = jax.ShapeDtypeStruct((r, c), F32)
    g_spec = pl.BlockSpec((tr, c), lambda i: (off + i, 0))
    return pl.pallas_call(
        body, name=name, grid=(r // tr,), in_specs=[_rows(tr, c), g_spec, _rows(tr, c), _rows(tr, c)],
        out_specs=[_rows(tr, c)] * 4, out_shape=[out, out, out, out], compiler_params=_cp(("parallel",)))(w, g, m, v)


def _position():
    x, y, c = lax.axis_index("x"), lax.axis_index("y"), lax.axis_index("c")
    chips = [(1 - x, y), (x, 1 - y), (1 - x, 1 - y)]
    return x, y, c, chips


def _gather_rider(parts):
    n = len(parts)
    pairs = [(j, k) for j in range(3) for k in range(n)]

    def piece(out_refs, k, chip, core):
        half = parts[k].shape[0] // 2
        return out_refs[k].at[2 * chip[0] + chip[1], pl.ds(core * half, half), :]

    def over_ici(in_refs, out_refs, sems, j, k):
        x, y, c, chips = _position()
        half = parts[k].shape[0] // 2
        return pltpu.make_async_remote_copy(
            src_ref=in_refs[k].at[pl.ds(c * half, half), :], dst_ref=piece(out_refs, k, (x, y), c),
            send_sem=sems[0].at[n * j + k], recv_sem=sems[1].at[n * j + k], device_id=(*chips[j], c), device_id_type=MESH)

    def to_sibling(out_refs, sems, j, k):
        x, y, c, chips = _position()
        landed = piece(out_refs, k, chips[j], c)
        return pltpu.make_async_remote_copy(
            src_ref=landed, dst_ref=landed, send_sem=sems[2].at[n * j + k], recv_sem=sems[3].at[n * j + k],
            device_id=(x, y, 1 - c), device_id_type=MESH)

    def start(in_refs, out_refs, sems):
        for j, k in pairs:
            over_ici(in_refs, out_refs, sems, j, k).start()

    def finish(in_refs, out_refs, sems):
        for j, k in pairs:
            over_ici(in_refs, out_refs, sems, j, k).wait_recv()
            to_sibling(out_refs, sems, j, k).start()
        for j, k in pairs:
            to_sibling(out_refs, sems, j, k).wait_recv()
        for j, k in pairs:
            over_ici(in_refs, out_refs, sems, j, k).wait_send()
            to_sibling(out_refs, sems, j, k).wait_send()

    return _Rider(list(parts), [jax.ShapeDtypeStruct((N_CHIPS,) + p.shape, p.dtype) for p in parts], [3 * n] * 4,
                  start, finish)


def _scatter_rider(parts):
    n = len(parts)
    pairs = [(j, k) for j in range(3) for k in range(n)]

    def copy(in_refs, out_refs, sems, j, k):
        x, y, c, chips = _position()
        return pltpu.make_async_remote_copy(
            src_ref=in_refs[k].at[2 * chips[j][0] + chips[j][1]], dst_ref=out_refs[k].at[2 * x + y],
            send_sem=sems[0].at[n * j + k], recv_sem=sems[1].at[n * j + k], device_id=(*chips[j], c), device_id_type=MESH)

    def start(in_refs, out_refs, sems):
        for j, k in pairs:
            copy(in_refs, out_refs, sems, j, k).start()

    def finish(in_refs, out_refs, sems):
        for j, k in pairs:
            copy(in_refs, out_refs, sems, j, k).wait()

    return _Rider(list(parts), [jax.ShapeDtypeStruct(p.shape, p.dtype) for p in parts], [3 * n] * 2, start, finish)


def _pair_send_call(parts, name):
    n = len(parts)

    def body(*refs):
        in_refs, out_refs = refs[:n], refs[n:2 * n]
        send_sems, recv_sems = refs[2 * n:]
        x, y, c, _ = _position()
        copies = []
        for k in range(n):
            half = parts[k].shape[1] // 2
            cp = pltpu.make_async_remote_copy(
                src_ref=in_refs[k].at[:, pl.ds((1 - c) * half, half), :], dst_ref=out_refs[k],
                send_sem=send_sems.at[k], recv_sem=recv_sems.at[k], device_id=(x, y, 1 - c), device_id_type=MESH)
            cp.start()
            copies.append(cp)
        for cp in copies:
            cp.wait()

    sems = pltpu.SemaphoreType.DMA((n,))
    return pl.pallas_call(
        body, name=name, in_specs=[HBM] * n, out_specs=[HBM] * n,
        out_shape=[jax.ShapeDtypeStruct((p.shape[0], p.shape[1] // 2, p.shape[2]), p.dtype) for p in parts],
        scratch_shapes=[sems, sems])(*parts)


def _pair_swap_call(parts, name):
    n = len(parts)

    def body(*refs):
        out_refs = refs[n:2 * n]
        send_sems, recv_sems = refs[2 * n:]
        x, y, c, _ = _position()
        copies = []
        for k in range(n):
            half = parts[k].shape[0] // 2
            mine = out_refs[k].at[pl.ds(c * half, half), :]
            cp = pltpu.make_async_remote_copy(
                src_ref=mine, dst_ref=mine, send_sem=send_sems.at[k], recv_sem=recv_sems.at[k],
                device_id=(x, y, 1 - c), device_id_type=MESH)
            cp.start()
            copies.append(cp)
        for cp in copies:
            cp.wait()

    sems = pltpu.SemaphoreType.DMA((n,))
    return pl.pallas_call(
        body, name=name, in_specs=[HBM] * n, out_specs=[HBM] * n,
        out_shape=[jax.ShapeDtypeStruct(p.shape, p.dtype) for p in parts],
        input_output_aliases={k: k for k in range(n)},
        scratch_shapes=[sems, sems])(*parts)


def _all_gather_small_call(block, name):
    r, w = block.shape

    def body(in_ref, out_ref, send_sems, recv_sems, local_sem):
        x, y, c, _ = _position()
        me = 4 * x + 2 * y + c
        own = pltpu.make_async_copy(in_ref, out_ref.at[me], local_sem)
        own.start()
        copies = []
        for k in range(1, 8):
            peer = (x ^ (k >> 2), y ^ ((k >> 1) & 1), c ^ (k & 1))
            cp = pltpu.make_async_remote_copy(
                src_ref=in_ref, dst_ref=out_ref.at[me], send_sem=send_sems.at[k - 1], recv_sem=recv_sems.at[k - 1],
                device_id=peer, device_id_type=MESH)
            cp.start()
            copies.append(cp)
        for cp in copies:
            cp.wait()
        own.wait()

    return pl.pallas_call(
        body, name=name, in_specs=[HBM], out_specs=HBM,
        out_shape=jax.ShapeDtypeStruct((8, r, w), block.dtype),
        scratch_shapes=[pltpu.SemaphoreType.DMA((7,)), pltpu.SemaphoreType.DMA((7,)), pltpu.SemaphoreType.DMA])(block)


BIG = {
    "ffn1_w_in": ((D_MODEL, 2 * D_FF), 1), "ffn1_w_out": ((D_FF, D_MODEL), 0),
    "w_in": ((D_MODEL, 4256), 1), "w_q_up": ((Q_LORA, HEADS * MLA_QK), 1), "w_kv_up": ((KV_LORA, 1024), 1),
    "w_branch_mla": ((512, D_MODEL), 1), "w_branch_sb": ((SB_WIDTH, D_MODEL), 1), "w_out": ((D_MODEL, D_MODEL), 0),
    "ffn2_w_in": ((D_MODEL, 2 * D_FF), 1), "ffn2_w_out": ((D_FF, D_MODEL), 0),
    "w_ple_gate": ((D_MODEL, D_MODEL), 0), "w_ple_proj": ((PLE_DIM, D_MODEL), 1),
}
GAINS = {"ffn1_norm": 1024, "mix_norm": 1024, "q_latent_norm": 384, "kv_latent_norm": 256, "q_head_norm": 96,
         "k_head_norm": 96, "ffn2_norm": 1024, "ple_norm": 1024}
WEIGHT_ORDER = ["ffn1_norm", "ffn1_w_in", "ffn1_w_out", "mix_norm", "w_in", "q_latent_norm", "w_q_up",
                "kv_latent_norm", "w_kv_up", "q_head_norm", "k_head_norm", "w_branch_mla", "w_branch_sb", "w_out",
                "ffn2_norm", "ffn2_w_in", "ffn2_w_out", "ple_norm", "w_ple_gate", "w_ple_proj"]


def _shard_shape(name):
    (r, c), axis = BIG[name]
    return (r // N_CHIPS, c) if axis == 0 else (r, c // N_CHIPS)


GATHER_GROUPS = [
    [("ffn1_w_in",)],
    [("ffn1_w_out",), ("w_in",)],
    [("w_out",), ("w_kv_up", "w_branch_mla", "w_branch_sb"), ("w_q_up",)],
    [("ffn2_w_in",), ("ffn2_w_out", "w_ple_gate"), ("w_ple_proj",)],
]
REDUCE_GROUPS = [
    [("ffn2_w_in",), ("ffn2_w_out", "w_out", "w_ple_gate"), ("w_branch_mla", "w_branch_sb", "w_ple_proj")],
    [("w_in",), ("w_kv_up",), ("w_q_up",)],
    [("ffn1_w_out",)],
    [("ffn1_w_in",)],
]


def _join_parts(shards, group):
    return [shards[part[0]] if len(part) == 1 else jnp.concatenate([shards[n] for n in part], axis=-2) for part in group]


def _part_rows(group):
    where = {}
    for k, part in enumerate(group):
        at = 0
        for n in part:
            where[n] = (k, at)
            at += _shard_shape(n)[0]
    return where


def _split_parts(parts, group):
    return {n: parts[k][..., at:at + _shard_shape(n)[0], :] for n, (k, at) in _part_rows(group).items()}


def _to_shards(name, full):
    (r, c), axis = BIG[name]
    if axis == 0:
        return full.reshape(N_CHIPS, r // N_CHIPS, c)
    return full.reshape(r, N_CHIPS, c // N_CHIPS).transpose(1, 0, 2)


def _from_shards(name, shards):
    (r, c), axis = BIG[name]
    if axis == 0:
        return shards.reshape(r, c)
    return shards.transpose(1, 0, 2).reshape(r, c)


def _relayout_w_in(w):
    d = w.shape[0]
    z = lambda n: jnp.zeros((d, n), w.dtype)
    return jnp.concatenate([w[:, :640], z(MLA_NOPE), w[:, 640:672], z(HEAD_PAD - MLA_QK), w[:, 672:]], axis=1)


def _unlayout_w_in(g):
    return jnp.concatenate([g[:, :640], g[:, 640 + MLA_NOPE:640 + MLA_QK], g[:, 768:]], axis=1)


def _pad_heads(v):
    lead = v.shape[:-1]
    return jnp.pad(v.reshape(lead + (HEADS, MLA_QK)), [(0, 0)] * len(lead) + [(0, 0), (0, HEAD_PAD - MLA_QK)]).reshape(
        lead + (HEADS * HEAD_PAD,))


SHARD_MAJOR = ("ffn1_w_in", "ffn2_w_in")
TRANSPOSED_UPDATE = ("w_in",)


def _step(x, p, pos, tgt, gains, weights, dist):
    d = D_MODEL
    full = {} if dist is not None else {n: _to_shards(n, w) if n in SHARD_MAJOR else w for n, w in weights.items()}
    reduced = {}

    def gather_rider(g):
        if dist is None:
            return None, None
        mine = _join_parts(weights, GATHER_GROUPS[g])
        return mine, _gather_rider(mine)

    def gathered(g, mine, others):
        if dist is not None:
            parts = [lax.dynamic_update_slice_in_dim(o, m[None], dist[0], axis=0) for o, m in zip(others, mine)]
            for n, shards in _split_parts(parts, GATHER_GROUPS[g]).items():
                full[n] = shards if n in SHARD_MAJOR else _from_shards(n, shards)

    def reduce_before(g):
        if dist is None:
            return None, None
        group = REDUCE_GROUPS[g]
        shards = {n: grads[n] if grads[n].ndim == 3 else _to_shards(n, grads[n].astype(BF16)) for part in group for n in part}
        partial = _join_parts(shards, group)
        from_sibling = _pair_send_call(partial, "grads%d_pair_send" % g)
        pair_sum = [_pair_sum_call(a, b, dist[1], BF16, "grads%d_pair_sum_%d" % (g, k))
                    for k, (a, b) in enumerate(zip(partial, from_sibling))]
        return pair_sum, _scatter_rider(pair_sum)

    def reduce_after(g, pair_sum, by_chip):
        if dist is not None:
            chip, core = dist
            by_chip = [lax.dynamic_update_slice_in_dim(t, lax.dynamic_slice_in_dim(o, chip, 1, axis=0), chip, axis=0)
                       for t, o in zip(by_chip, pair_sum)]
            bufs = _pair_swap_call([_chip_sum_call(t, core, "grads%d_chip_sum_%d" % (g, k)) for k, t in enumerate(by_chip)],
                                   "grads%d_pair_swap" % g)
            for n, (k, row0) in _part_rows(REDUCE_GROUPS[g]).items():
                reduced[n] = (bufs[k], row0)

    mine, rider = gather_rider(0)
    u1, got = _norm_call(x, gains["ffn1_norm"], "norm_ffn1", rider)
    gathered(0, mine, got)
    wts = full
    inv_freq = ROPE_BASE ** (-jnp.arange(0, MLA_ROPE, 2, dtype=F32) / MLA_ROPE)
    zeros = lambda n: jnp.zeros((n,), F32)
    freq = jnp.concatenate([zeros(MLA_NOPE), inv_freq, inv_freq, zeros(HEAD_PAD - MLA_QK)])[None]
    sign = jnp.concatenate([zeros(MLA_NOPE), -jnp.ones((16,), F32), jnp.ones((16,), F32), zeros(HEAD_PAD - MLA_QK)])[None]
    pad_gain = lambda g: jnp.pad(g, ((0, 0), (0, HEAD_PAD - MLA_QK)))
    g_qh, g_kh = pad_gain(gains["q_head_norm"]), pad_gain(gains["k_head_norm"])

    mine, rider = gather_rider(1)
    (a1, b1, hm1), got = _ffn_in_call(u1, wts["ffn1_w_in"], "ffn1_in", rider)
    gathered(1, mine, got)
    mine, rider = gather_rider(2)
    (h1, um), got = _ffn_out_call(hm1, wts["ffn1_w_out"], x, gains["mix_norm"], "ffn1_out", rider)
    gathered(2, mine, got)
    w_in = _relayout_w_in(wts["w_in"])
    wq = _pad_heads(wts["w_q_up"])
    wkv = wts["w_kv_up"]
    wbm = jnp.pad(wts["w_branch_mla"].reshape(HEADS, 64, d), ((0, 0), (64, 0), (0, 0))).reshape(HEADS * HEAD_PAD, d)
    wbs, wo = wts["w_branch_sb"], wts["w_out"]
    cq, ckv, krope, sbq, sbk, sbv, gates = _mix_in_call(um, w_in, "mix_in")
    prep_args = (cq, ckv, krope, pos, freq, sign, gains["q_latent_norm"], gains["kv_latent_norm"], g_qh, g_kh, wq, wkv)
    q, k, v = _mla_prep_call(*prep_args, "mla_prep")
    mine, rider = gather_rider(3)
    (om, lse), got = _mla_fwd_call(q, k, v, "mla_fwd", rider)
    gathered(3, mine, got)
    osb = _sb_fwd_call(sbq, sbk, sbv, "sb_fwd")
    h2, bm, bs, mg, u2 = _merge_out_call(om, osb, gates, h1, wbm, wbs, wo, gains["ffn2_norm"], "merge_out")
    (a2, b2, hm2), _ = _ffn_in_call(u2, wts["ffn2_w_in"], "ffn2_in")
    (h3, _), _ = _ffn_out_call(hm2, wts["ffn2_w_out"], h2, gains["ple_norm"], "ffn2_out")

    grads, gg = {}, {}
    dh3, dh3s, un, dgl, dpp, gg["ple_norm"], sq = _ple_call(
        h3, gains["ple_norm"], wts["w_ple_gate"], p, wts["w_ple_proj"], tgt, "ple")
    grads["w_ple_gate"] = _tn_call(un, dgl, "dw_ple_gate")
    grads["w_ple_proj"] = _tn_call(p, dpp, "dw_ple_proj")

    (da2, db2), _ = _ffn_bwd_a_call(dh3s, a2, b2, wts["ffn2_w_out"], "ffn2_bwd_act")
    grads["ffn2_w_out"] = _tn_call(hm2, dh3s, "dw_ffn2_out")
    grads["ffn2_w_in"] = jnp.concatenate([_tn_call(u2, da2, "dw_ffn2_in_a", shard_cols=D_FF // 2),
                                          _tn_call(u2, db2, "dw_ffn2_in_b", shard_cols=D_FF // 2)], axis=0)
    dh2, dh2b, gg["ffn2_norm"] = _norm_bwd_call([da2, db2], [wts["ffn2_w_in"]], h2, gains["ffn2_norm"], dh3,
                                                "ffn2_bwd_norm", half_out=False)

    dgates, dbm, dbs, dom, dos = _merge_bwd_call(dh2b, gates, bm, bs, wo, wbm, wbs, "merge_bwd")
    grads["w_out"] = _tn_call(mg, dh2b, "dw_out")
    grads["w_branch_mla"] = _tn_call(om, dbm, "dw_branch_mla").reshape(HEADS, HEAD_PAD, d)[:, 64:, :].reshape(512, d)
    grads["w_branch_sb"] = _tn_call(osb, dbs, "dw_branch_sb")
    pair_sum, rider = reduce_before(0)
    (dq, dk, dv), got = _mla_bwd_call(q, k, v, om, dom, lse, "mla_bwd", rider)
    reduce_after(0, pair_sum, got)
    dsq, dsk, dsv = _sb_bwd_call(sbq, sbk, sbv, dos, osb, "sb_bwd")
    (dcq, dckv, dkr, dwq, grads["w_kv_up"], gg["q_latent_norm"], gg["kv_latent_norm"], dgqh, dgkh) = \
        _mla_prep_bwd_call(*prep_args, dq, dk, dv, "mla_prep_bwd")
    grads["w_q_up"] = dwq.reshape(Q_LORA, HEADS, HEAD_PAD)[:, :, :MLA_QK].reshape(Q_LORA, HEADS * MLA_QK)
    gg["q_head_norm"], gg["k_head_norm"] = dgqh[:, :MLA_QK], dgkh[:, :MLA_QK]
    dproj = jnp.concatenate([dcq, dckv, dkr, dsq, dsk.astype(BF16), dsv.astype(BF16), dgates], axis=1)
    grads["w_in"] = _unlayout_w_in(_tn_call(um, dproj, "dw_in"))
    dh1, dh1s, gg["mix_norm"] = _norm_bwd_call([dproj], [w_in], h1, gains["mix_norm"], dh2, "mix_bwd_norm", half_out=True)

    pair_sum, rider = reduce_before(1)
    (da1, db1), got = _ffn_bwd_a_call(dh1s, a1, b1, wts["ffn1_w_out"], "ffn1_bwd_act", rider)
    reduce_after(1, pair_sum, got)
    grads["ffn1_w_out"] = _tn_call(hm1, dh1s, "dw_ffn1_out")
    pair_sum, rider = reduce_before(2)
    res = _tn_call(u1, da1, "dw_ffn1_in_a", shard_cols=D_FF // 2, rider=rider)
    dwa, got = (res, None) if rider is None else res
    reduce_after(2, pair_sum, got)
    grads["ffn1_w_in"] = jnp.concatenate([dwa, _tn_call(u1, db1, "dw_ffn1_in_b", shard_cols=D_FF // 2)], axis=0)
    pair_sum, rider = reduce_before(3)
    res = _norm_bwd_call([da1, db1], [wts["ffn1_w_in"]], x, gains["ffn1_norm"], dh1, "ffn1_bwd_norm",
                         half_out=False, rider=rider)
    (dx, _, gg["ffn1_norm"]), got = (res, None) if rider is None else res
    reduce_after(3, pair_sum, got)
    return sq, dx, gg, (grads if dist is None else reduced)


def kernel(x, p, positions, ffn1_norm, ffn1_w_in, ffn1_w_out, mix_norm, w_in, q_latent_norm, w_q_up, kv_latent_norm, w_kv_up, q_head_norm, k_head_norm, w_branch_mla, w_branch_sb, w_out, ffn2_norm, ffn2_w_in, ffn2_w_out, ple_norm, w_ple_gate, w_ple_proj, loss_target, m_ffn1_norm, m_ffn1_w_in, m_ffn1_w_out, m_mix_norm, m_w_in, m_q_latent_norm, m_w_q_up, m_kv_latent_norm, m_w_kv_up, m_q_head_norm, m_k_head_norm, m_w_branch_mla, m_w_branch_sb, m_w_out, m_ffn2_norm, m_ffn2_w_in, m_ffn2_w_out, m_ple_norm, m_w_ple_gate, m_w_ple_proj, v_ffn1_norm, v_ffn1_w_in, v_ffn1_w_out, v_mix_norm, v_w_in, v_q_latent_norm, v_w_q_up, v_kv_latent_norm, v_w_kv_up, v_q_head_norm, v_k_head_norm, v_w_branch_mla, v_w_branch_sb, v_w_out, v_ffn2_norm, v_ffn2_w_in, v_ffn2_w_out, v_ple_norm, v_w_ple_gate, v_w_ple_proj):
    given = dict(locals())
    w_shard = {n: given[n][0] for n in WEIGHT_ORDER}
    m_shard = {n: given["m_" + n][0] for n in WEIGHT_ORDER}
    v_shard = {n: given["v_" + n][0] for n in WEIGHT_ORDER}
    gains = {n: w_shard[n][None] for n in GAINS}

    chip = 2 * lax.axis_index("x") + lax.axis_index("y")
    sq, dx, gain_grads, reduced = _step(x[0], p[0, 0], positions.reshape(-1, 1), loss_target[0], gains,
                                        {n: w_shard[n].astype(BF16) for n in BIG}, (chip, lax.axis_index("c")))

    rows = [jnp.pad(gain_grads[n], ((0, 0), (0, D_MODEL - GAINS[n]))) for n in GAINS] + [sq]
    gain_block = jnp.concatenate(rows + [jnp.zeros((16 - len(rows), D_MODEL), F32)], axis=0)
    gain_sum = _sum_call(_all_gather_small_call(gain_block, "gains_all_gather"), F32, "gains_sum")
    loss = 0.5 * jnp.sum(gain_sum[len(GAINS)]) / D_MODEL

    outs = {"grad": {}, "delta": {}, "new_m": {}, "new_v": {}}
    gain_pack = lambda t: jnp.concatenate([jnp.pad(t[n][None], ((0, 0), (0, D_MODEL - GAINS[n]))) for n in GAINS], axis=0)
    packed = _adamw_call(gain_pack(w_shard), gain_sum, 0, gain_pack(m_shard), gain_pack(v_shard), "adamw_gains")
    for i, n in enumerate(GAINS):
        for kind, t in zip(("grad", "delta", "new_m", "new_v"), packed):
            outs[kind][n] = t[i, :GAINS[n]][None]
    for n in BIG:
        buf, row0 = reduced[n]
        if n in TRANSPOSED_UPDATE:
            rows = _shard_shape(n)[0]
            res = _adamw_call(w_shard[n].T, buf[row0:row0 + rows].T, 0, m_shard[n].T, v_shard[n].T, "adamw_" + n)
            res = [t.T for t in res]
        else:
            res = _adamw_call(w_shard[n], buf, row0, m_shard[n], v_shard[n], "adamw_" + n)
        for kind, t in zip(("grad", "delta", "new_m", "new_v"), res):
            outs[kind][n] = t[None]

    return (loss, dx[None], *[outs["grad"][n] for n in WEIGHT_ORDER], *[outs["delta"][n] for n in WEIGHT_ORDER],
            *[outs["new_m"][n] for n in WEIGHT_ORDER], *[outs["new_v"][n] for n in WEIGHT_ORDER])
```

```python
import collections
import functools
import math

import jax
import jax.numpy as jnp
from jax import lax
from jax.experimental import pallas as pl
from jax.experimental.pallas import tpu as pltpu

F32 = jnp.float32
BF16 = jnp.bfloat16
MESH = pl.DeviceIdType.MESH

D_MODEL = 1024
D_FF = 2816
PLE_DIM = 256
NORM_EPS = 1e-6
HEADS = 8
MLA_NOPE = 64
MLA_ROPE = 32
MLA_QK = 96
Q_LORA = 384
KV_LORA = 256
SB_WIDTH = 512
ROPE_BASE = 10000.0
LOG2_E = math.log2(math.e)
HEAD_PAD = 128
N_CHIPS = 4

ADAM_LR = 0.001
ADAM_B1 = 0.9
ADAM_B2 = 0.999
ADAM_EPS = 1e-08
ADAM_WD = 0.01
ADAM_STEP = 10

SEG_CQ = (0, 384)
SEG_CKV = (384, 256)
SEG_KROPE = (640, 128)
SEG_SBQ = (768, 512)
SEG_SBK = (1280, 512)
SEG_SBV = (1792, 512)
SEG_GATES = (2304, 2048)
IN_COLS_PAD = 4352

TM = 512
TM_SMALL = 512
TM_PREP_BWD = 256
TQ = 256
MLA_FWD_BLOCKS = 4
MLA_BWD_BLOCKS = 4
SB_FWD_BLOCKS = 4
SB_BWD_BLOCKS = 2
SB_HEAD = 64
SB_SCALE = 0.125
SB_DEAD = -104.0
COL_CHUNK = 256
TN_MAX_COLS = 2816
TN_OPERAND_BYTES = 34 * 1024 * 1024
MAX_ROW_TILE = 512
VMEM_LIMIT = 56 * 1024 * 1024

NT = (((1,), (1,)), ((), ()))
TN = (((0,), (0,)), ((), ()))


def _cp(sem):
    return pltpu.CompilerParams(dimension_semantics=sem, vmem_limit_bytes=VMEM_LIMIT)


def _rows(tm, w):
    return pl.BlockSpec((tm, w), lambda i: (i, 0))


def _whole(shape):
    return pl.BlockSpec(shape, lambda i: (0,) * len(shape))


def _dot(a, b):
    return jnp.dot(a, b, preferred_element_type=F32)


def _dot_nt(a, b):
    return lax.dot_general(a, b, NT, preferred_element_type=F32)


def _dot_tn(a, b):
    return lax.dot_general(a, b, TN, preferred_element_type=F32)


def _rstd(x, n):
    return lax.rsqrt(jnp.sum(x * x, axis=-1, keepdims=True) / n + NORM_EPS)


def _rms_bwd(x, r, g, dy, n):
    gy = dy * g
    return r * gy - x * ((r * r * r) * (jnp.sum(x * gy, axis=-1, keepdims=True) / n))


def _sigmoid(x):
    return jax.nn.sigmoid(x)


def _pick(n, cands):
    for c in cands:
        if n % c == 0:
            return c
    return n


def _row_tile(r):
    for t in range(min(r, MAX_ROW_TILE) // 16 * 16, 15, -16):
        if r % t == 0:
            return t
    return r


HBM = pl.BlockSpec(memory_space=pl.ANY)

_Rider = collections.namedtuple("_Rider", "ins out_shape sems start finish")


def _with_rider(body, rider, *, name, grid, in_specs, out_specs, out_shape, args, sem, scratch=()):
    if rider is None:
        return pl.pallas_call(body, name=name, grid=grid, in_specs=in_specs, out_specs=out_specs, out_shape=out_shape,
                              scratch_shapes=list(scratch), compiler_params=_cp(sem))(*args), None
    ni, no, nri, nro = len(in_specs), len(out_specs), len(rider.ins), len(rider.out_shape)

    def riding(*refs):
        ins, r_ins = refs[:ni], refs[ni:ni + nri]
        outs, r_outs = refs[ni + nri:ni + nri + no], refs[ni + nri + no:ni + nri + no + nro]
        scr = refs[ni + nri + no + nro:ni + nri + no + nro + len(scratch)]
        sems = refs[ni + nri + no + nro + len(scratch):]
        ids = [pl.program_id(a) for a in range(len(grid))]
        first = functools.reduce(jnp.logical_and, [i == 0 for i in ids])
        last = functools.reduce(jnp.logical_and, [i == g - 1 for i, g in zip(ids, grid)])

        @pl.when(first)
        def _():
            rider.start(r_ins, r_outs, sems)

        body(*ins, *outs, *scr)

        @pl.when(last)
        def _():
            rider.finish(r_ins, r_outs, sems)

    res = pl.pallas_call(
        riding, name=name, grid=grid, in_specs=list(in_specs) + [HBM] * nri, out_specs=list(out_specs) + [HBM] * nro,
        out_shape=list(out_shape) + list(rider.out_shape),
        scratch_shapes=list(scratch) + [pltpu.SemaphoreType.DMA((k,)) for k in rider.sems],
        compiler_params=_cp(("arbitrary",) * len(grid)))(*args, *rider.ins)
    return res[:no], res[no:]


def _norm_call(h, g, name, rider=None):
    s, d = h.shape
    tm = min(TM, s)

    def body(h_ref, g_ref, u_ref):
        x = h_ref[...]
        u_ref[...] = ((x * _rstd(x, d)) * g_ref[...]).astype(BF16)

    (u,), got = _with_rider(
        body, rider, name=name, grid=(s // tm,),
        in_specs=[_rows(tm, d), _whole((1, d))], out_specs=[_rows(tm, d)],
        out_shape=[jax.ShapeDtypeStruct((s, d), BF16)], args=(h, g), sem=("parallel",))
    return u, got


def _ffn_in_call(u, w, name, rider=None):
    s, d = u.shape
    tn = w.shape[2]
    nj = w.shape[0] // 2
    n = nj * tn
    tm = min(TM, s)

    def body(u_ref, wa_ref, wb_ref, a_ref, b_ref, hm_ref):
        uu = u_ref[...]
        a = _dot(uu, wa_ref[...])
        b = _dot(uu, wb_ref[...])
        a_ref[...] = a
        b_ref[...] = b
        hm_ref[...] = ((a * _sigmoid(a)) * b).astype(BF16)

    blk = pl.BlockSpec((tm, tn), lambda j, i: (i, j))
    return _with_rider(
        body, rider, name=name, grid=(nj, s // tm),
        in_specs=[pl.BlockSpec((tm, d), lambda j, i: (i, 0)),
                  pl.BlockSpec((None, d, tn), lambda j, i: (j, 0, 0)),
                  pl.BlockSpec((None, d, tn), lambda j, i: (j + nj, 0, 0))],
        out_specs=[blk, blk, blk],
        out_shape=[jax.ShapeDtypeStruct((s, n), F32), jax.ShapeDtypeStruct((s, n), F32),
                   jax.ShapeDtypeStruct((s, n), BF16)],
        args=(u, w, w), sem=("parallel", "parallel"))


def _ffn_out_call(hm, w, h, gain, name, rider=None):
    s, n = hm.shape
    d = w.shape[1]
    tm = min(TM, s)

    def body(hm_ref, w_ref, h_ref, g_ref, o_ref, u_ref):
        x = h_ref[...] + 0.5 * _dot(hm_ref[...], w_ref[...])
        o_ref[...] = x
        u_ref[...] = ((x * _rstd(x, d)) * g_ref[...]).astype(BF16)

    return _with_rider(
        body, rider, name=name, grid=(s // tm,),
        in_specs=[_rows(tm, n), _whole((n, d)), _rows(tm, d), _whole((1, d))], out_specs=[_rows(tm, d), _rows(tm, d)],
        out_shape=[jax.ShapeDtypeStruct((s, d), F32), jax.ShapeDtypeStruct((s, d), BF16)], args=(hm, w, h, gain),
        sem=("parallel",))


def _mix_in_call(u, wt, name):
    s, d = u.shape
    tm = min(TM_SMALL, s)
    segs = [(SEG_CQ, F32), (SEG_CKV, F32), (SEG_KROPE, F32), (SEG_SBQ, BF16), (SEG_SBK, BF16),
            (SEG_SBV, BF16), (SEG_GATES, F32)]

    def body(u_ref, w_ref, *outs):
        uu = u_ref[...]
        for ((off, width), _), o_ref in zip(segs, outs):
            o_ref[...] = _dot_nt(uu, w_ref[off:off + width, :]).astype(o_ref.dtype)

    return pl.pallas_call(
        body, name=name, grid=(s // tm,),
        in_specs=[_rows(tm, d), _whole((IN_COLS_PAD, d))],
        out_specs=[_rows(tm, width) for (_, width), _ in segs],
        out_shape=[jax.ShapeDtypeStruct((s, width), dt) for (_, width), dt in segs],
        compiler_params=_cp(("parallel",)))(u, wt)


def _lane(shape):
    return lax.broadcasted_iota(jnp.int32, shape, len(shape) - 1)


def _rot_half(y):
    lane = _lane(y.shape)
    swapped = jnp.where(lane < MLA_NOPE + MLA_ROPE // 2, pltpu.roll(y, HEAD_PAD - 16, 1), pltpu.roll(y, 16, 1))
    return jnp.where((lane >= MLA_NOPE) & (lane < MLA_QK), swapped, 0.0)


def _rope_tables(pos_ref, freq_ref, sign_ref):
    ang = pos_ref[...].astype(F32) * freq_ref[...]
    return jnp.cos(ang), jnp.sin(ang) * sign_ref[...]


def _head_fwd(x, g, cosv, ssv):
    r = _rstd(x, MLA_QK)
    y = (x * r) * g
    return y * cosv + _rot_half(y) * ssv, r


def _head_bwd(x, r, g, cosv, ssv, dout):
    dy = dout * cosv + _rot_half(dout * ssv)
    return _rms_bwd(x, r, g, dy, MLA_QK), jnp.sum(dy * (x * r), axis=0, keepdims=True)


def _mla_prep_call(cq, ckv, krope, pos, freq, sign, g_ql, g_kvl, g_qh, g_kh, wq, wkv, name):
    s = cq.shape[0]
    tm = min(TM_SMALL, s)
    width = HEADS * HEAD_PAD

    def body(cq_ref, ckv_ref, kr_ref, pos_ref, freq_ref, sign_ref, gql_ref, gkvl_ref, gqh_ref, gkh_ref,
             wq_ref, wkv_ref, q_ref, k_ref, v_ref):
        cosv, ssv = _rope_tables(pos_ref, freq_ref, sign_ref)
        x = cq_ref[...]
        qr = _dot(((x * _rstd(x, Q_LORA)) * gql_ref[...]).astype(BF16), wq_ref[...])
        x = ckv_ref[...]
        kv = _dot(((x * _rstd(x, KV_LORA)) * gkvl_ref[...]).astype(BF16), wkv_ref[...])
        kr = kr_ref[...]
        lane = _lane((tm, HEAD_PAD))
        for h in range(HEADS):
            sl = slice(h * HEAD_PAD, (h + 1) * HEAD_PAD)
            qh, _ = _head_fwd(qr[:, sl], gqh_ref[...], cosv, ssv)
            q_ref[:, sl] = qh.astype(BF16)
            kvh = kv[:, sl]
            kh, _ = _head_fwd(jnp.where(lane < MLA_NOPE, kvh, kr), gkh_ref[...], cosv, ssv)
            k_ref[:, sl] = kh.astype(BF16)
            v_ref[:, sl] = jnp.where(lane >= MLA_NOPE, kvh, jnp.where(lane == 0, 1.0, 0.0)).astype(BF16)

    out = jax.ShapeDtypeStruct((s, width), BF16)
    return pl.pallas_call(
        body, name=name, grid=(s // tm,),
        in_specs=[_rows(tm, Q_LORA), _rows(tm, KV_LORA), _rows(tm, HEAD_PAD), _rows(tm, 1),
                  _whole((1, HEAD_PAD)), _whole((1, HEAD_PAD)), _whole((1, Q_LORA)), _whole((1, KV_LORA)),
                  _whole((1, HEAD_PAD)), _whole((1, HEAD_PAD)), _whole((Q_LORA, width)), _whole((KV_LORA, width))],
        out_specs=[_rows(tm, width)] * 3, out_shape=[out, out, out],
        compiler_params=_cp(("parallel",)))(cq, ckv, krope, pos, freq, sign, g_ql, g_kvl, g_qh, g_kh, wq, wkv)


def _attn_specs(s, nb):
    qspec = pl.BlockSpec((TQ, nb * HEAD_PAD), lambda g, i: (i, g))
    kspec = pl.BlockSpec((s, nb * HEAD_PAD), lambda g, i: (0, g))
    return qspec, kspec


def _lanes(b):
    return slice(b * HEAD_PAD, (b + 1) * HEAD_PAD)


def _tri(cmp):
    r = lax.broadcasted_iota(jnp.int32, (TQ, TQ), 0)
    c = lax.broadcasted_iota(jnp.int32, (TQ, TQ), 1)
    return cmp(r, c)


def _mla_fwd_call(q, k, v, name, rider=None):
    s, width = q.shape
    scale = 1.0 / math.sqrt(MLA_QK)

    nb = MLA_FWD_BLOCKS

    def body(q_ref, k_ref, v_ref, o_ref, lse_ref):
        qi = pl.program_id(1)
        qs = [q_ref[:, _lanes(b)] for b in range(nb)]
        causal = _tri(lambda r, c: c <= r)

        def step(kb, carry, diag):
            ks = pl.multiple_of(kb * TQ, TQ)
            heads = range(nb)
            scs = [_dot_nt(qs[b], k_ref[pl.ds(ks, TQ), _lanes(b)]) * (scale * LOG2_E) for b in heads]
            if diag:
                scs = [jnp.where(causal, sc, -1e30) for sc in scs]
            mns = [jnp.maximum(carry[b][0], jnp.max(scs[b], axis=-1, keepdims=True)) for b in heads]
            als = [jnp.exp2(carry[b][0] - mns[b]) for b in heads]
            ps = [jnp.exp2(scs[b] - mns[b]).astype(BF16) for b in heads]
            accs = [als[b] * carry[b][1] + _dot(ps[b], v_ref[pl.ds(ks, TQ), _lanes(b)]) for b in heads]
            return tuple((mns[b], accs[b]) for b in heads)

        init = tuple((jnp.full((TQ, 1), -1e30, F32), jnp.zeros((TQ, HEAD_PAD), F32)) for _ in range(nb))
        carry = step(qi, init, True)
        carry = lax.fori_loop(0, qi, lambda kb, c: step(kb, c, False), carry)
        for b in range(nb):
            m, acc = carry[b]
            l = acc[:, 0:1]
            o_ref[:, _lanes(b)] = (acc / l).astype(BF16)
            lse_ref[:, _lanes(b)] = jnp.broadcast_to(m * (1.0 / LOG2_E) + jnp.log(l), (TQ, HEAD_PAD))

    qspec, kspec = _attn_specs(s, nb)
    return _with_rider(
        body, rider, name=name, grid=(width // (nb * HEAD_PAD), s // TQ),
        in_specs=[qspec, kspec, kspec], out_specs=[qspec, qspec],
        out_shape=[jax.ShapeDtypeStruct((s, width), BF16), jax.ShapeDtypeStruct((s, width), F32)],
        args=(q, k, v), sem=("parallel", "arbitrary"))


def _mla_bwd_call(q, k, v, o, do, lse, name, rider=None):
    s, width = q.shape
    scale = 1.0 / math.sqrt(MLA_QK)
    nb = MLA_BWD_BLOCKS

    def body(q_ref, k_ref, v_ref, o_ref, do_ref, lse_ref, dq_ref, dk_ref, dv_ref):
        qi = pl.program_id(1)

        @pl.when(qi == 0)
        def _():
            dk_ref[...] = jnp.zeros_like(dk_ref)
            dv_ref[...] = jnp.zeros_like(dv_ref)

        qs = [q_ref[:, _lanes(b)] for b in range(nb)]
        dos = [do_ref[:, _lanes(b)] for b in range(nb)]
        lses = [lse_ref[:, b * HEAD_PAD:b * HEAD_PAD + 1] for b in range(nb)]
        dlts = [jnp.sum(dos[b].astype(F32) * o_ref[:, _lanes(b)].astype(F32), axis=-1, keepdims=True) for b in range(nb)]
        causal = _tri(lambda r, c: c <= r)

        def step(kb, dqs, diag):
            ks = pl.multiple_of(kb * TQ, TQ)
            heads = range(nb)
            kts = [k_ref[pl.ds(ks, TQ), _lanes(b)] for b in heads]
            scs = [_dot_nt(qs[b], kts[b]) for b in heads]
            dps = [_dot_nt(dos[b], v_ref[pl.ds(ks, TQ), _lanes(b)]) for b in heads]
            ps = [jnp.exp(scs[b] * scale - lses[b]) for b in heads]
            if diag:
                ps = [jnp.where(causal, p, 0.0) for p in ps]
            dss = [(ps[b] * (dps[b] - dlts[b]) * scale).astype(BF16) for b in heads]
            dvs = [_dot_tn(ps[b].astype(BF16), dos[b]) for b in heads]
            dks = [_dot_tn(dss[b], qs[b]) for b in heads]
            out = tuple(dqs[b] + _dot(dss[b], kts[b]) for b in heads)
            for b in heads:
                dv_ref[pl.ds(ks, TQ), _lanes(b)] += dvs[b]
                dk_ref[pl.ds(ks, TQ), _lanes(b)] += dks[b]
            return out

        dqs = step(qi, tuple(jnp.zeros((TQ, HEAD_PAD), F32) for _ in range(nb)), True)
        dqs = lax.fori_loop(0, qi, lambda kb, c: step(kb, c, False), dqs)
        for b in range(nb):
            dq_ref[:, _lanes(b)] = dqs[b]

    qspec, kspec = _attn_specs(s, nb)
    out = jax.ShapeDtypeStruct((s, width), F32)
    return _with_rider(
        body, rider, name=name, grid=(width // (nb * HEAD_PAD), s // TQ),
        in_specs=[qspec, kspec, kspec, qspec, qspec, qspec], out_specs=[qspec, kspec, kspec],
        out_shape=[out, out, out], args=(q, k, v, o, do, lse), sem=("parallel", "arbitrary"))


def _dot_hilo(x, u):
    hi = x.astype(BF16)
    lo = (x - hi.astype(F32)).astype(BF16)
    return _dot(hi, u) + _dot(lo, u)


def _sb_logs(z):
    ls = jnp.minimum(z, 0.0) - jnp.log(1.0 + jnp.exp(-jnp.abs(z)))
    return ls, ls - z


def _sb_head_q(qb, first, hh):
    keep = first if hh == 0 else jnp.logical_not(first)
    return jnp.where(keep, qb, jnp.zeros_like(qb)) * jnp.asarray(SB_SCALE, qb.dtype)


def _sb_fwd_call(q, k, v, name):
    s, width = q.shape
    nb = SB_FWD_BLOCKS
    chains = [(b, hh) for b in range(nb) for hh in range(HEAD_PAD // SB_HEAD)]

    def body(q_ref, k_ref, v_ref, o_ref):
        qi = pl.program_id(1)
        strict = _tri(lambda r, c: c < r)
        after = _tri(lambda r, c: r > c).astype(BF16)
        first = _lane((1, HEAD_PAD)) < SB_HEAD
        qhs = [_sb_head_q(q_ref[:, _lanes(b)], first, hh) for b, hh in chains]

        def step(kb, carry, diag):
            ks = pl.multiple_of(kb * TQ, TQ)
            ids = range(len(chains))
            zs = [_dot_nt(qhs[ci], k_ref[pl.ds(ks, TQ), _lanes(chains[ci][0])]) for ci in ids]
            logs = [_sb_logs(z) for z in zs]
            lss = [lg[0] for lg in logs]
            l1ms = [jnp.where(strict, lg[1], 0.0) if diag else lg[1] for lg in logs]
            sufs = [_dot_hilo(l1m, after) for l1m in l1ms]
            as_ = [jnp.exp(lss[ci] + sufs[ci] + carry[ci][0]) for ci in ids]
            if diag:
                as_ = [jnp.where(strict, a, 0.0) for a in as_]
            accs = [carry[ci][1] + _dot(as_[ci].astype(BF16), v_ref[pl.ds(ks, TQ), _lanes(chains[ci][0])]) for ci in ids]
            return tuple((carry[ci][0] + jnp.sum(l1ms[ci], axis=-1, keepdims=True), accs[ci]) for ci in ids)

        init = tuple((jnp.zeros((TQ, 1), F32), jnp.zeros((TQ, HEAD_PAD), F32)) for _ in chains)
        carry = _sb_sweep(step, qi, init)
        for b in range(nb):
            o_ref[:, _lanes(b)] = jnp.where(first, carry[2 * b][1], carry[2 * b + 1][1])

    qspec, kspec = _attn_specs(s, nb)
    return pl.pallas_call(
        body, name=name, grid=(width // (nb * HEAD_PAD), s // TQ),
        in_specs=[qspec, kspec, kspec], out_specs=qspec, out_shape=jax.ShapeDtypeStruct((s, width), F32),
        compiler_params=_cp(("parallel", "arbitrary")))(q, k, v)


def _sb_sweep(step, qi, init):
    def live(carry):
        top = carry[0][0]
        for c in carry[1:]:
            top = jnp.maximum(top, c[0])
        return jnp.max(top)

    carry = step(qi, init, True)

    def cond(state):
        j, alive, _ = state
        return jnp.logical_and(j < qi, alive > SB_DEAD)

    def body(state):
        j, _, carry = state
        carry = step(qi - 1 - j, carry, False)
        return j + 1, live(carry), carry

    return lax.while_loop(cond, body, (jnp.int32(0), live(carry), carry))[2]


def _sb_bwd_call(q, k, v, do, o, name):
    s, width = q.shape
    nb = SB_BWD_BLOCKS
    chains = [(b, hh) for b in range(nb) for hh in range(HEAD_PAD // SB_HEAD)]

    def body(q_ref, k_ref, v_ref, do_ref, o_ref, dq_ref, dk_ref, dv_ref):
        qi = pl.program_id(1)

        @pl.when(qi == 0)
        def _():
            dk_ref[...] = jnp.zeros_like(dk_ref)
            dv_ref[...] = jnp.zeros_like(dv_ref)

        strict = _tri(lambda r, c: c < r)
        after = _tri(lambda r, c: r > c).astype(BF16)
        from_here = _tri(lambda r, c: r >= c).astype(BF16)
        first = _lane((1, HEAD_PAD)) < SB_HEAD
        qhs = [_sb_head_q(q_ref[:, _lanes(b)], first, hh) for b, hh in chains]
        dohs = []
        for b, hh in chains:
            dob = do_ref[:, _lanes(b)]
            dohs.append(jnp.where(first if hh == 0 else jnp.logical_not(first), dob, jnp.zeros_like(dob)))
        gtots = [jnp.sum(dohs[ci].astype(F32) * o_ref[:, _lanes(chains[ci][0])], axis=-1, keepdims=True)
                 for ci in range(len(chains))]

        def step(kb, carry, diag):
            ks = pl.multiple_of(kb * TQ, TQ)
            ids = range(len(chains))
            kts = [k_ref[pl.ds(ks, TQ), _lanes(b)] for b, _ in chains]
            zs = [_dot_nt(qhs[ci], kts[ci]) for ci in ids]
            das = [_dot_nt(dohs[ci], v_ref[pl.ds(ks, TQ), _lanes(chains[ci][0])]) for ci in ids]
            logs = [_sb_logs(z) for z in zs]
            lss = [lg[0] for lg in logs]
            l1ms = [jnp.where(strict, lg[1], 0.0) if diag else lg[1] for lg in logs]
            sufs = [_dot_hilo(l1m, after) for l1m in l1ms]
            as_ = [jnp.exp(lss[ci] + sufs[ci] + carry[ci][0]) for ci in ids]
            if diag:
                as_ = [jnp.where(strict, a, 0.0) for a in as_]
            abs_ = [a.astype(BF16) for a in as_]
            gs = [abs_[ci].astype(F32) * das[ci] for ci in ids]
            cexs = [gtots[ci] - (carry[ci][1] + _dot_hilo(gs[ci], from_here)) for ci in ids]
            dzs = [gs[ci] - jnp.exp(lss[ci]) * (gs[ci] + cexs[ci]) for ci in ids]
            if diag:
                dzs = [jnp.where(strict, dz, 0.0) for dz in dzs]
            dzbs = [dz.astype(BF16) for dz in dzs]
            dvps = [_dot_tn(abs_[ci], dohs[ci]) for ci in ids]
            dkps = [_dot_tn(dzbs[ci], qhs[ci]) for ci in ids]
            out = tuple((carry[ci][0] + jnp.sum(l1ms[ci], axis=-1, keepdims=True),
                         carry[ci][1] + jnp.sum(gs[ci], axis=-1, keepdims=True),
                         carry[ci][2] + _dot(dzbs[ci], kts[ci])) for ci in ids)
            for b in range(nb):
                dk_ref[pl.ds(ks, TQ), _lanes(b)] += dkps[2 * b] + dkps[2 * b + 1]
                dv_ref[pl.ds(ks, TQ), _lanes(b)] += dvps[2 * b] + dvps[2 * b + 1]
            return out

        init = tuple((jnp.zeros((TQ, 1), F32), jnp.zeros((TQ, 1), F32), jnp.zeros((TQ, HEAD_PAD), F32)) for _ in chains)
        carry = _sb_sweep(step, qi, init)
        for b in range(nb):
            dq_ref[:, _lanes(b)] = (jnp.where(first, carry[2 * b][2], carry[2 * b + 1][2]) * SB_SCALE).astype(BF16)

    qspec, kspec = _attn_specs(s, nb)
    return pl.pallas_call(
        body, name=name, grid=(width // (nb * HEAD_PAD), s // TQ),
        in_specs=[qspec, kspec, kspec, qspec, qspec], out_specs=[qspec, kspec, kspec],
        out_shape=[jax.ShapeDtypeStruct((s, width), BF16), jax.ShapeDtypeStruct((s, width), F32),
                   jax.ShapeDtypeStruct((s, width), F32)],
        compiler_params=_cp(("parallel", "arbitrary")))(q, k, v, do, o)


def _merge_out_call(om, osb, gates, h, wbm, wbs, wo, gain, name):
    s, d = h.shape
    tm = min(TM_SMALL, s)

    def body(om_ref, os_ref, g_ref, h_ref, wbm_ref, wbs_ref, wo_ref, gain_ref, h2_ref, bm_ref, bs_ref, mg_ref, u_ref):
        bm = _dot(om_ref[...], wbm_ref[...])
        bs = _dot(os_ref[...].astype(BF16), wbs_ref[...])
        mg = (_sigmoid(g_ref[:, :d]) * bm + _sigmoid(g_ref[:, d:]) * bs).astype(BF16)
        bm_ref[...] = bm
        bs_ref[...] = bs
        mg_ref[...] = mg
        x = h_ref[...] + _dot(mg, wo_ref[...])
        h2_ref[...] = x
        u_ref[...] = ((x * _rstd(x, d)) * gain_ref[...]).astype(BF16)

    return pl.pallas_call(
        body, name=name, grid=(s // tm,),
        in_specs=[_rows(tm, om.shape[1]), _rows(tm, SB_WIDTH), _rows(tm, 2 * d), _rows(tm, d),
                  _whole(wbm.shape), _whole(wbs.shape), _whole(wo.shape), _whole((1, d))],
        out_specs=[_rows(tm, d)] * 5,
        out_shape=[jax.ShapeDtypeStruct((s, d), F32), jax.ShapeDtypeStruct((s, d), F32),
                   jax.ShapeDtypeStruct((s, d), F32), jax.ShapeDtypeStruct((s, d), BF16),
                   jax.ShapeDtypeStruct((s, d), BF16)],
        compiler_params=_cp(("parallel",)))(om, osb, gates, h, wbm, wbs, wo, gain)


def _ple_call(h, g, wg, p, wp, tgt, name):
    s, d = h.shape
    tm = min(TM_SMALL, s)

    def body(h_ref, g_ref, wg_ref, p_ref, wp_ref, t_ref, dh_ref, dhs_ref, un_ref, dgl_ref, dpp_ref, dg_ref, sq_ref):
        @pl.when(pl.program_id(0) == 0)
        def _():
            dg_ref[...] = jnp.zeros_like(dg_ref)
            sq_ref[...] = jnp.zeros_like(sq_ref)

        x = h_ref[...]
        gain = g_ref[...]
        r = _rstd(x, d)
        xh = x * r
        un = (xh * gain).astype(BF16)
        sg = _sigmoid(_dot(un, wg_ref[...]))
        pp = _dot(p_ref[...].astype(BF16), wp_ref[...])
        diff = (x + sg * pp) - t_ref[...]
        sq_ref[...] += jnp.sum(diff * diff, axis=0, keepdims=True)
        dy = diff * (1.0 / d)
        dgl = ((dy * pp) * (sg * (1.0 - sg))).astype(BF16)
        dun = _dot_nt(dgl, wg_ref[...])
        dg_ref[...] += jnp.sum(dun * xh, axis=0, keepdims=True)
        dh = dy + _rms_bwd(x, r, gain, dun, d)
        dh_ref[...] = dh
        dhs_ref[...] = (0.5 * dh).astype(BF16)
        un_ref[...] = un
        dgl_ref[...] = dgl
        dpp_ref[...] = (dy * sg).astype(BF16)

    bf = jax.ShapeDtypeStruct((s, d), BF16)
    vec = jax.ShapeDtypeStruct((1, d), F32)
    return pl.pallas_call(
        body, name=name, grid=(s // tm,),
        in_specs=[_rows(tm, d), _whole((1, d)), _whole(wg.shape), _rows(tm, PLE_DIM), _whole(wp.shape), _rows(tm, d)],
        out_specs=[_rows(tm, d)] * 5 + [_whole((1, d))] * 2,
        out_shape=[jax.ShapeDtypeStruct((s, d), F32), bf, bf, bf, bf, vec, vec],
        compiler_params=_cp(("arbitrary",)))(h, g, wg, p, wp, tgt)


def _ffn_bwd_a_call(dhs, a, b, wo, name, rider=None):
    s, n = a.shape
    d = dhs.shape[1]
    tn = n // 2
    tm = min(TM, s)

    def body(dh_ref, a_ref, b_ref, wo_ref, da_ref, db_ref):
        dh = dh_ref[...]
        chunks = [slice(c0, min(c0 + COL_CHUNK, tn)) for c0 in range(0, tn, COL_CHUNK)]
        dhms = [_dot_nt(dh, wo_ref[sl, :]) for sl in chunks]
        for sl, dhm in zip(chunks, dhms):
            av = a_ref[:, sl]
            sa = _sigmoid(av)
            da_ref[:, sl] = (dhm * b_ref[:, sl] * (sa * (1.0 + av * (1.0 - sa)))).astype(BF16)
            db_ref[:, sl] = (dhm * (av * sa)).astype(BF16)

    blk = pl.BlockSpec((tm, tn), lambda j, i: (i, j))
    return _with_rider(
        body, rider, name=name, grid=(n // tn, s // tm),
        in_specs=[pl.BlockSpec((tm, d), lambda j, i: (i, 0)), blk, blk, pl.BlockSpec((tn, d), lambda j, i: (j, 0))],
        out_specs=[blk, blk],
        out_shape=[jax.ShapeDtypeStruct((s, n), BF16)] * 2, args=(dhs, a, b, wo), sem=("parallel", "parallel"))


def _norm_bwd_call(dy_list, w_list, h, g, dh_in, name, half_out, rider=None, w_transposed=False):
    s, d = h.shape
    tm = min(TM_SMALL, s)
    nk, nw = len(dy_list), len(w_list)
    factor = 0.5 if half_out else 1.0
    sharded = nw == 1 and w_list[0].ndim == 3

    def body(*refs):
        dy_refs = refs[:nk]
        w_refs = refs[nk:nk + nw]
        h_ref, g_ref, dhin_ref, dh_ref, dhb_ref, dg_ref = refs[nk + nw:]

        @pl.when(pl.program_id(0) == 0)
        def _():
            dg_ref[...] = jnp.zeros_like(dg_ref)

        if sharded:
            c = w_list[0].shape[2]
            per = dy_list[0].shape[1] // c
            du = None
            for k in range(w_list[0].shape[0]):
                part = _dot_nt(dy_refs[k // per][:, (k % per) * c:(k % per + 1) * c], w_refs[0][k])
                du = part if du is None else du + part
        else:
            mm = _dot if w_transposed else _dot_nt
            du = mm(dy_refs[0][...], w_refs[0][...])
            for dy_ref, w_ref in zip(dy_refs[1:], w_refs[1:]):
                du = du + mm(dy_ref[...], w_ref[...])
        x = h_ref[...]
        r = _rstd(x, d)
        dg_ref[...] += jnp.sum(du * (x * r), axis=0, keepdims=True)
        dh = dhin_ref[...] + _rms_bwd(x, r, g_ref[...], du, d)
        dh_ref[...] = dh
        dhb_ref[...] = (factor * dh).astype(BF16)

    outs, got = _with_rider(
        body, rider, name=name, grid=(s // tm,),
        in_specs=[_rows(tm, dy.shape[1]) for dy in dy_list] + [_whole(w.shape) for w in w_list]
        + [_rows(tm, d), _whole((1, d)), _rows(tm, d)],
        out_specs=[_rows(tm, d), _rows(tm, d), _whole((1, d))],
        out_shape=[jax.ShapeDtypeStruct((s, d), F32), jax.ShapeDtypeStruct((s, d), BF16),
                   jax.ShapeDtypeStruct((1, d), F32)],
        args=(*dy_list, *w_list, h, g, dh_in), sem=("arbitrary",))
    return outs if rider is None else (outs, got)


def _merge_bwd_call(dhb, gates, bm, bs, wo, wbm, wbs, name):
    s, d = bm.shape
    tm = min(TM_SMALL, s)

    def body(dh_ref, g_ref, bm_ref, bs_ref, wo_ref, wbm_ref, wbs_ref, dg_ref, dbm_ref, dbs_ref, dom_ref, dos_ref):
        dmg = _dot_nt(dh_ref[...], wo_ref[...])
        s1 = _sigmoid(g_ref[:, :d])
        s2 = _sigmoid(g_ref[:, d:])
        dg_ref[:, :d] = (dmg * bm_ref[...] * (s1 * (1.0 - s1))).astype(BF16)
        dg_ref[:, d:] = (dmg * bs_ref[...] * (s2 * (1.0 - s2))).astype(BF16)
        dbm = (dmg * s1).astype(BF16)
        dbs = (dmg * s2).astype(BF16)
        dbm_ref[...] = dbm
        dbs_ref[...] = dbs
        dom_ref[...] = _dot_nt(dbm, wbm_ref[...]).astype(BF16)
        dos_ref[...] = _dot_nt(dbs, wbs_ref[...]).astype(BF16)

    wm = wbm.shape[0]
    return pl.pallas_call(
        body, name=name, grid=(s // tm,),
        in_specs=[_rows(tm, d), _rows(tm, 2 * d), _rows(tm, d), _rows(tm, d),
                  _whole(wo.shape), _whole(wbm.shape), _whole(wbs.shape)],
        out_specs=[_rows(tm, 2 * d), _rows(tm, d), _rows(tm, d), _rows(tm, wm), _rows(tm, SB_WIDTH)],
        out_shape=[jax.ShapeDtypeStruct((s, 2 * d), BF16), jax.ShapeDtypeStruct((s, d), BF16),
                   jax.ShapeDtypeStruct((s, d), BF16), jax.ShapeDtypeStruct((s, wm), BF16),
                   jax.ShapeDtypeStruct((s, SB_WIDTH), BF16)],
        compiler_params=_cp(("parallel",)))(dhb, gates, bm, bs, wo, wbm, wbs)


def _mla_prep_bwd_call(cq, ckv, krope, pos, freq, sign, g_ql, g_kvl, g_qh, g_kh, wq, wkv, dq, dk, dv, name):
    s = cq.shape[0]
    tm = min(TM_PREP_BWD, s)
    width = HEADS * HEAD_PAD

    def body(cq_ref, ckv_ref, kr_ref, pos_ref, freq_ref, sign_ref, gql_ref, gkvl_ref, gqh_ref, gkh_ref,
             wq_ref, wkv_ref, dq_ref, dk_ref, dv_ref,
             dcq_ref, dckv_ref, dkr_ref, dwq_ref, dwkv_ref, dgql_ref, dgkvl_ref, dgqh_ref, dgkh_ref, dqr_ref, dkv_ref):
        @pl.when(pl.program_id(0) == 0)
        def _():
            for ref in (dwq_ref, dwkv_ref, dgql_ref, dgkvl_ref, dgqh_ref, dgkh_ref):
                ref[...] = jnp.zeros_like(ref)

        cosv, ssv = _rope_tables(pos_ref, freq_ref, sign_ref)
        xq = cq_ref[...]
        rq = _rstd(xq, Q_LORA)
        cqn = ((xq * rq) * gql_ref[...]).astype(BF16)
        qr = _dot(cqn, wq_ref[...])
        xk = ckv_ref[...]
        rk = _rstd(xk, KV_LORA)
        ckvn = ((xk * rk) * gkvl_ref[...]).astype(BF16)
        kv = _dot(ckvn, wkv_ref[...])
        kr = kr_ref[...]
        lane = _lane((tm, HEAD_PAD))
        dkr = jnp.zeros((tm, HEAD_PAD), F32)
        dgqh = jnp.zeros((1, HEAD_PAD), F32)
        dgkh = jnp.zeros((1, HEAD_PAD), F32)
        for h in range(HEADS):
            sl = slice(h * HEAD_PAD, (h + 1) * HEAD_PAD)
            x = qr[:, sl]
            dx, dgh = _head_bwd(x, _rstd(x, MLA_QK), gqh_ref[...], cosv, ssv, dq_ref[:, sl])
            dqr_ref[:, sl] = dx.astype(BF16)
            dgqh = dgqh + dgh
            x = jnp.where(lane < MLA_NOPE, kv[:, sl], kr)
            dx, dgh = _head_bwd(x, _rstd(x, MLA_QK), gkh_ref[...], cosv, ssv, dk_ref[:, sl])
            dgkh = dgkh + dgh
            dkr = dkr + jnp.where(lane >= MLA_NOPE, dx, 0.0)
            dkv_ref[:, sl] = jnp.where(lane < MLA_NOPE, dx, dv_ref[:, sl]).astype(BF16)
        dgqh_ref[...] += dgqh
        dgkh_ref[...] += dgkh
        dkr_ref[...] = dkr.astype(BF16)
        dqr = dqr_ref[...]
        dkvb = dkv_ref[...]
        dwq_ref[...] += _dot_tn(cqn, dqr)
        dwkv_ref[...] += _dot_tn(ckvn, dkvb)
        dcqn = _dot_nt(dqr, wq_ref[...])
        dgql_ref[...] += jnp.sum(dcqn * (xq * rq), axis=0, keepdims=True)
        dcq_ref[...] = _rms_bwd(xq, rq, gql_ref[...], dcqn, Q_LORA).astype(BF16)
        dckvn = _dot_nt(dkvb, wkv_ref[...])
        dgkvl_ref[...] += jnp.sum(dckvn * (xk * rk), axis=0, keepdims=True)
        dckv_ref[...] = _rms_bwd(xk, rk, gkvl_ref[...], dckvn, KV_LORA).astype(BF16)

    vec = lambda n: jax.ShapeDtypeStruct((1, n), F32)
    outs = pl.pallas_call(
        body, name=name, grid=(s // tm,),
        in_specs=[_rows(tm, Q_LORA), _rows(tm, KV_LORA), _rows(tm, HEAD_PAD), _rows(tm, 1),
                  _whole((1, HEAD_PAD)), _whole((1, HEAD_PAD)), _whole((1, Q_LORA)), _whole((1, KV_LORA)),
                  _whole((1, HEAD_PAD)), _whole((1, HEAD_PAD)), _whole((Q_LORA, width)), _whole((KV_LORA, width)),
                  _rows(tm, width), _rows(tm, width), _rows(tm, width)],
        out_specs=[_rows(tm, Q_LORA), _rows(tm, KV_LORA), _rows(tm, HEAD_PAD), _whole((Q_LORA, width)),
                   _whole((KV_LORA, width)), _whole((1, Q_LORA)), _whole((1, KV_LORA)), _whole((1, HEAD_PAD)),
                   _whole((1, HEAD_PAD)), _rows(tm, width), _rows(tm, width)],
        out_shape=[jax.ShapeDtypeStruct((s, Q_LORA), BF16), jax.ShapeDtypeStruct((s, KV_LORA), BF16),
                   jax.ShapeDtypeStruct((s, HEAD_PAD), BF16), jax.ShapeDtypeStruct((Q_LORA, width), F32),
                   jax.ShapeDtypeStruct((KV_LORA, width), F32), vec(Q_LORA), vec(KV_LORA), vec(HEAD_PAD), vec(HEAD_PAD),
                   jax.ShapeDtypeStruct((s, width), BF16), jax.ShapeDtypeStruct((s, width), BF16)],
        compiler_params=_cp(("arbitrary",)))(cq, ckv, krope, pos, freq, sign, g_ql, g_kvl, g_qh, g_kh, wq, wkv, dq, dk, dv)
    return outs[:9]


def _tn_call(a, b, name, shard_cols=None, rider=None):
    s, ka = a.shape
    nb = b.shape[1]
    ti = _pick(ka, (512, 256, 128))
    if shard_cols is not None:
        tj = shard_cols
    else:
        tj = nb if nb <= TN_MAX_COLS else _pick(nb, (2176, 1024, 512, 256, 128))
    ts = s if 2 * s * (ti + tj) * a.dtype.itemsize <= TN_OPERAND_BYTES else s // 2
    ns = s // ts

    def body(a_ref, b_ref, o_ref, acc_ref):
        part = _dot_tn(a_ref[...].astype(BF16), b_ref[...].astype(BF16))
        if ns == 1:
            o_ref[...] = part.astype(o_ref.dtype)
            return

        @pl.when(pl.program_id(2) == 0)
        def _():
            acc_ref[...] = part

        @pl.when(pl.program_id(2) != 0)
        def _():
            acc_ref[...] += part

        @pl.when(pl.program_id(2) == ns - 1)
        def _():
            o_ref[...] = acc_ref[...].astype(o_ref.dtype)

    if shard_cols is None:
        out_spec = pl.BlockSpec((ti, tj), lambda i, j, t: (i, j))
        out_shape = jax.ShapeDtypeStruct((ka, nb), BF16)
    else:
        out_spec = pl.BlockSpec((None, ti, tj), lambda i, j, t: (j, i, 0))
        out_shape = jax.ShapeDtypeStruct((nb // tj, ka, tj), BF16)
    (out,), got = _with_rider(
        body, rider, name=name, grid=(ka // ti, nb // tj, ns),
        in_specs=[pl.BlockSpec((ts, ti), lambda i, j, t: (t, i)), pl.BlockSpec((ts, tj), lambda i, j, t: (t, j))],
        out_specs=[out_spec], out_shape=[out_shape], scratch=[pltpu.VMEM((ti, tj), F32)], args=(a, b),
        sem=("parallel", "parallel", "arbitrary"))
    return out if rider is None else (out, got)


def _sum_call(parts, out_dtype, name):
    n, r, w = parts.shape
    tr = _row_tile(r)

    def body(p_ref, o_ref):
        acc = p_ref[0].astype(F32)
        for k in range(1, n):
            acc = acc + p_ref[k].astype(F32)
        o_ref[...] = acc.astype(out_dtype)

    return pl.pallas_call(
        body, name=name, grid=(r // tr,),
        in_specs=[pl.BlockSpec((n, tr, w), lambda i: (0, i, 0))], out_specs=_rows(tr, w),
        out_shape=jax.ShapeDtypeStruct((r, w), out_dtype), compiler_params=_cp(("parallel",)))(parts)


def _chip_sum_call(by_chip, core, name):
    n, r, w = by_chip.shape
    tr = _row_tile(r)
    nblk = r // tr

    def body(c_ref, p_ref, o_ref):
        acc = p_ref[0].astype(F32)
        for k in range(1, n):
            acc = acc + p_ref[k].astype(F32)
        o_ref[...] = acc

    return pl.pallas_call(
        body, name=name,
        grid_spec=pltpu.PrefetchScalarGridSpec(
            num_scalar_prefetch=1, grid=(nblk,),
            in_specs=[pl.BlockSpec((n, tr, w), lambda i, c_ref: (0, i, 0))],
            out_specs=pl.BlockSpec((tr, w), lambda i, c_ref: (c_ref[0] * nblk + i, 0))),
        out_shape=jax.ShapeDtypeStruct((2 * r, w), F32),
        compiler_params=_cp(("parallel",)))(core.reshape(1).astype(jnp.int32), by_chip)


def _pair_sum_call(full, other, core, out_dtype, name):
    n, r, w = other.shape
    tr = _row_tile(r)
    nblk = r // tr

    def body(c_ref, a_ref, b_ref, o_ref):
        o_ref[...] = (a_ref[...].astype(F32) + b_ref[...].astype(F32)).astype(out_dtype)

    spec = pl.BlockSpec((None, tr, w), lambda k, i, c_ref: (k, i, 0))
    return pl.pallas_call(
        body, name=name,
        grid_spec=pltpu.PrefetchScalarGridSpec(
            num_scalar_prefetch=1, grid=(n, nblk),
            in_specs=[pl.BlockSpec((None, tr, w), lambda k, i, c_ref: (k, c_ref[0] * nblk + i, 0)), spec],
            out_specs=spec),
        out_shape=jax.ShapeDtypeStruct((n, r, w), out_dtype),
        compiler_params=_cp(("parallel", "parallel")))(core.reshape(1).astype(jnp.int32), full, other)


def _adamw_call(w, g, row0, m, v, name):
    r, c = w.shape
    span = math.gcd(r, row0) if row0 else r
    tr = next((t for t in range(min(span, 256) // 8 * 8, 0, -8) if span % t == 0), span)
    off = row0 // tr

    def body(w_ref, g_ref, m_ref, v_ref, g_out_ref, d_ref, nm_ref, nv_ref):
        gg = g_ref[...]
        g_out_ref[...] = gg
        nm = ADAM_B1 * m_ref[...] + (1.0 - ADAM_B1) * gg
        nv = ADAM_B2 * v_ref[...] + (1.0 - ADAM_B2) * (gg * gg)
        m_hat = nm / (1.0 - ADAM_B1 ** ADAM_STEP)
        v_hat = nv / (1.0 - ADAM_B2 ** ADAM_STEP)
        d_ref[...] = -ADAM_LR * (m_hat / (jnp.sqrt(v_hat) + ADAM_EPS) + ADAM_WD * w_ref[...])
        nm_ref[...] = nm
        nv_ref[...] = nv

    out = jax.ShapeDtypeStruct((r, c), F32)
    g_spec = pl.BlockSpec((tr, c), lambda i: (off + i, 0))
    return pl.pallas_call(
        body, name=name, grid=(r // tr,), in_specs=[_rows(tr, c), g_spec, _rows(tr, c), _rows(tr, c)],
        out_specs=[_rows(tr, c)] * 4, out_shape=[out, out, out, out], compiler_params=_cp(("parallel",)))(w, g, m, v)


def _position():
    x, y, c = lax.axis_index("x"), lax.axis_index("y"), lax.axis_index("c")
    chips = [(1 - x, y), (x, 1 - y), (1 - x, 1 - y)]
    return x, y, c, chips


def _gather_rider(parts):
    n = len(parts)
    pairs = [(j, k) for j in range(3) for k in range(n)]

    def piece(out_refs, k, chip, core):
        half = parts[k].shape[0] // 2
        return out_refs[k].at[2 * chip[0] + chip[1], pl.ds(core * half, half), :]

    def over_ici(in_refs, out_refs, sems, j, k):
        x, y, c, chips = _position()
        half = parts[k].shape[0] // 2
        return pltpu.make_async_remote_copy(
            src_ref=in_refs[k].at[pl.ds(c * half, half), :], dst_ref=piece(out_refs, k, (x, y), c),
            send_sem=sems[0].at[n * j + k], recv_sem=sems[1].at[n * j + k], device_id=(*chips[j], c), device_id_type=MESH)

    def to_sibling(out_refs, sems, j, k):
        x, y, c, chips = _position()
        landed = piece(out_refs, k, chips[j], c)
        return pltpu.make_async_remote_copy(
            src_ref=landed, dst_ref=landed, send_sem=sems[2].at[n * j + k], recv_sem=sems[3].at[n * j + k],
            device_id=(x, y, 1 - c), device_id_type=MESH)

    def start(in_refs, out_refs, sems):
        for j, k in pairs:
            over_ici(in_refs, out_refs, sems, j, k).start()

    def finish(in_refs, out_refs, sems):
        for j, k in pairs:
            over_ici(in_refs, out_refs, sems, j, k).wait_recv()
            to_sibling(out_refs, sems, j, k).start()
        for j, k in pairs:
            to_sibling(out_refs, sems, j, k).wait_recv()
        for j, k in pairs:
            over_ici(in_refs, out_refs, sems, j, k).wait_send()
            to_sibling(out_refs, sems, j, k).wait_send()

    return _Rider(list(parts), [jax.ShapeDtypeStruct((N_CHIPS,) + p.shape, p.dtype) for p in parts], [3 * n] * 4,
                  start, finish)


def _scatter_rider(parts):
    n = len(parts)
    pairs = [(j, k) for j in range(3) for k in range(n)]

    def copy(in_refs, out_refs, sems, j, k):
        x, y, c, chips = _position()
        return pltpu.make_async_remote_copy(
            src_ref=in_refs[k].at[2 * chips[j][0] + chips[j][1]], dst_ref=out_refs[k].at[2 * x + y],
            send_sem=sems[0].at[n * j + k], recv_sem=sems[1].at[n * j + k], device_id=(*chips[j], c), device_id_type=MESH)

    def start(in_refs, out_refs, sems):
        for j, k in pairs:
            copy(in_refs, out_refs, sems, j, k).start()

    def finish(in_refs, out_refs, sems):
        for j, k in pairs:
            copy(in_refs, out_refs, sems, j, k).wait()

    return _Rider(list(parts), [jax.ShapeDtypeStruct(p.shape, p.dtype) for p in parts], [3 * n] * 2, start, finish)


def _pair_send_call(parts, name):
    n = len(parts)

    def body(*refs):
        in_refs, out_refs = refs[:n], refs[n:2 * n]
        send_sems, recv_sems = refs[2 * n:]
        x, y, c, _ = _position()
        copies = []
        for k in range(n):
            half = parts[k].shape[1] // 2
            cp = pltpu.make_async_remote_copy(
                src_ref=in_refs[k].at[:, pl.ds((1 - c) * half, half), :], dst_ref=out_refs[k],
                send_sem=send_sems.at[k], recv_sem=recv_sems.at[k], device_id=(x, y, 1 - c), device_id_type=MESH)
            cp.start()
            copies.append(cp)
        for cp in copies:
            cp.wait()

    sems = pltpu.SemaphoreType.DMA((n,))
    return pl.pallas_call(
        body, name=name, in_specs=[HBM] * n, out_specs=[HBM] * n,
        out_shape=[jax.ShapeDtypeStruct((p.shape[0], p.shape[1] // 2, p.shape[2]), p.dtype) for p in parts],
        scratch_shapes=[sems, sems])(*parts)


def _pair_swap_call(parts, name):
    n = len(parts)

    def body(*refs):
        out_refs = refs[n:2 * n]
        send_sems, recv_sems = refs[2 * n:]
        x, y, c, _ = _position()
        copies = []
        for k in range(n):
            half = parts[k].shape[0] // 2
            mine = out_refs[k].at[pl.ds(c * half, half), :]
            cp = pltpu.make_async_remote_copy(
                src_ref=mine, dst_ref=mine, send_sem=send_sems.at[k], recv_sem=recv_sems.at[k],
                device_id=(x, y, 1 - c), device_id_type=MESH)
            cp.start()
            copies.append(cp)
        for cp in copies:
            cp.wait()

    sems = pltpu.SemaphoreType.DMA((n,))
    return pl.pallas_call(
        body, name=name, in_specs=[HBM] * n, out_specs=[HBM] * n,
        out_shape=[jax.ShapeDtypeStruct(p.shape, p.dtype) for p in parts],
        input_output_aliases={k: k for k in range(n)},
        scratch_shapes=[sems, sems])(*parts)


def _all_gather_small_call(block, name):
    r, w = block.shape

    def body(in_ref, out_ref, send_sems, recv_sems, local_sem):
        x, y, c, _ = _position()
        me = 4 * x + 2 * y + c
        own = pltpu.make_async_copy(in_ref, out_ref.at[me], local_sem)
        own.start()
        copies = []
        for k in range(1, 8):
            peer = (x ^ (k >> 2), y ^ ((k >> 1) & 1), c ^ (k & 1))
            cp = pltpu.make_async_remote_copy(
                src_ref=in_ref, dst_ref=out_ref.at[me], send_sem=send_sems.at[k - 1], recv_sem=recv_sems.at[k - 1],
                device_id=peer, device_id_type=MESH)
            cp.start()
            copies.append(cp)
        for cp in copies:
            cp.wait()
        own.wait()

    return pl.pallas_call(
        body, name=name, in_specs=[HBM], out_specs=HBM,
        out_shape=jax.ShapeDtypeStruct((8, r, w), block.dtype),
        scratch_shapes=[pltpu.SemaphoreType.DMA((7,)), pltpu.SemaphoreType.DMA((7,)), pltpu.SemaphoreType.DMA])(block)


BIG = {
    "ffn1_w_in": ((D_MODEL, 2 * D_FF), 1), "ffn1_w_out": ((D_FF, D_MODEL), 0),
    "w_in": ((D_MODEL, 4256), 1), "w_q_up": ((Q_LORA, HEADS * MLA_QK), 1), "w_kv_up": ((KV_LORA, 1024), 1),
    "w_branch_mla": ((512, D_MODEL), 1), "w_branch_sb": ((SB_WIDTH, D_MODEL), 1), "w_out": ((D_MODEL, D_MODEL), 0),
    "ffn2_w_in": ((D_MODEL, 2 * D_FF), 1), "ffn2_w_out": ((D_FF, D_MODEL), 0),
    "w_ple_gate": ((D_MODEL, D_MODEL), 0), "w_ple_proj": ((PLE_DIM, D_MODEL), 1),
}
GAINS = {"ffn1_norm": 1024, "mix_norm": 1024, "q_latent_norm": 384, "kv_latent_norm": 256, "q_head_norm": 96,
         "k_head_norm": 96, "ffn2_norm": 1024, "ple_norm": 1024}
WEIGHT_ORDER = ["ffn1_norm", "ffn1_w_in", "ffn1_w_out", "mix_norm", "w_in", "q_latent_norm", "w_q_up",
                "kv_latent_norm", "w_kv_up", "q_head_norm", "k_head_norm", "w_branch_mla", "w_branch_sb", "w_out",
                "ffn2_norm", "ffn2_w_in", "ffn2_w_out", "ple_norm", "w_ple_gate", "w_ple_proj"]


W_IN_SHARD_ROWS = 1088


def _shard_shape(name):
    (r, c), axis = BIG[name]
    if name in TRANSPOSED_UPDATE:
        return (W_IN_SHARD_ROWS, r)
    return (r // N_CHIPS, c) if axis == 0 else (r, c // N_CHIPS)


GATHER_GROUPS = [
    [("ffn1_w_in",)],
    [("ffn1_w_out",), ("w_in",)],
    [("w_out",), ("w_kv_up", "w_branch_mla", "w_branch_sb"), ("w_q_up",)],
    [("ffn2_w_in",), ("ffn2_w_out", "w_ple_gate"), ("w_ple_proj",)],
]
REDUCE_GROUPS = [
    [("ffn2_w_in",), ("ffn2_w_out", "w_out", "w_ple_gate"), ("w_branch_mla", "w_branch_sb", "w_ple_proj")],
    [("w_in",), ("w_kv_up",), ("w_q_up",)],
    [("ffn1_w_out",)],
    [("ffn1_w_in",)],
]


def _join_parts(shards, group):
    return [shards[part[0]] if len(part) == 1 else jnp.concatenate([shards[n] for n in part], axis=-2) for part in group]


def _part_rows(group):
    where = {}
    for k, part in enumerate(group):
        at = 0
        for n in part:
            where[n] = (k, at)
            at += _shard_shape(n)[0]
    return where


def _split_parts(parts, group):
    return {n: parts[k][..., at:at + _shard_shape(n)[0], :] for n, (k, at) in _part_rows(group).items()}


def _exchange_form(name, shard):
    if name in TRANSPOSED_UPDATE:
        t = shard.T.astype(BF16)
        return jnp.pad(t, ((0, W_IN_SHARD_ROWS - t.shape[0]), (0, 0)))
    return shard.astype(BF16)


def _to_shards(name, full):
    (r, c), axis = BIG[name]
    if axis == 0:
        return full.reshape(N_CHIPS, r // N_CHIPS, c)
    return full.reshape(r, N_CHIPS, c // N_CHIPS).transpose(1, 0, 2)


def _from_shards(name, shards):
    (r, c), axis = BIG[name]
    if axis == 0:
        return shards.reshape(r, c)
    return shards.transpose(1, 0, 2).reshape(r, c)


def _relayout_w_in(wt):
    d = wt.shape[1]
    z = lambda n: jnp.zeros((n, d), wt.dtype)
    return jnp.concatenate([wt[:640], z(MLA_NOPE), wt[640:672], z(HEAD_PAD - MLA_QK), wt[672:]], axis=0)


def _unlayout_w_in(gt):
    full = jnp.concatenate([gt[:640], gt[640 + MLA_NOPE:640 + MLA_QK], gt[768:]], axis=0)
    shards = full.reshape(N_CHIPS, -1, gt.shape[1])
    return jnp.pad(shards, ((0, 0), (0, W_IN_SHARD_ROWS - shards.shape[1]), (0, 0)))


def _pad_heads(v):
    lead = v.shape[:-1]
    return jnp.pad(v.reshape(lead + (HEADS, MLA_QK)), [(0, 0)] * len(lead) + [(0, 0), (0, HEAD_PAD - MLA_QK)]).reshape(
        lead + (HEADS * HEAD_PAD,))


SHARD_MAJOR = ("ffn1_w_in", "ffn2_w_in")
TRANSPOSED_UPDATE = ("w_in",)


def _step(x, p, pos, tgt, gains, weights, dist):
    d = D_MODEL
    full = {} if dist is not None else {
        n: _to_shards(n, w) if n in SHARD_MAJOR else (w.T if n in TRANSPOSED_UPDATE else w) for n, w in weights.items()}
    reduced = {}

    def gather_rider(g):
        if dist is None:
            return None, None
        mine = _join_parts(weights, GATHER_GROUPS[g])
        return mine, _gather_rider(mine)

    def gathered(g, mine, others):
        if dist is not None:
            parts = [lax.dynamic_update_slice_in_dim(o, m[None], dist[0], axis=0) for o, m in zip(others, mine)]
            for n, shards in _split_parts(parts, GATHER_GROUPS[g]).items():
                if n in TRANSPOSED_UPDATE:
                    (d_in, c_out), _ = BIG[n]
                    full[n] = shards[:, :c_out // N_CHIPS].reshape(c_out, d_in)
                else:
                    full[n] = shards if n in SHARD_MAJOR else _from_shards(n, shards)

    def reduce_before(g):
        if dist is None:
            return None, None
        group = REDUCE_GROUPS[g]
        shards = {n: grads[n] if grads[n].ndim == 3 else _to_shards(n, grads[n].astype(BF16)) for part in group for n in part}
        partial = _join_parts(shards, group)
        from_sibling = _pair_send_call(partial, "grads%d_pair_send" % g)
        pair_sum = [_pair_sum_call(a, b, dist[1], BF16, "grads%d_pair_sum_%d" % (g, k))
                    for k, (a, b) in enumerate(zip(partial, from_sibling))]
        return pair_sum, _scatter_rider(pair_sum)

    def reduce_after(g, pair_sum, by_chip):
        if dist is not None:
            chip, core = dist
            by_chip = [lax.dynamic_update_slice_in_dim(t, lax.dynamic_slice_in_dim(o, chip, 1, axis=0), chip, axis=0)
                       for t, o in zip(by_chip, pair_sum)]
            bufs = _pair_swap_call([_chip_sum_call(t, core, "grads%d_chip_sum_%d" % (g, k)) for k, t in enumerate(by_chip)],
                                   "grads%d_pair_swap" % g)
            for n, (k, row0) in _part_rows(REDUCE_GROUPS[g]).items():
                reduced[n] = (bufs[k], row0)

    mine, rider = gather_rider(0)
    u1, got = _norm_call(x, gains["ffn1_norm"], "norm_ffn1", rider)
    gathered(0, mine, got)
    wts = full
    inv_freq = ROPE_BASE ** (-jnp.arange(0, MLA_ROPE, 2, dtype=F32) / MLA_ROPE)
    zeros = lambda n: jnp.zeros((n,), F32)
    freq = jnp.concatenate([zeros(MLA_NOPE), inv_freq, inv_freq, zeros(HEAD_PAD - MLA_QK)])[None]
    sign = jnp.concatenate([zeros(MLA_NOPE), -jnp.ones((16,), F32), jnp.ones((16,), F32), zeros(HEAD_PAD - MLA_QK)])[None]
    pad_gain = lambda g: jnp.pad(g, ((0, 0), (0, HEAD_PAD - MLA_QK)))
    g_qh, g_kh = pad_gain(gains["q_head_norm"]), pad_gain(gains["k_head_norm"])

    mine, rider = gather_rider(1)
    (a1, b1, hm1), got = _ffn_in_call(u1, wts["ffn1_w_in"], "ffn1_in", rider)
    gathered(1, mine, got)
    mine, rider = gather_rider(2)
    (h1, um), got = _ffn_out_call(hm1, wts["ffn1_w_out"], x, gains["mix_norm"], "ffn1_out", rider)
    gathered(2, mine, got)
    w_in = _relayout_w_in(wts["w_in"])
    wq = _pad_heads(wts["w_q_up"])
    wkv = wts["w_kv_up"]
    wbm = jnp.pad(wts["w_branch_mla"].reshape(HEADS, 64, d), ((0, 0), (64, 0), (0, 0))).reshape(HEADS * HEAD_PAD, d)
    wbs, wo = wts["w_branch_sb"], wts["w_out"]
    cq, ckv, krope, sbq, sbk, sbv, gates = _mix_in_call(um, w_in, "mix_in")
    prep_args = (cq, ckv, krope, pos, freq, sign, gains["q_latent_norm"], gains["kv_latent_norm"], g_qh, g_kh, wq, wkv)
    q, k, v = _mla_prep_call(*prep_args, "mla_prep")
    mine, rider = gather_rider(3)
    (om, lse), got = _mla_fwd_call(q, k, v, "mla_fwd", rider)
    gathered(3, mine, got)
    osb = _sb_fwd_call(sbq, sbk, sbv, "sb_fwd")
    h2, bm, bs, mg, u2 = _merge_out_call(om, osb, gates, h1, wbm, wbs, wo, gains["ffn2_norm"], "merge_out")
    (a2, b2, hm2), _ = _ffn_in_call(u2, wts["ffn2_w_in"], "ffn2_in")
    (h3, _), _ = _ffn_out_call(hm2, wts["ffn2_w_out"], h2, gains["ple_norm"], "ffn2_out")

    grads, gg = {}, {}
    dh3, dh3s, un, dgl, dpp, gg["ple_norm"], sq = _ple_call(
        h3, gains["ple_norm"], wts["w_ple_gate"], p, wts["w_ple_proj"], tgt, "ple")
    grads["w_ple_gate"] = _tn_call(un, dgl, "dw_ple_gate")
    grads["w_ple_proj"] = _tn_call(p, dpp, "dw_ple_proj")

    (da2, db2), _ = _ffn_bwd_a_call(dh3s, a2, b2, wts["ffn2_w_out"], "ffn2_bwd_act")
    grads["ffn2_w_out"] = _tn_call(hm2, dh3s, "dw_ffn2_out")
    grads["ffn2_w_in"] = jnp.concatenate([_tn_call(u2, da2, "dw_ffn2_in_a", shard_cols=D_FF // 2),
                                          _tn_call(u2, db2, "dw_ffn2_in_b", shard_cols=D_FF // 2)], axis=0)
    dh2, dh2b, gg["ffn2_norm"] = _norm_bwd_call([da2, db2], [wts["ffn2_w_in"]], h2, gains["ffn2_norm"], dh3,
                                                "ffn2_bwd_norm", half_out=False)

    dgates, dbm, dbs, dom, dos = _merge_bwd_call(dh2b, gates, bm, bs, wo, wbm, wbs, "merge_bwd")
    grads["w_out"] = _tn_call(mg, dh2b, "dw_out")
    grads["w_branch_mla"] = _tn_call(om, dbm, "dw_branch_mla").reshape(HEADS, HEAD_PAD, d)[:, 64:, :].reshape(512, d)
    grads["w_branch_sb"] = _tn_call(osb, dbs, "dw_branch_sb")
    pair_sum, rider = reduce_before(0)
    (dq, dk, dv), got = _mla_bwd_call(q, k, v, om, dom, lse, "mla_bwd", rider)
    reduce_after(0, pair_sum, got)
    dsq, dsk, dsv = _sb_bwd_call(sbq, sbk, sbv, dos, osb, "sb_bwd")
    (dcq, dckv, dkr, dwq, grads["w_kv_up"], gg["q_latent_norm"], gg["kv_latent_norm"], dgqh, dgkh) = \
        _mla_prep_bwd_call(*prep_args, dq, dk, dv, "mla_prep_bwd")
    grads["w_q_up"] = dwq.reshape(Q_LORA, HEADS, HEAD_PAD)[:, :, :MLA_QK].reshape(Q_LORA, HEADS * MLA_QK)
    gg["q_head_norm"], gg["k_head_norm"] = dgqh[:, :MLA_QK], dgkh[:, :MLA_QK]
    dproj = jnp.concatenate([dcq, dckv, dkr, dsq, dsk.astype(BF16), dsv.astype(BF16), dgates], axis=1)
    grads["w_in"] = _unlayout_w_in(_tn_call(dproj, um, "dw_in"))
    dh1, dh1s, gg["mix_norm"] = _norm_bwd_call([dproj], [w_in], h1, gains["mix_norm"], dh2, "mix_bwd_norm", half_out=True,
                                               w_transposed=True)

    pair_sum, rider = reduce_before(1)
    (da1, db1), got = _ffn_bwd_a_call(dh1s, a1, b1, wts["ffn1_w_out"], "ffn1_bwd_act", rider)
    reduce_after(1, pair_sum, got)
    grads["ffn1_w_out"] = _tn_call(hm1, dh1s, "dw_ffn1_out")
    pair_sum, rider = reduce_before(2)
    res = _tn_call(u1, da1, "dw_ffn1_in_a", shard_cols=D_FF // 2, rider=rider)
    dwa, got = (res, None) if rider is None else res
    reduce_after(2, pair_sum, got)
    grads["ffn1_w_in"] = jnp.concatenate([dwa, _tn_call(u1, db1, "dw_ffn1_in_b", shard_cols=D_FF // 2)], axis=0)
    pair_sum, rider = reduce_before(3)
    res = _norm_bwd_call([da1, db1], [wts["ffn1_w_in"]], x, gains["ffn1_norm"], dh1, "ffn1_bwd_norm",
                         half_out=False, rider=rider)
    (dx, _, gg["ffn1_norm"]), got = (res, None) if rider is None else res
    reduce_after(3, pair_sum, got)
    return sq, dx, gg, (grads if dist is None else reduced)


def kernel(x, p, positions, ffn1_norm, ffn1_w_in, ffn1_w_out, mix_norm, w_in, q_latent_norm, w_q_up, kv_latent_norm, w_kv_up, q_head_norm, k_head_norm, w_branch_mla, w_branch_sb, w_out, ffn2_norm, ffn2_w_in, ffn2_w_out, ple_norm, w_ple_gate, w_ple_proj, loss_target, m_ffn1_norm, m_ffn1_w_in, m_ffn1_w_out, m_mix_norm, m_w_in, m_q_latent_norm, m_w_q_up, m_kv_latent_norm, m_w_kv_up, m_q_head_norm, m_k_head_norm, m_w_branch_mla, m_w_branch_sb, m_w_out, m_ffn2_norm, m_ffn2_w_in, m_ffn2_w_out, m_ple_norm, m_w_ple_gate, m_w_ple_proj, v_ffn1_norm, v_ffn1_w_in, v_ffn1_w_out, v_mix_norm, v_w_in, v_q_latent_norm, v_w_q_up, v_kv_latent_norm, v_w_kv_up, v_q_head_norm, v_k_head_norm, v_w_branch_mla, v_w_branch_sb, v_w_out, v_ffn2_norm, v_ffn2_w_in, v_ffn2_w_out, v_ple_norm, v_w_ple_gate, v_w_ple_proj):
    given = dict(locals())
    w_shard = {n: given[n][0] for n in WEIGHT_ORDER}
    m_shard = {n: given["m_" + n][0] for n in WEIGHT_ORDER}
    v_shard = {n: given["v_" + n][0] for n in WEIGHT_ORDER}
    gains = {n: w_shard[n][None] for n in GAINS}

    chip = 2 * lax.axis_index("x") + lax.axis_index("y")
    sq, dx, gain_grads, reduced = _step(x[0], p[0, 0], positions.reshape(-1, 1), loss_target[0], gains,
                                        {n: _exchange_form(n, w_shard[n]) for n in BIG}, (chip, lax.axis_index("c")))

    rows = [jnp.pad(gain_grads[n], ((0, 0), (0, D_MODEL - GAINS[n]))) for n in GAINS] + [sq]
    gain_block = jnp.concatenate(rows + [jnp.zeros((16 - len(rows), D_MODEL), F32)], axis=0)
    gain_sum = _sum_call(_all_gather_small_call(gain_block, "gains_all_gather"), F32, "gains_sum")
    loss = 0.5 * jnp.sum(gain_sum[len(GAINS)]) / D_MODEL

    outs = {"grad": {}, "delta": {}, "new_m": {}, "new_v": {}}
    gain_pack = lambda t: jnp.concatenate([jnp.pad(t[n][None], ((0, 0), (0, D_MODEL - GAINS[n]))) for n in GAINS], axis=0)
    packed = _adamw_call(gain_pack(w_shard), gain_sum, 0, gain_pack(m_shard), gain_pack(v_shard), "adamw_gains")
    for i, n in enumerate(GAINS):
        for kind, t in zip(("grad", "delta", "new_m", "new_v"), packed):
            outs[kind][n] = t[i, :GAINS[n]][None]
    for n in BIG:
        buf, row0 = reduced[n]
        if n in TRANSPOSED_UPDATE:
            res = [t.T for t in _adamw_call(w_shard[n].T, buf, row0, m_shard[n].T, v_shard[n].T, "adamw_" + n)]
        else:
            res = _adamw_call(w_shard[n], buf, row0, m_shard[n], v_shard[n], "adamw_" + n)
        for kind, t in zip(("grad", "delta", "new_m", "new_v"), res):
            outs[kind][n] = t[None]

    return (loss, dx[None], *[outs["grad"][n] for n in WEIGHT_ORDER], *[outs["delta"][n] for n in WEIGHT_ORDER],
            *[outs["new_m"][n] for n in WEIGHT_ORDER], *[outs["new_v"][n] for n in WEIGHT_ORDER])
```

```python
import collections
import functools
import math

import jax
import jax.numpy as jnp
from jax import lax
from jax.experimental import pallas as pl
from jax.experimental.pallas import tpu as pltpu

F32 = jnp.float32
BF16 = jnp.bfloat16
MESH = pl.DeviceIdType.MESH

D_MODEL = 1024
D_FF = 2816
PLE_DIM = 256
NORM_EPS = 1e-6
HEADS = 8
MLA_NOPE = 64
MLA_ROPE = 32
MLA_QK = 96
Q_LORA = 384
KV_LORA = 256
SB_WIDTH = 512
ROPE_BASE = 10000.0
LOG2_E = math.log2(math.e)
HEAD_PAD = 128
N_CHIPS = 4

ADAM_LR = 0.001
ADAM_B1 = 0.9
ADAM_B2 = 0.999
ADAM_EPS = 1e-08
ADAM_WD = 0.01
ADAM_STEP = 10

SEG_CQ = (0, 384)
SEG_CKV = (384, 256)
SEG_KROPE = (640, 128)
SEG_SBQ = (768, 512)
SEG_SBK = (1280, 512)
SEG_SBV = (1792, 512)
SEG_GATES = (2304, 2048)
IN_COLS_PAD = 4352

TM = 512
TM_SMALL = 512
TM_PREP_BWD = 256
TQ = 256
MLA_FWD_BLOCKS = 4
MLA_BWD_BLOCKS = 4
SB_FWD_BLOCKS = 4
SB_BWD_BLOCKS = 2
SB_HEAD = 64
SB_SCALE = 0.125
SB_DEAD = -104.0
COL_CHUNK = 256
TN_MAX_COLS = 2816
TN_OPERAND_BYTES = 34 * 1024 * 1024
MAX_ROW_TILE = 512
VMEM_LIMIT = 56 * 1024 * 1024

NT = (((1,), (1,)), ((), ()))
TN = (((0,), (0,)), ((), ()))


def _cp(sem):
    return pltpu.CompilerParams(dimension_semantics=sem, vmem_limit_bytes=VMEM_LIMIT)


def _rows(tm, w):
    return pl.BlockSpec((tm, w), lambda i: (i, 0))


def _whole(shape):
    return pl.BlockSpec(shape, lambda i: (0,) * len(shape))


def _dot(a, b):
    return jnp.dot(a, b, preferred_element_type=F32)


def _dot_nt(a, b):
    return lax.dot_general(a, b, NT, preferred_element_type=F32)


def _dot_tn(a, b):
    return lax.dot_general(a, b, TN, preferred_element_type=F32)


def _rstd(x, n):
    return lax.rsqrt(jnp.sum(x * x, axis=-1, keepdims=True) / n + NORM_EPS)


def _rms_bwd(x, r, g, dy, n):
    gy = dy * g
    return r * gy - x * ((r * r * r) * (jnp.sum(x * gy, axis=-1, keepdims=True) / n))


def _sigmoid(x):
    return jax.nn.sigmoid(x)


def _pick(n, cands):
    for c in cands:
        if n % c == 0:
            return c
    return n


def _row_tile(r):
    for t in range(min(r, MAX_ROW_TILE) // 16 * 16, 15, -16):
        if r % t == 0:
            return t
    return r


HBM = pl.BlockSpec(memory_space=pl.ANY)

_Rider = collections.namedtuple("_Rider", "ins out_shape sems start finish")


def _with_rider(body, rider, *, name, grid, in_specs, out_specs, out_shape, args, sem, scratch=()):
    if rider is None:
        return pl.pallas_call(body, name=name, grid=grid, in_specs=in_specs, out_specs=out_specs, out_shape=out_shape,
                              scratch_shapes=list(scratch), compiler_params=_cp(sem))(*args), None
    ni, no, nri, nro = len(in_specs), len(out_specs), len(rider.ins), len(rider.out_shape)

    def riding(*refs):
        ins, r_ins = refs[:ni], refs[ni:ni + nri]
        outs, r_outs = refs[ni + nri:ni + nri + no], refs[ni + nri + no:ni + nri + no + nro]
        scr = refs[ni + nri + no + nro:ni + nri + no + nro + len(scratch)]
        sems = refs[ni + nri + no + nro + len(scratch):]
        ids = [pl.program_id(a) for a in range(len(grid))]
        first = functools.reduce(jnp.logical_and, [i == 0 for i in ids])
        last = functools.reduce(jnp.logical_and, [i == g - 1 for i, g in zip(ids, grid)])

        @pl.when(first)
        def _():
            rider.start(r_ins, r_outs, sems)

        body(*ins, *outs, *scr)

        @pl.when(last)
        def _():
            rider.finish(r_ins, r_outs, sems)

    res = pl.pallas_call(
        riding, name=name, grid=grid, in_specs=list(in_specs) + [HBM] * nri, out_specs=list(out_specs) + [HBM] * nro,
        out_shape=list(out_shape) + list(rider.out_shape),
        scratch_shapes=list(scratch) + [pltpu.SemaphoreType.DMA((k,)) for k in rider.sems],
        compiler_params=_cp(("arbitrary",) * len(grid)))(*args, *rider.ins)
    return res[:no], res[no:]


def _norm_call(h, g, name, rider=None):
    s, d = h.shape
    tm = min(TM, s)

    def body(h_ref, g_ref, u_ref):
        x = h_ref[...]
        u_ref[...] = ((x * _rstd(x, d)) * g_ref[...]).astype(BF16)

    (u,), got = _with_rider(
        body, rider, name=name, grid=(s // tm,),
        in_specs=[_rows(tm, d), _whole((1, d))], out_specs=[_rows(tm, d)],
        out_shape=[jax.ShapeDtypeStruct((s, d), BF16)], args=(h, g), sem=("parallel",))
    return u, got


def _ffn_in_call(u, w, name, rider=None):
    s, d = u.shape
    tn = w.shape[2]
    nj = w.shape[0] // 2
    n = nj * tn
    tm = min(TM, s)

    def body(u_ref, wa_ref, wb_ref, a_ref, b_ref, hm_ref):
        uu = u_ref[...]
        a = _dot(uu, wa_ref[...])
        b = _dot(uu, wb_ref[...])
        a_ref[...] = a
        b_ref[...] = b
        hm_ref[...] = ((a * _sigmoid(a)) * b).astype(BF16)

    blk = pl.BlockSpec((tm, tn), lambda j, i: (i, j))
    return _with_rider(
        body, rider, name=name, grid=(nj, s // tm),
        in_specs=[pl.BlockSpec((tm, d), lambda j, i: (i, 0)),
                  pl.BlockSpec((None, d, tn), lambda j, i: (j, 0, 0)),
                  pl.BlockSpec((None, d, tn), lambda j, i: (j + nj, 0, 0))],
        out_specs=[blk, blk, blk],
        out_shape=[jax.ShapeDtypeStruct((s, n), F32), jax.ShapeDtypeStruct((s, n), F32),
                   jax.ShapeDtypeStruct((s, n), BF16)],
        args=(u, w, w), sem=("parallel", "parallel"))


def _ffn_out_call(hm, w, h, gain, name, rider=None):
    s, n = hm.shape
    d = w.shape[1]
    tm = min(TM, s)

    def body(hm_ref, w_ref, h_ref, g_ref, o_ref, u_ref):
        x = h_ref[...] + 0.5 * _dot(hm_ref[...], w_ref[...])
        o_ref[...] = x
        u_ref[...] = ((x * _rstd(x, d)) * g_ref[...]).astype(BF16)

    return _with_rider(
        body, rider, name=name, grid=(s // tm,),
        in_specs=[_rows(tm, n), _whole((n, d)), _rows(tm, d), _whole((1, d))], out_specs=[_rows(tm, d), _rows(tm, d)],
        out_shape=[jax.ShapeDtypeStruct((s, d), F32), jax.ShapeDtypeStruct((s, d), BF16)], args=(hm, w, h, gain),
        sem=("parallel",))


def _mix_in_call(u, wt, name, rider=None):
    s, d = u.shape
    tm = min(TM_SMALL, s)
    segs = [(SEG_CQ, F32), (SEG_CKV, F32), (SEG_KROPE, F32), (SEG_SBQ, BF16), (SEG_SBK, BF16),
            (SEG_SBV, BF16), (SEG_GATES, F32)]

    def body(u_ref, w_ref, *outs):
        uu = u_ref[...]
        for ((off, width), _), o_ref in zip(segs, outs):
            o_ref[...] = _dot_nt(uu, w_ref[off:off + width, :]).astype(o_ref.dtype)

    return _with_rider(
        body, rider, name=name, grid=(s // tm,),
        in_specs=[_rows(tm, d), _whole((IN_COLS_PAD, d))],
        out_specs=[_rows(tm, width) for (_, width), _ in segs],
        out_shape=[jax.ShapeDtypeStruct((s, width), dt) for (_, width), dt in segs],
        args=(u, wt), sem=("parallel",))


def _lane(shape):
    return lax.broadcasted_iota(jnp.int32, shape, len(shape) - 1)


def _rot_half(y):
    lane = _lane(y.shape)
    swapped = jnp.where(lane < MLA_NOPE + MLA_ROPE // 2, pltpu.roll(y, HEAD_PAD - 16, 1), pltpu.roll(y, 16, 1))
    return jnp.where((lane >= MLA_NOPE) & (lane < MLA_QK), swapped, 0.0)


def _rope_tables(pos_ref, freq_ref, sign_ref):
    ang = pos_ref[...].astype(F32) * freq_ref[...]
    return jnp.cos(ang), jnp.sin(ang) * sign_ref[...]


def _head_fwd(x, g, cosv, ssv):
    r = _rstd(x, MLA_QK)
    y = (x * r) * g
    return y * cosv + _rot_half(y) * ssv, r


def _head_bwd(x, r, g, cosv, ssv, dout):
    dy = dout * cosv + _rot_half(dout * ssv)
    return _rms_bwd(x, r, g, dy, MLA_QK), jnp.sum(dy * (x * r), axis=0, keepdims=True)


def _mla_prep_call(cq, ckv, krope, pos, freq, sign, g_ql, g_kvl, g_qh, g_kh, wq, wkv, name):
    s = cq.shape[0]
    tm = min(TM_SMALL, s)
    width = HEADS * HEAD_PAD

    def body(cq_ref, ckv_ref, kr_ref, pos_ref, freq_ref, sign_ref, gql_ref, gkvl_ref, gqh_ref, gkh_ref,
             wq_ref, wkv_ref, q_ref, k_ref, v_ref):
        cosv, ssv = _rope_tables(pos_ref, freq_ref, sign_ref)
        x = cq_ref[...]
        qr = _dot(((x * _rstd(x, Q_LORA)) * gql_ref[...]).astype(BF16), wq_ref[...])
        x = ckv_ref[...]
        kv = _dot(((x * _rstd(x, KV_LORA)) * gkvl_ref[...]).astype(BF16), wkv_ref[...])
        kr = kr_ref[...]
        lane = _lane((tm, HEAD_PAD))
        for h in range(HEADS):
            sl = slice(h * HEAD_PAD, (h + 1) * HEAD_PAD)
            qh, _ = _head_fwd(qr[:, sl], gqh_ref[...], cosv, ssv)
            q_ref[:, sl] = qh.astype(BF16)
            kvh = kv[:, sl]
            kh, _ = _head_fwd(jnp.where(lane < MLA_NOPE, kvh, kr), gkh_ref[...], cosv, ssv)
            k_ref[:, sl] = kh.astype(BF16)
            v_ref[:, sl] = jnp.where(lane >= MLA_NOPE, kvh, jnp.where(lane == 0, 1.0, 0.0)).astype(BF16)

    out = jax.ShapeDtypeStruct((s, width), BF16)
    return pl.pallas_call(
        body, name=name, grid=(s // tm,),
        in_specs=[_rows(tm, Q_LORA), _rows(tm, KV_LORA), _rows(tm, HEAD_PAD), _rows(tm, 1),
                  _whole((1, HEAD_PAD)), _whole((1, HEAD_PAD)), _whole((1, Q_LORA)), _whole((1, KV_LORA)),
                  _whole((1, HEAD_PAD)), _whole((1, HEAD_PAD)), _whole((Q_LORA, width)), _whole((KV_LORA, width))],
        out_specs=[_rows(tm, width)] * 3, out_shape=[out, out, out],
        compiler_params=_cp(("parallel",)))(cq, ckv, krope, pos, freq, sign, g_ql, g_kvl, g_qh, g_kh, wq, wkv)


def _attn_specs(s, nb):
    qspec = pl.BlockSpec((TQ, nb * HEAD_PAD), lambda g, i: (i, g))
    kspec = pl.BlockSpec((s, nb * HEAD_PAD), lambda g, i: (0, g))
    return qspec, kspec


def _lanes(b):
    return slice(b * HEAD_PAD, (b + 1) * HEAD_PAD)


def _tri(cmp):
    r = lax.broadcasted_iota(jnp.int32, (TQ, TQ), 0)
    c = lax.broadcasted_iota(jnp.int32, (TQ, TQ), 1)
    return cmp(r, c)


def _mla_fwd_call(q, k, v, name, rider=None):
    s, width = q.shape
    scale = 1.0 / math.sqrt(MLA_QK)

    nb = MLA_FWD_BLOCKS

    def body(q_ref, k_ref, v_ref, o_ref, lse_ref):
        qi = pl.program_id(1)
        qs = [q_ref[:, _lanes(b)] for b in range(nb)]
        causal = _tri(lambda r, c: c <= r)

        def step(kb, carry, diag):
            ks = pl.multiple_of(kb * TQ, TQ)
            heads = range(nb)
            scs = [_dot_nt(qs[b], k_ref[pl.ds(ks, TQ), _lanes(b)]) * (scale * LOG2_E) for b in heads]
            if diag:
                scs = [jnp.where(causal, sc, -1e30) for sc in scs]
            mns = [jnp.maximum(carry[b][0], jnp.max(scs[b], axis=-1, keepdims=True)) for b in heads]
            als = [jnp.exp2(carry[b][0] - mns[b]) for b in heads]
            ps = [jnp.exp2(scs[b] - mns[b]).astype(BF16) for b in heads]
            accs = [als[b] * carry[b][1] + _dot(ps[b], v_ref[pl.ds(ks, TQ), _lanes(b)]) for b in heads]
            return tuple((mns[b], accs[b]) for b in heads)

        init = tuple((jnp.full((TQ, 1), -1e30, F32), jnp.zeros((TQ, HEAD_PAD), F32)) for _ in range(nb))
        carry = step(qi, init, True)
        carry = lax.fori_loop(0, qi, lambda kb, c: step(kb, c, False), carry)
        for b in range(nb):
            m, acc = carry[b]
            l = acc[:, 0:1]
            o_ref[:, _lanes(b)] = (acc / l).astype(BF16)
            lse_ref[:, _lanes(b)] = jnp.broadcast_to(m * (1.0 / LOG2_E) + jnp.log(l), (TQ, HEAD_PAD))

    qspec, kspec = _attn_specs(s, nb)
    return _with_rider(
        body, rider, name=name, grid=(width // (nb * HEAD_PAD), s // TQ),
        in_specs=[qspec, kspec, kspec], out_specs=[qspec, qspec],
        out_shape=[jax.ShapeDtypeStruct((s, width), BF16), jax.ShapeDtypeStruct((s, width), F32)],
        args=(q, k, v), sem=("parallel", "arbitrary"))


def _mla_bwd_call(q, k, v, o, do, lse, name, rider=None):
    s, width = q.shape
    scale = 1.0 / math.sqrt(MLA_QK)
    nb = MLA_BWD_BLOCKS

    def body(q_ref, k_ref, v_ref, o_ref, do_ref, lse_ref, dq_ref, dk_ref, dv_ref):
        qi = pl.program_id(1)

        @pl.when(qi == 0)
        def _():
            dk_ref[...] = jnp.zeros_like(dk_ref)
            dv_ref[...] = jnp.zeros_like(dv_ref)

        qs = [q_ref[:, _lanes(b)] for b in range(nb)]
        dos = [do_ref[:, _lanes(b)] for b in range(nb)]
        lses = [lse_ref[:, b * HEAD_PAD:b * HEAD_PAD + 1] for b in range(nb)]
        dlts = [jnp.sum(dos[b].astype(F32) * o_ref[:, _lanes(b)].astype(F32), axis=-1, keepdims=True) for b in range(nb)]
        causal = _tri(lambda r, c: c <= r)

        def step(kb, dqs, diag):
            ks = pl.multiple_of(kb * TQ, TQ)
            heads = range(nb)
            kts = [k_ref[pl.ds(ks, TQ), _lanes(b)] for b in heads]
            scs = [_dot_nt(qs[b], kts[b]) for b in heads]
            dps = [_dot_nt(dos[b], v_ref[pl.ds(ks, TQ), _lanes(b)]) for b in heads]
            ps = [jnp.exp(scs[b] * scale - lses[b]) for b in heads]
            if diag:
                ps = [jnp.where(causal, p, 0.0) for p in ps]
            dss = [(ps[b] * (dps[b] - dlts[b]) * scale).astype(BF16) for b in heads]
            dvs = [_dot_tn(ps[b].astype(BF16), dos[b]) for b in heads]
            dks = [_dot_tn(dss[b], qs[b]) for b in heads]
            out = tuple(dqs[b] + _dot(dss[b], kts[b]) for b in heads)
            for b in heads:
                dv_ref[pl.ds(ks, TQ), _lanes(b)] += dvs[b]
                dk_ref[pl.ds(ks, TQ), _lanes(b)] += dks[b]
            return out

        dqs = step(qi, tuple(jnp.zeros((TQ, HEAD_PAD), F32) for _ in range(nb)), True)
        dqs = lax.fori_loop(0, qi, lambda kb, c: step(kb, c, False), dqs)
        for b in range(nb):
            dq_ref[:, _lanes(b)] = dqs[b]

    qspec, kspec = _attn_specs(s, nb)
    out = jax.ShapeDtypeStruct((s, width), F32)
    return _with_rider(
        body, rider, name=name, grid=(width // (nb * HEAD_PAD), s // TQ),
        in_specs=[qspec, kspec, kspec, qspec, qspec, qspec], out_specs=[qspec, kspec, kspec],
        out_shape=[out, out, out], args=(q, k, v, o, do, lse), sem=("parallel", "arbitrary"))


def _dot_hilo(x, u):
    hi = x.astype(BF16)
    lo = (x - hi.astype(F32)).astype(BF16)
    return _dot(hi, u) + _dot(lo, u)


def _sb_logs(z):
    ls = jnp.minimum(z, 0.0) - jnp.log(1.0 + jnp.exp(-jnp.abs(z)))
    return ls, ls - z


def _sb_head_q(qb, first, hh):
    keep = first if hh == 0 else jnp.logical_not(first)
    return jnp.where(keep, qb, jnp.zeros_like(qb)) * jnp.asarray(SB_SCALE, qb.dtype)


def _sb_fwd_call(q, k, v, name):
    s, width = q.shape
    nb = SB_FWD_BLOCKS
    chains = [(b, hh) for b in range(nb) for hh in range(HEAD_PAD // SB_HEAD)]

    def body(q_ref, k_ref, v_ref, o_ref):
        qi = pl.program_id(1)
        strict = _tri(lambda r, c: c < r)
        after = _tri(lambda r, c: r > c).astype(BF16)
        first = _lane((1, HEAD_PAD)) < SB_HEAD
        qhs = [_sb_head_q(q_ref[:, _lanes(b)], first, hh) for b, hh in chains]

        def step(kb, carry, diag):
            ks = pl.multiple_of(kb * TQ, TQ)
            ids = range(len(chains))
            zs = [_dot_nt(qhs[ci], k_ref[pl.ds(ks, TQ), _lanes(chains[ci][0])]) for ci in ids]
            logs = [_sb_logs(z) for z in zs]
            lss = [lg[0] for lg in logs]
            l1ms = [jnp.where(strict, lg[1], 0.0) if diag else lg[1] for lg in logs]
            sufs = [_dot_hilo(l1m, after) for l1m in l1ms]
            as_ = [jnp.exp(lss[ci] + sufs[ci] + carry[ci][0]) for ci in ids]
            if diag:
                as_ = [jnp.where(strict, a, 0.0) for a in as_]
            accs = [carry[ci][1] + _dot(as_[ci].astype(BF16), v_ref[pl.ds(ks, TQ), _lanes(chains[ci][0])]) for ci in ids]
            return tuple((carry[ci][0] + jnp.sum(l1ms[ci], axis=-1, keepdims=True), accs[ci]) for ci in ids)

        init = tuple((jnp.zeros((TQ, 1), F32), jnp.zeros((TQ, HEAD_PAD), F32)) for _ in chains)
        carry = _sb_sweep(step, qi, init)
        for b in range(nb):
            o_ref[:, _lanes(b)] = jnp.where(first, carry[2 * b][1], carry[2 * b + 1][1])

    qspec, kspec = _attn_specs(s, nb)
    return pl.pallas_call(
        body, name=name, grid=(width // (nb * HEAD_PAD), s // TQ),
        in_specs=[qspec, kspec, kspec], out_specs=qspec, out_shape=jax.ShapeDtypeStruct((s, width), F32),
        compiler_params=_cp(("parallel", "arbitrary")))(q, k, v)


def _sb_sweep(step, qi, init):
    def live(carry):
        top = carry[0][0]
        for c in carry[1:]:
            top = jnp.maximum(top, c[0])
        return jnp.max(top)

    carry = step(qi, init, True)

    def cond(state):
        j, alive, _ = state
        return jnp.logical_and(j < qi, alive > SB_DEAD)

    def body(state):
        j, _, carry = state
        carry = step(qi - 1 - j, carry, False)
        return j + 1, live(carry), carry

    return lax.while_loop(cond, body, (jnp.int32(0), live(carry), carry))[2]


def _sb_bwd_call(q, k, v, do, o, name):
    s, width = q.shape
    nb = SB_BWD_BLOCKS
    chains = [(b, hh) for b in range(nb) for hh in range(HEAD_PAD // SB_HEAD)]

    def body(q_ref, k_ref, v_ref, do_ref, o_ref, dq_ref, dk_ref, dv_ref):
        qi = pl.program_id(1)

        @pl.when(qi == 0)
        def _():
            dk_ref[...] = jnp.zeros_like(dk_ref)
            dv_ref[...] = jnp.zeros_like(dv_ref)

        strict = _tri(lambda r, c: c < r)
        after = _tri(lambda r, c: r > c).astype(BF16)
        from_here = _tri(lambda r, c: r >= c).astype(BF16)
        first = _lane((1, HEAD_PAD)) < SB_HEAD
        qhs = [_sb_head_q(q_ref[:, _lanes(b)], first, hh) for b, hh in chains]
        dohs = []
        for b, hh in chains:
            dob = do_ref[:, _lanes(b)]
            dohs.append(jnp.where(first if hh == 0 else jnp.logical_not(first), dob, jnp.zeros_like(dob)))
        gtots = [jnp.sum(dohs[ci].astype(F32) * o_ref[:, _lanes(chains[ci][0])], axis=-1, keepdims=True)
                 for ci in range(len(chains))]

        def step(kb, carry, diag):
            ks = pl.multiple_of(kb * TQ, TQ)
            ids = range(len(chains))
            kts = [k_ref[pl.ds(ks, TQ), _lanes(b)] for b, _ in chains]
            zs = [_dot_nt(qhs[ci], kts[ci]) for ci in ids]
            das = [_dot_nt(dohs[ci], v_ref[pl.ds(ks, TQ), _lanes(chains[ci][0])]) for ci in ids]
            logs = [_sb_logs(z) for z in zs]
            lss = [lg[0] for lg in logs]
            l1ms = [jnp.where(strict, lg[1], 0.0) if diag else lg[1] for lg in logs]
            sufs = [_dot_hilo(l1m, after) for l1m in l1ms]
            as_ = [jnp.exp(lss[ci] + sufs[ci] + carry[ci][0]) for ci in ids]
            if diag:
                as_ = [jnp.where(strict, a, 0.0) for a in as_]
            abs_ = [a.astype(BF16) for a in as_]
            gs = [abs_[ci].astype(F32) * das[ci] for ci in ids]
            cexs = [gtots[ci] - (carry[ci][1] + _dot_hilo(gs[ci], from_here)) for ci in ids]
            dzs = [gs[ci] - jnp.exp(lss[ci]) * (gs[ci] + cexs[ci]) for ci in ids]
            if diag:
                dzs = [jnp.where(strict, dz, 0.0) for dz in dzs]
            dzbs = [dz.astype(BF16) for dz in dzs]
            dvps = [_dot_tn(abs_[ci], dohs[ci]) for ci in ids]
            dkps = [_dot_tn(dzbs[ci], qhs[ci]) for ci in ids]
            out = tuple((carry[ci][0] + jnp.sum(l1ms[ci], axis=-1, keepdims=True),
                         carry[ci][1] + jnp.sum(gs[ci], axis=-1, keepdims=True),
                         carry[ci][2] + _dot(dzbs[ci], kts[ci])) for ci in ids)
            for b in range(nb):
                dk_ref[pl.ds(ks, TQ), _lanes(b)] += dkps[2 * b] + dkps[2 * b + 1]
                dv_ref[pl.ds(ks, TQ), _lanes(b)] += dvps[2 * b] + dvps[2 * b + 1]
            return out

        init = tuple((jnp.zeros((TQ, 1), F32), jnp.zeros((TQ, 1), F32), jnp.zeros((TQ, HEAD_PAD), F32)) for _ in chains)
        carry = _sb_sweep(step, qi, init)
        for b in range(nb):
            dq_ref[:, _lanes(b)] = (jnp.where(first, carry[2 * b][2], carry[2 * b + 1][2]) * SB_SCALE).astype(BF16)

    qspec, kspec = _attn_specs(s, nb)
    return pl.pallas_call(
        body, name=name, grid=(width // (nb * HEAD_PAD), s // TQ),
        in_specs=[qspec, kspec, kspec, qspec, qspec], out_specs=[qspec, kspec, kspec],
        out_shape=[jax.ShapeDtypeStruct((s, width), BF16), jax.ShapeDtypeStruct((s, width), F32),
                   jax.ShapeDtypeStruct((s, width), F32)],
        compiler_params=_cp(("parallel", "arbitrary")))(q, k, v, do, o)


def _merge_out_call(om, osb, gates, h, wbm, wbs, wo, gain, name):
    s, d = h.shape
    tm = min(TM_SMALL, s)

    def body(om_ref, os_ref, g_ref, h_ref, wbm_ref, wbs_ref, wo_ref, gain_ref, h2_ref, bm_ref, bs_ref, mg_ref, u_ref):
        bm = _dot(om_ref[...], wbm_ref[...])
        bs = _dot(os_ref[...].astype(BF16), wbs_ref[...])
        mg = (_sigmoid(g_ref[:, :d]) * bm + _sigmoid(g_ref[:, d:]) * bs).astype(BF16)
        bm_ref[...] = bm
        bs_ref[...] = bs
        mg_ref[...] = mg
        x = h_ref[...] + _dot(mg, wo_ref[...])
        h2_ref[...] = x
        u_ref[...] = ((x * _rstd(x, d)) * gain_ref[...]).astype(BF16)

    return pl.pallas_call(
        body, name=name, grid=(s // tm,),
        in_specs=[_rows(tm, om.shape[1]), _rows(tm, SB_WIDTH), _rows(tm, 2 * d), _rows(tm, d),
                  _whole(wbm.shape), _whole(wbs.shape), _whole(wo.shape), _whole((1, d))],
        out_specs=[_rows(tm, d)] * 5,
        out_shape=[jax.ShapeDtypeStruct((s, d), F32), jax.ShapeDtypeStruct((s, d), F32),
                   jax.ShapeDtypeStruct((s, d), F32), jax.ShapeDtypeStruct((s, d), BF16),
                   jax.ShapeDtypeStruct((s, d), BF16)],
        compiler_params=_cp(("parallel",)))(om, osb, gates, h, wbm, wbs, wo, gain)


def _ple_call(h, g, wg, p, wp, tgt, name):
    s, d = h.shape
    tm = min(TM_SMALL, s)

    def body(h_ref, g_ref, wg_ref, p_ref, wp_ref, t_ref, dh_ref, dhs_ref, un_ref, dgl_ref, dpp_ref, dg_ref, sq_ref):
        @pl.when(pl.program_id(0) == 0)
        def _():
            dg_ref[...] = jnp.zeros_like(dg_ref)
            sq_ref[...] = jnp.zeros_like(sq_ref)

        x = h_ref[...]
        gain = g_ref[...]
        r = _rstd(x, d)
        xh = x * r
        un = (xh * gain).astype(BF16)
        sg = _sigmoid(_dot(un, wg_ref[...]))
        pp = _dot(p_ref[...].astype(BF16), wp_ref[...])
        diff = (x + sg * pp) - t_ref[...]
        sq_ref[...] += jnp.sum(diff * diff, axis=0, keepdims=True)
        dy = diff * (1.0 / d)
        dgl = ((dy * pp) * (sg * (1.0 - sg))).astype(BF16)
        dun = _dot_nt(dgl, wg_ref[...])
        dg_ref[...] += jnp.sum(dun * xh, axis=0, keepdims=True)
        dh = dy + _rms_bwd(x, r, gain, dun, d)
        dh_ref[...] = dh
        dhs_ref[...] = (0.5 * dh).astype(BF16)
        un_ref[...] = un
        dgl_ref[...] = dgl
        dpp_ref[...] = (dy * sg).astype(BF16)

    bf = jax.ShapeDtypeStruct((s, d), BF16)
    vec = jax.ShapeDtypeStruct((1, d), F32)
    return pl.pallas_call(
        body, name=name, grid=(s // tm,),
        in_specs=[_rows(tm, d), _whole((1, d)), _whole(wg.shape), _rows(tm, PLE_DIM), _whole(wp.shape), _rows(tm, d)],
        out_specs=[_rows(tm, d)] * 5 + [_whole((1, d))] * 2,
        out_shape=[jax.ShapeDtypeStruct((s, d), F32), bf, bf, bf, bf, vec, vec],
        compiler_params=_cp(("arbitrary",)))(h, g, wg, p, wp, tgt)


def _ffn_bwd_a_call(dhs, a, b, wo, name, rider=None):
    s, n = a.shape
    d = dhs.shape[1]
    tn = n // 2
    tm = min(TM, s)

    def body(dh_ref, a_ref, b_ref, wo_ref, da_ref, db_ref):
        dh = dh_ref[...]
        chunks = [slice(c0, min(c0 + COL_CHUNK, tn)) for c0 in range(0, tn, COL_CHUNK)]
        dhms = [_dot_nt(dh, wo_ref[sl, :]) for sl in chunks]
        for sl, dhm in zip(chunks, dhms):
            av = a_ref[:, sl]
            sa = _sigmoid(av)
            da_ref[:, sl] = (dhm * b_ref[:, sl] * (sa * (1.0 + av * (1.0 - sa)))).astype(BF16)
            db_ref[:, sl] = (dhm * (av * sa)).astype(BF16)

    blk = pl.BlockSpec((tm, tn), lambda j, i: (i, j))
    return _with_rider(
        body, rider, name=name, grid=(n // tn, s // tm),
        in_specs=[pl.BlockSpec((tm, d), lambda j, i: (i, 0)), blk, blk, pl.BlockSpec((tn, d), lambda j, i: (j, 0))],
        out_specs=[blk, blk],
        out_shape=[jax.ShapeDtypeStruct((s, n), BF16)] * 2, args=(dhs, a, b, wo), sem=("parallel", "parallel"))


def _norm_bwd_call(dy_list, w_list, h, g, dh_in, name, half_out, rider=None, w_transposed=False):
    s, d = h.shape
    tm = min(TM_SMALL, s)
    nk, nw = len(dy_list), len(w_list)
    factor = 0.5 if half_out else 1.0
    sharded = nw == 1 and w_list[0].ndim == 3

    def body(*refs):
        dy_refs = refs[:nk]
        w_refs = refs[nk:nk + nw]
        h_ref, g_ref, dhin_ref, dh_ref, dhb_ref, dg_ref = refs[nk + nw:]

        @pl.when(pl.program_id(0) == 0)
        def _():
            dg_ref[...] = jnp.zeros_like(dg_ref)

        if sharded:
            c = w_list[0].shape[2]
            per = dy_list[0].shape[1] // c
            du = None
            for k in range(w_list[0].shape[0]):
                part = _dot_nt(dy_refs[k // per][:, (k % per) * c:(k % per + 1) * c], w_refs[0][k])
                du = part if du is None else du + part
        else:
            mm = _dot if w_transposed else _dot_nt
            du = mm(dy_refs[0][...], w_refs[0][...])
            for dy_ref, w_ref in zip(dy_refs[1:], w_refs[1:]):
                du = du + mm(dy_ref[...], w_ref[...])
        x = h_ref[...]
        r = _rstd(x, d)
        dg_ref[...] += jnp.sum(du * (x * r), axis=0, keepdims=True)
        dh = dhin_ref[...] + _rms_bwd(x, r, g_ref[...], du, d)
        dh_ref[...] = dh
        dhb_ref[...] = (factor * dh).astype(BF16)

    outs, got = _with_rider(
        body, rider, name=name, grid=(s // tm,),
        in_specs=[_rows(tm, dy.shape[1]) for dy in dy_list] + [_whole(w.shape) for w in w_list]
        + [_rows(tm, d), _whole((1, d)), _rows(tm, d)],
        out_specs=[_rows(tm, d), _rows(tm, d), _whole((1, d))],
        out_shape=[jax.ShapeDtypeStruct((s, d), F32), jax.ShapeDtypeStruct((s, d), BF16),
                   jax.ShapeDtypeStruct((1, d), F32)],
        args=(*dy_list, *w_list, h, g, dh_in), sem=("arbitrary",))
    return outs if rider is None else (outs, got)


def _merge_bwd_call(dhb, gates, bm, bs, wo, wbm, wbs, name):
    s, d = bm.shape
    tm = min(TM_SMALL, s)

    def body(dh_ref, g_ref, bm_ref, bs_ref, wo_ref, wbm_ref, wbs_ref, dg_ref, dbm_ref, dbs_ref, dom_ref, dos_ref):
        dmg = _dot_nt(dh_ref[...], wo_ref[...])
        s1 = _sigmoid(g_ref[:, :d])
        s2 = _sigmoid(g_ref[:, d:])
        dg_ref[:, :d] = (dmg * bm_ref[...] * (s1 * (1.0 - s1))).astype(BF16)
        dg_ref[:, d:] = (dmg * bs_ref[...] * (s2 * (1.0 - s2))).astype(BF16)
        dbm = (dmg * s1).astype(BF16)
        dbs = (dmg * s2).astype(BF16)
        dbm_ref[...] = dbm
        dbs_ref[...] = dbs
        dom_ref[...] = _dot_nt(dbm, wbm_ref[...]).astype(BF16)
        dos_ref[...] = _dot_nt(dbs, wbs_ref[...]).astype(BF16)

    wm = wbm.shape[0]
    return pl.pallas_call(
        body, name=name, grid=(s // tm,),
        in_specs=[_rows(tm, d), _rows(tm, 2 * d), _rows(tm, d), _rows(tm, d),
                  _whole(wo.shape), _whole(wbm.shape), _whole(wbs.shape)],
        out_specs=[_rows(tm, 2 * d), _rows(tm, d), _rows(tm, d), _rows(tm, wm), _rows(tm, SB_WIDTH)],
        out_shape=[jax.ShapeDtypeStruct((s, 2 * d), BF16), jax.ShapeDtypeStruct((s, d), BF16),
                   jax.ShapeDtypeStruct((s, d), BF16), jax.ShapeDtypeStruct((s, wm), BF16),
                   jax.ShapeDtypeStruct((s, SB_WIDTH), BF16)],
        compiler_params=_cp(("parallel",)))(dhb, gates, bm, bs, wo, wbm, wbs)


def _mla_prep_bwd_call(cq, ckv, krope, pos, freq, sign, g_ql, g_kvl, g_qh, g_kh, wq, wkv, dq, dk, dv, name):
    s = cq.shape[0]
    tm = min(TM_PREP_BWD, s)
    width = HEADS * HEAD_PAD

    def body(cq_ref, ckv_ref, kr_ref, pos_ref, freq_ref, sign_ref, gql_ref, gkvl_ref, gqh_ref, gkh_ref,
             wq_ref, wkv_ref, dq_ref, dk_ref, dv_ref,
             dcq_ref, dckv_ref, dkr_ref, dwq_ref, dwkv_ref, dgql_ref, dgkvl_ref, dgqh_ref, dgkh_ref, dqr_ref, dkv_ref):
        @pl.when(pl.program_id(0) == 0)
        def _():
            for ref in (dwq_ref, dwkv_ref, dgql_ref, dgkvl_ref, dgqh_ref, dgkh_ref):
                ref[...] = jnp.zeros_like(ref)

        cosv, ssv = _rope_tables(pos_ref, freq_ref, sign_ref)
        xq = cq_ref[...]
        rq = _rstd(xq, Q_LORA)
        cqn = ((xq * rq) * gql_ref[...]).astype(BF16)
        qr = _dot(cqn, wq_ref[...])
        xk = ckv_ref[...]
        rk = _rstd(xk, KV_LORA)
        ckvn = ((xk * rk) * gkvl_ref[...]).astype(BF16)
        kv = _dot(ckvn, wkv_ref[...])
        kr = kr_ref[...]
        lane = _lane((tm, HEAD_PAD))
        dkr = jnp.zeros((tm, HEAD_PAD), F32)
        dgqh = jnp.zeros((1, HEAD_PAD), F32)
        dgkh = jnp.zeros((1, HEAD_PAD), F32)
        for h in range(HEADS):
            sl = slice(h * HEAD_PAD, (h + 1) * HEAD_PAD)
            x = qr[:, sl]
            dx, dgh = _head_bwd(x, _rstd(x, MLA_QK), gqh_ref[...], cosv, ssv, dq_ref[:, sl])
            dqr_ref[:, sl] = dx.astype(BF16)
            dgqh = dgqh + dgh
            x = jnp.where(lane < MLA_NOPE, kv[:, sl], kr)
            dx, dgh = _head_bwd(x, _rstd(x, MLA_QK), gkh_ref[...], cosv, ssv, dk_ref[:, sl])
            dgkh = dgkh + dgh
            dkr = dkr + jnp.where(lane >= MLA_NOPE, dx, 0.0)
            dkv_ref[:, sl] = jnp.where(lane < MLA_NOPE, dx, dv_ref[:, sl]).astype(BF16)
        dgqh_ref[...] += dgqh
        dgkh_ref[...] += dgkh
        dkr_ref[...] = dkr.astype(BF16)
        dqr = dqr_ref[...]
        dkvb = dkv_ref[...]
        dwq_ref[...] += _dot_tn(cqn, dqr)
        dwkv_ref[...] += _dot_tn(ckvn, dkvb)
        dcqn = _dot_nt(dqr, wq_ref[...])
        dgql_ref[...] += jnp.sum(dcqn * (xq * rq), axis=0, keepdims=True)
        dcq_ref[...] = _rms_bwd(xq, rq, gql_ref[...], dcqn, Q_LORA).astype(BF16)
        dckvn = _dot_nt(dkvb, wkv_ref[...])
        dgkvl_ref[...] += jnp.sum(dckvn * (xk * rk), axis=0, keepdims=True)
        dckv_ref[...] = _rms_bwd(xk, rk, gkvl_ref[...], dckvn, KV_LORA).astype(BF16)

    vec = lambda n: jax.ShapeDtypeStruct((1, n), F32)
    outs = pl.pallas_call(
        body, name=name, grid=(s // tm,),
        in_specs=[_rows(tm, Q_LORA), _rows(tm, KV_LORA), _rows(tm, HEAD_PAD), _rows(tm, 1),
                  _whole((1, HEAD_PAD)), _whole((1, HEAD_PAD)), _whole((1, Q_LORA)), _whole((1, KV_LORA)),
                  _whole((1, HEAD_PAD)), _whole((1, HEAD_PAD)), _whole((Q_LORA, width)), _whole((KV_LORA, width)),
                  _rows(tm, width), _rows(tm, width), _rows(tm, width)],
        out_specs=[_rows(tm, Q_LORA), _rows(tm, KV_LORA), _rows(tm, HEAD_PAD), _whole((Q_LORA, width)),
                   _whole((KV_LORA, width)), _whole((1, Q_LORA)), _whole((1, KV_LORA)), _whole((1, HEAD_PAD)),
                   _whole((1, HEAD_PAD)), _rows(tm, width), _rows(tm, width)],
        out_shape=[jax.ShapeDtypeStruct((s, Q_LORA), BF16), jax.ShapeDtypeStruct((s, KV_LORA), BF16),
                   jax.ShapeDtypeStruct((s, HEAD_PAD), BF16), jax.ShapeDtypeStruct((Q_LORA, width), F32),
                   jax.ShapeDtypeStruct((KV_LORA, width), F32), vec(Q_LORA), vec(KV_LORA), vec(HEAD_PAD), vec(HEAD_PAD),
                   jax.ShapeDtypeStruct((s, width), BF16), jax.ShapeDtypeStruct((s, width), BF16)],
        compiler_params=_cp(("arbitrary",)))(cq, ckv, krope, pos, freq, sign, g_ql, g_kvl, g_qh, g_kh, wq, wkv, dq, dk, dv)
    return outs[:9]


def _tn_call(a, b, name, shard_cols=None, rider=None):
    s, ka = a.shape
    nb = b.shape[1]
    ti = _pick(ka, (512, 256, 128))
    if shard_cols is not None:
        tj = shard_cols
    else:
        tj = nb if nb <= TN_MAX_COLS else _pick(nb, (2176, 1024, 512, 256, 128))
    ts = s if 2 * s * (ti + tj) * a.dtype.itemsize <= TN_OPERAND_BYTES else s // 2
    ns = s // ts

    def body(a_ref, b_ref, o_ref, acc_ref):
        part = _dot_tn(a_ref[...].astype(BF16), b_ref[...].astype(BF16))
        if ns == 1:
            o_ref[...] = part.astype(o_ref.dtype)
            return

        @pl.when(pl.program_id(2) == 0)
        def _():
            acc_ref[...] = part

        @pl.when(pl.program_id(2) != 0)
        def _():
            acc_ref[...] += part

        @pl.when(pl.program_id(2) == ns - 1)
        def _():
            o_ref[...] = acc_ref[...].astype(o_ref.dtype)

    if shard_cols is None:
        out_spec = pl.BlockSpec((ti, tj), lambda i, j, t: (i, j))
        out_shape = jax.ShapeDtypeStruct((ka, nb), BF16)
    else:
        out_spec = pl.BlockSpec((None, ti, tj), lambda i, j, t: (j, i, 0))
        out_shape = jax.ShapeDtypeStruct((nb // tj, ka, tj), BF16)
    (out,), got = _with_rider(
        body, rider, name=name, grid=(ka // ti, nb // tj, ns),
        in_specs=[pl.BlockSpec((ts, ti), lambda i, j, t: (t, i)), pl.BlockSpec((ts, tj), lambda i, j, t: (t, j))],
        out_specs=[out_spec], out_shape=[out_shape], scratch=[pltpu.VMEM((ti, tj), F32)], args=(a, b),
        sem=("parallel", "parallel", "arbitrary"))
    return out if rider is None else (out, got)


def _sum_call(parts, out_dtype, name):
    n, r, w = parts.shape
    tr = _row_tile(r)

    def body(p_ref, o_ref):
        acc = p_ref[0].astype(F32)
        for k in range(1, n):
            acc = acc + p_ref[k].astype(F32)
        o_ref[...] = acc.astype(out_dtype)

    return pl.pallas_call(
        body, name=name, grid=(r // tr,),
        in_specs=[pl.BlockSpec((n, tr, w), lambda i: (0, i, 0))], out_specs=_rows(tr, w),
        out_shape=jax.ShapeDtypeStruct((r, w), out_dtype), compiler_params=_cp(("parallel",)))(parts)


def _chip_sum_call(by_chip, core, name):
    n, r, w = by_chip.shape
    tr = _row_tile(r)
    nblk = r // tr

    def body(c_ref, p_ref, o_ref):
        acc = p_ref[0].astype(F32)
        for k in range(1, n):
            acc = acc + p_ref[k].astype(F32)
        o_ref[...] = acc

    return pl.pallas_call(
        body, name=name,
        grid_spec=pltpu.PrefetchScalarGridSpec(
            num_scalar_prefetch=1, grid=(nblk,),
            in_specs=[pl.BlockSpec((n, tr, w), lambda i, c_ref: (0, i, 0))],
            out_specs=pl.BlockSpec((tr, w), lambda i, c_ref: (c_ref[0] * nblk + i, 0))),
        out_shape=jax.ShapeDtypeStruct((2 * r, w), F32),
        compiler_params=_cp(("parallel",)))(core.reshape(1).astype(jnp.int32), by_chip)


def _pair_sum_call(full, other, core, out_dtype, name):
    n, r, w = other.shape
    tr = _row_tile(r)
    nblk = r // tr

    def body(c_ref, a_ref, b_ref, o_ref):
        o_ref[...] = (a_ref[...].astype(F32) + b_ref[...].astype(F32)).astype(out_dtype)

    spec = pl.BlockSpec((None, tr, w), lambda k, i, c_ref: (k, i, 0))
    return pl.pallas_call(
        body, name=name,
        grid_spec=pltpu.PrefetchScalarGridSpec(
            num_scalar_prefetch=1, grid=(n, nblk),
            in_specs=[pl.BlockSpec((None, tr, w), lambda k, i, c_ref: (k, c_ref[0] * nblk + i, 0)), spec],
            out_specs=spec),
        out_shape=jax.ShapeDtypeStruct((n, r, w), out_dtype),
        compiler_params=_cp(("parallel", "parallel")))(core.reshape(1).astype(jnp.int32), full, other)


def _adamw_call(w, g, row0, m, v, name):
    r, c = w.shape
    span = math.gcd(r, row0) if row0 else r
    tr = next((t for t in range(min(span, 256) // 8 * 8, 0, -8) if span % t == 0), span)
    off = row0 // tr

    def body(w_ref, g_ref, m_ref, v_ref, g_out_ref, d_ref, nm_ref, nv_ref):
        gg = g_ref[...]
        g_out_ref[...] = gg
        nm = ADAM_B1 * m_ref[...] + (1.0 - ADAM_B1) * gg
        nv = ADAM_B2 * v_ref[...] + (1.0 - ADAM_B2) * (gg * gg)
        m_hat = nm / (1.0 - ADAM_B1 ** ADAM_STEP)
        v_hat = nv / (1.0 - ADAM_B2 ** ADAM_STEP)
        d_ref[...] = -ADAM_LR * (m_hat / (jnp.sqrt(v_hat) + ADAM_EPS) + ADAM_WD * w_ref[...])
        nm_ref[...] = nm
        nv_ref[...] = nv

    out = jax.ShapeDtypeStruct((r, c), F32)
    g_spec = pl.BlockSpec((tr, c), lambda i: (off + i, 0))
    return pl.pallas_call(
        body, name=name, grid=(r // tr,), in_specs=[_rows(tr, c), g_spec, _rows(tr, c), _rows(tr, c)],
        out_specs=[_rows(tr, c)] * 4, out_shape=[out, out, out, out], compiler_params=_cp(("parallel",)))(w, g, m, v)


def _position():
    x, y, c = lax.axis_index("x"), lax.axis_index("y"), lax.axis_index("c")
    chips = [(1 - x, y), (x, 1 - y), (1 - x, 1 - y)]
    return x, y, c, chips


def _gather_rider(parts):
    n = len(parts)
    pairs = [(j, k) for j in range(3) for k in range(n)]

    def piece(out_refs, k, chip, core):
        half = parts[k].shape[0] // 2
        return out_refs[k].at[2 * chip[0] + chip[1], pl.ds(core * half, half), :]

    def over_ici(in_refs, out_refs, sems, j, k):
        x, y, c, chips = _position()
        half = parts[k].shape[0] // 2
        return pltpu.make_async_remote_copy(
            src_ref=in_refs[k].at[pl.ds(c * half, half), :], dst_ref=piece(out_refs, k, (x, y), c),
            send_sem=sems[0].at[n * j + k], recv_sem=sems[1].at[n * j + k], device_id=(*chips[j], c), device_id_type=MESH)

    def to_sibling(out_refs, sems, j, k):
        x, y, c, chips = _position()
        landed = piece(out_refs, k, chips[j], c)
        return pltpu.make_async_remote_copy(
            src_ref=landed, dst_ref=landed, send_sem=sems[2].at[n * j + k], recv_sem=sems[3].at[n * j + k],
            device_id=(x, y, 1 - c), device_id_type=MESH)

    def start(in_refs, out_refs, sems):
        for j, k in pairs:
            over_ici(in_refs, out_refs, sems, j, k).start()

    def finish(in_refs, out_refs, sems):
        for j, k in pairs:
            over_ici(in_refs, out_refs, sems, j, k).wait_recv()
            to_sibling(out_refs, sems, j, k).start()
        for j, k in pairs:
            to_sibling(out_refs, sems, j, k).wait_recv()
        for j, k in pairs:
            over_ici(in_refs, out_refs, sems, j, k).wait_send()
            to_sibling(out_refs, sems, j, k).wait_send()

    return _Rider(list(parts), [jax.ShapeDtypeStruct((N_CHIPS,) + p.shape, p.dtype) for p in parts], [3 * n] * 4,
                  start, finish)


def _scatter_rider(parts):
    n = len(parts)
    pairs = [(j, k) for j in range(3) for k in range(n)]

    def copy(in_refs, out_refs, sems, j, k):
        x, y, c, chips = _position()
        return pltpu.make_async_remote_copy(
            src_ref=in_refs[k].at[2 * chips[j][0] + chips[j][1]], dst_ref=out_refs[k].at[2 * x + y],
            send_sem=sems[0].at[n * j + k], recv_sem=sems[1].at[n * j + k], device_id=(*chips[j], c), device_id_type=MESH)

    def start(in_refs, out_refs, sems):
        for j, k in pairs:
            copy(in_refs, out_refs, sems, j, k).start()

    def finish(in_refs, out_refs, sems):
        for j, k in pairs:
            copy(in_refs, out_refs, sems, j, k).wait()

    return _Rider(list(parts), [jax.ShapeDtypeStruct(p.shape, p.dtype) for p in parts], [3 * n] * 2, start, finish)


def _pair_send_call(parts, name):
    n = len(parts)

    def body(*refs):
        in_refs, out_refs = refs[:n], refs[n:2 * n]
        send_sems, recv_sems = refs[2 * n:]
        x, y, c, _ = _position()
        copies = []
        for k in range(n):
            half = parts[k].shape[1] // 2
            cp = pltpu.make_async_remote_copy(
                src_ref=in_refs[k].at[:, pl.ds((1 - c) * half, half), :], dst_ref=out_refs[k],
                send_sem=send_sems.at[k], recv_sem=recv_sems.at[k], device_id=(x, y, 1 - c), device_id_type=MESH)
            cp.start()
            copies.append(cp)
        for cp in copies:
            cp.wait()

    sems = pltpu.SemaphoreType.DMA((n,))
    return pl.pallas_call(
        body, name=name, in_specs=[HBM] * n, out_specs=[HBM] * n,
        out_shape=[jax.ShapeDtypeStruct((p.shape[0], p.shape[1] // 2, p.shape[2]), p.dtype) for p in parts],
        scratch_shapes=[sems, sems])(*parts)


def _pair_swap_call(parts, name):
    n = len(parts)

    def body(*refs):
        out_refs = refs[n:2 * n]
        send_sems, recv_sems = refs[2 * n:]
        x, y, c, _ = _position()
        copies = []
        for k in range(n):
            half = parts[k].shape[0] // 2
            mine = out_refs[k].at[pl.ds(c * half, half), :]
            cp = pltpu.make_async_remote_copy(
                src_ref=mine, dst_ref=mine, send_sem=send_sems.at[k], recv_sem=recv_sems.at[k],
                device_id=(x, y, 1 - c), device_id_type=MESH)
            cp.start()
            copies.append(cp)
        for cp in copies:
            cp.wait()

    sems = pltpu.SemaphoreType.DMA((n,))
    return pl.pallas_call(
        body, name=name, in_specs=[HBM] * n, out_specs=[HBM] * n,
        out_shape=[jax.ShapeDtypeStruct(p.shape, p.dtype) for p in parts],
        input_output_aliases={k: k for k in range(n)},
        scratch_shapes=[sems, sems])(*parts)


def _all_gather_small_call(block, name):
    r, w = block.shape

    def body(in_ref, out_ref, send_sems, recv_sems, local_sem):
        x, y, c, _ = _position()
        me = 4 * x + 2 * y + c
        own = pltpu.make_async_copy(in_ref, out_ref.at[me], local_sem)
        own.start()
        copies = []
        for k in range(1, 8):
            peer = (x ^ (k >> 2), y ^ ((k >> 1) & 1), c ^ (k & 1))
            cp = pltpu.make_async_remote_copy(
                src_ref=in_ref, dst_ref=out_ref.at[me], send_sem=send_sems.at[k - 1], recv_sem=recv_sems.at[k - 1],
                device_id=peer, device_id_type=MESH)
            cp.start()
            copies.append(cp)
        for cp in copies:
            cp.wait()
        own.wait()

    return pl.pallas_call(
        body, name=name, in_specs=[HBM], out_specs=HBM,
        out_shape=jax.ShapeDtypeStruct((8, r, w), block.dtype),
        scratch_shapes=[pltpu.SemaphoreType.DMA((7,)), pltpu.SemaphoreType.DMA((7,)), pltpu.SemaphoreType.DMA])(block)


BIG = {
    "ffn1_w_in": ((D_MODEL, 2 * D_FF), 1), "ffn1_w_out": ((D_FF, D_MODEL), 0),
    "w_in": ((D_MODEL, 4256), 1), "w_q_up": ((Q_LORA, HEADS * MLA_QK), 1), "w_kv_up": ((KV_LORA, 1024), 1),
    "w_branch_mla": ((512, D_MODEL), 1), "w_branch_sb": ((SB_WIDTH, D_MODEL), 1), "w_out": ((D_MODEL, D_MODEL), 0),
    "ffn2_w_in": ((D_MODEL, 2 * D_FF), 1), "ffn2_w_out": ((D_FF, D_MODEL), 0),
    "w_ple_gate": ((D_MODEL, D_MODEL), 0), "w_ple_proj": ((PLE_DIM, D_MODEL), 1),
}
GAINS = {"ffn1_norm": 1024, "mix_norm": 1024, "q_latent_norm": 384, "kv_latent_norm": 256, "q_head_norm": 96,
         "k_head_norm": 96, "ffn2_norm": 1024, "ple_norm": 1024}
WEIGHT_ORDER = ["ffn1_norm", "ffn1_w_in", "ffn1_w_out", "mix_norm", "w_in", "q_latent_norm", "w_q_up",
                "kv_latent_norm", "w_kv_up", "q_head_norm", "k_head_norm", "w_branch_mla", "w_branch_sb", "w_out",
                "ffn2_norm", "ffn2_w_in", "ffn2_w_out", "ple_norm", "w_ple_gate", "w_ple_proj"]


W_IN_SHARD_ROWS = 1088


def _shard_shape(name):
    (r, c), axis = BIG[name]
    if name in TRANSPOSED_UPDATE:
        return (W_IN_SHARD_ROWS, r)
    return (r // N_CHIPS, c) if axis == 0 else (r, c // N_CHIPS)


GATHER_GROUPS = [
    [("ffn1_w_in",)],
    [("ffn1_w_out",)],
    [("w_in",)],
    [("w_out",), ("w_kv_up", "w_branch_mla", "w_branch_sb"), ("w_q_up",)],
    [("ffn2_w_in",), ("ffn2_w_out", "w_ple_gate"), ("w_ple_proj",)],
]
REDUCE_GROUPS = [
    [("ffn2_w_in",), ("ffn2_w_out", "w_out", "w_ple_gate"), ("w_branch_mla", "w_branch_sb", "w_ple_proj")],
    [("w_in",), ("w_kv_up",), ("w_q_up",)],
    [("ffn1_w_out",)],
    [("ffn1_w_in",)],
]


def _join_parts(shards, group):
    return [shards[part[0]] if len(part) == 1 else jnp.concatenate([shards[n] for n in part], axis=-2) for part in group]


def _part_rows(group):
    where = {}
    for k, part in enumerate(group):
        at = 0
        for n in part:
            where[n] = (k, at)
            at += _shard_shape(n)[0]
    return where


def _split_parts(parts, group):
    return {n: parts[k][..., at:at + _shard_shape(n)[0], :] for n, (k, at) in _part_rows(group).items()}


def _exchange_form(name, shard):
    if name in TRANSPOSED_UPDATE:
        t = shard.T.astype(BF16)
        return jnp.pad(t, ((0, W_IN_SHARD_ROWS - t.shape[0]), (0, 0)))
    return shard.astype(BF16)


def _to_shards(name, full):
    (r, c), axis = BIG[name]
    if axis == 0:
        return full.reshape(N_CHIPS, r // N_CHIPS, c)
    return full.reshape(r, N_CHIPS, c // N_CHIPS).transpose(1, 0, 2)


def _from_shards(name, shards):
    (r, c), axis = BIG[name]
    if axis == 0:
        return shards.reshape(r, c)
    return shards.transpose(1, 0, 2).reshape(r, c)


def _relayout_w_in(wt):
    d = wt.shape[1]
    z = lambda n: jnp.zeros((n, d), wt.dtype)
    return jnp.concatenate([wt[:640], z(MLA_NOPE), wt[640:672], z(HEAD_PAD - MLA_QK), wt[672:]], axis=0)


def _unlayout_w_in(gt):
    full = jnp.concatenate([gt[:640], gt[640 + MLA_NOPE:640 + MLA_QK], gt[768:]], axis=0)
    shards = full.reshape(N_CHIPS, -1, gt.shape[1])
    return jnp.pad(shards, ((0, 0), (0, W_IN_SHARD_ROWS - shards.shape[1]), (0, 0)))


def _pad_heads(v):
    lead = v.shape[:-1]
    return jnp.pad(v.reshape(lead + (HEADS, MLA_QK)), [(0, 0)] * len(lead) + [(0, 0), (0, HEAD_PAD - MLA_QK)]).reshape(
        lead + (HEADS * HEAD_PAD,))


SHARD_MAJOR = ("ffn1_w_in", "ffn2_w_in")
TRANSPOSED_UPDATE = ("w_in",)


def _step(x, p, pos, tgt, gains, weights, dist):
    d = D_MODEL
    full = {} if dist is not None else {
        n: _to_shards(n, w) if n in SHARD_MAJOR else (w.T if n in TRANSPOSED_UPDATE else w) for n, w in weights.items()}
    reduced = {}

    def gather_rider(g):
        if dist is None:
            return None, None
        mine = _join_parts(weights, GATHER_GROUPS[g])
        return mine, _gather_rider(mine)

    def gathered(g, mine, others):
        if dist is not None:
            parts = [lax.dynamic_update_slice_in_dim(o, m[None], dist[0], axis=0) for o, m in zip(others, mine)]
            for n, shards in _split_parts(parts, GATHER_GROUPS[g]).items():
                if n in TRANSPOSED_UPDATE:
                    (d_in, c_out), _ = BIG[n]
                    full[n] = shards[:, :c_out // N_CHIPS].reshape(c_out, d_in)
                else:
                    full[n] = shards if n in SHARD_MAJOR else _from_shards(n, shards)

    def reduce_before(g):
        if dist is None:
            return None, None
        group = REDUCE_GROUPS[g]
        shards = {n: grads[n] if grads[n].ndim == 3 else _to_shards(n, grads[n].astype(BF16)) for part in group for n in part}
        partial = _join_parts(shards, group)
        from_sibling = _pair_send_call(partial, "grads%d_pair_send" % g)
        pair_sum = [_pair_sum_call(a, b, dist[1], BF16, "grads%d_pair_sum_%d" % (g, k))
                    for k, (a, b) in enumerate(zip(partial, from_sibling))]
        return pair_sum, _scatter_rider(pair_sum)

    def reduce_after(g, pair_sum, by_chip):
        if dist is not None:
            chip, core = dist
            by_chip = [lax.dynamic_update_slice_in_dim(t, lax.dynamic_slice_in_dim(o, chip, 1, axis=0), chip, axis=0)
                       for t, o in zip(by_chip, pair_sum)]
            bufs = _pair_swap_call([_chip_sum_call(t, core, "grads%d_chip_sum_%d" % (g, k)) for k, t in enumerate(by_chip)],
                                   "grads%d_pair_swap" % g)
            for n, (k, row0) in _part_rows(REDUCE_GROUPS[g]).items():
                reduced[n] = (bufs[k], row0)

    mine, rider = gather_rider(0)
    u1, got = _norm_call(x, gains["ffn1_norm"], "norm_ffn1", rider)
    gathered(0, mine, got)
    wts = full
    inv_freq = ROPE_BASE ** (-jnp.arange(0, MLA_ROPE, 2, dtype=F32) / MLA_ROPE)
    zeros = lambda n: jnp.zeros((n,), F32)
    freq = jnp.concatenate([zeros(MLA_NOPE), inv_freq, inv_freq, zeros(HEAD_PAD - MLA_QK)])[None]
    sign = jnp.concatenate([zeros(MLA_NOPE), -jnp.ones((16,), F32), jnp.ones((16,), F32), zeros(HEAD_PAD - MLA_QK)])[None]
    pad_gain = lambda g: jnp.pad(g, ((0, 0), (0, HEAD_PAD - MLA_QK)))
    g_qh, g_kh = pad_gain(gains["q_head_norm"]), pad_gain(gains["k_head_norm"])

    mine, rider = gather_rider(1)
    (a1, b1, hm1), got = _ffn_in_call(u1, wts["ffn1_w_in"], "ffn1_in", rider)
    gathered(1, mine, got)
    mine, rider = gather_rider(2)
    (h1, um), got = _ffn_out_call(hm1, wts["ffn1_w_out"], x, gains["mix_norm"], "ffn1_out", rider)
    gathered(2, mine, got)
    w_in = _relayout_w_in(wts["w_in"])
    mine, rider = gather_rider(3)
    (cq, ckv, krope, sbq, sbk, sbv, gates), got = _mix_in_call(um, w_in, "mix_in", rider)
    gathered(3, mine, got)
    wq = _pad_heads(wts["w_q_up"])
    wkv = wts["w_kv_up"]
    wbm = jnp.pad(wts["w_branch_mla"].reshape(HEADS, 64, d), ((0, 0), (64, 0), (0, 0))).reshape(HEADS * HEAD_PAD, d)
    wbs, wo = wts["w_branch_sb"], wts["w_out"]
    prep_args = (cq, ckv, krope, pos, freq, sign, gains["q_latent_norm"], gains["kv_latent_norm"], g_qh, g_kh, wq, wkv)
    q, k, v = _mla_prep_call(*prep_args, "mla_prep")
    mine, rider = gather_rider(4)
    (om, lse), got = _mla_fwd_call(q, k, v, "mla_fwd", rider)
    gathered(4, mine, got)
    osb = _sb_fwd_call(sbq, sbk, sbv, "sb_fwd")
    h2, bm, bs, mg, u2 = _merge_out_call(om, osb, gates, h1, wbm, wbs, wo, gains["ffn2_norm"], "merge_out")
    (a2, b2, hm2), _ = _ffn_in_call(u2, wts["ffn2_w_in"], "ffn2_in")
    (h3, _), _ = _ffn_out_call(hm2, wts["ffn2_w_out"], h2, gains["ple_norm"], "ffn2_out")

    grads, gg = {}, {}
    dh3, dh3s, un, dgl, dpp, gg["ple_norm"], sq = _ple_call(
        h3, gains["ple_norm"], wts["w_ple_gate"], p, wts["w_ple_proj"], tgt, "ple")
    grads["w_ple_gate"] = _tn_call(un, dgl, "dw_ple_gate")
    grads["w_ple_proj"] = _tn_call(p, dpp, "dw_ple_proj")

    (da2, db2), _ = _ffn_bwd_a_call(dh3s, a2, b2, wts["ffn2_w_out"], "ffn2_bwd_act")
    grads["ffn2_w_out"] = _tn_call(hm2, dh3s, "dw_ffn2_out")
    grads["ffn2_w_in"] = jnp.concatenate([_tn_call(u2, da2, "dw_ffn2_in_a", shard_cols=D_FF // 2),
                                          _tn_call(u2, db2, "dw_ffn2_in_b", shard_cols=D_FF // 2)], axis=0)
    dh2, dh2b, gg["ffn2_norm"] = _norm_bwd_call([da2, db2], [wts["ffn2_w_in"]], h2, gains["ffn2_norm"], dh3,
                                                "ffn2_bwd_norm", half_out=False)

    dgates, dbm, dbs, dom, dos = _merge_bwd_call(dh2b, gates, bm, bs, wo, wbm, wbs, "merge_bwd")
    grads["w_out"] = _tn_call(mg, dh2b, "dw_out")
    grads["w_branch_mla"] = _tn_call(om, dbm, "dw_branch_mla").reshape(HEADS, HEAD_PAD, d)[:, 64:, :].reshape(512, d)
    grads["w_branch_sb"] = _tn_call(osb, dbs, "dw_branch_sb")
    pair_sum, rider = reduce_before(0)
    (dq, dk, dv), got = _mla_bwd_call(q, k, v, om, dom, lse, "mla_bwd", rider)
    reduce_after(0, pair_sum, got)
    dsq, dsk, dsv = _sb_bwd_call(sbq, sbk, sbv, dos, osb, "sb_bwd")
    (dcq, dckv, dkr, dwq, grads["w_kv_up"], gg["q_latent_norm"], gg["kv_latent_norm"], dgqh, dgkh) = \
        _mla_prep_bwd_call(*prep_args, dq, dk, dv, "mla_prep_bwd")
    grads["w_q_up"] = dwq.reshape(Q_LORA, HEADS, HEAD_PAD)[:, :, :MLA_QK].reshape(Q_LORA, HEADS * MLA_QK)
    gg["q_head_norm"], gg["k_head_norm"] = dgqh[:, :MLA_QK], dgkh[:, :MLA_QK]
    dproj = jnp.concatenate([dcq, dckv, dkr, dsq, dsk.astype(BF16), dsv.astype(BF16), dgates], axis=1)
    grads["w_in"] = _unlayout_w_in(_tn_call(dproj, um, "dw_in"))
    dh1, dh1s, gg["mix_norm"] = _norm_bwd_call([dproj], [w_in], h1, gains["mix_norm"], dh2, "mix_bwd_norm", half_out=True,
                                               w_transposed=True)

    pair_sum, rider = reduce_before(1)
    (da1, db1), got = _ffn_bwd_a_call(dh1s, a1, b1, wts["ffn1_w_out"], "ffn1_bwd_act", rider)
    reduce_after(1, pair_sum, got)
    grads["ffn1_w_out"] = _tn_call(hm1, dh1s, "dw_ffn1_out")
    pair_sum, rider = reduce_before(2)
    res = _tn_call(u1, da1, "dw_ffn1_in_a", shard_cols=D_FF // 2, rider=rider)
    dwa, got = (res, None) if rider is None else res
    reduce_after(2, pair_sum, got)
    grads["ffn1_w_in"] = jnp.concatenate([dwa, _tn_call(u1, db1, "dw_ffn1_in_b", shard_cols=D_FF // 2)], axis=0)
    pair_sum, rider = reduce_before(3)
    res = _norm_bwd_call([da1, db1], [wts["ffn1_w_in"]], x, gains["ffn1_norm"], dh1, "ffn1_bwd_norm",
                         half_out=False, rider=rider)
    (dx, _, gg["ffn1_norm"]), got = (res, None) if rider is None else res
    reduce_after(3, pair_sum, got)
    return sq, dx, gg, (grads if dist is None else reduced)


def kernel(x, p, positions, ffn1_norm, ffn1_w_in, ffn1_w_out, mix_norm, w_in, q_latent_norm, w_q_up, kv_latent_norm, w_kv_up, q_head_norm, k_head_norm, w_branch_mla, w_branch_sb, w_out, ffn2_norm, ffn2_w_in, ffn2_w_out, ple_norm, w_ple_gate, w_ple_proj, loss_target, m_ffn1_norm, m_ffn1_w_in, m_ffn1_w_out, m_mix_norm, m_w_in, m_q_latent_norm, m_w_q_up, m_kv_latent_norm, m_w_kv_up, m_q_head_norm, m_k_head_norm, m_w_branch_mla, m_w_branch_sb, m_w_out, m_ffn2_norm, m_ffn2_w_in, m_ffn2_w_out, m_ple_norm, m_w_ple_gate, m_w_ple_proj, v_ffn1_norm, v_ffn1_w_in, v_ffn1_w_out, v_mix_norm, v_w_in, v_q_latent_norm, v_w_q_up, v_kv_latent_norm, v_w_kv_up, v_q_head_norm, v_k_head_norm, v_w_branch_mla, v_w_branch_sb, v_w_out, v_ffn2_norm, v_ffn2_w_in, v_ffn2_w_out, v_ple_norm, v_w_ple_gate, v_w_ple_proj):
    given = dict(locals())
    w_shard = {n: given[n][0] for n in WEIGHT_ORDER}
    m_shard = {n: given["m_" + n][0] for n in WEIGHT_ORDER}
    v_shard = {n: given["v_" + n][0] for n in WEIGHT_ORDER}
    gains = {n: w_shard[n][None] for n in GAINS}

    chip = 2 * lax.axis_index("x") + lax.axis_index("y")
    sq, dx, gain_grads, reduced = _step(x[0], p[0, 0], positions.reshape(-1, 1), loss_target[0], gains,
                                        {n: _exchange_form(n, w_shard[n]) for n in BIG}, (chip, lax.axis_index("c")))

    rows = [jnp.pad(gain_grads[n], ((0, 0), (0, D_MODEL - GAINS[n]))) for n in GAINS] + [sq]
    gain_block = jnp.concatenate(rows + [jnp.zeros((16 - len(rows), D_MODEL), F32)], axis=0)
    gain_sum = _sum_call(_all_gather_small_call(gain_block, "gains_all_gather"), F32, "gains_sum")
    loss = 0.5 * jnp.sum(gain_sum[len(GAINS)]) / D_MODEL

    outs = {"grad": {}, "delta": {}, "new_m": {}, "new_v": {}}
    gain_pack = lambda t: jnp.concatenate([jnp.pad(t[n][None], ((0, 0), (0, D_MODEL - GAINS[n]))) for n in GAINS], axis=0)
    packed = _adamw_call(gain_pack(w_shard), gain_sum, 0, gain_pack(m_shard), gain_pack(v_shard), "adamw_gains")
    for i, n in enumerate(GAINS):
        for kind, t in zip(("grad", "delta", "new_m", "new_v"), packed):
            outs[kind][n] = t[i, :GAINS[n]][None]
    for n in BIG:
        buf, row0 = reduced[n]
        if n in TRANSPOSED_UPDATE:
            res = [t.T for t in _adamw_call(w_shard[n].T, buf, row0, m_shard[n].T, v_shard[n].T, "adamw_" + n)]
        else:
            res = _adamw_call(w_shard[n], buf, row0, m_shard[n], v_shard[n], "adamw_" + n)
        for kind, t in zip(("grad", "delta", "new_m", "new_v"), res):
            outs[kind][n] = t[None]

    return (loss, dx[None], *[outs["grad"][n] for n in WEIGHT_ORDER], *[outs["delta"][n] for n in WEIGHT_ORDER],
            *[outs["new_m"][n] for n in WEIGHT_ORDER], *[outs["new_v"][n] for n in WEIGHT_ORDER])
```

```python
import collections
import functools
import math

import jax
import jax.numpy as jnp
from jax import lax
from jax.experimental import pallas as pl
from jax.experimental.pallas import tpu as pltpu

F32 = jnp.float32
BF16 = jnp.bfloat16
MESH = pl.DeviceIdType.MESH

D_MODEL = 1024
D_FF = 2816
PLE_DIM = 256
NORM_EPS = 1e-6
HEADS = 8
MLA_NOPE = 64
MLA_ROPE = 32
MLA_QK = 96
Q_LORA = 384
KV_LORA = 256
SB_WIDTH = 512
ROPE_BASE = 10000.0
LOG2_E = math.log2(math.e)
HEAD_PAD = 128
N_CHIPS = 4

ADAM_LR = 0.001
ADAM_B1 = 0.9
ADAM_B2 = 0.999
ADAM_EPS = 1e-08
ADAM_WD = 0.01
ADAM_STEP = 10

SEG_CQ = (0, 384)
SEG_CKV = (384, 256)
SEG_KROPE = (640, 128)
SEG_SBQ = (768, 512)
SEG_SBK = (1280, 512)
SEG_SBV = (1792, 512)
SEG_GATES = (2304, 2048)
IN_COLS_PAD = 4352

TM = 512
TM_SMALL = 512
TM_PREP_BWD = 256
TQ = 256
MLA_FWD_BLOCKS = 4
MLA_BWD_BLOCKS = 4
SB_FWD_BLOCKS = 4
SB_BWD_BLOCKS = 2
SB_HEAD = 64
SB_SCALE = 0.125
SB_DEAD = -104.0
COL_CHUNK = 256
TN_MAX_COLS = 2816
TN_OPERAND_BYTES = 34 * 1024 * 1024
MAX_ROW_TILE = 512
VMEM_LIMIT = 56 * 1024 * 1024

NT = (((1,), (1,)), ((), ()))
TN = (((0,), (0,)), ((), ()))


def _cp(sem):
    return pltpu.CompilerParams(dimension_semantics=sem, vmem_limit_bytes=VMEM_LIMIT)


def _rows(tm, w):
    return pl.BlockSpec((tm, w), lambda i: (i, 0))


def _whole(shape):
    return pl.BlockSpec(shape, lambda i: (0,) * len(shape))


def _dot(a, b):
    return jnp.dot(a, b, preferred_element_type=F32)


def _dot_nt(a, b):
    return lax.dot_general(a, b, NT, preferred_element_type=F32)


def _dot_tn(a, b):
    return lax.dot_general(a, b, TN, preferred_element_type=F32)


def _rstd(x, n):
    return lax.rsqrt(jnp.sum(x * x, axis=-1, keepdims=True) / n + NORM_EPS)


def _rms_bwd(x, r, g, dy, n):
    gy = dy * g
    return r * gy - x * ((r * r * r) * (jnp.sum(x * gy, axis=-1, keepdims=True) / n))


def _sigmoid(x):
    return jax.nn.sigmoid(x)


def _pick(n, cands):
    for c in cands:
        if n % c == 0:
            return c
    return n


def _row_tile(r):
    for t in range(min(r, MAX_ROW_TILE) // 16 * 16, 15, -16):
        if r % t == 0:
            return t
    return r


HBM = pl.BlockSpec(memory_space=pl.ANY)

_Rider = collections.namedtuple("_Rider", "ins out_shape sems start relay finish")


def _with_rider(body, rider, *, name, grid, in_specs, out_specs, out_shape, args, sem, scratch=()):
    if rider is None:
        return pl.pallas_call(body, name=name, grid=grid, in_specs=in_specs, out_specs=out_specs, out_shape=out_shape,
                              scratch_shapes=list(scratch), compiler_params=_cp(sem))(*args), None
    ni, no, nri, nro = len(in_specs), len(out_specs), len(rider.ins), len(rider.out_shape)

    def riding(*refs):
        ins, r_ins = refs[:ni], refs[ni:ni + nri]
        outs, r_outs = refs[ni + nri:ni + nri + no], refs[ni + nri + no:ni + nri + no + nro]
        scr = refs[ni + nri + no + nro:ni + nri + no + nro + len(scratch)]
        sems = refs[ni + nri + no + nro + len(scratch):]
        step = pl.program_id(0)
        for a in range(1, len(grid)):
            step = step * grid[a] + pl.program_id(a)
        steps = math.prod(grid)

        @pl.when(step == 0)
        def _():
            rider.start(r_ins, r_outs, sems)

        body(*ins, *outs, *scr)

        if steps >= 3:
            @pl.when(step == steps - 2)
            def _():
                rider.relay(r_ins, r_outs, sems)

        @pl.when(step == steps - 1)
        def _():
            if steps < 3:
                rider.relay(r_ins, r_outs, sems)
            rider.finish(r_ins, r_outs, sems)

    res = pl.pallas_call(
        riding, name=name, grid=grid, in_specs=list(in_specs) + [HBM] * nri, out_specs=list(out_specs) + [HBM] * nro,
        out_shape=list(out_shape) + list(rider.out_shape),
        scratch_shapes=list(scratch) + [pltpu.SemaphoreType.DMA((k,)) for k in rider.sems],
        compiler_params=_cp(("arbitrary",) * len(grid)))(*args, *rider.ins)
    return res[:no], res[no:]


def _norm_call(h, g, name, rider=None):
    s, d = h.shape
    tm = min(TM, s)

    def body(h_ref, g_ref, u_ref):
        x = h_ref[...]
        u_ref[...] = ((x * _rstd(x, d)) * g_ref[...]).astype(BF16)

    (u,), got = _with_rider(
        body, rider, name=name, grid=(s // tm,),
        in_specs=[_rows(tm, d), _whole((1, d))], out_specs=[_rows(tm, d)],
        out_shape=[jax.ShapeDtypeStruct((s, d), BF16)], args=(h, g), sem=("parallel",))
    return u, got


def _ffn_in_call(u, w, name, rider=None):
    s, d = u.shape
    tn = w.shape[2]
    nj = w.shape[0] // 2
    n = nj * tn
    tm = min(TM, s)

    def body(u_ref, wa_ref, wb_ref, a_ref, b_ref, hm_ref):
        uu = u_ref[...]
        a = _dot(uu, wa_ref[...])
        b = _dot(uu, wb_ref[...])
        a_ref[...] = a
        b_ref[...] = b
        hm_ref[...] = ((a * _sigmoid(a)) * b).astype(BF16)

    blk = pl.BlockSpec((tm, tn), lambda j, i: (i, j))
    return _with_rider(
        body, rider, name=name, grid=(nj, s // tm),
        in_specs=[pl.BlockSpec((tm, d), lambda j, i: (i, 0)),
                  pl.BlockSpec((None, d, tn), lambda j, i: (j, 0, 0)),
                  pl.BlockSpec((None, d, tn), lambda j, i: (j + nj, 0, 0))],
        out_specs=[blk, blk, blk],
        out_shape=[jax.ShapeDtypeStruct((s, n), F32), jax.ShapeDtypeStruct((s, n), F32),
                   jax.ShapeDtypeStruct((s, n), BF16)],
        args=(u, w, w), sem=("parallel", "parallel"))


def _ffn_out_call(hm, w, h, gain, name, rider=None):
    s, n = hm.shape
    d = w.shape[1]
    tm = min(TM, s)

    def body(hm_ref, w_ref, h_ref, g_ref, o_ref, u_ref):
        x = h_ref[...] + 0.5 * _dot(hm_ref[...], w_ref[...])
        o_ref[...] = x
        u_ref[...] = ((x * _rstd(x, d)) * g_ref[...]).astype(BF16)

    return _with_rider(
        body, rider, name=name, grid=(s // tm,),
        in_specs=[_rows(tm, n), _whole((n, d)), _rows(tm, d), _whole((1, d))], out_specs=[_rows(tm, d), _rows(tm, d)],
        out_shape=[jax.ShapeDtypeStruct((s, d), F32), jax.ShapeDtypeStruct((s, d), BF16)], args=(hm, w, h, gain),
        sem=("parallel",))


def _mix_in_call(u, wt, name, rider=None):
    s, d = u.shape
    tm = min(TM_SMALL, s)
    segs = [(SEG_CQ, F32), (SEG_CKV, F32), (SEG_KROPE, F32), (SEG_SBQ, BF16), (SEG_SBK, BF16),
            (SEG_SBV, BF16), (SEG_GATES, F32)]

    def body(u_ref, w_ref, *outs):
        uu = u_ref[...]
        for ((off, width), _), o_ref in zip(segs, outs):
            o_ref[...] = _dot_nt(uu, w_ref[off:off + width, :]).astype(o_ref.dtype)

    return _with_rider(
        body, rider, name=name, grid=(s // tm,),
        in_specs=[_rows(tm, d), _whole((IN_COLS_PAD, d))],
        out_specs=[_rows(tm, width) for (_, width), _ in segs],
        out_shape=[jax.ShapeDtypeStruct((s, width), dt) for (_, width), dt in segs],
        args=(u, wt), sem=("parallel",))


def _lane(shape):
    return lax.broadcasted_iota(jnp.int32, shape, len(shape) - 1)


def _rot_half(y):
    lane = _lane(y.shape)
    swapped = jnp.where(lane < MLA_NOPE + MLA_ROPE // 2, pltpu.roll(y, HEAD_PAD - 16, 1), pltpu.roll(y, 16, 1))
    return jnp.where((lane >= MLA_NOPE) & (lane < MLA_QK), swapped, 0.0)


def _rope_tables(pos_ref, freq_ref, sign_ref):
    ang = pos_ref[...].astype(F32) * freq_ref[...]
    return jnp.cos(ang), jnp.sin(ang) * sign_ref[...]


def _head_fwd(x, g, cosv, ssv):
    r = _rstd(x, MLA_QK)
    y = (x * r) * g
    return y * cosv + _rot_half(y) * ssv, r


def _head_bwd(x, r, g, cosv, ssv, dout):
    dy = dout * cosv + _rot_half(dout * ssv)
    return _rms_bwd(x, r, g, dy, MLA_QK), jnp.sum(dy * (x * r), axis=0, keepdims=True)


def _mla_prep_call(cq, ckv, krope, pos, freq, sign, g_ql, g_kvl, g_qh, g_kh, wq, wkv, name):
    s = cq.shape[0]
    tm = min(TM_SMALL, s)
    width = HEADS * HEAD_PAD

    def body(cq_ref, ckv_ref, kr_ref, pos_ref, freq_ref, sign_ref, gql_ref, gkvl_ref, gqh_ref, gkh_ref,
             wq_ref, wkv_ref, q_ref, k_ref, v_ref):
        cosv, ssv = _rope_tables(pos_ref, freq_ref, sign_ref)
        x = cq_ref[...]
        qr = _dot(((x * _rstd(x, Q_LORA)) * gql_ref[...]).astype(BF16), wq_ref[...])
        x = ckv_ref[...]
        kv = _dot(((x * _rstd(x, KV_LORA)) * gkvl_ref[...]).astype(BF16), wkv_ref[...])
        kr = kr_ref[...]
        lane = _lane((tm, HEAD_PAD))
        for h in range(HEADS):
            sl = slice(h * HEAD_PAD, (h + 1) * HEAD_PAD)
            qh, _ = _head_fwd(qr[:, sl], gqh_ref[...], cosv, ssv)
            q_ref[:, sl] = qh.astype(BF16)
            kvh = kv[:, sl]
            kh, _ = _head_fwd(jnp.where(lane < MLA_NOPE, kvh, kr), gkh_ref[...], cosv, ssv)
            k_ref[:, sl] = kh.astype(BF16)
            v_ref[:, sl] = jnp.where(lane >= MLA_NOPE, kvh, jnp.where(lane == 0, 1.0, 0.0)).astype(BF16)

    out = jax.ShapeDtypeStruct((s, width), BF16)
    return pl.pallas_call(
        body, name=name, grid=(s // tm,),
        in_specs=[_rows(tm, Q_LORA), _rows(tm, KV_LORA), _rows(tm, HEAD_PAD), _rows(tm, 1),
                  _whole((1, HEAD_PAD)), _whole((1, HEAD_PAD)), _whole((1, Q_LORA)), _whole((1, KV_LORA)),
                  _whole((1, HEAD_PAD)), _whole((1, HEAD_PAD)), _whole((Q_LORA, width)), _whole((KV_LORA, width))],
        out_specs=[_rows(tm, width)] * 3, out_shape=[out, out, out],
        compiler_params=_cp(("parallel",)))(cq, ckv, krope, pos, freq, sign, g_ql, g_kvl, g_qh, g_kh, wq, wkv)


def _attn_specs(s, nb):
    qspec = pl.BlockSpec((TQ, nb * HEAD_PAD), lambda g, i: (i, g))
    kspec = pl.BlockSpec((s, nb * HEAD_PAD), lambda g, i: (0, g))
    return qspec, kspec


def _lanes(b):
    return slice(b * HEAD_PAD, (b + 1) * HEAD_PAD)


def _tri(cmp):
    r = lax.broadcasted_iota(jnp.int32, (TQ, TQ), 0)
    c = lax.broadcasted_iota(jnp.int32, (TQ, TQ), 1)
    return cmp(r, c)


def _mla_fwd_call(q, k, v, name, rider=None):
    s, width = q.shape
    scale = 1.0 / math.sqrt(MLA_QK)

    nb = MLA_FWD_BLOCKS

    def body(q_ref, k_ref, v_ref, o_ref, lse_ref):
        qi = pl.program_id(1)
        qs = [q_ref[:, _lanes(b)] for b in range(nb)]
        causal = _tri(lambda r, c: c <= r)

        def step(kb, carry, diag):
            ks = pl.multiple_of(kb * TQ, TQ)
            heads = range(nb)
            scs = [_dot_nt(qs[b], k_ref[pl.ds(ks, TQ), _lanes(b)]) * (scale * LOG2_E) for b in heads]
            if diag:
                scs = [jnp.where(causal, sc, -1e30) for sc in scs]
            mns = [jnp.maximum(carry[b][0], jnp.max(scs[b], axis=-1, keepdims=True)) for b in heads]
            als = [jnp.exp2(carry[b][0] - mns[b]) for b in heads]
            ps = [jnp.exp2(scs[b] - mns[b]).astype(BF16) for b in heads]
            accs = [als[b] * carry[b][1] + _dot(ps[b], v_ref[pl.ds(ks, TQ), _lanes(b)]) for b in heads]
            return tuple((mns[b], accs[b]) for b in heads)

        init = tuple((jnp.full((TQ, 1), -1e30, F32), jnp.zeros((TQ, HEAD_PAD), F32)) for _ in range(nb))
        carry = step(qi, init, True)
        carry = lax.fori_loop(0, qi, lambda kb, c: step(kb, c, False), carry)
        for b in range(nb):
            m, acc = carry[b]
            l = acc[:, 0:1]
            o_ref[:, _lanes(b)] = (acc / l).astype(BF16)
            lse_ref[:, _lanes(b)] = jnp.broadcast_to(m * (1.0 / LOG2_E) + jnp.log(l), (TQ, HEAD_PAD))

    qspec, kspec = _attn_specs(s, nb)
    return _with_rider(
        body, rider, name=name, grid=(width // (nb * HEAD_PAD), s // TQ),
        in_specs=[qspec, kspec, kspec], out_specs=[qspec, qspec],
        out_shape=[jax.ShapeDtypeStruct((s, width), BF16), jax.ShapeDtypeStruct((s, width), F32)],
        args=(q, k, v), sem=("parallel", "arbitrary"))


def _mla_bwd_call(q, k, v, o, do, lse, name, rider=None):
    s, width = q.shape
    scale = 1.0 / math.sqrt(MLA_QK)
    nb = MLA_BWD_BLOCKS

    def body(q_ref, k_ref, v_ref, o_ref, do_ref, lse_ref, dq_ref, dk_ref, dv_ref):
        qi = pl.program_id(1)

        @pl.when(qi == 0)
        def _():
            dk_ref[...] = jnp.zeros_like(dk_ref)
            dv_ref[...] = jnp.zeros_like(dv_ref)

        qs = [q_ref[:, _lanes(b)] for b in range(nb)]
        dos = [do_ref[:, _lanes(b)] for b in range(nb)]
        lses = [lse_ref[:, b * HEAD_PAD:b * HEAD_PAD + 1] for b in range(nb)]
        dlts = [jnp.sum(dos[b].astype(F32) * o_ref[:, _lanes(b)].astype(F32), axis=-1, keepdims=True) for b in range(nb)]
        causal = _tri(lambda r, c: c <= r)

        def step(kb, dqs, diag):
            ks = pl.multiple_of(kb * TQ, TQ)
            heads = range(nb)
            kts = [k_ref[pl.ds(ks, TQ), _lanes(b)] for b in heads]
            scs = [_dot_nt(qs[b], kts[b]) for b in heads]
            dps = [_dot_nt(dos[b], v_ref[pl.ds(ks, TQ), _lanes(b)]) for b in heads]
            ps = [jnp.exp(scs[b] * scale - lses[b]) for b in heads]
            if diag:
                ps = [jnp.where(causal, p, 0.0) for p in ps]
            dss = [(ps[b] * (dps[b] - dlts[b]) * scale).astype(BF16) for b in heads]
            dvs = [_dot_tn(ps[b].astype(BF16), dos[b]) for b in heads]
            dks = [_dot_tn(dss[b], qs[b]) for b in heads]
            out = tuple(dqs[b] + _dot(dss[b], kts[b]) for b in heads)
            for b in heads:
                dv_ref[pl.ds(ks, TQ), _lanes(b)] += dvs[b]
                dk_ref[pl.ds(ks, TQ), _lanes(b)] += dks[b]
            return out

        dqs = step(qi, tuple(jnp.zeros((TQ, HEAD_PAD), F32) for _ in range(nb)), True)
        dqs = lax.fori_loop(0, qi, lambda kb, c: step(kb, c, False), dqs)
        for b in range(nb):
            dq_ref[:, _lanes(b)] = dqs[b]

    qspec, kspec = _attn_specs(s, nb)
    out = jax.ShapeDtypeStruct((s, width), F32)
    return _with_rider(
        body, rider, name=name, grid=(width // (nb * HEAD_PAD), s // TQ),
        in_specs=[qspec, kspec, kspec, qspec, qspec, qspec], out_specs=[qspec, kspec, kspec],
        out_shape=[out, out, out], args=(q, k, v, o, do, lse), sem=("parallel", "arbitrary"))


def _dot_hilo(x, u):
    hi = x.astype(BF16)
    lo = (x - hi.astype(F32)).astype(BF16)
    return _dot(hi, u) + _dot(lo, u)


def _sb_logs(z):
    ls = jnp.minimum(z, 0.0) - jnp.log(1.0 + jnp.exp(-jnp.abs(z)))
    return ls, ls - z


def _sb_head_q(qb, first, hh):
    keep = first if hh == 0 else jnp.logical_not(first)
    return jnp.where(keep, qb, jnp.zeros_like(qb)) * jnp.asarray(SB_SCALE, qb.dtype)


def _sb_fwd_call(q, k, v, name):
    s, width = q.shape
    nb = SB_FWD_BLOCKS
    chains = [(b, hh) for b in range(nb) for hh in range(HEAD_PAD // SB_HEAD)]

    def body(q_ref, k_ref, v_ref, o_ref):
        qi = pl.program_id(1)
        strict = _tri(lambda r, c: c < r)
        after = _tri(lambda r, c: r > c).astype(BF16)
        first = _lane((1, HEAD_PAD)) < SB_HEAD
        qhs = [_sb_head_q(q_ref[:, _lanes(b)], first, hh) for b, hh in chains]

        def step(kb, carry, diag):
            ks = pl.multiple_of(kb * TQ, TQ)
            ids = range(len(chains))
            zs = [_dot_nt(qhs[ci], k_ref[pl.ds(ks, TQ), _lanes(chains[ci][0])]) for ci in ids]
            logs = [_sb_logs(z) for z in zs]
            lss = [lg[0] for lg in logs]
            l1ms = [jnp.where(strict, lg[1], 0.0) if diag else lg[1] for lg in logs]
            sufs = [_dot_hilo(l1m, after) for l1m in l1ms]
            as_ = [jnp.exp(lss[ci] + sufs[ci] + carry[ci][0]) for ci in ids]
            if diag:
                as_ = [jnp.where(strict, a, 0.0) for a in as_]
            accs = [carry[ci][1] + _dot(as_[ci].astype(BF16), v_ref[pl.ds(ks, TQ), _lanes(chains[ci][0])]) for ci in ids]
            return tuple((carry[ci][0] + jnp.sum(l1ms[ci], axis=-1, keepdims=True), accs[ci]) for ci in ids)

        init = tuple((jnp.zeros((TQ, 1), F32), jnp.zeros((TQ, HEAD_PAD), F32)) for _ in chains)
        carry = _sb_sweep(step, qi, init)
        for b in range(nb):
            o_ref[:, _lanes(b)] = jnp.where(first, carry[2 * b][1], carry[2 * b + 1][1])

    qspec, kspec = _attn_specs(s, nb)
    return pl.pallas_call(
        body, name=name, grid=(width // (nb * HEAD_PAD), s // TQ),
        in_specs=[qspec, kspec, kspec], out_specs=qspec, out_shape=jax.ShapeDtypeStruct((s, width), F32),
        compiler_params=_cp(("parallel", "arbitrary")))(q, k, v)


def _sb_sweep(step, qi, init):
    def live(carry):
        top = carry[0][0]
        for c in carry[1:]:
            top = jnp.maximum(top, c[0])
        return jnp.max(top)

    carry = step(qi, init, True)

    def cond(state):
        j, alive, _ = state
        return jnp.logical_and(j < qi, alive > SB_DEAD)

    def body(state):
        j, _, carry = state
        carry = step(qi - 1 - j, carry, False)
        return j + 1, live(carry), carry

    return lax.while_loop(cond, body, (jnp.int32(0), live(carry), carry))[2]


def _sb_bwd_call(q, k, v, do, o, name):
    s, width = q.shape
    nb = SB_BWD_BLOCKS
    chains = [(b, hh) for b in range(nb) for hh in range(HEAD_PAD // SB_HEAD)]

    def body(q_ref, k_ref, v_ref, do_ref, o_ref, dq_ref, dk_ref, dv_ref):
        qi = pl.program_id(1)

        @pl.when(qi == 0)
        def _():
            dk_ref[...] = jnp.zeros_like(dk_ref)
            dv_ref[...] = jnp.zeros_like(dv_ref)

        strict = _tri(lambda r, c: c < r)
        after = _tri(lambda r, c: r > c).astype(BF16)
        from_here = _tri(lambda r, c: r >= c).astype(BF16)
        first = _lane((1, HEAD_PAD)) < SB_HEAD
        qhs = [_sb_head_q(q_ref[:, _lanes(b)], first, hh) for b, hh in chains]
        dohs = []
        for b, hh in chains:
            dob = do_ref[:, _lanes(b)]
            dohs.append(jnp.where(first if hh == 0 else jnp.logical_not(first), dob, jnp.zeros_like(dob)))
        gtots = [jnp.sum(dohs[ci].astype(F32) * o_ref[:, _lanes(chains[ci][0])], axis=-1, keepdims=True)
                 for ci in range(len(chains))]

        def step(kb, carry, diag):
            ks = pl.multiple_of(kb * TQ, TQ)
            ids = range(len(chains))
            kts = [k_ref[pl.ds(ks, TQ), _lanes(b)] for b, _ in chains]
            zs = [_dot_nt(qhs[ci], kts[ci]) for ci in ids]
            das = [_dot_nt(dohs[ci], v_ref[pl.ds(ks, TQ), _lanes(chains[ci][0])]) for ci in ids]
            logs = [_sb_logs(z) for z in zs]
            lss = [lg[0] for lg in logs]
            l1ms = [jnp.where(strict, lg[1], 0.0) if diag else lg[1] for lg in logs]
            sufs = [_dot_hilo(l1m, after) for l1m in l1ms]
            as_ = [jnp.exp(lss[ci] + sufs[ci] + carry[ci][0]) for ci in ids]
            if diag:
                as_ = [jnp.where(strict, a, 0.0) for a in as_]
            abs_ = [a.astype(BF16) for a in as_]
            gs = [abs_[ci].astype(F32) * das[ci] for ci in ids]
            cexs = [gtots[ci] - (carry[ci][1] + _dot_hilo(gs[ci], from_here)) for ci in ids]
            dzs = [gs[ci] - jnp.exp(lss[ci]) * (gs[ci] + cexs[ci]) for ci in ids]
            if diag:
                dzs = [jnp.where(strict, dz, 0.0) for dz in dzs]
            dzbs = [dz.astype(BF16) for dz in dzs]
            dvps = [_dot_tn(abs_[ci], dohs[ci]) for ci in ids]
            dkps = [_dot_tn(dzbs[ci], qhs[ci]) for ci in ids]
            out = tuple((carry[ci][0] + jnp.sum(l1ms[ci], axis=-1, keepdims=True),
                         carry[ci][1] + jnp.sum(gs[ci], axis=-1, keepdims=True),
                         carry[ci][2] + _dot(dzbs[ci], kts[ci])) for ci in ids)
            for b in range(nb):
                dk_ref[pl.ds(ks, TQ), _lanes(b)] += dkps[2 * b] + dkps[2 * b + 1]
                dv_ref[pl.ds(ks, TQ), _lanes(b)] += dvps[2 * b] + dvps[2 * b + 1]
            return out

        init = tuple((jnp.zeros((TQ, 1), F32), jnp.zeros((TQ, 1), F32), jnp.zeros((TQ, HEAD_PAD), F32)) for _ in chains)
        carry = _sb_sweep(step, qi, init)
        for b in range(nb):
            dq_ref[:, _lanes(b)] = (jnp.where(first, carry[2 * b][2], carry[2 * b + 1][2]) * SB_SCALE).astype(BF16)

    qspec, kspec = _attn_specs(s, nb)
    return pl.pallas_call(
        body, name=name, grid=(width // (nb * HEAD_PAD), s // TQ),
        in_specs=[qspec, kspec, kspec, qspec, qspec], out_specs=[qspec, kspec, kspec],
        out_shape=[jax.ShapeDtypeStruct((s, width), BF16), jax.ShapeDtypeStruct((s, width), F32),
                   jax.ShapeDtypeStruct((s, width), F32)],
        compiler_params=_cp(("parallel", "arbitrary")))(q, k, v, do, o)


def _merge_out_call(om, osb, gates, h, wbm, wbs, wo, gain, name):
    s, d = h.shape
    tm = min(TM_SMALL, s)

    def body(om_ref, os_ref, g_ref, h_ref, wbm_ref, wbs_ref, wo_ref, gain_ref, h2_ref, bm_ref, bs_ref, mg_ref, u_ref):
        bm = _dot(om_ref[...], wbm_ref[...])
        bs = _dot(os_ref[...].astype(BF16), wbs_ref[...])
        mg = (_sigmoid(g_ref[:, :d]) * bm + _sigmoid(g_ref[:, d:]) * bs).astype(BF16)
        bm_ref[...] = bm
        bs_ref[...] = bs
        mg_ref[...] = mg
        x = h_ref[...] + _dot(mg, wo_ref[...])
        h2_ref[...] = x
        u_ref[...] = ((x * _rstd(x, d)) * gain_ref[...]).astype(BF16)

    return pl.pallas_call(
        body, name=name, grid=(s // tm,),
        in_specs=[_rows(tm, om.shape[1]), _rows(tm, SB_WIDTH), _rows(tm, 2 * d), _rows(tm, d),
                  _whole(wbm.shape), _whole(wbs.shape), _whole(wo.shape), _whole((1, d))],
        out_specs=[_rows(tm, d)] * 5,
        out_shape=[jax.ShapeDtypeStruct((s, d), F32), jax.ShapeDtypeStruct((s, d), F32),
                   jax.ShapeDtypeStruct((s, d), F32), jax.ShapeDtypeStruct((s, d), BF16),
                   jax.ShapeDtypeStruct((s, d), BF16)],
        compiler_params=_cp(("parallel",)))(om, osb, gates, h, wbm, wbs, wo, gain)


def _ple_call(h, g, wg, p, wp, tgt, name):
    s, d = h.shape
    tm = min(TM_SMALL, s)

    def body(h_ref, g_ref, wg_ref, p_ref, wp_ref, t_ref, dh_ref, dhs_ref, un_ref, dgl_ref, dpp_ref, dg_ref, sq_ref):
        @pl.when(pl.program_id(0) == 0)
        def _():
            dg_ref[...] = jnp.zeros_like(dg_ref)
            sq_ref[...] = jnp.zeros_like(sq_ref)

        x = h_ref[...]
        gain = g_ref[...]
        r = _rstd(x, d)
        xh = x * r
        un = (xh * gain).astype(BF16)
        sg = _sigmoid(_dot(un, wg_ref[...]))
        pp = _dot(p_ref[...].astype(BF16), wp_ref[...])
        diff = (x + sg * pp) - t_ref[...]
        sq_ref[...] += jnp.sum(diff * diff, axis=0, keepdims=True)
        dy = diff * (1.0 / d)
        dgl = ((dy * pp) * (sg * (1.0 - sg))).astype(BF16)
        dun = _dot_nt(dgl, wg_ref[...])
        dg_ref[...] += jnp.sum(dun * xh, axis=0, keepdims=True)
        dh = dy + _rms_bwd(x, r, gain, dun, d)
        dh_ref[...] = dh
        dhs_ref[...] = (0.5 * dh).astype(BF16)
        un_ref[...] = un
        dgl_ref[...] = dgl
        dpp_ref[...] = (dy * sg).astype(BF16)

    bf = jax.ShapeDtypeStruct((s, d), BF16)
    vec = jax.ShapeDtypeStruct((1, d), F32)
    return pl.pallas_call(
        body, name=name, grid=(s // tm,),
        in_specs=[_rows(tm, d), _whole((1, d)), _whole(wg.shape), _rows(tm, PLE_DIM), _whole(wp.shape), _rows(tm, d)],
        out_specs=[_rows(tm, d)] * 5 + [_whole((1, d))] * 2,
        out_shape=[jax.ShapeDtypeStruct((s, d), F32), bf, bf, bf, bf, vec, vec],
        compiler_params=_cp(("arbitrary",)))(h, g, wg, p, wp, tgt)


def _ffn_bwd_a_call(dhs, a, b, wo, name, rider=None):
    s, n = a.shape
    d = dhs.shape[1]
    tn = n // 2
    tm = min(TM, s)

    def body(dh_ref, a_ref, b_ref, wo_ref, da_ref, db_ref):
        dh = dh_ref[...]
        chunks = [slice(c0, min(c0 + COL_CHUNK, tn)) for c0 in range(0, tn, COL_CHUNK)]
        dhms = [_dot_nt(dh, wo_ref[sl, :]) for sl in chunks]
        for sl, dhm in zip(chunks, dhms):
            av = a_ref[:, sl]
            sa = _sigmoid(av)
            da_ref[:, sl] = (dhm * b_ref[:, sl] * (sa * (1.0 + av * (1.0 - sa)))).astype(BF16)
            db_ref[:, sl] = (dhm * (av * sa)).astype(BF16)

    blk = pl.BlockSpec((tm, tn), lambda j, i: (i, j))
    return _with_rider(
        body, rider, name=name, grid=(n // tn, s // tm),
        in_specs=[pl.BlockSpec((tm, d), lambda j, i: (i, 0)), blk, blk, pl.BlockSpec((tn, d), lambda j, i: (j, 0))],
        out_specs=[blk, blk],
        out_shape=[jax.ShapeDtypeStruct((s, n), BF16)] * 2, args=(dhs, a, b, wo), sem=("parallel", "parallel"))


def _norm_bwd_call(dy_list, w_list, h, g, dh_in, name, half_out, rider=None, w_transposed=False):
    s, d = h.shape
    tm = min(TM_SMALL, s)
    nk, nw = len(dy_list), len(w_list)
    factor = 0.5 if half_out else 1.0
    sharded = nw == 1 and w_list[0].ndim == 3

    def body(*refs):
        dy_refs = refs[:nk]
        w_refs = refs[nk:nk + nw]
        h_ref, g_ref, dhin_ref, dh_ref, dhb_ref, dg_ref = refs[nk + nw:]

        @pl.when(pl.program_id(0) == 0)
        def _():
            dg_ref[...] = jnp.zeros_like(dg_ref)

        if sharded:
            c = w_list[0].shape[2]
            per = dy_list[0].shape[1] // c
            du = None
            for k in range(w_list[0].shape[0]):
                part = _dot_nt(dy_refs[k // per][:, (k % per) * c:(k % per + 1) * c], w_refs[0][k])
                du = part if du is None else du + part
        else:
            mm = _dot if w_transposed else _dot_nt
            du = mm(dy_refs[0][...], w_refs[0][...])
            for dy_ref, w_ref in zip(dy_refs[1:], w_refs[1:]):
                du = du + mm(dy_ref[...], w_ref[...])
        x = h_ref[...]
        r = _rstd(x, d)
        dg_ref[...] += jnp.sum(du * (x * r), axis=0, keepdims=True)
        dh = dhin_ref[...] + _rms_bwd(x, r, g_ref[...], du, d)
        dh_ref[...] = dh
        dhb_ref[...] = (factor * dh).astype(BF16)

    outs, got = _with_rider(
        body, rider, name=name, grid=(s // tm,),
        in_specs=[_rows(tm, dy.shape[1]) for dy in dy_list] + [_whole(w.shape) for w in w_list]
        + [_rows(tm, d), _whole((1, d)), _rows(tm, d)],
        out_specs=[_rows(tm, d), _rows(tm, d), _whole((1, d))],
        out_shape=[jax.ShapeDtypeStruct((s, d), F32), jax.ShapeDtypeStruct((s, d), BF16),
                   jax.ShapeDtypeStruct((1, d), F32)],
        args=(*dy_list, *w_list, h, g, dh_in), sem=("arbitrary",))
    return outs if rider is None else (outs, got)


def _merge_bwd_call(dhb, gates, bm, bs, wo, wbm, wbs, name):
    s, d = bm.shape
    tm = min(TM_SMALL, s)

    def body(dh_ref, g_ref, bm_ref, bs_ref, wo_ref, wbm_ref, wbs_ref, dg_ref, dbm_ref, dbs_ref, dom_ref, dos_ref):
        dmg = _dot_nt(dh_ref[...], wo_ref[...])
        s1 = _sigmoid(g_ref[:, :d])
        s2 = _sigmoid(g_ref[:, d:])
        dg_ref[:, :d] = (dmg * bm_ref[...] * (s1 * (1.0 - s1))).astype(BF16)
        dg_ref[:, d:] = (dmg * bs_ref[...] * (s2 * (1.0 - s2))).astype(BF16)
        dbm = (dmg * s1).astype(BF16)
        dbs = (dmg * s2).astype(BF16)
        dbm_ref[...] = dbm
        dbs_ref[...] = dbs
        dom_ref[...] = _dot_nt(dbm, wbm_ref[...]).astype(BF16)
        dos_ref[...] = _dot_nt(dbs, wbs_ref[...]).astype(BF16)

    wm = wbm.shape[0]
    return pl.pallas_call(
        body, name=name, grid=(s // tm,),
        in_specs=[_rows(tm, d), _rows(tm, 2 * d), _rows(tm, d), _rows(tm, d),
                  _whole(wo.shape), _whole(wbm.shape), _whole(wbs.shape)],
        out_specs=[_rows(tm, 2 * d), _rows(tm, d), _rows(tm, d), _rows(tm, wm), _rows(tm, SB_WIDTH)],
        out_shape=[jax.ShapeDtypeStruct((s, 2 * d), BF16), jax.ShapeDtypeStruct((s, d), BF16),
                   jax.ShapeDtypeStruct((s, d), BF16), jax.ShapeDtypeStruct((s, wm), BF16),
                   jax.ShapeDtypeStruct((s, SB_WIDTH), BF16)],
        compiler_params=_cp(("parallel",)))(dhb, gates, bm, bs, wo, wbm, wbs)


def _mla_prep_bwd_call(cq, ckv, krope, pos, freq, sign, g_ql, g_kvl, g_qh, g_kh, wq, wkv, dq, dk, dv, name):
    s = cq.shape[0]
    tm = min(TM_PREP_BWD, s)
    width = HEADS * HEAD_PAD

    def body(cq_ref, ckv_ref, kr_ref, pos_ref, freq_ref, sign_ref, gql_ref, gkvl_ref, gqh_ref, gkh_ref,
             wq_ref, wkv_ref, dq_ref, dk_ref, dv_ref,
             dcq_ref, dckv_ref, dkr_ref, dwq_ref, dwkv_ref, dgql_ref, dgkvl_ref, dgqh_ref, dgkh_ref, dqr_ref, dkv_ref):
        @pl.when(pl.program_id(0) == 0)
        def _():
            for ref in (dwq_ref, dwkv_ref, dgql_ref, dgkvl_ref, dgqh_ref, dgkh_ref):
                ref[...] = jnp.zeros_like(ref)

        cosv, ssv = _rope_tables(pos_ref, freq_ref, sign_ref)
        xq = cq_ref[...]
        rq = _rstd(xq, Q_LORA)
        cqn = ((xq * rq) * gql_ref[...]).astype(BF16)
        qr = _dot(cqn, wq_ref[...])
        xk = ckv_ref[...]
        rk = _rstd(xk, KV_LORA)
        ckvn = ((xk * rk) * gkvl_ref[...]).astype(BF16)
        kv = _dot(ckvn, wkv_ref[...])
        kr = kr_ref[...]
        lane = _lane((tm, HEAD_PAD))
        dkr = jnp.zeros((tm, HEAD_PAD), F32)
        dgqh = jnp.zeros((1, HEAD_PAD), F32)
        dgkh = jnp.zeros((1, HEAD_PAD), F32)
        for h in range(HEADS):
            sl = slice(h * HEAD_PAD, (h + 1) * HEAD_PAD)
            x = qr[:, sl]
            dx, dgh = _head_bwd(x, _rstd(x, MLA_QK), gqh_ref[...], cosv, ssv, dq_ref[:, sl])
            dqr_ref[:, sl] = dx.astype(BF16)
            dgqh = dgqh + dgh
            x = jnp.where(lane < MLA_NOPE, kv[:, sl], kr)
            dx, dgh = _head_bwd(x, _rstd(x, MLA_QK), gkh_ref[...], cosv, ssv, dk_ref[:, sl])
            dgkh = dgkh + dgh
            dkr = dkr + jnp.where(lane >= MLA_NOPE, dx, 0.0)
            dkv_ref[:, sl] = jnp.where(lane < MLA_NOPE, dx, dv_ref[:, sl]).astype(BF16)
        dgqh_ref[...] += dgqh
        dgkh_ref[...] += dgkh
        dkr_ref[...] = dkr.astype(BF16)
        dqr = dqr_ref[...]
        dkvb = dkv_ref[...]
        dwq_ref[...] += _dot_tn(cqn, dqr)
        dwkv_ref[...] += _dot_tn(ckvn, dkvb)
        dcqn = _dot_nt(dqr, wq_ref[...])
        dgql_ref[...] += jnp.sum(dcqn * (xq * rq), axis=0, keepdims=True)
        dcq_ref[...] = _rms_bwd(xq, rq, gql_ref[...], dcqn, Q_LORA).astype(BF16)
        dckvn = _dot_nt(dkvb, wkv_ref[...])
        dgkvl_ref[...] += jnp.sum(dckvn * (xk * rk), axis=0, keepdims=True)
        dckv_ref[...] = _rms_bwd(xk, rk, gkvl_ref[...], dckvn, KV_LORA).astype(BF16)

    vec = lambda n: jax.ShapeDtypeStruct((1, n), F32)
    outs = pl.pallas_call(
        body, name=name, grid=(s // tm,),
        in_specs=[_rows(tm, Q_LORA), _rows(tm, KV_LORA), _rows(tm, HEAD_PAD), _rows(tm, 1),
                  _whole((1, HEAD_PAD)), _whole((1, HEAD_PAD)), _whole((1, Q_LORA)), _whole((1, KV_LORA)),
                  _whole((1, HEAD_PAD)), _whole((1, HEAD_PAD)), _whole((Q_LORA, width)), _whole((KV_LORA, width)),
                  _rows(tm, width), _rows(tm, width), _rows(tm, width)],
        out_specs=[_rows(tm, Q_LORA), _rows(tm, KV_LORA), _rows(tm, HEAD_PAD), _whole((Q_LORA, width)),
                   _whole((KV_LORA, width)), _whole((1, Q_LORA)), _whole((1, KV_LORA)), _whole((1, HEAD_PAD)),
                   _whole((1, HEAD_PAD)), _rows(tm, width), _rows(tm, width)],
        out_shape=[jax.ShapeDtypeStruct((s, Q_LORA), BF16), jax.ShapeDtypeStruct((s, KV_LORA), BF16),
                   jax.ShapeDtypeStruct((s, HEAD_PAD), BF16), jax.ShapeDtypeStruct((Q_LORA, width), F32),
                   jax.ShapeDtypeStruct((KV_LORA, width), F32), vec(Q_LORA), vec(KV_LORA), vec(HEAD_PAD), vec(HEAD_PAD),
                   jax.ShapeDtypeStruct((s, width), BF16), jax.ShapeDtypeStruct((s, width), BF16)],
        compiler_params=_cp(("arbitrary",)))(cq, ckv, krope, pos, freq, sign, g_ql, g_kvl, g_qh, g_kh, wq, wkv, dq, dk, dv)
    return outs[:9]


def _tn_call(a, b, name, shard_cols=None, rider=None):
    s, ka = a.shape
    nb = b.shape[1]
    ti = _pick(ka, (512, 256, 128))
    if shard_cols is not None:
        tj = shard_cols
    else:
        tj = nb if nb <= TN_MAX_COLS else _pick(nb, (2176, 1024, 512, 256, 128))
    ts = s if 2 * s * (ti + tj) * a.dtype.itemsize <= TN_OPERAND_BYTES else s // 2
    ns = s // ts

    def body(a_ref, b_ref, o_ref, acc_ref):
        part = _dot_tn(a_ref[...].astype(BF16), b_ref[...].astype(BF16))
        if ns == 1:
            o_ref[...] = part.astype(o_ref.dtype)
            return

        @pl.when(pl.program_id(2) == 0)
        def _():
            acc_ref[...] = part

        @pl.when(pl.program_id(2) != 0)
        def _():
            acc_ref[...] += part

        @pl.when(pl.program_id(2) == ns - 1)
        def _():
            o_ref[...] = acc_ref[...].astype(o_ref.dtype)

    if shard_cols is None:
        out_spec = pl.BlockSpec((ti, tj), lambda i, j, t: (i, j))
        out_shape = jax.ShapeDtypeStruct((ka, nb), BF16)
    else:
        out_spec = pl.BlockSpec((None, ti, tj), lambda i, j, t: (j, i, 0))
        out_shape = jax.ShapeDtypeStruct((nb // tj, ka, tj), BF16)
    (out,), got = _with_rider(
        body, rider, name=name, grid=(ka // ti, nb // tj, ns),
        in_specs=[pl.BlockSpec((ts, ti), lambda i, j, t: (t, i)), pl.BlockSpec((ts, tj), lambda i, j, t: (t, j))],
        out_specs=[out_spec], out_shape=[out_shape], scratch=[pltpu.VMEM((ti, tj), F32)], args=(a, b),
        sem=("parallel", "parallel", "arbitrary"))
    return out if rider is None else (out, got)


def _sum_call(parts, out_dtype, name):
    n, r, w = parts.shape
    tr = _row_tile(r)

    def body(p_ref, o_ref):
        acc = p_ref[0].astype(F32)
        for k in range(1, n):
            acc = acc + p_ref[k].astype(F32)
        o_ref[...] = acc.astype(out_dtype)

    return pl.pallas_call(
        body, name=name, grid=(r // tr,),
        in_specs=[pl.BlockSpec((n, tr, w), lambda i: (0, i, 0))], out_specs=_rows(tr, w),
        out_shape=jax.ShapeDtypeStruct((r, w), out_dtype), compiler_params=_cp(("parallel",)))(parts)


def _chip_sum_call(by_chip, core, name):
    n, r, w = by_chip.shape
    tr = _row_tile(r)
    nblk = r // tr

    def body(c_ref, p_ref, o_ref):
        acc = p_ref[0].astype(F32)
        for k in range(1, n):
            acc = acc + p_ref[k].astype(F32)
        o_ref[...] = acc

    return pl.pallas_call(
        body, name=name,
        grid_spec=pltpu.PrefetchScalarGridSpec(
            num_scalar_prefetch=1, grid=(nblk,),
            in_specs=[pl.BlockSpec((n, tr, w), lambda i, c_ref: (0, i, 0))],
            out_specs=pl.BlockSpec((tr, w), lambda i, c_ref: (c_ref[0] * nblk + i, 0))),
        out_shape=jax.ShapeDtypeStruct((2 * r, w), F32),
        compiler_params=_cp(("parallel",)))(core.reshape(1).astype(jnp.int32), by_chip)


def _pair_sum_call(full, other, core, out_dtype, name):
    n, r, w = other.shape
    tr = _row_tile(r)
    nblk = r // tr

    def body(c_ref, a_ref, b_ref, o_ref):
        o_ref[...] = (a_ref[...].astype(F32) + b_ref[...].astype(F32)).astype(out_dtype)

    spec = pl.BlockSpec((None, tr, w), lambda k, i, c_ref: (k, i, 0))
    return pl.pallas_call(
        body, name=name,
        grid_spec=pltpu.PrefetchScalarGridSpec(
            num_scalar_prefetch=1, grid=(n, nblk),
            in_specs=[pl.BlockSpec((None, tr, w), lambda k, i, c_ref: (k, c_ref[0] * nblk + i, 0)), spec],
            out_specs=spec),
        out_shape=jax.ShapeDtypeStruct((n, r, w), out_dtype),
        compiler_params=_cp(("parallel", "parallel")))(core.reshape(1).astype(jnp.int32), full, other)


def _adamw_call(w, g, row0, m, v, name):
    r, c = w.shape
    span = math.gcd(r, row0) if row0 else r
    tr = next((t for t in range(min(span, 256) // 8 * 8, 0, -8) if span % t == 0), span)
    off = row0 // tr

    def body(w_ref, g_ref, m_ref, v_ref, g_out_ref, d_ref, nm_ref, nv_ref):
        gg = g_ref[...]
        g_out_ref[...] = gg
        nm = ADAM_B1 * m_ref[...] + (1.0 - ADAM_B1) * gg
        nv = ADAM_B2 * v_ref[...] + (1.0 - ADAM_B2) * (gg * gg)
        m_hat = nm / (1.0 - ADAM_B1 ** ADAM_STEP)
        v_hat = nv / (1.0 - ADAM_B2 ** ADAM_STEP)
        d_ref[...] = -ADAM_LR * (m_hat / (jnp.sqrt(v_hat) + ADAM_EPS) + ADAM_WD * w_ref[...])
        nm_ref[...] = nm
        nv_ref[...] = nv

    out = jax.ShapeDtypeStruct((r, c), F32)
    g_spec = pl.BlockSpec((tr, c), lambda i: (off + i, 0))
    return pl.pallas_call(
        body, name=name, grid=(r // tr,), in_specs=[_rows(tr, c), g_spec, _rows(tr, c), _rows(tr, c)],
        out_specs=[_rows(tr, c)] * 4, out_shape=[out, out, out, out], compiler_params=_cp(("parallel",)))(w, g, m, v)


def _position():
    x, y, c = lax.axis_index("x"), lax.axis_index("y"), lax.axis_index("c")
    chips = [(1 - x, y), (x, 1 - y), (1 - x, 1 - y)]
    return x, y, c, chips


def _gather_rider(parts):
    n = len(parts)
    pairs = [(j, k) for j in range(3) for k in range(n)]

    def piece(out_refs, k, chip, core):
        half = parts[k].shape[0] // 2
        return out_refs[k].at[2 * chip[0] + chip[1], pl.ds(core * half, half), :]

    def over_ici(in_refs, out_refs, sems, j, k):
        x, y, c, chips = _position()
        half = parts[k].shape[0] // 2
        return pltpu.make_async_remote_copy(
            src_ref=in_refs[k].at[pl.ds(c * half, half), :], dst_ref=piece(out_refs, k, (x, y), c),
            send_sem=sems[0].at[n * j + k], recv_sem=sems[1].at[n * j + k], device_id=(*chips[j], c), device_id_type=MESH)

    def to_sibling(out_refs, sems, j, k):
        x, y, c, chips = _position()
        landed = piece(out_refs, k, chips[j], c)
        return pltpu.make_async_remote_copy(
            src_ref=landed, dst_ref=landed, send_sem=sems[2].at[n * j + k], recv_sem=sems[3].at[n * j + k],
            device_id=(x, y, 1 - c), device_id_type=MESH)

    def start(in_refs, out_refs, sems):
        for j, k in pairs:
            over_ici(in_refs, out_refs, sems, j, k).start()

    def relay(in_refs, out_refs, sems):
        for j, k in pairs:
            over_ici(in_refs, out_refs, sems, j, k).wait_recv()
            to_sibling(out_refs, sems, j, k).start()

    def finish(in_refs, out_refs, sems):
        for j, k in pairs:
            to_sibling(out_refs, sems, j, k).wait_recv()
        for j, k in pairs:
            over_ici(in_refs, out_refs, sems, j, k).wait_send()
            to_sibling(out_refs, sems, j, k).wait_send()

    return _Rider(list(parts), [jax.ShapeDtypeStruct((N_CHIPS,) + p.shape, p.dtype) for p in parts], [3 * n] * 4,
                  start, relay, finish)


def _scatter_rider(parts):
    n = len(parts)
    pairs = [(j, k) for j in range(3) for k in range(n)]

    def copy(in_refs, out_refs, sems, j, k):
        x, y, c, chips = _position()
        return pltpu.make_async_remote_copy(
            src_ref=in_refs[k].at[2 * chips[j][0] + chips[j][1]], dst_ref=out_refs[k].at[2 * x + y],
            send_sem=sems[0].at[n * j + k], recv_sem=sems[1].at[n * j + k], device_id=(*chips[j], c), device_id_type=MESH)

    def start(in_refs, out_refs, sems):
        for j, k in pairs:
            copy(in_refs, out_refs, sems, j, k).start()

    def finish(in_refs, out_refs, sems):
        for j, k in pairs:
            copy(in_refs, out_refs, sems, j, k).wait()

    return _Rider(list(parts), [jax.ShapeDtypeStruct(p.shape, p.dtype) for p in parts], [3 * n] * 2, start,
                  lambda in_refs, out_refs, sems: None, finish)


def _pair_send_call(parts, name):
    n = len(parts)

    def body(*refs):
        in_refs, out_refs = refs[:n], refs[n:2 * n]
        send_sems, recv_sems = refs[2 * n:]
        x, y, c, _ = _position()
        copies = []
        for k in range(n):
            half = parts[k].shape[1] // 2
            cp = pltpu.make_async_remote_copy(
                src_ref=in_refs[k].at[:, pl.ds((1 - c) * half, half), :], dst_ref=out_refs[k],
                send_sem=send_sems.at[k], recv_sem=recv_sems.at[k], device_id=(x, y, 1 - c), device_id_type=MESH)
            cp.start()
            copies.append(cp)
        for cp in copies:
            cp.wait()

    sems = pltpu.SemaphoreType.DMA((n,))
    return pl.pallas_call(
        body, name=name, in_specs=[HBM] * n, out_specs=[HBM] * n,
        out_shape=[jax.ShapeDtypeStruct((p.shape[0], p.shape[1] // 2, p.shape[2]), p.dtype) for p in parts],
        scratch_shapes=[sems, sems])(*parts)


def _pair_swap_call(parts, name):
    n = len(parts)

    def body(*refs):
        out_refs = refs[n:2 * n]
        send_sems, recv_sems = refs[2 * n:]
        x, y, c, _ = _position()
        copies = []
        for k in range(n):
            half = parts[k].shape[0] // 2
            mine = out_refs[k].at[pl.ds(c * half, half), :]
            cp = pltpu.make_async_remote_copy(
                src_ref=mine, dst_ref=mine, send_sem=send_sems.at[k], recv_sem=recv_sems.at[k],
                device_id=(x, y, 1 - c), device_id_type=MESH)
            cp.start()
            copies.append(cp)
        for cp in copies:
            cp.wait()

    sems = pltpu.SemaphoreType.DMA((n,))
    return pl.pallas_call(
        body, name=name, in_specs=[HBM] * n, out_specs=[HBM] * n,
        out_shape=[jax.ShapeDtypeStruct(p.shape, p.dtype) for p in parts],
        input_output_aliases={k: k for k in range(n)},
        scratch_shapes=[sems, sems])(*parts)


def _all_gather_small_call(block, name):
    r, w = block.shape

    def body(in_ref, out_ref, send_sems, recv_sems, local_sem):
        x, y, c, _ = _position()
        me = 4 * x + 2 * y + c
        own = pltpu.make_async_copy(in_ref, out_ref.at[me], local_sem)
        own.start()
        copies = []
        for k in range(1, 8):
            peer = (x ^ (k >> 2), y ^ ((k >> 1) & 1), c ^ (k & 1))
            cp = pltpu.make_async_remote_copy(
                src_ref=in_ref, dst_ref=out_ref.at[me], send_sem=send_sems.at[k - 1], recv_sem=recv_sems.at[k - 1],
                device_id=peer, device_id_type=MESH)
            cp.start()
            copies.append(cp)
        for cp in copies:
            cp.wait()
        own.wait()

    return pl.pallas_call(
        body, name=name, in_specs=[HBM], out_specs=HBM,
        out_shape=jax.ShapeDtypeStruct((8, r, w), block.dtype),
        scratch_shapes=[pltpu.SemaphoreType.DMA((7,)), pltpu.SemaphoreType.DMA((7,)), pltpu.SemaphoreType.DMA])(block)


BIG = {
    "ffn1_w_in": ((D_MODEL, 2 * D_FF), 1), "ffn1_w_out": ((D_FF, D_MODEL), 0),
    "w_in": ((D_MODEL, 4256), 1), "w_q_up": ((Q_LORA, HEADS * MLA_QK), 1), "w_kv_up": ((KV_LORA, 1024), 1),
    "w_branch_mla": ((512, D_MODEL), 1), "w_branch_sb": ((SB_WIDTH, D_MODEL), 1), "w_out": ((D_MODEL, D_MODEL), 0),
    "ffn2_w_in": ((D_MODEL, 2 * D_FF), 1), "ffn2_w_out": ((D_FF, D_MODEL), 0),
    "w_ple_gate": ((D_MODEL, D_MODEL), 0), "w_ple_proj": ((PLE_DIM, D_MODEL), 1),
}
GAINS = {"ffn1_norm": 1024, "mix_norm": 1024, "q_latent_norm": 384, "kv_latent_norm": 256, "q_head_norm": 96,
         "k_head_norm": 96, "ffn2_norm": 1024, "ple_norm": 1024}
WEIGHT_ORDER = ["ffn1_norm", "ffn1_w_in", "ffn1_w_out", "mix_norm", "w_in", "q_latent_norm", "w_q_up",
                "kv_latent_norm", "w_kv_up", "q_head_norm", "k_head_norm", "w_branch_mla", "w_branch_sb", "w_out",
                "ffn2_norm", "ffn2_w_in", "ffn2_w_out", "ple_norm", "w_ple_gate", "w_ple_proj"]


W_IN_SHARD_ROWS = 1088


def _shard_shape(name):
    (r, c), axis = BIG[name]
    if name in TRANSPOSED_UPDATE:
        return (W_IN_SHARD_ROWS, r)
    return (r // N_CHIPS, c) if axis == 0 else (r, c // N_CHIPS)


GATHER_GROUPS = [
    [("ffn1_w_in",)],
    [("ffn1_w_out",), ("w_in",)],
    [("w_out",), ("w_kv_up", "w_branch_mla", "w_branch_sb"), ("w_q_up",)],
    [("ffn2_w_in",), ("ffn2_w_out", "w_ple_gate"), ("w_ple_proj",)],
]
REDUCE_GROUPS = [
    [("ffn2_w_in",), ("ffn2_w_out", "w_out", "w_ple_gate"), ("w_branch_mla", "w_branch_sb", "w_ple_proj")],
    [("w_in",), ("w_kv_up",), ("w_q_up",)],
    [("ffn1_w_out",)],
    [("ffn1_w_in",)],
]


def _join_parts(shards, group):
    return [shards[part[0]] if len(part) == 1 else jnp.concatenate([shards[n] for n in part], axis=-2) for part in group]


def _part_rows(group):
    where = {}
    for k, part in enumerate(group):
        at = 0
        for n in part:
            where[n] = (k, at)
            at += _shard_shape(n)[0]
    return where


def _split_parts(parts, group):
    return {n: parts[k][..., at:at + _shard_shape(n)[0], :] for n, (k, at) in _part_rows(group).items()}


def _exchange_form(name, shard):
    if name in TRANSPOSED_UPDATE:
        t = shard.T.astype(BF16)
        return jnp.pad(t, ((0, W_IN_SHARD_ROWS - t.shape[0]), (0, 0)))
    return shard.astype(BF16)


def _to_shards(name, full):
    (r, c), axis = BIG[name]
    if axis == 0:
        return full.reshape(N_CHIPS, r // N_CHIPS, c)
    return full.reshape(r, N_CHIPS, c // N_CHIPS).transpose(1, 0, 2)


def _from_shards(name, shards):
    (r, c), axis = BIG[name]
    if axis == 0:
        return shards.reshape(r, c)
    return shards.transpose(1, 0, 2).reshape(r, c)


def _relayout_w_in(wt):
    d = wt.shape[1]
    z = lambda n: jnp.zeros((n, d), wt.dtype)
    return jnp.concatenate([wt[:640], z(MLA_NOPE), wt[640:672], z(HEAD_PAD - MLA_QK), wt[672:]], axis=0)


def _unlayout_w_in(gt):
    full = jnp.concatenate([gt[:640], gt[640 + MLA_NOPE:640 + MLA_QK], gt[768:]], axis=0)
    shards = full.reshape(N_CHIPS, -1, gt.shape[1])
    return jnp.pad(shards, ((0, 0), (0, W_IN_SHARD_ROWS - shards.shape[1]), (0, 0)))


def _pad_heads(v):
    lead = v.shape[:-1]
    return jnp.pad(v.reshape(lead + (HEADS, MLA_QK)), [(0, 0)] * len(lead) + [(0, 0), (0, HEAD_PAD - MLA_QK)]).reshape(
        lead + (HEADS * HEAD_PAD,))


SHARD_MAJOR = ("ffn1_w_in", "ffn2_w_in")
TRANSPOSED_UPDATE = ("w_in",)


def _step(x, p, pos, tgt, gains, weights, dist):
    d = D_MODEL
    full = {} if dist is not None else {
        n: _to_shards(n, w) if n in SHARD_MAJOR else (w.T if n in TRANSPOSED_UPDATE else w) for n, w in weights.items()}
    reduced = {}

    def gather_rider(g):
        if dist is None:
            return None, None
        mine = _join_parts(weights, GATHER_GROUPS[g])
        return mine, _gather_rider(mine)

    def gathered(g, mine, others):
        if dist is not None:
            parts = [lax.dynamic_update_slice_in_dim(o, m[None], dist[0], axis=0) for o, m in zip(others, mine)]
            for n, shards in _split_parts(parts, GATHER_GROUPS[g]).items():
                if n in TRANSPOSED_UPDATE:
                    (d_in, c_out), _ = BIG[n]
                    full[n] = shards[:, :c_out // N_CHIPS].reshape(c_out, d_in)
                else:
                    full[n] = shards if n in SHARD_MAJOR else _from_shards(n, shards)

    def reduce_before(g):
        if dist is None:
            return None, None
        group = REDUCE_GROUPS[g]
        shards = {n: grads[n] if grads[n].ndim == 3 else _to_shards(n, grads[n].astype(BF16)) for part in group for n in part}
        partial = _join_parts(shards, group)
        from_sibling = _pair_send_call(partial, "grads%d_pair_send" % g)
        pair_sum = [_pair_sum_call(a, b, dist[1], BF16, "grads%d_pair_sum_%d" % (g, k))
                    for k, (a, b) in enumerate(zip(partial, from_sibling))]
        return pair_sum, _scatter_rider(pair_sum)

    def reduce_after(g, pair_sum, by_chip):
        if dist is not None:
            chip, core = dist
            by_chip = [lax.dynamic_update_slice_in_dim(t, lax.dynamic_slice_in_dim(o, chip, 1, axis=0), chip, axis=0)
                       for t, o in zip(by_chip, pair_sum)]
            bufs = _pair_swap_call([_chip_sum_call(t, core, "grads%d_chip_sum_%d" % (g, k)) for k, t in enumerate(by_chip)],
                                   "grads%d_pair_swap" % g)
            for n, (k, row0) in _part_rows(REDUCE_GROUPS[g]).items():
                reduced[n] = (bufs[k], row0)

    mine, rider = gather_rider(0)
    u1, got = _norm_call(x, gains["ffn1_norm"], "norm_ffn1", rider)
    gathered(0, mine, got)
    wts = full
    inv_freq = ROPE_BASE ** (-jnp.arange(0, MLA_ROPE, 2, dtype=F32) / MLA_ROPE)
    zeros = lambda n: jnp.zeros((n,), F32)
    freq = jnp.concatenate([zeros(MLA_NOPE), inv_freq, inv_freq, zeros(HEAD_PAD - MLA_QK)])[None]
    sign = jnp.concatenate([zeros(MLA_NOPE), -jnp.ones((16,), F32), jnp.ones((16,), F32), zeros(HEAD_PAD - MLA_QK)])[None]
    pad_gain = lambda g: jnp.pad(g, ((0, 0), (0, HEAD_PAD - MLA_QK)))
    g_qh, g_kh = pad_gain(gains["q_head_norm"]), pad_gain(gains["k_head_norm"])

    mine, rider = gather_rider(1)
    (a1, b1, hm1), got = _ffn_in_call(u1, wts["ffn1_w_in"], "ffn1_in", rider)
    gathered(1, mine, got)
    mine, rider = gather_rider(2)
    (h1, um), got = _ffn_out_call(hm1, wts["ffn1_w_out"], x, gains["mix_norm"], "ffn1_out", rider)
    gathered(2, mine, got)
    w_in = _relayout_w_in(wts["w_in"])
    (cq, ckv, krope, sbq, sbk, sbv, gates), _ = _mix_in_call(um, w_in, "mix_in")
    wq = _pad_heads(wts["w_q_up"])
    wkv = wts["w_kv_up"]
    wbm = jnp.pad(wts["w_branch_mla"].reshape(HEADS, 64, d), ((0, 0), (64, 0), (0, 0))).reshape(HEADS * HEAD_PAD, d)
    wbs, wo = wts["w_branch_sb"], wts["w_out"]
    prep_args = (cq, ckv, krope, pos, freq, sign, gains["q_latent_norm"], gains["kv_latent_norm"], g_qh, g_kh, wq, wkv)
    q, k, v = _mla_prep_call(*prep_args, "mla_prep")
    mine, rider = gather_rider(3)
    (om, lse), got = _mla_fwd_call(q, k, v, "mla_fwd", rider)
    gathered(3, mine, got)
    osb = _sb_fwd_call(sbq, sbk, sbv, "sb_fwd")
    h2, bm, bs, mg, u2 = _merge_out_call(om, osb, gates, h1, wbm, wbs, wo, gains["ffn2_norm"], "merge_out")
    (a2, b2, hm2), _ = _ffn_in_call(u2, wts["ffn2_w_in"], "ffn2_in")
    (h3, _), _ = _ffn_out_call(hm2, wts["ffn2_w_out"], h2, gains["ple_norm"], "ffn2_out")

    grads, gg = {}, {}
    dh3, dh3s, un, dgl, dpp, gg["ple_norm"], sq = _ple_call(
        h3, gains["ple_norm"], wts["w_ple_gate"], p, wts["w_ple_proj"], tgt, "ple")
    grads["w_ple_gate"] = _tn_call(un, dgl, "dw_ple_gate")
    grads["w_ple_proj"] = _tn_call(p, dpp, "dw_ple_proj")

    (da2, db2), _ = _ffn_bwd_a_call(dh3s, a2, b2, wts["ffn2_w_out"], "ffn2_bwd_act")
    grads["ffn2_w_out"] = _tn_call(hm2, dh3s, "dw_ffn2_out")
    grads["ffn2_w_in"] = jnp.concatenate([_tn_call(u2, da2, "dw_ffn2_in_a", shard_cols=D_FF // 2),
                                          _tn_call(u2, db2, "dw_ffn2_in_b", shard_cols=D_FF // 2)], axis=0)
    dh2, dh2b, gg["ffn2_norm"] = _norm_bwd_call([da2, db2], [wts["ffn2_w_in"]], h2, gains["ffn2_norm"], dh3,
                                                "ffn2_bwd_norm", half_out=False)

    dgates, dbm, dbs, dom, dos = _merge_bwd_call(dh2b, gates, bm, bs, wo, wbm, wbs, "merge_bwd")
    grads["w_out"] = _tn_call(mg, dh2b, "dw_out")
    grads["w_branch_mla"] = _tn_call(om, dbm, "dw_branch_mla").reshape(HEADS, HEAD_PAD, d)[:, 64:, :].reshape(512, d)
    grads["w_branch_sb"] = _tn_call(osb, dbs, "dw_branch_sb")
    pair_sum, rider = reduce_before(0)
    (dq, dk, dv), got = _mla_bwd_call(q, k, v, om, dom, lse, "mla_bwd", rider)
    reduce_after(0, pair_sum, got)
    dsq, dsk, dsv = _sb_bwd_call(sbq, sbk, sbv, dos, osb, "sb_bwd")
    (dcq, dckv, dkr, dwq, grads["w_kv_up"], gg["q_latent_norm"], gg["kv_latent_norm"], dgqh, dgkh) = \
        _mla_prep_bwd_call(*prep_args, dq, dk, dv, "mla_prep_bwd")
    grads["w_q_up"] = dwq.reshape(Q_LORA, HEADS, HEAD_PAD)[:, :, :MLA_QK].reshape(Q_LORA, HEADS * MLA_QK)
    gg["q_head_norm"], gg["k_head_norm"] = dgqh[:, :MLA_QK], dgkh[:, :MLA_QK]
    dproj = jnp.concatenate([dcq, dckv, dkr, dsq, dsk.astype(BF16), dsv.astype(BF16), dgates], axis=1)
    grads["w_in"] = _unlayout_w_in(_tn_call(dproj, um, "dw_in"))
    dh1, dh1s, gg["mix_norm"] = _norm_bwd_call([dproj], [w_in], h1, gains["mix_norm"], dh2, "mix_bwd_norm", half_out=True,
                                               w_transposed=True)

    pair_sum, rider = reduce_before(1)
    (da1, db1), got = _ffn_bwd_a_call(dh1s, a1, b1, wts["ffn1_w_out"], "ffn1_bwd_act", rider)
    reduce_after(1, pair_sum, got)
    grads["ffn1_w_out"] = _tn_call(hm1, dh1s, "dw_ffn1_out")
    pair_sum, rider = reduce_before(2)
    res = _tn_call(u1, da1, "dw_ffn1_in_a", shard_cols=D_FF // 2, rider=rider)
    dwa, got = (res, None) if rider is None else res
    reduce_after(2, pair_sum, got)
    grads["ffn1_w_in"] = jnp.concatenate([dwa, _tn_call(u1, db1, "dw_ffn1_in_b", shard_cols=D_FF // 2)], axis=0)
    pair_sum, rider = reduce_before(3)
    res = _norm_bwd_call([da1, db1], [wts["ffn1_w_in"]], x, gains["ffn1_norm"], dh1, "ffn1_bwd_norm",
                         half_out=False, rider=rider)
    (dx, _, gg["ffn1_norm"]), got = (res, None) if rider is None else res
    reduce_after(3, pair_sum, got)
    return sq, dx, gg, (grads if dist is None else reduced)


def kernel(x, p, positions, ffn1_norm, ffn1_w_in, ffn1_w_out, mix_norm, w_in, q_latent_norm, w_q_up, kv_latent_norm, w_kv_up, q_head_norm, k_head_norm, w_branch_mla, w_branch_sb, w_out, ffn2_norm, ffn2_w_in, ffn2_w_out, ple_norm, w_ple_gate, w_ple_proj, loss_target, m_ffn1_norm, m_ffn1_w_in, m_ffn1_w_out, m_mix_norm, m_w_in, m_q_latent_norm, m_w_q_up, m_kv_latent_norm, m_w_kv_up, m_q_head_norm, m_k_head_norm, m_w_branch_mla, m_w_branch_sb, m_w_out, m_ffn2_norm, m_ffn2_w_in, m_ffn2_w_out, m_ple_norm, m_w_ple_gate, m_w_ple_proj, v_ffn1_norm, v_ffn1_w_in, v_ffn1_w_out, v_mix_norm, v_w_in, v_q_latent_norm, v_w_q_up, v_kv_latent_norm, v_w_kv_up, v_q_head_norm, v_k_head_norm, v_w_branch_mla, v_w_branch_sb, v_w_out, v_ffn2_norm, v_ffn2_w_in, v_ffn2_w_out, v_ple_norm, v_w_ple_gate, v_w_ple_proj):
    given = dict(locals())
    w_shard = {n: given[n][0] for n in WEIGHT_ORDER}
    m_shard = {n: given["m_" + n][0] for n in WEIGHT_ORDER}
    v_shard = {n: given["v_" + n][0] for n in WEIGHT_ORDER}
    gains = {n: w_shard[n][None] for n in GAINS}

    chip = 2 * lax.axis_index("x") + lax.axis_index("y")
    sq, dx, gain_grads, reduced = _step(x[0], p[0, 0], positions.reshape(-1, 1), loss_target[0], gains,
                                        {n: _exchange_form(n, w_shard[n]) for n in BIG}, (chip, lax.axis_index("c")))

    rows = [jnp.pad(gain_grads[n], ((0, 0), (0, D_MODEL - GAINS[n]))) for n in GAINS] + [sq]
    gain_block = jnp.concatenate(rows + [jnp.zeros((16 - len(rows), D_MODEL), F32)], axis=0)
    gain_sum = _sum_call(_all_gather_small_call(gain_block, "gains_all_gather"), F32, "gains_sum")
    loss = 0.5 * jnp.sum(gain_sum[len(GAINS)]) / D_MODEL

    outs = {"grad": {}, "delta": {}, "new_m": {}, "new_v": {}}
    gain_pack = lambda t: jnp.concatenate([jnp.pad(t[n][None], ((0, 0), (0, D_MODEL - GAINS[n]))) for n in GAINS], axis=0)
    packed = _adamw_call(gain_pack(w_shard), gain_sum, 0, gain_pack(m_shard), gain_pack(v_shard), "adamw_gains")
    for i, n in enumerate(GAINS):
        for kind, t in zip(("grad", "delta", "new_m", "new_v"), packed):
            outs[kind][n] = t[i, :GAINS[n]][None]
    for n in BIG:
        buf, row0 = reduced[n]
        if n in TRANSPOSED_UPDATE:
            res = [t.T for t in _adamw_call(w_shard[n].T, buf, row0, m_shard[n].T, v_shard[n].T, "adamw_" + n)]
        else:
            res = _adamw_call(w_shard[n], buf, row0, m_shard[n], v_shard[n], "adamw_" + n)
        for kind, t in zip(("grad", "delta", "new_m", "new_v"), res):
            outs[kind][n] = t[None]

    return (loss, dx[None], *[outs["grad"][n] for n in WEIGHT_ORDER], *[outs["delta"][n] for n in WEIGHT_ORDER],
            *[outs["new_m"][n] for n in WEIGHT_ORDER], *[outs["new_v"][n] for n in WEIGHT_ORDER])
```

```python
import collections
import functools
import math

import jax
import jax.numpy as jnp
from jax import lax
from jax.experimental import pallas as pl
from jax.experimental.pallas import tpu as pltpu

F32 = jnp.float32
BF16 = jnp.bfloat16
MESH = pl.DeviceIdType.MESH

D_MODEL = 1024
D_FF = 2816
PLE_DIM = 256
NORM_EPS = 1e-6
HEADS = 8
MLA_NOPE = 64
MLA_ROPE = 32
MLA_QK = 96
Q_LORA = 384
KV_LORA = 256
SB_WIDTH = 512
ROPE_BASE = 10000.0
LOG2_E = math.log2(math.e)
HEAD_PAD = 128
N_CHIPS = 4

ADAM_LR = 0.001
ADAM_B1 = 0.9
ADAM_B2 = 0.999
ADAM_EPS = 1e-08
ADAM_WD = 0.01
ADAM_STEP = 10

SEG_CQ = (0, 384)
SEG_CKV = (384, 256)
SEG_KROPE = (640, 128)
SEG_SBQ = (768, 512)
SEG_SBK = (1280, 512)
SEG_SBV = (1792, 512)
SEG_GATES = (2304, 2048)
IN_COLS_PAD = 4352

TM = 1024
TM_SMALL = 512
TM_PREP_BWD = 256
TQ = 256
MLA_FWD_BLOCKS = 4
MLA_BWD_BLOCKS = 4
SB_FWD_BLOCKS = 4
SB_BWD_BLOCKS = 2
SB_HEAD = 64
SB_SCALE = 0.125
SB_DEAD = -104.0
COL_CHUNK = 256
TN_MAX_COLS = 2816
TN_OPERAND_BYTES = 34 * 1024 * 1024
MAX_ROW_TILE = 512
VMEM_LIMIT = 56 * 1024 * 1024

NT = (((1,), (1,)), ((), ()))
TN = (((0,), (0,)), ((), ()))


def _cp(sem):
    return pltpu.CompilerParams(dimension_semantics=sem, vmem_limit_bytes=VMEM_LIMIT)


def _rows(tm, w):
    return pl.BlockSpec((tm, w), lambda i: (i, 0))


def _whole(shape):
    return pl.BlockSpec(shape, lambda i: (0,) * len(shape))


def _dot(a, b):
    return jnp.dot(a, b, preferred_element_type=F32)


def _dot_nt(a, b):
    return lax.dot_general(a, b, NT, preferred_element_type=F32)


def _dot_tn(a, b):
    return lax.dot_general(a, b, TN, preferred_element_type=F32)


def _rstd(x, n):
    return lax.rsqrt(jnp.sum(x * x, axis=-1, keepdims=True) / n + NORM_EPS)


def _rms_bwd(x, r, g, dy, n):
    gy = dy * g
    return r * gy - x * ((r * r * r) * (jnp.sum(x * gy, axis=-1, keepdims=True) / n))


def _sigmoid(x):
    return jax.nn.sigmoid(x)


def _pick(n, cands):
    for c in cands:
        if n % c == 0:
            return c
    return n


def _row_tile(r):
    for t in range(min(r, MAX_ROW_TILE) // 16 * 16, 15, -16):
        if r % t == 0:
            return t
    return r


HBM = pl.BlockSpec(memory_space=pl.ANY)

_Rider = collections.namedtuple("_Rider", "ins out_shape sems start relay finish")


def _with_rider(body, rider, *, name, grid, in_specs, out_specs, out_shape, args, sem, scratch=()):
    if rider is None:
        return pl.pallas_call(body, name=name, grid=grid, in_specs=in_specs, out_specs=out_specs, out_shape=out_shape,
                              scratch_shapes=list(scratch), compiler_params=_cp(sem))(*args), None
    ni, no, nri, nro = len(in_specs), len(out_specs), len(rider.ins), len(rider.out_shape)

    def riding(*refs):
        ins, r_ins = refs[:ni], refs[ni:ni + nri]
        outs, r_outs = refs[ni + nri:ni + nri + no], refs[ni + nri + no:ni + nri + no + nro]
        scr = refs[ni + nri + no + nro:ni + nri + no + nro + len(scratch)]
        sems = refs[ni + nri + no + nro + len(scratch):]
        step = pl.program_id(0)
        for a in range(1, len(grid)):
            step = step * grid[a] + pl.program_id(a)
        steps = math.prod(grid)

        @pl.when(step == 0)
        def _():
            rider.start(r_ins, r_outs, sems)

        body(*ins, *outs, *scr)

        if steps >= 3:
            @pl.when(step == steps - 2)
            def _():
                rider.relay(r_ins, r_outs, sems)

        @pl.when(step == steps - 1)
        def _():
            if steps < 3:
                rider.relay(r_ins, r_outs, sems)
            rider.finish(r_ins, r_outs, sems)

    res = pl.pallas_call(
        riding, name=name, grid=grid, in_specs=list(in_specs) + [HBM] * nri, out_specs=list(out_specs) + [HBM] * nro,
        out_shape=list(out_shape) + list(rider.out_shape),
        scratch_shapes=list(scratch) + [pltpu.SemaphoreType.DMA((k,)) for k in rider.sems],
        compiler_params=_cp(("arbitrary",) * len(grid)))(*args, *rider.ins)
    return res[:no], res[no:]


def _norm_call(h, g, name, rider=None):
    s, d = h.shape
    tm = min(TM, s)

    def body(h_ref, g_ref, u_ref):
        x = h_ref[...]
        u_ref[...] = ((x * _rstd(x, d)) * g_ref[...]).astype(BF16)

    (u,), got = _with_rider(
        body, rider, name=name, grid=(s // tm,),
        in_specs=[_rows(tm, d), _whole((1, d))], out_specs=[_rows(tm, d)],
        out_shape=[jax.ShapeDtypeStruct((s, d), BF16)], args=(h, g), sem=("parallel",))
    return u, got


def _ffn_in_call(u, w, name, rider=None):
    s, d = u.shape
    tn = w.shape[2]
    nj = w.shape[0] // 2
    n = nj * tn
    tm = min(TM, s)

    def body(u_ref, wa_ref, wb_ref, a_ref, b_ref, hm_ref):
        uu = u_ref[...]
        a = _dot(uu, wa_ref[...])
        b = _dot(uu, wb_ref[...])
        a_ref[...] = a
        b_ref[...] = b
        hm_ref[...] = ((a * _sigmoid(a)) * b).astype(BF16)

    blk = pl.BlockSpec((tm, tn), lambda j, i: (i, j))
    return _with_rider(
        body, rider, name=name, grid=(nj, s // tm),
        in_specs=[pl.BlockSpec((tm, d), lambda j, i: (i, 0)),
                  pl.BlockSpec((None, d, tn), lambda j, i: (j, 0, 0)),
                  pl.BlockSpec((None, d, tn), lambda j, i: (j + nj, 0, 0))],
        out_specs=[blk, blk, blk],
        out_shape=[jax.ShapeDtypeStruct((s, n), F32), jax.ShapeDtypeStruct((s, n), F32),
                   jax.ShapeDtypeStruct((s, n), BF16)],
        args=(u, w, w), sem=("parallel", "parallel"))


def _ffn_out_call(hm, w, h, gain, name, rider=None):
    s, n = hm.shape
    d = w.shape[1]
    tm = min(TM, s)

    def body(hm_ref, w_ref, h_ref, g_ref, o_ref, u_ref):
        x = h_ref[...] + 0.5 * _dot(hm_ref[...], w_ref[...])
        o_ref[...] = x
        u_ref[...] = ((x * _rstd(x, d)) * g_ref[...]).astype(BF16)

    return _with_rider(
        body, rider, name=name, grid=(s // tm,),
        in_specs=[_rows(tm, n), _whole((n, d)), _rows(tm, d), _whole((1, d))], out_specs=[_rows(tm, d), _rows(tm, d)],
        out_shape=[jax.ShapeDtypeStruct((s, d), F32), jax.ShapeDtypeStruct((s, d), BF16)], args=(hm, w, h, gain),
        sem=("parallel",))


def _mix_in_call(u, wt, name, rider=None):
    s, d = u.shape
    tm = min(TM_SMALL, s)
    segs = [(SEG_CQ, F32), (SEG_CKV, F32), (SEG_KROPE, F32), (SEG_SBQ, BF16), (SEG_SBK, BF16),
            (SEG_SBV, BF16), (SEG_GATES, F32)]

    def body(u_ref, w_ref, *outs):
        uu = u_ref[...]
        for ((off, width), _), o_ref in zip(segs, outs):
            o_ref[...] = _dot_nt(uu, w_ref[off:off + width, :]).astype(o_ref.dtype)

    return _with_rider(
        body, rider, name=name, grid=(s // tm,),
        in_specs=[_rows(tm, d), _whole((IN_COLS_PAD, d))],
        out_specs=[_rows(tm, width) for (_, width), _ in segs],
        out_shape=[jax.ShapeDtypeStruct((s, width), dt) for (_, width), dt in segs],
        args=(u, wt), sem=("parallel",))


def _lane(shape):
    return lax.broadcasted_iota(jnp.int32, shape, len(shape) - 1)


def _rot_half(y):
    lane = _lane(y.shape)
    swapped = jnp.where(lane < MLA_NOPE + MLA_ROPE // 2, pltpu.roll(y, HEAD_PAD - 16, 1), pltpu.roll(y, 16, 1))
    return jnp.where((lane >= MLA_NOPE) & (lane < MLA_QK), swapped, 0.0)


def _rope_tables(pos_ref, freq_ref, sign_ref):
    ang = pos_ref[...].astype(F32) * freq_ref[...]
    return jnp.cos(ang), jnp.sin(ang) * sign_ref[...]


def _head_fwd(x, g, cosv, ssv):
    r = _rstd(x, MLA_QK)
    y = (x * r) * g
    return y * cosv + _rot_half(y) * ssv, r


def _head_bwd(x, r, g, cosv, ssv, dout):
    dy = dout * cosv + _rot_half(dout * ssv)
    return _rms_bwd(x, r, g, dy, MLA_QK), jnp.sum(dy * (x * r), axis=0, keepdims=True)


def _mla_prep_call(cq, ckv, krope, pos, freq, sign, g_ql, g_kvl, g_qh, g_kh, wq, wkv, name):
    s = cq.shape[0]
    tm = min(TM_SMALL, s)
    width = HEADS * HEAD_PAD

    def body(cq_ref, ckv_ref, kr_ref, pos_ref, freq_ref, sign_ref, gql_ref, gkvl_ref, gqh_ref, gkh_ref,
             wq_ref, wkv_ref, q_ref, k_ref, v_ref):
        cosv, ssv = _rope_tables(pos_ref, freq_ref, sign_ref)
        x = cq_ref[...]
        qr = _dot(((x * _rstd(x, Q_LORA)) * gql_ref[...]).astype(BF16), wq_ref[...])
        x = ckv_ref[...]
        kv = _dot(((x * _rstd(x, KV_LORA)) * gkvl_ref[...]).astype(BF16), wkv_ref[...])
        kr = kr_ref[...]
        lane = _lane((tm, HEAD_PAD))
        for h in range(HEADS):
            sl = slice(h * HEAD_PAD, (h + 1) * HEAD_PAD)
            qh, _ = _head_fwd(qr[:, sl], gqh_ref[...], cosv, ssv)
            q_ref[:, sl] = qh.astype(BF16)
            kvh = kv[:, sl]
            kh, _ = _head_fwd(jnp.where(lane < MLA_NOPE, kvh, kr), gkh_ref[...], cosv, ssv)
            k_ref[:, sl] = kh.astype(BF16)
            v_ref[:, sl] = jnp.where(lane >= MLA_NOPE, kvh, jnp.where(lane == 0, 1.0, 0.0)).astype(BF16)

    out = jax.ShapeDtypeStruct((s, width), BF16)
    return pl.pallas_call(
        body, name=name, grid=(s // tm,),
        in_specs=[_rows(tm, Q_LORA), _rows(tm, KV_LORA), _rows(tm, HEAD_PAD), _rows(tm, 1),
                  _whole((1, HEAD_PAD)), _whole((1, HEAD_PAD)), _whole((1, Q_LORA)), _whole((1, KV_LORA)),
                  _whole((1, HEAD_PAD)), _whole((1, HEAD_PAD)), _whole((Q_LORA, width)), _whole((KV_LORA, width))],
        out_specs=[_rows(tm, width)] * 3, out_shape=[out, out, out],
        compiler_params=_cp(("parallel",)))(cq, ckv, krope, pos, freq, sign, g_ql, g_kvl, g_qh, g_kh, wq, wkv)


def _attn_specs(s, nb):
    qspec = pl.BlockSpec((TQ, nb * HEAD_PAD), lambda g, i: (i, g))
    kspec = pl.BlockSpec((s, nb * HEAD_PAD), lambda g, i: (0, g))
    return qspec, kspec


def _lanes(b):
    return slice(b * HEAD_PAD, (b + 1) * HEAD_PAD)


def _tri(cmp):
    r = lax.broadcasted_iota(jnp.int32, (TQ, TQ), 0)
    c = lax.broadcasted_iota(jnp.int32, (TQ, TQ), 1)
    return cmp(r, c)


def _mla_fwd_call(q, k, v, name, rider=None):
    s, width = q.shape
    scale = 1.0 / math.sqrt(MLA_QK)

    nb = MLA_FWD_BLOCKS

    def body(q_ref, k_ref, v_ref, o_ref, lse_ref):
        qi = pl.program_id(1)
        qs = [q_ref[:, _lanes(b)] for b in range(nb)]
        causal = _tri(lambda r, c: c <= r)

        def step(kb, carry, diag):
            ks = pl.multiple_of(kb * TQ, TQ)
            heads = range(nb)
            scs = [_dot_nt(qs[b], k_ref[pl.ds(ks, TQ), _lanes(b)]) * (scale * LOG2_E) for b in heads]
            if diag:
                scs = [jnp.where(causal, sc, -1e30) for sc in scs]
            mns = [jnp.maximum(carry[b][0], jnp.max(scs[b], axis=-1, keepdims=True)) for b in heads]
            als = [jnp.exp2(carry[b][0] - mns[b]) for b in heads]
            ps = [jnp.exp2(scs[b] - mns[b]).astype(BF16) for b in heads]
            accs = [als[b] * carry[b][1] + _dot(ps[b], v_ref[pl.ds(ks, TQ), _lanes(b)]) for b in heads]
            return tuple((mns[b], accs[b]) for b in heads)

        init = tuple((jnp.full((TQ, 1), -1e30, F32), jnp.zeros((TQ, HEAD_PAD), F32)) for _ in range(nb))
        carry = step(qi, init, True)
        carry = lax.fori_loop(0, qi, lambda kb, c: step(kb, c, False), carry)
        for b in range(nb):
            m, acc = carry[b]
            l = acc[:, 0:1]
            o_ref[:, _lanes(b)] = (acc / l).astype(BF16)
            lse_ref[:, _lanes(b)] = jnp.broadcast_to(m * (1.0 / LOG2_E) + jnp.log(l), (TQ, HEAD_PAD))

    qspec, kspec = _attn_specs(s, nb)
    return _with_rider(
        body, rider, name=name, grid=(width // (nb * HEAD_PAD), s // TQ),
        in_specs=[qspec, kspec, kspec], out_specs=[qspec, qspec],
        out_shape=[jax.ShapeDtypeStruct((s, width), BF16), jax.ShapeDtypeStruct((s, width), F32)],
        args=(q, k, v), sem=("parallel", "arbitrary"))


def _mla_bwd_call(q, k, v, o, do, lse, name, rider=None):
    s, width = q.shape
    scale = 1.0 / math.sqrt(MLA_QK)
    nb = MLA_BWD_BLOCKS

    def body(q_ref, k_ref, v_ref, o_ref, do_ref, lse_ref, dq_ref, dk_ref, dv_ref):
        qi = pl.program_id(1)

        @pl.when(qi == 0)
        def _():
            dk_ref[...] = jnp.zeros_like(dk_ref)
            dv_ref[...] = jnp.zeros_like(dv_ref)

        qs = [q_ref[:, _lanes(b)] for b in range(nb)]
        dos = [do_ref[:, _lanes(b)] for b in range(nb)]
        lses = [lse_ref[:, b * HEAD_PAD:b * HEAD_PAD + 1] for b in range(nb)]
        dlts = [jnp.sum(dos[b].astype(F32) * o_ref[:, _lanes(b)].astype(F32), axis=-1, keepdims=True) for b in range(nb)]
        causal = _tri(lambda r, c: c <= r)

        def step(kb, dqs, diag):
            ks = pl.multiple_of(kb * TQ, TQ)
            heads = range(nb)
            kts = [k_ref[pl.ds(ks, TQ), _lanes(b)] for b in heads]
            scs = [_dot_nt(qs[b], kts[b]) for b in heads]
            dps = [_dot_nt(dos[b], v_ref[pl.ds(ks, TQ), _lanes(b)]) for b in heads]
            ps = [jnp.exp(scs[b] * scale - lses[b]) for b in heads]
            if diag:
                ps = [jnp.where(causal, p, 0.0) for p in ps]
            dss = [(ps[b] * (dps[b] - dlts[b]) * scale).astype(BF16) for b in heads]
            dvs = [_dot_tn(ps[b].astype(BF16), dos[b]) for b in heads]
            dks = [_dot_tn(dss[b], qs[b]) for b in heads]
            out = tuple(dqs[b] + _dot(dss[b], kts[b]) for b in heads)
            for b in heads:
                dv_ref[pl.ds(ks, TQ), _lanes(b)] += dvs[b]
                dk_ref[pl.ds(ks, TQ), _lanes(b)] += dks[b]
            return out

        dqs = step(qi, tuple(jnp.zeros((TQ, HEAD_PAD), F32) for _ in range(nb)), True)
        dqs = lax.fori_loop(0, qi, lambda kb, c: step(kb, c, False), dqs)
        for b in range(nb):
            dq_ref[:, _lanes(b)] = dqs[b]

    qspec, kspec = _attn_specs(s, nb)
    out = jax.ShapeDtypeStruct((s, width), F32)
    return _with_rider(
        body, rider, name=name, grid=(width // (nb * HEAD_PAD), s // TQ),
        in_specs=[qspec, kspec, kspec, qspec, qspec, qspec], out_specs=[qspec, kspec, kspec],
        out_shape=[out, out, out], args=(q, k, v, o, do, lse), sem=("parallel", "arbitrary"))


def _dot_hilo(x, u):
    hi = x.astype(BF16)
    lo = (x - hi.astype(F32)).astype(BF16)
    return _dot(hi, u) + _dot(lo, u)


def _sb_logs(z):
    ls = jnp.minimum(z, 0.0) - jnp.log(1.0 + jnp.exp(-jnp.abs(z)))
    return ls, ls - z


def _sb_head_q(qb, first, hh):
    keep = first if hh == 0 else jnp.logical_not(first)
    return jnp.where(keep, qb, jnp.zeros_like(qb)) * jnp.asarray(SB_SCALE, qb.dtype)


def _sb_fwd_call(q, k, v, name):
    s, width = q.shape
    nb = SB_FWD_BLOCKS
    chains = [(b, hh) for b in range(nb) for hh in range(HEAD_PAD // SB_HEAD)]

    def body(q_ref, k_ref, v_ref, o_ref):
        qi = pl.program_id(1)
        strict = _tri(lambda r, c: c < r)
        after = _tri(lambda r, c: r > c).astype(BF16)
        first = _lane((1, HEAD_PAD)) < SB_HEAD
        qhs = [_sb_head_q(q_ref[:, _lanes(b)], first, hh) for b, hh in chains]

        def step(kb, carry, diag):
            ks = pl.multiple_of(kb * TQ, TQ)
            ids = range(len(chains))
            zs = [_dot_nt(qhs[ci], k_ref[pl.ds(ks, TQ), _lanes(chains[ci][0])]) for ci in ids]
            logs = [_sb_logs(z) for z in zs]
            lss = [lg[0] for lg in logs]
            l1ms = [jnp.where(strict, lg[1], 0.0) if diag else lg[1] for lg in logs]
            sufs = [_dot_hilo(l1m, after) for l1m in l1ms]
            as_ = [jnp.exp(lss[ci] + sufs[ci] + carry[ci][0]) for ci in ids]
            if diag:
                as_ = [jnp.where(strict, a, 0.0) for a in as_]
            accs = [carry[ci][1] + _dot(as_[ci].astype(BF16), v_ref[pl.ds(ks, TQ), _lanes(chains[ci][0])]) for ci in ids]
            return tuple((carry[ci][0] + jnp.sum(l1ms[ci], axis=-1, keepdims=True), accs[ci]) for ci in ids)

        init = tuple((jnp.zeros((TQ, 1), F32), jnp.zeros((TQ, HEAD_PAD), F32)) for _ in chains)
        carry = _sb_sweep(step, qi, init)
        for b in range(nb):
            o_ref[:, _lanes(b)] = jnp.where(first, carry[2 * b][1], carry[2 * b + 1][1])

    qspec, kspec = _attn_specs(s, nb)
    return pl.pallas_call(
        body, name=name, grid=(width // (nb * HEAD_PAD), s // TQ),
        in_specs=[qspec, kspec, kspec], out_specs=qspec, out_shape=jax.ShapeDtypeStruct((s, width), F32),
        compiler_params=_cp(("parallel", "arbitrary")))(q, k, v)


def _sb_sweep(step, qi, init):
    def live(carry):
        top = carry[0][0]
        for c in carry[1:]:
            top = jnp.maximum(top, c[0])
        return jnp.max(top)

    carry = step(qi, init, True)

    def cond(state):
        j, alive, _ = state
        return jnp.logical_and(j < qi, alive > SB_DEAD)

    def body(state):
        j, _, carry = state
        carry = step(qi - 1 - j, carry, False)
        return j + 1, live(carry), carry

    return lax.while_loop(cond, body, (jnp.int32(0), live(carry), carry))[2]


def _sb_bwd_call(q, k, v, do, o, name):
    s, width = q.shape
    nb = SB_BWD_BLOCKS
    chains = [(b, hh) for b in range(nb) for hh in range(HEAD_PAD // SB_HEAD)]

    def body(q_ref, k_ref, v_ref, do_ref, o_ref, dq_ref, dk_ref, dv_ref):
        qi = pl.program_id(1)

        @pl.when(qi == 0)
        def _():
            dk_ref[...] = jnp.zeros_like(dk_ref)
            dv_ref[...] = jnp.zeros_like(dv_ref)

        strict = _tri(lambda r, c: c < r)
        after = _tri(lambda r, c: r > c).astype(BF16)
        from_here = _tri(lambda r, c: r >= c).astype(BF16)
        first = _lane((1, HEAD_PAD)) < SB_HEAD
        qhs = [_sb_head_q(q_ref[:, _lanes(b)], first, hh) for b, hh in chains]
        dohs = []
        for b, hh in chains:
            dob = do_ref[:, _lanes(b)]
            dohs.append(jnp.where(first if hh == 0 else jnp.logical_not(first), dob, jnp.zeros_like(dob)))
        gtots = [jnp.sum(dohs[ci].astype(F32) * o_ref[:, _lanes(chains[ci][0])], axis=-1, keepdims=True)
                 for ci in range(len(chains))]

        def step(kb, carry, diag):
            ks = pl.multiple_of(kb * TQ, TQ)
            ids = range(len(chains))
            kts = [k_ref[pl.ds(ks, TQ), _lanes(b)] for b, _ in chains]
            zs = [_dot_nt(qhs[ci], kts[ci]) for ci in ids]
            das = [_dot_nt(dohs[ci], v_ref[pl.ds(ks, TQ), _lanes(chains[ci][0])]) for ci in ids]
            logs = [_sb_logs(z) for z in zs]
            lss = [lg[0] for lg in logs]
            l1ms = [jnp.where(strict, lg[1], 0.0) if diag else lg[1] for lg in logs]
            sufs = [_dot_hilo(l1m, after) for l1m in l1ms]
            as_ = [jnp.exp(lss[ci] + sufs[ci] + carry[ci][0]) for ci in ids]
            if diag:
                as_ = [jnp.where(strict, a, 0.0) for a in as_]
            abs_ = [a.astype(BF16) for a in as_]
            gs = [abs_[ci].astype(F32) * das[ci] for ci in ids]
            cexs = [gtots[ci] - (carry[ci][1] + _dot_hilo(gs[ci], from_here)) for ci in ids]
            dzs = [gs[ci] - jnp.exp(lss[ci]) * (gs[ci] + cexs[ci]) for ci in ids]
            if diag:
                dzs = [jnp.where(strict, dz, 0.0) for dz in dzs]
            dzbs = [dz.astype(BF16) for dz in dzs]
            dvps = [_dot_tn(abs_[ci], dohs[ci]) for ci in ids]
            dkps = [_dot_tn(dzbs[ci], qhs[ci]) for ci in ids]
            out = tuple((carry[ci][0] + jnp.sum(l1ms[ci], axis=-1, keepdims=True),
                         carry[ci][1] + jnp.sum(gs[ci], axis=-1, keepdims=True),
                         carry[ci][2] + _dot(dzbs[ci], kts[ci])) for ci in ids)
            for b in range(nb):
                dk_ref[pl.ds(ks, TQ), _lanes(b)] += dkps[2 * b] + dkps[2 * b + 1]
                dv_ref[pl.ds(ks, TQ), _lanes(b)] += dvps[2 * b] + dvps[2 * b + 1]
            return out

        init = tuple((jnp.zeros((TQ, 1), F32), jnp.zeros((TQ, 1), F32), jnp.zeros((TQ, HEAD_PAD), F32)) for _ in chains)
        carry = _sb_sweep(step, qi, init)
        for b in range(nb):
            dq_ref[:, _lanes(b)] = (jnp.where(first, carry[2 * b][2], carry[2 * b + 1][2]) * SB_SCALE).astype(BF16)

    qspec, kspec = _attn_specs(s, nb)
    return pl.pallas_call(
        body, name=name, grid=(width // (nb * HEAD_PAD), s // TQ),
        in_specs=[qspec, kspec, kspec, qspec, qspec], out_specs=[qspec, kspec, kspec],
        out_shape=[jax.ShapeDtypeStruct((s, width), BF16), jax.ShapeDtypeStruct((s, width), F32),
                   jax.ShapeDtypeStruct((s, width), F32)],
        compiler_params=_cp(("parallel", "arbitrary")))(q, k, v, do, o)


def _merge_out_call(om, osb, gates, h, wbm, wbs, wo, gain, name):
    s, d = h.shape
    tm = min(TM_SMALL, s)

    def body(om_ref, os_ref, g_ref, h_ref, wbm_ref, wbs_ref, wo_ref, gain_ref, h2_ref, bm_ref, bs_ref, mg_ref, u_ref):
        bm = _dot(om_ref[...], wbm_ref[...])
        bs = _dot(os_ref[...].astype(BF16), wbs_ref[...])
        mg = (_sigmoid(g_ref[:, :d]) * bm + _sigmoid(g_ref[:, d:]) * bs).astype(BF16)
        bm_ref[...] = bm
        bs_ref[...] = bs
        mg_ref[...] = mg
        x = h_ref[...] + _dot(mg, wo_ref[...])
        h2_ref[...] = x
        u_ref[...] = ((x * _rstd(x, d)) * gain_ref[...]).astype(BF16)

    return pl.pallas_call(
        body, name=name, grid=(s // tm,),
        in_specs=[_rows(tm, om.shape[1]), _rows(tm, SB_WIDTH), _rows(tm, 2 * d), _rows(tm, d),
                  _whole(wbm.shape), _whole(wbs.shape), _whole(wo.shape), _whole((1, d))],
        out_specs=[_rows(tm, d)] * 5,
        out_shape=[jax.ShapeDtypeStruct((s, d), F32), jax.ShapeDtypeStruct((s, d), F32),
                   jax.ShapeDtypeStruct((s, d), F32), jax.ShapeDtypeStruct((s, d), BF16),
                   jax.ShapeDtypeStruct((s, d), BF16)],
        compiler_params=_cp(("parallel",)))(om, osb, gates, h, wbm, wbs, wo, gain)


def _ple_call(h, g, wg, p, wp, tgt, name):
    s, d = h.shape
    tm = min(TM_SMALL, s)

    def body(h_ref, g_ref, wg_ref, p_ref, wp_ref, t_ref, dh_ref, dhs_ref, un_ref, dgl_ref, dpp_ref, dg_ref, sq_ref):
        @pl.when(pl.program_id(0) == 0)
        def _():
            dg_ref[...] = jnp.zeros_like(dg_ref)
            sq_ref[...] = jnp.zeros_like(sq_ref)

        x = h_ref[...]
        gain = g_ref[...]
        r = _rstd(x, d)
        xh = x * r
        un = (xh * gain).astype(BF16)
        sg = _sigmoid(_dot(un, wg_ref[...]))
        pp = _dot(p_ref[...].astype(BF16), wp_ref[...])
        diff = (x + sg * pp) - t_ref[...]
        sq_ref[...] += jnp.sum(diff * diff, axis=0, keepdims=True)
        dy = diff * (1.0 / d)
        dgl = ((dy * pp) * (sg * (1.0 - sg))).astype(BF16)
        dun = _dot_nt(dgl, wg_ref[...])
        dg_ref[...] += jnp.sum(dun * xh, axis=0, keepdims=True)
        dh = dy + _rms_bwd(x, r, gain, dun, d)
        dh_ref[...] = dh
        dhs_ref[...] = (0.5 * dh).astype(BF16)
        un_ref[...] = un
        dgl_ref[...] = dgl
        dpp_ref[...] = (dy * sg).astype(BF16)

    bf = jax.ShapeDtypeStruct((s, d), BF16)
    vec = jax.ShapeDtypeStruct((1, d), F32)
    return pl.pallas_call(
        body, name=name, grid=(s // tm,),
        in_specs=[_rows(tm, d), _whole((1, d)), _whole(wg.shape), _rows(tm, PLE_DIM), _whole(wp.shape), _rows(tm, d)],
        out_specs=[_rows(tm, d)] * 5 + [_whole((1, d))] * 2,
        out_shape=[jax.ShapeDtypeStruct((s, d), F32), bf, bf, bf, bf, vec, vec],
        compiler_params=_cp(("arbitrary",)))(h, g, wg, p, wp, tgt)


def _ffn_bwd_a_call(dhs, a, b, wo, name, rider=None):
    s, n = a.shape
    d = dhs.shape[1]
    tn = n // 2
    tm = min(TM, s)

    def body(dh_ref, a_ref, b_ref, wo_ref, da_ref, db_ref):
        dh = dh_ref[...]
        chunks = [slice(c0, min(c0 + COL_CHUNK, tn)) for c0 in range(0, tn, COL_CHUNK)]
        dhms = [_dot_nt(dh, wo_ref[sl, :]) for sl in chunks]
        for sl, dhm in zip(chunks, dhms):
            av = a_ref[:, sl]
            sa = _sigmoid(av)
            da_ref[:, sl] = (dhm * b_ref[:, sl] * (sa * (1.0 + av * (1.0 - sa)))).astype(BF16)
            db_ref[:, sl] = (dhm * (av * sa)).astype(BF16)

    blk = pl.BlockSpec((tm, tn), lambda j, i: (i, j))
    return _with_rider(
        body, rider, name=name, grid=(n // tn, s // tm),
        in_specs=[pl.BlockSpec((tm, d), lambda j, i: (i, 0)), blk, blk, pl.BlockSpec((tn, d), lambda j, i: (j, 0))],
        out_specs=[blk, blk],
        out_shape=[jax.ShapeDtypeStruct((s, n), BF16)] * 2, args=(dhs, a, b, wo), sem=("parallel", "parallel"))


def _norm_bwd_call(dy_list, w_list, h, g, dh_in, name, half_out, rider=None, w_transposed=False):
    s, d = h.shape
    tm = min(TM_SMALL, s)
    nk, nw = len(dy_list), len(w_list)
    factor = 0.5 if half_out else 1.0
    sharded = nw == 1 and w_list[0].ndim == 3

    def body(*refs):
        dy_refs = refs[:nk]
        w_refs = refs[nk:nk + nw]
        h_ref, g_ref, dhin_ref, dh_ref, dhb_ref, dg_ref = refs[nk + nw:]

        @pl.when(pl.program_id(0) == 0)
        def _():
            dg_ref[...] = jnp.zeros_like(dg_ref)

        if sharded:
            c = w_list[0].shape[2]
            per = dy_list[0].shape[1] // c
            du = None
            for k in range(w_list[0].shape[0]):
                part = _dot_nt(dy_refs[k // per][:, (k % per) * c:(k % per + 1) * c], w_refs[0][k])
                du = part if du is None else du + part
        else:
            mm = _dot if w_transposed else _dot_nt
            du = mm(dy_refs[0][...], w_refs[0][...])
            for dy_ref, w_ref in zip(dy_refs[1:], w_refs[1:]):
                du = du + mm(dy_ref[...], w_ref[...])
        x = h_ref[...]
        r = _rstd(x, d)
        dg_ref[...] += jnp.sum(du * (x * r), axis=0, keepdims=True)
        dh = dhin_ref[...] + _rms_bwd(x, r, g_ref[...], du, d)
        dh_ref[...] = dh
        dhb_ref[...] = (factor * dh).astype(BF16)

    outs, got = _with_rider(
        body, rider, name=name, grid=(s // tm,),
        in_specs=[_rows(tm, dy.shape[1]) for dy in dy_list] + [_whole(w.shape) for w in w_list]
        + [_rows(tm, d), _whole((1, d)), _rows(tm, d)],
        out_specs=[_rows(tm, d), _rows(tm, d), _whole((1, d))],
        out_shape=[jax.ShapeDtypeStruct((s, d), F32), jax.ShapeDtypeStruct((s, d), BF16),
                   jax.ShapeDtypeStruct((1, d), F32)],
        args=(*dy_list, *w_list, h, g, dh_in), sem=("arbitrary",))
    return outs if rider is None else (outs, got)


def _merge_bwd_call(dhb, gates, bm, bs, wo, wbm, wbs, name):
    s, d = bm.shape
    tm = min(TM_SMALL, s)

    def body(dh_ref, g_ref, bm_ref, bs_ref, wo_ref, wbm_ref, wbs_ref, dg_ref, dbm_ref, dbs_ref, dom_ref, dos_ref):
        dmg = _dot_nt(dh_ref[...], wo_ref[...])
        s1 = _sigmoid(g_ref[:, :d])
        s2 = _sigmoid(g_ref[:, d:])
        dg_ref[:, :d] = (dmg * bm_ref[...] * (s1 * (1.0 - s1))).astype(BF16)
        dg_ref[:, d:] = (dmg * bs_ref[...] * (s2 * (1.0 - s2))).astype(BF16)
        dbm = (dmg * s1).astype(BF16)
        dbs = (dmg * s2).astype(BF16)
        dbm_ref[...] = dbm
        dbs_ref[...] = dbs
        dom_ref[...] = _dot_nt(dbm, wbm_ref[...]).astype(BF16)
        dos_ref[...] = _dot_nt(dbs, wbs_ref[...]).astype(BF16)

    wm = wbm.shape[0]
    return pl.pallas_call(
        body, name=name, grid=(s // tm,),
        in_specs=[_rows(tm, d), _rows(tm, 2 * d), _rows(tm, d), _rows(tm, d),
                  _whole(wo.shape), _whole(wbm.shape), _whole(wbs.shape)],
        out_specs=[_rows(tm, 2 * d), _rows(tm, d), _rows(tm, d), _rows(tm, wm), _rows(tm, SB_WIDTH)],
        out_shape=[jax.ShapeDtypeStruct((s, 2 * d), BF16), jax.ShapeDtypeStruct((s, d), BF16),
                   jax.ShapeDtypeStruct((s, d), BF16), jax.ShapeDtypeStruct((s, wm), BF16),
                   jax.ShapeDtypeStruct((s, SB_WIDTH), BF16)],
        compiler_params=_cp(("parallel",)))(dhb, gates, bm, bs, wo, wbm, wbs)


def _mla_prep_bwd_call(cq, ckv, krope, pos, freq, sign, g_ql, g_kvl, g_qh, g_kh, wq, wkv, dq, dk, dv, name):
    s = cq.shape[0]
    tm = min(TM_PREP_BWD, s)
    width = HEADS * HEAD_PAD

    def body(cq_ref, ckv_ref, kr_ref, pos_ref, freq_ref, sign_ref, gql_ref, gkvl_ref, gqh_ref, gkh_ref,
             wq_ref, wkv_ref, dq_ref, dk_ref, dv_ref,
             dcq_ref, dckv_ref, dkr_ref, dwq_ref, dwkv_ref, dgql_ref, dgkvl_ref, dgqh_ref, dgkh_ref, dqr_ref, dkv_ref):
        @pl.when(pl.program_id(0) == 0)
        def _():
            for ref in (dwq_ref, dwkv_ref, dgql_ref, dgkvl_ref, dgqh_ref, dgkh_ref):
                ref[...] = jnp.zeros_like(ref)

        cosv, ssv = _rope_tables(pos_ref, freq_ref, sign_ref)
        xq = cq_ref[...]
        rq = _rstd(xq, Q_LORA)
        cqn = ((xq * rq) * gql_ref[...]).astype(BF16)
        qr = _dot(cqn, wq_ref[...])
        xk = ckv_ref[...]
        rk = _rstd(xk, KV_LORA)
        ckvn = ((xk * rk) * gkvl_ref[...]).astype(BF16)
        kv = _dot(ckvn, wkv_ref[...])
        kr = kr_ref[...]
        lane = _lane((tm, HEAD_PAD))
        dkr = jnp.zeros((tm, HEAD_PAD), F32)
        dgqh = jnp.zeros((1, HEAD_PAD), F32)
        dgkh = jnp.zeros((1, HEAD_PAD), F32)
        for h in range(HEADS):
            sl = slice(h * HEAD_PAD, (h + 1) * HEAD_PAD)
            x = qr[:, sl]
            dx, dgh = _head_bwd(x, _rstd(x, MLA_QK), gqh_ref[...], cosv, ssv, dq_ref[:, sl])
            dqr_ref[:, sl] = dx.astype(BF16)
            dgqh = dgqh + dgh
            x = jnp.where(lane < MLA_NOPE, kv[:, sl], kr)
            dx, dgh = _head_bwd(x, _rstd(x, MLA_QK), gkh_ref[...], cosv, ssv, dk_ref[:, sl])
            dgkh = dgkh + dgh
            dkr = dkr + jnp.where(lane >= MLA_NOPE, dx, 0.0)
            dkv_ref[:, sl] = jnp.where(lane < MLA_NOPE, dx, dv_ref[:, sl]).astype(BF16)
        dgqh_ref[...] += dgqh
        dgkh_ref[...] += dgkh
        dkr_ref[...] = dkr.astype(BF16)
        dqr = dqr_ref[...]
        dkvb = dkv_ref[...]
        dwq_ref[...] += _dot_tn(cqn, dqr)
        dwkv_ref[...] += _dot_tn(ckvn, dkvb)
        dcqn = _dot_nt(dqr, wq_ref[...])
        dgql_ref[...] += jnp.sum(dcqn * (xq * rq), axis=0, keepdims=True)
        dcq_ref[...] = _rms_bwd(xq, rq, gql_ref[...], dcqn, Q_LORA).astype(BF16)
        dckvn = _dot_nt(dkvb, wkv_ref[...])
        dgkvl_ref[...] += jnp.sum(dckvn * (xk * rk), axis=0, keepdims=True)
        dckv_ref[...] = _rms_bwd(xk, rk, gkvl_ref[...], dckvn, KV_LORA).astype(BF16)

    vec = lambda n: jax.ShapeDtypeStruct((1, n), F32)
    outs = pl.pallas_call(
        body, name=name, grid=(s // tm,),
        in_specs=[_rows(tm, Q_LORA), _rows(tm, KV_LORA), _rows(tm, HEAD_PAD), _rows(tm, 1),
                  _whole((1, HEAD_PAD)), _whole((1, HEAD_PAD)), _whole((1, Q_LORA)), _whole((1, KV_LORA)),
                  _whole((1, HEAD_PAD)), _whole((1, HEAD_PAD)), _whole((Q_LORA, width)), _whole((KV_LORA, width)),
                  _rows(tm, width), _rows(tm, width), _rows(tm, width)],
        out_specs=[_rows(tm, Q_LORA), _rows(tm, KV_LORA), _rows(tm, HEAD_PAD), _whole((Q_LORA, width)),
                   _whole((KV_LORA, width)), _whole((1, Q_LORA)), _whole((1, KV_LORA)), _whole((1, HEAD_PAD)),
                   _whole((1, HEAD_PAD)), _rows(tm, width), _rows(tm, width)],
        out_shape=[jax.ShapeDtypeStruct((s, Q_LORA), BF16), jax.ShapeDtypeStruct((s, KV_LORA), BF16),
                   jax.ShapeDtypeStruct((s, HEAD_PAD), BF16), jax.ShapeDtypeStruct((Q_LORA, width), F32),
                   jax.ShapeDtypeStruct((KV_LORA, width), F32), vec(Q_LORA), vec(KV_LORA), vec(HEAD_PAD), vec(HEAD_PAD),
                   jax.ShapeDtypeStruct((s, width), BF16), jax.ShapeDtypeStruct((s, width), BF16)],
        compiler_params=_cp(("arbitrary",)))(cq, ckv, krope, pos, freq, sign, g_ql, g_kvl, g_qh, g_kh, wq, wkv, dq, dk, dv)
    return outs[:9]


def _tn_call(a, b, name, shard_cols=None, rider=None):
    s, ka = a.shape
    nb = b.shape[1]
    ti = _pick(ka, (512, 256, 128))
    if shard_cols is not None:
        tj = shard_cols
    else:
        tj = nb if nb <= TN_MAX_COLS else _pick(nb, (2176, 1024, 512, 256, 128))
    ts = s if 2 * s * (ti + tj) * a.dtype.itemsize <= TN_OPERAND_BYTES else s // 2
    ns = s // ts

    def body(a_ref, b_ref, o_ref, acc_ref):
        part = _dot_tn(a_ref[...].astype(BF16), b_ref[...].astype(BF16))
        if ns == 1:
            o_ref[...] = part.astype(o_ref.dtype)
            return

        @pl.when(pl.program_id(2) == 0)
        def _():
            acc_ref[...] = part

        @pl.when(pl.program_id(2) != 0)
        def _():
            acc_ref[...] += part

        @pl.when(pl.program_id(2) == ns - 1)
        def _():
            o_ref[...] = acc_ref[...].astype(o_ref.dtype)

    if shard_cols is None:
        out_spec = pl.BlockSpec((ti, tj), lambda i, j, t: (i, j))
        out_shape = jax.ShapeDtypeStruct((ka, nb), BF16)
    else:
        out_spec = pl.BlockSpec((None, ti, tj), lambda i, j, t: (j, i, 0))
        out_shape = jax.ShapeDtypeStruct((nb // tj, ka, tj), BF16)
    (out,), got = _with_rider(
        body, rider, name=name, grid=(ka // ti, nb // tj, ns),
        in_specs=[pl.BlockSpec((ts, ti), lambda i, j, t: (t, i)), pl.BlockSpec((ts, tj), lambda i, j, t: (t, j))],
        out_specs=[out_spec], out_shape=[out_shape], scratch=[pltpu.VMEM((ti, tj), F32)], args=(a, b),
        sem=("parallel", "parallel", "arbitrary"))
    return out if rider is None else (out, got)


def _sum_call(parts, out_dtype, name):
    n, r, w = parts.shape
    tr = _row_tile(r)

    def body(p_ref, o_ref):
        acc = p_ref[0].astype(F32)
        for k in range(1, n):
            acc = acc + p_ref[k].astype(F32)
        o_ref[...] = acc.astype(out_dtype)

    return pl.pallas_call(
        body, name=name, grid=(r // tr,),
        in_specs=[pl.BlockSpec((n, tr, w), lambda i: (0, i, 0))], out_specs=_rows(tr, w),
        out_shape=jax.ShapeDtypeStruct((r, w), out_dtype), compiler_params=_cp(("parallel",)))(parts)


def _chip_sum_call(by_chip, core, name):
    n, r, w = by_chip.shape
    tr = _row_tile(r)
    nblk = r // tr

    def body(c_ref, p_ref, o_ref):
        acc = p_ref[0].astype(F32)
        for k in range(1, n):
            acc = acc + p_ref[k].astype(F32)
        o_ref[...] = acc

    return pl.pallas_call(
        body, name=name,
        grid_spec=pltpu.PrefetchScalarGridSpec(
            num_scalar_prefetch=1, grid=(nblk,),
            in_specs=[pl.BlockSpec((n, tr, w), lambda i, c_ref: (0, i, 0))],
            out_specs=pl.BlockSpec((tr, w), lambda i, c_ref: (c_ref[0] * nblk + i, 0))),
        out_shape=jax.ShapeDtypeStruct((2 * r, w), F32),
        compiler_params=_cp(("parallel",)))(core.reshape(1).astype(jnp.int32), by_chip)


def _pair_sum_call(full, other, core, out_dtype, name):
    n, r, w = other.shape
    tr = _row_tile(r)
    nblk = r // tr

    def body(c_ref, a_ref, b_ref, o_ref):
        o_ref[...] = (a_ref[...].astype(F32) + b_ref[...].astype(F32)).astype(out_dtype)

    spec = pl.BlockSpec((None, tr, w), lambda k, i, c_ref: (k, i, 0))
    return pl.pallas_call(
        body, name=name,
        grid_spec=pltpu.PrefetchScalarGridSpec(
            num_scalar_prefetch=1, grid=(n, nblk),
            in_specs=[pl.BlockSpec((None, tr, w), lambda k, i, c_ref: (k, c_ref[0] * nblk + i, 0)), spec],
            out_specs=spec),
        out_shape=jax.ShapeDtypeStruct((n, r, w), out_dtype),
        compiler_params=_cp(("parallel", "parallel")))(core.reshape(1).astype(jnp.int32), full, other)


def _adamw_call(w, g, row0, m, v, name):
    r, c = w.shape
    span = math.gcd(r, row0) if row0 else r
    tr = next((t for t in range(min(span, 256) // 8 * 8, 0, -8) if span % t == 0), span)
    off = row0 // tr

    def body(w_ref, g_ref, m_ref, v_ref, g_out_ref, d_ref, nm_ref, nv_ref):
        gg = g_ref[...]
        g_out_ref[...] = gg
        nm = ADAM_B1 * m_ref[...] + (1.0 - ADAM_B1) * gg
        nv = ADAM_B2 * v_ref[...] + (1.0 - ADAM_B2) * (gg * gg)
        m_hat = nm / (1.0 - ADAM_B1 ** ADAM_STEP)
        v_hat = nv / (1.0 - ADAM_B2 ** ADAM_STEP)
        d_ref[...] = -ADAM_LR * (m_hat / (jnp.sqrt(v_hat) + ADAM_EPS) + ADAM_WD * w_ref[...])
        nm_ref[...] = nm
        nv_ref[...] = nv

    out = jax.ShapeDtypeStruct((r, c), F32)
    g_spec = pl.BlockSpec((tr, c), lambda i: (off + i, 0))
    return pl.pallas_call(
        body, name=name, grid=(r // tr,), in_specs=[_rows(tr, c), g_spec, _rows(tr, c), _rows(tr, c)],
        out_specs=[_rows(tr, c)] * 4, out_shape=[out, out, out, out], compiler_params=_cp(("parallel",)))(w, g, m, v)


def _position():
    x, y, c = lax.axis_index("x"), lax.axis_index("y"), lax.axis_index("c")
    chips = [(1 - x, y), (x, 1 - y), (1 - x, 1 - y)]
    return x, y, c, chips


def _gather_rider(parts):
    n = len(parts)
    pairs = [(j, k) for j in range(3) for k in range(n)]

    def piece(out_refs, k, chip, core):
        half = parts[k].shape[0] // 2
        return out_refs[k].at[2 * chip[0] + chip[1], pl.ds(core * half, half), :]

    def over_ici(in_refs, out_refs, sems, j, k):
        x, y, c, chips = _position()
        half = parts[k].shape[0] // 2
        return pltpu.make_async_remote_copy(
            src_ref=in_refs[k].at[pl.ds(c * half, half), :], dst_ref=piece(out_refs, k, (x, y), c),
            send_sem=sems[0].at[n * j + k], recv_sem=sems[1].at[n * j + k], device_id=(*chips[j], c), device_id_type=MESH)

    def to_sibling(out_refs, sems, j, k):
        x, y, c, chips = _position()
        landed = piece(out_refs, k, chips[j], c)
        return pltpu.make_async_remote_copy(
            src_ref=landed, dst_ref=landed, send_sem=sems[2].at[n * j + k], recv_sem=sems[3].at[n * j + k],
            device_id=(x, y, 1 - c), device_id_type=MESH)

    def start(in_refs, out_refs, sems):
        for j, k in pairs:
            over_ici(in_refs, out_refs, sems, j, k).start()

    def relay(in_refs, out_refs, sems):
        for j, k in pairs:
            over_ici(in_refs, out_refs, sems, j, k).wait_recv()
            to_sibling(out_refs, sems, j, k).start()

    def finish(in_refs, out_refs, sems):
        for j, k in pairs:
            to_sibling(out_refs, sems, j, k).wait_recv()
        for j, k in pairs:
            over_ici(in_refs, out_refs, sems, j, k).wait_send()
            to_sibling(out_refs, sems, j, k).wait_send()

    return _Rider(list(parts), [jax.ShapeDtypeStruct((N_CHIPS,) + p.shape, p.dtype) for p in parts], [3 * n] * 4,
                  start, relay, finish)


def _scatter_rider(parts):
    n = len(parts)
    pairs = [(j, k) for j in range(3) for k in range(n)]

    def copy(in_refs, out_refs, sems, j, k):
        x, y, c, chips = _position()
        return pltpu.make_async_remote_copy(
            src_ref=in_refs[k].at[2 * chips[j][0] + chips[j][1]], dst_ref=out_refs[k].at[2 * x + y],
            send_sem=sems[0].at[n * j + k], recv_sem=sems[1].at[n * j + k], device_id=(*chips[j], c), device_id_type=MESH)

    def start(in_refs, out_refs, sems):
        for j, k in pairs:
            copy(in_refs, out_refs, sems, j, k).start()

    def finish(in_refs, out_refs, sems):
        for j, k in pairs:
            copy(in_refs, out_refs, sems, j, k).wait()

    return _Rider(list(parts), [jax.ShapeDtypeStruct(p.shape, p.dtype) for p in parts], [3 * n] * 2, start,
                  lambda in_refs, out_refs, sems: None, finish)


def _pair_send_call(parts, name):
    n = len(parts)

    def body(*refs):
        in_refs, out_refs = refs[:n], refs[n:2 * n]
        send_sems, recv_sems = refs[2 * n:]
        x, y, c, _ = _position()
        copies = []
        for k in range(n):
            half = parts[k].shape[1] // 2
            cp = pltpu.make_async_remote_copy(
                src_ref=in_refs[k].at[:, pl.ds((1 - c) * half, half), :], dst_ref=out_refs[k],
                send_sem=send_sems.at[k], recv_sem=recv_sems.at[k], device_id=(x, y, 1 - c), device_id_type=MESH)
            cp.start()
            copies.append(cp)
        for cp in copies:
            cp.wait()

    sems = pltpu.SemaphoreType.DMA((n,))
    return pl.pallas_call(
        body, name=name, in_specs=[HBM] * n, out_specs=[HBM] * n,
        out_shape=[jax.ShapeDtypeStruct((p.shape[0], p.shape[1] // 2, p.shape[2]), p.dtype) for p in parts],
        scratch_shapes=[sems, sems])(*parts)


def _pair_swap_call(parts, name):
    n = len(parts)

    def body(*refs):
        out_refs = refs[n:2 * n]
        send_sems, recv_sems = refs[2 * n:]
        x, y, c, _ = _position()
        copies = []
        for k in range(n):
            half = parts[k].shape[0] // 2
            mine = out_refs[k].at[pl.ds(c * half, half), :]
            cp = pltpu.make_async_remote_copy(
                src_ref=mine, dst_ref=mine, send_sem=send_sems.at[k], recv_sem=recv_sems.at[k],
                device_id=(x, y, 1 - c), device_id_type=MESH)
            cp.start()
            copies.append(cp)
        for cp in copies:
            cp.wait()

    sems = pltpu.SemaphoreType.DMA((n,))
    return pl.pallas_call(
        body, name=name, in_specs=[HBM] * n, out_specs=[HBM] * n,
        out_shape=[jax.ShapeDtypeStruct(p.shape, p.dtype) for p in parts],
        input_output_aliases={k: k for k in range(n)},
        scratch_shapes=[sems, sems])(*parts)


def _all_gather_small_call(block, name):
    r, w = block.shape

    def body(in_ref, out_ref, send_sems, recv_sems, local_sem):
        x, y, c, _ = _position()
        me = 4 * x + 2 * y + c
        own = pltpu.make_async_copy(in_ref, out_ref.at[me], local_sem)
        own.start()
        copies = []
        for k in range(1, 8):
            peer = (x ^ (k >> 2), y ^ ((k >> 1) & 1), c ^ (k & 1))
            cp = pltpu.make_async_remote_copy(
                src_ref=in_ref, dst_ref=out_ref.at[me], send_sem=send_sems.at[k - 1], recv_sem=recv_sems.at[k - 1],
                device_id=peer, device_id_type=MESH)
            cp.start()
            copies.append(cp)
        for cp in copies:
            cp.wait()
        own.wait()

    return pl.pallas_call(
        body, name=name, in_specs=[HBM], out_specs=HBM,
        out_shape=jax.ShapeDtypeStruct((8, r, w), block.dtype),
        scratch_shapes=[pltpu.SemaphoreType.DMA((7,)), pltpu.SemaphoreType.DMA((7,)), pltpu.SemaphoreType.DMA])(block)


BIG = {
    "ffn1_w_in": ((D_MODEL, 2 * D_FF), 1), "ffn1_w_out": ((D_FF, D_MODEL), 0),
    "w_in": ((D_MODEL, 4256), 1), "w_q_up": ((Q_LORA, HEADS * MLA_QK), 1), "w_kv_up": ((KV_LORA, 1024), 1),
    "w_branch_mla": ((512, D_MODEL), 1), "w_branch_sb": ((SB_WIDTH, D_MODEL), 1), "w_out": ((D_MODEL, D_MODEL), 0),
    "ffn2_w_in": ((D_MODEL, 2 * D_FF), 1), "ffn2_w_out": ((D_FF, D_MODEL), 0),
    "w_ple_gate": ((D_MODEL, D_MODEL), 0), "w_ple_proj": ((PLE_DIM, D_MODEL), 1),
}
GAINS = {"ffn1_norm": 1024, "mix_norm": 1024, "q_latent_norm": 384, "kv_latent_norm": 256, "q_head_norm": 96,
         "k_head_norm": 96, "ffn2_norm": 1024, "ple_norm": 1024}
WEIGHT_ORDER = ["ffn1_norm", "ffn1_w_in", "ffn1_w_out", "mix_norm", "w_in", "q_latent_norm", "w_q_up",
                "kv_latent_norm", "w_kv_up", "q_head_norm", "k_head_norm", "w_branch_mla", "w_branch_sb", "w_out",
                "ffn2_norm", "ffn2_w_in", "ffn2_w_out", "ple_norm", "w_ple_gate", "w_ple_proj"]


W_IN_SHARD_ROWS = 1088


def _shard_shape(name):
    (r, c), axis = BIG[name]
    if name in TRANSPOSED_UPDATE:
        return (W_IN_SHARD_ROWS, r)
    return (r // N_CHIPS, c) if axis == 0 else (r, c // N_CHIPS)


GATHER_GROUPS = [
    [("ffn1_w_in",)],
    [("ffn1_w_out",), ("w_in",)],
    [("w_out",), ("w_kv_up", "w_branch_mla", "w_branch_sb"), ("w_q_up",)],
    [("ffn2_w_in",), ("ffn2_w_out", "w_ple_gate"), ("w_ple_proj",)],
]
REDUCE_GROUPS = [
    [("ffn2_w_in",), ("ffn2_w_out", "w_out", "w_ple_gate"), ("w_branch_mla", "w_branch_sb", "w_ple_proj")],
    [("w_in",), ("w_kv_up",), ("w_q_up",)],
    [("ffn1_w_out",)],
    [("ffn1_w_in",)],
]


def _join_parts(shards, group):
    return [shards[part[0]] if len(part) == 1 else jnp.concatenate([shards[n] for n in part], axis=-2) for part in group]


def _part_rows(group):
    where = {}
    for k, part in enumerate(group):
        at = 0
        for n in part:
            where[n] = (k, at)
            at += _shard_shape(n)[0]
    return where


def _split_parts(parts, group):
    return {n: parts[k][..., at:at + _shard_shape(n)[0], :] for n, (k, at) in _part_rows(group).items()}


def _exchange_form(name, shard):
    if name in TRANSPOSED_UPDATE:
        t = shard.T.astype(BF16)
        return jnp.pad(t, ((0, W_IN_SHARD_ROWS - t.shape[0]), (0, 0)))
    return shard.astype(BF16)


def _to_shards(name, full):
    (r, c), axis = BIG[name]
    if axis == 0:
        return full.reshape(N_CHIPS, r // N_CHIPS, c)
    return full.reshape(r, N_CHIPS, c // N_CHIPS).transpose(1, 0, 2)


def _from_shards(name, shards):
    (r, c), axis = BIG[name]
    if axis == 0:
        return shards.reshape(r, c)
    return shards.transpose(1, 0, 2).reshape(r, c)


def _relayout_w_in(wt):
    d = wt.shape[1]
    z = lambda n: jnp.zeros((n, d), wt.dtype)
    return jnp.concatenate([wt[:640], z(MLA_NOPE), wt[640:672], z(HEAD_PAD - MLA_QK), wt[672:]], axis=0)


def _unlayout_w_in(gt):
    full = jnp.concatenate([gt[:640], gt[640 + MLA_NOPE:640 + MLA_QK], gt[768:]], axis=0)
    shards = full.reshape(N_CHIPS, -1, gt.shape[1])
    return jnp.pad(shards, ((0, 0), (0, W_IN_SHARD_ROWS - shards.shape[1]), (0, 0)))


def _pad_heads(v):
    lead = v.shape[:-1]
    return jnp.pad(v.reshape(lead + (HEADS, MLA_QK)), [(0, 0)] * len(lead) + [(0, 0), (0, HEAD_PAD - MLA_QK)]).reshape(
        lead + (HEADS * HEAD_PAD,))


SHARD_MAJOR = ("ffn1_w_in", "ffn2_w_in")
TRANSPOSED_UPDATE = ("w_in",)


def _step(x, p, pos, tgt, gains, weights, dist):
    d = D_MODEL
    full = {} if dist is not None else {
        n: _to_shards(n, w) if n in SHARD_MAJOR else (w.T if n in TRANSPOSED_UPDATE else w) for n, w in weights.items()}
    reduced = {}

    def gather_rider(g):
        if dist is None:
            return None, None
        mine = _join_parts(weights, GATHER_GROUPS[g])
        return mine, _gather_rider(mine)

    def gathered(g, mine, others):
        if dist is not None:
            parts = [lax.dynamic_update_slice_in_dim(o, m[None], dist[0], axis=0) for o, m in zip(others, mine)]
            for n, shards in _split_parts(parts, GATHER_GROUPS[g]).items():
                if n in TRANSPOSED_UPDATE:
                    (d_in, c_out), _ = BIG[n]
                    full[n] = shards[:, :c_out // N_CHIPS].reshape(c_out, d_in)
                else:
                    full[n] = shards if n in SHARD_MAJOR else _from_shards(n, shards)

    def reduce_before(g):
        if dist is None:
            return None, None
        group = REDUCE_GROUPS[g]
        shards = {n: grads[n] if grads[n].ndim == 3 else _to_shards(n, grads[n].astype(BF16)) for part in group for n in part}
        partial = _join_parts(shards, group)
        from_sibling = _pair_send_call(partial, "grads%d_pair_send" % g)
        pair_sum = [_pair_sum_call(a, b, dist[1], BF16, "grads%d_pair_sum_%d" % (g, k))
                    for k, (a, b) in enumerate(zip(partial, from_sibling))]
        return pair_sum, _scatter_rider(pair_sum)

    def reduce_after(g, pair_sum, by_chip):
        if dist is not None:
            chip, core = dist
            by_chip = [lax.dynamic_update_slice_in_dim(t, lax.dynamic_slice_in_dim(o, chip, 1, axis=0), chip, axis=0)
                       for t, o in zip(by_chip, pair_sum)]
            bufs = _pair_swap_call([_chip_sum_call(t, core, "grads%d_chip_sum_%d" % (g, k)) for k, t in enumerate(by_chip)],
                                   "grads%d_pair_swap" % g)
            for n, (k, row0) in _part_rows(REDUCE_GROUPS[g]).items():
                reduced[n] = (bufs[k], row0)

    mine, rider = gather_rider(0)
    u1, got = _norm_call(x, gains["ffn1_norm"], "norm_ffn1", rider)
    gathered(0, mine, got)
    wts = full
    inv_freq = ROPE_BASE ** (-jnp.arange(0, MLA_ROPE, 2, dtype=F32) / MLA_ROPE)
    zeros = lambda n: jnp.zeros((n,), F32)
    freq = jnp.concatenate([zeros(MLA_NOPE), inv_freq, inv_freq, zeros(HEAD_PAD - MLA_QK)])[None]
    sign = jnp.concatenate([zeros(MLA_NOPE), -jnp.ones((16,), F32), jnp.ones((16,), F32), zeros(HEAD_PAD - MLA_QK)])[None]
    pad_gain = lambda g: jnp.pad(g, ((0, 0), (0, HEAD_PAD - MLA_QK)))
    g_qh, g_kh = pad_gain(gains["q_head_norm"]), pad_gain(gains["k_head_norm"])

    mine, rider = gather_rider(1)
    (a1, b1, hm1), got = _ffn_in_call(u1, wts["ffn1_w_in"], "ffn1_in", rider)
    gathered(1, mine, got)
    mine, rider = gather_rider(2)
    (h1, um), got = _ffn_out_call(hm1, wts["ffn1_w_out"], x, gains["mix_norm"], "ffn1_out", rider)
    gathered(2, mine, got)
    w_in = _relayout_w_in(wts["w_in"])
    (cq, ckv, krope, sbq, sbk, sbv, gates), _ = _mix_in_call(um, w_in, "mix_in")
    wq = _pad_heads(wts["w_q_up"])
    wkv = wts["w_kv_up"]
    wbm = jnp.pad(wts["w_branch_mla"].reshape(HEADS, 64, d), ((0, 0), (64, 0), (0, 0))).reshape(HEADS * HEAD_PAD, d)
    wbs, wo = wts["w_branch_sb"], wts["w_out"]
    prep_args = (cq, ckv, krope, pos, freq, sign, gains["q_latent_norm"], gains["kv_latent_norm"], g_qh, g_kh, wq, wkv)
    q, k, v = _mla_prep_call(*prep_args, "mla_prep")
    mine, rider = gather_rider(3)
    (om, lse), got = _mla_fwd_call(q, k, v, "mla_fwd", rider)
    gathered(3, mine, got)
    osb = _sb_fwd_call(sbq, sbk, sbv, "sb_fwd")
    h2, bm, bs, mg, u2 = _merge_out_call(om, osb, gates, h1, wbm, wbs, wo, gains["ffn2_norm"], "merge_out")
    (a2, b2, hm2), _ = _ffn_in_call(u2, wts["ffn2_w_in"], "ffn2_in")
    (h3, _), _ = _ffn_out_call(hm2, wts["ffn2_w_out"], h2, gains["ple_norm"], "ffn2_out")

    grads, gg = {}, {}
    dh3, dh3s, un, dgl, dpp, gg["ple_norm"], sq = _ple_call(
        h3, gains["ple_norm"], wts["w_ple_gate"], p, wts["w_ple_proj"], tgt, "ple")
    grads["w_ple_gate"] = _tn_call(un, dgl, "dw_ple_gate")
    grads["w_ple_proj"] = _tn_call(p, dpp, "dw_ple_proj")

    (da2, db2), _ = _ffn_bwd_a_call(dh3s, a2, b2, wts["ffn2_w_out"], "ffn2_bwd_act")
    grads["ffn2_w_out"] = _tn_call(hm2, dh3s, "dw_ffn2_out")
    grads["ffn2_w_in"] = jnp.concatenate([_tn_call(u2, da2, "dw_ffn2_in_a", shard_cols=D_FF // 2),
                                          _tn_call(u2, db2, "dw_ffn2_in_b", shard_cols=D_FF // 2)], axis=0)
    dh2, dh2b, gg["ffn2_norm"] = _norm_bwd_call([da2, db2], [wts["ffn2_w_in"]], h2, gains["ffn2_norm"], dh3,
                                                "ffn2_bwd_norm", half_out=False)

    dgates, dbm, dbs, dom, dos = _merge_bwd_call(dh2b, gates, bm, bs, wo, wbm, wbs, "merge_bwd")
    grads["w_out"] = _tn_call(mg, dh2b, "dw_out")
    grads["w_branch_mla"] = _tn_call(om, dbm, "dw_branch_mla").reshape(HEADS, HEAD_PAD, d)[:, 64:, :].reshape(512, d)
    grads["w_branch_sb"] = _tn_call(osb, dbs, "dw_branch_sb")
    pair_sum, rider = reduce_before(0)
    (dq, dk, dv), got = _mla_bwd_call(q, k, v, om, dom, lse, "mla_bwd", rider)
    reduce_after(0, pair_sum, got)
    dsq, dsk, dsv = _sb_bwd_call(sbq, sbk, sbv, dos, osb, "sb_bwd")
    (dcq, dckv, dkr, dwq, grads["w_kv_up"], gg["q_latent_norm"], gg["kv_latent_norm"], dgqh, dgkh) = \
        _mla_prep_bwd_call(*prep_args, dq, dk, dv, "mla_prep_bwd")
    grads["w_q_up"] = dwq.reshape(Q_LORA, HEADS, HEAD_PAD)[:, :, :MLA_QK].reshape(Q_LORA, HEADS * MLA_QK)
    gg["q_head_norm"], gg["k_head_norm"] = dgqh[:, :MLA_QK], dgkh[:, :MLA_QK]
    dproj = jnp.concatenate([dcq, dckv, dkr, dsq, dsk.astype(BF16), dsv.astype(BF16), dgates], axis=1)
    grads["w_in"] = _unlayout_w_in(_tn_call(dproj, um, "dw_in"))
    dh1, dh1s, gg["mix_norm"] = _norm_bwd_call([dproj], [w_in], h1, gains["mix_norm"], dh2, "mix_bwd_norm", half_out=True,
                                               w_transposed=True)

    pair_sum, rider = reduce_before(1)
    (da1, db1), got = _ffn_bwd_a_call(dh1s, a1, b1, wts["ffn1_w_out"], "ffn1_bwd_act", rider)
    reduce_after(1, pair_sum, got)
    grads["ffn1_w_out"] = _tn_call(hm1, dh1s, "dw_ffn1_out")
    pair_sum, rider = reduce_before(2)
    res = _tn_call(u1, da1, "dw_ffn1_in_a", shard_cols=D_FF // 2, rider=rider)
    dwa, got = (res, None) if rider is None else res
    reduce_after(2, pair_sum, got)
    grads["ffn1_w_in"] = jnp.concatenate([dwa, _tn_call(u1, db1, "dw_ffn1_in_b", shard_cols=D_FF // 2)], axis=0)
    pair_sum, rider = reduce_before(3)
    res = _norm_bwd_call([da1, db1], [wts["ffn1_w_in"]], x, gains["ffn1_norm"], dh1, "ffn1_bwd_norm",
                         half_out=False, rider=rider)
    (dx, _, gg["ffn1_norm"]), got = (res, None) if rider is None else res
    reduce_after(3, pair_sum, got)
    return sq, dx, gg, (grads if dist is None else reduced)


def kernel(x, p, positions, ffn1_norm, ffn1_w_in, ffn1_w_out, mix_norm, w_in, q_latent_norm, w_q_up, kv_latent_norm, w_kv_up, q_head_norm, k_head_norm, w_branch_mla, w_branch_sb, w_out, ffn2_norm, ffn2_w_in, ffn2_w_out, ple_norm, w_ple_gate, w_ple_proj, loss_target, m_ffn1_norm, m_ffn1_w_in, m_ffn1_w_out, m_mix_norm, m_w_in, m_q_latent_norm, m_w_q_up, m_kv_latent_norm, m_w_kv_up, m_q_head_norm, m_k_head_norm, m_w_branch_mla, m_w_branch_sb, m_w_out, m_ffn2_norm, m_ffn2_w_in, m_ffn2_w_out, m_ple_norm, m_w_ple_gate, m_w_ple_proj, v_ffn1_norm, v_ffn1_w_in, v_ffn1_w_out, v_mix_norm, v_w_in, v_q_latent_norm, v_w_q_up, v_kv_latent_norm, v_w_kv_up, v_q_head_norm, v_k_head_norm, v_w_branch_mla, v_w_branch_sb, v_w_out, v_ffn2_norm, v_ffn2_w_in, v_ffn2_w_out, v_ple_norm, v_w_ple_gate, v_w_ple_proj):
    given = dict(locals())
    w_shard = {n: given[n][0] for n in WEIGHT_ORDER}
    m_shard = {n: given["m_" + n][0] for n in WEIGHT_ORDER}
    v_shard = {n: given["v_" + n][0] for n in WEIGHT_ORDER}
    gains = {n: w_shard[n][None] for n in GAINS}

    chip = 2 * lax.axis_index("x") + lax.axis_index("y")
    sq, dx, gain_grads, reduced = _step(x[0], p[0, 0], positions.reshape(-1, 1), loss_target[0], gains,
                                        {n: _exchange_form(n, w_shard[n]) for n in BIG}, (chip, lax.axis_index("c")))

    rows = [jnp.pad(gain_grads[n], ((0, 0), (0, D_MODEL - GAINS[n]))) for n in GAINS] + [sq]
    gain_block = jnp.concatenate(rows + [jnp.zeros((16 - len(rows), D_MODEL), F32)], axis=0)
    gain_sum = _sum_call(_all_gather_small_call(gain_block, "gains_all_gather"), F32, "gains_sum")
    loss = 0.5 * jnp.sum(gain_sum[len(GAINS)]) / D_MODEL

    outs = {"grad": {}, "delta": {}, "new_m": {}, "new_v": {}}
    gain_pack = lambda t: jnp.concatenate([jnp.pad(t[n][None], ((0, 0), (0, D_MODEL - GAINS[n]))) for n in GAINS], axis=0)
    packed = _adamw_call(gain_pack(w_shard), gain_sum, 0, gain_pack(m_shard), gain_pack(v_shard), "adamw_gains")
    for i, n in enumerate(GAINS):
        for kind, t in zip(("grad", "delta", "new_m", "new_v"), packed):
            outs[kind][n] = t[i, :GAINS[n]][None]
    for n in BIG:
        buf, row0 = reduced[n]
        if n in TRANSPOSED_UPDATE:
            res = [t.T for t in _adamw_call(w_shard[n].T, buf, row0, m_shard[n].T, v_shard[n].T, "adamw_" + n)]
        else:
            res = _adamw_call(w_shard[n], buf, row0, m_shard[n], v_shard[n], "adamw_" + n)
        for kind, t in zip(("grad", "delta", "new_m", "new_v"), res):
            outs[kind][n] = t[None]

    return (loss, dx[None], *[outs["grad"][n] for n in WEIGHT_ORDER], *[outs["delta"][n] for n in WEIGHT_ORDER],
            *[outs["new_m"][n] for n in WEIGHT_ORDER], *[outs["new_v"][n] for n in WEIGHT_ORDER])
```

```python
import collections
import functools
import math

import jax
import jax.numpy as jnp
from jax import lax
from jax.experimental import pallas as pl
from jax.experimental.pallas import tpu as pltpu

F32 = jnp.float32
BF16 = jnp.bfloat16
MESH = pl.DeviceIdType.MESH

D_MODEL = 1024
D_FF = 2816
PLE_DIM = 256
NORM_EPS = 1e-6
HEADS = 8
MLA_NOPE = 64
MLA_ROPE = 32
MLA_QK = 96
Q_LORA = 384
KV_LORA = 256
SB_WIDTH = 512
ROPE_BASE = 10000.0
LOG2_E = math.log2(math.e)
HEAD_PAD = 128
N_CHIPS = 4

ADAM_LR = 0.001
ADAM_B1 = 0.9
ADAM_B2 = 0.999
ADAM_EPS = 1e-08
ADAM_WD = 0.01
ADAM_STEP = 10

SEG_CQ = (0, 384)
SEG_CKV = (384, 256)
SEG_KROPE = (640, 128)
SEG_SBQ = (768, 512)
SEG_SBK = (1280, 512)
SEG_SBV = (1792, 512)
SEG_GATES = (2304, 2048)
IN_COLS_PAD = 4352

TM = 1024
TM_SMALL = 512
TM_PREP_BWD = 256
TQ = 256
MLA_FWD_BLOCKS = 4
MLA_BWD_BLOCKS = 4
SB_FWD_BLOCKS = 4
SB_BWD_BLOCKS = 2
SB_HEAD = 64
SB_SCALE = 0.125
SB_DEAD = -104.0
COL_CHUNK = 256
TN_MAX_COLS = 2816
TN_OPERAND_BYTES = 34 * 1024 * 1024
MAX_ROW_TILE = 512
VMEM_LIMIT = 56 * 1024 * 1024

NT = (((1,), (1,)), ((), ()))
TN = (((0,), (0,)), ((), ()))


def _cp(sem):
    return pltpu.CompilerParams(dimension_semantics=sem, vmem_limit_bytes=VMEM_LIMIT)


def _rows(tm, w):
    return pl.BlockSpec((tm, w), lambda i: (i, 0))


def _whole(shape):
    return pl.BlockSpec(shape, lambda i: (0,) * len(shape))


def _dot(a, b):
    return jnp.dot(a, b, preferred_element_type=F32)


def _dot_nt(a, b):
    return lax.dot_general(a, b, NT, preferred_element_type=F32)


def _dot_tn(a, b):
    return lax.dot_general(a, b, TN, preferred_element_type=F32)


def _rstd(x, n):
    return lax.rsqrt(jnp.sum(x * x, axis=-1, keepdims=True) / n + NORM_EPS)


def _rms_bwd(x, r, g, dy, n):
    gy = dy * g
    return r * gy - x * ((r * r * r) * (jnp.sum(x * gy, axis=-1, keepdims=True) / n))


def _sigmoid(x):
    return jax.nn.sigmoid(x)


def _pick(n, cands):
    for c in cands:
        if n % c == 0:
            return c
    return n


def _row_tile(r):
    for t in range(min(r, MAX_ROW_TILE) // 16 * 16, 15, -16):
        if r % t == 0:
            return t
    return r


HBM = pl.BlockSpec(memory_space=pl.ANY)

_Rider = collections.namedtuple("_Rider", "ins out_shape sems start relay finish")


def _with_rider(body, rider, *, name, grid, in_specs, out_specs, out_shape, args, sem, scratch=()):
    if rider is None:
        return pl.pallas_call(body, name=name, grid=grid, in_specs=in_specs, out_specs=out_specs, out_shape=out_shape,
                              scratch_shapes=list(scratch), compiler_params=_cp(sem))(*args), None
    ni, no, nri, nro = len(in_specs), len(out_specs), len(rider.ins), len(rider.out_shape)

    def riding(*refs):
        ins, r_ins = refs[:ni], refs[ni:ni + nri]
        outs, r_outs = refs[ni + nri:ni + nri + no], refs[ni + nri + no:ni + nri + no + nro]
        scr = refs[ni + nri + no + nro:ni + nri + no + nro + len(scratch)]
        sems = refs[ni + nri + no + nro + len(scratch):]
        step = pl.program_id(0)
        for a in range(1, len(grid)):
            step = step * grid[a] + pl.program_id(a)
        steps = math.prod(grid)

        @pl.when(step == 0)
        def _():
            rider.start(r_ins, r_outs, sems)

        body(*ins, *outs, *scr)

        if steps >= 3:
            @pl.when(step == steps - 2)
            def _():
                rider.relay(r_ins, r_outs, sems)

        @pl.when(step == steps - 1)
        def _():
            if steps < 3:
                rider.relay(r_ins, r_outs, sems)
            rider.finish(r_ins, r_outs, sems)

    res = pl.pallas_call(
        riding, name=name, grid=grid, in_specs=list(in_specs) + [HBM] * nri, out_specs=list(out_specs) + [HBM] * nro,
        out_shape=list(out_shape) + list(rider.out_shape),
        scratch_shapes=list(scratch) + [pltpu.SemaphoreType.DMA((k,)) for k in rider.sems],
        compiler_params=_cp(("arbitrary",) * len(grid)))(*args, *rider.ins)
    return res[:no], res[no:]


def _norm_call(h, g, name, rider=None):
    s, d = h.shape
    tm = min(TM, s)

    def body(h_ref, g_ref, u_ref):
        x = h_ref[...]
        u_ref[...] = ((x * _rstd(x, d)) * g_ref[...]).astype(BF16)

    (u,), got = _with_rider(
        body, rider, name=name, grid=(s // tm,),
        in_specs=[_rows(tm, d), _whole((1, d))], out_specs=[_rows(tm, d)],
        out_shape=[jax.ShapeDtypeStruct((s, d), BF16)], args=(h, g), sem=("parallel",))
    return u, got


def _ffn_in_call(u, w, name, rider=None):
    s, d = u.shape
    tn = w.shape[2]
    nj = w.shape[0] // 2
    n = nj * tn
    tm = min(TM, s)

    def body(u_ref, wa_ref, wb_ref, a_ref, b_ref, hm_ref):
        uu = u_ref[...]
        a = _dot(uu, wa_ref[...])
        b = _dot(uu, wb_ref[...])
        a_ref[...] = a
        b_ref[...] = b
        hm_ref[...] = ((a * _sigmoid(a)) * b).astype(BF16)

    blk = pl.BlockSpec((tm, tn), lambda j, i: (i, j))
    return _with_rider(
        body, rider, name=name, grid=(nj, s // tm),
        in_specs=[pl.BlockSpec((tm, d), lambda j, i: (i, 0)),
                  pl.BlockSpec((None, d, tn), lambda j, i: (j, 0, 0)),
                  pl.BlockSpec((None, d, tn), lambda j, i: (j + nj, 0, 0))],
        out_specs=[blk, blk, blk],
        out_shape=[jax.ShapeDtypeStruct((s, n), F32), jax.ShapeDtypeStruct((s, n), F32),
                   jax.ShapeDtypeStruct((s, n), BF16)],
        args=(u, w, w), sem=("parallel", "parallel"))


def _ffn_out_call(hm, w, h, gain, name, rider=None):
    s, n = hm.shape
    d = w.shape[1]
    tm = min(TM, s)

    def body(hm_ref, w_ref, h_ref, g_ref, o_ref, u_ref):
        x = h_ref[...] + 0.5 * _dot(hm_ref[...], w_ref[...])
        o_ref[...] = x
        u_ref[...] = ((x * _rstd(x, d)) * g_ref[...]).astype(BF16)

    return _with_rider(
        body, rider, name=name, grid=(s // tm,),
        in_specs=[_rows(tm, n), _whole((n, d)), _rows(tm, d), _whole((1, d))], out_specs=[_rows(tm, d), _rows(tm, d)],
        out_shape=[jax.ShapeDtypeStruct((s, d), F32), jax.ShapeDtypeStruct((s, d), BF16)], args=(hm, w, h, gain),
        sem=("parallel",))


def _mix_in_call(u, wt, name, rider=None):
    s, d = u.shape
    tm = min(TM_SMALL, s)
    segs = [(SEG_CQ, F32), (SEG_CKV, F32), (SEG_KROPE, F32), (SEG_SBQ, BF16), (SEG_SBK, BF16),
            (SEG_SBV, BF16), (SEG_GATES, F32)]

    def body(u_ref, w_ref, *outs):
        uu = u_ref[...]
        for ((off, width), _), o_ref in zip(segs, outs):
            o_ref[...] = _dot_nt(uu, w_ref[off:off + width, :]).astype(o_ref.dtype)

    return _with_rider(
        body, rider, name=name, grid=(s // tm,),
        in_specs=[_rows(tm, d), _whole((IN_COLS_PAD, d))],
        out_specs=[_rows(tm, width) for (_, width), _ in segs],
        out_shape=[jax.ShapeDtypeStruct((s, width), dt) for (_, width), dt in segs],
        args=(u, wt), sem=("parallel",))


def _lane(shape):
    return lax.broadcasted_iota(jnp.int32, shape, len(shape) - 1)


def _rot_half(y):
    lane = _lane(y.shape)
    swapped = jnp.where(lane < MLA_NOPE + MLA_ROPE // 2, pltpu.roll(y, HEAD_PAD - 16, 1), pltpu.roll(y, 16, 1))
    return jnp.where((lane >= MLA_NOPE) & (lane < MLA_QK), swapped, 0.0)


def _rope_tables(pos_ref, freq_ref, sign_ref):
    ang = pos_ref[...].astype(F32) * freq_ref[...]
    return jnp.cos(ang), jnp.sin(ang) * sign_ref[...]


def _head_fwd(x, g, cosv, ssv):
    r = _rstd(x, MLA_QK)
    y = (x * r) * g
    return y * cosv + _rot_half(y) * ssv, r


def _head_bwd(x, r, g, cosv, ssv, dout):
    dy = dout * cosv + _rot_half(dout * ssv)
    return _rms_bwd(x, r, g, dy, MLA_QK), jnp.sum(dy * (x * r), axis=0, keepdims=True)


def _mla_prep_call(cq, ckv, krope, pos, freq, sign, g_ql, g_kvl, g_qh, g_kh, wq, wkv, name):
    s = cq.shape[0]
    tm = min(TM_SMALL, s)
    width = HEADS * HEAD_PAD

    def body(cq_ref, ckv_ref, kr_ref, pos_ref, freq_ref, sign_ref, gql_ref, gkvl_ref, gqh_ref, gkh_ref,
             wq_ref, wkv_ref, q_ref, k_ref, v_ref):
        cosv, ssv = _rope_tables(pos_ref, freq_ref, sign_ref)
        x = cq_ref[...]
        qr = _dot(((x * _rstd(x, Q_LORA)) * gql_ref[...]).astype(BF16), wq_ref[...])
        x = ckv_ref[...]
        kv = _dot(((x * _rstd(x, KV_LORA)) * gkvl_ref[...]).astype(BF16), wkv_ref[...])
        kr = kr_ref[...]
        lane = _lane((tm, HEAD_PAD))
        for h in range(HEADS):
            sl = slice(h * HEAD_PAD, (h + 1) * HEAD_PAD)
            qh, _ = _head_fwd(qr[:, sl], gqh_ref[...], cosv, ssv)
            q_ref[:, sl] = qh.astype(BF16)
            kvh = kv[:, sl]
            kh, _ = _head_fwd(jnp.where(lane < MLA_NOPE, kvh, kr), gkh_ref[...], cosv, ssv)
            k_ref[:, sl] = kh.astype(BF16)
            v_ref[:, sl] = jnp.where(lane >= MLA_NOPE, kvh, jnp.where(lane == 0, 1.0, 0.0)).astype(BF16)

    out = jax.ShapeDtypeStruct((s, width), BF16)
    return pl.pallas_call(
        body, name=name, grid=(s // tm,),
        in_specs=[_rows(tm, Q_LORA), _rows(tm, KV_LORA), _rows(tm, HEAD_PAD), _rows(tm, 1),
                  _whole((1, HEAD_PAD)), _whole((1, HEAD_PAD)), _whole((1, Q_LORA)), _whole((1, KV_LORA)),
                  _whole((1, HEAD_PAD)), _whole((1, HEAD_PAD)), _whole((Q_LORA, width)), _whole((KV_LORA, width))],
        out_specs=[_rows(tm, width)] * 3, out_shape=[out, out, out],
        compiler_params=_cp(("parallel",)))(cq, ckv, krope, pos, freq, sign, g_ql, g_kvl, g_qh, g_kh, wq, wkv)


def _attn_specs(s, nb):
    qspec = pl.BlockSpec((TQ, nb * HEAD_PAD), lambda g, i: (i, g))
    kspec = pl.BlockSpec((s, nb * HEAD_PAD), lambda g, i: (0, g))
    return qspec, kspec


def _lanes(b):
    return slice(b * HEAD_PAD, (b + 1) * HEAD_PAD)


def _tri(cmp):
    r = lax.broadcasted_iota(jnp.int32, (TQ, TQ), 0)
    c = lax.broadcasted_iota(jnp.int32, (TQ, TQ), 1)
    return cmp(r, c)


def _mla_fwd_call(q, k, v, name, rider=None):
    s, width = q.shape
    scale = 1.0 / math.sqrt(MLA_QK)

    nb = MLA_FWD_BLOCKS

    def body(q_ref, k_ref, v_ref, o_ref, lse_ref):
        qi = pl.program_id(1)
        qs = [q_ref[:, _lanes(b)] for b in range(nb)]
        causal = _tri(lambda r, c: c <= r)

        def step(kb, carry, diag):
            ks = pl.multiple_of(kb * TQ, TQ)
            heads = range(nb)
            scs = [_dot_nt(qs[b], k_ref[pl.ds(ks, TQ), _lanes(b)]) * (scale * LOG2_E) for b in heads]
            if diag:
                scs = [jnp.where(causal, sc, -1e30) for sc in scs]
            mns = [jnp.maximum(carry[b][0], jnp.max(scs[b], axis=-1, keepdims=True)) for b in heads]
            als = [jnp.exp2(carry[b][0] - mns[b]) for b in heads]
            ps = [jnp.exp2(scs[b] - mns[b]).astype(BF16) for b in heads]
            accs = [als[b] * carry[b][1] + _dot(ps[b], v_ref[pl.ds(ks, TQ), _lanes(b)]) for b in heads]
            return tuple((mns[b], accs[b]) for b in heads)

        init = tuple((jnp.full((TQ, 1), -1e30, F32), jnp.zeros((TQ, HEAD_PAD), F32)) for _ in range(nb))
        carry = step(qi, init, True)
        carry = lax.fori_loop(0, qi, lambda kb, c: step(kb, c, False), carry)
        for b in range(nb):
            m, acc = carry[b]
            l = acc[:, 0:1]
            o_ref[:, _lanes(b)] = (acc / l).astype(BF16)
            lse_ref[:, _lanes(b)] = jnp.broadcast_to(m * (1.0 / LOG2_E) + jnp.log(l), (TQ, HEAD_PAD))

    qspec, kspec = _attn_specs(s, nb)
    return _with_rider(
        body, rider, name=name, grid=(width // (nb * HEAD_PAD), s // TQ),
        in_specs=[qspec, kspec, kspec], out_specs=[qspec, qspec],
        out_shape=[jax.ShapeDtypeStruct((s, width), BF16), jax.ShapeDtypeStruct((s, width), F32)],
        args=(q, k, v), sem=("parallel", "arbitrary"))


def _mla_bwd_call(q, k, v, o, do, lse, name, rider=None):
    s, width = q.shape
    scale = 1.0 / math.sqrt(MLA_QK)
    nb = MLA_BWD_BLOCKS

    def body(q_ref, k_ref, v_ref, o_ref, do_ref, lse_ref, dq_ref, dk_ref, dv_ref):
        qi = pl.program_id(1)

        @pl.when(qi == 0)
        def _():
            dk_ref[...] = jnp.zeros_like(dk_ref)
            dv_ref[...] = jnp.zeros_like(dv_ref)

        qs = [q_ref[:, _lanes(b)] for b in range(nb)]
        dos = [do_ref[:, _lanes(b)] for b in range(nb)]
        lses = [lse_ref[:, b * HEAD_PAD:b * HEAD_PAD + 1] for b in range(nb)]
        dlts = [jnp.sum(dos[b].astype(F32) * o_ref[:, _lanes(b)].astype(F32), axis=-1, keepdims=True) for b in range(nb)]
        causal = _tri(lambda r, c: c <= r)

        def step(kb, dqs, diag):
            ks = pl.multiple_of(kb * TQ, TQ)
            heads = range(nb)
            kts = [k_ref[pl.ds(ks, TQ), _lanes(b)] for b in heads]
            scs = [_dot_nt(qs[b], kts[b]) for b in heads]
            dps = [_dot_nt(dos[b], v_ref[pl.ds(ks, TQ), _lanes(b)]) for b in heads]
            ps = [jnp.exp(scs[b] * scale - lses[b]) for b in heads]
            if diag:
                ps = [jnp.where(causal, p, 0.0) for p in ps]
            dss = [(ps[b] * (dps[b] - dlts[b]) * scale).astype(BF16) for b in heads]
            dvs = [_dot_tn(ps[b].astype(BF16), dos[b]) for b in heads]
            dks = [_dot_tn(dss[b], qs[b]) for b in heads]
            out = tuple(dqs[b] + _dot(dss[b], kts[b]) for b in heads)
            for b in heads:
                dv_ref[pl.ds(ks, TQ), _lanes(b)] += dvs[b]
                dk_ref[pl.ds(ks, TQ), _lanes(b)] += dks[b]
            return out

        dqs = step(qi, tuple(jnp.zeros((TQ, HEAD_PAD), F32) for _ in range(nb)), True)
        dqs = lax.fori_loop(0, qi, lambda kb, c: step(kb, c, False), dqs)
        for b in range(nb):
            dq_ref[:, _lanes(b)] = dqs[b]

    qspec, kspec = _attn_specs(s, nb)
    out = jax.ShapeDtypeStruct((s, width), F32)
    return _with_rider(
        body, rider, name=name, grid=(width // (nb * HEAD_PAD), s // TQ),
        in_specs=[qspec, kspec, kspec, qspec, qspec, qspec], out_specs=[qspec, kspec, kspec],
        out_shape=[out, out, out], args=(q, k, v, o, do, lse), sem=("parallel", "arbitrary"))


def _dot_hilo(x, u):
    hi = x.astype(BF16)
    lo = (x - hi.astype(F32)).astype(BF16)
    return _dot(hi, u) + _dot(lo, u)


def _sb_logs(z):
    ls = jnp.minimum(z, 0.0) - jnp.log(1.0 + jnp.exp(-jnp.abs(z)))
    return ls, ls - z


def _sb_head_q(qb, first, hh):
    keep = first if hh == 0 else jnp.logical_not(first)
    return jnp.where(keep, qb, jnp.zeros_like(qb)) * jnp.asarray(SB_SCALE, qb.dtype)


def _sb_fwd_call(q, k, v, name):
    s, width = q.shape
    nb = SB_FWD_BLOCKS
    chains = [(b, hh) for b in range(nb) for hh in range(HEAD_PAD // SB_HEAD)]

    def body(q_ref, k_ref, v_ref, o_ref):
        qi = pl.program_id(1)
        strict = _tri(lambda r, c: c < r)
        after = _tri(lambda r, c: r > c).astype(BF16)
        first = _lane((1, HEAD_PAD)) < SB_HEAD
        qhs = [_sb_head_q(q_ref[:, _lanes(b)], first, hh) for b, hh in chains]

        def step(kb, carry, diag):
            ks = pl.multiple_of(kb * TQ, TQ)
            ids = range(len(chains))
            zs = [_dot_nt(qhs[ci], k_ref[pl.ds(ks, TQ), _lanes(chains[ci][0])]) for ci in ids]
            logs = [_sb_logs(z) for z in zs]
            lss = [lg[0] for lg in logs]
            l1ms = [jnp.where(strict, lg[1], 0.0) if diag else lg[1] for lg in logs]
            sufs = [_dot_hilo(l1m, after) for l1m in l1ms]
            as_ = [jnp.exp(lss[ci] + sufs[ci] + carry[ci][0]) for ci in ids]
            if diag:
                as_ = [jnp.where(strict, a, 0.0) for a in as_]
            accs = [carry[ci][1] + _dot(as_[ci].astype(BF16), v_ref[pl.ds(ks, TQ), _lanes(chains[ci][0])]) for ci in ids]
            return tuple((carry[ci][0] + jnp.sum(l1ms[ci], axis=-1, keepdims=True), accs[ci]) for ci in ids)

        init = tuple((jnp.zeros((TQ, 1), F32), jnp.zeros((TQ, HEAD_PAD), F32)) for _ in chains)
        carry = _sb_sweep(step, qi, init)
        for b in range(nb):
            o_ref[:, _lanes(b)] = jnp.where(first, carry[2 * b][1], carry[2 * b + 1][1])

    qspec, kspec = _attn_specs(s, nb)
    return pl.pallas_call(
        body, name=name, grid=(width // (nb * HEAD_PAD), s // TQ),
        in_specs=[qspec, kspec, kspec], out_specs=qspec, out_shape=jax.ShapeDtypeStruct((s, width), F32),
        compiler_params=_cp(("parallel", "arbitrary")))(q, k, v)


def _sb_sweep(step, qi, init):
    def live(carry):
        top = carry[0][0]
        for c in carry[1:]:
            top = jnp.maximum(top, c[0])
        return jnp.max(top)

    carry = step(qi, init, True)

    def cond(state):
        j, alive, _ = state
        return jnp.logical_and(j < qi, alive > SB_DEAD)

    def body(state):
        j, _, carry = state
        carry = step(qi - 1 - j, carry, False)
        return j + 1, live(carry), carry

    return lax.while_loop(cond, body, (jnp.int32(0), live(carry), carry))[2]


def _sb_bwd_call(q, k, v, do, o, name):
    s, width = q.shape
    nb = SB_BWD_BLOCKS
    chains = [(b, hh) for b in range(nb) for hh in range(HEAD_PAD // SB_HEAD)]

    def body(q_ref, k_ref, v_ref, do_ref, o_ref, dq_ref, dk_ref, dv_ref):
        qi = pl.program_id(1)

        @pl.when(qi == 0)
        def _():
            dk_ref[...] = jnp.zeros_like(dk_ref)
            dv_ref[...] = jnp.zeros_like(dv_ref)

        strict = _tri(lambda r, c: c < r)
        after = _tri(lambda r, c: r > c).astype(BF16)
        from_here = _tri(lambda r, c: r >= c).astype(BF16)
        first = _lane((1, HEAD_PAD)) < SB_HEAD
        qhs = [_sb_head_q(q_ref[:, _lanes(b)], first, hh) for b, hh in chains]
        dohs = []
        for b, hh in chains:
            dob = do_ref[:, _lanes(b)]
            dohs.append(jnp.where(first if hh == 0 else jnp.logical_not(first), dob, jnp.zeros_like(dob)))
        gtots = [jnp.sum(dohs[ci].astype(F32) * o_ref[:, _lanes(chains[ci][0])], axis=-1, keepdims=True)
                 for ci in range(len(chains))]

        def step(kb, carry, diag):
            ks = pl.multiple_of(kb * TQ, TQ)
            ids = range(len(chains))
            kts = [k_ref[pl.ds(ks, TQ), _lanes(b)] for b, _ in chains]
            zs = [_dot_nt(qhs[ci], kts[ci]) for ci in ids]
            das = [_dot_nt(dohs[ci], v_ref[pl.ds(ks, TQ), _lanes(chains[ci][0])]) for ci in ids]
            logs = [_sb_logs(z) for z in zs]
            lss = [lg[0] for lg in logs]
            l1ms = [jnp.where(strict, lg[1], 0.0) if diag else lg[1] for lg in logs]
            sufs = [_dot_hilo(l1m, after) for l1m in l1ms]
            as_ = [jnp.exp(lss[ci] + sufs[ci] + carry[ci][0]) for ci in ids]
            if diag:
                as_ = [jnp.where(strict, a, 0.0) for a in as_]
            abs_ = [a.astype(BF16) for a in as_]
            gs = [abs_[ci].astype(F32) * das[ci] for ci in ids]
            cexs = [gtots[ci] - (carry[ci][1] + _dot_hilo(gs[ci], from_here)) for ci in ids]
            dzs = [gs[ci] - jnp.exp(lss[ci]) * (gs[ci] + cexs[ci]) for ci in ids]
            if diag:
                dzs = [jnp.where(strict, dz, 0.0) for dz in dzs]
            dzbs = [dz.astype(BF16) for dz in dzs]
            dvps = [_dot_tn(abs_[ci], dohs[ci]) for ci in ids]
            dkps = [_dot_tn(dzbs[ci], qhs[ci]) for ci in ids]
            out = tuple((carry[ci][0] + jnp.sum(l1ms[ci], axis=-1, keepdims=True),
                         carry[ci][1] + jnp.sum(gs[ci], axis=-1, keepdims=True),
                         carry[ci][2] + _dot(dzbs[ci], kts[ci])) for ci in ids)
            for b in range(nb):
                dk_ref[pl.ds(ks, TQ), _lanes(b)] += dkps[2 * b] + dkps[2 * b + 1]
                dv_ref[pl.ds(ks, TQ), _lanes(b)] += dvps[2 * b] + dvps[2 * b + 1]
            return out

        init = tuple((jnp.zeros((TQ, 1), F32), jnp.zeros((TQ, 1), F32), jnp.zeros((TQ, HEAD_PAD), F32)) for _ in chains)
        carry = _sb_sweep(step, qi, init)
        for b in range(nb):
            dq_ref[:, _lanes(b)] = (jnp.where(first, carry[2 * b][2], carry[2 * b + 1][2]) * SB_SCALE).astype(BF16)

    qspec, kspec = _attn_specs(s, nb)
    return pl.pallas_call(
        body, name=name, grid=(width // (nb * HEAD_PAD), s // TQ),
        in_specs=[qspec, kspec, kspec, qspec, qspec], out_specs=[qspec, kspec, kspec],
        out_shape=[jax.ShapeDtypeStruct((s, width), BF16), jax.ShapeDtypeStruct((s, width), F32),
                   jax.ShapeDtypeStruct((s, width), F32)],
        compiler_params=_cp(("parallel", "arbitrary")))(q, k, v, do, o)


def _merge_out_call(om, osb, gates, h, wbm, wbs, wo, gain, name):
    s, d = h.shape
    tm = min(TM_SMALL, s)

    def body(om_ref, os_ref, g_ref, h_ref, wbm_ref, wbs_ref, wo_ref, gain_ref, h2_ref, bm_ref, bs_ref, mg_ref, u_ref):
        bm = _dot(om_ref[...], wbm_ref[...])
        bs = _dot(os_ref[...].astype(BF16), wbs_ref[...])
        mg = (_sigmoid(g_ref[:, :d]) * bm + _sigmoid(g_ref[:, d:]) * bs).astype(BF16)
        bm_ref[...] = bm
        bs_ref[...] = bs
        mg_ref[...] = mg
        x = h_ref[...] + _dot(mg, wo_ref[...])
        h2_ref[...] = x
        u_ref[...] = ((x * _rstd(x, d)) * gain_ref[...]).astype(BF16)

    return pl.pallas_call(
        body, name=name, grid=(s // tm,),
        in_specs=[_rows(tm, om.shape[1]), _rows(tm, SB_WIDTH), _rows(tm, 2 * d), _rows(tm, d),
                  _whole(wbm.shape), _whole(wbs.shape), _whole(wo.shape), _whole((1, d))],
        out_specs=[_rows(tm, d)] * 5,
        out_shape=[jax.ShapeDtypeStruct((s, d), F32), jax.ShapeDtypeStruct((s, d), F32),
                   jax.ShapeDtypeStruct((s, d), F32), jax.ShapeDtypeStruct((s, d), BF16),
                   jax.ShapeDtypeStruct((s, d), BF16)],
        compiler_params=_cp(("parallel",)))(om, osb, gates, h, wbm, wbs, wo, gain)


def _ple_call(h, g, wg, p, wp, tgt, name):
    s, d = h.shape
    tm = min(TM_SMALL, s)

    def body(h_ref, g_ref, wg_ref, p_ref, wp_ref, t_ref, dh_ref, dhs_ref, un_ref, dgl_ref, dpp_ref, dg_ref, sq_ref):
        @pl.when(pl.program_id(0) == 0)
        def _():
            dg_ref[...] = jnp.zeros_like(dg_ref)
            sq_ref[...] = jnp.zeros_like(sq_ref)

        x = h_ref[...]
        gain = g_ref[...]
        r = _rstd(x, d)
        xh = x * r
        un = (xh * gain).astype(BF16)
        sg = _sigmoid(_dot(un, wg_ref[...]))
        pp = _dot(p_ref[...].astype(BF16), wp_ref[...])
        diff = (x + sg * pp) - t_ref[...]
        sq_ref[...] += jnp.sum(diff * diff, axis=0, keepdims=True)
        dy = diff * (1.0 / d)
        dgl = ((dy * pp) * (sg * (1.0 - sg))).astype(BF16)
        dun = _dot_nt(dgl, wg_ref[...])
        dg_ref[...] += jnp.sum(dun * xh, axis=0, keepdims=True)
        dh = dy + _rms_bwd(x, r, gain, dun, d)
        dh_ref[...] = dh
        dhs_ref[...] = (0.5 * dh).astype(BF16)
        un_ref[...] = un
        dgl_ref[...] = dgl
        dpp_ref[...] = (dy * sg).astype(BF16)

    bf = jax.ShapeDtypeStruct((s, d), BF16)
    vec = jax.ShapeDtypeStruct((1, d), F32)
    return pl.pallas_call(
        body, name=name, grid=(s // tm,),
        in_specs=[_rows(tm, d), _whole((1, d)), _whole(wg.shape), _rows(tm, PLE_DIM), _whole(wp.shape), _rows(tm, d)],
        out_specs=[_rows(tm, d)] * 5 + [_whole((1, d))] * 2,
        out_shape=[jax.ShapeDtypeStruct((s, d), F32), bf, bf, bf, bf, vec, vec],
        compiler_params=_cp(("arbitrary",)))(h, g, wg, p, wp, tgt)


def _ffn_bwd_a_call(dhs, a, b, wo, name, rider=None):
    s, n = a.shape
    d = dhs.shape[1]
    tn = n // 2
    tm = min(TM, s)

    def body(dh_ref, a_ref, b_ref, wo_ref, da_ref, db_ref):
        dh = dh_ref[...]
        chunks = [slice(c0, min(c0 + COL_CHUNK, tn)) for c0 in range(0, tn, COL_CHUNK)]
        dhms = [_dot_nt(dh, wo_ref[sl, :]) for sl in chunks]
        for sl, dhm in zip(chunks, dhms):
            av = a_ref[:, sl]
            sa = _sigmoid(av)
            da_ref[:, sl] = (dhm * b_ref[:, sl] * (sa * (1.0 + av * (1.0 - sa)))).astype(BF16)
            db_ref[:, sl] = (dhm * (av * sa)).astype(BF16)

    blk = pl.BlockSpec((tm, tn), lambda j, i: (i, j))
    return _with_rider(
        body, rider, name=name, grid=(n // tn, s // tm),
        in_specs=[pl.BlockSpec((tm, d), lambda j, i: (i, 0)), blk, blk, pl.BlockSpec((tn, d), lambda j, i: (j, 0))],
        out_specs=[blk, blk],
        out_shape=[jax.ShapeDtypeStruct((s, n), BF16)] * 2, args=(dhs, a, b, wo), sem=("parallel", "parallel"))


def _norm_bwd_call(dy_list, w_list, h, g, dh_in, name, half_out, rider=None, w_transposed=False):
    s, d = h.shape
    tm = min(TM_SMALL, s)
    nk, nw = len(dy_list), len(w_list)
    factor = 0.5 if half_out else 1.0
    sharded = nw == 1 and w_list[0].ndim == 3

    def body(*refs):
        dy_refs = refs[:nk]
        w_refs = refs[nk:nk + nw]
        h_ref, g_ref, dhin_ref, dh_ref, dhb_ref, dg_ref = refs[nk + nw:]

        @pl.when(pl.program_id(0) == 0)
        def _():
            dg_ref[...] = jnp.zeros_like(dg_ref)

        if sharded:
            c = w_list[0].shape[2]
            per = dy_list[0].shape[1] // c
            du = None
            for k in range(w_list[0].shape[0]):
                part = _dot_nt(dy_refs[k // per][:, (k % per) * c:(k % per + 1) * c], w_refs[0][k])
                du = part if du is None else du + part
        else:
            mm = _dot if w_transposed else _dot_nt
            du = mm(dy_refs[0][...], w_refs[0][...])
            for dy_ref, w_ref in zip(dy_refs[1:], w_refs[1:]):
                du = du + mm(dy_ref[...], w_ref[...])
        x = h_ref[...]
        r = _rstd(x, d)
        dg_ref[...] += jnp.sum(du * (x * r), axis=0, keepdims=True)
        dh = dhin_ref[...] + _rms_bwd(x, r, g_ref[...], du, d)
        dh_ref[...] = dh
        dhb_ref[...] = (factor * dh).astype(BF16)

    outs, got = _with_rider(
        body, rider, name=name, grid=(s // tm,),
        in_specs=[_rows(tm, dy.shape[1]) for dy in dy_list] + [_whole(w.shape) for w in w_list]
        + [_rows(tm, d), _whole((1, d)), _rows(tm, d)],
        out_specs=[_rows(tm, d), _rows(tm, d), _whole((1, d))],
        out_shape=[jax.ShapeDtypeStruct((s, d), F32), jax.ShapeDtypeStruct((s, d), BF16),
                   jax.ShapeDtypeStruct((1, d), F32)],
        args=(*dy_list, *w_list, h, g, dh_in), sem=("arbitrary",))
    return outs if rider is None else (outs, got)


def _merge_bwd_call(dhb, gates, bm, bs, wo, wbm, wbs, name):
    s, d = bm.shape
    tm = min(TM_SMALL, s)

    def body(dh_ref, g_ref, bm_ref, bs_ref, wo_ref, wbm_ref, wbs_ref, dg_ref, dbm_ref, dbs_ref, dom_ref, dos_ref):
        halves = [slice(k * (tm // 2), (k + 1) * (tm // 2)) for k in range(2)]
        dmgs = [_dot_nt(dh_ref[rows, :], wo_ref[...]) for rows in halves]
        dbms, dbss = [], []
        for rows, dmg in zip(halves, dmgs):
            s1 = _sigmoid(g_ref[rows, :d])
            s2 = _sigmoid(g_ref[rows, d:])
            dg_ref[rows, :d] = (dmg * bm_ref[rows, :] * (s1 * (1.0 - s1))).astype(BF16)
            dg_ref[rows, d:] = (dmg * bs_ref[rows, :] * (s2 * (1.0 - s2))).astype(BF16)
            dbms.append((dmg * s1).astype(BF16))
            dbss.append((dmg * s2).astype(BF16))
        doms = [_dot_nt(dbm, wbm_ref[...]) for dbm in dbms]
        doss = [_dot_nt(dbs, wbs_ref[...]) for dbs in dbss]
        for k, rows in enumerate(halves):
            dbm_ref[rows, :] = dbms[k]
            dbs_ref[rows, :] = dbss[k]
            dom_ref[rows, :] = doms[k].astype(BF16)
            dos_ref[rows, :] = doss[k].astype(BF16)

    wm = wbm.shape[0]
    return pl.pallas_call(
        body, name=name, grid=(s // tm,),
        in_specs=[_rows(tm, d), _rows(tm, 2 * d), _rows(tm, d), _rows(tm, d),
                  _whole(wo.shape), _whole(wbm.shape), _whole(wbs.shape)],
        out_specs=[_rows(tm, 2 * d), _rows(tm, d), _rows(tm, d), _rows(tm, wm), _rows(tm, SB_WIDTH)],
        out_shape=[jax.ShapeDtypeStruct((s, 2 * d), BF16), jax.ShapeDtypeStruct((s, d), BF16),
                   jax.ShapeDtypeStruct((s, d), BF16), jax.ShapeDtypeStruct((s, wm), BF16),
                   jax.ShapeDtypeStruct((s, SB_WIDTH), BF16)],
        compiler_params=_cp(("parallel",)))(dhb, gates, bm, bs, wo, wbm, wbs)


def _mla_prep_bwd_call(cq, ckv, krope, pos, freq, sign, g_ql, g_kvl, g_qh, g_kh, wq, wkv, dq, dk, dv, name):
    s = cq.shape[0]
    tm = min(TM_PREP_BWD, s)
    width = HEADS * HEAD_PAD

    def body(cq_ref, ckv_ref, kr_ref, pos_ref, freq_ref, sign_ref, gql_ref, gkvl_ref, gqh_ref, gkh_ref,
             wq_ref, wkv_ref, dq_ref, dk_ref, dv_ref,
             dcq_ref, dckv_ref, dkr_ref, dwq_ref, dwkv_ref, dgql_ref, dgkvl_ref, dgqh_ref, dgkh_ref, dqr_ref, dkv_ref):
        @pl.when(pl.program_id(0) == 0)
        def _():
            for ref in (dwq_ref, dwkv_ref, dgql_ref, dgkvl_ref, dgqh_ref, dgkh_ref):
                ref[...] = jnp.zeros_like(ref)

        cosv, ssv = _rope_tables(pos_ref, freq_ref, sign_ref)
        xq = cq_ref[...]
        rq = _rstd(xq, Q_LORA)
        cqn = ((xq * rq) * gql_ref[...]).astype(BF16)
        qr = _dot(cqn, wq_ref[...])
        xk = ckv_ref[...]
        rk = _rstd(xk, KV_LORA)
        ckvn = ((xk * rk) * gkvl_ref[...]).astype(BF16)
        kv = _dot(ckvn, wkv_ref[...])
        kr = kr_ref[...]
        lane = _lane((tm, HEAD_PAD))
        dkr = jnp.zeros((tm, HEAD_PAD), F32)
        dgqh = jnp.zeros((1, HEAD_PAD), F32)
        dgkh = jnp.zeros((1, HEAD_PAD), F32)
        for h in range(HEADS):
            sl = slice(h * HEAD_PAD, (h + 1) * HEAD_PAD)
            x = qr[:, sl]
            dx, dgh = _head_bwd(x, _rstd(x, MLA_QK), gqh_ref[...], cosv, ssv, dq_ref[:, sl])
            dqr_ref[:, sl] = dx.astype(BF16)
            dgqh = dgqh + dgh
            x = jnp.where(lane < MLA_NOPE, kv[:, sl], kr)
            dx, dgh = _head_bwd(x, _rstd(x, MLA_QK), gkh_ref[...], cosv, ssv, dk_ref[:, sl])
            dgkh = dgkh + dgh
            dkr = dkr + jnp.where(lane >= MLA_NOPE, dx, 0.0)
            dkv_ref[:, sl] = jnp.where(lane < MLA_NOPE, dx, dv_ref[:, sl]).astype(BF16)
        dgqh_ref[...] += dgqh
        dgkh_ref[...] += dgkh
        dkr_ref[...] = dkr.astype(BF16)
        dqr = dqr_ref[...]
        dkvb = dkv_ref[...]
        dwq_ref[...] += _dot_tn(cqn, dqr)
        dwkv_ref[...] += _dot_tn(ckvn, dkvb)
        dcqn = _dot_nt(dqr, wq_ref[...])
        dgql_ref[...] += jnp.sum(dcqn * (xq * rq), axis=0, keepdims=True)
        dcq_ref[...] = _rms_bwd(xq, rq, gql_ref[...], dcqn, Q_LORA).astype(BF16)
        dckvn = _dot_nt(dkvb, wkv_ref[...])
        dgkvl_ref[...] += jnp.sum(dckvn * (xk * rk), axis=0, keepdims=True)
        dckv_ref[...] = _rms_bwd(xk, rk, gkvl_ref[...], dckvn, KV_LORA).astype(BF16)

    vec = lambda n: jax.ShapeDtypeStruct((1, n), F32)
    outs = pl.pallas_call(
        body, name=name, grid=(s // tm,),
        in_specs=[_rows(tm, Q_LORA), _rows(tm, KV_LORA), _rows(tm, HEAD_PAD), _rows(tm, 1),
                  _whole((1, HEAD_PAD)), _whole((1, HEAD_PAD)), _whole((1, Q_LORA)), _whole((1, KV_LORA)),
                  _whole((1, HEAD_PAD)), _whole((1, HEAD_PAD)), _whole((Q_LORA, width)), _whole((KV_LORA, width)),
                  _rows(tm, width), _rows(tm, width), _rows(tm, width)],
        out_specs=[_rows(tm, Q_LORA), _rows(tm, KV_LORA), _rows(tm, HEAD_PAD), _whole((Q_LORA, width)),
                   _whole((KV_LORA, width)), _whole((1, Q_LORA)), _whole((1, KV_LORA)), _whole((1, HEAD_PAD)),
                   _whole((1, HEAD_PAD)), _rows(tm, width), _rows(tm, width)],
        out_shape=[jax.ShapeDtypeStruct((s, Q_LORA), BF16), jax.ShapeDtypeStruct((s, KV_LORA), BF16),
                   jax.ShapeDtypeStruct((s, HEAD_PAD), BF16), jax.ShapeDtypeStruct((Q_LORA, width), F32),
                   jax.ShapeDtypeStruct((KV_LORA, width), F32), vec(Q_LORA), vec(KV_LORA), vec(HEAD_PAD), vec(HEAD_PAD),
                   jax.ShapeDtypeStruct((s, width), BF16), jax.ShapeDtypeStruct((s, width), BF16)],
        compiler_params=_cp(("arbitrary",)))(cq, ckv, krope, pos, freq, sign, g_ql, g_kvl, g_qh, g_kh, wq, wkv, dq, dk, dv)
    return outs[:9]


def _tn_call(a, b, name, shard_cols=None, rider=None):
    s, ka = a.shape
    nb = b.shape[1]
    ti = _pick(ka, (512, 256, 128))
    if shard_cols is not None:
        tj = shard_cols
    else:
        tj = nb if nb <= TN_MAX_COLS else _pick(nb, (2176, 1024, 512, 256, 128))
    ts = s if 2 * s * (ti + tj) * a.dtype.itemsize <= TN_OPERAND_BYTES else s // 2
    ns = s // ts

    def body(a_ref, b_ref, o_ref, acc_ref):
        part = _dot_tn(a_ref[...].astype(BF16), b_ref[...].astype(BF16))
        if ns == 1:
            o_ref[...] = part.astype(o_ref.dtype)
            return

        @pl.when(pl.program_id(2) == 0)
        def _():
            acc_ref[...] = part

        @pl.when(pl.program_id(2) != 0)
        def _():
            acc_ref[...] += part

        @pl.when(pl.program_id(2) == ns - 1)
        def _():
            o_ref[...] = acc_ref[...].astype(o_ref.dtype)

    if shard_cols is None:
        out_spec = pl.BlockSpec((ti, tj), lambda i, j, t: (i, j))
        out_shape = jax.ShapeDtypeStruct((ka, nb), BF16)
    else:
        out_spec = pl.BlockSpec((None, ti, tj), lambda i, j, t: (j, i, 0))
        out_shape = jax.ShapeDtypeStruct((nb // tj, ka, tj), BF16)
    (out,), got = _with_rider(
        body, rider, name=name, grid=(ka // ti, nb // tj, ns),
        in_specs=[pl.BlockSpec((ts, ti), lambda i, j, t: (t, i)), pl.BlockSpec((ts, tj), lambda i, j, t: (t, j))],
        out_specs=[out_spec], out_shape=[out_shape], scratch=[pltpu.VMEM((ti, tj), F32)], args=(a, b),
        sem=("parallel", "parallel", "arbitrary"))
    return out if rider is None else (out, got)


def _sum_call(parts, out_dtype, name):
    n, r, w = parts.shape
    tr = _row_tile(r)

    def body(p_ref, o_ref):
        acc = p_ref[0].astype(F32)
        for k in range(1, n):
            acc = acc + p_ref[k].astype(F32)
        o_ref[...] = acc.astype(out_dtype)

    return pl.pallas_call(
        body, name=name, grid=(r // tr,),
        in_specs=[pl.BlockSpec((n, tr, w), lambda i: (0, i, 0))], out_specs=_rows(tr, w),
        out_shape=jax.ShapeDtypeStruct((r, w), out_dtype), compiler_params=_cp(("parallel",)))(parts)


def _chip_sum_call(by_chip, core, name):
    n, r, w = by_chip.shape
    tr = _row_tile(r)
    nblk = r // tr

    def body(c_ref, p_ref, o_ref):
        acc = p_ref[0].astype(F32)
        for k in range(1, n):
            acc = acc + p_ref[k].astype(F32)
        o_ref[...] = acc

    return pl.pallas_call(
        body, name=name,
        grid_spec=pltpu.PrefetchScalarGridSpec(
            num_scalar_prefetch=1, grid=(nblk,),
            in_specs=[pl.BlockSpec((n, tr, w), lambda i, c_ref: (0, i, 0))],
            out_specs=pl.BlockSpec((tr, w), lambda i, c_ref: (c_ref[0] * nblk + i, 0))),
        out_shape=jax.ShapeDtypeStruct((2 * r, w), F32),
        compiler_params=_cp(("parallel",)))(core.reshape(1).astype(jnp.int32), by_chip)


def _pair_sum_call(full, other, core, out_dtype, name):
    n, r, w = other.shape
    tr = _row_tile(r)
    nblk = r // tr

    def body(c_ref, a_ref, b_ref, o_ref):
        o_ref[...] = (a_ref[...].astype(F32) + b_ref[...].astype(F32)).astype(out_dtype)

    spec = pl.BlockSpec((None, tr, w), lambda k, i, c_ref: (k, i, 0))
    return pl.pallas_call(
        body, name=name,
        grid_spec=pltpu.PrefetchScalarGridSpec(
            num_scalar_prefetch=1, grid=(n, nblk),
            in_specs=[pl.BlockSpec((None, tr, w), lambda k, i, c_ref: (k, c_ref[0] * nblk + i, 0)), spec],
            out_specs=spec),
        out_shape=jax.ShapeDtypeStruct((n, r, w), out_dtype),
        compiler_params=_cp(("parallel", "parallel")))(core.reshape(1).astype(jnp.int32), full, other)


def _adamw_call(w, g, row0, m, v, name):
    r, c = w.shape
    span = math.gcd(r, row0) if row0 else r
    tr = next((t for t in range(min(span, 256) // 8 * 8, 0, -8) if span % t == 0), span)
    off = row0 // tr

    def body(w_ref, g_ref, m_ref, v_ref, g_out_ref, d_ref, nm_ref, nv_ref):
        gg = g_ref[...]
        g_out_ref[...] = gg
        nm = ADAM_B1 * m_ref[...] + (1.0 - ADAM_B1) * gg
        nv = ADAM_B2 * v_ref[...] + (1.0 - ADAM_B2) * (gg * gg)
        m_hat = nm / (1.0 - ADAM_B1 ** ADAM_STEP)
        v_hat = nv / (1.0 - ADAM_B2 ** ADAM_STEP)
        d_ref[...] = -ADAM_LR * (m_hat / (jnp.sqrt(v_hat) + ADAM_EPS) + ADAM_WD * w_ref[...])
        nm_ref[...] = nm
        nv_ref[...] = nv

    out = jax.ShapeDtypeStruct((r, c), F32)
    g_spec = pl.BlockSpec((tr, c), lambda i: (off + i, 0))
    return pl.pallas_call(
        body, name=name, grid=(r // tr,), in_specs=[_rows(tr, c), g_spec, _rows(tr, c), _rows(tr, c)],
        out_specs=[_rows(tr, c)] * 4, out_shape=[out, out, out, out], compiler_params=_cp(("parallel",)))(w, g, m, v)


def _position():
    x, y, c = lax.axis_index("x"), lax.axis_index("y"), lax.axis_index("c")
    chips = [(1 - x, y), (x, 1 - y), (1 - x, 1 - y)]
    return x, y, c, chips


def _gather_rider(parts):
    n = len(parts)
    pairs = [(j, k) for j in range(3) for k in range(n)]

    def piece(out_refs, k, chip, core):
        half = parts[k].shape[0] // 2
        return out_refs[k].at[2 * chip[0] + chip[1], pl.ds(core * half, half), :]

    def over_ici(in_refs, out_refs, sems, j, k):
        x, y, c, chips = _position()
        half = parts[k].shape[0] // 2
        return pltpu.make_async_remote_copy(
            src_ref=in_refs[k].at[pl.ds(c * half, half), :], dst_ref=piece(out_refs, k, (x, y), c),
            send_sem=sems[0].at[n * j + k], recv_sem=sems[1].at[n * j + k], device_id=(*chips[j], c), device_id_type=MESH)

    def to_sibling(out_refs, sems, j, k):
        x, y, c, chips = _position()
        landed = piece(out_refs, k, chips[j], c)
        return pltpu.make_async_remote_copy(
            src_ref=landed, dst_ref=landed, send_sem=sems[2].at[n * j + k], recv_sem=sems[3].at[n * j + k],
            device_id=(x, y, 1 - c), device_id_type=MESH)

    def start(in_refs, out_refs, sems):
        for j, k in pairs:
            over_ici(in_refs, out_refs, sems, j, k).start()

    def relay(in_refs, out_refs, sems):
        for j, k in pairs:
            over_ici(in_refs, out_refs, sems, j, k).wait_recv()
            to_sibling(out_refs, sems, j, k).start()

    def finish(in_refs, out_refs, sems):
        for j, k in pairs:
            to_sibling(out_refs, sems, j, k).wait_recv()
        for j, k in pairs:
            over_ici(in_refs, out_refs, sems, j, k).wait_send()
            to_sibling(out_refs, sems, j, k).wait_send()

    return _Rider(list(parts), [jax.ShapeDtypeStruct((N_CHIPS,) + p.shape, p.dtype) for p in parts], [3 * n] * 4,
                  start, relay, finish)


def _scatter_rider(parts):
    n = len(parts)
    pairs = [(j, k) for j in range(3) for k in range(n)]

    def copy(in_refs, out_refs, sems, j, k):
        x, y, c, chips = _position()
        return pltpu.make_async_remote_copy(
            src_ref=in_refs[k].at[2 * chips[j][0] + chips[j][1]], dst_ref=out_refs[k].at[2 * x + y],
            send_sem=sems[0].at[n * j + k], recv_sem=sems[1].at[n * j + k], device_id=(*chips[j], c), device_id_type=MESH)

    def start(in_refs, out_refs, sems):
        for j, k in pairs:
            copy(in_refs, out_refs, sems, j, k).start()

    def finish(in_refs, out_refs, sems):
        for j, k in pairs:
            copy(in_refs, out_refs, sems, j, k).wait()

    return _Rider(list(parts), [jax.ShapeDtypeStruct(p.shape, p.dtype) for p in parts], [3 * n] * 2, start,
                  lambda in_refs, out_refs, sems: None, finish)


def _pair_send_call(parts, name):
    n = len(parts)

    def body(*refs):
        in_refs, out_refs = refs[:n], refs[n:2 * n]
        send_sems, recv_sems = refs[2 * n:]
        x, y, c, _ = _position()
        copies = []
        for k in range(n):
            half = parts[k].shape[1] // 2
            cp = pltpu.make_async_remote_copy(
                src_ref=in_refs[k].at[:, pl.ds((1 - c) * half, half), :], dst_ref=out_refs[k],
                send_sem=send_sems.at[k], recv_sem=recv_sems.at[k], device_id=(x, y, 1 - c), device_id_type=MESH)
            cp.start()
            copies.append(cp)
        for cp in copies:
            cp.wait()

    sems = pltpu.SemaphoreType.DMA((n,))
    return pl.pallas_call(
        body, name=name, in_specs=[HBM] * n, out_specs=[HBM] * n,
        out_shape=[jax.ShapeDtypeStruct((p.shape[0], p.shape[1] // 2, p.shape[2]), p.dtype) for p in parts],
        scratch_shapes=[sems, sems])(*parts)


def _pair_swap_call(parts, name):
    n = len(parts)

    def body(*refs):
        out_refs = refs[n:2 * n]
        send_sems, recv_sems = refs[2 * n:]
        x, y, c, _ = _position()
        copies = []
        for k in range(n):
            half = parts[k].shape[0] // 2
            mine = out_refs[k].at[pl.ds(c * half, half), :]
            cp = pltpu.make_async_remote_copy(
                src_ref=mine, dst_ref=mine, send_sem=send_sems.at[k], recv_sem=recv_sems.at[k],
                device_id=(x, y, 1 - c), device_id_type=MESH)
            cp.start()
            copies.append(cp)
        for cp in copies:
            cp.wait()

    sems = pltpu.SemaphoreType.DMA((n,))
    return pl.pallas_call(
        body, name=name, in_specs=[HBM] * n, out_specs=[HBM] * n,
        out_shape=[jax.ShapeDtypeStruct(p.shape, p.dtype) for p in parts],
        input_output_aliases={k: k for k in range(n)},
        scratch_shapes=[sems, sems])(*parts)


def _all_gather_small_call(block, name):
    r, w = block.shape

    def body(in_ref, out_ref, send_sems, recv_sems, local_sem):
        x, y, c, _ = _position()
        me = 4 * x + 2 * y + c
        own = pltpu.make_async_copy(in_ref, out_ref.at[me], local_sem)
        own.start()
        copies = []
        for k in range(1, 8):
            peer = (x ^ (k >> 2), y ^ ((k >> 1) & 1), c ^ (k & 1))
            cp = pltpu.make_async_remote_copy(
                src_ref=in_ref, dst_ref=out_ref.at[me], send_sem=send_sems.at[k - 1], recv_sem=recv_sems.at[k - 1],
                device_id=peer, device_id_type=MESH)
            cp.start()
            copies.append(cp)
        for cp in copies:
            cp.wait()
        own.wait()

    return pl.pallas_call(
        body, name=name, in_specs=[HBM], out_specs=HBM,
        out_shape=jax.ShapeDtypeStruct((8, r, w), block.dtype),
        scratch_shapes=[pltpu.SemaphoreType.DMA((7,)), pltpu.SemaphoreType.DMA((7,)), pltpu.SemaphoreType.DMA])(block)


BIG = {
    "ffn1_w_in": ((D_MODEL, 2 * D_FF), 1), "ffn1_w_out": ((D_FF, D_MODEL), 0),
    "w_in": ((D_MODEL, 4256), 1), "w_q_up": ((Q_LORA, HEADS * MLA_QK), 1), "w_kv_up": ((KV_LORA, 1024), 1),
    "w_branch_mla": ((512, D_MODEL), 1), "w_branch_sb": ((SB_WIDTH, D_MODEL), 1), "w_out": ((D_MODEL, D_MODEL), 0),
    "ffn2_w_in": ((D_MODEL, 2 * D_FF), 1), "ffn2_w_out": ((D_FF, D_MODEL), 0),
    "w_ple_gate": ((D_MODEL, D_MODEL), 0), "w_ple_proj": ((PLE_DIM, D_MODEL), 1),
}
GAINS = {"ffn1_norm": 1024, "mix_norm": 1024, "q_latent_norm": 384, "kv_latent_norm": 256, "q_head_norm": 96,
         "k_head_norm": 96, "ffn2_norm": 1024, "ple_norm": 1024}
WEIGHT_ORDER = ["ffn1_norm", "ffn1_w_in", "ffn1_w_out", "mix_norm", "w_in", "q_latent_norm", "w_q_up",
                "kv_latent_norm", "w_kv_up", "q_head_norm", "k_head_norm", "w_branch_mla", "w_branch_sb", "w_out",
                "ffn2_norm", "ffn2_w_in", "ffn2_w_out", "ple_norm", "w_ple_gate", "w_ple_proj"]


W_IN_SHARD_ROWS = 1088


def _shard_shape(name):
    (r, c), axis = BIG[name]
    if name in TRANSPOSED_UPDATE:
        return (W_IN_SHARD_ROWS, r)
    return (r // N_CHIPS, c) if axis == 0 else (r, c // N_CHIPS)


GATHER_GROUPS = [
    [("ffn1_w_in",)],
    [("ffn1_w_out",), ("w_in",)],
    [("w_out",), ("w_kv_up", "w_branch_mla", "w_branch_sb"), ("w_q_up",)],
    [("ffn2_w_in",), ("ffn2_w_out", "w_ple_gate"), ("w_ple_proj",)],
]
REDUCE_GROUPS = [
    [("ffn2_w_in",), ("ffn2_w_out", "w_out", "w_ple_gate"), ("w_branch_mla", "w_branch_sb", "w_ple_proj")],
    [("w_in",), ("w_kv_up",), ("w_q_up",)],
    [("ffn1_w_out",)],
    [("ffn1_w_in",)],
]


def _join_parts(shards, group):
    return [shards[part[0]] if len(part) == 1 else jnp.concatenate([shards[n] for n in part], axis=-2) for part in group]


def _part_rows(group):
    where = {}
    for k, part in enumerate(group):
        at = 0
        for n in part:
            where[n] = (k, at)
            at += _shard_shape(n)[0]
    return where


def _split_parts(parts, group):
    return {n: parts[k][..., at:at + _shard_shape(n)[0], :] for n, (k, at) in _part_rows(group).items()}


def _exchange_form(name, shard):
    if name in TRANSPOSED_UPDATE:
        t = shard.T.astype(BF16)
        return jnp.pad(t, ((0, W_IN_SHARD_ROWS - t.shape[0]), (0, 0)))
    return shard.astype(BF16)


def _to_shards(name, full):
    (r, c), axis = BIG[name]
    if axis == 0:
        return full.reshape(N_CHIPS, r // N_CHIPS, c)
    return full.reshape(r, N_CHIPS, c // N_CHIPS).transpose(1, 0, 2)


def _from_shards(name, shards):
    (r, c), axis = BIG[name]
    if axis == 0:
        return shards.reshape(r, c)
    return shards.transpose(1, 0, 2).reshape(r, c)


def _relayout_w_in(wt):
    d = wt.shape[1]
    z = lambda n: jnp.zeros((n, d), wt.dtype)
    return jnp.concatenate([wt[:640], z(MLA_NOPE), wt[640:672], z(HEAD_PAD - MLA_QK), wt[672:]], axis=0)


def _unlayout_w_in(gt):
    full = jnp.concatenate([gt[:640], gt[640 + MLA_NOPE:640 + MLA_QK], gt[768:]], axis=0)
    shards = full.reshape(N_CHIPS, -1, gt.shape[1])
    return jnp.pad(shards, ((0, 0), (0, W_IN_SHARD_ROWS - shards.shape[1]), (0, 0)))


def _pad_heads(v):
    lead = v.shape[:-1]
    return jnp.pad(v.reshape(lead + (HEADS, MLA_QK)), [(0, 0)] * len(lead) + [(0, 0), (0, HEAD_PAD - MLA_QK)]).reshape(
        lead + (HEADS * HEAD_PAD,))


SHARD_MAJOR = ("ffn1_w_in", "ffn2_w_in")
TRANSPOSED_UPDATE = ("w_in",)


def _step(x, p, pos, tgt, gains, weights, dist):
    d = D_MODEL
    full = {} if dist is not None else {
        n: _to_shards(n, w) if n in SHARD_MAJOR else (w.T if n in TRANSPOSED_UPDATE else w) for n, w in weights.items()}
    reduced = {}

    def gather_rider(g):
        if dist is None:
            return None, None
        mine = _join_parts(weights, GATHER_GROUPS[g])
        return mine, _gather_rider(mine)

    def gathered(g, mine, others):
        if dist is not None:
            parts = [lax.dynamic_update_slice_in_dim(o, m[None], dist[0], axis=0) for o, m in zip(others, mine)]
            for n, shards in _split_parts(parts, GATHER_GROUPS[g]).items():
                if n in TRANSPOSED_UPDATE:
                    (d_in, c_out), _ = BIG[n]
                    full[n] = shards[:, :c_out // N_CHIPS].reshape(c_out, d_in)
                else:
                    full[n] = shards if n in SHARD_MAJOR else _from_shards(n, shards)

    def reduce_before(g):
        if dist is None:
            return None, None
        group = REDUCE_GROUPS[g]
        shards = {n: grads[n] if grads[n].ndim == 3 else _to_shards(n, grads[n].astype(BF16)) for part in group for n in part}
        partial = _join_parts(shards, group)
        from_sibling = _pair_send_call(partial, "grads%d_pair_send" % g)
        pair_sum = [_pair_sum_call(a, b, dist[1], BF16, "grads%d_pair_sum_%d" % (g, k))
                    for k, (a, b) in enumerate(zip(partial, from_sibling))]
        return pair_sum, _scatter_rider(pair_sum)

    def reduce_after(g, pair_sum, by_chip):
        if dist is not None:
            chip, core = dist
            by_chip = [lax.dynamic_update_slice_in_dim(t, lax.dynamic_slice_in_dim(o, chip, 1, axis=0), chip, axis=0)
                       for t, o in zip(by_chip, pair_sum)]
            bufs = _pair_swap_call([_chip_sum_call(t, core, "grads%d_chip_sum_%d" % (g, k)) for k, t in enumerate(by_chip)],
                                   "grads%d_pair_swap" % g)
            for n, (k, row0) in _part_rows(REDUCE_GROUPS[g]).items():
                reduced[n] = (bufs[k], row0)

    mine, rider = gather_rider(0)
    u1, got = _norm_call(x, gains["ffn1_norm"], "norm_ffn1", rider)
    gathered(0, mine, got)
    wts = full
    inv_freq = ROPE_BASE ** (-jnp.arange(0, MLA_ROPE, 2, dtype=F32) / MLA_ROPE)
    zeros = lambda n: jnp.zeros((n,), F32)
    freq = jnp.concatenate([zeros(MLA_NOPE), inv_freq, inv_freq, zeros(HEAD_PAD - MLA_QK)])[None]
    sign = jnp.concatenate([zeros(MLA_NOPE), -jnp.ones((16,), F32), jnp.ones((16,), F32), zeros(HEAD_PAD - MLA_QK)])[None]
    pad_gain = lambda g: jnp.pad(g, ((0, 0), (0, HEAD_PAD - MLA_QK)))
    g_qh, g_kh = pad_gain(gains["q_head_norm"]), pad_gain(gains["k_head_norm"])

    mine, rider = gather_rider(1)
    (a1, b1, hm1), got = _ffn_in_call(u1, wts["ffn1_w_in"], "ffn1_in", rider)
    gathered(1, mine, got)
    mine, rider = gather_rider(2)
    (h1, um), got = _ffn_out_call(hm1, wts["ffn1_w_out"], x, gains["mix_norm"], "ffn1_out", rider)
    gathered(2, mine, got)
    w_in = _relayout_w_in(wts["w_in"])
    (cq, ckv, krope, sbq, sbk, sbv, gates), _ = _mix_in_call(um, w_in, "mix_in")
    wq = _pad_heads(wts["w_q_up"])
    wkv = wts["w_kv_up"]
    wbm = jnp.pad(wts["w_branch_mla"].reshape(HEADS, 64, d), ((0, 0), (64, 0), (0, 0))).reshape(HEADS * HEAD_PAD, d)
    wbs, wo = wts["w_branch_sb"], wts["w_out"]
    prep_args = (cq, ckv, krope, pos, freq, sign, gains["q_latent_norm"], gains["kv_latent_norm"], g_qh, g_kh, wq, wkv)
    q, k, v = _mla_prep_call(*prep_args, "mla_prep")
    mine, rider = gather_rider(3)
    (om, lse), got = _mla_fwd_call(q, k, v, "mla_fwd", rider)
    gathered(3, mine, got)
    osb = _sb_fwd_call(sbq, sbk, sbv, "sb_fwd")
    h2, bm, bs, mg, u2 = _merge_out_call(om, osb, gates, h1, wbm, wbs, wo, gains["ffn2_norm"], "merge_out")
    (a2, b2, hm2), _ = _ffn_in_call(u2, wts["ffn2_w_in"], "ffn2_in")
    (h3, _), _ = _ffn_out_call(hm2, wts["ffn2_w_out"], h2, gains["ple_norm"], "ffn2_out")

    grads, gg = {}, {}
    dh3, dh3s, un, dgl, dpp, gg["ple_norm"], sq = _ple_call(
        h3, gains["ple_norm"], wts["w_ple_gate"], p, wts["w_ple_proj"], tgt, "ple")
    grads["w_ple_gate"] = _tn_call(un, dgl, "dw_ple_gate")
    grads["w_ple_proj"] = _tn_call(p, dpp, "dw_ple_proj")

    (da2, db2), _ = _ffn_bwd_a_call(dh3s, a2, b2, wts["ffn2_w_out"], "ffn2_bwd_act")
    grads["ffn2_w_out"] = _tn_call(hm2, dh3s, "dw_ffn2_out")
    grads["ffn2_w_in"] = jnp.concatenate([_tn_call(u2, da2, "dw_ffn2_in_a", shard_cols=D_FF // 2),
                                          _tn_call(u2, db2, "dw_ffn2_in_b", shard_cols=D_FF // 2)], axis=0)
    dh2, dh2b, gg["ffn2_norm"] = _norm_bwd_call([da2, db2], [wts["ffn2_w_in"]], h2, gains["ffn2_norm"], dh3,
                                                "ffn2_bwd_norm", half_out=False)

    dgates, dbm, dbs, dom, dos = _merge_bwd_call(dh2b, gates, bm, bs, wo, wbm, wbs, "merge_bwd")
    grads["w_out"] = _tn_call(mg, dh2b, "dw_out")
    grads["w_branch_mla"] = _tn_call(om, dbm, "dw_branch_mla").reshape(HEADS, HEAD_PAD, d)[:, 64:, :].reshape(512, d)
    grads["w_branch_sb"] = _tn_call(osb, dbs, "dw_branch_sb")
    pair_sum, rider = reduce_before(0)
    (dq, dk, dv), got = _mla_bwd_call(q, k, v, om, dom, lse, "mla_bwd", rider)
    reduce_after(0, pair_sum, got)
    dsq, dsk, dsv = _sb_bwd_call(sbq, sbk, sbv, dos, osb, "sb_bwd")
    (dcq, dckv, dkr, dwq, grads["w_kv_up"], gg["q_latent_norm"], gg["kv_latent_norm"], dgqh, dgkh) = \
        _mla_prep_bwd_call(*prep_args, dq, dk, dv, "mla_prep_bwd")
    grads["w_q_up"] = dwq.reshape(Q_LORA, HEADS, HEAD_PAD)[:, :, :MLA_QK].reshape(Q_LORA, HEADS * MLA_QK)
    gg["q_head_norm"], gg["k_head_norm"] = dgqh[:, :MLA_QK], dgkh[:, :MLA_QK]
    dproj = jnp.concatenate([dcq, dckv, dkr, dsq, dsk.astype(BF16), dsv.astype(BF16), dgates], axis=1)
    grads["w_in"] = _unlayout_w_in(_tn_call(dproj, um, "dw_in"))
    dh1, dh1s, gg["mix_norm"] = _norm_bwd_call([dproj], [w_in], h1, gains["mix_norm"], dh2, "mix_bwd_norm", half_out=True,
                                               w_transposed=True)

    pair_sum, rider = reduce_before(1)
    (da1, db1), got = _ffn_bwd_a_call(dh1s, a1, b1, wts["ffn1_w_out"], "ffn1_bwd_act", rider)
    reduce_after(1, pair_sum, got)
    grads["ffn1_w_out"] = _tn_call(hm1, dh1s, "dw_ffn1_out")
    pair_sum, rider = reduce_before(2)
    res = _tn_call(u1, da1, "dw_ffn1_in_a", shard_cols=D_FF // 2, rider=rider)
    dwa, got = (res, None) if rider is None else res
    reduce_after(2, pair_sum, got)
    grads["ffn1_w_in"] = jnp.concatenate([dwa, _tn_call(u1, db1, "dw_ffn1_in_b", shard_cols=D_FF // 2)], axis=0)
    pair_sum, rider = reduce_before(3)
    res = _norm_bwd_call([da1, db1], [wts["ffn1_w_in"]], x, gains["ffn1_norm"], dh1, "ffn1_bwd_norm",
                         half_out=False, rider=rider)
    (dx, _, gg["ffn1_norm"]), got = (res, None) if rider is None else res
    reduce_after(3, pair_sum, got)
    return sq, dx, gg, (grads if dist is None else reduced)


def kernel(x, p, positions, ffn1_norm, ffn1_w_in, ffn1_w_out, mix_norm, w_in, q_latent_norm, w_q_up, kv_latent_norm, w_kv_up, q_head_norm, k_head_norm, w_branch_mla, w_branch_sb, w_out, ffn2_norm, ffn2_w_in, ffn2_w_out, ple_norm, w_ple_gate, w_ple_proj, loss_target, m_ffn1_norm, m_ffn1_w_in, m_ffn1_w_out, m_mix_norm, m_w_in, m_q_latent_norm, m_w_q_up, m_kv_latent_norm, m_w_kv_up, m_q_head_norm, m_k_head_norm, m_w_branch_mla, m_w_branch_sb, m_w_out, m_ffn2_norm, m_ffn2_w_in, m_ffn2_w_out, m_ple_norm, m_w_ple_gate, m_w_ple_proj, v_ffn1_norm, v_ffn1_w_in, v_ffn1_w_out, v_mix_norm, v_w_in, v_q_latent_norm, v_w_q_up, v_kv_latent_norm, v_w_kv_up, v_q_head_norm, v_k_head_norm, v_w_branch_mla, v_w_branch_sb, v_w_out, v_ffn2_norm, v_ffn2_w_in, v_ffn2_w_out, v_ple_norm, v_w_ple_gate, v_w_ple_proj):
    given = dict(locals())
    w_shard = {n: given[n][0] for n in WEIGHT_ORDER}
    m_shard = {n: given["m_" + n][0] for n in WEIGHT_ORDER}
    v_shard = {n: given["v_" + n][0] for n in WEIGHT_ORDER}
    gains = {n: w_shard[n][None] for n in GAINS}

    chip = 2 * lax.axis_index("x") + lax.axis_index("y")
    sq, dx, gain_grads, reduced = _step(x[0], p[0, 0], positions.reshape(-1, 1), loss_target[0], gains,
                                        {n: _exchange_form(n, w_shard[n]) for n in BIG}, (chip, lax.axis_index("c")))

    rows = [jnp.pad(gain_grads[n], ((0, 0), (0, D_MODEL - GAINS[n]))) for n in GAINS] + [sq]
    gain_block = jnp.concatenate(rows + [jnp.zeros((16 - len(rows), D_MODEL), F32)], axis=0)
    gain_sum = _sum_call(_all_gather_small_call(gain_block, "gains_all_gather"), F32, "gains_sum")
    loss = 0.5 * jnp.sum(gain_sum[len(GAINS)]) / D_MODEL

    outs = {"grad": {}, "delta": {}, "new_m": {}, "new_v": {}}
    gain_pack = lambda t: jnp.concatenate([jnp.pad(t[n][None], ((0, 0), (0, D_MODEL - GAINS[n]))) for n in GAINS], axis=0)
    packed = _adamw_call(gain_pack(w_shard), gain_sum, 0, gain_pack(m_shard), gain_pack(v_shard), "adamw_gains")
    for i, n in enumerate(GAINS):
        for kind, t in zip(("grad", "delta", "new_m", "new_v"), packed):
            outs[kind][n] = t[i, :GAINS[n]][None]
    for n in BIG:
        buf, row0 = reduced[n]
        if n in TRANSPOSED_UPDATE:
            res = [t.T for t in _adamw_call(w_shard[n].T, buf, row0, m_shard[n].T, v_shard[n].T, "adamw_" + n)]
        else:
            res = _adamw_call(w_shard[n], buf, row0, m_shard[n], v_shard[n], "adamw_" + n)
        for kind, t in zip(("grad", "delta", "new_m", "new_v"), res):
            outs[kind][n] = t[None]

    return (loss, dx[None], *[outs["grad"][n] for n in WEIGHT_ORDER], *[outs["delta"][n] for n in WEIGHT_ORDER],
            *[outs["new_m"][n] for n in WEIGHT_ORDER], *[outs["new_v"][n] for n in WEIGHT_ORDER])
```

```python
import collections
import functools
import math

import jax
import jax.numpy as jnp
from jax import lax
from jax.experimental import pallas as pl
from jax.experimental.pallas import tpu as pltpu

F32 = jnp.float32
BF16 = jnp.bfloat16
MESH = pl.DeviceIdType.MESH

D_MODEL = 1024
D_FF = 2816
PLE_DIM = 256
NORM_EPS = 1e-6
HEADS = 8
MLA_NOPE = 64
MLA_ROPE = 32
MLA_QK = 96
Q_LORA = 384
KV_LORA = 256
SB_WIDTH = 512
ROPE_BASE = 10000.0
LOG2_E = math.log2(math.e)
HEAD_PAD = 128
N_CHIPS = 4

ADAM_LR = 0.001
ADAM_B1 = 0.9
ADAM_B2 = 0.999
ADAM_EPS = 1e-08
ADAM_WD = 0.01
ADAM_STEP = 10

SEG_CQ = (0, 384)
SEG_CKV = (384, 256)
SEG_KROPE = (640, 128)
SEG_SBQ = (768, 512)
SEG_SBK = (1280, 512)
SEG_SBV = (1792, 512)
SEG_GATES = (2304, 2048)
IN_COLS_PAD = 4352

TM = 1024
TM_SMALL = 512
TM_PREP_BWD = 256
TQ = 256
MLA_FWD_BLOCKS = 4
MLA_BWD_BLOCKS = 4
SB_FWD_BLOCKS = 4
SB_BWD_BLOCKS = 2
SB_HEAD = 64
SB_SCALE = 0.125
SB_DEAD = -104.0
COL_CHUNK = 256
TN_MAX_COLS = 2816
TN_OPERAND_BYTES = 34 * 1024 * 1024
MAX_ROW_TILE = 512
VMEM_LIMIT = 56 * 1024 * 1024

NT = (((1,), (1,)), ((), ()))
TN = (((0,), (0,)), ((), ()))


def _cp(sem):
    return pltpu.CompilerParams(dimension_semantics=sem, vmem_limit_bytes=VMEM_LIMIT)


def _rows(tm, w):
    return pl.BlockSpec((tm, w), lambda i: (i, 0))


def _whole(shape):
    return pl.BlockSpec(shape, lambda i: (0,) * len(shape))


def _dot(a, b):
    return jnp.dot(a, b, preferred_element_type=F32)


def _dot_nt(a, b):
    return lax.dot_general(a, b, NT, preferred_element_type=F32)


def _dot_tn(a, b):
    return lax.dot_general(a, b, TN, preferred_element_type=F32)


def _rstd(x, n):
    return lax.rsqrt(jnp.sum(x * x, axis=-1, keepdims=True) / n + NORM_EPS)


def _rms_bwd(x, r, g, dy, n):
    gy = dy * g
    return r * gy - x * ((r * r * r) * (jnp.sum(x * gy, axis=-1, keepdims=True) / n))


def _sigmoid(x):
    return jax.nn.sigmoid(x)


def _pick(n, cands):
    for c in cands:
        if n % c == 0:
            return c
    return n


def _row_tile(r):
    for t in range(min(r, MAX_ROW_TILE) // 16 * 16, 15, -16):
        if r % t == 0:
            return t
    return r


HBM = pl.BlockSpec(memory_space=pl.ANY)

_Rider = collections.namedtuple("_Rider", "ins out_shape sems start relay finish")


def _with_rider(body, rider, *, name, grid, in_specs, out_specs, out_shape, args, sem, scratch=()):
    if rider is None:
        return pl.pallas_call(body, name=name, grid=grid, in_specs=in_specs, out_specs=out_specs, out_shape=out_shape,
                              scratch_shapes=list(scratch), compiler_params=_cp(sem))(*args), None
    ni, no, nri, nro = len(in_specs), len(out_specs), len(rider.ins), len(rider.out_shape)

    def riding(*refs):
        ins, r_ins = refs[:ni], refs[ni:ni + nri]
        outs, r_outs = refs[ni + nri:ni + nri + no], refs[ni + nri + no:ni + nri + no + nro]
        scr = refs[ni + nri + no + nro:ni + nri + no + nro + len(scratch)]
        sems = refs[ni + nri + no + nro + len(scratch):]
        step = pl.program_id(0)
        for a in range(1, len(grid)):
            step = step * grid[a] + pl.program_id(a)
        steps = math.prod(grid)

        @pl.when(step == 0)
        def _():
            rider.start(r_ins, r_outs, sems)

        body(*ins, *outs, *scr)

        if steps >= 3:
            @pl.when(step == steps - 2)
            def _():
                rider.relay(r_ins, r_outs, sems)

        @pl.when(step == steps - 1)
        def _():
            if steps < 3:
                rider.relay(r_ins, r_outs, sems)
            rider.finish(r_ins, r_outs, sems)

    res = pl.pallas_call(
        riding, name=name, grid=grid, in_specs=list(in_specs) + [HBM] * nri, out_specs=list(out_specs) + [HBM] * nro,
        out_shape=list(out_shape) + list(rider.out_shape),
        scratch_shapes=list(scratch) + [pltpu.SemaphoreType.DMA((k,)) for k in rider.sems],
        compiler_params=_cp(("arbitrary",) * len(grid)))(*args, *rider.ins)
    return res[:no], res[no:]


def _norm_call(h, g, name, rider=None):
    s, d = h.shape
    tm = min(TM, s)

    def body(h_ref, g_ref, u_ref):
        x = h_ref[...]
        u_ref[...] = ((x * _rstd(x, d)) * g_ref[...]).astype(BF16)

    (u,), got = _with_rider(
        body, rider, name=name, grid=(s // tm,),
        in_specs=[_rows(tm, d), _whole((1, d))], out_specs=[_rows(tm, d)],
        out_shape=[jax.ShapeDtypeStruct((s, d), BF16)], args=(h, g), sem=("parallel",))
    return u, got


def _ffn_in_call(u, w, name, rider=None):
    s, d = u.shape
    tn = w.shape[2]
    nj = w.shape[0] // 2
    n = nj * tn
    tm = min(TM, s)

    def body(u_ref, wa_ref, wb_ref, a_ref, b_ref, hm_ref):
        uu = u_ref[...]
        a = _dot(uu, wa_ref[...])
        b = _dot(uu, wb_ref[...])
        a_ref[...] = a
        b_ref[...] = b
        hm_ref[...] = ((a * _sigmoid(a)) * b).astype(BF16)

    blk = pl.BlockSpec((tm, tn), lambda j, i: (i, j))
    return _with_rider(
        body, rider, name=name, grid=(nj, s // tm),
        in_specs=[pl.BlockSpec((tm, d), lambda j, i: (i, 0)),
                  pl.BlockSpec((None, d, tn), lambda j, i: (j, 0, 0)),
                  pl.BlockSpec((None, d, tn), lambda j, i: (j + nj, 0, 0))],
        out_specs=[blk, blk, blk],
        out_shape=[jax.ShapeDtypeStruct((s, n), F32), jax.ShapeDtypeStruct((s, n), F32),
                   jax.ShapeDtypeStruct((s, n), BF16)],
        args=(u, w, w), sem=("parallel", "parallel"))


def _ffn_out_call(hm, w, h, gain, name, rider=None):
    s, n = hm.shape
    d = w.shape[1]
    tm = min(TM, s)

    def body(hm_ref, w_ref, h_ref, g_ref, o_ref, u_ref):
        x = h_ref[...] + 0.5 * _dot(hm_ref[...], w_ref[...])
        o_ref[...] = x
        u_ref[...] = ((x * _rstd(x, d)) * g_ref[...]).astype(BF16)

    return _with_rider(
        body, rider, name=name, grid=(s // tm,),
        in_specs=[_rows(tm, n), _whole((n, d)), _rows(tm, d), _whole((1, d))], out_specs=[_rows(tm, d), _rows(tm, d)],
        out_shape=[jax.ShapeDtypeStruct((s, d), F32), jax.ShapeDtypeStruct((s, d), BF16)], args=(hm, w, h, gain),
        sem=("parallel",))


def _mix_in_call(u, wt, name, rider=None):
    s, d = u.shape
    tm = min(TM_SMALL, s)
    segs = [(SEG_CQ, F32), (SEG_CKV, F32), (SEG_KROPE, F32), (SEG_SBQ, BF16), (SEG_SBK, BF16),
            (SEG_SBV, BF16), (SEG_GATES, F32)]

    def body(u_ref, w_ref, *outs):
        uu = u_ref[...]
        for ((off, width), _), o_ref in zip(segs, outs):
            o_ref[...] = _dot_nt(uu, w_ref[off:off + width, :]).astype(o_ref.dtype)

    return _with_rider(
        body, rider, name=name, grid=(s // tm,),
        in_specs=[_rows(tm, d), _whole((IN_COLS_PAD, d))],
        out_specs=[_rows(tm, width) for (_, width), _ in segs],
        out_shape=[jax.ShapeDtypeStruct((s, width), dt) for (_, width), dt in segs],
        args=(u, wt), sem=("parallel",))


def _lane(shape):
    return lax.broadcasted_iota(jnp.int32, shape, len(shape) - 1)


def _rot_half(y):
    lane = _lane(y.shape)
    swapped = jnp.where(lane < MLA_NOPE + MLA_ROPE // 2, pltpu.roll(y, HEAD_PAD - 16, 1), pltpu.roll(y, 16, 1))
    return jnp.where((lane >= MLA_NOPE) & (lane < MLA_QK), swapped, 0.0)


def _rope_tables(pos_ref, freq_ref, sign_ref):
    ang = pos_ref[...].astype(F32) * freq_ref[...]
    return jnp.cos(ang), jnp.sin(ang) * sign_ref[...]


def _head_fwd(x, g, cosv, ssv):
    r = _rstd(x, MLA_QK)
    y = (x * r) * g
    return y * cosv + _rot_half(y) * ssv, r


def _head_bwd(x, r, g, cosv, ssv, dout):
    dy = dout * cosv + _rot_half(dout * ssv)
    return _rms_bwd(x, r, g, dy, MLA_QK), jnp.sum(dy * (x * r), axis=0, keepdims=True)


def _mla_prep_call(cq, ckv, krope, pos, freq, sign, g_ql, g_kvl, g_qh, g_kh, wq, wkv, name):
    s = cq.shape[0]
    tm = min(TM_SMALL, s)
    width = HEADS * HEAD_PAD

    def body(cq_ref, ckv_ref, kr_ref, pos_ref, freq_ref, sign_ref, gql_ref, gkvl_ref, gqh_ref, gkh_ref,
             wq_ref, wkv_ref, q_ref, k_ref, v_ref):
        cosv, ssv = _rope_tables(pos_ref, freq_ref, sign_ref)
        x = cq_ref[...]
        qr = _dot(((x * _rstd(x, Q_LORA)) * gql_ref[...]).astype(BF16), wq_ref[...])
        x = ckv_ref[...]
        kv = _dot(((x * _rstd(x, KV_LORA)) * gkvl_ref[...]).astype(BF16), wkv_ref[...])
        kr = kr_ref[...]
        lane = _lane((tm, HEAD_PAD))
        for h in range(HEADS):
            sl = slice(h * HEAD_PAD, (h + 1) * HEAD_PAD)
            qh, _ = _head_fwd(qr[:, sl], gqh_ref[...], cosv, ssv)
            q_ref[:, sl] = qh.astype(BF16)
            kvh = kv[:, sl]
            kh, _ = _head_fwd(jnp.where(lane < MLA_NOPE, kvh, kr), gkh_ref[...], cosv, ssv)
            k_ref[:, sl] = kh.astype(BF16)
            v_ref[:, sl] = jnp.where(lane >= MLA_NOPE, kvh, jnp.where(lane == 0, 1.0, 0.0)).astype(BF16)

    out = jax.ShapeDtypeStruct((s, width), BF16)
    return pl.pallas_call(
        body, name=name, grid=(s // tm,),
        in_specs=[_rows(tm, Q_LORA), _rows(tm, KV_LORA), _rows(tm, HEAD_PAD), _rows(tm, 1),
                  _whole((1, HEAD_PAD)), _whole((1, HEAD_PAD)), _whole((1, Q_LORA)), _whole((1, KV_LORA)),
                  _whole((1, HEAD_PAD)), _whole((1, HEAD_PAD)), _whole((Q_LORA, width)), _whole((KV_LORA, width))],
        out_specs=[_rows(tm, width)] * 3, out_shape=[out, out, out],
        compiler_params=_cp(("parallel",)))(cq, ckv, krope, pos, freq, sign, g_ql, g_kvl, g_qh, g_kh, wq, wkv)


def _attn_specs(s, nb):
    qspec = pl.BlockSpec((TQ, nb * HEAD_PAD), lambda g, i: (i, g))
    kspec = pl.BlockSpec((s, nb * HEAD_PAD), lambda g, i: (0, g))
    return qspec, kspec


def _lanes(b):
    return slice(b * HEAD_PAD, (b + 1) * HEAD_PAD)


def _tri(cmp):
    r = lax.broadcasted_iota(jnp.int32, (TQ, TQ), 0)
    c = lax.broadcasted_iota(jnp.int32, (TQ, TQ), 1)
    return cmp(r, c)


def _mla_fwd_call(q, k, v, name, rider=None):
    s, width = q.shape
    scale = 1.0 / math.sqrt(MLA_QK)

    nb = MLA_FWD_BLOCKS

    def body(q_ref, k_ref, v_ref, o_ref, lse_ref):
        qi = pl.program_id(1)
        qs = [q_ref[:, _lanes(b)] for b in range(nb)]
        causal = _tri(lambda r, c: c <= r)

        def step(kb, carry, diag):
            ks = pl.multiple_of(kb * TQ, TQ)
            heads = range(nb)
            scs = [_dot_nt(qs[b], k_ref[pl.ds(ks, TQ), _lanes(b)]) * (scale * LOG2_E) for b in heads]
            if diag:
                scs = [jnp.where(causal, sc, -1e30) for sc in scs]
            mns = [jnp.maximum(carry[b][0], jnp.max(scs[b], axis=-1, keepdims=True)) for b in heads]
            als = [jnp.exp2(carry[b][0] - mns[b]) for b in heads]
            ps = [jnp.exp2(scs[b] - mns[b]).astype(BF16) for b in heads]
            accs = [als[b] * carry[b][1] + _dot(ps[b], v_ref[pl.ds(ks, TQ), _lanes(b)]) for b in heads]
            return tuple((mns[b], accs[b]) for b in heads)

        init = tuple((jnp.full((TQ, 1), -1e30, F32), jnp.zeros((TQ, HEAD_PAD), F32)) for _ in range(nb))
        carry = step(qi, init, True)
        carry = lax.fori_loop(0, qi, lambda kb, c: step(kb, c, False), carry)
        for b in range(nb):
            m, acc = carry[b]
            l = acc[:, 0:1]
            o_ref[:, _lanes(b)] = (acc / l).astype(BF16)
            lse_ref[:, _lanes(b)] = jnp.broadcast_to(m * (1.0 / LOG2_E) + jnp.log(l), (TQ, HEAD_PAD))

    qspec, kspec = _attn_specs(s, nb)
    return _with_rider(
        body, rider, name=name, grid=(width // (nb * HEAD_PAD), s // TQ),
        in_specs=[qspec, kspec, kspec], out_specs=[qspec, qspec],
        out_shape=[jax.ShapeDtypeStruct((s, width), BF16), jax.ShapeDtypeStruct((s, width), F32)],
        args=(q, k, v), sem=("parallel", "arbitrary"))


def _mla_bwd_call(q, k, v, o, do, lse, name, rider=None):
    s, width = q.shape
    scale = 1.0 / math.sqrt(MLA_QK)
    nb = MLA_BWD_BLOCKS

    def body(q_ref, k_ref, v_ref, o_ref, do_ref, lse_ref, dq_ref, dk_ref, dv_ref):
        qi = pl.program_id(1)

        @pl.when(qi == 0)
        def _():
            dk_ref[...] = jnp.zeros_like(dk_ref)
            dv_ref[...] = jnp.zeros_like(dv_ref)

        qs = [q_ref[:, _lanes(b)] for b in range(nb)]
        dos = [do_ref[:, _lanes(b)] for b in range(nb)]
        lses = [lse_ref[:, b * HEAD_PAD:b * HEAD_PAD + 1] * LOG2_E for b in range(nb)]
        dlts = [jnp.sum(dos[b].astype(F32) * o_ref[:, _lanes(b)].astype(F32), axis=-1, keepdims=True) for b in range(nb)]
        causal = _tri(lambda r, c: c <= r)

        def step(kb, dqs, diag):
            ks = pl.multiple_of(kb * TQ, TQ)
            heads = range(nb)
            kts = [k_ref[pl.ds(ks, TQ), _lanes(b)] for b in heads]
            scs = [_dot_nt(qs[b], kts[b]) for b in heads]
            dps = [_dot_nt(dos[b], v_ref[pl.ds(ks, TQ), _lanes(b)]) for b in heads]
            ps = [jnp.exp2(scs[b] * (scale * LOG2_E) - lses[b]) for b in heads]
            if diag:
                ps = [jnp.where(causal, p, 0.0) for p in ps]
            dss = [(ps[b] * (dps[b] - dlts[b]) * scale).astype(BF16) for b in heads]
            dvs = [_dot_tn(ps[b].astype(BF16), dos[b]) for b in heads]
            dks = [_dot_tn(dss[b], qs[b]) for b in heads]
            out = tuple(dqs[b] + _dot(dss[b], kts[b]) for b in heads)
            for b in heads:
                dv_ref[pl.ds(ks, TQ), _lanes(b)] += dvs[b]
                dk_ref[pl.ds(ks, TQ), _lanes(b)] += dks[b]
            return out

        dqs = step(qi, tuple(jnp.zeros((TQ, HEAD_PAD), F32) for _ in range(nb)), True)
        dqs = lax.fori_loop(0, qi, lambda kb, c: step(kb, c, False), dqs)
        for b in range(nb):
            dq_ref[:, _lanes(b)] = dqs[b]

    qspec, kspec = _attn_specs(s, nb)
    out = jax.ShapeDtypeStruct((s, width), F32)
    return _with_rider(
        body, rider, name=name, grid=(width // (nb * HEAD_PAD), s // TQ),
        in_specs=[qspec, kspec, kspec, qspec, qspec, qspec], out_specs=[qspec, kspec, kspec],
        out_shape=[out, out, out], args=(q, k, v, o, do, lse), sem=("parallel", "arbitrary"))


def _dot_hilo(x, u):
    hi = x.astype(BF16)
    lo = (x - hi.astype(F32)).astype(BF16)
    return _dot(hi, u) + _dot(lo, u)


def _sb_logs(z):
    ls = jnp.minimum(z, 0.0) - jnp.log(1.0 + jnp.exp(-jnp.abs(z)))
    return ls, ls - z


def _sb_head_q(qb, first, hh):
    keep = first if hh == 0 else jnp.logical_not(first)
    return jnp.where(keep, qb, jnp.zeros_like(qb)) * jnp.asarray(SB_SCALE, qb.dtype)


def _sb_fwd_call(q, k, v, name):
    s, width = q.shape
    nb = SB_FWD_BLOCKS
    chains = [(b, hh) for b in range(nb) for hh in range(HEAD_PAD // SB_HEAD)]

    def body(q_ref, k_ref, v_ref, o_ref):
        qi = pl.program_id(1)
        strict = _tri(lambda r, c: c < r)
        after = _tri(lambda r, c: r > c).astype(BF16)
        first = _lane((1, HEAD_PAD)) < SB_HEAD
        qhs = [_sb_head_q(q_ref[:, _lanes(b)], first, hh) for b, hh in chains]

        def step(kb, carry, diag):
            ks = pl.multiple_of(kb * TQ, TQ)
            ids = range(len(chains))
            zs = [_dot_nt(qhs[ci], k_ref[pl.ds(ks, TQ), _lanes(chains[ci][0])]) for ci in ids]
            logs = [_sb_logs(z) for z in zs]
            lss = [lg[0] for lg in logs]
            l1ms = [jnp.where(strict, lg[1], 0.0) if diag else lg[1] for lg in logs]
            sufs = [_dot_hilo(l1m, after) for l1m in l1ms]
            as_ = [jnp.exp(lss[ci] + sufs[ci] + carry[ci][0]) for ci in ids]
            if diag:
                as_ = [jnp.where(strict, a, 0.0) for a in as_]
            accs = [carry[ci][1] + _dot(as_[ci].astype(BF16), v_ref[pl.ds(ks, TQ), _lanes(chains[ci][0])]) for ci in ids]
            return tuple((carry[ci][0] + jnp.sum(l1ms[ci], axis=-1, keepdims=True), accs[ci]) for ci in ids)

        init = tuple((jnp.zeros((TQ, 1), F32), jnp.zeros((TQ, HEAD_PAD), F32)) for _ in chains)
        carry = _sb_sweep(step, qi, init)
        for b in range(nb):
            o_ref[:, _lanes(b)] = jnp.where(first, carry[2 * b][1], carry[2 * b + 1][1])

    qspec, kspec = _attn_specs(s, nb)
    return pl.pallas_call(
        body, name=name, grid=(width // (nb * HEAD_PAD), s // TQ),
        in_specs=[qspec, kspec, kspec], out_specs=qspec, out_shape=jax.ShapeDtypeStruct((s, width), F32),
        compiler_params=_cp(("parallel", "arbitrary")))(q, k, v)


def _sb_sweep(step, qi, init):
    def live(carry):
        top = carry[0][0]
        for c in carry[1:]:
            top = jnp.maximum(top, c[0])
        return jnp.max(top)

    carry = step(qi, init, True)

    def cond(state):
        j, alive, _ = state
        return jnp.logical_and(j < qi, alive > SB_DEAD)

    def body(state):
        j, _, carry = state
        carry = step(qi - 1 - j, carry, False)
        return j + 1, live(carry), carry

    return lax.while_loop(cond, body, (jnp.int32(0), live(carry), carry))[2]


def _sb_bwd_call(q, k, v, do, o, name):
    s, width = q.shape
    nb = SB_BWD_BLOCKS
    chains = [(b, hh) for b in range(nb) for hh in range(HEAD_PAD // SB_HEAD)]

    def body(q_ref, k_ref, v_ref, do_ref, o_ref, dq_ref, dk_ref, dv_ref):
        qi = pl.program_id(1)

        @pl.when(qi == 0)
        def _():
            dk_ref[...] = jnp.zeros_like(dk_ref)
            dv_ref[...] = jnp.zeros_like(dv_ref)

        strict = _tri(lambda r, c: c < r)
        after = _tri(lambda r, c: r > c).astype(BF16)
        from_here = _tri(lambda r, c: r >= c).astype(BF16)
        first = _lane((1, HEAD_PAD)) < SB_HEAD
        qhs = [_sb_head_q(q_ref[:, _lanes(b)], first, hh) for b, hh in chains]
        dohs = []
        for b, hh in chains:
            dob = do_ref[:, _lanes(b)]
            dohs.append(jnp.where(first if hh == 0 else jnp.logical_not(first), dob, jnp.zeros_like(dob)))
        gtots = [jnp.sum(dohs[ci].astype(F32) * o_ref[:, _lanes(chains[ci][0])], axis=-1, keepdims=True)
                 for ci in range(len(chains))]

        def step(kb, carry, diag):
            ks = pl.multiple_of(kb * TQ, TQ)
            ids = range(len(chains))
            kts = [k_ref[pl.ds(ks, TQ), _lanes(b)] for b, _ in chains]
            zs = [_dot_nt(qhs[ci], kts[ci]) for ci in ids]
            das = [_dot_nt(dohs[ci], v_ref[pl.ds(ks, TQ), _lanes(chains[ci][0])]) for ci in ids]
            logs = [_sb_logs(z) for z in zs]
            lss = [lg[0] for lg in logs]
            l1ms = [jnp.where(strict, lg[1], 0.0) if diag else lg[1] for lg in logs]
            sufs = [_dot_hilo(l1m, after) for l1m in l1ms]
            as_ = [jnp.exp(lss[ci] + sufs[ci] + carry[ci][0]) for ci in ids]
            if diag:
                as_ = [jnp.where(strict, a, 0.0) for a in as_]
            abs_ = [a.astype(BF16) for a in as_]
            gs = [abs_[ci].astype(F32) * das[ci] for ci in ids]
            cexs = [gtots[ci] - (carry[ci][1] + _dot_hilo(gs[ci], from_here)) for ci in ids]
            dzs = [gs[ci] - jnp.exp(lss[ci]) * (gs[ci] + cexs[ci]) for ci in ids]
            if diag:
                dzs = [jnp.where(strict, dz, 0.0) for dz in dzs]
            dzbs = [dz.astype(BF16) for dz in dzs]
            dvps = [_dot_tn(abs_[ci], dohs[ci]) for ci in ids]
            dkps = [_dot_tn(dzbs[ci], qhs[ci]) for ci in ids]
            out = tuple((carry[ci][0] + jnp.sum(l1ms[ci], axis=-1, keepdims=True),
                         carry[ci][1] + jnp.sum(gs[ci], axis=-1, keepdims=True),
                         carry[ci][2] + _dot(dzbs[ci], kts[ci])) for ci in ids)
            for b in range(nb):
                dk_ref[pl.ds(ks, TQ), _lanes(b)] += dkps[2 * b] + dkps[2 * b + 1]
                dv_ref[pl.ds(ks, TQ), _lanes(b)] += dvps[2 * b] + dvps[2 * b + 1]
            return out

        init = tuple((jnp.zeros((TQ, 1), F32), jnp.zeros((TQ, 1), F32), jnp.zeros((TQ, HEAD_PAD), F32)) for _ in chains)
        carry = _sb_sweep(step, qi, init)
        for b in range(nb):
            dq_ref[:, _lanes(b)] = (jnp.where(first, carry[2 * b][2], carry[2 * b + 1][2]) * SB_SCALE).astype(BF16)

    qspec, kspec = _attn_specs(s, nb)
    return pl.pallas_call(
        body, name=name, grid=(width // (nb * HEAD_PAD), s // TQ),
        in_specs=[qspec, kspec, kspec, qspec, qspec], out_specs=[qspec, kspec, kspec],
        out_shape=[jax.ShapeDtypeStruct((s, width), BF16), jax.ShapeDtypeStruct((s, width), F32),
                   jax.ShapeDtypeStruct((s, width), F32)],
        compiler_params=_cp(("parallel", "arbitrary")))(q, k, v, do, o)


def _merge_out_call(om, osb, gates, h, wbm, wbs, wo, gain, name):
    s, d = h.shape
    tm = min(TM_SMALL, s)

    def body(om_ref, os_ref, g_ref, h_ref, wbm_ref, wbs_ref, wo_ref, gain_ref, h2_ref, bm_ref, bs_ref, mg_ref, u_ref):
        bm = _dot(om_ref[...], wbm_ref[...])
        bs = _dot(os_ref[...].astype(BF16), wbs_ref[...])
        mg = (_sigmoid(g_ref[:, :d]) * bm + _sigmoid(g_ref[:, d:]) * bs).astype(BF16)
        bm_ref[...] = bm
        bs_ref[...] = bs
        mg_ref[...] = mg
        x = h_ref[...] + _dot(mg, wo_ref[...])
        h2_ref[...] = x
        u_ref[...] = ((x * _rstd(x, d)) * gain_ref[...]).astype(BF16)

    return pl.pallas_call(
        body, name=name, grid=(s // tm,),
        in_specs=[_rows(tm, om.shape[1]), _rows(tm, SB_WIDTH), _rows(tm, 2 * d), _rows(tm, d),
                  _whole(wbm.shape), _whole(wbs.shape), _whole(wo.shape), _whole((1, d))],
        out_specs=[_rows(tm, d)] * 5,
        out_shape=[jax.ShapeDtypeStruct((s, d), F32), jax.ShapeDtypeStruct((s, d), F32),
                   jax.ShapeDtypeStruct((s, d), F32), jax.ShapeDtypeStruct((s, d), BF16),
                   jax.ShapeDtypeStruct((s, d), BF16)],
        compiler_params=_cp(("parallel",)))(om, osb, gates, h, wbm, wbs, wo, gain)


def _ple_call(h, g, wg, p, wp, tgt, name):
    s, d = h.shape
    tm = min(TM_SMALL, s)

    def body(h_ref, g_ref, wg_ref, p_ref, wp_ref, t_ref, dh_ref, dhs_ref, un_ref, dgl_ref, dpp_ref, dg_ref, sq_ref):
        @pl.when(pl.program_id(0) == 0)
        def _():
            dg_ref[...] = jnp.zeros_like(dg_ref)
            sq_ref[...] = jnp.zeros_like(sq_ref)

        x = h_ref[...]
        gain = g_ref[...]
        r = _rstd(x, d)
        xh = x * r
        un = (xh * gain).astype(BF16)
        sg = _sigmoid(_dot(un, wg_ref[...]))
        pp = _dot(p_ref[...].astype(BF16), wp_ref[...])
        diff = (x + sg * pp) - t_ref[...]
        sq_ref[...] += jnp.sum(diff * diff, axis=0, keepdims=True)
        dy = diff * (1.0 / d)
        dgl = ((dy * pp) * (sg * (1.0 - sg))).astype(BF16)
        dun = _dot_nt(dgl, wg_ref[...])
        dg_ref[...] += jnp.sum(dun * xh, axis=0, keepdims=True)
        dh = dy + _rms_bwd(x, r, gain, dun, d)
        dh_ref[...] = dh
        dhs_ref[...] = (0.5 * dh).astype(BF16)
        un_ref[...] = un
        dgl_ref[...] = dgl
        dpp_ref[...] = (dy * sg).astype(BF16)

    bf = jax.ShapeDtypeStruct((s, d), BF16)
    vec = jax.ShapeDtypeStruct((1, d), F32)
    return pl.pallas_call(
        body, name=name, grid=(s // tm,),
        in_specs=[_rows(tm, d), _whole((1, d)), _whole(wg.shape), _rows(tm, PLE_DIM), _whole(wp.shape), _rows(tm, d)],
        out_specs=[_rows(tm, d)] * 5 + [_whole((1, d))] * 2,
        out_shape=[jax.ShapeDtypeStruct((s, d), F32), bf, bf, bf, bf, vec, vec],
        compiler_params=_cp(("arbitrary",)))(h, g, wg, p, wp, tgt)


def _ffn_bwd_a_call(dhs, a, b, wo, name, rider=None):
    s, n = a.shape
    d = dhs.shape[1]
    tn = n // 2
    tm = min(TM, s)

    def body(dh_ref, a_ref, b_ref, wo_ref, da_ref, db_ref):
        dh = dh_ref[...]
        chunks = [slice(c0, min(c0 + COL_CHUNK, tn)) for c0 in range(0, tn, COL_CHUNK)]
        dhms = [_dot_nt(dh, wo_ref[sl, :]) for sl in chunks]
        for sl, dhm in zip(chunks, dhms):
            av = a_ref[:, sl]
            sa = _sigmoid(av)
            da_ref[:, sl] = (dhm * b_ref[:, sl] * (sa * (1.0 + av * (1.0 - sa)))).astype(BF16)
            db_ref[:, sl] = (dhm * (av * sa)).astype(BF16)

    blk = pl.BlockSpec((tm, tn), lambda j, i: (i, j))
    return _with_rider(
        body, rider, name=name, grid=(n // tn, s // tm),
        in_specs=[pl.BlockSpec((tm, d), lambda j, i: (i, 0)), blk, blk, pl.BlockSpec((tn, d), lambda j, i: (j, 0))],
        out_specs=[blk, blk],
        out_shape=[jax.ShapeDtypeStruct((s, n), BF16)] * 2, args=(dhs, a, b, wo), sem=("parallel", "parallel"))


def _norm_bwd_call(dy_list, w_list, h, g, dh_in, name, half_out, rider=None, w_transposed=False):
    s, d = h.shape
    tm = min(TM_SMALL, s)
    nk, nw = len(dy_list), len(w_list)
    factor = 0.5 if half_out else 1.0
    sharded = nw == 1 and w_list[0].ndim == 3

    def body(*refs):
        dy_refs = refs[:nk]
        w_refs = refs[nk:nk + nw]
        h_ref, g_ref, dhin_ref, dh_ref, dhb_ref, dg_ref = refs[nk + nw:]

        @pl.when(pl.program_id(0) == 0)
        def _():
            dg_ref[...] = jnp.zeros_like(dg_ref)

        if sharded:
            c = w_list[0].shape[2]
            per = dy_list[0].shape[1] // c
            du = None
            for k in range(w_list[0].shape[0]):
                part = _dot_nt(dy_refs[k // per][:, (k % per) * c:(k % per + 1) * c], w_refs[0][k])
                du = part if du is None else du + part
        else:
            mm = _dot if w_transposed else _dot_nt
            du = mm(dy_refs[0][...], w_refs[0][...])
            for dy_ref, w_ref in zip(dy_refs[1:], w_refs[1:]):
                du = du + mm(dy_ref[...], w_ref[...])
        x = h_ref[...]
        r = _rstd(x, d)
        dg_ref[...] += jnp.sum(du * (x * r), axis=0, keepdims=True)
        dh = dhin_ref[...] + _rms_bwd(x, r, g_ref[...], du, d)
        dh_ref[...] = dh
        dhb_ref[...] = (factor * dh).astype(BF16)

    outs, got = _with_rider(
        body, rider, name=name, grid=(s // tm,),
        in_specs=[_rows(tm, dy.shape[1]) for dy in dy_list] + [_whole(w.shape) for w in w_list]
        + [_rows(tm, d), _whole((1, d)), _rows(tm, d)],
        out_specs=[_rows(tm, d), _rows(tm, d), _whole((1, d))],
        out_shape=[jax.ShapeDtypeStruct((s, d), F32), jax.ShapeDtypeStruct((s, d), BF16),
                   jax.ShapeDtypeStruct((1, d), F32)],
        args=(*dy_list, *w_list, h, g, dh_in), sem=("arbitrary",))
    return outs if rider is None else (outs, got)


def _merge_bwd_call(dhb, gates, bm, bs, wo, wbm, wbs, name):
    s, d = bm.shape
    tm = min(TM_SMALL, s)

    def body(dh_ref, g_ref, bm_ref, bs_ref, wo_ref, wbm_ref, wbs_ref, dg_ref, dbm_ref, dbs_ref, dom_ref, dos_ref):
        dmg = _dot_nt(dh_ref[...], wo_ref[...])
        s1 = _sigmoid(g_ref[:, :d])
        s2 = _sigmoid(g_ref[:, d:])
        dg_ref[:, :d] = (dmg * bm_ref[...] * (s1 * (1.0 - s1))).astype(BF16)
        dg_ref[:, d:] = (dmg * bs_ref[...] * (s2 * (1.0 - s2))).astype(BF16)
        dbm = (dmg * s1).astype(BF16)
        dbs = (dmg * s2).astype(BF16)
        dbm_ref[...] = dbm
        dbs_ref[...] = dbs
        dom_ref[...] = _dot_nt(dbm, wbm_ref[...]).astype(BF16)
        dos_ref[...] = _dot_nt(dbs, wbs_ref[...]).astype(BF16)

    wm = wbm.shape[0]
    return pl.pallas_call(
        body, name=name, grid=(s // tm,),
        in_specs=[_rows(tm, d), _rows(tm, 2 * d), _rows(tm, d), _rows(tm, d),
                  _whole(wo.shape), _whole(wbm.shape), _whole(wbs.shape)],
        out_specs=[_rows(tm, 2 * d), _rows(tm, d), _rows(tm, d), _rows(tm, wm), _rows(tm, SB_WIDTH)],
        out_shape=[jax.ShapeDtypeStruct((s, 2 * d), BF16), jax.ShapeDtypeStruct((s, d), BF16),
                   jax.ShapeDtypeStruct((s, d), BF16), jax.ShapeDtypeStruct((s, wm), BF16),
                   jax.ShapeDtypeStruct((s, SB_WIDTH), BF16)],
        compiler_params=_cp(("parallel",)))(dhb, gates, bm, bs, wo, wbm, wbs)


def _mla_prep_bwd_call(cq, ckv, krope, pos, freq, sign, g_ql, g_kvl, g_qh, g_kh, wq, wkv, dq, dk, dv, name):
    s = cq.shape[0]
    tm = min(TM_PREP_BWD, s)
    width = HEADS * HEAD_PAD

    def body(cq_ref, ckv_ref, kr_ref, pos_ref, freq_ref, sign_ref, gql_ref, gkvl_ref, gqh_ref, gkh_ref,
             wq_ref, wkv_ref, dq_ref, dk_ref, dv_ref,
             dcq_ref, dckv_ref, dkr_ref, dwq_ref, dwkv_ref, dgql_ref, dgkvl_ref, dgqh_ref, dgkh_ref, dqr_ref, dkv_ref):
        @pl.when(pl.program_id(0) == 0)
        def _():
            for ref in (dwq_ref, dwkv_ref, dgql_ref, dgkvl_ref, dgqh_ref, dgkh_ref):
                ref[...] = jnp.zeros_like(ref)

        cosv, ssv = _rope_tables(pos_ref, freq_ref, sign_ref)
        xq = cq_ref[...]
        rq = _rstd(xq, Q_LORA)
        cqn = ((xq * rq) * gql_ref[...]).astype(BF16)
        qr = _dot(cqn, wq_ref[...])
        xk = ckv_ref[...]
        rk = _rstd(xk, KV_LORA)
        ckvn = ((xk * rk) * gkvl_ref[...]).astype(BF16)
        kv = _dot(ckvn, wkv_ref[...])
        kr = kr_ref[...]
        lane = _lane((tm, HEAD_PAD))
        dkr = jnp.zeros((tm, HEAD_PAD), F32)
        dgqh = jnp.zeros((1, HEAD_PAD), F32)
        dgkh = jnp.zeros((1, HEAD_PAD), F32)
        for h in range(HEADS):
            sl = slice(h * HEAD_PAD, (h + 1) * HEAD_PAD)
            x = qr[:, sl]
            dx, dgh = _head_bwd(x, _rstd(x, MLA_QK), gqh_ref[...], cosv, ssv, dq_ref[:, sl])
            dqr_ref[:, sl] = dx.astype(BF16)
            dgqh = dgqh + dgh
            x = jnp.where(lane < MLA_NOPE, kv[:, sl], kr)
            dx, dgh = _head_bwd(x, _rstd(x, MLA_QK), gkh_ref[...], cosv, ssv, dk_ref[:, sl])
            dgkh = dgkh + dgh
            dkr = dkr + jnp.where(lane >= MLA_NOPE, dx, 0.0)
            dkv_ref[:, sl] = jnp.where(lane < MLA_NOPE, dx, dv_ref[:, sl]).astype(BF16)
        dgqh_ref[...] += dgqh
        dgkh_ref[...] += dgkh
        dkr_ref[...] = dkr.astype(BF16)
        dqr = dqr_ref[...]
        dkvb = dkv_ref[...]
        dwq_ref[...] += _dot_tn(cqn, dqr)
        dwkv_ref[...] += _dot_tn(ckvn, dkvb)
        dcqn = _dot_nt(dqr, wq_ref[...])
        dgql_ref[...] += jnp.sum(dcqn * (xq * rq), axis=0, keepdims=True)
        dcq_ref[...] = _rms_bwd(xq, rq, gql_ref[...], dcqn, Q_LORA).astype(BF16)
        dckvn = _dot_nt(dkvb, wkv_ref[...])
        dgkvl_ref[...] += jnp.sum(dckvn * (xk * rk), axis=0, keepdims=True)
        dckv_ref[...] = _rms_bwd(xk, rk, gkvl_ref[...], dckvn, KV_LORA).astype(BF16)

    vec = lambda n: jax.ShapeDtypeStruct((1, n), F32)
    outs = pl.pallas_call(
        body, name=name, grid=(s // tm,),
        in_specs=[_rows(tm, Q_LORA), _rows(tm, KV_LORA), _rows(tm, HEAD_PAD), _rows(tm, 1),
                  _whole((1, HEAD_PAD)), _whole((1, HEAD_PAD)), _whole((1, Q_LORA)), _whole((1, KV_LORA)),
                  _whole((1, HEAD_PAD)), _whole((1, HEAD_PAD)), _whole((Q_LORA, width)), _whole((KV_LORA, width)),
                  _rows(tm, width), _rows(tm, width), _rows(tm, width)],
        out_specs=[_rows(tm, Q_LORA), _rows(tm, KV_LORA), _rows(tm, HEAD_PAD), _whole((Q_LORA, width)),
                   _whole((KV_LORA, width)), _whole((1, Q_LORA)), _whole((1, KV_LORA)), _whole((1, HEAD_PAD)),
                   _whole((1, HEAD_PAD)), _rows(tm, width), _rows(tm, width)],
        out_shape=[jax.ShapeDtypeStruct((s, Q_LORA), BF16), jax.ShapeDtypeStruct((s, KV_LORA), BF16),
                   jax.ShapeDtypeStruct((s, HEAD_PAD), BF16), jax.ShapeDtypeStruct((Q_LORA, width), F32),
                   jax.ShapeDtypeStruct((KV_LORA, width), F32), vec(Q_LORA), vec(KV_LORA), vec(HEAD_PAD), vec(HEAD_PAD),
                   jax.ShapeDtypeStruct((s, width), BF16), jax.ShapeDtypeStruct((s, width), BF16)],
        compiler_params=_cp(("arbitrary",)))(cq, ckv, krope, pos, freq, sign, g_ql, g_kvl, g_qh, g_kh, wq, wkv, dq, dk, dv)
    return outs[:9]


def _tn_call(a, b, name, shard_cols=None, rider=None):
    s, ka = a.shape
    nb = b.shape[1]
    ti = _pick(ka, (512, 256, 128))
    if shard_cols is not None:
        tj = shard_cols
    else:
        tj = nb if nb <= TN_MAX_COLS else _pick(nb, (2176, 1024, 512, 256, 128))
    ts = s if 2 * s * (ti + tj) * a.dtype.itemsize <= TN_OPERAND_BYTES else s // 2
    ns = s // ts

    def body(a_ref, b_ref, o_ref, acc_ref):
        part = _dot_tn(a_ref[...].astype(BF16), b_ref[...].astype(BF16))
        if ns == 1:
            o_ref[...] = part.astype(o_ref.dtype)
            return

        @pl.when(pl.program_id(2) == 0)
        def _():
            acc_ref[...] = part

        @pl.when(pl.program_id(2) != 0)
        def _():
            acc_ref[...] += part

        @pl.when(pl.program_id(2) == ns - 1)
        def _():
            o_ref[...] = acc_ref[...].astype(o_ref.dtype)

    if shard_cols is None:
        out_spec = pl.BlockSpec((ti, tj), lambda i, j, t: (i, j))
        out_shape = jax.ShapeDtypeStruct((ka, nb), BF16)
    else:
        out_spec = pl.BlockSpec((None, ti, tj), lambda i, j, t: (j, i, 0))
        out_shape = jax.ShapeDtypeStruct((nb // tj, ka, tj), BF16)
    (out,), got = _with_rider(
        body, rider, name=name, grid=(ka // ti, nb // tj, ns),
        in_specs=[pl.BlockSpec((ts, ti), lambda i, j, t: (t, i)), pl.BlockSpec((ts, tj), lambda i, j, t: (t, j))],
        out_specs=[out_spec], out_shape=[out_shape], scratch=[pltpu.VMEM((ti, tj), F32)], args=(a, b),
        sem=("parallel", "parallel", "arbitrary"))
    return out if rider is None else (out, got)


def _sum_call(parts, out_dtype, name):
    n, r, w = parts.shape
    tr = _row_tile(r)

    def body(p_ref, o_ref):
        acc = p_ref[0].astype(F32)
        for k in range(1, n):
            acc = acc + p_ref[k].astype(F32)
        o_ref[...] = acc.astype(out_dtype)

    return pl.pallas_call(
        body, name=name, grid=(r // tr,),
        in_specs=[pl.BlockSpec((n, tr, w), lambda i: (0, i, 0))], out_specs=_rows(tr, w),
        out_shape=jax.ShapeDtypeStruct((r, w), out_dtype), compiler_params=_cp(("parallel",)))(parts)


def _chip_sum_call(by_chip, core, name):
    n, r, w = by_chip.shape
    tr = _row_tile(r)
    nblk = r // tr

    def body(c_ref, p_ref, o_ref):
        acc = p_ref[0].astype(F32)
        for k in range(1, n):
            acc = acc + p_ref[k].astype(F32)
        o_ref[...] = acc

    return pl.pallas_call(
        body, name=name,
        grid_spec=pltpu.PrefetchScalarGridSpec(
            num_scalar_prefetch=1, grid=(nblk,),
            in_specs=[pl.BlockSpec((n, tr, w), lambda i, c_ref: (0, i, 0))],
            out_specs=pl.BlockSpec((tr, w), lambda i, c_ref: (c_ref[0] * nblk + i, 0))),
        out_shape=jax.ShapeDtypeStruct((2 * r, w), F32),
        compiler_params=_cp(("parallel",)))(core.reshape(1).astype(jnp.int32), by_chip)


def _pair_sum_call(full, other, core, out_dtype, name):
    n, r, w = other.shape
    tr = _row_tile(r)
    nblk = r // tr

    def body(c_ref, a_ref, b_ref, o_ref):
        o_ref[...] = (a_ref[...].astype(F32) + b_ref[...].astype(F32)).astype(out_dtype)

    spec = pl.BlockSpec((None, tr, w), lambda k, i, c_ref: (k, i, 0))
    return pl.pallas_call(
        body, name=name,
        grid_spec=pltpu.PrefetchScalarGridSpec(
            num_scalar_prefetch=1, grid=(n, nblk),
            in_specs=[pl.BlockSpec((None, tr, w), lambda k, i, c_ref: (k, c_ref[0] * nblk + i, 0)), spec],
            out_specs=spec),
        out_shape=jax.ShapeDtypeStruct((n, r, w), out_dtype),
        compiler_params=_cp(("parallel", "parallel")))(core.reshape(1).astype(jnp.int32), full, other)


def _adamw_call(w, g, row0, m, v, name):
    r, c = w.shape
    span = math.gcd(r, row0) if row0 else r
    tr = next((t for t in range(min(span, 256) // 8 * 8, 0, -8) if span % t == 0), span)
    off = row0 // tr

    def body(w_ref, g_ref, m_ref, v_ref, g_out_ref, d_ref, nm_ref, nv_ref):
        gg = g_ref[...]
        g_out_ref[...] = gg
        nm = ADAM_B1 * m_ref[...] + (1.0 - ADAM_B1) * gg
        nv = ADAM_B2 * v_ref[...] + (1.0 - ADAM_B2) * (gg * gg)
        m_hat = nm / (1.0 - ADAM_B1 ** ADAM_STEP)
        v_hat = nv / (1.0 - ADAM_B2 ** ADAM_STEP)
        d_ref[...] = -ADAM_LR * (m_hat / (jnp.sqrt(v_hat) + ADAM_EPS) + ADAM_WD * w_ref[...])
        nm_ref[...] = nm
        nv_ref[...] = nv

    out = jax.ShapeDtypeStruct((r, c), F32)
    g_spec = pl.BlockSpec((tr, c), lambda i: (off + i, 0))
    return pl.pallas_call(
        body, name=name, grid=(r // tr,), in_specs=[_rows(tr, c), g_spec, _rows(tr, c), _rows(tr, c)],
        out_specs=[_rows(tr, c)] * 4, out_shape=[out, out, out, out], compiler_params=_cp(("parallel",)))(w, g, m, v)


def _position():
    x, y, c = lax.axis_index("x"), lax.axis_index("y"), lax.axis_index("c")
    chips = [(1 - x, y), (x, 1 - y), (1 - x, 1 - y)]
    return x, y, c, chips


def _gather_rider(parts):
    n = len(parts)
    pairs = [(j, k) for j in range(3) for k in range(n)]

    def piece(out_refs, k, chip, core):
        half = parts[k].shape[0] // 2
        return out_refs[k].at[2 * chip[0] + chip[1], pl.ds(core * half, half), :]

    def over_ici(in_refs, out_refs, sems, j, k):
        x, y, c, chips = _position()
        half = parts[k].shape[0] // 2
        return pltpu.make_async_remote_copy(
            src_ref=in_refs[k].at[pl.ds(c * half, half), :], dst_ref=piece(out_refs, k, (x, y), c),
            send_sem=sems[0].at[n * j + k], recv_sem=sems[1].at[n * j + k], device_id=(*chips[j], c), device_id_type=MESH)

    def to_sibling(out_refs, sems, j, k):
        x, y, c, chips = _position()
        landed = piece(out_refs, k, chips[j], c)
        return pltpu.make_async_remote_copy(
            src_ref=landed, dst_ref=landed, send_sem=sems[2].at[n * j + k], recv_sem=sems[3].at[n * j + k],
            device_id=(x, y, 1 - c), device_id_type=MESH)

    def start(in_refs, out_refs, sems):
        for j, k in pairs:
            over_ici(in_refs, out_refs, sems, j, k).start()

    def relay(in_refs, out_refs, sems):
        for j, k in pairs:
            over_ici(in_refs, out_refs, sems, j, k).wait_recv()
            to_sibling(out_refs, sems, j, k).start()

    def finish(in_refs, out_refs, sems):
        for j, k in pairs:
            to_sibling(out_refs, sems, j, k).wait_recv()
        for j, k in pairs:
            over_ici(in_refs, out_refs, sems, j, k).wait_send()
            to_sibling(out_refs, sems, j, k).wait_send()

    return _Rider(list(parts), [jax.ShapeDtypeStruct((N_CHIPS,) + p.shape, p.dtype) for p in parts], [3 * n] * 4,
                  start, relay, finish)


def _scatter_rider(parts):
    n = len(parts)
    pairs = [(j, k) for j in range(3) for k in range(n)]

    def copy(in_refs, out_refs, sems, j, k):
        x, y, c, chips = _position()
        return pltpu.make_async_remote_copy(
            src_ref=in_refs[k].at[2 * chips[j][0] + chips[j][1]], dst_ref=out_refs[k].at[2 * x + y],
            send_sem=sems[0].at[n * j + k], recv_sem=sems[1].at[n * j + k], device_id=(*chips[j], c), device_id_type=MESH)

    def start(in_refs, out_refs, sems):
        for j, k in pairs:
            copy(in_refs, out_refs, sems, j, k).start()

    def finish(in_refs, out_refs, sems):
        for j, k in pairs:
            copy(in_refs, out_refs, sems, j, k).wait()

    return _Rider(list(parts), [jax.ShapeDtypeStruct(p.shape, p.dtype) for p in parts], [3 * n] * 2, start,
                  lambda in_refs, out_refs, sems: None, finish)


def _pair_send_call(parts, name):
    n = len(parts)

    def body(*refs):
        in_refs, out_refs = refs[:n], refs[n:2 * n]
        send_sems, recv_sems = refs[2 * n:]
        x, y, c, _ = _position()
        copies = []
        for k in range(n):
            half = parts[k].shape[1] // 2
            cp = pltpu.make_async_remote_copy(
                src_ref=in_refs[k].at[:, pl.ds((1 - c) * half, half), :], dst_ref=out_refs[k],
                send_sem=send_sems.at[k], recv_sem=recv_sems.at[k], device_id=(x, y, 1 - c), device_id_type=MESH)
            cp.start()
            copies.append(cp)
        for cp in copies:
            cp.wait()

    sems = pltpu.SemaphoreType.DMA((n,))
    return pl.pallas_call(
        body, name=name, in_specs=[HBM] * n, out_specs=[HBM] * n,
        out_shape=[jax.ShapeDtypeStruct((p.shape[0], p.shape[1] // 2, p.shape[2]), p.dtype) for p in parts],
        scratch_shapes=[sems, sems])(*parts)


def _pair_swap_call(parts, name):
    n = len(parts)

    def body(*refs):
        out_refs = refs[n:2 * n]
        send_sems, recv_sems = refs[2 * n:]
        x, y, c, _ = _position()
        copies = []
        for k in range(n):
            half = parts[k].shape[0] // 2
            mine = out_refs[k].at[pl.ds(c * half, half), :]
            cp = pltpu.make_async_remote_copy(
                src_ref=mine, dst_ref=mine, send_sem=send_sems.at[k], recv_sem=recv_sems.at[k],
                device_id=(x, y, 1 - c), device_id_type=MESH)
            cp.start()
            copies.append(cp)
        for cp in copies:
            cp.wait()

    sems = pltpu.SemaphoreType.DMA((n,))
    return pl.pallas_call(
        body, name=name, in_specs=[HBM] * n, out_specs=[HBM] * n,
        out_shape=[jax.ShapeDtypeStruct(p.shape, p.dtype) for p in parts],
        input_output_aliases={k: k for k in range(n)},
        scratch_shapes=[sems, sems])(*parts)


def _all_gather_small_call(block, name):
    r, w = block.shape

    def body(in_ref, out_ref, send_sems, recv_sems, local_sem):
        x, y, c, _ = _position()
        me = 4 * x + 2 * y + c
        own = pltpu.make_async_copy(in_ref, out_ref.at[me], local_sem)
        own.start()
        copies = []
        for k in range(1, 8):
            peer = (x ^ (k >> 2), y ^ ((k >> 1) & 1), c ^ (k & 1))
            cp = pltpu.make_async_remote_copy(
                src_ref=in_ref, dst_ref=out_ref.at[me], send_sem=send_sems.at[k - 1], recv_sem=recv_sems.at[k - 1],
                device_id=peer, device_id_type=MESH)
            cp.start()
            copies.append(cp)
        for cp in copies:
            cp.wait()
        own.wait()

    return pl.pallas_call(
        body, name=name, in_specs=[HBM], out_specs=HBM,
        out_shape=jax.ShapeDtypeStruct((8, r, w), block.dtype),
        scratch_shapes=[pltpu.SemaphoreType.DMA((7,)), pltpu.SemaphoreType.DMA((7,)), pltpu.SemaphoreType.DMA])(block)


BIG = {
    "ffn1_w_in": ((D_MODEL, 2 * D_FF), 1), "ffn1_w_out": ((D_FF, D_MODEL), 0),
    "w_in": ((D_MODEL, 4256), 1), "w_q_up": ((Q_LORA, HEADS * MLA_QK), 1), "w_kv_up": ((KV_LORA, 1024), 1),
    "w_branch_mla": ((512, D_MODEL), 1), "w_branch_sb": ((SB_WIDTH, D_MODEL), 1), "w_out": ((D_MODEL, D_MODEL), 0),
    "ffn2_w_in": ((D_MODEL, 2 * D_FF), 1), "ffn2_w_out": ((D_FF, D_MODEL), 0),
    "w_ple_gate": ((D_MODEL, D_MODEL), 0), "w_ple_proj": ((PLE_DIM, D_MODEL), 1),
}
GAINS = {"ffn1_norm": 1024, "mix_norm": 1024, "q_latent_norm": 384, "kv_latent_norm": 256, "q_head_norm": 96,
         "k_head_norm": 96, "ffn2_norm": 1024, "ple_norm": 1024}
WEIGHT_ORDER = ["ffn1_norm", "ffn1_w_in", "ffn1_w_out", "mix_norm", "w_in", "q_latent_norm", "w_q_up",
                "kv_latent_norm", "w_kv_up", "q_head_norm", "k_head_norm", "w_branch_mla", "w_branch_sb", "w_out",
                "ffn2_norm", "ffn2_w_in", "ffn2_w_out", "ple_norm", "w_ple_gate", "w_ple_proj"]


W_IN_SHARD_ROWS = 1088


def _shard_shape(name):
    (r, c), axis = BIG[name]
    if name in TRANSPOSED_UPDATE:
        return (W_IN_SHARD_ROWS, r)
    return (r // N_CHIPS, c) if axis == 0 else (r, c // N_CHIPS)


GATHER_GROUPS = [
    [("ffn1_w_in",)],
    [("ffn1_w_out",), ("w_in",)],
    [("w_out",), ("w_kv_up", "w_branch_mla", "w_branch_sb"), ("w_q_up",)],
    [("ffn2_w_in",), ("ffn2_w_out", "w_ple_gate"), ("w_ple_proj",)],
]
REDUCE_GROUPS = [
    [("ffn2_w_in",), ("ffn2_w_out", "w_out", "w_ple_gate"), ("w_branch_mla", "w_branch_sb", "w_ple_proj")],
    [("w_in",), ("w_kv_up",), ("w_q_up",)],
    [("ffn1_w_out",)],
    [("ffn1_w_in",)],
]


def _join_parts(shards, group):
    return [shards[part[0]] if len(part) == 1 else jnp.concatenate([shards[n] for n in part], axis=-2) for part in group]


def _part_rows(group):
    where = {}
    for k, part in enumerate(group):
        at = 0
        for n in part:
            where[n] = (k, at)
            at += _shard_shape(n)[0]
    return where


def _split_parts(parts, group):
    return {n: parts[k][..., at:at + _shard_shape(n)[0], :] for n, (k, at) in _part_rows(group).items()}


def _exchange_form(name, shard):
    if name in TRANSPOSED_UPDATE:
        t = shard.T.astype(BF16)
        return jnp.pad(t, ((0, W_IN_SHARD_ROWS - t.shape[0]), (0, 0)))
    return shard.astype(BF16)


def _to_shards(name, full):
    (r, c), axis = BIG[name]
    if axis == 0:
        return full.reshape(N_CHIPS, r // N_CHIPS, c)
    return full.reshape(r, N_CHIPS, c // N_CHIPS).transpose(1, 0, 2)


def _from_shards(name, shards):
    (r, c), axis = BIG[name]
    if axis == 0:
        return shards.reshape(r, c)
    return shards.transpose(1, 0, 2).reshape(r, c)


def _relayout_w_in(wt):
    d = wt.shape[1]
    z = lambda n: jnp.zeros((n, d), wt.dtype)
    return jnp.concatenate([wt[:640], z(MLA_NOPE), wt[640:672], z(HEAD_PAD - MLA_QK), wt[672:]], axis=0)


def _unlayout_w_in(gt):
    full = jnp.concatenate([gt[:640], gt[640 + MLA_NOPE:640 + MLA_QK], gt[768:]], axis=0)
    shards = full.reshape(N_CHIPS, -1, gt.shape[1])
    return jnp.pad(shards, ((0, 0), (0, W_IN_SHARD_ROWS - shards.shape[1]), (0, 0)))


def _pad_heads(v):
    lead = v.shape[:-1]
    return jnp.pad(v.reshape(lead + (HEADS, MLA_QK)), [(0, 0)] * len(lead) + [(0, 0), (0, HEAD_PAD - MLA_QK)]).reshape(
        lead + (HEADS * HEAD_PAD,))


SHARD_MAJOR = ("ffn1_w_in", "ffn2_w_in")
TRANSPOSED_UPDATE = ("w_in",)


def _step(x, p, pos, tgt, gains, weights, dist):
    d = D_MODEL
    full = {} if dist is not None else {
        n: _to_shards(n, w) if n in SHARD_MAJOR else (w.T if n in TRANSPOSED_UPDATE else w) for n, w in weights.items()}
    reduced = {}

    def gather_rider(g):
        if dist is None:
            return None, None
        mine = _join_parts(weights, GATHER_GROUPS[g])
        return mine, _gather_rider(mine)

    def gathered(g, mine, others):
        if dist is not None:
            parts = [lax.dynamic_update_slice_in_dim(o, m[None], dist[0], axis=0) for o, m in zip(others, mine)]
            for n, shards in _split_parts(parts, GATHER_GROUPS[g]).items():
                if n in TRANSPOSED_UPDATE:
                    (d_in, c_out), _ = BIG[n]
                    full[n] = shards[:, :c_out // N_CHIPS].reshape(c_out, d_in)
                else:
                    full[n] = shards if n in SHARD_MAJOR else _from_shards(n, shards)

    def reduce_before(g):
        if dist is None:
            return None, None
        group = REDUCE_GROUPS[g]
        shards = {n: grads[n] if grads[n].ndim == 3 else _to_shards(n, grads[n].astype(BF16)) for part in group for n in part}
        partial = _join_parts(shards, group)
        from_sibling = _pair_send_call(partial, "grads%d_pair_send" % g)
        pair_sum = [_pair_sum_call(a, b, dist[1], BF16, "grads%d_pair_sum_%d" % (g, k))
                    for k, (a, b) in enumerate(zip(partial, from_sibling))]
        return pair_sum, _scatter_rider(pair_sum)

    def reduce_after(g, pair_sum, by_chip):
        if dist is not None:
            chip, core = dist
            by_chip = [lax.dynamic_update_slice_in_dim(t, lax.dynamic_slice_in_dim(o, chip, 1, axis=0), chip, axis=0)
                       for t, o in zip(by_chip, pair_sum)]
            bufs = _pair_swap_call([_chip_sum_call(t, core, "grads%d_chip_sum_%d" % (g, k)) for k, t in enumerate(by_chip)],
                                   "grads%d_pair_swap" % g)
            for n, (k, row0) in _part_rows(REDUCE_GROUPS[g]).items():
                reduced[n] = (bufs[k], row0)

    mine, rider = gather_rider(0)
    u1, got = _norm_call(x, gains["ffn1_norm"], "norm_ffn1", rider)
    gathered(0, mine, got)
    wts = full
    inv_freq = ROPE_BASE ** (-jnp.arange(0, MLA_ROPE, 2, dtype=F32) / MLA_ROPE)
    zeros = lambda n: jnp.zeros((n,), F32)
    freq = jnp.concatenate([zeros(MLA_NOPE), inv_freq, inv_freq, zeros(HEAD_PAD - MLA_QK)])[None]
    sign = jnp.concatenate([zeros(MLA_NOPE), -jnp.ones((16,), F32), jnp.ones((16,), F32), zeros(HEAD_PAD - MLA_QK)])[None]
    pad_gain = lambda g: jnp.pad(g, ((0, 0), (0, HEAD_PAD - MLA_QK)))
    g_qh, g_kh = pad_gain(gains["q_head_norm"]), pad_gain(gains["k_head_norm"])

    mine, rider = gather_rider(1)
    (a1, b1, hm1), got = _ffn_in_call(u1, wts["ffn1_w_in"], "ffn1_in", rider)
    gathered(1, mine, got)
    mine, rider = gather_rider(2)
    (h1, um), got = _ffn_out_call(hm1, wts["ffn1_w_out"], x, gains["mix_norm"], "ffn1_out", rider)
    gathered(2, mine, got)
    w_in = _relayout_w_in(wts["w_in"])
    (cq, ckv, krope, sbq, sbk, sbv, gates), _ = _mix_in_call(um, w_in, "mix_in")
    wq = _pad_heads(wts["w_q_up"])
    wkv = wts["w_kv_up"]
    wbm = jnp.pad(wts["w_branch_mla"].reshape(HEADS, 64, d), ((0, 0), (64, 0), (0, 0))).reshape(HEADS * HEAD_PAD, d)
    wbs, wo = wts["w_branch_sb"], wts["w_out"]
    prep_args = (cq, ckv, krope, pos, freq, sign, gains["q_latent_norm"], gains["kv_latent_norm"], g_qh, g_kh, wq, wkv)
    q, k, v = _mla_prep_call(*prep_args, "mla_prep")
    mine, rider = gather_rider(3)
    (om, lse), got = _mla_fwd_call(q, k, v, "mla_fwd", rider)
    gathered(3, mine, got)
    osb = _sb_fwd_call(sbq, sbk, sbv, "sb_fwd")
    h2, bm, bs, mg, u2 = _merge_out_call(om, osb, gates, h1, wbm, wbs, wo, gains["ffn2_norm"], "merge_out")
    (a2, b2, hm2), _ = _ffn_in_call(u2, wts["ffn2_w_in"], "ffn2_in")
    (h3, _), _ = _ffn_out_call(hm2, wts["ffn2_w_out"], h2, gains["ple_norm"], "ffn2_out")

    grads, gg = {}, {}
    dh3, dh3s, un, dgl, dpp, gg["ple_norm"], sq = _ple_call(
        h3, gains["ple_norm"], wts["w_ple_gate"], p, wts["w_ple_proj"], tgt, "ple")
    grads["w_ple_gate"] = _tn_call(un, dgl, "dw_ple_gate")
    grads["w_ple_proj"] = _tn_call(p, dpp, "dw_ple_proj")

    (da2, db2), _ = _ffn_bwd_a_call(dh3s, a2, b2, wts["ffn2_w_out"], "ffn2_bwd_act")
    grads["ffn2_w_out"] = _tn_call(hm2, dh3s, "dw_ffn2_out")
    grads["ffn2_w_in"] = jnp.concatenate([_tn_call(u2, da2, "dw_ffn2_in_a", shard_cols=D_FF // 2),
                                          _tn_call(u2, db2, "dw_ffn2_in_b", shard_cols=D_FF // 2)], axis=0)
    dh2, dh2b, gg["ffn2_norm"] = _norm_bwd_call([da2, db2], [wts["ffn2_w_in"]], h2, gains["ffn2_norm"], dh3,
                                                "ffn2_bwd_norm", half_out=False)

    dgates, dbm, dbs, dom, dos = _merge_bwd_call(dh2b, gates, bm, bs, wo, wbm, wbs, "merge_bwd")
    grads["w_out"] = _tn_call(mg, dh2b, "dw_out")
    grads["w_branch_mla"] = _tn_call(om, dbm, "dw_branch_mla").reshape(HEADS, HEAD_PAD, d)[:, 64:, :].reshape(512, d)
    grads["w_branch_sb"] = _tn_call(osb, dbs, "dw_branch_sb")
    pair_sum, rider = reduce_before(0)
    (dq, dk, dv), got = _mla_bwd_call(q, k, v, om, dom, lse, "mla_bwd", rider)
    reduce_after(0, pair_sum, got)
    dsq, dsk, dsv = _sb_bwd_call(sbq, sbk, sbv, dos, osb, "sb_bwd")
    (dcq, dckv, dkr, dwq, grads["w_kv_up"], gg["q_latent_norm"], gg["kv_latent_norm"], dgqh, dgkh) = \
        _mla_prep_bwd_call(*prep_args, dq, dk, dv, "mla_prep_bwd")
    grads["w_q_up"] = dwq.reshape(Q_LORA, HEADS, HEAD_PAD)[:, :, :MLA_QK].reshape(Q_LORA, HEADS * MLA_QK)
    gg["q_head_norm"], gg["k_head_norm"] = dgqh[:, :MLA_QK], dgkh[:, :MLA_QK]
    dproj = jnp.concatenate([dcq, dckv, dkr, dsq, dsk.astype(BF16), dsv.astype(BF16), dgates], axis=1)
    grads["w_in"] = _unlayout_w_in(_tn_call(dproj, um, "dw_in"))
    dh1, dh1s, gg["mix_norm"] = _norm_bwd_call([dproj], [w_in], h1, gains["mix_norm"], dh2, "mix_bwd_norm", half_out=True,
                                               w_transposed=True)

    pair_sum, rider = reduce_before(1)
    (da1, db1), got = _ffn_bwd_a_call(dh1s, a1, b1, wts["ffn1_w_out"], "ffn1_bwd_act", rider)
    reduce_after(1, pair_sum, got)
    grads["ffn1_w_out"] = _tn_call(hm1, dh1s, "dw_ffn1_out")
    pair_sum, rider = reduce_before(2)
    res = _tn_call(u1, da1, "dw_ffn1_in_a", shard_cols=D_FF // 2, rider=rider)
    dwa, got = (res, None) if rider is None else res
    reduce_after(2, pair_sum, got)
    grads["ffn1_w_in"] = jnp.concatenate([dwa, _tn_call(u1, db1, "dw_ffn1_in_b", shard_cols=D_FF // 2)], axis=0)
    pair_sum, rider = reduce_before(3)
    res = _norm_bwd_call([da1, db1], [wts["ffn1_w_in"]], x, gains["ffn1_norm"], dh1, "ffn1_bwd_norm",
                         half_out=False, rider=rider)
    (dx, _, gg["ffn1_norm"]), got = (res, None) if rider is None else res
    reduce_after(3, pair_sum, got)
    return sq, dx, gg, (grads if dist is None else reduced)


def kernel(x, p, positions, ffn1_norm, ffn1_w_in, ffn1_w_out, mix_norm, w_in, q_latent_norm, w_q_up, kv_latent_norm, w_kv_up, q_head_norm, k_head_norm, w_branch_mla, w_branch_sb, w_out, ffn2_norm, ffn2_w_in, ffn2_w_out, ple_norm, w_ple_gate, w_ple_proj, loss_target, m_ffn1_norm, m_ffn1_w_in, m_ffn1_w_out, m_mix_norm, m_w_in, m_q_latent_norm, m_w_q_up, m_kv_latent_norm, m_w_kv_up, m_q_head_norm, m_k_head_norm, m_w_branch_mla, m_w_branch_sb, m_w_out, m_ffn2_norm, m_ffn2_w_in, m_ffn2_w_out, m_ple_norm, m_w_ple_gate, m_w_ple_proj, v_ffn1_norm, v_ffn1_w_in, v_ffn1_w_out, v_mix_norm, v_w_in, v_q_latent_norm, v_w_q_up, v_kv_latent_norm, v_w_kv_up, v_q_head_norm, v_k_head_norm, v_w_branch_mla, v_w_branch_sb, v_w_out, v_ffn2_norm, v_ffn2_w_in, v_ffn2_w_out, v_ple_norm, v_w_ple_gate, v_w_ple_proj):
    given = dict(locals())
    w_shard = {n: given[n][0] for n in WEIGHT_ORDER}
    m_shard = {n: given["m_" + n][0] for n in WEIGHT_ORDER}
    v_shard = {n: given["v_" + n][0] for n in WEIGHT_ORDER}
    gains = {n: w_shard[n][None] for n in GAINS}

    chip = 2 * lax.axis_index("x") + lax.axis_index("y")
    sq, dx, gain_grads, reduced = _step(x[0], p[0, 0], positions.reshape(-1, 1), loss_target[0], gains,
                                        {n: _exchange_form(n, w_shard[n]) for n in BIG}, (chip, lax.axis_index("c")))

    rows = [jnp.pad(gain_grads[n], ((0, 0), (0, D_MODEL - GAINS[n]))) for n in GAINS] + [sq]
    gain_block = jnp.concatenate(rows + [jnp.zeros((16 - len(rows), D_MODEL), F32)], axis=0)
    gain_sum = _sum_call(_all_gather_small_call(gain_block, "gains_all_gather"), F32, "gains_sum")
    loss = 0.5 * jnp.sum(gain_sum[len(GAINS)]) / D_MODEL

    outs = {"grad": {}, "delta": {}, "new_m": {}, "new_v": {}}
    gain_pack = lambda t: jnp.concatenate([jnp.pad(t[n][None], ((0, 0), (0, D_MODEL - GAINS[n]))) for n in GAINS], axis=0)
    packed = _adamw_call(gain_pack(w_shard), gain_sum, 0, gain_pack(m_shard), gain_pack(v_shard), "adamw_gains")
    for i, n in enumerate(GAINS):
        for kind, t in zip(("grad", "delta", "new_m", "new_v"), packed):
            outs[kind][n] = t[i, :GAINS[n]][None]
    for n in BIG:
        buf, row0 = reduced[n]
        if n in TRANSPOSED_UPDATE:
            res = [t.T for t in _adamw_call(w_shard[n].T, buf, row0, m_shard[n].T, v_shard[n].T, "adamw_" + n)]
        else:
            res = _adamw_call(w_shard[n], buf, row0, m_shard[n], v_shard[n], "adamw_" + n)
        for kind, t in zip(("grad", "delta", "new_m", "new_v"), res):
            outs[kind][n] = t[None]

    return (loss, dx[None], *[outs["grad"][n] for n in WEIGHT_ORDER], *[outs["delta"][n] for n in WEIGHT_ORDER],
            *[outs["new_m"][n] for n in WEIGHT_ORDER], *[outs["new_v"][n] for n in WEIGHT_ORDER])
```
